```python
import jax, jax.numpy as jnp
from jax import lax
import numpy as np

D_MODEL = 1024
BATCH = 8
SEQ = 2048
DEPTH = 2

PLE_DIM = 256
SC_WIDTH = 512
SC_KERNEL = 3
N_HEADS = 8
QK_NOPE = 64
QK_ROPE = 32
V_HEAD = 64
Q_LORA = 768
KV_LORA = 256
ROPE_THETA = 10000.0
Q_BLOCK = 128
CONF_WIDTH = 512
CONF_KERNEL = 31
N_GROUPS = 4
EXPERTS_PER_GROUP = 8
N_EXPERTS = N_GROUPS * EXPERTS_PER_GROUP
TOP_K = 2
EXPERT_HIDDEN = 256
EPS = 1e-6

SPLIT_SIZES = (D_MODEL, D_MODEL, D_MODEL,
               SC_WIDTH, SC_WIDTH, SC_WIDTH,
               Q_LORA, KV_LORA, QK_ROPE,
               CONF_WIDTH, CONF_WIDTH)
IN_COLS = 3 * D_MODEL + 3 * SC_WIDTH + Q_LORA + KV_LORA + QK_ROPE + 2 * CONF_WIDTH

kernel_name = 'hybrid_gated_sconv_mla_conformer_hmoe'


def rms_norm(x, g):
    x32 = x.astype(jnp.float32)
    y = x32 * lax.rsqrt(jnp.mean(x32 * x32, axis=-1, keepdims=True) + EPS) * g.astype(jnp.float32)
    return y.astype(x.dtype)


def layer_norm(x, g, b):
    x32 = x.astype(jnp.float32)
    mu = jnp.mean(x32, axis=-1, keepdims=True)
    xc = x32 - mu
    var = jnp.mean(xc * xc, axis=-1, keepdims=True)
    y = xc * lax.rsqrt(var + EPS) * g.astype(jnp.float32) + b.astype(jnp.float32)
    return y.astype(x.dtype)


def causal_depthwise_conv(x, w):
    k, c = w.shape
    return lax.conv_general_dilated(x, w[:, None, :], window_strides=(1,), padding=[(k - 1, 0)],
                                    dimension_numbers=('NWC', 'WIO', 'NWC'), feature_group_count=c)


def rope_cos_sin(positions, dim):
    inv_freq = ROPE_THETA ** (-jnp.arange(0, dim, 2, dtype=jnp.float32) / dim)
    ang = positions.astype(jnp.float32)[..., None] * inv_freq
    return jnp.cos(ang), jnp.sin(ang)


def apply_rope(x, cos, sin):
    half = x.shape[-1] // 2
    x1, x2 = x[..., :half], x[..., half:]
    c, s = cos.astype(x.dtype), sin.astype(x.dtype)
    return jnp.concatenate([x1 * c - x2 * s, x1 * s + x2 * c], axis=-1)


def mla_attention(q_nope, q_rope, k_nope, k_rope, v):
    b, s, h, _ = q_nope.shape
    nblk = s // Q_BLOCK
    scale = (QK_NOPE + QK_ROPE) ** -0.5
    k_idx = jnp.arange(s)

    def to_blocks(t):
        return jnp.moveaxis(t.reshape(b, nblk, Q_BLOCK, *t.shape[2:]), 1, 0)

    def one_block(args):
        qn, qr, blk = args
        sc = (jnp.einsum('bqhd,bkhd->bhqk', qn, k_nope)
              + jnp.einsum('bqhr,bkr->bhqk', qr, k_rope)).astype(jnp.float32) * scale
        q_idx = blk * Q_BLOCK + jnp.arange(Q_BLOCK)
        causal = q_idx[:, None] >= k_idx[None, :]
        sc = jnp.where(causal, sc, -jnp.inf)
        pr = jax.nn.softmax(sc, axis=-1).astype(v.dtype)
        return jnp.einsum('bhqk,bkhd->bqhd', pr, v)

    out = lax.map(one_block, (to_blocks(q_nope), to_blocks(q_rope), jnp.arange(nblk)))
    return jnp.moveaxis(out, 0, 1).reshape(b, s, h * V_HEAD)


def mixer_block(xn, cos, sin, w_in, conv_a_w, w_out_a, q_norm_g, w_uq, kv_norm_g, w_ukv, w_out_b,
                conv_c_w, ln_c_g, ln_c_b, w_out_c, w_o):
    b, s, _ = xn.shape
    proj = xn @ w_in
    points, acc = [], 0
    for size in SPLIT_SIZES[:-1]:
        acc += size
        points.append(acc)
    (gate_a, gate_b, gate_c, sc_b, sc_c, sc_u, q_lat, kv_lat, k_rope_raw,
     glu_val, glu_gate) = jnp.split(proj, points, axis=-1)

    y_a = (sc_b * causal_depthwise_conv(sc_c * sc_u, conv_a_w)) @ w_out_a

    q = (rms_norm(q_lat, q_norm_g) @ w_uq).reshape(b, s, N_HEADS, QK_NOPE + QK_ROPE)
    q_nope = q[..., :QK_NOPE]
    q_rope = apply_rope(q[..., QK_NOPE:], cos[:, :, None, :], sin[:, :, None, :])
    kv = (rms_norm(kv_lat, kv_norm_g) @ w_ukv).reshape(b, s, N_HEADS, QK_NOPE + V_HEAD)
    k_nope, v = kv[..., :QK_NOPE], kv[..., QK_NOPE:]
    k_rope = apply_rope(k_rope_raw, cos, sin)
    y_b = mla_attention(q_nope, q_rope, k_nope, k_rope, v) @ w_out_b

    u = glu_val * jax.nn.sigmoid(glu_gate)
    u = causal_depthwise_conv(u, conv_c_w)
    u = jax.nn.silu(layer_norm(u, ln_c_g, ln_c_b))
    y_c = u @ w_out_c

    merged = (jax.nn.sigmoid(gate_a) * y_a + jax.nn.sigmoid(gate_b) * y_b
              + jax.nn.sigmoid(gate_c) * y_c)
    return merged @ w_o


def hier_moe(xn, w_rg, b_rg, w_re, b_re, w_gate, w_up, w_down):
    b, s, _ = xn.shape
    grp_logits = (xn @ w_rg).astype(jnp.float32) + b_rg.astype(jnp.float32)
    pg = jax.nn.softmax(grp_logits, axis=-1)
    g_sel = jnp.argmax(grp_logits, axis=-1)
    g_onehot = jax.nn.one_hot(g_sel, N_GROUPS, dtype=jnp.float32)
    p_sel = jnp.max(pg, axis=-1, keepdims=True)
    exp_logits = ((xn @ w_re).astype(jnp.float32) + b_re.astype(jnp.float32)).reshape(
        b, s, N_GROUPS, EXPERTS_PER_GROUP)
    exp_sel = jnp.sum(exp_logits * g_onehot[..., None], axis=2)
    pe = jax.nn.softmax(exp_sel, axis=-1)
    top_w, top_i = lax.top_k(pe, TOP_K)
    top_w = top_w / jnp.sum(top_w, axis=-1, keepdims=True) * p_sel
    comb_in = jnp.sum(jax.nn.one_hot(top_i, EXPERTS_PER_GROUP, dtype=jnp.float32) * top_w[..., None], axis=-2)
    comb = (g_onehot[..., None] * comb_in[:, :, None, :]).astype(xn.dtype)
    y = jnp.zeros_like(xn)
    for g in range(N_GROUPS):
        sl = slice(g * EXPERTS_PER_GROUP, (g + 1) * EXPERTS_PER_GROUP)
        hg = jnp.einsum('bsd,edf->bsef', xn, w_gate[sl])
        hu = jnp.einsum('bsd,edf->bsef', xn, w_up[sl])
        hh = jax.nn.silu(hg) * hu * comb[:, :, g, :, None]
        y = y + jnp.einsum('bsef,efd->bsd', hh, w_down[sl])
    return y


def setup_inputs(seed: int = 0) -> dict:
    key = jax.random.key(seed)
    ks = iter(jax.random.split(key, 40))

    def w(shape, fan_in):
        return jax.random.normal(next(ks), shape, jnp.float32) * fan_in ** -0.5

    def gain(shape):
        return 1.0 + 0.02 * jax.random.normal(next(ks), shape, jnp.float32)

    def small(shape, scale):
        return scale * jax.random.normal(next(ks), shape, jnp.float32)

    L = DEPTH
    x = jax.random.normal(next(ks), (BATCH, SEQ, D_MODEL), jnp.float32)
    p = jax.random.normal(next(ks), (DEPTH, BATCH, SEQ, PLE_DIM), jnp.float32)
    start = jax.random.randint(next(ks), (BATCH, 1), 0, 1024, dtype=jnp.int32)
    positions = start + jnp.arange(SEQ, dtype=jnp.int32)[None, :]
    return {
        'x': x, 'p': p, 'positions': positions,
        'ln_mix_g': gain((L, D_MODEL)),
        'w_in': w((L, D_MODEL, IN_COLS), D_MODEL),
        'conv_a_w': w((L, SC_KERNEL, SC_WIDTH), SC_KERNEL),
        'w_out_a': w((L, SC_WIDTH, D_MODEL), SC_WIDTH),
        'q_norm_g': gain((L, Q_LORA)),
        'w_uq': w((L, Q_LORA, N_HEADS * (QK_NOPE + QK_ROPE)), Q_LORA),
        'kv_norm_g': gain((L, KV_LORA)),
        'w_ukv': w((L, KV_LORA, N_HEADS * (QK_NOPE + V_HEAD)), KV_LORA),
        'w_out_b': w((L, N_HEADS * V_HEAD, D_MODEL), N_HEADS * V_HEAD),
        'conv_c_w': w((L, CONF_KERNEL, CONF_WIDTH), CONF_KERNEL),
        'ln_c_g': gain((L, CONF_WIDTH)),
        'ln_c_b': small((L, CONF_WIDTH), 0.02),
        'w_out_c': w((L, CONF_WIDTH, D_MODEL), CONF_WIDTH),
        'w_o': w((L, D_MODEL, D_MODEL), D_MODEL),
        'ln_ffn_g': gain((L, D_MODEL)),
        'w_route_grp': w((L, D_MODEL, N_GROUPS), D_MODEL),
        'b_route_grp': small((L, N_GROUPS), 0.01),
        'w_route_exp': w((L, D_MODEL, N_EXPERTS), D_MODEL),
        'b_route_exp': small((L, N_EXPERTS), 0.01),
        'w_exp_gate': w((L, N_EXPERTS, D_MODEL, EXPERT_HIDDEN), D_MODEL),
        'w_exp_up': w((L, N_EXPERTS, D_MODEL, EXPERT_HIDDEN), D_MODEL),
        'w_exp_down': w((L, N_EXPERTS, EXPERT_HIDDEN, D_MODEL), EXPERT_HIDDEN),
        'ln_ple_g': gain((L, D_MODEL)),
        'w_ple_gate': w((L, D_MODEL, D_MODEL), D_MODEL),
        'w_ple': w((L, PLE_DIM, D_MODEL), PLE_DIM),
        'final_norm_g': gain((D_MODEL,)),
    }


def reference(x, p, positions, ln_mix_g, w_in, conv_a_w, w_out_a, q_norm_g, w_uq, kv_norm_g, w_ukv,
              w_out_b, conv_c_w, ln_c_g, ln_c_b, w_out_c, w_o, ln_ffn_g, w_route_grp, b_route_grp,
              w_route_exp, b_route_exp, w_exp_gate, w_exp_up, w_exp_down, ln_ple_g, w_ple_gate, w_ple,
              final_norm_g):
    cos, sin = rope_cos_sin(positions, QK_ROPE)
    for i in range(DEPTH):
        h = rms_norm(x, ln_mix_g[i])
        x = x + mixer_block(h, cos, sin, w_in[i], conv_a_w[i], w_out_a[i], q_norm_g[i], w_uq[i],
                            kv_norm_g[i], w_ukv[i], w_out_b[i], conv_c_w[i], ln_c_g[i], ln_c_b[i],
                            w_out_c[i], w_o[i])
        h = rms_norm(x, ln_ffn_g[i])
        x = x + hier_moe(h, w_route_grp[i], b_route_grp[i], w_route_exp[i], b_route_exp[i],
                         w_exp_gate[i], w_exp_up[i], w_exp_down[i])
        h = rms_norm(x, ln_ple_g[i])
        x = x + jax.nn.sigmoid(h @ w_ple_gate[i]) * (p[i] @ w_ple[i])
    return rms_norm(x, final_norm_g)
```

```python
import functools

import jax
import jax.numpy as jnp
from jax import lax
from jax.experimental import pallas as pl
from jax.experimental.pallas import tpu as pltpu

D_MODEL = 1024
BATCH = 8
SEQ = 2048
DEPTH = 2
TOKENS = BATCH * SEQ
PLE_DIM = 256
SC_WIDTH = 512
SC_KERNEL = 3
N_HEADS = 8
QK_NOPE = 64
QK_ROPE = 32
V_HEAD = 64
Q_LORA = 768
KV_LORA = 256
ROPE_THETA = 10000.0
CONF_WIDTH = 512
CONF_KERNEL = 31
N_GROUPS = 4
EXPERTS_PER_GROUP = 8
N_EXPERTS = N_GROUPS * EXPERTS_PER_GROUP
EXPERT_HIDDEN = 256
EPS = 1e-6

LANES = 128
HEAD_PAD = 128
F32 = jnp.float32
BF16 = jnp.bfloat16

COL_GATES = 0
COL_SC = 3 * D_MODEL
COL_QLAT = COL_SC + 3 * SC_WIDTH
COL_KVLAT = COL_QLAT + Q_LORA
COL_GLU = COL_KVLAT + KV_LORA
COL_KR = COL_GLU + 2 * CONF_WIDTH
COL_KRSW = COL_KR + HEAD_PAD
PROJ_COLS = COL_KRSW + HEAD_PAD

ROUTER_GROUP_LANE = N_EXPERTS

TM_INPROJ = 1024
TN_INPROJ = 2304
TM_QKV = 512
TQ_ATTN = 256
TM_MIX = 512
CONV_CHUNK = 64
HALO_C = 32
HALO_A = 16
TM_MOE = 1024
TM_PLE = 512


def _cparams(semantics, vmem_mb):
    return pltpu.CompilerParams(dimension_semantics=semantics, vmem_limit_bytes=vmem_mb * 1024 * 1024)


def _sigmoid(x):
    return 1.0 / (1.0 + jnp.exp(-x))


def _rms(x, g):
    return x * lax.rsqrt(jnp.mean(x * x, axis=-1, keepdims=True) + EPS) * g


def _rope_kernel(pos_ref, freq_ref, cos_ref, sin_ref):
    ang = pos_ref[...].astype(F32) * freq_ref[...]
    cos_ref[...] = jnp.cos(ang)
    sin_ref[...] = jnp.sin(ang)


def _rope_tables(positions):
    half = QK_ROPE // 2
    inv_freq = ROPE_THETA ** (-jnp.arange(0, QK_ROPE, 2, dtype=F32) / QK_ROPE)
    rows = TOKENS * half // LANES
    pos_rep = jnp.broadcast_to(positions.reshape(TOKENS, 1), (TOKENS, half)).reshape(rows, LANES)
    freq = jnp.tile(inv_freq, LANES // half).reshape(1, LANES)
    cos, sin = pl.pallas_call(
        _rope_kernel,
        out_shape=(jax.ShapeDtypeStruct((rows, LANES), F32),) * 2,
        name="rope_tables",
    )(pos_rep, freq)
    cos = cos.reshape(TOKENS, half)
    sin = sin.reshape(TOKENS, half)
    ones = jnp.ones((TOKENS, QK_NOPE), F32)
    zn = jnp.zeros((TOKENS, QK_NOPE), F32)
    zp = jnp.zeros((TOKENS, HEAD_PAD - QK_NOPE - QK_ROPE), F32)
    c_tab = jnp.concatenate([ones, cos, cos, zp], axis=1)
    s_tab = jnp.concatenate([zn, -sin, sin, zp], axis=1)
    return c_tab, s_tab


def _inproj_kernel(x_ref, g_ref, w_ref, o_ref, h_ref):
    @pl.when(pl.program_id(1) == 0)
    def _():
        h_ref[...] = _rms(x_ref[...], g_ref[...]).astype(BF16)

    o_ref[...] = jnp.dot(h_ref[...], w_ref[...], preferred_element_type=F32).astype(BF16)


def _inproj(x, g, w):
    tm, tn = TM_INPROJ, TN_INPROJ
    return pl.pallas_call(
        _inproj_kernel,
        grid=(TOKENS // tm, PROJ_COLS // tn),
        in_specs=[
            pl.BlockSpec((tm, D_MODEL), lambda i, j: (i, 0)),
            pl.BlockSpec((1, D_MODEL), lambda i, j: (0, 0)),
            pl.BlockSpec((D_MODEL, tn), lambda i, j: (0, j)),
        ],
        out_specs=pl.BlockSpec((tm, tn), lambda i, j: (i, j)),
        out_shape=jax.ShapeDtypeStruct((TOKENS, PROJ_COLS), BF16),
        scratch_shapes=[pltpu.VMEM((tm, D_MODEL), BF16)],
        compiler_params=_cparams(("parallel", "arbitrary"), 48),
        name="in_proj",
    )(x, g, w)


def _qkv_kernel(ql_ref, kvl_ref, kr_ref, krsw_ref, c_ref, s_ref, qg_ref, kvg_ref, wq_ref, wkv_ref,
                q_out, k_out, v_out):
    c = c_ref[...]
    s = s_ref[...]
    width = N_HEADS * HEAD_PAD
    qn = _rms(ql_ref[...].astype(F32), qg_ref[...]).astype(BF16)
    qq = jnp.dot(qn, wq_ref[...], preferred_element_type=F32)
    for h in range(N_HEADS):
        lo, hi = h * HEAD_PAD, (h + 1) * HEAD_PAD
        q_out[:, lo:hi] = (qq[:, lo:hi] * c + qq[:, width + lo:width + hi] * s).astype(BF16)
    kvn = _rms(kvl_ref[...].astype(F32), kvg_ref[...]).astype(BF16)
    kk = jnp.dot(kvn, wkv_ref[...], preferred_element_type=F32)
    kr = kr_ref[...].astype(F32) * c + krsw_ref[...].astype(F32) * s
    for h in range(N_HEADS):
        lo, hi = h * HEAD_PAD, (h + 1) * HEAD_PAD
        k_out[:, lo:hi] = (kk[:, lo:hi] + kr).astype(BF16)
    v_out[...] = kk[:, width:].astype(BF16)


def _qkv(proj, c_tab, s_tab, qg, kvg, wq, wkv):
    tm = TM_QKV
    width = N_HEADS * HEAD_PAD
    row = lambda blk: (lambda i: (i, blk))
    const = lambda i: (0, 0)
    return pl.pallas_call(
        _qkv_kernel,
        grid=(TOKENS // tm,),
        in_specs=[
            pl.BlockSpec((tm, Q_LORA), row(COL_QLAT // Q_LORA)),
            pl.BlockSpec((tm, KV_LORA), row(COL_KVLAT // KV_LORA)),
            pl.BlockSpec((tm, HEAD_PAD), row(COL_KR // HEAD_PAD)),
            pl.BlockSpec((tm, HEAD_PAD), row(COL_KRSW // HEAD_PAD)),
            pl.BlockSpec((tm, HEAD_PAD), row(0)),
            pl.BlockSpec((tm, HEAD_PAD), row(0)),
            pl.BlockSpec((1, Q_LORA), const),
            pl.BlockSpec((1, KV_LORA), const),
            pl.BlockSpec((Q_LORA, 2 * width), const),
            pl.BlockSpec((KV_LORA, 2 * width), const),
        ],
        out_specs=[pl.BlockSpec((tm, width), row(0))] * 3,
        out_shape=(jax.ShapeDtypeStruct((TOKENS, width), BF16),) * 3,
        compiler_params=_cparams(("parallel",), 48),
        name="qkv_prep",
    )(proj, proj, proj, proj, c_tab, s_tab, qg, kvg, wq, wkv)


def _attn_kernel(q_ref, k_ref, v_ref, o_ref):
    tq = TQ_ATTN
    qi = pl.program_id(2)
    nt = (((1,), (1,)), ((), ()))
    row_id = lax.broadcasted_iota(jnp.int32, (tq, tq), 0)
    col_id = lax.broadcasted_iota(jnp.int32, (tq, tq), 1)
    out = None
    for h in range(2):
        lo, hi = h * HEAD_PAD, (h + 1) * HEAD_PAD
        q = q_ref[:, lo:hi]

        def step(j, carry, masked):
            m, l, acc = carry
            start = pl.multiple_of(j * tq, tq)
            k = k_ref[pl.ds(start, tq), lo:hi]
            v = v_ref[pl.ds(start, tq), lo:hi]
            s = lax.dot_general(q, k, nt, preferred_element_type=F32)
            if masked:
                s = jnp.where(row_id >= col_id, s, -jnp.inf)
            m_new = jnp.maximum(m, jnp.max(s, axis=-1, keepdims=True))
            p = jnp.exp(s - m_new)
            alpha = jnp.exp(m - m_new)
            l = alpha * l + jnp.sum(p, axis=-1, keepdims=True)
            acc = alpha * acc + jnp.dot(p.astype(BF16), v, preferred_element_type=F32)
            return m_new, l, acc

        init = (jnp.full((tq, 1), -jnp.inf, F32), jnp.zeros((tq, 1), F32), jnp.zeros((tq, HEAD_PAD), F32))
        carry = lax.fori_loop(0, qi, functools.partial(step, masked=False), init)
        _, l, acc = step(qi, carry, True)
        res = acc / l
        out = res if out is None else out + res
    o_ref[...] = out.astype(BF16)


def _attention(q, k, v):
    tq = TQ_ATTN
    nq = SEQ // tq
    pair = 2 * HEAD_PAD
    return pl.pallas_call(
        _attn_kernel,
        grid=(BATCH, N_HEADS // 2, nq),
        in_specs=[
            pl.BlockSpec((tq, pair), lambda b, hp, i: (b * nq + i, hp)),
            pl.BlockSpec((SEQ, pair), lambda b, hp, i: (b, hp)),
            pl.BlockSpec((SEQ, pair), lambda b, hp, i: (b, hp)),
        ],
        out_specs=pl.BlockSpec((tq, HEAD_PAD), lambda b, hp, i: (b * nq + i, hp)),
        out_shape=jax.ShapeDtypeStruct((TOKENS, N_HEADS * V_HEAD), BF16),
        compiler_params=_cparams(("parallel", "parallel", "arbitrary"), 48),
        name="mla_attention",
    )(q, k, v)


def _mixer_tail_kernel(gates_ref, sc_ref, sch_ref, gv_ref, gg_ref, gvh_ref, ggh_ref, att_ref, x_ref,
                       cwa_ref, woa_ref, cwc_ref, lng_ref, lnb_ref, woc_ref, wob_ref, wo_ref,
                       out_ref, cbuf, ubuf, vbuf):
    tm = TM_MIX
    has_past = (pl.program_id(0) % (SEQ // tm)) != 0

    sc = sc_ref[...]
    sc_b = sc[:, 0:SC_WIDTH].astype(F32)
    cbuf[8:8 + tm, :] = sc[:, SC_WIDTH:2 * SC_WIDTH].astype(F32) * sc[:, 2 * SC_WIDTH:].astype(F32)
    sch = sch_ref[...].astype(F32)[HALO_A - 8:HALO_A, :]
    cbuf[0:8, :] = jnp.where(has_past, sch[:, SC_WIDTH:2 * SC_WIDTH] * sch[:, 2 * SC_WIDTH:], 0.0)
    conv_a = cwa_ref[0:1, :] * cbuf[6:6 + tm, :]
    for t in range(1, SC_KERNEL):
        conv_a = conv_a + cwa_ref[t:t + 1, :] * cbuf[6 + t:6 + t + tm, :]
    y_a = jnp.dot((sc_b * conv_a).astype(BF16), woa_ref[...], preferred_element_type=F32)

    ubuf[HALO_C:HALO_C + tm, :] = gv_ref[...].astype(F32) * _sigmoid(gg_ref[...].astype(F32))
    ubuf[0:HALO_C, :] = jnp.where(has_past, gvh_ref[...].astype(F32) * _sigmoid(ggh_ref[...].astype(F32)), 0.0)
    base = HALO_C - (CONF_KERNEL - 1)
    for r0 in range(0, tm, CONV_CHUNK):
        acc = cwc_ref[0:1, :] * ubuf[r0 + base:r0 + base + CONV_CHUNK, :]
        for t in range(1, CONF_KERNEL):
            acc = acc + cwc_ref[t:t + 1, :] * ubuf[r0 + base + t:r0 + base + t + CONV_CHUNK, :]
        mu = jnp.mean(acc, axis=-1, keepdims=True)
        xc = acc - mu
        var = jnp.mean(xc * xc, axis=-1, keepdims=True)
        y = xc * lax.rsqrt(var + EPS) * lng_ref[...] + lnb_ref[...]
        vbuf[r0:r0 + CONV_CHUNK, :] = (y * _sigmoid(y)).astype(BF16)
    y_c = jnp.dot(vbuf[...], woc_ref[...], preferred_element_type=F32)

    y_b = jnp.dot(att_ref[...], wob_ref[...], preferred_element_type=F32)

    g = gates_ref[...]
    merged = (_sigmoid(g[:, 0:D_MODEL].astype(F32)) * y_a
              + _sigmoid(g[:, D_MODEL:2 * D_MODEL].astype(F32)) * y_b
              + _sigmoid(g[:, 2 * D_MODEL:].astype(F32)) * y_c)
    out_ref[...] = x_ref[...] + jnp.dot(merged.astype(BF16), wo_ref[...], preferred_element_type=F32)


def _mixer_tail(proj, att, x, cwa, woa, cwc, lng, lnb, woc, wob, wo):
    tm = TM_MIX
    row = lambda width, col: pl.BlockSpec((tm, width), lambda i: (i, col // width))
    halo = lambda rows, width, col: pl.BlockSpec(
        (rows, width), lambda i: (jnp.maximum(i * (tm // rows) - 1, 0), col // width))
    const = lambda a: pl.BlockSpec(a.shape, lambda i: (0,) * a.ndim)
    weights = (cwa, woa, cwc, lng, lnb, woc, wob, wo)
    return pl.pallas_call(
        _mixer_tail_kernel,
        grid=(TOKENS // tm,),
        in_specs=[
            row(3 * D_MODEL, COL_GATES),
            row(3 * SC_WIDTH, COL_SC),
            halo(HALO_A, 3 * SC_WIDTH, COL_SC),
            row(CONF_WIDTH, COL_GLU),
            row(CONF_WIDTH, COL_GLU + CONF_WIDTH),
            halo(HALO_C, CONF_WIDTH, COL_GLU),
            halo(HALO_C, CONF_WIDTH, COL_GLU + CONF_WIDTH),
            pl.BlockSpec((tm, N_HEADS * V_HEAD), lambda i: (i, 0)),
            pl.BlockSpec((tm, D_MODEL), lambda i: (i, 0)),
        ] + [const(a) for a in weights],
        out_specs=pl.BlockSpec((tm, D_MODEL), lambda i: (i, 0)),
        out_shape=jax.ShapeDtypeStruct((TOKENS, D_MODEL), F32),
        scratch_shapes=[
            pltpu.VMEM((tm + 8, SC_WIDTH), F32),
            pltpu.VMEM((tm + HALO_C, CONF_WIDTH), F32),
            pltpu.VMEM((tm, CONF_WIDTH), BF16),
        ],
        compiler_params=_cparams(("parallel",), 56),
        name="mixer_tail",
    )(proj, proj, proj, proj, proj, proj, proj, att, x, *weights)


def _route(logits):
    lane = lax.broadcasted_iota(jnp.int32, logits.shape, 1)
    lane_f = lane.astype(F32)
    neg = -jnp.inf
    big = float(LANES)
    is_grp = (lane >= ROUTER_GROUP_LANE) & (lane < ROUTER_GROUP_LANE + N_GROUPS)
    glog = jnp.where(is_grp, logits, neg)
    gmax = jnp.max(glog, axis=-1, keepdims=True)
    gidx = jnp.min(jnp.where(glog == gmax, lane_f, big), axis=-1, keepdims=True)
    p_sel = 1.0 / jnp.sum(jnp.exp(glog - gmax), axis=-1, keepdims=True)
    first = (gidx - ROUTER_GROUP_LANE) * EXPERTS_PER_GROUP
    in_grp = (lane_f >= first) & (lane_f < first + EXPERTS_PER_GROUP)
    el = jnp.where(in_grp, logits, neg)
    m1 = jnp.max(el, axis=-1, keepdims=True)
    i1 = jnp.min(jnp.where(el == m1, lane_f, big), axis=-1, keepdims=True)
    el2 = jnp.where(lane_f == i1, neg, el)
    m2 = jnp.max(el2, axis=-1, keepdims=True)
    i2 = jnp.min(jnp.where(el2 == m2, lane_f, big), axis=-1, keepdims=True)
    e2 = jnp.exp(m2 - m1)
    w1 = p_sel / (1.0 + e2)
    w2 = w1 * e2
    return jnp.where(lane_f == i1, w1, 0.0) + jnp.where(lane_f == i2, w2, 0.0)


def _moe_kernel(x_ref, g_ref, wrh_ref, wrl_ref, br_ref, wgu_ref, wd_ref, out_ref, h_ref, comb_ref, acc_ref):
    e = pl.program_id(1)

    @pl.when(e == 0)
    def _():
        h = _rms(x_ref[...], g_ref[...])
        h_hi = h.astype(BF16)
        h_lo = (h - h_hi.astype(F32)).astype(BF16)
        h_ref[...] = h_hi
        logits = (jnp.dot(h_hi, wrh_ref[...], preferred_element_type=F32)
                  + jnp.dot(h_hi, wrl_ref[...], preferred_element_type=F32)
                  + jnp.dot(h_lo, wrh_ref[...], preferred_element_type=F32)
                  + br_ref[...])
        comb_ref[...] = _route(logits)
        acc_ref[...] = jnp.zeros_like(acc_ref)

    lane = lax.broadcasted_iota(jnp.int32, comb_ref.shape, 1)
    w_e = jnp.sum(jnp.where(lane == e, comb_ref[...], 0.0), axis=-1, keepdims=True)
    gu = jnp.dot(h_ref[...], wgu_ref[0], preferred_element_type=F32)
    hg = gu[:, :EXPERT_HIDDEN]
    hu = gu[:, EXPERT_HIDDEN:]
    hh = hg * _sigmoid(hg) * hu * w_e
    acc_ref[...] += jnp.dot(hh.astype(BF16), wd_ref[0], preferred_element_type=F32)

    @pl.when(e == N_EXPERTS - 1)
    def _():
        out_ref[...] = x_ref[...] + acc_ref[...]


def _moe(x, g, wrh, wrl, br, wgu, wd):
    tm = TM_MOE
    const = lambda i, e: (0, 0)
    return pl.pallas_call(
        _moe_kernel,
        grid=(TOKENS // tm, N_EXPERTS),
        in_specs=[
            pl.BlockSpec((tm, D_MODEL), lambda i, e: (i, 0)),
            pl.BlockSpec((1, D_MODEL), const),
            pl.BlockSpec((D_MODEL, LANES), const),
            pl.BlockSpec((D_MODEL, LANES), const),
            pl.BlockSpec((1, LANES), const),
            pl.BlockSpec((1, D_MODEL, 2 * EXPERT_HIDDEN), lambda i, e: (e, 0, 0)),
            pl.BlockSpec((1, EXPERT_HIDDEN, D_MODEL), lambda i, e: (e, 0, 0)),
        ],
        out_specs=pl.BlockSpec((tm, D_MODEL), lambda i, e: (i, 0)),
        out_shape=jax.ShapeDtypeStruct((TOKENS, D_MODEL), F32),
        scratch_shapes=[
            pltpu.VMEM((tm, D_MODEL), BF16),
            pltpu.VMEM((tm, LANES), F32),
            pltpu.VMEM((tm, D_MODEL), F32),
        ],
        compiler_params=_cparams(("parallel", "arbitrary"), 48),
        name="hier_moe",
    )(x, g, wrh, wrl, br, wgu, wd)


def _ple_kernel(x_ref, p_ref, g_ref, wg_ref, wp_ref, fg_ref, out_ref, *, final):
    x = x_ref[...]
    h = _rms(x, g_ref[...]).astype(BF16)
    gate = _sigmoid(jnp.dot(h, wg_ref[...], preferred_element_type=F32))
    emb = jnp.dot(p_ref[...].astype(BF16), wp_ref[...], preferred_element_type=F32)
    y = x + gate * emb
    if final:
        y = _rms(y, fg_ref[...])
    out_ref[...] = y


def _ple(x, p, g, wg, wp, fg, final):
    tm = TM_PLE
    const = lambda i: (0, 0)
    return pl.pallas_call(
        functools.partial(_ple_kernel, final=final),
        grid=(TOKENS // tm,),
        in_specs=[
            pl.BlockSpec((tm, D_MODEL), lambda i: (i, 0)),
            pl.BlockSpec((tm, PLE_DIM), lambda i: (i, 0)),
            pl.BlockSpec((1, D_MODEL), const),
            pl.BlockSpec((D_MODEL, D_MODEL), const),
            pl.BlockSpec((PLE_DIM, D_MODEL), const),
            pl.BlockSpec((1, D_MODEL), const),
        ],
        out_specs=pl.BlockSpec((tm, D_MODEL), lambda i: (i, 0)),
        out_shape=jax.ShapeDtypeStruct((TOKENS, D_MODEL), F32),
        compiler_params=_cparams(("parallel",), 48),
        name="ple_gate",
    )(x, p, g, wg, wp, fg)


def _in_proj_weight(w_in):
    o = 3 * D_MODEL + 3 * SC_WIDTH
    q_lat = w_in[:, o:o + Q_LORA]
    o += Q_LORA
    kv_lat = w_in[:, o:o + KV_LORA]
    o += KV_LORA
    k_rope = w_in[:, o:o + QK_ROPE]
    o += QK_ROPE
    glu = w_in[:, o:]
    half = QK_ROPE // 2
    zn = jnp.zeros((D_MODEL, QK_NOPE), F32)
    zp = jnp.zeros((D_MODEL, HEAD_PAD - QK_NOPE - QK_ROPE), F32)
    kr = jnp.concatenate([zn, k_rope, zp], axis=1)
    krsw = jnp.concatenate([zn, k_rope[:, half:], k_rope[:, :half], zp], axis=1)
    w = jnp.concatenate([w_in[:, :3 * D_MODEL + 3 * SC_WIDTH], q_lat, kv_lat, glu, kr, krsw], axis=1)
    return w.astype(BF16)


def _q_weight(w_uq):
    half = QK_ROPE // 2
    scale = (QK_NOPE + QK_ROPE) ** -0.5
    w = (w_uq * scale).reshape(Q_LORA, N_HEADS, QK_NOPE + QK_ROPE)
    nope, rope = w[:, :, :QK_NOPE], w[:, :, QK_NOPE:]
    rope_sw = jnp.concatenate([rope[:, :, half:], rope[:, :, :half]], axis=2)
    zp = jnp.zeros((Q_LORA, N_HEADS, HEAD_PAD - QK_NOPE - QK_ROPE), F32)
    plain = jnp.concatenate([nope, rope, zp], axis=2).reshape(Q_LORA, N_HEADS * HEAD_PAD)
    swapped = jnp.concatenate([jnp.zeros_like(nope), rope_sw, zp], axis=2).reshape(Q_LORA, N_HEADS * HEAD_PAD)
    return jnp.concatenate([plain, swapped], axis=1).astype(BF16)


def _kv_weight(w_ukv):
    w = w_ukv.reshape(KV_LORA, N_HEADS, QK_NOPE + V_HEAD)
    k_nope, v = w[:, :, :QK_NOPE], w[:, :, QK_NOPE:]
    z = jnp.zeros_like(v)
    k_part = jnp.concatenate([k_nope, jnp.zeros_like(k_nope)], axis=2).reshape(KV_LORA, N_HEADS * HEAD_PAD)
    odd = (jnp.arange(N_HEADS) % 2 == 1)[None, :, None]
    v_part = jnp.concatenate([jnp.where(odd, z, v), jnp.where(odd, v, z)], axis=2).reshape(KV_LORA, N_HEADS * HEAD_PAD)
    return jnp.concatenate([k_part, v_part], axis=1).astype(BF16)


def _router_weight(w_rg, b_rg, w_re, b_re):
    pad = LANES - N_EXPERTS - N_GROUPS
    w = jnp.concatenate([w_re, w_rg, jnp.zeros((D_MODEL, pad), F32)], axis=1)
    b = jnp.concatenate([b_re, b_rg, jnp.zeros((pad,), F32)]).reshape(1, LANES)
    w_hi = w.astype(BF16)
    w_lo = (w - w_hi.astype(F32)).astype(BF16)
    return w_hi, w_lo, b


def kernel(x, p, positions, ln_mix_g, w_in, conv_a_w, w_out_a, q_norm_g, w_uq, kv_norm_g, w_ukv, w_out_b, conv_c_w, ln_c_g, ln_c_b, w_out_c, w_o, ln_ffn_g, w_route_grp, b_route_grp, w_route_exp, b_route_exp, w_exp_gate, w_exp_up, w_exp_down, ln_ple_g, w_ple_gate, w_ple, final_norm_g):
    c_tab, s_tab = _rope_tables(positions)
    xf = x.reshape(TOKENS, D_MODEL)
    pf = p.reshape(DEPTH, TOKENS, PLE_DIM)
    row = lambda a: a.reshape(1, -1)
    for i in range(DEPTH):
        proj = _inproj(xf, row(ln_mix_g[i]), _in_proj_weight(w_in[i]))
        q, k, v = _qkv(proj, c_tab, s_tab, row(q_norm_g[i]), row(kv_norm_g[i]), _q_weight(w_uq[i]), _kv_weight(w_ukv[i]))
        att = _attention(q, k, v)
        xf = _mixer_tail(proj, att, xf, conv_a_w[i], w_out_a[i].astype(BF16), conv_c_w[i], row(ln_c_g[i]),
                         row(ln_c_b[i]), w_out_c[i].astype(BF16), w_out_b[i].astype(BF16), w_o[i].astype(BF16))
        wr_hi, wr_lo, br = _router_weight(w_route_grp[i], b_route_grp[i], w_route_exp[i], b_route_exp[i])
        wgu = jnp.concatenate([w_exp_gate[i], w_exp_up[i]], axis=2).astype(BF16)
        xf = _moe(xf, row(ln_ffn_g[i]), wr_hi, wr_lo, br, wgu, w_exp_down[i].astype(BF16))
        xf = _ple(xf, pf[i], row(ln_ple_g[i]), w_ple_gate[i].astype(BF16), w_ple[i].astype(BF16),
                  row(final_norm_g), final=(i == DEPTH - 1))
    return xf.reshape(BATCH, SEQ, D_MODEL)
```

```python
import functools

import jax
import jax.numpy as jnp
from jax import lax
from jax.experimental import pallas as pl
from jax.experimental.pallas import tpu as pltpu

D_MODEL = 1024
BATCH = 8
SEQ = 2048
DEPTH = 2
TOKENS = BATCH * SEQ
PLE_DIM = 256
SC_WIDTH = 512
SC_KERNEL = 3
N_HEADS = 8
QK_NOPE = 64
QK_ROPE = 32
V_HEAD = 64
Q_LORA = 768
KV_LORA = 256
ROPE_THETA = 10000.0
CONF_WIDTH = 512
CONF_KERNEL = 31
N_GROUPS = 4
EXPERTS_PER_GROUP = 8
N_EXPERTS = N_GROUPS * EXPERTS_PER_GROUP
EXPERT_HIDDEN = 256
EPS = 1e-6
LOG2_E = 1.4426950408889634

LANES = 128
HEAD_PAD = 128
F32 = jnp.float32
BF16 = jnp.bfloat16

COL_GATES = 0
COL_SC = 3 * D_MODEL
COL_QLAT = COL_SC + 3 * SC_WIDTH
COL_KVLAT = COL_QLAT + Q_LORA
COL_GLU = COL_KVLAT + KV_LORA
COL_KR = COL_GLU + 2 * CONF_WIDTH
COL_KRSW = COL_KR + HEAD_PAD
PROJ_COLS = COL_KRSW + HEAD_PAD

ROUTER_GROUP_LANE = N_EXPERTS

TM_INPROJ = 1024
TN_INPROJ = 2304
TM_QKV = 512
T_ATTN = 512
TM_MIX = 512
CONV_CHUNK = 64
HALO_C = 32
HALO_A = 16
TM_MOE = 1024
TM_PLE = 512


def _cparams(semantics, vmem_mb):
    return pltpu.CompilerParams(dimension_semantics=semantics, vmem_limit_bytes=vmem_mb * 1024 * 1024)


def _sigmoid(x):
    return 1.0 / (1.0 + jnp.exp(-x))


def _rms(x, g):
    return x * lax.rsqrt(jnp.mean(x * x, axis=-1, keepdims=True) + EPS) * g


def _rope_kernel(pos_ref, freq_ref, cos_ref, sin_ref):
    ang = pos_ref[...].astype(F32) * freq_ref[...]
    cos_ref[...] = jnp.cos(ang)
    sin_ref[...] = jnp.sin(ang)


def _rope_tables(positions):
    half = QK_ROPE // 2
    inv_freq = ROPE_THETA ** (-jnp.arange(0, QK_ROPE, 2, dtype=F32) / QK_ROPE)
    rows = TOKENS * half // LANES
    pos_rep = jnp.broadcast_to(positions.reshape(TOKENS, 1), (TOKENS, half)).reshape(rows, LANES)
    freq = jnp.tile(inv_freq, LANES // half).reshape(1, LANES)
    cos, sin = pl.pallas_call(
        _rope_kernel,
        out_shape=(jax.ShapeDtypeStruct((rows, LANES), F32),) * 2,
        name="rope_tables",
    )(pos_rep, freq)
    cos = cos.reshape(TOKENS, half)
    sin = sin.reshape(TOKENS, half)
    ones = jnp.ones((TOKENS, QK_NOPE), F32)
    zn = jnp.zeros((TOKENS, QK_NOPE), F32)
    zp = jnp.zeros((TOKENS, HEAD_PAD - QK_NOPE - QK_ROPE), F32)
    c_tab = jnp.concatenate([ones, cos, cos, zp], axis=1)
    s_tab = jnp.concatenate([zn, -sin, sin, zp], axis=1)
    return c_tab, s_tab


def _inproj_kernel(x_ref, g_ref, w_ref, o_ref, h_ref):
    @pl.when(pl.program_id(1) == 0)
    def _():
        h_ref[...] = _rms(x_ref[...], g_ref[...]).astype(BF16)

    o_ref[...] = jnp.dot(h_ref[...], w_ref[...], preferred_element_type=F32).astype(BF16)


def _inproj(x, g, w):
    tm, tn = TM_INPROJ, TN_INPROJ
    return pl.pallas_call(
        _inproj_kernel,
        grid=(TOKENS // tm, PROJ_COLS // tn),
        in_specs=[
            pl.BlockSpec((tm, D_MODEL), lambda i, j: (i, 0)),
            pl.BlockSpec((1, D_MODEL), lambda i, j: (0, 0)),
            pl.BlockSpec((D_MODEL, tn), lambda i, j: (0, j)),
        ],
        out_specs=pl.BlockSpec((tm, tn), lambda i, j: (i, j)),
        out_shape=jax.ShapeDtypeStruct((TOKENS, PROJ_COLS), BF16),
        scratch_shapes=[pltpu.VMEM((tm, D_MODEL), BF16)],
        compiler_params=_cparams(("parallel", "arbitrary"), 48),
        name="in_proj",
    )(x, g, w)


def _qkv_kernel(ql_ref, kvl_ref, kr_ref, krsw_ref, c_ref, s_ref, qg_ref, kvg_ref, wq_ref, wkv_ref,
                q_out, k_out, v_out):
    c = c_ref[...]
    s = s_ref[...]
    width = N_HEADS * HEAD_PAD
    qn = _rms(ql_ref[...].astype(F32), qg_ref[...]).astype(BF16)
    qq = jnp.dot(qn, wq_ref[...], preferred_element_type=F32)
    for h in range(N_HEADS):
        lo, hi = h * HEAD_PAD, (h + 1) * HEAD_PAD
        q_out[:, lo:hi] = (qq[:, lo:hi] * c + qq[:, width + lo:width + hi] * s).astype(BF16)
    kvn = _rms(kvl_ref[...].astype(F32), kvg_ref[...]).astype(BF16)
    kk = jnp.dot(kvn, wkv_ref[...], preferred_element_type=F32)
    kr = kr_ref[...].astype(F32) * c + krsw_ref[...].astype(F32) * s
    for h in range(N_HEADS):
        lo, hi = h * HEAD_PAD, (h + 1) * HEAD_PAD
        k_out[:, lo:hi] = (kk[:, lo:hi] + kr).astype(BF16)
    v_out[...] = kk[:, width:].astype(BF16)


def _qkv(proj, c_tab, s_tab, qg, kvg, wq, wkv):
    tm = TM_QKV
    width = N_HEADS * HEAD_PAD
    row = lambda blk: (lambda i: (i, blk))
    const = lambda i: (0, 0)
    return pl.pallas_call(
        _qkv_kernel,
        grid=(TOKENS // tm,),
        in_specs=[
            pl.BlockSpec((tm, Q_LORA), row(COL_QLAT // Q_LORA)),
            pl.BlockSpec((tm, KV_LORA), row(COL_KVLAT // KV_LORA)),
            pl.BlockSpec((tm, HEAD_PAD), row(COL_KR // HEAD_PAD)),
            pl.BlockSpec((tm, HEAD_PAD), row(COL_KRSW // HEAD_PAD)),
            pl.BlockSpec((tm, HEAD_PAD), row(0)),
            pl.BlockSpec((tm, HEAD_PAD), row(0)),
            pl.BlockSpec((1, Q_LORA), const),
            pl.BlockSpec((1, KV_LORA), const),
            pl.BlockSpec((Q_LORA, 2 * width), const),
            pl.BlockSpec((KV_LORA, 2 * width), const),
        ],
        out_specs=[pl.BlockSpec((tm, width), row(0))] * 3,
        out_shape=(jax.ShapeDtypeStruct((TOKENS, width), BF16),) * 3,
        compiler_params=_cparams(("parallel",), 48),
        name="qkv_prep",
    )(proj, proj, proj, proj, c_tab, s_tab, qg, kvg, wq, wkv)


def _attn_kernel(q_ref, k_ref, v_ref, o_ref, *state):
    t = T_ATTN
    qi = pl.program_id(2)
    nt = (((1,), (1,)), ((), ()))
    n_blk = t // LANES
    m_sc, l_sc, acc_sc = state[0:2], state[2:4], state[4:6]
    for h in range(2):
        m_sc[h][...] = jnp.full((t, LANES), -jnp.inf, F32)
        l_sc[h][...] = jnp.zeros((t, LANES), F32)
        acc_sc[h][...] = jnp.zeros((t, LANES), F32)

    def tile(j, masked):
        start = pl.multiple_of(j * t, t)
        for h in range(2):
            lo, hi = h * HEAD_PAD, (h + 1) * HEAD_PAD
            s = lax.dot_general(q_ref[:, lo:hi], k_ref[pl.ds(start, t), lo:hi], nt, preferred_element_type=F32)
            if masked:
                row_id = lax.broadcasted_iota(jnp.int32, (t, t), 0)
                col_id = lax.broadcasted_iota(jnp.int32, (t, t), 1)
                s = jnp.where(row_id >= col_id, s, -jnp.inf)
            blocks = [s[:, c * LANES:(c + 1) * LANES] for c in range(n_blk)]
            bmax = functools.reduce(jnp.maximum, blocks)
            m_old = m_sc[h][...]
            m_new = jnp.maximum(m_old, jnp.max(bmax, axis=-1, keepdims=True))
            alpha = jnp.exp2(m_old - m_new)
            ps = [jnp.exp2(b - m_new) for b in blocks]
            p = jnp.concatenate(ps, axis=1).astype(BF16)
            l_sc[h][...] = alpha * l_sc[h][...] + functools.reduce(jnp.add, ps)
            acc_sc[h][...] = alpha * acc_sc[h][...] + jnp.dot(p, v_ref[pl.ds(start, t), lo:hi],
                                                               preferred_element_type=F32)
            m_sc[h][...] = m_new

    def body(j, carry):
        tile(j, False)
        return carry

    lax.fori_loop(0, qi, body, 0)
    tile(qi, True)
    out = [acc_sc[h][...] / jnp.sum(l_sc[h][...], axis=-1, keepdims=True) for h in range(2)]
    o_ref[...] = (out[0] + out[1]).astype(BF16)


def _attention(q, k, v):
    t = T_ATTN
    nq = SEQ // t
    pair = 2 * HEAD_PAD
    return pl.pallas_call(
        _attn_kernel,
        grid=(BATCH, N_HEADS // 2, nq),
        in_specs=[
            pl.BlockSpec((t, pair), lambda b, hp, i: (b * nq + i, hp)),
            pl.BlockSpec((SEQ, pair), lambda b, hp, i: (b, hp)),
            pl.BlockSpec((SEQ, pair), lambda b, hp, i: (b, hp)),
        ],
        out_specs=pl.BlockSpec((t, HEAD_PAD), lambda b, hp, i: (b * nq + i, hp)),
        out_shape=jax.ShapeDtypeStruct((TOKENS, N_HEADS * V_HEAD), BF16),
        scratch_shapes=[pltpu.VMEM((t, LANES), F32)] * 6,
        compiler_params=_cparams(("parallel", "parallel", "arbitrary"), 48),
        name="mla_attention",
    )(q, k, v)


def _mixer_tail_kernel(gates_ref, sc_ref, sch_ref, gv_ref, gg_ref, gvh_ref, ggh_ref, att_ref, x_ref,
                       cwa_ref, woa_ref, cwc_ref, lng_ref, lnb_ref, woc_ref, wob_ref, wo_ref,
                       out_ref, cbuf, ubuf, shifted, vbuf):
    tm = TM_MIX
    has_past = (pl.program_id(0) % (SEQ // tm)) != 0

    sc = sc_ref[...]
    sc_b = sc[:, 0:SC_WIDTH].astype(F32)
    cbuf[8:8 + tm, :] = sc[:, SC_WIDTH:2 * SC_WIDTH].astype(F32) * sc[:, 2 * SC_WIDTH:].astype(F32)
    sch = sch_ref[...].astype(F32)[HALO_A - 8:HALO_A, :]
    cbuf[0:8, :] = jnp.where(has_past, sch[:, SC_WIDTH:2 * SC_WIDTH] * sch[:, 2 * SC_WIDTH:], 0.0)
    conv_a = cwa_ref[0:1, :] * cbuf[6:6 + tm, :]
    for t in range(1, SC_KERNEL):
        conv_a = conv_a + cwa_ref[t:t + 1, :] * cbuf[6 + t:6 + t + tm, :]
    y_a = jnp.dot((sc_b * conv_a).astype(BF16), woa_ref[...], preferred_element_type=F32)

    ubuf[HALO_C:HALO_C + tm, :] = gv_ref[...].astype(F32) * _sigmoid(gg_ref[...].astype(F32))
    ubuf[0:HALO_C, :] = jnp.where(has_past, gvh_ref[...].astype(F32) * _sigmoid(ggh_ref[...].astype(F32)), 0.0)
    rows = tm + HALO_C - 8
    for b in range(1, 8):
        shifted[b - 1, 0:rows, :] = ubuf[b:b + rows, :]
    base = HALO_C - (CONF_KERNEL - 1)
    for r0 in range(0, tm, CONV_CHUNK):
        acc = None
        for t in range(CONF_KERNEL):
            off = base + t
            a0 = r0 + off - off % 8
            src = ubuf[a0:a0 + CONV_CHUNK, :] if off % 8 == 0 else shifted[off % 8 - 1, a0:a0 + CONV_CHUNK, :]
            term = cwc_ref[t:t + 1, :] * src
            acc = term if acc is None else acc + term
        mu = jnp.mean(acc, axis=-1, keepdims=True)
        xc = acc - mu
        var = jnp.mean(xc * xc, axis=-1, keepdims=True)
        y = xc * lax.rsqrt(var + EPS) * lng_ref[...] + lnb_ref[...]
        vbuf[r0:r0 + CONV_CHUNK, :] = (y * _sigmoid(y)).astype(BF16)
    y_c = jnp.dot(vbuf[...], woc_ref[...], preferred_element_type=F32)

    y_b = jnp.dot(att_ref[...], wob_ref[...], preferred_element_type=F32)

    g = gates_ref[...]
    merged = (_sigmoid(g[:, 0:D_MODEL].astype(F32)) * y_a
              + _sigmoid(g[:, D_MODEL:2 * D_MODEL].astype(F32)) * y_b
              + _sigmoid(g[:, 2 * D_MODEL:].astype(F32)) * y_c)
    out_ref[...] = x_ref[...] + jnp.dot(merged.astype(BF16), wo_ref[...], preferred_element_type=F32)


def _mixer_tail(proj, att, x, cwa, woa, cwc, lng, lnb, woc, wob, wo):
    tm = TM_MIX
    row = lambda width, col: pl.BlockSpec((tm, width), lambda i: (i, col // width))
    halo = lambda rows, width, col: pl.BlockSpec(
        (rows, width), lambda i: (jnp.maximum(i * (tm // rows) - 1, 0), col // width))
    const = lambda a: pl.BlockSpec(a.shape, lambda i: (0,) * a.ndim)
    weights = (cwa, woa, cwc, lng, lnb, woc, wob, wo)
    return pl.pallas_call(
        _mixer_tail_kernel,
        grid=(TOKENS // tm,),
        in_specs=[
            row(3 * D_MODEL, COL_GATES),
            row(3 * SC_WIDTH, COL_SC),
            halo(HALO_A, 3 * SC_WIDTH, COL_SC),
            row(CONF_WIDTH, COL_GLU),
            row(CONF_WIDTH, COL_GLU + CONF_WIDTH),
            halo(HALO_C, CONF_WIDTH, COL_GLU),
            halo(HALO_C, CONF_WIDTH, COL_GLU + CONF_WIDTH),
            pl.BlockSpec((tm, N_HEADS * V_HEAD), lambda i: (i, 0)),
            pl.BlockSpec((tm, D_MODEL), lambda i: (i, 0)),
        ] + [const(a) for a in weights],
        out_specs=pl.BlockSpec((tm, D_MODEL), lambda i: (i, 0)),
        out_shape=jax.ShapeDtypeStruct((TOKENS, D_MODEL), F32),
        scratch_shapes=[
            pltpu.VMEM((tm + 8, SC_WIDTH), F32),
            pltpu.VMEM((tm + HALO_C, CONF_WIDTH), F32),
            pltpu.VMEM((7, tm + HALO_C - 8, CONF_WIDTH), F32),
            pltpu.VMEM((tm, CONF_WIDTH), BF16),
        ],
        compiler_params=_cparams(("parallel",), 56),
        name="mixer_tail",
    )(proj, proj, proj, proj, proj, proj, proj, att, x, *weights)


def _route(logits):
    lane = lax.broadcasted_iota(jnp.int32, logits.shape, 1)
    lane_f = lane.astype(F32)
    neg = -jnp.inf
    big = float(LANES)
    is_grp = (lane >= ROUTER_GROUP_LANE) & (lane < ROUTER_GROUP_LANE + N_GROUPS)
    glog = jnp.where(is_grp, logits, neg)
    gmax = jnp.max(glog, axis=-1, keepdims=True)
    gidx = jnp.min(jnp.where(glog == gmax, lane_f, big), axis=-1, keepdims=True)
    p_sel = 1.0 / jnp.sum(jnp.exp(glog - gmax), axis=-1, keepdims=True)
    first = (gidx - ROUTER_GROUP_LANE) * EXPERTS_PER_GROUP
    in_grp = (lane_f >= first) & (lane_f < first + EXPERTS_PER_GROUP)
    el = jnp.where(in_grp, logits, neg)
    m1 = jnp.max(el, axis=-1, keepdims=True)
    i1 = jnp.min(jnp.where(el == m1, lane_f, big), axis=-1, keepdims=True)
    el2 = jnp.where(lane_f == i1, neg, el)
    m2 = jnp.max(el2, axis=-1, keepdims=True)
    i2 = jnp.min(jnp.where(el2 == m2, lane_f, big), axis=-1, keepdims=True)
    e2 = jnp.exp(m2 - m1)
    w1 = p_sel / (1.0 + e2)
    w2 = w1 * e2
    return jnp.where(lane_f == i1, w1, 0.0) + jnp.where(lane_f == i2, w2, 0.0)


def _moe_kernel(x_ref, g_ref, wrh_ref, wrl_ref, br_ref, wgu_ref, wd_ref, out_ref, h_ref, comb_ref, acc_ref):
    e = pl.program_id(1)

    @pl.when(e == 0)
    def _():
        h = _rms(x_ref[...], g_ref[...])
        h_hi = h.astype(BF16)
        h_lo = (h - h_hi.astype(F32)).astype(BF16)
        h_ref[...] = h_hi
        logits = (jnp.dot(h_hi, wrh_ref[...], preferred_element_type=F32)
                  + jnp.dot(h_hi, wrl_ref[...], preferred_element_type=F32)
                  + jnp.dot(h_lo, wrh_ref[...], preferred_element_type=F32)
                  + br_ref[...])
        comb_ref[...] = _route(logits)
        acc_ref[...] = jnp.zeros_like(acc_ref)

    lane = lax.broadcasted_iota(jnp.int32, comb_ref.shape, 1)
    w_e = jnp.sum(jnp.where(lane == e, comb_ref[...], 0.0), axis=-1, keepdims=True)
    gu = jnp.dot(h_ref[...], wgu_ref[0], preferred_element_type=F32)
    hg = gu[:, :EXPERT_HIDDEN]
    hu = gu[:, EXPERT_HIDDEN:]
    hh = hg * _sigmoid(hg) * hu * w_e
    acc_ref[...] += jnp.dot(hh.astype(BF16), wd_ref[0], preferred_element_type=F32)

    @pl.when(e == N_EXPERTS - 1)
    def _():
        out_ref[...] = x_ref[...] + acc_ref[...]


def _moe(x, g, wrh, wrl, br, wgu, wd):
    tm = TM_MOE
    const = lambda i, e: (0, 0)
    return pl.pallas_call(
        _moe_kernel,
        grid=(TOKENS // tm, N_EXPERTS),
        in_specs=[
            pl.BlockSpec((tm, D_MODEL), lambda i, e: (i, 0)),
            pl.BlockSpec((1, D_MODEL), const),
            pl.BlockSpec((D_MODEL, LANES), const),
            pl.BlockSpec((D_MODEL, LANES), const),
            pl.BlockSpec((1, LANES), const),
            pl.BlockSpec((1, D_MODEL, 2 * EXPERT_HIDDEN), lambda i, e: (e, 0, 0)),
            pl.BlockSpec((1, EXPERT_HIDDEN, D_MODEL), lambda i, e: (e, 0, 0)),
        ],
        out_specs=pl.BlockSpec((tm, D_MODEL), lambda i, e: (i, 0)),
        out_shape=jax.ShapeDtypeStruct((TOKENS, D_MODEL), F32),
        scratch_shapes=[
            pltpu.VMEM((tm, D_MODEL), BF16),
            pltpu.VMEM((tm, LANES), F32),
            pltpu.VMEM((tm, D_MODEL), F32),
        ],
        compiler_params=_cparams(("parallel", "arbitrary"), 48),
        name="hier_moe",
    )(x, g, wrh, wrl, br, wgu, wd)


def _ple_kernel(x_ref, p_ref, g_ref, wg_ref, wp_ref, fg_ref, out_ref, *, final):
    x = x_ref[...]
    h = _rms(x, g_ref[...]).astype(BF16)
    gate = _sigmoid(jnp.dot(h, wg_ref[...], preferred_element_type=F32))
    emb = jnp.dot(p_ref[...].astype(BF16), wp_ref[...], preferred_element_type=F32)
    y = x + gate * emb
    if final:
        y = _rms(y, fg_ref[...])
    out_ref[...] = y


def _ple(x, p, g, wg, wp, fg, final):
    tm = TM_PLE
    const = lambda i: (0, 0)
    return pl.pallas_call(
        functools.partial(_ple_kernel, final=final),
        grid=(TOKENS // tm,),
        in_specs=[
            pl.BlockSpec((tm, D_MODEL), lambda i: (i, 0)),
            pl.BlockSpec((tm, PLE_DIM), lambda i: (i, 0)),
            pl.BlockSpec((1, D_MODEL), const),
            pl.BlockSpec((D_MODEL, D_MODEL), const),
            pl.BlockSpec((PLE_DIM, D_MODEL), const),
            pl.BlockSpec((1, D_MODEL), const),
        ],
        out_specs=pl.BlockSpec((tm, D_MODEL), lambda i: (i, 0)),
        out_shape=jax.ShapeDtypeStruct((TOKENS, D_MODEL), F32),
        compiler_params=_cparams(("parallel",), 48),
        name="ple_gate",
    )(x, p, g, wg, wp, fg)


def _in_proj_weight(w_in):
    o = 3 * D_MODEL + 3 * SC_WIDTH
    q_lat = w_in[:, o:o + Q_LORA]
    o += Q_LORA
    kv_lat = w_in[:, o:o + KV_LORA]
    o += KV_LORA
    k_rope = w_in[:, o:o + QK_ROPE]
    o += QK_ROPE
    glu = w_in[:, o:]
    half = QK_ROPE // 2
    zn = jnp.zeros((D_MODEL, QK_NOPE), F32)
    zp = jnp.zeros((D_MODEL, HEAD_PAD - QK_NOPE - QK_ROPE), F32)
    kr = jnp.concatenate([zn, k_rope, zp], axis=1)
    krsw = jnp.concatenate([zn, k_rope[:, half:], k_rope[:, :half], zp], axis=1)
    w = jnp.concatenate([w_in[:, :3 * D_MODEL + 3 * SC_WIDTH], q_lat, kv_lat, glu, kr, krsw], axis=1)
    return w.astype(BF16)


def _q_weight(w_uq):
    half = QK_ROPE // 2
    scale = (QK_NOPE + QK_ROPE) ** -0.5 * LOG2_E
    w = (w_uq * scale).reshape(Q_LORA, N_HEADS, QK_NOPE + QK_ROPE)
    nope, rope = w[:, :, :QK_NOPE], w[:, :, QK_NOPE:]
    rope_sw = jnp.concatenate([rope[:, :, half:], rope[:, :, :half]], axis=2)
    zp = jnp.zeros((Q_LORA, N_HEADS, HEAD_PAD - QK_NOPE - QK_ROPE), F32)
    plain = jnp.concatenate([nope, rope, zp], axis=2).reshape(Q_LORA, N_HEADS * HEAD_PAD)
    swapped = jnp.concatenate([jnp.zeros_like(nope), rope_sw, zp], axis=2).reshape(Q_LORA, N_HEADS * HEAD_PAD)
    return jnp.concatenate([plain, swapped], axis=1).astype(BF16)


def _kv_weight(w_ukv):
    w = w_ukv.reshape(KV_LORA, N_HEADS, QK_NOPE + V_HEAD)
    k_nope, v = w[:, :, :QK_NOPE], w[:, :, QK_NOPE:]
    z = jnp.zeros_like(v)
    k_part = jnp.concatenate([k_nope, jnp.zeros_like(k_nope)], axis=2).reshape(KV_LORA, N_HEADS * HEAD_PAD)
    odd = (jnp.arange(N_HEADS) % 2 == 1)[None, :, None]
    v_part = jnp.concatenate([jnp.where(odd, z, v), jnp.where(odd, v, z)], axis=2).reshape(KV_LORA, N_HEADS * HEAD_PAD)
    return jnp.concatenate([k_part, v_part], axis=1).astype(BF16)


def _router_weight(w_rg, b_rg, w_re, b_re):
    pad = LANES - N_EXPERTS - N_GROUPS
    w = jnp.concatenate([w_re, w_rg, jnp.zeros((D_MODEL, pad), F32)], axis=1)
    b = jnp.concatenate([b_re, b_rg, jnp.zeros((pad,), F32)]).reshape(1, LANES)
    w_hi = w.astype(BF16)
    w_lo = (w - w_hi.astype(F32)).astype(BF16)
    return w_hi, w_lo, b


def kernel(x, p, positions, ln_mix_g, w_in, conv_a_w, w_out_a, q_norm_g, w_uq, kv_norm_g, w_ukv, w_out_b, conv_c_w, ln_c_g, ln_c_b, w_out_c, w_o, ln_ffn_g, w_route_grp, b_route_grp, w_route_exp, b_route_exp, w_exp_gate, w_exp_up, w_exp_down, ln_ple_g, w_ple_gate, w_ple, final_norm_g):
    c_tab, s_tab = _rope_tables(positions)
    xf = x.reshape(TOKENS, D_MODEL)
    pf = p.reshape(DEPTH, TOKENS, PLE_DIM)
    row = lambda a: a.reshape(1, -1)
    for i in range(DEPTH):
        proj = _inproj(xf, row(ln_mix_g[i]), _in_proj_weight(w_in[i]))
        q, k, v = _qkv(proj, c_tab, s_tab, row(q_norm_g[i]), row(kv_norm_g[i]), _q_weight(w_uq[i]), _kv_weight(w_ukv[i]))
        att = _attention(q, k, v)
        xf = _mixer_tail(proj, att, xf, conv_a_w[i], w_out_a[i].astype(BF16), conv_c_w[i], row(ln_c_g[i]),
                         row(ln_c_b[i]), w_out_c[i].astype(BF16), w_out_b[i].astype(BF16), w_o[i].astype(BF16))
        wr_hi, wr_lo, br = _router_weight(w_route_grp[i], b_route_grp[i], w_route_exp[i], b_route_exp[i])
        wgu = jnp.concatenate([w_exp_gate[i], w_exp_up[i]], axis=2).astype(BF16)
        xf = _moe(xf, row(ln_ffn_g[i]), wr_hi, wr_lo, br, wgu, w_exp_down[i].astype(BF16))
        xf = _ple(xf, pf[i], row(ln_ple_g[i]), w_ple_gate[i].astype(BF16), w_ple[i].astype(BF16),
                  row(final_norm_g), final=(i == DEPTH - 1))
    return xf.reshape(BATCH, SEQ, D_MODEL)
```

```python
import functools

import jax
import jax.numpy as jnp
from jax import lax
from jax.experimental import pallas as pl
from jax.experimental.pallas import tpu as pltpu

D_MODEL = 1024
BATCH = 8
SEQ = 2048
DEPTH = 2
TOKENS = BATCH * SEQ
PLE_DIM = 256
SC_WIDTH = 512
SC_KERNEL = 3
N_HEADS = 8
QK_NOPE = 64
QK_ROPE = 32
V_HEAD = 64
Q_LORA = 768
KV_LORA = 256
ROPE_THETA = 10000.0
CONF_WIDTH = 512
CONF_KERNEL = 31
N_GROUPS = 4
EXPERTS_PER_GROUP = 8
N_EXPERTS = N_GROUPS * EXPERTS_PER_GROUP
EXPERT_HIDDEN = 256
EPS = 1e-6
LOG2_E = 1.4426950408889634

LANES = 128
HEAD_PAD = 128
F32 = jnp.float32
BF16 = jnp.bfloat16

COL_GATES = 0
COL_SC = 3 * D_MODEL
COL_QLAT = COL_SC + 3 * SC_WIDTH
COL_KVLAT = COL_QLAT + Q_LORA
COL_GLU = COL_KVLAT + KV_LORA
COL_KR = COL_GLU + 2 * CONF_WIDTH
COL_KRSW = COL_KR + HEAD_PAD
PROJ_COLS = COL_KRSW + HEAD_PAD

ROUTER_GROUP_LANE = N_EXPERTS

TM_INPROJ = 1024
TN_INPROJ = 2304
TM_QKV = 512
T_ATTN = 512
TM_MIX = 512
CONV_CHUNK = 64
HALO_C = 32
HALO_A = 16
TM_ROUTE = 512
TM_EXP = 256
TM_COMB = 256
TOP_K = 2
MAX_TILES = TOKENS * TOP_K // TM_EXP + N_EXPERTS
SLAB = D_MODEL // LANES
GATHER_UNROLL = 16
META_E1, META_E2, META_R1, META_R2, META_W1, META_W2 = range(6)


def _cparams(semantics, vmem_mb):
    return pltpu.CompilerParams(dimension_semantics=semantics, vmem_limit_bytes=vmem_mb * 1024 * 1024)


def _sigmoid(x):
    return 1.0 / (1.0 + jnp.exp(-x))


def _rms(x, g):
    return x * lax.rsqrt(jnp.mean(x * x, axis=-1, keepdims=True) + EPS) * g


def _rope_kernel(pos_ref, freq_ref, cos_ref, sin_ref):
    ang = pos_ref[...].astype(F32) * freq_ref[...]
    cos_ref[...] = jnp.cos(ang)
    sin_ref[...] = jnp.sin(ang)


def _rope_tables(positions):
    half = QK_ROPE // 2
    inv_freq = ROPE_THETA ** (-jnp.arange(0, QK_ROPE, 2, dtype=F32) / QK_ROPE)
    rows = TOKENS * half // LANES
    pos_rep = jnp.broadcast_to(positions.reshape(TOKENS, 1), (TOKENS, half)).reshape(rows, LANES)
    freq = jnp.tile(inv_freq, LANES // half).reshape(1, LANES)
    cos, sin = pl.pallas_call(
        _rope_kernel,
        out_shape=(jax.ShapeDtypeStruct((rows, LANES), F32),) * 2,
        name="rope_tables",
    )(pos_rep, freq)
    cos = cos.reshape(TOKENS, half)
    sin = sin.reshape(TOKENS, half)
    ones = jnp.ones((TOKENS, QK_NOPE), F32)
    zn = jnp.zeros((TOKENS, QK_NOPE), F32)
    zp = jnp.zeros((TOKENS, HEAD_PAD - QK_NOPE - QK_ROPE), F32)
    c_tab = jnp.concatenate([ones, cos, cos, zp], axis=1)
    s_tab = jnp.concatenate([zn, -sin, sin, zp], axis=1)
    return c_tab, s_tab


def _inproj_kernel(x_ref, g_ref, w_ref, o_ref, h_ref):
    @pl.when(pl.program_id(1) == 0)
    def _():
        h_ref[...] = _rms(x_ref[...], g_ref[...]).astype(BF16)

    o_ref[...] = jnp.dot(h_ref[...], w_ref[...], preferred_element_type=F32).astype(BF16)


def _inproj(x, g, w):
    tm, tn = TM_INPROJ, TN_INPROJ
    return pl.pallas_call(
        _inproj_kernel,
        grid=(TOKENS // tm, PROJ_COLS // tn),
        in_specs=[
            pl.BlockSpec((tm, D_MODEL), lambda i, j: (i, 0)),
            pl.BlockSpec((1, D_MODEL), lambda i, j: (0, 0)),
            pl.BlockSpec((D_MODEL, tn), lambda i, j: (0, j)),
        ],
        out_specs=pl.BlockSpec((tm, tn), lambda i, j: (i, j)),
        out_shape=jax.ShapeDtypeStruct((TOKENS, PROJ_COLS), BF16),
        scratch_shapes=[pltpu.VMEM((tm, D_MODEL), BF16)],
        compiler_params=_cparams(("parallel", "arbitrary"), 48),
        name="in_proj",
    )(x, g, w)


def _qkv_kernel(ql_ref, kvl_ref, kr_ref, krsw_ref, c_ref, s_ref, qg_ref, kvg_ref, wq_ref, wkv_ref,
                q_out, k_out, v_out):
    c = c_ref[...]
    s = s_ref[...]
    width = N_HEADS * HEAD_PAD
    qn = _rms(ql_ref[...].astype(F32), qg_ref[...]).astype(BF16)
    qq = jnp.dot(qn, wq_ref[...], preferred_element_type=F32)
    for h in range(N_HEADS):
        lo, hi = h * HEAD_PAD, (h + 1) * HEAD_PAD
        q_out[:, lo:hi] = (qq[:, lo:hi] * c + qq[:, width + lo:width + hi] * s).astype(BF16)
    kvn = _rms(kvl_ref[...].astype(F32), kvg_ref[...]).astype(BF16)
    kk = jnp.dot(kvn, wkv_ref[...], preferred_element_type=F32)
    kr = kr_ref[...].astype(F32) * c + krsw_ref[...].astype(F32) * s
    for h in range(N_HEADS):
        lo, hi = h * HEAD_PAD, (h + 1) * HEAD_PAD
        k_out[:, lo:hi] = (kk[:, lo:hi] + kr).astype(BF16)
    v_out[...] = kk[:, width:].astype(BF16)


def _qkv(proj, c_tab, s_tab, qg, kvg, wq, wkv):
    tm = TM_QKV
    width = N_HEADS * HEAD_PAD
    row = lambda blk: (lambda i: (i, blk))
    const = lambda i: (0, 0)
    return pl.pallas_call(
        _qkv_kernel,
        grid=(TOKENS // tm,),
        in_specs=[
            pl.BlockSpec((tm, Q_LORA), row(COL_QLAT // Q_LORA)),
            pl.BlockSpec((tm, KV_LORA), row(COL_KVLAT // KV_LORA)),
            pl.BlockSpec((tm, HEAD_PAD), row(COL_KR // HEAD_PAD)),
            pl.BlockSpec((tm, HEAD_PAD), row(COL_KRSW // HEAD_PAD)),
            pl.BlockSpec((tm, HEAD_PAD), row(0)),
            pl.BlockSpec((tm, HEAD_PAD), row(0)),
            pl.BlockSpec((1, Q_LORA), const),
            pl.BlockSpec((1, KV_LORA), const),
            pl.BlockSpec((Q_LORA, 2 * width), const),
            pl.BlockSpec((KV_LORA, 2 * width), const),
        ],
        out_specs=[pl.BlockSpec((tm, width), row(0))] * 3,
        out_shape=(jax.ShapeDtypeStruct((TOKENS, width), BF16),) * 3,
        compiler_params=_cparams(("parallel",), 48),
        name="qkv_prep",
    )(proj, proj, proj, proj, c_tab, s_tab, qg, kvg, wq, wkv)


def _attn_kernel(q_ref, k_ref, v_ref, o_ref, *state):
    t = T_ATTN
    qi = pl.program_id(2)
    nt = (((1,), (1,)), ((), ()))
    n_blk = t // LANES
    m_sc, l_sc, acc_sc = state[0:2], state[2:4], state[4:6]
    for h in range(2):
        m_sc[h][...] = jnp.full((t, LANES), -jnp.inf, F32)
        l_sc[h][...] = jnp.zeros((t, LANES), F32)
        acc_sc[h][...] = jnp.zeros((t, LANES), F32)

    def tile(j, masked):
        start = pl.multiple_of(j * t, t)
        for h in range(2):
            lo, hi = h * HEAD_PAD, (h + 1) * HEAD_PAD
            s = lax.dot_general(q_ref[:, lo:hi], k_ref[pl.ds(start, t), lo:hi], nt, preferred_element_type=F32)
            if masked:
                row_id = lax.broadcasted_iota(jnp.int32, (t, t), 0)
                col_id = lax.broadcasted_iota(jnp.int32, (t, t), 1)
                s = jnp.where(row_id >= col_id, s, -jnp.inf)
            blocks = [s[:, c * LANES:(c + 1) * LANES] for c in range(n_blk)]
            bmax = functools.reduce(jnp.maximum, blocks)
            m_old = m_sc[h][...]
            m_new = jnp.maximum(m_old, jnp.max(bmax, axis=-1, keepdims=True))
            alpha = jnp.exp2(m_old - m_new)
            ps = [jnp.exp2(b - m_new) for b in blocks]
            p = jnp.concatenate(ps, axis=1).astype(BF16)
            l_sc[h][...] = alpha * l_sc[h][...] + functools.reduce(jnp.add, ps)
            acc_sc[h][...] = alpha * acc_sc[h][...] + jnp.dot(p, v_ref[pl.ds(start, t), lo:hi],
                                                               preferred_element_type=F32)
            m_sc[h][...] = m_new

    def body(j, carry):
        tile(j, False)
        return carry

    lax.fori_loop(0, qi, body, 0)
    tile(qi, True)
    out = [acc_sc[h][...] / jnp.sum(l_sc[h][...], axis=-1, keepdims=True) for h in range(2)]
    o_ref[...] = (out[0] + out[1]).astype(BF16)


def _attention(q, k, v):
    t = T_ATTN
    nq = SEQ // t
    pair = 2 * HEAD_PAD
    return pl.pallas_call(
        _attn_kernel,
        grid=(BATCH, N_HEADS // 2, nq),
        in_specs=[
            pl.BlockSpec((t, pair), lambda b, hp, i: (b * nq + i, hp)),
            pl.BlockSpec((SEQ, pair), lambda b, hp, i: (b, hp)),
            pl.BlockSpec((SEQ, pair), lambda b, hp, i: (b, hp)),
        ],
        out_specs=pl.BlockSpec((t, HEAD_PAD), lambda b, hp, i: (b * nq + i, hp)),
        out_shape=jax.ShapeDtypeStruct((TOKENS, N_HEADS * V_HEAD), BF16),
        scratch_shapes=[pltpu.VMEM((t, LANES), F32)] * 6,
        compiler_params=_cparams(("parallel", "parallel", "arbitrary"), 48),
        name="mla_attention",
    )(q, k, v)


def _mixer_tail_kernel(gates_ref, sc_ref, sch_ref, gv_ref, gg_ref, gvh_ref, ggh_ref, att_ref, x_ref,
                       cwa_ref, woa_ref, cwc_ref, lng_ref, lnb_ref, woc_ref, wob_ref, wo_ref,
                       out_ref, cbuf, ubuf, shifted, vbuf):
    tm = TM_MIX
    has_past = (pl.program_id(0) % (SEQ // tm)) != 0

    sc = sc_ref[...]
    sc_b = sc[:, 0:SC_WIDTH].astype(F32)
    cbuf[8:8 + tm, :] = sc[:, SC_WIDTH:2 * SC_WIDTH].astype(F32) * sc[:, 2 * SC_WIDTH:].astype(F32)
    sch = sch_ref[...].astype(F32)[HALO_A - 8:HALO_A, :]
    cbuf[0:8, :] = jnp.where(has_past, sch[:, SC_WIDTH:2 * SC_WIDTH] * sch[:, 2 * SC_WIDTH:], 0.0)
    conv_a = cwa_ref[0:1, :] * cbuf[6:6 + tm, :]
    for t in range(1, SC_KERNEL):
        conv_a = conv_a + cwa_ref[t:t + 1, :] * cbuf[6 + t:6 + t + tm, :]
    y_a = jnp.dot((sc_b * conv_a).astype(BF16), woa_ref[...], preferred_element_type=F32)

    ubuf[HALO_C:HALO_C + tm, :] = gv_ref[...].astype(F32) * _sigmoid(gg_ref[...].astype(F32))
    ubuf[0:HALO_C, :] = jnp.where(has_past, gvh_ref[...].astype(F32) * _sigmoid(ggh_ref[...].astype(F32)), 0.0)
    rows = tm + HALO_C - 8
    for b in range(1, 8):
        shifted[b - 1, 0:rows, :] = ubuf[b:b + rows, :]
    base = HALO_C - (CONF_KERNEL - 1)
    for r0 in range(0, tm, CONV_CHUNK):
        acc = None
        for t in range(CONF_KERNEL):
            off = base + t
            a0 = r0 + off - off % 8
            src = ubuf[a0:a0 + CONV_CHUNK, :] if off % 8 == 0 else shifted[off % 8 - 1, a0:a0 + CONV_CHUNK, :]
            term = cwc_ref[t:t + 1, :] * src
            acc = term if acc is None else acc + term
        mu = jnp.mean(acc, axis=-1, keepdims=True)
        xc = acc - mu
        var = jnp.mean(xc * xc, axis=-1, keepdims=True)
        y = xc * lax.rsqrt(var + EPS) * lng_ref[...] + lnb_ref[...]
        vbuf[r0:r0 + CONV_CHUNK, :] = (y * _sigmoid(y)).astype(BF16)
    y_c = jnp.dot(vbuf[...], woc_ref[...], preferred_element_type=F32)

    y_b = jnp.dot(att_ref[...], wob_ref[...], preferred_element_type=F32)

    g = gates_ref[...]
    merged = (_sigmoid(g[:, 0:D_MODEL].astype(F32)) * y_a
              + _sigmoid(g[:, D_MODEL:2 * D_MODEL].astype(F32)) * y_b
              + _sigmoid(g[:, 2 * D_MODEL:].astype(F32)) * y_c)
    out_ref[...] = x_ref[...] + jnp.dot(merged.astype(BF16), wo_ref[...], preferred_element_type=F32)


def _mixer_tail(proj, att, x, cwa, woa, cwc, lng, lnb, woc, wob, wo):
    tm = TM_MIX
    row = lambda width, col: pl.BlockSpec((tm, width), lambda i: (i, col // width))
    halo = lambda rows, width, col: pl.BlockSpec(
        (rows, width), lambda i: (jnp.maximum(i * (tm // rows) - 1, 0), col // width))
    const = lambda a: pl.BlockSpec(a.shape, lambda i: (0,) * a.ndim)
    weights = (cwa, woa, cwc, lng, lnb, woc, wob, wo)
    return pl.pallas_call(
        _mixer_tail_kernel,
        grid=(TOKENS // tm,),
        in_specs=[
            row(3 * D_MODEL, COL_GATES),
            row(3 * SC_WIDTH, COL_SC),
            halo(HALO_A, 3 * SC_WIDTH, COL_SC),
            row(CONF_WIDTH, COL_GLU),
            row(CONF_WIDTH, COL_GLU + CONF_WIDTH),
            halo(HALO_C, CONF_WIDTH, COL_GLU),
            halo(HALO_C, CONF_WIDTH, COL_GLU + CONF_WIDTH),
            pl.BlockSpec((tm, N_HEADS * V_HEAD), lambda i: (i, 0)),
            pl.BlockSpec((tm, D_MODEL), lambda i: (i, 0)),
        ] + [const(a) for a in weights],
        out_specs=pl.BlockSpec((tm, D_MODEL), lambda i: (i, 0)),
        out_shape=jax.ShapeDtypeStruct((TOKENS, D_MODEL), F32),
        scratch_shapes=[
            pltpu.VMEM((tm + 8, SC_WIDTH), F32),
            pltpu.VMEM((tm + HALO_C, CONF_WIDTH), F32),
            pltpu.VMEM((7, tm + HALO_C - 8, CONF_WIDTH), F32),
            pltpu.VMEM((tm, CONF_WIDTH), BF16),
        ],
        compiler_params=_cparams(("parallel",), 56),
        name="mixer_tail",
    )(proj, proj, proj, proj, proj, proj, proj, att, x, *weights)


def _route(logits):
    lane = lax.broadcasted_iota(jnp.int32, logits.shape, 1)
    lane_f = lane.astype(F32)
    neg = -jnp.inf
    big = float(LANES)
    is_grp = (lane >= ROUTER_GROUP_LANE) & (lane < ROUTER_GROUP_LANE + N_GROUPS)
    glog = jnp.where(is_grp, logits, neg)
    gmax = jnp.max(glog, axis=-1, keepdims=True)
    gidx = jnp.min(jnp.where(glog == gmax, lane_f, big), axis=-1, keepdims=True)
    p_sel = 1.0 / jnp.sum(jnp.exp(glog - gmax), axis=-1, keepdims=True)
    first = (gidx - ROUTER_GROUP_LANE) * EXPERTS_PER_GROUP
    in_grp = (lane_f >= first) & (lane_f < first + EXPERTS_PER_GROUP)
    el = jnp.where(in_grp, logits, neg)
    m1 = jnp.max(el, axis=-1, keepdims=True)
    i1 = jnp.min(jnp.where(el == m1, lane_f, big), axis=-1, keepdims=True)
    el2 = jnp.where(lane_f == i1, neg, el)
    m2 = jnp.max(el2, axis=-1, keepdims=True)
    i2 = jnp.min(jnp.where(el2 == m2, lane_f, big), axis=-1, keepdims=True)
    e2 = jnp.exp(m2 - m1)
    w1 = p_sel / (1.0 + e2)
    w2 = w1 * e2
    return i1, i2, w1, w2


def _router_kernel(x_ref, g_ref, wrh_ref, wrl_ref, br_ref, hn_ref, meta_ref, cnt_ref, run_ref):
    tm = TM_ROUTE

    @pl.when(pl.program_id(0) == 0)
    def _():
        run_ref[...] = jnp.zeros_like(run_ref)

    h = _rms(x_ref[...], g_ref[...])
    for s in range(SLAB):
        hn_ref[pl.ds(s, tm, stride=SLAB), :] = h[:, s * LANES:(s + 1) * LANES]
    h_hi = h.astype(BF16)
    h_lo = (h - h_hi.astype(F32)).astype(BF16)
    logits = (jnp.dot(h_hi, wrh_ref[...], preferred_element_type=F32)
              + jnp.dot(h_hi, wrl_ref[...], preferred_element_type=F32)
              + jnp.dot(h_lo, wrh_ref[...], preferred_element_type=F32)
              + br_ref[...])
    i1, i2, w1, w2 = _route(logits)
    lane = lax.broadcasted_iota(jnp.int32, (tm, LANES), 1)
    lane_f = lane.astype(F32)
    oh1 = lane_f == i1
    oh2 = lane_f == i2
    onehot = jnp.where(oh1, 1.0, 0.0) + jnp.where(oh2, 1.0, 0.0)
    row_id = lax.broadcasted_iota(jnp.int32, (tm, tm), 0)
    col_id = lax.broadcasted_iota(jnp.int32, (tm, tm), 1)
    below = jnp.where(row_id > col_id, 1.0, 0.0).astype(BF16)
    before = run_ref[...] + jnp.dot(below, onehot.astype(BF16), preferred_element_type=F32)
    r1 = jnp.sum(jnp.where(oh1, before, 0.0), axis=-1, keepdims=True)
    r2 = jnp.sum(jnp.where(oh2, before, 0.0), axis=-1, keepdims=True)
    run_ref[...] += jnp.sum(onehot, axis=0, keepdims=True)
    cnt_ref[...] = run_ref[...]
    meta = jnp.zeros((tm, LANES), F32)
    for col, val in enumerate((i1, i2, r1, r2, w1, w2)):
        meta = jnp.where(lane == col, val, meta)
    meta_ref[...] = meta


def _router(x, g, wrh, wrl, br):
    tm = TM_ROUTE
    const = lambda i: (0, 0)
    return pl.pallas_call(
        _router_kernel,
        grid=(TOKENS // tm,),
        in_specs=[
            pl.BlockSpec((tm, D_MODEL), lambda i: (i, 0)),
            pl.BlockSpec((1, D_MODEL), const),
            pl.BlockSpec((D_MODEL, LANES), const),
            pl.BlockSpec((D_MODEL, LANES), const),
            pl.BlockSpec((1, LANES), const),
        ],
        out_specs=[
            pl.BlockSpec((tm * SLAB, LANES), lambda i: (i, 0)),
            pl.BlockSpec((tm, LANES), lambda i: (i, 0)),
            pl.BlockSpec((1, LANES), const),
        ],
        out_shape=(
            jax.ShapeDtypeStruct((TOKENS * SLAB, LANES), F32),
            jax.ShapeDtypeStruct((TOKENS, LANES), F32),
            jax.ShapeDtypeStruct((1, LANES), F32),
        ),
        scratch_shapes=[pltpu.VMEM((1, LANES), F32)],
        compiler_params=_cparams(("arbitrary",), 48),
        name="moe_router",
    )(x, g, wrh, wrl, br)


def _dispatch_plan(meta, counts):
    e1 = meta[:, META_E1].astype(jnp.int32)
    e2 = meta[:, META_E2].astype(jnp.int32)
    r1 = meta[:, META_R1].astype(jnp.int32)
    r2 = meta[:, META_R2].astype(jnp.int32)
    cnt = counts[0, :N_EXPERTS].astype(jnp.int32)
    tiles = (cnt + TM_EXP - 1) // TM_EXP
    tile_end = jnp.cumsum(tiles)
    first_slot = (tile_end - tiles) * TM_EXP
    pos1 = first_slot[e1] + r1
    pos2 = first_slot[e2] + r2
    n_tiles = tile_end[-1:]
    tile_id = jnp.minimum(jnp.arange(MAX_TILES, dtype=jnp.int32), n_tiles - 1)
    tile_expert = jnp.searchsorted(tile_end, tile_id, side="right").astype(jnp.int32)
    tok = jnp.arange(TOKENS, dtype=jnp.int32)
    src = jnp.zeros((MAX_TILES * TM_EXP,), jnp.int32).at[pos1].set(tok).at[pos2].set(tok)
    return pos1, pos2, tile_expert, n_tiles.astype(jnp.int32), src


def _start_slab_gathers(idx_refs, base, n_rows, src_hbm, dst_bufs, sem):
    def body(c, carry):
        for u in range(GATHER_UNROLL):
            r = c * GATHER_UNROLL + u
            for idx_ref, dst in zip(idx_refs, dst_bufs):
                t = idx_ref[base + r]
                pltpu.make_async_copy(src_hbm.at[pl.ds(pl.multiple_of(t * SLAB, SLAB), SLAB), :],
                                      dst.at[pl.ds(pl.multiple_of(r * SLAB, SLAB), SLAB), :], sem).start()
        return carry

    lax.fori_loop(0, n_rows // GATHER_UNROLL, body, 0)


def _wait_slab_gathers(n_rows, src_hbm, dst, sem):
    pltpu.make_async_copy(src_hbm.at[pl.ds(0, n_rows * SLAB), :], dst, sem).wait()


def _expert_kernel(te_ref, nt_ref, src_ref, hn_ref, wg_ref, wu_ref, wd_ref, ys_ref, xbuf, sem):
    del te_ref
    tm = TM_EXP
    i = pl.program_id(0)
    n = nt_ref[0]

    def gather(tile, slot):
        _start_slab_gathers((src_ref,), tile * tm, tm, hn_ref, (xbuf.at[slot],), sem.at[slot])

    @pl.when(i == 0)
    def _():
        gather(0, 0)

    @pl.when(i + 1 < n)
    def _():
        gather(i + 1, (i + 1) % 2)

    @pl.when(i < n)
    def _():
        slot = i % 2
        _wait_slab_gathers(tm, hn_ref, xbuf.at[slot], sem.at[slot])
        xt = jnp.concatenate([xbuf[slot, pl.ds(s, tm, stride=SLAB), :].astype(BF16) for s in range(SLAB)], axis=1)
        hg = jnp.dot(xt, wg_ref[...].astype(BF16), preferred_element_type=F32)
        hu = jnp.dot(xt, wu_ref[...].astype(BF16), preferred_element_type=F32)
        hh = (hg * _sigmoid(hg) * hu).astype(BF16)
        y = jnp.dot(hh, wd_ref[...].astype(BF16), preferred_element_type=F32)
        for s in range(SLAB):
            ys_ref[pl.ds(s, tm, stride=SLAB), :] = y[:, s * LANES:(s + 1) * LANES]

    @pl.when(i >= n)
    def _():
        ys_ref[...] = jnp.zeros_like(ys_ref)


def _experts(layer, tile_expert, n_tiles, src, hn, w_gate, w_up, w_down):
    tm = TM_EXP
    wspec = lambda rows, cols: pl.BlockSpec((None, None, rows, cols), lambda i, te, nt, sr: (layer, te[i], 0, 0))
    return pl.pallas_call(
        _expert_kernel,
        grid_spec=pltpu.PrefetchScalarGridSpec(
            num_scalar_prefetch=3,
            grid=(MAX_TILES,),
            in_specs=[
                pl.BlockSpec(memory_space=pl.ANY),
                wspec(D_MODEL, EXPERT_HIDDEN),
                wspec(D_MODEL, EXPERT_HIDDEN),
                wspec(EXPERT_HIDDEN, D_MODEL),
            ],
            out_specs=pl.BlockSpec((tm * SLAB, LANES), lambda i, te, nt, sr: (i, 0)),
            scratch_shapes=[pltpu.VMEM((2, tm * SLAB, LANES), F32), pltpu.SemaphoreType.DMA((2,))],
        ),
        out_shape=jax.ShapeDtypeStruct((MAX_TILES * tm * SLAB, LANES), F32),
        compiler_params=_cparams(("arbitrary",), 48),
        name="moe_experts",
    )(tile_expert, n_tiles, src, hn, w_gate, w_up, w_down)


def _combine_ple_kernel(pos1_ref, pos2_ref, x_ref, meta_ref, ys_ref, p_ref, g_ref, wg_ref, wp_ref, fg_ref,
                        out_ref, cbuf, sem, *, final):
    tm = TM_COMB
    i = pl.program_id(0)

    def gather(tile, slot):
        _start_slab_gathers((pos1_ref, pos2_ref), tile * tm, tm, ys_ref, (cbuf.at[slot, 0], cbuf.at[slot, 1]),
                            sem.at[slot])

    @pl.when(i == 0)
    def _():
        gather(0, 0)

    @pl.when(i + 1 < pl.num_programs(0))
    def _():
        gather(i + 1, (i + 1) % 2)

    slot = i % 2
    for k in range(2):
        _wait_slab_gathers(tm, ys_ref, cbuf.at[slot, k], sem.at[slot])
    meta = meta_ref[...]
    w1 = meta[:, META_W1:META_W1 + 1]
    w2 = meta[:, META_W2:META_W2 + 1]
    moe = jnp.concatenate([w1 * cbuf[slot, 0, pl.ds(s, tm, stride=SLAB), :]
                           + w2 * cbuf[slot, 1, pl.ds(s, tm, stride=SLAB), :] for s in range(SLAB)], axis=1)
    x = x_ref[...] + moe
    h = _rms(x, g_ref[...]).astype(BF16)
    gate = _sigmoid(jnp.dot(h, wg_ref[...], preferred_element_type=F32))
    emb = jnp.dot(p_ref[...].astype(BF16), wp_ref[...], preferred_element_type=F32)
    y = x + gate * emb
    if final:
        y = _rms(y, fg_ref[...])
    out_ref[...] = y


def _combine_ple(layer, pos1, pos2, x, meta, ys, p, g, wg, wp, fg, final):
    tm = TM_COMB
    const = lambda i, p1, p2: (0, 0)
    rows = lambda i, p1, p2: (i, 0)
    return pl.pallas_call(
        functools.partial(_combine_ple_kernel, final=final),
        grid_spec=pltpu.PrefetchScalarGridSpec(
            num_scalar_prefetch=2,
            grid=(TOKENS // tm,),
            in_specs=[
                pl.BlockSpec((tm, D_MODEL), rows),
                pl.BlockSpec((tm, LANES), rows),
                pl.BlockSpec(memory_space=pl.ANY),
                pl.BlockSpec((None, tm, PLE_DIM), lambda i, p1, p2: (layer, i, 0)),
                pl.BlockSpec((1, D_MODEL), const),
                pl.BlockSpec((D_MODEL, D_MODEL), const),
                pl.BlockSpec((PLE_DIM, D_MODEL), const),
                pl.BlockSpec((1, D_MODEL), const),
            ],
            out_specs=pl.BlockSpec((tm, D_MODEL), rows),
            scratch_shapes=[pltpu.VMEM((2, 2, tm * SLAB, LANES), F32), pltpu.SemaphoreType.DMA((2,))],
        ),
        out_shape=jax.ShapeDtypeStruct((TOKENS, D_MODEL), F32),
        compiler_params=_cparams(("arbitrary",), 48),
        name="moe_combine_ple",
    )(pos1, pos2, x, meta, ys, p, g, wg, wp, fg)


def _in_proj_weight(w_in):
    o = 3 * D_MODEL + 3 * SC_WIDTH
    q_lat = w_in[:, o:o + Q_LORA]
    o += Q_LORA
    kv_lat = w_in[:, o:o + KV_LORA]
    o += KV_LORA
    k_rope = w_in[:, o:o + QK_ROPE]
    o += QK_ROPE
    glu = w_in[:, o:]
    half = QK_ROPE // 2
    zn = jnp.zeros((D_MODEL, QK_NOPE), F32)
    zp = jnp.zeros((D_MODEL, HEAD_PAD - QK_NOPE - QK_ROPE), F32)
    kr = jnp.concatenate([zn, k_rope, zp], axis=1)
    krsw = jnp.concatenate([zn, k_rope[:, half:], k_rope[:, :half], zp], axis=1)
    w = jnp.concatenate([w_in[:, :3 * D_MODEL + 3 * SC_WIDTH], q_lat, kv_lat, glu, kr, krsw], axis=1)
    return w.astype(BF16)


def _q_weight(w_uq):
    half = QK_ROPE // 2
    scale = (QK_NOPE + QK_ROPE) ** -0.5 * LOG2_E
    w = (w_uq * scale).reshape(Q_LORA, N_HEADS, QK_NOPE + QK_ROPE)
    nope, rope = w[:, :, :QK_NOPE], w[:, :, QK_NOPE:]
    rope_sw = jnp.concatenate([rope[:, :, half:], rope[:, :, :half]], axis=2)
    zp = jnp.zeros((Q_LORA, N_HEADS, HEAD_PAD - QK_NOPE - QK_ROPE), F32)
    plain = jnp.concatenate([nope, rope, zp], axis=2).reshape(Q_LORA, N_HEADS * HEAD_PAD)
    swapped = jnp.concatenate([jnp.zeros_like(nope), rope_sw, zp], axis=2).reshape(Q_LORA, N_HEADS * HEAD_PAD)
    return jnp.concatenate([plain, swapped], axis=1).astype(BF16)


def _kv_weight(w_ukv):
    w = w_ukv.reshape(KV_LORA, N_HEADS, QK_NOPE + V_HEAD)
    k_nope, v = w[:, :, :QK_NOPE], w[:, :, QK_NOPE:]
    z = jnp.zeros_like(v)
    k_part = jnp.concatenate([k_nope, jnp.zeros_like(k_nope)], axis=2).reshape(KV_LORA, N_HEADS * HEAD_PAD)
    odd = (jnp.arange(N_HEADS) % 2 == 1)[None, :, None]
    v_part = jnp.concatenate([jnp.where(odd, z, v), jnp.where(odd, v, z)], axis=2).reshape(KV_LORA, N_HEADS * HEAD_PAD)
    return jnp.concatenate([k_part, v_part], axis=1).astype(BF16)


def _router_weight(w_rg, b_rg, w_re, b_re):
    pad = LANES - N_EXPERTS - N_GROUPS
    w = jnp.concatenate([w_re, w_rg, jnp.zeros((D_MODEL, pad), F32)], axis=1)
    b = jnp.concatenate([b_re, b_rg, jnp.zeros((pad,), F32)]).reshape(1, LANES)
    w_hi = w.astype(BF16)
    w_lo = (w - w_hi.astype(F32)).astype(BF16)
    return w_hi, w_lo, b


def kernel(x, p, positions, ln_mix_g, w_in, conv_a_w, w_out_a, q_norm_g, w_uq, kv_norm_g, w_ukv, w_out_b, conv_c_w, ln_c_g, ln_c_b, w_out_c, w_o, ln_ffn_g, w_route_grp, b_route_grp, w_route_exp, b_route_exp, w_exp_gate, w_exp_up, w_exp_down, ln_ple_g, w_ple_gate, w_ple, final_norm_g):
    c_tab, s_tab = _rope_tables(positions)
    xf = x.reshape(TOKENS, D_MODEL)
    pf = p.reshape(DEPTH, TOKENS, PLE_DIM)
    row = lambda a: a.reshape(1, -1)
    for i in range(DEPTH):
        proj = _inproj(xf, row(ln_mix_g[i]), _in_proj_weight(w_in[i]))
        q, k, v = _qkv(proj, c_tab, s_tab, row(q_norm_g[i]), row(kv_norm_g[i]), _q_weight(w_uq[i]), _kv_weight(w_ukv[i]))
        att = _attention(q, k, v)
        xf = _mixer_tail(proj, att, xf, conv_a_w[i], w_out_a[i].astype(BF16), conv_c_w[i], row(ln_c_g[i]),
                         row(ln_c_b[i]), w_out_c[i].astype(BF16), w_out_b[i].astype(BF16), w_o[i].astype(BF16))
        wr_hi, wr_lo, br = _router_weight(w_route_grp[i], b_route_grp[i], w_route_exp[i], b_route_exp[i])
        hn, meta, counts = _router(xf, row(ln_ffn_g[i]), wr_hi, wr_lo, br)
        pos1, pos2, tile_expert, n_tiles, src = _dispatch_plan(meta, counts)
        ys = _experts(i, tile_expert, n_tiles, src, hn, w_exp_gate, w_exp_up, w_exp_down)
        xf = _combine_ple(i, pos1, pos2, xf, meta, ys, pf, row(ln_ple_g[i]), w_ple_gate[i].astype(BF16),
                          w_ple[i].astype(BF16), row(final_norm_g), final=(i == DEPTH - 1))
    return xf.reshape(BATCH, SEQ, D_MODEL)
```

```python
import functools

import jax
import jax.numpy as jnp
from jax import lax
from jax.experimental import pallas as pl
from jax.experimental.pallas import tpu as pltpu

D_MODEL = 1024
BATCH = 8
SEQ = 2048
DEPTH = 2
TOKENS = BATCH * SEQ
PLE_DIM = 256
SC_WIDTH = 512
SC_KERNEL = 3
N_HEADS = 8
QK_NOPE = 64
QK_ROPE = 32
V_HEAD = 64
Q_LORA = 768
KV_LORA = 256
ROPE_THETA = 10000.0
CONF_WIDTH = 512
CONF_KERNEL = 31
N_GROUPS = 4
EXPERTS_PER_GROUP = 8
N_EXPERTS = N_GROUPS * EXPERTS_PER_GROUP
EXPERT_HIDDEN = 256
EPS = 1e-6
LOG2_E = 1.4426950408889634

LANES = 128
HEAD_PAD = 128
F32 = jnp.float32
BF16 = jnp.bfloat16

COL_GATES = 0
COL_SC = 3 * D_MODEL
COL_QLAT = COL_SC + 3 * SC_WIDTH
COL_KVLAT = COL_QLAT + Q_LORA
COL_GLU = COL_KVLAT + KV_LORA
COL_KR = COL_GLU + 2 * CONF_WIDTH
COL_KRSW = COL_KR + HEAD_PAD
PROJ_COLS = COL_KRSW + HEAD_PAD

ROUTER_GROUP_LANE = N_EXPERTS

TM_INPROJ = 1024
TN_INPROJ = 2304
TM_QKV = 512
T_ATTN = 512
TM_MIX = 512
CONV_CHUNK = 64
HALO_C = 32
HALO_A = 16
TM_ROUTE = 512
TM_EXP = 256
TM_DISP = 256
TM_COMB = 256
TOP_K = 2
MAX_TILES = TOKENS * TOP_K // TM_EXP + N_EXPERTS
SLAB = D_MODEL // LANES
GATHER_UNROLL = 16
N_DMA_PRIORITIES = 2
META_E1, META_E2, META_R1, META_R2, META_W1, META_W2 = range(6)
META_ROWS = 8


def _cparams(semantics, vmem_mb):
    return pltpu.CompilerParams(dimension_semantics=semantics, vmem_limit_bytes=vmem_mb * 1024 * 1024)


def _sigmoid(x):
    return 1.0 / (1.0 + jnp.exp(-x))


def _rms(x, g):
    return x * lax.rsqrt(jnp.mean(x * x, axis=-1, keepdims=True) + EPS) * g


def _rope_kernel(pos_ref, freq_ref, cos_ref, sin_ref):
    ang = pos_ref[...].astype(F32) * freq_ref[...]
    cos_ref[...] = jnp.cos(ang)
    sin_ref[...] = jnp.sin(ang)


def _rope_tables(positions):
    half = QK_ROPE // 2
    inv_freq = ROPE_THETA ** (-jnp.arange(0, QK_ROPE, 2, dtype=F32) / QK_ROPE)
    rows = TOKENS * half // LANES
    pos_rep = jnp.broadcast_to(positions.reshape(TOKENS, 1), (TOKENS, half)).reshape(rows, LANES)
    freq = jnp.tile(inv_freq, LANES // half).reshape(1, LANES)
    cos, sin = pl.pallas_call(
        _rope_kernel,
        out_shape=(jax.ShapeDtypeStruct((rows, LANES), F32),) * 2,
        name="rope_tables",
    )(pos_rep, freq)
    cos = cos.reshape(TOKENS, half)
    sin = sin.reshape(TOKENS, half)
    ones = jnp.ones((TOKENS, QK_NOPE), F32)
    zn = jnp.zeros((TOKENS, QK_NOPE), F32)
    zp = jnp.zeros((TOKENS, HEAD_PAD - QK_NOPE - QK_ROPE), F32)
    c_tab = jnp.concatenate([ones, cos, cos, zp], axis=1)
    s_tab = jnp.concatenate([zn, -sin, sin, zp], axis=1)
    return c_tab, s_tab


def _inproj_kernel(x_ref, g_ref, w_ref, o_ref, h_ref):
    @pl.when(pl.program_id(1) == 0)
    def _():
        h_ref[...] = _rms(x_ref[...], g_ref[...]).astype(BF16)

    o_ref[...] = jnp.dot(h_ref[...], w_ref[...], preferred_element_type=F32).astype(BF16)


def _inproj(x, g, w):
    tm, tn = TM_INPROJ, TN_INPROJ
    return pl.pallas_call(
        _inproj_kernel,
        grid=(TOKENS // tm, PROJ_COLS // tn),
        in_specs=[
            pl.BlockSpec((tm, D_MODEL), lambda i, j: (i, 0)),
            pl.BlockSpec((1, D_MODEL), lambda i, j: (0, 0)),
            pl.BlockSpec((D_MODEL, tn), lambda i, j: (0, j)),
        ],
        out_specs=pl.BlockSpec((tm, tn), lambda i, j: (i, j)),
        out_shape=jax.ShapeDtypeStruct((TOKENS, PROJ_COLS), BF16),
        scratch_shapes=[pltpu.VMEM((tm, D_MODEL), BF16)],
        compiler_params=_cparams(("parallel", "arbitrary"), 48),
        name="in_proj",
    )(x, g, w)


def _qkv_kernel(ql_ref, kvl_ref, kr_ref, krsw_ref, c_ref, s_ref, qg_ref, kvg_ref, wq_ref, wkv_ref,
                q_out, k_out, v_out):
    c = c_ref[...]
    s = s_ref[...]
    width = N_HEADS * HEAD_PAD
    qn = _rms(ql_ref[...].astype(F32), qg_ref[...]).astype(BF16)
    qq = jnp.dot(qn, wq_ref[...], preferred_element_type=F32)
    for h in range(N_HEADS):
        lo, hi = h * HEAD_PAD, (h + 1) * HEAD_PAD
        q_out[:, lo:hi] = (qq[:, lo:hi] * c + qq[:, width + lo:width + hi] * s).astype(BF16)
    kvn = _rms(kvl_ref[...].astype(F32), kvg_ref[...]).astype(BF16)
    kk = jnp.dot(kvn, wkv_ref[...], preferred_element_type=F32)
    kr = kr_ref[...].astype(F32) * c + krsw_ref[...].astype(F32) * s
    for h in range(N_HEADS):
        lo, hi = h * HEAD_PAD, (h + 1) * HEAD_PAD
        k_out[:, lo:hi] = (kk[:, lo:hi] + kr).astype(BF16)
    v_out[...] = kk[:, width:].astype(BF16)


def _qkv(proj, c_tab, s_tab, qg, kvg, wq, wkv):
    tm = TM_QKV
    width = N_HEADS * HEAD_PAD
    row = lambda blk: (lambda i: (i, blk))
    const = lambda i: (0, 0)
    return pl.pallas_call(
        _qkv_kernel,
        grid=(TOKENS // tm,),
        in_specs=[
            pl.BlockSpec((tm, Q_LORA), row(COL_QLAT // Q_LORA)),
            pl.BlockSpec((tm, KV_LORA), row(COL_KVLAT // KV_LORA)),
            pl.BlockSpec((tm, HEAD_PAD), row(COL_KR // HEAD_PAD)),
            pl.BlockSpec((tm, HEAD_PAD), row(COL_KRSW // HEAD_PAD)),
            pl.BlockSpec((tm, HEAD_PAD), row(0)),
            pl.BlockSpec((tm, HEAD_PAD), row(0)),
            pl.BlockSpec((1, Q_LORA), const),
            pl.BlockSpec((1, KV_LORA), const),
            pl.BlockSpec((Q_LORA, 2 * width), const),
            pl.BlockSpec((KV_LORA, 2 * width), const),
        ],
        out_specs=[pl.BlockSpec((tm, width), row(0))] * 3,
        out_shape=(jax.ShapeDtypeStruct((TOKENS, width), BF16),) * 3,
        compiler_params=_cparams(("parallel",), 48),
        name="qkv_prep",
    )(proj, proj, proj, proj, c_tab, s_tab, qg, kvg, wq, wkv)


def _attn_kernel(q_ref, k_ref, v_ref, o_ref, *state):
    t = T_ATTN
    qi = pl.program_id(2)
    nt = (((1,), (1,)), ((), ()))
    n_blk = t // LANES
    m_sc, l_sc, acc_sc = state[0:2], state[2:4], state[4:6]
    for h in range(2):
        m_sc[h][...] = jnp.full((t, LANES), -jnp.inf, F32)
        l_sc[h][...] = jnp.zeros((t, LANES), F32)
        acc_sc[h][...] = jnp.zeros((t, LANES), F32)

    def tile(j, masked):
        start = pl.multiple_of(j * t, t)
        for h in range(2):
            lo, hi = h * HEAD_PAD, (h + 1) * HEAD_PAD
            s = lax.dot_general(q_ref[:, lo:hi], k_ref[pl.ds(start, t), lo:hi], nt, preferred_element_type=F32)
            if masked:
                row_id = lax.broadcasted_iota(jnp.int32, (t, t), 0)
                col_id = lax.broadcasted_iota(jnp.int32, (t, t), 1)
                s = jnp.where(row_id >= col_id, s, -jnp.inf)
            blocks = [s[:, c * LANES:(c + 1) * LANES] for c in range(n_blk)]
            bmax = functools.reduce(jnp.maximum, blocks)
            m_old = m_sc[h][...]
            m_new = jnp.maximum(m_old, jnp.max(bmax, axis=-1, keepdims=True))
            alpha = jnp.exp2(m_old - m_new)
            ps = [jnp.exp2(b - m_new) for b in blocks]
            p = jnp.concatenate(ps, axis=1).astype(BF16)
            l_sc[h][...] = alpha * l_sc[h][...] + functools.reduce(jnp.add, ps)
            acc_sc[h][...] = alpha * acc_sc[h][...] + jnp.dot(p, v_ref[pl.ds(start, t), lo:hi],
                                                               preferred_element_type=F32)
            m_sc[h][...] = m_new

    def body(j, carry):
        tile(j, False)
        return carry

    lax.fori_loop(0, qi, body, 0)
    tile(qi, True)
    out = [acc_sc[h][...] / jnp.sum(l_sc[h][...], axis=-1, keepdims=True) for h in range(2)]
    o_ref[...] = (out[0] + out[1]).astype(BF16)


def _attention(q, k, v):
    t = T_ATTN
    nq = SEQ // t
    pair = 2 * HEAD_PAD
    return pl.pallas_call(
        _attn_kernel,
        grid=(BATCH, N_HEADS // 2, nq),
        in_specs=[
            pl.BlockSpec((t, pair), lambda b, hp, i: (b * nq + i, hp)),
            pl.BlockSpec((SEQ, pair), lambda b, hp, i: (b, hp)),
            pl.BlockSpec((SEQ, pair), lambda b, hp, i: (b, hp)),
        ],
        out_specs=pl.BlockSpec((t, HEAD_PAD), lambda b, hp, i: (b * nq + i, hp)),
        out_shape=jax.ShapeDtypeStruct((TOKENS, N_HEADS * V_HEAD), BF16),
        scratch_shapes=[pltpu.VMEM((t, LANES), F32)] * 6,
        compiler_params=_cparams(("parallel", "parallel", "arbitrary"), 48),
        name="mla_attention",
    )(q, k, v)


def _mixer_tail_kernel(gates_ref, sc_ref, sch_ref, gv_ref, gg_ref, gvh_ref, ggh_ref, att_ref, x_ref,
                       cwa_ref, woa_ref, cwc_ref, lng_ref, lnb_ref, woc_ref, wob_ref, wo_ref,
                       out_ref, cbuf, ubuf, shifted, vbuf):
    tm = TM_MIX
    has_past = (pl.program_id(0) % (SEQ // tm)) != 0

    sc = sc_ref[...]
    sc_b = sc[:, 0:SC_WIDTH].astype(F32)
    cbuf[8:8 + tm, :] = sc[:, SC_WIDTH:2 * SC_WIDTH].astype(F32) * sc[:, 2 * SC_WIDTH:].astype(F32)
    sch = sch_ref[...].astype(F32)[HALO_A - 8:HALO_A, :]
    cbuf[0:8, :] = jnp.where(has_past, sch[:, SC_WIDTH:2 * SC_WIDTH] * sch[:, 2 * SC_WIDTH:], 0.0)
    conv_a = cwa_ref[0:1, :] * cbuf[6:6 + tm, :]
    for t in range(1, SC_KERNEL):
        conv_a = conv_a + cwa_ref[t:t + 1, :] * cbuf[6 + t:6 + t + tm, :]
    y_a = jnp.dot((sc_b * conv_a).astype(BF16), woa_ref[...], preferred_element_type=F32)

    ubuf[HALO_C:HALO_C + tm, :] = gv_ref[...].astype(F32) * _sigmoid(gg_ref[...].astype(F32))
    ubuf[0:HALO_C, :] = jnp.where(has_past, gvh_ref[...].astype(F32) * _sigmoid(ggh_ref[...].astype(F32)), 0.0)
    rows = tm + HALO_C - 8
    for b in range(1, 8):
        shifted[b - 1, 0:rows, :] = ubuf[b:b + rows, :]
    base = HALO_C - (CONF_KERNEL - 1)
    for r0 in range(0, tm, CONV_CHUNK):
        acc = None
        for t in range(CONF_KERNEL):
            off = base + t
            a0 = r0 + off - off % 8
            src = ubuf[a0:a0 + CONV_CHUNK, :] if off % 8 == 0 else shifted[off % 8 - 1, a0:a0 + CONV_CHUNK, :]
            term = cwc_ref[t:t + 1, :] * src
            acc = term if acc is None else acc + term
        mu = jnp.mean(acc, axis=-1, keepdims=True)
        xc = acc - mu
        var = jnp.mean(xc * xc, axis=-1, keepdims=True)
        y = xc * lax.rsqrt(var + EPS) * lng_ref[...] + lnb_ref[...]
        vbuf[r0:r0 + CONV_CHUNK, :] = (y * _sigmoid(y)).astype(BF16)
    y_c = jnp.dot(vbuf[...], woc_ref[...], preferred_element_type=F32)

    y_b = jnp.dot(att_ref[...], wob_ref[...], preferred_element_type=F32)

    g = gates_ref[...]
    merged = (_sigmoid(g[:, 0:D_MODEL].astype(F32)) * y_a
              + _sigmoid(g[:, D_MODEL:2 * D_MODEL].astype(F32)) * y_b
              + _sigmoid(g[:, 2 * D_MODEL:].astype(F32)) * y_c)
    out_ref[...] = x_ref[...] + jnp.dot(merged.astype(BF16), wo_ref[...], preferred_element_type=F32)


def _mixer_tail(proj, att, x, cwa, woa, cwc, lng, lnb, woc, wob, wo):
    tm = TM_MIX
    row = lambda width, col: pl.BlockSpec((tm, width), lambda i: (i, col // width))
    halo = lambda rows, width, col: pl.BlockSpec(
        (rows, width), lambda i: (jnp.maximum(i * (tm // rows) - 1, 0), col // width))
    const = lambda a: pl.BlockSpec(a.shape, lambda i: (0,) * a.ndim)
    weights = (cwa, woa, cwc, lng, lnb, woc, wob, wo)
    return pl.pallas_call(
        _mixer_tail_kernel,
        grid=(TOKENS // tm,),
        in_specs=[
            row(3 * D_MODEL, COL_GATES),
            row(3 * SC_WIDTH, COL_SC),
            halo(HALO_A, 3 * SC_WIDTH, COL_SC),
            row(CONF_WIDTH, COL_GLU),
            row(CONF_WIDTH, COL_GLU + CONF_WIDTH),
            halo(HALO_C, CONF_WIDTH, COL_GLU),
            halo(HALO_C, CONF_WIDTH, COL_GLU + CONF_WIDTH),
            pl.BlockSpec((tm, N_HEADS * V_HEAD), lambda i: (i, 0)),
            pl.BlockSpec((tm, D_MODEL), lambda i: (i, 0)),
        ] + [const(a) for a in weights],
        out_specs=pl.BlockSpec((tm, D_MODEL), lambda i: (i, 0)),
        out_shape=jax.ShapeDtypeStruct((TOKENS, D_MODEL), F32),
        scratch_shapes=[
            pltpu.VMEM((tm + 8, SC_WIDTH), F32),
            pltpu.VMEM((tm + HALO_C, CONF_WIDTH), F32),
            pltpu.VMEM((7, tm + HALO_C - 8, CONF_WIDTH), F32),
            pltpu.VMEM((tm, CONF_WIDTH), BF16),
        ],
        compiler_params=_cparams(("parallel",), 56),
        name="mixer_tail",
    )(proj, proj, proj, proj, proj, proj, proj, att, x, *weights)


def _route(logits):
    lane = lax.broadcasted_iota(jnp.int32, logits.shape, 1)
    lane_f = lane.astype(F32)
    neg = -jnp.inf
    big = float(LANES)
    is_grp = (lane >= ROUTER_GROUP_LANE) & (lane < ROUTER_GROUP_LANE + N_GROUPS)
    glog = jnp.where(is_grp, logits, neg)
    gmax = jnp.max(glog, axis=-1, keepdims=True)
    gidx = jnp.min(jnp.where(glog == gmax, lane_f, big), axis=-1, keepdims=True)
    p_sel = 1.0 / jnp.sum(jnp.exp(glog - gmax), axis=-1, keepdims=True)
    first = (gidx - ROUTER_GROUP_LANE) * EXPERTS_PER_GROUP
    in_grp = (lane_f >= first) & (lane_f < first + EXPERTS_PER_GROUP)
    el = jnp.where(in_grp, logits, neg)
    m1 = jnp.max(el, axis=-1, keepdims=True)
    i1 = jnp.min(jnp.where(el == m1, lane_f, big), axis=-1, keepdims=True)
    el2 = jnp.where(lane_f == i1, neg, el)
    m2 = jnp.max(el2, axis=-1, keepdims=True)
    i2 = jnp.min(jnp.where(el2 == m2, lane_f, big), axis=-1, keepdims=True)
    e2 = jnp.exp(m2 - m1)
    w1 = p_sel / (1.0 + e2)
    w2 = w1 * e2
    return i1, i2, w1, w2


def _router_kernel(x_ref, g_ref, wrh_ref, wrl_ref, br_ref, meta_ref, meta_t_ref, cnt_ref, run_ref):
    tm = TM_ROUTE

    @pl.when(pl.program_id(0) == 0)
    def _():
        run_ref[...] = jnp.zeros_like(run_ref)

    h = _rms(x_ref[...], g_ref[...])
    h_hi = h.astype(BF16)
    h_lo = (h - h_hi.astype(F32)).astype(BF16)
    logits = (jnp.dot(h_hi, wrh_ref[...], preferred_element_type=F32)
              + jnp.dot(h_hi, wrl_ref[...], preferred_element_type=F32)
              + jnp.dot(h_lo, wrh_ref[...], preferred_element_type=F32)
              + br_ref[...])
    i1, i2, w1, w2 = _route(logits)
    lane = lax.broadcasted_iota(jnp.int32, (tm, LANES), 1)
    lane_f = lane.astype(F32)
    oh1 = lane_f == i1
    oh2 = lane_f == i2
    onehot = jnp.where(oh1, 1.0, 0.0) + jnp.where(oh2, 1.0, 0.0)
    row_id = lax.broadcasted_iota(jnp.int32, (tm, tm), 0)
    col_id = lax.broadcasted_iota(jnp.int32, (tm, tm), 1)
    below = jnp.where(row_id > col_id, 1.0, 0.0).astype(BF16)
    before = run_ref[...] + jnp.dot(below, onehot.astype(BF16), preferred_element_type=F32)
    r1 = jnp.sum(jnp.where(oh1, before, 0.0), axis=-1, keepdims=True)
    r2 = jnp.sum(jnp.where(oh2, before, 0.0), axis=-1, keepdims=True)
    run_ref[...] += jnp.sum(onehot, axis=0, keepdims=True)
    cnt_ref[...] = run_ref[...]
    meta = jnp.zeros((tm, LANES), F32)
    for col, val in enumerate((i1, i2, r1, r2, w1, w2)):
        meta = jnp.where(lane == col, val, meta)
    meta_ref[...] = meta
    meta_t_ref[...] = meta.T[0:META_ROWS, :]


def _router(x, g, wrh, wrl, br):
    tm = TM_ROUTE
    const = lambda i: (0, 0)
    return pl.pallas_call(
        _router_kernel,
        grid=(TOKENS // tm,),
        in_specs=[
            pl.BlockSpec((tm, D_MODEL), lambda i: (i, 0)),
            pl.BlockSpec((1, D_MODEL), const),
            pl.BlockSpec((D_MODEL, LANES), const),
            pl.BlockSpec((D_MODEL, LANES), const),
            pl.BlockSpec((1, LANES), const),
        ],
        out_specs=[
            pl.BlockSpec((tm, LANES), lambda i: (i, 0)),
            pl.BlockSpec((META_ROWS, tm), lambda i: (0, i)),
            pl.BlockSpec((1, LANES), const),
        ],
        out_shape=(
            jax.ShapeDtypeStruct((TOKENS, LANES), F32),
            jax.ShapeDtypeStruct((META_ROWS, TOKENS), F32),
            jax.ShapeDtypeStruct((1, LANES), F32),
        ),
        scratch_shapes=[pltpu.VMEM((1, LANES), F32)],
        compiler_params=_cparams(("arbitrary",), 48),
        name="moe_router",
    )(x, g, wrh, wrl, br)


def _dispatch_plan(meta_t, counts):
    e1 = meta_t[META_E1].astype(jnp.int32)
    e2 = meta_t[META_E2].astype(jnp.int32)
    r1 = meta_t[META_R1].astype(jnp.int32)
    r2 = meta_t[META_R2].astype(jnp.int32)
    cnt = counts[0, :N_EXPERTS].astype(jnp.int32)
    tiles = (cnt + TM_EXP - 1) // TM_EXP
    tile_end = jnp.cumsum(tiles)
    first_slot = ((tile_end - tiles) * TM_EXP)[:, None]
    expert = jnp.arange(N_EXPERTS, dtype=jnp.int32)[:, None]
    pos1 = jnp.sum(jnp.where(e1[None, :] == expert, first_slot, 0), axis=0) + r1
    pos2 = jnp.sum(jnp.where(e2[None, :] == expert, first_slot, 0), axis=0) + r2
    n_tiles = tile_end[-1:]
    tile_id = jnp.minimum(jnp.arange(MAX_TILES, dtype=jnp.int32), n_tiles - 1)
    tile_expert = jnp.sum((tile_id[:, None] >= tile_end[None, :]).astype(jnp.int32), axis=1)
    return pos1, pos2, tile_expert, n_tiles


def _dispatch_kernel(pos1_ref, pos2_ref, x_ref, g_ref, xs_in_ref, xs_ref, slab, sem):
    del xs_in_ref
    tm = TM_DISP
    i = pl.program_id(0)
    last = pl.num_programs(0) - 1
    slot = i % 2

    def wait_copies(sl):
        for _ in range(TOP_K):
            pltpu.make_async_copy(slab.at[sl], xs_ref.at[pl.ds(0, tm * SLAB), :], sem.at[sl]).wait()

    @pl.when(i >= 2)
    def _():
        wait_copies(slot)

    h = _rms(x_ref[...], g_ref[...])
    for s in range(SLAB):
        slab[slot, pl.ds(s, tm, stride=SLAB), :] = h[:, s * LANES:(s + 1) * LANES]

    def body(c, carry):
        for u in range(GATHER_UNROLL):
            r = c * GATHER_UNROLL + u
            src = slab.at[slot, pl.ds(pl.multiple_of(r * SLAB, SLAB), SLAB), :]
            for k, pos_ref in enumerate((pos1_ref, pos2_ref)):
                p = pos_ref[i * tm + r]
                pltpu.make_async_copy(src, xs_ref.at[pl.ds(pl.multiple_of(p * SLAB, SLAB), SLAB), :],
                                      sem.at[slot]).start(priority=k % N_DMA_PRIORITIES)
        return carry

    lax.fori_loop(0, tm // GATHER_UNROLL, body, 0)

    @pl.when(i == last)
    def _():
        wait_copies(1 - slot)
        wait_copies(slot)


def _dispatch(pos1, pos2, x, g):
    tm = TM_DISP
    xs0 = jnp.zeros((MAX_TILES * TM_EXP * SLAB, LANES), F32)
    return pl.pallas_call(
        _dispatch_kernel,
        grid_spec=pltpu.PrefetchScalarGridSpec(
            num_scalar_prefetch=2,
            grid=(TOKENS // tm,),
            in_specs=[
                pl.BlockSpec((tm, D_MODEL), lambda i, p1, p2: (i, 0)),
                pl.BlockSpec((1, D_MODEL), lambda i, p1, p2: (0, 0)),
                pl.BlockSpec(memory_space=pl.ANY),
            ],
            out_specs=pl.BlockSpec(memory_space=pl.ANY),
            scratch_shapes=[pltpu.VMEM((2, tm * SLAB, LANES), F32), pltpu.SemaphoreType.DMA((2,))],
        ),
        out_shape=jax.ShapeDtypeStruct(xs0.shape, F32),
        input_output_aliases={4: 0},
        compiler_params=_cparams(("arbitrary",), 48),
        name="moe_dispatch",
    )(pos1, pos2, x, g, xs0)


def _expert_kernel(te_ref, nt_ref, xs_ref, wg_ref, wu_ref, wd_ref, ys_ref):
    del te_ref
    tm = TM_EXP
    i = pl.program_id(0)
    n = nt_ref[0]

    @pl.when(i < n)
    def _():
        xt = jnp.concatenate([xs_ref[pl.ds(s, tm, stride=SLAB), :].astype(BF16) for s in range(SLAB)], axis=1)
        hg = jnp.dot(xt, wg_ref[...].astype(BF16), preferred_element_type=F32)
        hu = jnp.dot(xt, wu_ref[...].astype(BF16), preferred_element_type=F32)
        hh = (hg * _sigmoid(hg) * hu).astype(BF16)
        y = jnp.dot(hh, wd_ref[...].astype(BF16), preferred_element_type=F32)
        for s in range(SLAB):
            ys_ref[pl.ds(s, tm, stride=SLAB), :] = y[:, s * LANES:(s + 1) * LANES]

    @pl.when(i >= n)
    def _():
        ys_ref[...] = jnp.zeros_like(ys_ref)


def _experts(layer, tile_expert, n_tiles, xs, w_gate, w_up, w_down):
    tm = TM_EXP
    wspec = lambda rows, cols: pl.BlockSpec((None, None, rows, cols), lambda i, te, nt: (layer, te[i], 0, 0))
    return pl.pallas_call(
        _expert_kernel,
        grid_spec=pltpu.PrefetchScalarGridSpec(
            num_scalar_prefetch=2,
            grid=(MAX_TILES,),
            in_specs=[
                pl.BlockSpec((tm * SLAB, LANES), lambda i, te, nt: (jnp.minimum(i, nt[0] - 1), 0)),
                wspec(D_MODEL, EXPERT_HIDDEN),
                wspec(D_MODEL, EXPERT_HIDDEN),
                wspec(EXPERT_HIDDEN, D_MODEL),
            ],
            out_specs=pl.BlockSpec((tm * SLAB, LANES), lambda i, te, nt: (i, 0)),
        ),
        out_shape=jax.ShapeDtypeStruct((MAX_TILES * tm * SLAB, LANES), F32),
        compiler_params=_cparams(("arbitrary",), 48),
        name="moe_experts",
    )(tile_expert, n_tiles, xs, w_gate, w_up, w_down)


def _start_slab_gathers(idx_refs, base, n_rows, src_hbm, dst_bufs, sem):
    def body(c, carry):
        for u in range(GATHER_UNROLL):
            r = c * GATHER_UNROLL + u
            for k, (idx_ref, dst) in enumerate(zip(idx_refs, dst_bufs)):
                t = idx_ref[base + r]
                pltpu.make_async_copy(src_hbm.at[pl.ds(pl.multiple_of(t * SLAB, SLAB), SLAB), :],
                                      dst.at[pl.ds(pl.multiple_of(r * SLAB, SLAB), SLAB), :],
                                      sem).start(priority=k % N_DMA_PRIORITIES)
        return carry

    lax.fori_loop(0, n_rows // GATHER_UNROLL, body, 0)


def _wait_slab_gathers(n_rows, src_hbm, dst, sem):
    pltpu.make_async_copy(src_hbm.at[pl.ds(0, n_rows * SLAB), :], dst, sem).wait()


def _combine_ple_kernel(pos1_ref, pos2_ref, x_ref, meta_ref, ys_ref, p_ref, g_ref, wg_ref, wp_ref, fg_ref,
                        out_ref, cbuf, sem, *, final):
    tm = TM_COMB
    i = pl.program_id(0)

    def gather(tile, slot):
        _start_slab_gathers((pos1_ref, pos2_ref), tile * tm, tm, ys_ref, (cbuf.at[slot, 0], cbuf.at[slot, 1]),
                            sem.at[slot])

    @pl.when(i == 0)
    def _():
        gather(0, 0)

    @pl.when(i + 1 < pl.num_programs(0))
    def _():
        gather(i + 1, (i + 1) % 2)

    slot = i % 2
    for k in range(2):
        _wait_slab_gathers(tm, ys_ref, cbuf.at[slot, k], sem.at[slot])
    meta = meta_ref[...]
    w1 = meta[:, META_W1:META_W1 + 1]
    w2 = meta[:, META_W2:META_W2 + 1]
    moe = jnp.concatenate([w1 * cbuf[slot, 0, pl.ds(s, tm, stride=SLAB), :]
                           + w2 * cbuf[slot, 1, pl.ds(s, tm, stride=SLAB), :] for s in range(SLAB)], axis=1)
    x = x_ref[...] + moe
    h = _rms(x, g_ref[...]).astype(BF16)
    gate = _sigmoid(jnp.dot(h, wg_ref[...], preferred_element_type=F32))
    emb = jnp.dot(p_ref[...].astype(BF16), wp_ref[...], preferred_element_type=F32)
    y = x + gate * emb
    if final:
        y = _rms(y, fg_ref[...])
    out_ref[...] = y


def _combine_ple(layer, pos1, pos2, x, meta, ys, p, g, wg, wp, fg, final):
    tm = TM_COMB
    const = lambda i, p1, p2: (0, 0)
    rows = lambda i, p1, p2: (i, 0)
    return pl.pallas_call(
        functools.partial(_combine_ple_kernel, final=final),
        grid_spec=pltpu.PrefetchScalarGridSpec(
            num_scalar_prefetch=2,
            grid=(TOKENS // tm,),
            in_specs=[
                pl.BlockSpec((tm, D_MODEL), rows),
                pl.BlockSpec((tm, LANES), rows),
                pl.BlockSpec(memory_space=pl.ANY),
                pl.BlockSpec((None, tm, PLE_DIM), lambda i, p1, p2: (layer, i, 0)),
                pl.BlockSpec((1, D_MODEL), const),
                pl.BlockSpec((D_MODEL, D_MODEL), const),
                pl.BlockSpec((PLE_DIM, D_MODEL), const),
                pl.BlockSpec((1, D_MODEL), const),
            ],
            out_specs=pl.BlockSpec((tm, D_MODEL), rows),
            scratch_shapes=[pltpu.VMEM((2, 2, tm * SLAB, LANES), F32), pltpu.SemaphoreType.DMA((2,))],
        ),
        out_shape=jax.ShapeDtypeStruct((TOKENS, D_MODEL), F32),
        compiler_params=_cparams(("arbitrary",), 48),
        name="moe_combine_ple",
    )(pos1, pos2, x, meta, ys, p, g, wg, wp, fg)


def _in_proj_weight(w_in):
    o = 3 * D_MODEL + 3 * SC_WIDTH
    q_lat = w_in[:, o:o + Q_LORA]
    o += Q_LORA
    kv_lat = w_in[:, o:o + KV_LORA]
    o += KV_LORA
    k_rope = w_in[:, o:o + QK_ROPE]
    o += QK_ROPE
    glu = w_in[:, o:]
    half = QK_ROPE // 2
    zn = jnp.zeros((D_MODEL, QK_NOPE), F32)
    zp = jnp.zeros((D_MODEL, HEAD_PAD - QK_NOPE - QK_ROPE), F32)
    kr = jnp.concatenate([zn, k_rope, zp], axis=1)
    krsw = jnp.concatenate([zn, k_rope[:, half:], k_rope[:, :half], zp], axis=1)
    w = jnp.concatenate([w_in[:, :3 * D_MODEL + 3 * SC_WIDTH], q_lat, kv_lat, glu, kr, krsw], axis=1)
    return w.astype(BF16)


def _q_weight(w_uq):
    half = QK_ROPE // 2
    scale = (QK_NOPE + QK_ROPE) ** -0.5 * LOG2_E
    w = (w_uq * scale).reshape(Q_LORA, N_HEADS, QK_NOPE + QK_ROPE)
    nope, rope = w[:, :, :QK_NOPE], w[:, :, QK_NOPE:]
    rope_sw = jnp.concatenate([rope[:, :, half:], rope[:, :, :half]], axis=2)
    zp = jnp.zeros((Q_LORA, N_HEADS, HEAD_PAD - QK_NOPE - QK_ROPE), F32)
    plain = jnp.concatenate([nope, rope, zp], axis=2).reshape(Q_LORA, N_HEADS * HEAD_PAD)
    swapped = jnp.concatenate([jnp.zeros_like(nope), rope_sw, zp], axis=2).reshape(Q_LORA, N_HEADS * HEAD_PAD)
    return jnp.concatenate([plain, swapped], axis=1).astype(BF16)


def _kv_weight(w_ukv):
    w = w_ukv.reshape(KV_LORA, N_HEADS, QK_NOPE + V_HEAD)
    k_nope, v = w[:, :, :QK_NOPE], w[:, :, QK_NOPE:]
    z = jnp.zeros_like(v)
    k_part = jnp.concatenate([k_nope, jnp.zeros_like(k_nope)], axis=2).reshape(KV_LORA, N_HEADS * HEAD_PAD)
    odd = (jnp.arange(N_HEADS) % 2 == 1)[None, :, None]
    v_part = jnp.concatenate([jnp.where(odd, z, v), jnp.where(odd, v, z)], axis=2).reshape(KV_LORA, N_HEADS * HEAD_PAD)
    return jnp.concatenate([k_part, v_part], axis=1).astype(BF16)


def _router_weight(w_rg, b_rg, w_re, b_re):
    pad = LANES - N_EXPERTS - N_GROUPS
    w = jnp.concatenate([w_re, w_rg, jnp.zeros((D_MODEL, pad), F32)], axis=1)
    b = jnp.concatenate([b_re, b_rg, jnp.zeros((pad,), F32)]).reshape(1, LANES)
    w_hi = w.astype(BF16)
    w_lo = (w - w_hi.astype(F32)).astype(BF16)
    return w_hi, w_lo, b


def kernel(x, p, positions, ln_mix_g, w_in, conv_a_w, w_out_a, q_norm_g, w_uq, kv_norm_g, w_ukv, w_out_b, conv_c_w, ln_c_g, ln_c_b, w_out_c, w_o, ln_ffn_g, w_route_grp, b_route_grp, w_route_exp, b_route_exp, w_exp_gate, w_exp_up, w_exp_down, ln_ple_g, w_ple_gate, w_ple, final_norm_g):
    c_tab, s_tab = _rope_tables(positions)
    xf = x.reshape(TOKENS, D_MODEL)
    pf = p.reshape(DEPTH, TOKENS, PLE_DIM)
    row = lambda a: a.reshape(1, -1)
    for i in range(DEPTH):
        proj = _inproj(xf, row(ln_mix_g[i]), _in_proj_weight(w_in[i]))
        q, k, v = _qkv(proj, c_tab, s_tab, row(q_norm_g[i]), row(kv_norm_g[i]), _q_weight(w_uq[i]), _kv_weight(w_ukv[i]))
        att = _attention(q, k, v)
        xf = _mixer_tail(proj, att, xf, conv_a_w[i], w_out_a[i].astype(BF16), conv_c_w[i], row(ln_c_g[i]),
                         row(ln_c_b[i]), w_out_c[i].astype(BF16), w_out_b[i].astype(BF16), w_o[i].astype(BF16))
        wr_hi, wr_lo, br = _router_weight(w_route_grp[i], b_route_grp[i], w_route_exp[i], b_route_exp[i])
        meta, meta_t, counts = _router(xf, row(ln_ffn_g[i]), wr_hi, wr_lo, br)
        pos1, pos2, tile_expert, n_tiles = _dispatch_plan(meta_t, counts)
        xs = _dispatch(pos1, pos2, xf, row(ln_ffn_g[i]))
        ys = _experts(i, tile_expert, n_tiles, xs, w_exp_gate, w_exp_up, w_exp_down)
        xf = _combine_ple(i, pos1, pos2, xf, meta, ys, pf, row(ln_ple_g[i]), w_ple_gate[i].astype(BF16),
                          w_ple[i].astype(BF16), row(final_norm_g), final=(i == DEPTH - 1))
    return xf.reshape(BATCH, SEQ, D_MODEL)
```

```python
import functools

import jax
import jax.numpy as jnp
from jax import lax
from jax.experimental import pallas as pl
from jax.experimental.pallas import tpu as pltpu

D_MODEL = 1024
BATCH = 8
SEQ = 2048
DEPTH = 2
TOKENS = BATCH * SEQ
PLE_DIM = 256
SC_WIDTH = 512
SC_KERNEL = 3
N_HEADS = 8
QK_NOPE = 64
QK_ROPE = 32
V_HEAD = 64
Q_LORA = 768
KV_LORA = 256
ROPE_THETA = 10000.0
CONF_WIDTH = 512
CONF_KERNEL = 31
N_GROUPS = 4
EXPERTS_PER_GROUP = 8
N_EXPERTS = N_GROUPS * EXPERTS_PER_GROUP
EXPERT_HIDDEN = 256
EPS = 1e-6
LOG2_E = 1.4426950408889634

LANES = 128
HEAD_PAD = 128
F32 = jnp.float32
BF16 = jnp.bfloat16

COL_GATES = 0
COL_SC = 3 * D_MODEL
COL_QLAT = COL_SC + 3 * SC_WIDTH
COL_KVLAT = COL_QLAT + Q_LORA
COL_GLU = COL_KVLAT + KV_LORA
COL_KR = COL_GLU + 2 * CONF_WIDTH
COL_KRSW = COL_KR + HEAD_PAD
PROJ_COLS = COL_KRSW + HEAD_PAD

ROUTER_GROUP_LANE = N_EXPERTS

TM_INPROJ = 1024
TN_INPROJ = 2304
TM_QKV = 512
T_ATTN = 512
HEADS_PER_STEP = 4
TM_MIX = 512
CONV_CHUNK = 64
HALO_C = 32
HALO_A = 16
TM_ROUTE = 512
TM_EXP = 256
EXPERT_PREFETCH = 2
EXPERT_SLOTS = EXPERT_PREFETCH + 1
TM_DISP = 256
TM_COMB = 256
TOP_K = 2
MAX_TILES = TOKENS * TOP_K // TM_EXP + N_EXPERTS
SLAB = D_MODEL // LANES
GATHER_UNROLL = 16
N_DMA_PRIORITIES = 2
META_E1, META_E2, META_R1, META_R2, META_W1, META_W2 = range(6)
META_ROWS = 8
SORTED_ROWS = MAX_TILES * TM_EXP * SLAB


def _cparams(semantics, vmem_mb):
    return pltpu.CompilerParams(dimension_semantics=semantics, vmem_limit_bytes=vmem_mb * 1024 * 1024)


def _sigmoid(x):
    return 1.0 / (1.0 + jnp.exp(-x))


def _rms(x, g):
    return x * lax.rsqrt(jnp.mean(x * x, axis=-1, keepdims=True) + EPS) * g


def _rope_kernel(pos_ref, freq_ref, cos_ref, sin_ref):
    ang = pos_ref[...].astype(F32) * freq_ref[...]
    cos_ref[...] = jnp.cos(ang)
    sin_ref[...] = jnp.sin(ang)


def _rope_tables(positions):
    half = QK_ROPE // 2
    inv_freq = ROPE_THETA ** (-jnp.arange(0, QK_ROPE, 2, dtype=F32) / QK_ROPE)
    rows = TOKENS * half // LANES
    pos_rep = jnp.broadcast_to(positions.reshape(TOKENS, 1), (TOKENS, half)).reshape(rows, LANES)
    freq = jnp.tile(inv_freq, LANES // half).reshape(1, LANES)
    cos, sin = pl.pallas_call(
        _rope_kernel,
        out_shape=(jax.ShapeDtypeStruct((rows, LANES), F32),) * 2,
        name="rope_tables",
    )(pos_rep, freq)
    cos = cos.reshape(TOKENS, half)
    sin = sin.reshape(TOKENS, half)
    ones = jnp.ones((TOKENS, QK_NOPE), F32)
    zn = jnp.zeros((TOKENS, QK_NOPE), F32)
    zp = jnp.zeros((TOKENS, HEAD_PAD - QK_NOPE - QK_ROPE), F32)
    c_tab = jnp.concatenate([ones, cos, cos, zp], axis=1)
    s_tab = jnp.concatenate([zn, -sin, sin, zp], axis=1)
    return c_tab, s_tab


def _inproj_kernel(x_ref, g_ref, w_ref, o_ref, h_ref):
    @pl.when(pl.program_id(1) == 0)
    def _():
        h_ref[...] = _rms(x_ref[...], g_ref[...]).astype(BF16)

    o_ref[...] = jnp.dot(h_ref[...], w_ref[...], preferred_element_type=F32).astype(BF16)


def _inproj(x, g, w):
    tm, tn = TM_INPROJ, TN_INPROJ
    return pl.pallas_call(
        _inproj_kernel,
        grid=(TOKENS // tm, PROJ_COLS // tn),
        in_specs=[
            pl.BlockSpec((tm, D_MODEL), lambda i, j: (i, 0)),
            pl.BlockSpec((1, D_MODEL), lambda i, j: (0, 0)),
            pl.BlockSpec((D_MODEL, tn), lambda i, j: (0, j)),
        ],
        out_specs=pl.BlockSpec((tm, tn), lambda i, j: (i, j)),
        out_shape=jax.ShapeDtypeStruct((TOKENS, PROJ_COLS), BF16),
        scratch_shapes=[pltpu.VMEM((tm, D_MODEL), BF16)],
        compiler_params=_cparams(("parallel", "arbitrary"), 48),
        name="in_proj",
    )(x, g, w)


def _qkv_kernel(ql_ref, kvl_ref, kr_ref, krsw_ref, c_ref, s_ref, qg_ref, kvg_ref, wq_ref, wkv_ref,
                q_out, k_out, v_out):
    c = c_ref[...]
    s = s_ref[...]
    width = N_HEADS * HEAD_PAD
    qn = _rms(ql_ref[...].astype(F32), qg_ref[...]).astype(BF16)
    qq = jnp.dot(qn, wq_ref[...], preferred_element_type=F32)
    for h in range(N_HEADS):
        lo, hi = h * HEAD_PAD, (h + 1) * HEAD_PAD
        q_out[:, lo:hi] = (qq[:, lo:hi] * c + qq[:, width + lo:width + hi] * s).astype(BF16)
    kvn = _rms(kvl_ref[...].astype(F32), kvg_ref[...]).astype(BF16)
    kk = jnp.dot(kvn, wkv_ref[...], preferred_element_type=F32)
    kr = kr_ref[...].astype(F32) * c + krsw_ref[...].astype(F32) * s
    for h in range(N_HEADS):
        lo, hi = h * HEAD_PAD, (h + 1) * HEAD_PAD
        k_out[:, lo:hi] = (kk[:, lo:hi] + kr).astype(BF16)
    v_out[...] = kk[:, width:].astype(BF16)


def _qkv(proj, c_tab, s_tab, qg, kvg, wq, wkv):
    tm = TM_QKV
    width = N_HEADS * HEAD_PAD
    row = lambda blk: (lambda i: (i, blk))
    const = lambda i: (0, 0)
    return pl.pallas_call(
        _qkv_kernel,
        grid=(TOKENS // tm,),
        in_specs=[
            pl.BlockSpec((tm, Q_LORA), row(COL_QLAT // Q_LORA)),
            pl.BlockSpec((tm, KV_LORA), row(COL_KVLAT // KV_LORA)),
            pl.BlockSpec((tm, HEAD_PAD), row(COL_KR // HEAD_PAD)),
            pl.BlockSpec((tm, HEAD_PAD), row(COL_KRSW // HEAD_PAD)),
            pl.BlockSpec((tm, HEAD_PAD), row(0)),
            pl.BlockSpec((tm, HEAD_PAD), row(0)),
            pl.BlockSpec((1, Q_LORA), const),
            pl.BlockSpec((1, KV_LORA), const),
            pl.BlockSpec((Q_LORA, 2 * width), const),
            pl.BlockSpec((KV_LORA, 2 * width), const),
        ],
        out_specs=[pl.BlockSpec((tm, width), row(0))] * 3,
        out_shape=(jax.ShapeDtypeStruct((TOKENS, width), BF16),) * 3,
        compiler_params=_cparams(("parallel",), 48),
        name="qkv_prep",
    )(proj, proj, proj, proj, c_tab, s_tab, qg, kvg, wq, wkv)


def _attn_kernel(q_ref, k_ref, v_ref, o_ref, *state):
    t = T_ATTN
    nh = HEADS_PER_STEP
    qi = pl.program_id(2)
    nt = (((1,), (1,)), ((), ()))
    n_blk = t // LANES
    m_sc, l_sc, acc_sc = state[0:nh], state[nh:2 * nh], state[2 * nh:3 * nh]
    for h in range(nh):
        m_sc[h][...] = jnp.full((t, LANES), -jnp.inf, F32)
        l_sc[h][...] = jnp.zeros((t, LANES), F32)
        acc_sc[h][...] = jnp.zeros((t, LANES), F32)

    def tile(j, masked):
        start = pl.multiple_of(j * t, t)
        for h in range(nh):
            lo, hi = h * HEAD_PAD, (h + 1) * HEAD_PAD
            s = lax.dot_general(q_ref[:, lo:hi], k_ref[pl.ds(start, t), lo:hi], nt, preferred_element_type=F32)
            if masked:
                row_id = lax.broadcasted_iota(jnp.int32, (t, t), 0)
                col_id = lax.broadcasted_iota(jnp.int32, (t, t), 1)
                s = jnp.where(row_id >= col_id, s, -jnp.inf)
            blocks = [s[:, c * LANES:(c + 1) * LANES] for c in range(n_blk)]
            bmax = functools.reduce(jnp.maximum, blocks)
            m_old = m_sc[h][...]
            m_new = jnp.maximum(m_old, jnp.max(bmax, axis=-1, keepdims=True))
            alpha = jnp.exp2(m_old - m_new)
            ps = [jnp.exp2(b - m_new) for b in blocks]
            p = jnp.concatenate(ps, axis=1).astype(BF16)
            l_sc[h][...] = alpha * l_sc[h][...] + functools.reduce(jnp.add, ps)
            acc_sc[h][...] = alpha * acc_sc[h][...] + jnp.dot(p, v_ref[pl.ds(start, t), lo:hi],
                                                               preferred_element_type=F32)
            m_sc[h][...] = m_new

    def body(j, carry):
        tile(j, False)
        return carry

    lax.fori_loop(0, qi, body, 0)
    tile(qi, True)
    out = [acc_sc[h][...] / jnp.sum(l_sc[h][...], axis=-1, keepdims=True) for h in range(nh)]
    o_ref[...] = jnp.concatenate([out[h] + out[h + 1] for h in range(0, nh, 2)], axis=1).astype(BF16)


def _attention(q, k, v):
    t = T_ATTN
    nq = SEQ // t
    nh = HEADS_PER_STEP
    return pl.pallas_call(
        _attn_kernel,
        grid=(BATCH, N_HEADS // nh, nq),
        in_specs=[
            pl.BlockSpec((t, nh * HEAD_PAD), lambda b, hg, i: (b * nq + i, hg)),
            pl.BlockSpec((SEQ, nh * HEAD_PAD), lambda b, hg, i: (b, hg)),
            pl.BlockSpec((SEQ, nh * HEAD_PAD), lambda b, hg, i: (b, hg)),
        ],
        out_specs=pl.BlockSpec((t, nh * V_HEAD), lambda b, hg, i: (b * nq + i, hg)),
        out_shape=jax.ShapeDtypeStruct((TOKENS, N_HEADS * V_HEAD), BF16),
        scratch_shapes=[pltpu.VMEM((t, LANES), F32)] * (3 * nh),
        compiler_params=_cparams(("parallel", "parallel", "arbitrary"), 48),
        name="mla_attention",
    )(q, k, v)


def _mixer_tail_kernel(gates_ref, sc_ref, sch_ref, gv_ref, gg_ref, gvh_ref, ggh_ref, att_ref, x_ref,
                       cwa_ref, woa_ref, cwc_ref, lng_ref, lnb_ref, woc_ref, wob_ref, wo_ref,
                       out_ref, cbuf, ubuf, shifted, vbuf):
    tm = TM_MIX
    has_past = (pl.program_id(0) % (SEQ // tm)) != 0

    sc = sc_ref[...]
    sc_b = sc[:, 0:SC_WIDTH].astype(F32)
    cbuf[8:8 + tm, :] = sc[:, SC_WIDTH:2 * SC_WIDTH].astype(F32) * sc[:, 2 * SC_WIDTH:].astype(F32)
    sch = sch_ref[...].astype(F32)[HALO_A - 8:HALO_A, :]
    cbuf[0:8, :] = jnp.where(has_past, sch[:, SC_WIDTH:2 * SC_WIDTH] * sch[:, 2 * SC_WIDTH:], 0.0)
    conv_a = cwa_ref[0:1, :] * cbuf[6:6 + tm, :]
    for t in range(1, SC_KERNEL):
        conv_a = conv_a + cwa_ref[t:t + 1, :] * cbuf[6 + t:6 + t + tm, :]
    y_a = jnp.dot((sc_b * conv_a).astype(BF16), woa_ref[...], preferred_element_type=F32)

    ubuf[HALO_C:HALO_C + tm, :] = gv_ref[...].astype(F32) * _sigmoid(gg_ref[...].astype(F32))
    ubuf[0:HALO_C, :] = jnp.where(has_past, gvh_ref[...].astype(F32) * _sigmoid(ggh_ref[...].astype(F32)), 0.0)
    rows = tm + HALO_C - 8
    for b in range(1, 8):
        shifted[b - 1, 0:rows, :] = ubuf[b:b + rows, :]
    base = HALO_C - (CONF_KERNEL - 1)
    for r0 in range(0, tm, CONV_CHUNK):
        acc = None
        for t in range(CONF_KERNEL):
            off = base + t
            a0 = r0 + off - off % 8
            src = ubuf[a0:a0 + CONV_CHUNK, :] if off % 8 == 0 else shifted[off % 8 - 1, a0:a0 + CONV_CHUNK, :]
            term = cwc_ref[t:t + 1, :] * src
            acc = term if acc is None else acc + term
        mu = jnp.mean(acc, axis=-1, keepdims=True)
        xc = acc - mu
        var = jnp.mean(xc * xc, axis=-1, keepdims=True)
        y = xc * lax.rsqrt(var + EPS) * lng_ref[...] + lnb_ref[...]
        vbuf[r0:r0 + CONV_CHUNK, :] = (y * _sigmoid(y)).astype(BF16)
    y_c = jnp.dot(vbuf[...], woc_ref[...], preferred_element_type=F32)

    y_b = jnp.dot(att_ref[...], wob_ref[...], preferred_element_type=F32)

    g = gates_ref[...]
    merged = (_sigmoid(g[:, 0:D_MODEL].astype(F32)) * y_a
              + _sigmoid(g[:, D_MODEL:2 * D_MODEL].astype(F32)) * y_b
              + _sigmoid(g[:, 2 * D_MODEL:].astype(F32)) * y_c)
    out_ref[...] = x_ref[...] + jnp.dot(merged.astype(BF16), wo_ref[...], preferred_element_type=F32)


def _mixer_tail(proj, att, x, cwa, woa, cwc, lng, lnb, woc, wob, wo):
    tm = TM_MIX
    row = lambda width, col: pl.BlockSpec((tm, width), lambda i: (i, col // width))
    halo = lambda rows, width, col: pl.BlockSpec(
        (rows, width), lambda i: (jnp.maximum(i * (tm // rows) - 1, 0), col // width))
    const = lambda a: pl.BlockSpec(a.shape, lambda i: (0,) * a.ndim)
    weights = (cwa, woa, cwc, lng, lnb, woc, wob, wo)
    return pl.pallas_call(
        _mixer_tail_kernel,
        grid=(TOKENS // tm,),
        in_specs=[
            row(3 * D_MODEL, COL_GATES),
            row(3 * SC_WIDTH, COL_SC),
            halo(HALO_A, 3 * SC_WIDTH, COL_SC),
            row(CONF_WIDTH, COL_GLU),
            row(CONF_WIDTH, COL_GLU + CONF_WIDTH),
            halo(HALO_C, CONF_WIDTH, COL_GLU),
            halo(HALO_C, CONF_WIDTH, COL_GLU + CONF_WIDTH),
            pl.BlockSpec((tm, N_HEADS * V_HEAD), lambda i: (i, 0)),
            pl.BlockSpec((tm, D_MODEL), lambda i: (i, 0)),
        ] + [const(a) for a in weights],
        out_specs=pl.BlockSpec((tm, D_MODEL), lambda i: (i, 0)),
        out_shape=jax.ShapeDtypeStruct((TOKENS, D_MODEL), F32),
        scratch_shapes=[
            pltpu.VMEM((tm + 8, SC_WIDTH), F32),
            pltpu.VMEM((tm + HALO_C, CONF_WIDTH), F32),
            pltpu.VMEM((7, tm + HALO_C - 8, CONF_WIDTH), F32),
            pltpu.VMEM((tm, CONF_WIDTH), BF16),
        ],
        compiler_params=_cparams(("parallel",), 56),
        name="mixer_tail",
    )(proj, proj, proj, proj, proj, proj, proj, att, x, *weights)


def _route(logits):
    lane = lax.broadcasted_iota(jnp.int32, logits.shape, 1)
    lane_f = lane.astype(F32)
    neg = -jnp.inf
    big = float(LANES)
    is_grp = (lane >= ROUTER_GROUP_LANE) & (lane < ROUTER_GROUP_LANE + N_GROUPS)
    glog = jnp.where(is_grp, logits, neg)
    gmax = jnp.max(glog, axis=-1, keepdims=True)
    gidx = jnp.min(jnp.where(glog == gmax, lane_f, big), axis=-1, keepdims=True)
    p_sel = 1.0 / jnp.sum(jnp.exp(glog - gmax), axis=-1, keepdims=True)
    first = (gidx - ROUTER_GROUP_LANE) * EXPERTS_PER_GROUP
    in_grp = (lane_f >= first) & (lane_f < first + EXPERTS_PER_GROUP)
    el = jnp.where(in_grp, logits, neg)
    m1 = jnp.max(el, axis=-1, keepdims=True)
    i1 = jnp.min(jnp.where(el == m1, lane_f, big), axis=-1, keepdims=True)
    el2 = jnp.where(lane_f == i1, neg, el)
    m2 = jnp.max(el2, axis=-1, keepdims=True)
    i2 = jnp.min(jnp.where(el2 == m2, lane_f, big), axis=-1, keepdims=True)
    e2 = jnp.exp(m2 - m1)
    w1 = p_sel / (1.0 + e2)
    w2 = w1 * e2
    return i1, i2, w1, w2


def _router_kernel(x_ref, g_ref, wrh_ref, wrl_ref, br_ref, meta_ref, meta_t_ref, cnt_ref, xs0_ref, run_ref):
    tm = TM_ROUTE

    @pl.when(pl.program_id(0) == 0)
    def _():
        run_ref[...] = jnp.zeros_like(run_ref)

    h = _rms(x_ref[...], g_ref[...])
    h_hi = h.astype(BF16)
    h_lo = (h - h_hi.astype(F32)).astype(BF16)
    logits = (jnp.dot(h_hi, wrh_ref[...], preferred_element_type=F32)
              + jnp.dot(h_hi, wrl_ref[...], preferred_element_type=F32)
              + jnp.dot(h_lo, wrh_ref[...], preferred_element_type=F32)
              + br_ref[...])
    i1, i2, w1, w2 = _route(logits)
    lane = lax.broadcasted_iota(jnp.int32, (tm, LANES), 1)
    lane_f = lane.astype(F32)
    oh1 = lane_f == i1
    oh2 = lane_f == i2
    onehot = jnp.where(oh1, 1.0, 0.0) + jnp.where(oh2, 1.0, 0.0)
    row_id = lax.broadcasted_iota(jnp.int32, (tm, tm), 0)
    col_id = lax.broadcasted_iota(jnp.int32, (tm, tm), 1)
    below = jnp.where(row_id > col_id, 1.0, 0.0).astype(BF16)
    before = run_ref[...] + jnp.dot(below, onehot.astype(BF16), preferred_element_type=F32)
    r1 = jnp.sum(jnp.where(oh1, before, 0.0), axis=-1, keepdims=True)
    r2 = jnp.sum(jnp.where(oh2, before, 0.0), axis=-1, keepdims=True)
    run_ref[...] += jnp.sum(onehot, axis=0, keepdims=True)
    cnt_ref[...] = run_ref[...]
    meta = jnp.zeros((tm, LANES), F32)
    for col, val in enumerate((i1, i2, r1, r2, w1, w2)):
        meta = jnp.where(lane == col, val, meta)
    meta_ref[...] = meta
    meta_t_ref[...] = meta.T[0:META_ROWS, :]
    xs0_ref[...] = jnp.zeros_like(xs0_ref)


def _router(x, g, wrh, wrl, br):
    tm = TM_ROUTE
    const = lambda i: (0, 0)
    return pl.pallas_call(
        _router_kernel,
        grid=(TOKENS // tm,),
        in_specs=[
            pl.BlockSpec((tm, D_MODEL), lambda i: (i, 0)),
            pl.BlockSpec((1, D_MODEL), const),
            pl.BlockSpec((D_MODEL, LANES), const),
            pl.BlockSpec((D_MODEL, LANES), const),
            pl.BlockSpec((1, LANES), const),
        ],
        out_specs=[
            pl.BlockSpec((tm, LANES), lambda i: (i, 0)),
            pl.BlockSpec((META_ROWS, tm), lambda i: (0, i)),
            pl.BlockSpec((1, LANES), const),
            pl.BlockSpec((SORTED_ROWS // (TOKENS // tm), LANES), lambda i: (i, 0)),
        ],
        out_shape=(
            jax.ShapeDtypeStruct((TOKENS, LANES), F32),
            jax.ShapeDtypeStruct((META_ROWS, TOKENS), F32),
            jax.ShapeDtypeStruct((1, LANES), F32),
            jax.ShapeDtypeStruct((SORTED_ROWS, LANES), F32),
        ),
        scratch_shapes=[pltpu.VMEM((1, LANES), F32)],
        compiler_params=_cparams(("arbitrary",), 48),
        name="moe_router",
    )(x, g, wrh, wrl, br)


def _dispatch_plan(meta_t, counts):
    e1 = meta_t[META_E1].astype(jnp.int32)
    e2 = meta_t[META_E2].astype(jnp.int32)
    r1 = meta_t[META_R1].astype(jnp.int32)
    r2 = meta_t[META_R2].astype(jnp.int32)
    cnt = counts[0, :N_EXPERTS].astype(jnp.int32)
    tiles = (cnt + TM_EXP - 1) // TM_EXP
    tile_end = jnp.cumsum(tiles)
    first_slot = ((tile_end - tiles) * TM_EXP)[:, None]
    expert = jnp.arange(N_EXPERTS, dtype=jnp.int32)[:, None]
    pos1 = jnp.sum(jnp.where(e1[None, :] == expert, first_slot, 0), axis=0) + r1
    pos2 = jnp.sum(jnp.where(e2[None, :] == expert, first_slot, 0), axis=0) + r2
    n_tiles = tile_end[-1:]
    tile_id = jnp.minimum(jnp.arange(MAX_TILES, dtype=jnp.int32), n_tiles - 1)
    tile_expert = jnp.sum((tile_id[:, None] >= tile_end[None, :]).astype(jnp.int32), axis=1)
    return pos1, pos2, tile_expert, n_tiles


def _dispatch_kernel(pos1_ref, pos2_ref, x_ref, g_ref, xs_in_ref, xs_ref, slab, sem):
    del xs_in_ref
    tm = TM_DISP
    i = pl.program_id(0)
    last = pl.num_programs(0) - 1
    slot = i % 2

    def wait_copies(sl):
        for _ in range(TOP_K):
            pltpu.make_async_copy(slab.at[sl], xs_ref.at[pl.ds(0, tm * SLAB), :], sem.at[sl]).wait()

    @pl.when(i >= 2)
    def _():
        wait_copies(slot)

    h = _rms(x_ref[...], g_ref[...])
    for s in range(SLAB):
        slab[slot, pl.ds(s, tm, stride=SLAB), :] = h[:, s * LANES:(s + 1) * LANES]

    def body(c, carry):
        for u in range(GATHER_UNROLL):
            r = c * GATHER_UNROLL + u
            src = slab.at[slot, pl.ds(pl.multiple_of(r * SLAB, SLAB), SLAB), :]
            for k, pos_ref in enumerate((pos1_ref, pos2_ref)):
                p = pos_ref[i * tm + r]
                pltpu.make_async_copy(src, xs_ref.at[pl.ds(pl.multiple_of(p * SLAB, SLAB), SLAB), :],
                                      sem.at[slot]).start(priority=k % N_DMA_PRIORITIES)
        return carry

    lax.fori_loop(0, tm // GATHER_UNROLL, body, 0)

    @pl.when(i == last)
    def _():
        wait_copies(1 - slot)
        wait_copies(slot)


def _dispatch(pos1, pos2, x, g, xs0):
    tm = TM_DISP
    return pl.pallas_call(
        _dispatch_kernel,
        grid_spec=pltpu.PrefetchScalarGridSpec(
            num_scalar_prefetch=2,
            grid=(TOKENS // tm,),
            in_specs=[
                pl.BlockSpec((tm, D_MODEL), lambda i, p1, p2: (i, 0)),
                pl.BlockSpec((1, D_MODEL), lambda i, p1, p2: (0, 0)),
                pl.BlockSpec(memory_space=pl.ANY),
            ],
            out_specs=pl.BlockSpec(memory_space=pl.ANY),
            scratch_shapes=[pltpu.VMEM((2, tm * SLAB, LANES), F32), pltpu.SemaphoreType.DMA((2,))],
        ),
        out_shape=jax.ShapeDtypeStruct((SORTED_ROWS, LANES), F32),
        input_output_aliases={4: 0},
        compiler_params=_cparams(("arbitrary",), 48),
        name="moe_dispatch",
    )(pos1, pos2, x, g, xs0)


def _expert_kernel(te_ref, nt_ref, xs_ref, wg_ref, wu_ref, wd_ref, ys_ref, xbuf, sem):
    del te_ref
    tm = TM_EXP
    i = pl.program_id(0)
    n = nt_ref[0]

    def tile_copy(tile):
        slot = tile % EXPERT_SLOTS
        rows = pl.ds(pl.multiple_of(tile * (tm * SLAB), tm * SLAB), tm * SLAB)
        return pltpu.make_async_copy(xs_ref.at[rows, :], xbuf.at[slot], sem.at[slot])

    @pl.when(i == 0)
    def _():
        for ahead in range(EXPERT_PREFETCH):
            @pl.when(ahead < n)
            def _():
                tile_copy(ahead).start()

    @pl.when(i + EXPERT_PREFETCH < n)
    def _():
        tile_copy(i + EXPERT_PREFETCH).start()

    @pl.when(i < n)
    def _():
        tile_copy(i).wait()
        slot = i % EXPERT_SLOTS
        xt = jnp.concatenate([xbuf[slot, pl.ds(s, tm, stride=SLAB), :].astype(BF16) for s in range(SLAB)], axis=1)
        hg = jnp.dot(xt, wg_ref[...].astype(BF16), preferred_element_type=F32)
        hu = jnp.dot(xt, wu_ref[...].astype(BF16), preferred_element_type=F32)
        hh = (hg * _sigmoid(hg) * hu).astype(BF16)
        y = jnp.dot(hh, wd_ref[...].astype(BF16), preferred_element_type=F32)
        for s in range(SLAB):
            ys_ref[pl.ds(s, tm, stride=SLAB), :] = y[:, s * LANES:(s + 1) * LANES]

    @pl.when(i >= n)
    def _():
        ys_ref[...] = jnp.zeros_like(ys_ref)


def _experts(layer, tile_expert, n_tiles, xs, w_gate, w_up, w_down):
    tm = TM_EXP
    wspec = lambda rows, cols: pl.BlockSpec((None, None, rows, cols), lambda i, te, nt: (layer, te[i], 0, 0))
    return pl.pallas_call(
        _expert_kernel,
        grid_spec=pltpu.PrefetchScalarGridSpec(
            num_scalar_prefetch=2,
            grid=(MAX_TILES,),
            in_specs=[
                pl.BlockSpec(memory_space=pl.ANY),
                wspec(D_MODEL, EXPERT_HIDDEN),
                wspec(D_MODEL, EXPERT_HIDDEN),
                wspec(EXPERT_HIDDEN, D_MODEL),
            ],
            out_specs=pl.BlockSpec((tm * SLAB, LANES), lambda i, te, nt: (i, 0)),
            scratch_shapes=[pltpu.VMEM((EXPERT_SLOTS, tm * SLAB, LANES), F32), pltpu.SemaphoreType.DMA((EXPERT_SLOTS,))],
        ),
        out_shape=jax.ShapeDtypeStruct((SORTED_ROWS, LANES), F32),
        compiler_params=_cparams(("arbitrary",), 48),
        name="moe_experts",
    )(tile_expert, n_tiles, xs, w_gate, w_up, w_down)


def _start_slab_gathers(idx_refs, base, n_rows, src_hbm, dst_bufs, sem):
    def body(c, carry):
        for u in range(GATHER_UNROLL):
            r = c * GATHER_UNROLL + u
            for k, (idx_ref, dst) in enumerate(zip(idx_refs, dst_bufs)):
                t = idx_ref[base + r]
                pltpu.make_async_copy(src_hbm.at[pl.ds(pl.multiple_of(t * SLAB, SLAB), SLAB), :],
                                      dst.at[pl.ds(pl.multiple_of(r * SLAB, SLAB), SLAB), :],
                                      sem).start(priority=k % N_DMA_PRIORITIES)
        return carry

    lax.fori_loop(0, n_rows // GATHER_UNROLL, body, 0)


def _wait_slab_gathers(n_rows, src_hbm, dst, sem):
    pltpu.make_async_copy(src_hbm.at[pl.ds(0, n_rows * SLAB), :], dst, sem).wait()


def _combine_ple_kernel(pos1_ref, pos2_ref, x_ref, meta_ref, ys_ref, p_ref, g_ref, wg_ref, wp_ref, fg_ref,
                        out_ref, cbuf, sem, *, final):
    tm = TM_COMB
    i = pl.program_id(0)

    def gather(tile, slot):
        _start_slab_gathers((pos1_ref, pos2_ref), tile * tm, tm, ys_ref, (cbuf.at[slot, 0], cbuf.at[slot, 1]),
                            sem.at[slot])

    @pl.when(i == 0)
    def _():
        gather(0, 0)

    @pl.when(i + 1 < pl.num_programs(0))
    def _():
        gather(i + 1, (i + 1) % 2)

    slot = i % 2
    for k in range(2):
        _wait_slab_gathers(tm, ys_ref, cbuf.at[slot, k], sem.at[slot])
    meta = meta_ref[...]
    w1 = meta[:, META_W1:META_W1 + 1]
    w2 = meta[:, META_W2:META_W2 + 1]
    moe = jnp.concatenate([w1 * cbuf[slot, 0, pl.ds(s, tm, stride=SLAB), :]
                           + w2 * cbuf[slot, 1, pl.ds(s, tm, stride=SLAB), :] for s in range(SLAB)], axis=1)
    x = x_ref[...] + moe
    h = _rms(x, g_ref[...]).astype(BF16)
    gate = _sigmoid(jnp.dot(h, wg_ref[...], preferred_element_type=F32))
    emb = jnp.dot(p_ref[...].astype(BF16), wp_ref[...], preferred_element_type=F32)
    y = x + gate * emb
    if final:
        y = _rms(y, fg_ref[...])
    out_ref[...] = y


def _combine_ple(layer, pos1, pos2, x, meta, ys, p, g, wg, wp, fg, final):
    tm = TM_COMB
    const = lambda i, p1, p2: (0, 0)
    rows = lambda i, p1, p2: (i, 0)
    return pl.pallas_call(
        functools.partial(_combine_ple_kernel, final=final),
        grid_spec=pltpu.PrefetchScalarGridSpec(
            num_scalar_prefetch=2,
            grid=(TOKENS // tm,),
            in_specs=[
                pl.BlockSpec((tm, D_MODEL), rows),
                pl.BlockSpec((tm, LANES), rows),
                pl.BlockSpec(memory_space=pl.ANY),
                pl.BlockSpec((None, tm, PLE_DIM), lambda i, p1, p2: (layer, i, 0)),
                pl.BlockSpec((1, D_MODEL), const),
                pl.BlockSpec((D_MODEL, D_MODEL), const),
                pl.BlockSpec((PLE_DIM, D_MODEL), const),
                pl.BlockSpec((1, D_MODEL), const),
            ],
            out_specs=pl.BlockSpec((tm, D_MODEL), rows),
            scratch_shapes=[pltpu.VMEM((2, 2, tm * SLAB, LANES), F32), pltpu.SemaphoreType.DMA((2,))],
        ),
        out_shape=jax.ShapeDtypeStruct((TOKENS, D_MODEL), F32),
        compiler_params=_cparams(("arbitrary",), 48),
        name="moe_combine_ple",
    )(pos1, pos2, x, meta, ys, p, g, wg, wp, fg)


def _in_proj_weight(w_in):
    o = 3 * D_MODEL + 3 * SC_WIDTH
    q_lat = w_in[:, o:o + Q_LORA]
    o += Q_LORA
    kv_lat = w_in[:, o:o + KV_LORA]
    o += KV_LORA
    k_rope = w_in[:, o:o + QK_ROPE]
    o += QK_ROPE
    glu = w_in[:, o:]
    half = QK_ROPE // 2
    zn = jnp.zeros((D_MODEL, QK_NOPE), F32)
    zp = jnp.zeros((D_MODEL, HEAD_PAD - QK_NOPE - QK_ROPE), F32)
    kr = jnp.concatenate([zn, k_rope, zp], axis=1)
    krsw = jnp.concatenate([zn, k_rope[:, half:], k_rope[:, :half], zp], axis=1)
    w = jnp.concatenate([w_in[:, :3 * D_MODEL + 3 * SC_WIDTH], q_lat, kv_lat, glu, kr, krsw], axis=1)
    return w.astype(BF16)


def _q_weight(w_uq):
    half = QK_ROPE // 2
    scale = (QK_NOPE + QK_ROPE) ** -0.5 * LOG2_E
    w = (w_uq * scale).reshape(Q_LORA, N_HEADS, QK_NOPE + QK_ROPE)
    nope, rope = w[:, :, :QK_NOPE], w[:, :, QK_NOPE:]
    rope_sw = jnp.concatenate([rope[:, :, half:], rope[:, :, :half]], axis=2)
    zp = jnp.zeros((Q_LORA, N_HEADS, HEAD_PAD - QK_NOPE - QK_ROPE), F32)
    plain = jnp.concatenate([nope, rope, zp], axis=2).reshape(Q_LORA, N_HEADS * HEAD_PAD)
    swapped = jnp.concatenate([jnp.zeros_like(nope), rope_sw, zp], axis=2).reshape(Q_LORA, N_HEADS * HEAD_PAD)
    return jnp.concatenate([plain, swapped], axis=1).astype(BF16)


def _kv_weight(w_ukv):
    w = w_ukv.reshape(KV_LORA, N_HEADS, QK_NOPE + V_HEAD)
    k_nope, v = w[:, :, :QK_NOPE], w[:, :, QK_NOPE:]
    z = jnp.zeros_like(v)
    k_part = jnp.concatenate([k_nope, jnp.zeros_like(k_nope)], axis=2).reshape(KV_LORA, N_HEADS * HEAD_PAD)
    odd = (jnp.arange(N_HEADS) % 2 == 1)[None, :, None]
    v_part = jnp.concatenate([jnp.where(odd, z, v), jnp.where(odd, v, z)], axis=2).reshape(KV_LORA, N_HEADS * HEAD_PAD)
    return jnp.concatenate([k_part, v_part], axis=1).astype(BF16)


def _router_weight(w_rg, b_rg, w_re, b_re):
    pad = LANES - N_EXPERTS - N_GROUPS
    w = jnp.concatenate([w_re, w_rg, jnp.zeros((D_MODEL, pad), F32)], axis=1)
    b = jnp.concatenate([b_re, b_rg, jnp.zeros((pad,), F32)]).reshape(1, LANES)
    w_hi = w.astype(BF16)
    w_lo = (w - w_hi.astype(F32)).astype(BF16)
    return w_hi, w_lo, b


def kernel(x, p, positions, ln_mix_g, w_in, conv_a_w, w_out_a, q_norm_g, w_uq, kv_norm_g, w_ukv, w_out_b, conv_c_w, ln_c_g, ln_c_b, w_out_c, w_o, ln_ffn_g, w_route_grp, b_route_grp, w_route_exp, b_route_exp, w_exp_gate, w_exp_up, w_exp_down, ln_ple_g, w_ple_gate, w_ple, final_norm_g):
    c_tab, s_tab = _rope_tables(positions)
    xf = x.reshape(TOKENS, D_MODEL)
    pf = p.reshape(DEPTH, TOKENS, PLE_DIM)
    row = lambda a: a.reshape(1, -1)
    for i in range(DEPTH):
        proj = _inproj(xf, row(ln_mix_g[i]), _in_proj_weight(w_in[i]))
        q, k, v = _qkv(proj, c_tab, s_tab, row(q_norm_g[i]), row(kv_norm_g[i]), _q_weight(w_uq[i]), _kv_weight(w_ukv[i]))
        att = _attention(q, k, v)
        xf = _mixer_tail(proj, att, xf, conv_a_w[i], w_out_a[i].astype(BF16), conv_c_w[i], row(ln_c_g[i]),
                         row(ln_c_b[i]), w_out_c[i].astype(BF16), w_out_b[i].astype(BF16), w_o[i].astype(BF16))
        wr_hi, wr_lo, br = _router_weight(w_route_grp[i], b_route_grp[i], w_route_exp[i], b_route_exp[i])
        meta, meta_t, counts, xs0 = _router(xf, row(ln_ffn_g[i]), wr_hi, wr_lo, br)
        pos1, pos2, tile_expert, n_tiles = _dispatch_plan(meta_t, counts)
        xs = _dispatch(pos1, pos2, xf, row(ln_ffn_g[i]), xs0)
        ys = _experts(i, tile_expert, n_tiles, xs, w_exp_gate, w_exp_up, w_exp_down)
        xf = _combine_ple(i, pos1, pos2, xf, meta, ys, pf, row(ln_ple_g[i]), w_ple_gate[i].astype(BF16),
                          w_ple[i].astype(BF16), row(final_norm_g), final=(i == DEPTH - 1))
    return xf.reshape(BATCH, SEQ, D_MODEL)
```

```python
import functools

import jax
import jax.numpy as jnp
from jax import lax
from jax.experimental import pallas as pl
from jax.experimental.pallas import tpu as pltpu

D_MODEL = 1024
BATCH = 8
SEQ = 2048
DEPTH = 2
TOKENS = BATCH * SEQ
PLE_DIM = 256
SC_WIDTH = 512
SC_KERNEL = 3
N_HEADS = 8
QK_NOPE = 64
QK_ROPE = 32
V_HEAD = 64
Q_LORA = 768
KV_LORA = 256
ROPE_THETA = 10000.0
CONF_WIDTH = 512
CONF_KERNEL = 31
N_GROUPS = 4
EXPERTS_PER_GROUP = 8
N_EXPERTS = N_GROUPS * EXPERTS_PER_GROUP
EXPERT_HIDDEN = 256
EPS = 1e-6
LOG2_E = 1.4426950408889634

LANES = 128
HEAD_PAD = 128
F32 = jnp.float32
BF16 = jnp.bfloat16

COL_GATES = 0
COL_SC = 3 * D_MODEL
COL_QLAT = COL_SC + 3 * SC_WIDTH
COL_KVLAT = COL_QLAT + Q_LORA
COL_GLU = COL_KVLAT + KV_LORA
COL_KR = COL_GLU + 2 * CONF_WIDTH
COL_KRSW = COL_KR + HEAD_PAD
PROJ_COLS = COL_KRSW + HEAD_PAD

ROUTER_GROUP_LANE = N_EXPERTS

TM_INPROJ = 1024
TN_INPROJ = 2304
TM_QKV = 512
T_ATTN = 512
HEADS_PER_STEP = 4
TM_MIX = 512
CONV_CHUNK = 64
HALO_C = 32
HALO_A = 16
TM_ROUTE = 512
TM_EXP = 256
EXPERT_PREFETCH = 2
EXPERT_SLOTS = EXPERT_PREFETCH + 1
TM_DISP = 256
TM_COMB = 256
TOP_K = 2
MAX_TILES = TOKENS * TOP_K // TM_EXP + N_EXPERTS
SLAB = D_MODEL // LANES
GATHER_UNROLL = 16
N_DMA_PRIORITIES = 2
META_E1, META_E2, META_R1, META_R2, META_W1, META_W2 = range(6)
META_ROWS = 8
SORTED_ROWS = MAX_TILES * TM_EXP * SLAB


def _cparams(semantics, vmem_mb):
    return pltpu.CompilerParams(dimension_semantics=semantics, vmem_limit_bytes=vmem_mb * 1024 * 1024)


def _sigmoid(x):
    return 1.0 / (1.0 + jnp.exp(-x))


def _rms(x, g):
    return x * lax.rsqrt(jnp.mean(x * x, axis=-1, keepdims=True) + EPS) * g


def _place(x, onehot):
    x1 = x.astype(BF16)
    r1 = x - x1.astype(F32)
    x2 = r1.astype(BF16)
    x3 = (r1 - x2.astype(F32)).astype(BF16)
    return (jnp.dot(x1, onehot, preferred_element_type=F32) + jnp.dot(x2, onehot, preferred_element_type=F32)
            + jnp.dot(x3, onehot, preferred_element_type=F32))


def _rope_kernel(pos_ref, freq_ref, c_ref, s_ref):
    half = QK_ROPE // 2
    per_row = LANES // half
    rows = TOKENS // per_row
    ang = pos_ref[...].astype(F32) * freq_ref[...]
    cos = jnp.cos(ang)
    sin = jnp.sin(ang)
    src = lax.broadcasted_iota(jnp.int32, (LANES, LANES), 0)
    dst = lax.broadcasted_iota(jnp.int32, (LANES, LANES), 1)
    lane = lax.broadcasted_iota(jnp.int32, (1, LANES), 1)
    ones_nope = jnp.where(lane < QK_NOPE, 1.0, 0.0)
    sign = jnp.where(lane < QK_NOPE + half, -1.0, 1.0)
    for j in range(per_row):
        f = src - half * j
        hit = (dst == QK_NOPE + f) | (dst == QK_NOPE + half + f)
        onehot = jnp.where((f >= 0) & (f < half) & hit, 1.0, 0.0).astype(BF16)
        c_ref[pl.ds(j, rows, stride=per_row), :] = _place(cos, onehot) + ones_nope
        s_ref[pl.ds(j, rows, stride=per_row), :] = _place(sin, onehot) * sign


def _rope_tables(positions):
    half = QK_ROPE // 2
    inv_freq = ROPE_THETA ** (-jnp.arange(0, QK_ROPE, 2, dtype=F32) / QK_ROPE)
    rows = TOKENS * half // LANES
    pos_rep = jnp.broadcast_to(positions.reshape(TOKENS, 1), (TOKENS, half)).reshape(rows, LANES)
    freq = jnp.tile(inv_freq, LANES // half).reshape(1, LANES)
    return pl.pallas_call(
        _rope_kernel,
        out_shape=(jax.ShapeDtypeStruct((TOKENS, HEAD_PAD), F32),) * 2,
        compiler_params=pltpu.CompilerParams(vmem_limit_bytes=48 * 1024 * 1024),
        name="rope_tables",
    )(pos_rep, freq)


def _inproj_kernel(x_ref, g_ref, w_ref, o_ref, h_ref):
    @pl.when(pl.program_id(1) == 0)
    def _():
        h_ref[...] = _rms(x_ref[...], g_ref[...]).astype(BF16)

    o_ref[...] = jnp.dot(h_ref[...], w_ref[...], preferred_element_type=F32).astype(BF16)


def _inproj(x, g, w):
    tm, tn = TM_INPROJ, TN_INPROJ
    return pl.pallas_call(
        _inproj_kernel,
        grid=(TOKENS // tm, PROJ_COLS // tn),
        in_specs=[
            pl.BlockSpec((tm, D_MODEL), lambda i, j: (i, 0)),
            pl.BlockSpec((1, D_MODEL), lambda i, j: (0, 0)),
            pl.BlockSpec((D_MODEL, tn), lambda i, j: (0, j)),
        ],
        out_specs=pl.BlockSpec((tm, tn), lambda i, j: (i, j)),
        out_shape=jax.ShapeDtypeStruct((TOKENS, PROJ_COLS), BF16),
        scratch_shapes=[pltpu.VMEM((tm, D_MODEL), BF16)],
        compiler_params=_cparams(("parallel", "arbitrary"), 48),
        name="in_proj",
    )(x, g, w)


def _qkv_kernel(ql_ref, kvl_ref, kr_ref, krsw_ref, c_ref, s_ref, qg_ref, kvg_ref, wq_ref, wkv_ref,
                q_out, k_out, v_out):
    c = c_ref[...]
    s = s_ref[...]
    width = N_HEADS * HEAD_PAD
    half = QK_ROPE // 2
    low_half = lax.broadcasted_iota(jnp.int32, c.shape, 1) < QK_NOPE + half

    def swap_halves(x):
        return jnp.where(low_half, pltpu.roll(x, HEAD_PAD - half, axis=1), pltpu.roll(x, half, axis=1))

    qn = _rms(ql_ref[...].astype(F32), qg_ref[...]).astype(BF16)
    qq = jnp.dot(qn, wq_ref[...], preferred_element_type=F32)
    for h in range(N_HEADS):
        lo, hi = h * HEAD_PAD, (h + 1) * HEAD_PAD
        q_out[:, lo:hi] = (qq[:, lo:hi] * c + swap_halves(qq[:, lo:hi]) * s).astype(BF16)
    kvn = _rms(kvl_ref[...].astype(F32), kvg_ref[...]).astype(BF16)
    kk = jnp.dot(kvn, wkv_ref[...], preferred_element_type=F32)
    kr = kr_ref[...].astype(F32) * c + krsw_ref[...].astype(F32) * s
    for h in range(N_HEADS):
        lo, hi = h * HEAD_PAD, (h + 1) * HEAD_PAD
        k_out[:, lo:hi] = (kk[:, lo:hi] + kr).astype(BF16)
    v_out[...] = kk[:, width:].astype(BF16)


def _qkv(proj, c_tab, s_tab, qg, kvg, wq, wkv):
    tm = TM_QKV
    width = N_HEADS * HEAD_PAD
    row = lambda blk: (lambda i: (i, blk))
    const = lambda i: (0, 0)
    return pl.pallas_call(
        _qkv_kernel,
        grid=(TOKENS // tm,),
        in_specs=[
            pl.BlockSpec((tm, Q_LORA), row(COL_QLAT // Q_LORA)),
            pl.BlockSpec((tm, KV_LORA), row(COL_KVLAT // KV_LORA)),
            pl.BlockSpec((tm, HEAD_PAD), row(COL_KR // HEAD_PAD)),
            pl.BlockSpec((tm, HEAD_PAD), row(COL_KRSW // HEAD_PAD)),
            pl.BlockSpec((tm, HEAD_PAD), row(0)),
            pl.BlockSpec((tm, HEAD_PAD), row(0)),
            pl.BlockSpec((1, Q_LORA), const),
            pl.BlockSpec((1, KV_LORA), const),
            pl.BlockSpec((Q_LORA, width), const),
            pl.BlockSpec((KV_LORA, 2 * width), const),
        ],
        out_specs=[pl.BlockSpec((tm, width), row(0))] * 3,
        out_shape=(jax.ShapeDtypeStruct((TOKENS, width), BF16),) * 3,
        compiler_params=_cparams(("parallel",), 48),
        name="qkv_prep",
    )(proj, proj, proj, proj, c_tab, s_tab, qg, kvg, wq, wkv)


def _attn_kernel(q_ref, k_ref, v_ref, o_ref, *state):
    t = T_ATTN
    nh = HEADS_PER_STEP
    qi = pl.program_id(2)
    nt = (((1,), (1,)), ((), ()))
    m_sc, l_sc, acc_sc = state[0:nh], state[nh:2 * nh], state[2 * nh:3 * nh]
    for h in range(nh):
        m_sc[h][...] = jnp.full((t, LANES), -jnp.inf, F32)
        l_sc[h][...] = jnp.zeros((t, LANES), F32)
        acc_sc[h][...] = jnp.zeros((t, LANES), F32)

    def block(j, r0, nr, c0, nc, masked):
        start = pl.multiple_of(j * t + c0, nc)
        rows = slice(r0, r0 + nr)
        for h in range(nh):
            lo, hi = h * HEAD_PAD, (h + 1) * HEAD_PAD
            s = lax.dot_general(q_ref[rows, lo:hi], k_ref[pl.ds(start, nc), lo:hi], nt, preferred_element_type=F32)
            if masked:
                row_id = r0 + lax.broadcasted_iota(jnp.int32, (nr, nc), 0)
                col_id = c0 + lax.broadcasted_iota(jnp.int32, (nr, nc), 1)
                s = jnp.where(row_id >= col_id, s, -jnp.inf)
            blocks = [s[:, c * LANES:(c + 1) * LANES] for c in range(nc // LANES)]
            bmax = functools.reduce(jnp.maximum, blocks)
            m_old = m_sc[h][rows, :]
            m_new = jnp.maximum(m_old, jnp.max(bmax, axis=-1, keepdims=True))
            alpha = jnp.exp2(m_old - m_new)
            ps = [jnp.exp2(b - m_new) for b in blocks]
            p = jnp.concatenate(ps, axis=1).astype(BF16)
            l_sc[h][rows, :] = alpha * l_sc[h][rows, :] + functools.reduce(jnp.add, ps)
            acc_sc[h][rows, :] = alpha * acc_sc[h][rows, :] + jnp.dot(p, v_ref[pl.ds(start, nc), lo:hi],
                                                                       preferred_element_type=F32)
            m_sc[h][rows, :] = m_new

    def body(j, carry):
        block(j, 0, t, 0, t, False)
        return carry

    lax.fori_loop(0, qi, body, 0)
    half = t // 2
    block(qi, 0, t, 0, half, True)
    block(qi, half, half, half, half, True)
    out = [acc_sc[h][...] / jnp.sum(l_sc[h][...], axis=-1, keepdims=True) for h in range(nh)]
    o_ref[...] = jnp.concatenate([out[h] + out[h + 1] for h in range(0, nh, 2)], axis=1).astype(BF16)


def _attention(q, k, v):
    t = T_ATTN
    nq = SEQ // t
    nh = HEADS_PER_STEP
    return pl.pallas_call(
        _attn_kernel,
        grid=(BATCH, N_HEADS // nh, nq),
        in_specs=[
            pl.BlockSpec((t, nh * HEAD_PAD), lambda b, hg, i: (b * nq + i, hg)),
            pl.BlockSpec((SEQ, nh * HEAD_PAD), lambda b, hg, i: (b, hg)),
            pl.BlockSpec((SEQ, nh * HEAD_PAD), lambda b, hg, i: (b, hg)),
        ],
        out_specs=pl.BlockSpec((t, nh * V_HEAD), lambda b, hg, i: (b * nq + i, hg)),
        out_shape=jax.ShapeDtypeStruct((TOKENS, N_HEADS * V_HEAD), BF16),
        scratch_shapes=[pltpu.VMEM((t, LANES), F32)] * (3 * nh),
        compiler_params=_cparams(("parallel", "parallel", "arbitrary"), 48),
        name="mla_attention",
    )(q, k, v)


def _mixer_tail_kernel(gates_ref, sc_ref, sch_ref, gv_ref, gg_ref, gvh_ref, ggh_ref, att_ref, x_ref,
                       cwa_ref, woa_ref, cwc_ref, lng_ref, lnb_ref, woc_ref, wob_ref, wo_ref,
                       out_ref, cbuf, ubuf, shifted, vbuf):
    tm = TM_MIX
    has_past = (pl.program_id(0) % (SEQ // tm)) != 0

    sc = sc_ref[...]
    sc_b = sc[:, 0:SC_WIDTH].astype(F32)
    cbuf[8:8 + tm, :] = sc[:, SC_WIDTH:2 * SC_WIDTH].astype(F32) * sc[:, 2 * SC_WIDTH:].astype(F32)
    sch = sch_ref[...].astype(F32)[HALO_A - 8:HALO_A, :]
    cbuf[0:8, :] = jnp.where(has_past, sch[:, SC_WIDTH:2 * SC_WIDTH] * sch[:, 2 * SC_WIDTH:], 0.0)
    conv_a = cwa_ref[0:1, :] * cbuf[6:6 + tm, :]
    for t in range(1, SC_KERNEL):
        conv_a = conv_a + cwa_ref[t:t + 1, :] * cbuf[6 + t:6 + t + tm, :]
    y_a = jnp.dot((sc_b * conv_a).astype(BF16), woa_ref[...], preferred_element_type=F32)

    ubuf[HALO_C:HALO_C + tm, :] = gv_ref[...].astype(F32) * _sigmoid(gg_ref[...].astype(F32))
    ubuf[0:HALO_C, :] = jnp.where(has_past, gvh_ref[...].astype(F32) * _sigmoid(ggh_ref[...].astype(F32)), 0.0)
    rows = tm + HALO_C - 8
    for b in range(1, 8):
        shifted[b - 1, 0:rows, :] = ubuf[b:b + rows, :]
    base = HALO_C - (CONF_KERNEL - 1)
    for r0 in range(0, tm, CONV_CHUNK):
        acc = None
        for t in range(CONF_KERNEL):
            off = base + t
            a0 = r0 + off - off % 8
            src = ubuf[a0:a0 + CONV_CHUNK, :] if off % 8 == 0 else shifted[off % 8 - 1, a0:a0 + CONV_CHUNK, :]
            term = cwc_ref[t:t + 1, :] * src
            acc = term if acc is None else acc + term
        mu = jnp.mean(acc, axis=-1, keepdims=True)
        xc = acc - mu
        var = jnp.mean(xc * xc, axis=-1, keepdims=True)
        y = xc * lax.rsqrt(var + EPS) * lng_ref[...] + lnb_ref[...]
        vbuf[r0:r0 + CONV_CHUNK, :] = (y * _sigmoid(y)).astype(BF16)
    y_c = jnp.dot(vbuf[...], woc_ref[...], preferred_element_type=F32)

    y_b = jnp.dot(att_ref[...], wob_ref[...], preferred_element_type=F32)

    g = gates_ref[...]
    merged = (_sigmoid(g[:, 0:D_MODEL].astype(F32)) * y_a
              + _sigmoid(g[:, D_MODEL:2 * D_MODEL].astype(F32)) * y_b
              + _sigmoid(g[:, 2 * D_MODEL:].astype(F32)) * y_c)
    out_ref[...] = x_ref[...] + jnp.dot(merged.astype(BF16), wo_ref[...], preferred_element_type=F32)


def _mixer_tail(proj, att, x, cwa, woa, cwc, lng, lnb, woc, wob, wo):
    tm = TM_MIX
    row = lambda width, col: pl.BlockSpec((tm, width), lambda i: (i, col // width))
    halo = lambda rows, width, col: pl.BlockSpec(
        (rows, width), lambda i: (jnp.maximum(i * (tm // rows) - 1, 0), col // width))
    const = lambda a: pl.BlockSpec(a.shape, lambda i: (0,) * a.ndim)
    weights = (cwa, woa, cwc, lng, lnb, woc, wob, wo)
    return pl.pallas_call(
        _mixer_tail_kernel,
        grid=(TOKENS // tm,),
        in_specs=[
            row(3 * D_MODEL, COL_GATES),
            row(3 * SC_WIDTH, COL_SC),
            halo(HALO_A, 3 * SC_WIDTH, COL_SC),
            row(CONF_WIDTH, COL_GLU),
            row(CONF_WIDTH, COL_GLU + CONF_WIDTH),
            halo(HALO_C, CONF_WIDTH, COL_GLU),
            halo(HALO_C, CONF_WIDTH, COL_GLU + CONF_WIDTH),
            pl.BlockSpec((tm, N_HEADS * V_HEAD), lambda i: (i, 0)),
            pl.BlockSpec((tm, D_MODEL), lambda i: (i, 0)),
        ] + [const(a) for a in weights],
        out_specs=pl.BlockSpec((tm, D_MODEL), lambda i: (i, 0)),
        out_shape=jax.ShapeDtypeStruct((TOKENS, D_MODEL), F32),
        scratch_shapes=[
            pltpu.VMEM((tm + 8, SC_WIDTH), F32),
            pltpu.VMEM((tm + HALO_C, CONF_WIDTH), F32),
            pltpu.VMEM((7, tm + HALO_C - 8, CONF_WIDTH), F32),
            pltpu.VMEM((tm, CONF_WIDTH), BF16),
        ],
        compiler_params=_cparams(("parallel",), 56),
        name="mixer_tail",
    )(proj, proj, proj, proj, proj, proj, proj, att, x, *weights)


def _route(logits):
    lane = lax.broadcasted_iota(jnp.int32, logits.shape, 1)
    lane_f = lane.astype(F32)
    neg = -jnp.inf
    big = float(LANES)
    is_grp = (lane >= ROUTER_GROUP_LANE) & (lane < ROUTER_GROUP_LANE + N_GROUPS)
    glog = jnp.where(is_grp, logits, neg)
    gmax = jnp.max(glog, axis=-1, keepdims=True)
    gidx = jnp.min(jnp.where(glog == gmax, lane_f, big), axis=-1, keepdims=True)
    p_sel = 1.0 / jnp.sum(jnp.exp(glog - gmax), axis=-1, keepdims=True)
    first = (gidx - ROUTER_GROUP_LANE) * EXPERTS_PER_GROUP
    in_grp = (lane_f >= first) & (lane_f < first + EXPERTS_PER_GROUP)
    el = jnp.where(in_grp, logits, neg)
    m1 = jnp.max(el, axis=-1, keepdims=True)
    i1 = jnp.min(jnp.where(el == m1, lane_f, big), axis=-1, keepdims=True)
    el2 = jnp.where(lane_f == i1, neg, el)
    m2 = jnp.max(el2, axis=-1, keepdims=True)
    i2 = jnp.min(jnp.where(el2 == m2, lane_f, big), axis=-1, keepdims=True)
    e2 = jnp.exp(m2 - m1)
    w1 = p_sel / (1.0 + e2)
    w2 = w1 * e2
    return i1, i2, w1, w2


def _router_kernel(x_ref, g_ref, wrh_ref, wrl_ref, br_ref, meta_ref, meta_t_ref, cnt_ref, xs0_ref, run_ref):
    tm = TM_ROUTE

    @pl.when(pl.program_id(0) == 0)
    def _():
        run_ref[...] = jnp.zeros_like(run_ref)

    h = _rms(x_ref[...], g_ref[...])
    h_hi = h.astype(BF16)
    h_lo = (h - h_hi.astype(F32)).astype(BF16)
    logits = (jnp.dot(h_hi, wrh_ref[...], preferred_element_type=F32)
              + jnp.dot(h_hi, wrl_ref[...], preferred_element_type=F32)
              + jnp.dot(h_lo, wrh_ref[...], preferred_element_type=F32)
              + br_ref[...])
    i1, i2, w1, w2 = _route(logits)
    lane = lax.broadcasted_iota(jnp.int32, (tm, LANES), 1)
    lane_f = lane.astype(F32)
    oh1 = lane_f == i1
    oh2 = lane_f == i2
    onehot = jnp.where(oh1, 1.0, 0.0) + jnp.where(oh2, 1.0, 0.0)
    row_id = lax.broadcasted_iota(jnp.int32, (tm, tm), 0)
    col_id = lax.broadcasted_iota(jnp.int32, (tm, tm), 1)
    below = jnp.where(row_id > col_id, 1.0, 0.0).astype(BF16)
    before = run_ref[...] + jnp.dot(below, onehot.astype(BF16), preferred_element_type=F32)
    r1 = jnp.sum(jnp.where(oh1, before, 0.0), axis=-1, keepdims=True)
    r2 = jnp.sum(jnp.where(oh2, before, 0.0), axis=-1, keepdims=True)
    run_ref[...] += jnp.sum(onehot, axis=0, keepdims=True)
    cnt_ref[...] = run_ref[...]
    meta = jnp.zeros((tm, LANES), F32)
    for col, val in enumerate((i1, i2, r1, r2, w1, w2)):
        meta = jnp.where(lane == col, val, meta)
    meta_ref[...] = meta
    meta_t_ref[...] = meta.T[0:META_ROWS, :]
    xs0_ref[...] = jnp.zeros_like(xs0_ref)


def _router(x, g, wrh, wrl, br):
    tm = TM_ROUTE
    const = lambda i: (0, 0)
    return pl.pallas_call(
        _router_kernel,
        grid=(TOKENS // tm,),
        in_specs=[
            pl.BlockSpec((tm, D_MODEL), lambda i: (i, 0)),
            pl.BlockSpec((1, D_MODEL), const),
            pl.BlockSpec((D_MODEL, LANES), const),
            pl.BlockSpec((D_MODEL, LANES), const),
            pl.BlockSpec((1, LANES), const),
        ],
        out_specs=[
            pl.BlockSpec((tm, LANES), lambda i: (i, 0)),
            pl.BlockSpec((META_ROWS, tm), lambda i: (0, i)),
            pl.BlockSpec((1, LANES), const),
            pl.BlockSpec((SORTED_ROWS // (TOKENS // tm), LANES), lambda i: (i, 0)),
        ],
        out_shape=(
            jax.ShapeDtypeStruct((TOKENS, LANES), F32),
            jax.ShapeDtypeStruct((META_ROWS, TOKENS), F32),
            jax.ShapeDtypeStruct((1, LANES), F32),
            jax.ShapeDtypeStruct((SORTED_ROWS, LANES), F32),
        ),
        scratch_shapes=[pltpu.VMEM((1, LANES), F32)],
        compiler_params=_cparams(("arbitrary",), 48),
        name="moe_router",
    )(x, g, wrh, wrl, br)


def _dispatch_plan(meta_t, counts):
    e1 = meta_t[META_E1].astype(jnp.int32)
    e2 = meta_t[META_E2].astype(jnp.int32)
    r1 = meta_t[META_R1].astype(jnp.int32)
    r2 = meta_t[META_R2].astype(jnp.int32)
    cnt = counts[0, :N_EXPERTS].astype(jnp.int32)
    tiles = (cnt + TM_EXP - 1) // TM_EXP
    tile_end = jnp.cumsum(tiles)
    first_slot = ((tile_end - tiles) * TM_EXP)[:, None]
    expert = jnp.arange(N_EXPERTS, dtype=jnp.int32)[:, None]
    pos1 = jnp.sum(jnp.where(e1[None, :] == expert, first_slot, 0), axis=0) + r1
    pos2 = jnp.sum(jnp.where(e2[None, :] == expert, first_slot, 0), axis=0) + r2
    n_tiles = tile_end[-1:]
    tile_id = jnp.minimum(jnp.arange(MAX_TILES, dtype=jnp.int32), n_tiles - 1)
    tile_expert = jnp.sum((tile_id[:, None] >= tile_end[None, :]).astype(jnp.int32), axis=1)
    return pos1, pos2, tile_expert, n_tiles


def _dispatch_kernel(pos1_ref, pos2_ref, x_ref, g_ref, xs_in_ref, xs_ref, slab, sem):
    del xs_in_ref
    tm = TM_DISP
    i = pl.program_id(0)
    last = pl.num_programs(0) - 1
    slot = i % 2

    def wait_copies(sl):
        for _ in range(TOP_K):
            pltpu.make_async_copy(slab.at[sl], xs_ref.at[pl.ds(0, tm * SLAB), :], sem.at[sl]).wait()

    @pl.when(i >= 2)
    def _():
        wait_copies(slot)

    h = _rms(x_ref[...], g_ref[...])
    for s in range(SLAB):
        slab[slot, pl.ds(s, tm, stride=SLAB), :] = h[:, s * LANES:(s + 1) * LANES]

    def body(c, carry):
        for u in range(GATHER_UNROLL):
            r = c * GATHER_UNROLL + u
            src = slab.at[slot, pl.ds(pl.multiple_of(r * SLAB, SLAB), SLAB), :]
            for k, pos_ref in enumerate((pos1_ref, pos2_ref)):
                p = pos_ref[i * tm + r]
                pltpu.make_async_copy(src, xs_ref.at[pl.ds(pl.multiple_of(p * SLAB, SLAB), SLAB), :],
                                      sem.at[slot]).start(priority=k % N_DMA_PRIORITIES)
        return carry

    lax.fori_loop(0, tm // GATHER_UNROLL, body, 0)

    @pl.when(i == last)
    def _():
        wait_copies(1 - slot)
        wait_copies(slot)


def _dispatch(pos1, pos2, x, g, xs0):
    tm = TM_DISP
    return pl.pallas_call(
        _dispatch_kernel,
        grid_spec=pltpu.PrefetchScalarGridSpec(
            num_scalar_prefetch=2,
            grid=(TOKENS // tm,),
            in_specs=[
                pl.BlockSpec((tm, D_MODEL), lambda i, p1, p2: (i, 0)),
                pl.BlockSpec((1, D_MODEL), lambda i, p1, p2: (0, 0)),
                pl.BlockSpec(memory_space=pl.ANY),
            ],
            out_specs=pl.BlockSpec(memory_space=pl.ANY),
            scratch_shapes=[pltpu.VMEM((2, tm * SLAB, LANES), F32), pltpu.SemaphoreType.DMA((2,))],
        ),
        out_shape=jax.ShapeDtypeStruct((SORTED_ROWS, LANES), F32),
        input_output_aliases={4: 0},
        compiler_params=_cparams(("arbitrary",), 48),
        name="moe_dispatch",
    )(pos1, pos2, x, g, xs0)


def _expert_kernel(te_ref, nt_ref, xs_ref, wg_ref, wu_ref, wd_ref, ys_ref, xbuf, sem):
    del te_ref
    tm = TM_EXP
    i = pl.program_id(0)
    n = nt_ref[0]

    def tile_copy(tile):
        slot = tile % EXPERT_SLOTS
        rows = pl.ds(pl.multiple_of(tile * (tm * SLAB), tm * SLAB), tm * SLAB)
        return pltpu.make_async_copy(xs_ref.at[rows, :], xbuf.at[slot], sem.at[slot])

    @pl.when(i == 0)
    def _():
        for ahead in range(EXPERT_PREFETCH):
            @pl.when(ahead < n)
            def _():
                tile_copy(ahead).start()

    @pl.when(i + EXPERT_PREFETCH < n)
    def _():
        tile_copy(i + EXPERT_PREFETCH).start()

    @pl.when(i < n)
    def _():
        tile_copy(i).wait()
        slot = i % EXPERT_SLOTS
        xt = jnp.concatenate([xbuf[slot, pl.ds(s, tm, stride=SLAB), :].astype(BF16) for s in range(SLAB)], axis=1)
        hg = jnp.dot(xt, wg_ref[...].astype(BF16), preferred_element_type=F32)
        hu = jnp.dot(xt, wu_ref[...].astype(BF16), preferred_element_type=F32)
        hh = (hg * _sigmoid(hg) * hu).astype(BF16)
        y = jnp.dot(hh, wd_ref[...].astype(BF16), preferred_element_type=F32)
        for s in range(SLAB):
            ys_ref[pl.ds(s, tm, stride=SLAB), :] = y[:, s * LANES:(s + 1) * LANES]

    @pl.when(i >= n)
    def _():
        ys_ref[...] = jnp.zeros_like(ys_ref)


def _experts(layer, tile_expert, n_tiles, xs, w_gate, w_up, w_down):
    tm = TM_EXP
    wspec = lambda rows, cols: pl.BlockSpec((None, None, rows, cols), lambda i, te, nt: (layer, te[i], 0, 0))
    return pl.pallas_call(
        _expert_kernel,
        grid_spec=pltpu.PrefetchScalarGridSpec(
            num_scalar_prefetch=2,
            grid=(MAX_TILES,),
            in_specs=[
                pl.BlockSpec(memory_space=pl.ANY),
                wspec(D_MODEL, EXPERT_HIDDEN),
                wspec(D_MODEL, EXPERT_HIDDEN),
                wspec(EXPERT_HIDDEN, D_MODEL),
            ],
            out_specs=pl.BlockSpec((tm * SLAB, LANES), lambda i, te, nt: (i, 0)),
            scratch_shapes=[pltpu.VMEM((EXPERT_SLOTS, tm * SLAB, LANES), F32), pltpu.SemaphoreType.DMA((EXPERT_SLOTS,))],
        ),
        out_shape=jax.ShapeDtypeStruct((SORTED_ROWS, LANES), F32),
        compiler_params=_cparams(("arbitrary",), 48),
        name="moe_experts",
    )(tile_expert, n_tiles, xs, w_gate, w_up, w_down)


def _start_slab_gathers(idx_refs, base, n_rows, src_hbm, dst_bufs, sem):
    def body(c, carry):
        for u in range(GATHER_UNROLL):
            r = c * GATHER_UNROLL + u
            for k, (idx_ref, dst) in enumerate(zip(idx_refs, dst_bufs)):
                t = idx_ref[base + r]
                pltpu.make_async_copy(src_hbm.at[pl.ds(pl.multiple_of(t * SLAB, SLAB), SLAB), :],
                                      dst.at[pl.ds(pl.multiple_of(r * SLAB, SLAB), SLAB), :],
                                      sem).start(priority=k % N_DMA_PRIORITIES)
        return carry

    lax.fori_loop(0, n_rows // GATHER_UNROLL, body, 0)


def _wait_slab_gathers(n_rows, src_hbm, dst, sem):
    pltpu.make_async_copy(src_hbm.at[pl.ds(0, n_rows * SLAB), :], dst, sem).wait()


def _combine_ple_kernel(pos1_ref, pos2_ref, x_ref, meta_ref, ys_ref, p_ref, g_ref, wg_ref, wp_ref, fg_ref,
                        out_ref, cbuf, sem, *, final):
    tm = TM_COMB
    i = pl.program_id(0)

    def gather(tile, slot):
        _start_slab_gathers((pos1_ref, pos2_ref), tile * tm, tm, ys_ref, (cbuf.at[slot, 0], cbuf.at[slot, 1]),
                            sem.at[slot])

    @pl.when(i == 0)
    def _():
        gather(0, 0)

    @pl.when(i + 1 < pl.num_programs(0))
    def _():
        gather(i + 1, (i + 1) % 2)

    slot = i % 2
    for k in range(2):
        _wait_slab_gathers(tm, ys_ref, cbuf.at[slot, k], sem.at[slot])
    meta = meta_ref[...]
    w1 = meta[:, META_W1:META_W1 + 1]
    w2 = meta[:, META_W2:META_W2 + 1]
    moe = jnp.concatenate([w1 * cbuf[slot, 0, pl.ds(s, tm, stride=SLAB), :]
                           + w2 * cbuf[slot, 1, pl.ds(s, tm, stride=SLAB), :] for s in range(SLAB)], axis=1)
    x = x_ref[...] + moe
    h = _rms(x, g_ref[...]).astype(BF16)
    gate = _sigmoid(jnp.dot(h, wg_ref[...], preferred_element_type=F32))
    emb = jnp.dot(p_ref[...].astype(BF16), wp_ref[...], preferred_element_type=F32)
    y = x + gate * emb
    if final:
        y = _rms(y, fg_ref[...])
    out_ref[...] = y


def _combine_ple(layer, pos1, pos2, x, meta, ys, p, g, wg, wp, fg, final):
    tm = TM_COMB
    const = lambda i, p1, p2: (0, 0)
    rows = lambda i, p1, p2: (i, 0)
    return pl.pallas_call(
        functools.partial(_combine_ple_kernel, final=final),
        grid_spec=pltpu.PrefetchScalarGridSpec(
            num_scalar_prefetch=2,
            grid=(TOKENS // tm,),
            in_specs=[
                pl.BlockSpec((tm, D_MODEL), rows),
                pl.BlockSpec((tm, LANES), rows),
                pl.BlockSpec(memory_space=pl.ANY),
                pl.BlockSpec((None, tm, PLE_DIM), lambda i, p1, p2: (layer, i, 0)),
                pl.BlockSpec((1, D_MODEL), const),
                pl.BlockSpec((D_MODEL, D_MODEL), const),
                pl.BlockSpec((PLE_DIM, D_MODEL), const),
                pl.BlockSpec((1, D_MODEL), const),
            ],
            out_specs=pl.BlockSpec((tm, D_MODEL), rows),
            scratch_shapes=[pltpu.VMEM((2, 2, tm * SLAB, LANES), F32), pltpu.SemaphoreType.DMA((2,))],
        ),
        out_shape=jax.ShapeDtypeStruct((TOKENS, D_MODEL), F32),
        compiler_params=_cparams(("arbitrary",), 48),
        name="moe_combine_ple",
    )(pos1, pos2, x, meta, ys, p, g, wg, wp, fg)


def _in_proj_weight(w_in):
    o = 3 * D_MODEL + 3 * SC_WIDTH
    q_lat = w_in[:, o:o + Q_LORA]
    o += Q_LORA
    kv_lat = w_in[:, o:o + KV_LORA]
    o += KV_LORA
    k_rope = w_in[:, o:o + QK_ROPE]
    o += QK_ROPE
    glu = w_in[:, o:]
    half = QK_ROPE // 2
    zn = jnp.zeros((D_MODEL, QK_NOPE), F32)
    zp = jnp.zeros((D_MODEL, HEAD_PAD - QK_NOPE - QK_ROPE), F32)
    kr = jnp.concatenate([zn, k_rope, zp], axis=1)
    krsw = jnp.concatenate([zn, k_rope[:, half:], k_rope[:, :half], zp], axis=1)
    w = jnp.concatenate([w_in[:, :3 * D_MODEL + 3 * SC_WIDTH], q_lat, kv_lat, glu, kr, krsw], axis=1)
    return w.astype(BF16)


def _q_weight(w_uq):
    scale = (QK_NOPE + QK_ROPE) ** -0.5 * LOG2_E
    w = (w_uq * scale).reshape(Q_LORA, N_HEADS, QK_NOPE + QK_ROPE)
    zp = jnp.zeros((Q_LORA, N_HEADS, HEAD_PAD - QK_NOPE - QK_ROPE), F32)
    return jnp.concatenate([w, zp], axis=2).reshape(Q_LORA, N_HEADS * HEAD_PAD).astype(BF16)


def _kv_weight(w_ukv):
    w = w_ukv.reshape(KV_LORA, N_HEADS, QK_NOPE + V_HEAD)
    k_nope, v = w[:, :, :QK_NOPE], w[:, :, QK_NOPE:]
    z = jnp.zeros_like(v)
    k_part = jnp.concatenate([k_nope, jnp.zeros_like(k_nope)], axis=2).reshape(KV_LORA, N_HEADS * HEAD_PAD)
    odd = (jnp.arange(N_HEADS) % 2 == 1)[None, :, None]
    v_part = jnp.concatenate([jnp.where(odd, z, v), jnp.where(odd, v, z)], axis=2).reshape(KV_LORA, N_HEADS * HEAD_PAD)
    return jnp.concatenate([k_part, v_part], axis=1).astype(BF16)


def _router_weight(w_rg, b_rg, w_re, b_re):
    pad = LANES - N_EXPERTS - N_GROUPS
    w = jnp.concatenate([w_re, w_rg, jnp.zeros((D_MODEL, pad), F32)], axis=1)
    b = jnp.concatenate([b_re, b_rg, jnp.zeros((pad,), F32)]).reshape(1, LANES)
    w_hi = w.astype(BF16)
    w_lo = (w - w_hi.astype(F32)).astype(BF16)
    return w_hi, w_lo, b


def kernel(x, p, positions, ln_mix_g, w_in, conv_a_w, w_out_a, q_norm_g, w_uq, kv_norm_g, w_ukv, w_out_b, conv_c_w, ln_c_g, ln_c_b, w_out_c, w_o, ln_ffn_g, w_route_grp, b_route_grp, w_route_exp, b_route_exp, w_exp_gate, w_exp_up, w_exp_down, ln_ple_g, w_ple_gate, w_ple, final_norm_g):
    c_tab, s_tab = _rope_tables(positions)
    xf = x.reshape(TOKENS, D_MODEL)
    pf = p.reshape(DEPTH, TOKENS, PLE_DIM)
    row = lambda a: a.reshape(1, -1)
    for i in range(DEPTH):
        proj = _inproj(xf, row(ln_mix_g[i]), _in_proj_weight(w_in[i]))
        q, k, v = _qkv(proj, c_tab, s_tab, row(q_norm_g[i]), row(kv_norm_g[i]), _q_weight(w_uq[i]), _kv_weight(w_ukv[i]))
        att = _attention(q, k, v)
        xf = _mixer_tail(proj, att, xf, conv_a_w[i], w_out_a[i].astype(BF16), conv_c_w[i], row(ln_c_g[i]),
                         row(ln_c_b[i]), w_out_c[i].astype(BF16), w_out_b[i].astype(BF16), w_o[i].astype(BF16))
        wr_hi, wr_lo, br = _router_weight(w_route_grp[i], b_route_grp[i], w_route_exp[i], b_route_exp[i])
        meta, meta_t, counts, xs0 = _router(xf, row(ln_ffn_g[i]), wr_hi, wr_lo, br)
        pos1, pos2, tile_expert, n_tiles = _dispatch_plan(meta_t, counts)
        xs = _dispatch(pos1, pos2, xf, row(ln_ffn_g[i]), xs0)
        ys = _experts(i, tile_expert, n_tiles, xs, w_exp_gate, w_exp_up, w_exp_down)
        xf = _combine_ple(i, pos1, pos2, xf, meta, ys, pf, row(ln_ple_g[i]), w_ple_gate[i].astype(BF16),
                          w_ple[i].astype(BF16), row(final_norm_g), final=(i == DEPTH - 1))
    return xf.reshape(BATCH, SEQ, D_MODEL)
```

```python
import functools

import jax
import jax.numpy as jnp
from jax import lax
from jax.experimental import pallas as pl
from jax.experimental.pallas import tpu as pltpu

D_MODEL = 1024
BATCH = 8
SEQ = 2048
DEPTH = 2
TOKENS = BATCH * SEQ
PLE_DIM = 256
SC_WIDTH = 512
SC_KERNEL = 3
N_HEADS = 8
QK_NOPE = 64
QK_ROPE = 32
V_HEAD = 64
Q_LORA = 768
KV_LORA = 256
ROPE_THETA = 10000.0
CONF_WIDTH = 512
CONF_KERNEL = 31
N_GROUPS = 4
EXPERTS_PER_GROUP = 8
N_EXPERTS = N_GROUPS * EXPERTS_PER_GROUP
EXPERT_HIDDEN = 256
EPS = 1e-6
LOG2_E = 1.4426950408889634

LANES = 128
HEAD_PAD = 128
F32 = jnp.float32
BF16 = jnp.bfloat16

COL_GATES = 0
COL_SC = 3 * D_MODEL
COL_QLAT = COL_SC + 3 * SC_WIDTH
COL_KVLAT = COL_QLAT + Q_LORA
MAIN_COLS = COL_KVLAT + KV_LORA
TCOL_GLU = 0
TCOL_KR = 2 * CONF_WIDTH
TCOL_KRSW = TCOL_KR + HEAD_PAD
TAIL_COLS = TCOL_KRSW + HEAD_PAD

ROUTER_GROUP_LANE = N_EXPERTS

TM_INPROJ = 1024
TN_INPROJ = MAIN_COLS // 2
TN_SPLIT = 1536
TM_QKV = 512
T_ATTN = 512
HEADS_PER_STEP = 4
TM_MIX = 512
CONV_CHUNK = 64
HALO_C = 32
HALO_A = 16
TM_ROUTE = 512
TM_EXP = 256
EXPERT_PREFETCH = 2
EXPERT_SLOTS = EXPERT_PREFETCH + 1
TM_DISP = 512
TM_COMB = 512
TOP_K = 2
MAX_TILES = TOKENS * TOP_K // TM_EXP + N_EXPERTS
SLAB = D_MODEL // LANES
GATHER_UNROLL = 16
N_DMA_PRIORITIES = 2
META_E1, META_E2, META_R1, META_R2, META_W1, META_W2 = range(6)
META_ROWS = 8
SORTED_ROWS = MAX_TILES * TM_EXP * SLAB


def _cparams(semantics, vmem_mb):
    return pltpu.CompilerParams(dimension_semantics=semantics, vmem_limit_bytes=vmem_mb * 1024 * 1024)


def _sigmoid(x):
    return 1.0 / (1.0 + jnp.exp(-x))


def _rms(x, g):
    return x * lax.rsqrt(jnp.mean(x * x, axis=-1, keepdims=True) + EPS) * g


def _place(x, onehot):
    x1 = x.astype(BF16)
    r1 = x - x1.astype(F32)
    x2 = r1.astype(BF16)
    x3 = (r1 - x2.astype(F32)).astype(BF16)
    return (jnp.dot(x1, onehot, preferred_element_type=F32) + jnp.dot(x2, onehot, preferred_element_type=F32)
            + jnp.dot(x3, onehot, preferred_element_type=F32))


def _rope_kernel(pos_ref, freq_ref, c_ref, s_ref):
    half = QK_ROPE // 2
    per_row = LANES // half
    rows = TOKENS // per_row
    ang = pos_ref[...].astype(F32) * freq_ref[...]
    cos = jnp.cos(ang)
    sin = jnp.sin(ang)
    src = lax.broadcasted_iota(jnp.int32, (LANES, LANES), 0)
    dst = lax.broadcasted_iota(jnp.int32, (LANES, LANES), 1)
    lane = lax.broadcasted_iota(jnp.int32, (1, LANES), 1)
    ones_nope = jnp.where(lane < QK_NOPE, 1.0, 0.0)
    sign = jnp.where(lane < QK_NOPE + half, -1.0, 1.0)
    for j in range(per_row):
        f = src - half * j
        hit = (dst == QK_NOPE + f) | (dst == QK_NOPE + half + f)
        onehot = jnp.where((f >= 0) & (f < half) & hit, 1.0, 0.0).astype(BF16)
        c_ref[pl.ds(j, rows, stride=per_row), :] = _place(cos, onehot) + ones_nope
        s_ref[pl.ds(j, rows, stride=per_row), :] = _place(sin, onehot) * sign


def _rope_tables(positions):
    half = QK_ROPE // 2
    inv_freq = ROPE_THETA ** (-jnp.arange(0, QK_ROPE, 2, dtype=F32) / QK_ROPE)
    rows = TOKENS * half // LANES
    pos_rep = jnp.broadcast_to(positions.reshape(TOKENS, 1), (TOKENS, half)).reshape(rows, LANES)
    freq = jnp.tile(inv_freq, LANES // half).reshape(1, LANES)
    return pl.pallas_call(
        _rope_kernel,
        out_shape=(jax.ShapeDtypeStruct((TOKENS, HEAD_PAD), F32),) * 2,
        compiler_params=pltpu.CompilerParams(vmem_limit_bytes=48 * 1024 * 1024),
        name="rope_tables",
    )(pos_rep, freq)


def _inproj_kernel(x_ref, g_ref, wm_ref, wt_ref, om_ref, ot_ref, h_ref):
    j = pl.program_id(1)
    n_main = MAIN_COLS // TN_INPROJ

    @pl.when(j == 0)
    def _():
        h_ref[...] = _rms(x_ref[...], g_ref[...]).astype(BF16)

    @pl.when(j < n_main)
    def _():
        for lo, hi in ((0, TN_SPLIT), (TN_SPLIT, TN_INPROJ)):
            om_ref[:, lo:hi] = jnp.dot(h_ref[...], wm_ref[:, lo:hi], preferred_element_type=F32).astype(BF16)

    @pl.when(j == n_main)
    def _():
        ot_ref[...] = jnp.dot(h_ref[...], wt_ref[...], preferred_element_type=F32).astype(BF16)


def _inproj(x, g, w_main, w_tail):
    tm, tn = TM_INPROJ, TN_INPROJ
    n_main = MAIN_COLS // tn
    main_col = lambda i, j: jnp.minimum(j, n_main - 1)
    return pl.pallas_call(
        _inproj_kernel,
        grid=(TOKENS // tm, n_main + 1),
        in_specs=[
            pl.BlockSpec((tm, D_MODEL), lambda i, j: (i, 0)),
            pl.BlockSpec((1, D_MODEL), lambda i, j: (0, 0)),
            pl.BlockSpec((D_MODEL, tn), lambda i, j: (0, main_col(i, j))),
            pl.BlockSpec((D_MODEL, TAIL_COLS), lambda i, j: (0, 0)),
        ],
        out_specs=[
            pl.BlockSpec((tm, tn), lambda i, j: (i, main_col(i, j))),
            pl.BlockSpec((tm, TAIL_COLS), lambda i, j: (i, 0)),
        ],
        out_shape=(jax.ShapeDtypeStruct((TOKENS, MAIN_COLS), BF16), jax.ShapeDtypeStruct((TOKENS, TAIL_COLS), BF16)),
        scratch_shapes=[pltpu.VMEM((tm, D_MODEL), BF16)],
        compiler_params=_cparams(("parallel", "arbitrary"), 56),
        name="in_proj",
    )(x, g, w_main, w_tail)


def _qkv_kernel(ql_ref, kvl_ref, kr_ref, krsw_ref, c_ref, s_ref, qg_ref, kvg_ref, wq_ref, wkv_ref,
                q_out, k_out, v_out):
    c = c_ref[...]
    s = s_ref[...]
    width = N_HEADS * HEAD_PAD
    half = QK_ROPE // 2
    low_half = lax.broadcasted_iota(jnp.int32, c.shape, 1) < QK_NOPE + half

    def swap_halves(x):
        return jnp.where(low_half, pltpu.roll(x, HEAD_PAD - half, axis=1), pltpu.roll(x, half, axis=1))

    qn = _rms(ql_ref[...].astype(F32), qg_ref[...]).astype(BF16)
    qq = jnp.dot(qn, wq_ref[...], preferred_element_type=F32)
    for h in range(N_HEADS):
        lo, hi = h * HEAD_PAD, (h + 1) * HEAD_PAD
        q_out[:, lo:hi] = (qq[:, lo:hi] * c + swap_halves(qq[:, lo:hi]) * s).astype(BF16)
    kvn = _rms(kvl_ref[...].astype(F32), kvg_ref[...]).astype(BF16)
    kk = jnp.dot(kvn, wkv_ref[...], preferred_element_type=F32)
    kr = kr_ref[...].astype(F32) * c + krsw_ref[...].astype(F32) * s
    for h in range(N_HEADS):
        lo, hi = h * HEAD_PAD, (h + 1) * HEAD_PAD
        k_out[:, lo:hi] = (kk[:, lo:hi] + kr).astype(BF16)
    v_out[...] = kk[:, width:].astype(BF16)


def _qkv(proj, tail, c_tab, s_tab, qg, kvg, wq, wkv):
    tm = TM_QKV
    width = N_HEADS * HEAD_PAD
    row = lambda blk: (lambda i: (i, blk))
    const = lambda i: (0, 0)
    return pl.pallas_call(
        _qkv_kernel,
        grid=(TOKENS // tm,),
        in_specs=[
            pl.BlockSpec((tm, Q_LORA), row(COL_QLAT // Q_LORA)),
            pl.BlockSpec((tm, KV_LORA), row(COL_KVLAT // KV_LORA)),
            pl.BlockSpec((tm, HEAD_PAD), row(TCOL_KR // HEAD_PAD)),
            pl.BlockSpec((tm, HEAD_PAD), row(TCOL_KRSW // HEAD_PAD)),
            pl.BlockSpec((tm, HEAD_PAD), row(0)),
            pl.BlockSpec((tm, HEAD_PAD), row(0)),
            pl.BlockSpec((1, Q_LORA), const),
            pl.BlockSpec((1, KV_LORA), const),
            pl.BlockSpec((Q_LORA, width), const),
            pl.BlockSpec((KV_LORA, 2 * width), const),
        ],
        out_specs=[pl.BlockSpec((tm, width), row(0))] * 3,
        out_shape=(jax.ShapeDtypeStruct((TOKENS, width), BF16),) * 3,
        compiler_params=_cparams(("parallel",), 48),
        name="qkv_prep",
    )(proj, proj, tail, tail, c_tab, s_tab, qg, kvg, wq, wkv)


def _attn_kernel(q_ref, k_ref, v_ref, o_ref, *state):
    t = T_ATTN
    nh = HEADS_PER_STEP
    qi = pl.program_id(2)
    nt = (((1,), (1,)), ((), ()))
    m_sc, l_sc, acc_sc = state[0:nh], state[nh:2 * nh], state[2 * nh:3 * nh]
    for h in range(nh):
        m_sc[h][...] = jnp.full((t, LANES), -jnp.inf, F32)
        l_sc[h][...] = jnp.zeros((t, LANES), F32)
        acc_sc[h][...] = jnp.zeros((t, LANES), F32)

    def block(j, r0, nr, c0, nc, masked):
        start = pl.multiple_of(j * t + c0, nc)
        rows = slice(r0, r0 + nr)
        for h in range(nh):
            lo, hi = h * HEAD_PAD, (h + 1) * HEAD_PAD
            s = lax.dot_general(q_ref[rows, lo:hi], k_ref[pl.ds(start, nc), lo:hi], nt, preferred_element_type=F32)
            if masked:
                row_id = r0 + lax.broadcasted_iota(jnp.int32, (nr, nc), 0)
                col_id = c0 + lax.broadcasted_iota(jnp.int32, (nr, nc), 1)
                s = jnp.where(row_id >= col_id, s, -jnp.inf)
            blocks = [s[:, c * LANES:(c + 1) * LANES] for c in range(nc // LANES)]
            bmax = functools.reduce(jnp.maximum, blocks)
            m_old = m_sc[h][rows, :]
            m_new = jnp.maximum(m_old, jnp.max(bmax, axis=-1, keepdims=True))
            alpha = jnp.exp2(m_old - m_new)
            ps = [jnp.exp2(b - m_new) for b in blocks]
            p = jnp.concatenate(ps, axis=1).astype(BF16)
            l_sc[h][rows, :] = alpha * l_sc[h][rows, :] + functools.reduce(jnp.add, ps)
            acc_sc[h][rows, :] = alpha * acc_sc[h][rows, :] + jnp.dot(p, v_ref[pl.ds(start, nc), lo:hi],
                                                                       preferred_element_type=F32)
            m_sc[h][rows, :] = m_new

    def body(j, carry):
        block(j, 0, t, 0, t, False)
        return carry

    lax.fori_loop(0, qi, body, 0)
    half = t // 2
    block(qi, 0, t, 0, half, True)
    block(qi, half, half, half, half, True)
    out = [acc_sc[h][...] / jnp.sum(l_sc[h][...], axis=-1, keepdims=True) for h in range(nh)]
    o_ref[...] = jnp.concatenate([out[h] + out[h + 1] for h in range(0, nh, 2)], axis=1).astype(BF16)


def _attention(q, k, v):
    t = T_ATTN
    nq = SEQ // t
    nh = HEADS_PER_STEP
    return pl.pallas_call(
        _attn_kernel,
        grid=(BATCH, N_HEADS // nh, nq),
        in_specs=[
            pl.BlockSpec((t, nh * HEAD_PAD), lambda b, hg, i: (b * nq + i, hg)),
            pl.BlockSpec((SEQ, nh * HEAD_PAD), lambda b, hg, i: (b, hg)),
            pl.BlockSpec((SEQ, nh * HEAD_PAD), lambda b, hg, i: (b, hg)),
        ],
        out_specs=pl.BlockSpec((t, nh * V_HEAD), lambda b, hg, i: (b * nq + i, hg)),
        out_shape=jax.ShapeDtypeStruct((TOKENS, N_HEADS * V_HEAD), BF16),
        scratch_shapes=[pltpu.VMEM((t, LANES), F32)] * (3 * nh),
        compiler_params=_cparams(("parallel", "parallel", "arbitrary"), 48),
        name="mla_attention",
    )(q, k, v)


def _mixer_tail_kernel(gates_ref, sc_ref, sch_ref, gv_ref, gg_ref, gvh_ref, ggh_ref, att_ref, x_ref,
                       cwa_ref, woa_ref, cwc_ref, lng_ref, lnb_ref, woc_ref, wob_ref, wo_ref,
                       out_ref, cbuf, ubuf, shifted, vbuf):
    tm = TM_MIX
    has_past = (pl.program_id(0) % (SEQ // tm)) != 0

    sc = sc_ref[...]
    sc_b = sc[:, 0:SC_WIDTH].astype(F32)
    cbuf[8:8 + tm, :] = sc[:, SC_WIDTH:2 * SC_WIDTH].astype(F32) * sc[:, 2 * SC_WIDTH:].astype(F32)
    sch = sch_ref[...].astype(F32)[HALO_A - 8:HALO_A, :]
    cbuf[0:8, :] = jnp.where(has_past, sch[:, SC_WIDTH:2 * SC_WIDTH] * sch[:, 2 * SC_WIDTH:], 0.0)
    conv_a = cwa_ref[0:1, :] * cbuf[6:6 + tm, :]
    for t in range(1, SC_KERNEL):
        conv_a = conv_a + cwa_ref[t:t + 1, :] * cbuf[6 + t:6 + t + tm, :]
    y_a = jnp.dot((sc_b * conv_a).astype(BF16), woa_ref[...], preferred_element_type=F32)

    ubuf[HALO_C:HALO_C + tm, :] = gv_ref[...].astype(F32) * _sigmoid(gg_ref[...].astype(F32))
    ubuf[0:HALO_C, :] = jnp.where(has_past, gvh_ref[...].astype(F32) * _sigmoid(ggh_ref[...].astype(F32)), 0.0)
    rows = tm + HALO_C - 8
    u_all = ubuf[...]
    for b in range(1, 8):
        shifted[b - 1, 0:rows, :] = pltpu.roll(u_all, tm + HALO_C - b, axis=0)[0:rows, :]
    base = HALO_C - (CONF_KERNEL - 1)
    for r0 in range(0, tm, CONV_CHUNK):
        acc = None
        for t in range(CONF_KERNEL):
            off = base + t
            a0 = r0 + off - off % 8
            src = ubuf[a0:a0 + CONV_CHUNK, :] if off % 8 == 0 else shifted[off % 8 - 1, a0:a0 + CONV_CHUNK, :]
            term = cwc_ref[t:t + 1, :] * src
            acc = term if acc is None else acc + term
        mu = jnp.mean(acc, axis=-1, keepdims=True)
        xc = acc - mu
        var = jnp.mean(xc * xc, axis=-1, keepdims=True)
        y = xc * lax.rsqrt(var + EPS) * lng_ref[...] + lnb_ref[...]
        vbuf[r0:r0 + CONV_CHUNK, :] = (y * _sigmoid(y)).astype(BF16)
    y_c = jnp.dot(vbuf[...], woc_ref[...], preferred_element_type=F32)

    y_b = jnp.dot(att_ref[...], wob_ref[...], preferred_element_type=F32)

    g = gates_ref[...]
    merged = (_sigmoid(g[:, 0:D_MODEL].astype(F32)) * y_a
              + _sigmoid(g[:, D_MODEL:2 * D_MODEL].astype(F32)) * y_b
              + _sigmoid(g[:, 2 * D_MODEL:].astype(F32)) * y_c)
    out_ref[...] = x_ref[...] + jnp.dot(merged.astype(BF16), wo_ref[...], preferred_element_type=F32)


def _mixer_tail(proj, tail, att, x, cwa, woa, cwc, lng, lnb, woc, wob, wo):
    tm = TM_MIX
    row = lambda width, col: pl.BlockSpec((tm, width), lambda i: (i, col // width))
    halo = lambda rows, width, col: pl.BlockSpec(
        (rows, width), lambda i: (jnp.maximum(i * (tm // rows) - 1, 0), col // width))
    const = lambda a: pl.BlockSpec(a.shape, lambda i: (0,) * a.ndim)
    weights = (cwa, woa, cwc, lng, lnb, woc, wob, wo)
    return pl.pallas_call(
        _mixer_tail_kernel,
        grid=(TOKENS // tm,),
        in_specs=[
            row(3 * D_MODEL, COL_GATES),
            row(3 * SC_WIDTH, COL_SC),
            halo(HALO_A, 3 * SC_WIDTH, COL_SC),
            row(CONF_WIDTH, TCOL_GLU),
            row(CONF_WIDTH, TCOL_GLU + CONF_WIDTH),
            halo(HALO_C, CONF_WIDTH, TCOL_GLU),
            halo(HALO_C, CONF_WIDTH, TCOL_GLU + CONF_WIDTH),
            pl.BlockSpec((tm, N_HEADS * V_HEAD), lambda i: (i, 0)),
            pl.BlockSpec((tm, D_MODEL), lambda i: (i, 0)),
        ] + [const(a) for a in weights],
        out_specs=pl.BlockSpec((tm, D_MODEL), lambda i: (i, 0)),
        out_shape=jax.ShapeDtypeStruct((TOKENS, D_MODEL), F32),
        scratch_shapes=[
            pltpu.VMEM((tm + 8, SC_WIDTH), F32),
            pltpu.VMEM((tm + HALO_C, CONF_WIDTH), F32),
            pltpu.VMEM((7, tm + HALO_C - 8, CONF_WIDTH), F32),
            pltpu.VMEM((tm, CONF_WIDTH), BF16),
        ],
        compiler_params=_cparams(("parallel",), 56),
        name="mixer_tail",
    )(proj, proj, proj, tail, tail, tail, tail, att, x, *weights)


def _route(logits):
    lane = lax.broadcasted_iota(jnp.int32, logits.shape, 1)
    lane_f = lane.astype(F32)
    neg = -jnp.inf
    big = float(LANES)
    is_grp = (lane >= ROUTER_GROUP_LANE) & (lane < ROUTER_GROUP_LANE + N_GROUPS)
    glog = jnp.where(is_grp, logits, neg)
    gmax = jnp.max(glog, axis=-1, keepdims=True)
    gidx = jnp.min(jnp.where(glog == gmax, lane_f, big), axis=-1, keepdims=True)
    p_sel = 1.0 / jnp.sum(jnp.exp(glog - gmax), axis=-1, keepdims=True)
    first = (gidx - ROUTER_GROUP_LANE) * EXPERTS_PER_GROUP
    in_grp = (lane_f >= first) & (lane_f < first + EXPERTS_PER_GROUP)
    el = jnp.where(in_grp, logits, neg)
    m1 = jnp.max(el, axis=-1, keepdims=True)
    i1 = jnp.min(jnp.where(el == m1, lane_f, big), axis=-1, keepdims=True)
    el2 = jnp.where(lane_f == i1, neg, el)
    m2 = jnp.max(el2, axis=-1, keepdims=True)
    i2 = jnp.min(jnp.where(el2 == m2, lane_f, big), axis=-1, keepdims=True)
    e2 = jnp.exp(m2 - m1)
    w1 = p_sel / (1.0 + e2)
    w2 = w1 * e2
    return i1, i2, w1, w2


def _router_kernel(x_ref, g_ref, wr_ref, br_ref, meta_ref, meta_t_ref, cnt_ref, xs0_ref, run_ref):
    tm = TM_ROUTE

    @pl.when(pl.program_id(0) == 0)
    def _():
        run_ref[...] = jnp.zeros_like(run_ref)

    h = _rms(x_ref[...], g_ref[...])
    h_hi = h.astype(BF16)
    h_lo = (h - h_hi.astype(F32)).astype(BF16)
    hi_terms = jnp.dot(h_hi, wr_ref[...], preferred_element_type=F32)
    logits = (hi_terms[:, :LANES] + hi_terms[:, LANES:]
              + jnp.dot(h_lo, wr_ref[:, :LANES], preferred_element_type=F32) + br_ref[...])
    i1, i2, w1, w2 = _route(logits)
    lane = lax.broadcasted_iota(jnp.int32, (tm, LANES), 1)
    lane_f = lane.astype(F32)
    oh1 = lane_f == i1
    oh2 = lane_f == i2
    onehot = jnp.where(oh1, 1.0, 0.0) + jnp.where(oh2, 1.0, 0.0)
    row_id = lax.broadcasted_iota(jnp.int32, (tm, tm), 0)
    col_id = lax.broadcasted_iota(jnp.int32, (tm, tm), 1)
    below = jnp.where(row_id > col_id, 1.0, 0.0).astype(BF16)
    before = run_ref[...] + jnp.dot(below, onehot.astype(BF16), preferred_element_type=F32)
    r1 = jnp.sum(jnp.where(oh1, before, 0.0), axis=-1, keepdims=True)
    r2 = jnp.sum(jnp.where(oh2, before, 0.0), axis=-1, keepdims=True)
    run_ref[...] += jnp.sum(onehot, axis=0, keepdims=True)
    cnt_ref[...] = run_ref[...]
    meta = jnp.zeros((tm, LANES), F32)
    for col, val in enumerate((i1, i2, r1, r2, w1, w2)):
        meta = jnp.where(lane == col, val, meta)
    meta_ref[...] = meta
    meta_t_ref[...] = meta.T[0:META_ROWS, :]
    xs0_ref[...] = jnp.zeros_like(xs0_ref)


def _router(x, g, wr, br):
    tm = TM_ROUTE
    const = lambda i: (0, 0)
    return pl.pallas_call(
        _router_kernel,
        grid=(TOKENS // tm,),
        in_specs=[
            pl.BlockSpec((tm, D_MODEL), lambda i: (i, 0)),
            pl.BlockSpec((1, D_MODEL), const),
            pl.BlockSpec((D_MODEL, 2 * LANES), const),
            pl.BlockSpec((1, LANES), const),
        ],
        out_specs=[
            pl.BlockSpec((tm, LANES), lambda i: (i, 0)),
            pl.BlockSpec((META_ROWS, tm), lambda i: (0, i)),
            pl.BlockSpec((1, LANES), const),
            pl.BlockSpec((SORTED_ROWS // (TOKENS // tm), LANES), lambda i: (i, 0)),
        ],
        out_shape=(
            jax.ShapeDtypeStruct((TOKENS, LANES), F32),
            jax.ShapeDtypeStruct((META_ROWS, TOKENS), F32),
            jax.ShapeDtypeStruct((1, LANES), F32),
            jax.ShapeDtypeStruct((SORTED_ROWS, LANES), F32),
        ),
        scratch_shapes=[pltpu.VMEM((1, LANES), F32)],
        compiler_params=_cparams(("arbitrary",), 48),
        name="moe_router",
    )(x, g, wr, br)


def _dispatch_plan(meta_t, counts):
    e1 = meta_t[META_E1].astype(jnp.int32)
    e2 = meta_t[META_E2].astype(jnp.int32)
    r1 = meta_t[META_R1].astype(jnp.int32)
    r2 = meta_t[META_R2].astype(jnp.int32)
    cnt = counts[0, :N_EXPERTS].astype(jnp.int32)
    tiles = (cnt + TM_EXP - 1) // TM_EXP
    tile_end = jnp.cumsum(tiles)
    first_slot = ((tile_end - tiles) * TM_EXP)[:, None]
    expert = jnp.arange(N_EXPERTS, dtype=jnp.int32)[:, None]
    pos1 = jnp.sum(jnp.where(e1[None, :] == expert, first_slot, 0), axis=0) + r1
    pos2 = jnp.sum(jnp.where(e2[None, :] == expert, first_slot, 0), axis=0) + r2
    n_tiles = tile_end[-1:]
    tile_id = jnp.minimum(jnp.arange(MAX_TILES, dtype=jnp.int32), n_tiles - 1)
    tile_expert = jnp.sum((tile_id[:, None] >= tile_end[None, :]).astype(jnp.int32), axis=1)
    return pos1, pos2, tile_expert, n_tiles


def _dispatch_kernel(pos1_ref, pos2_ref, x_ref, g_ref, xs_in_ref, xs_ref, slab, sem):
    del xs_in_ref
    tm = TM_DISP
    i = pl.program_id(0)
    last = pl.num_programs(0) - 1
    slot = i % 2

    def wait_copies(sl):
        for _ in range(TOP_K):
            pltpu.make_async_copy(slab.at[sl], xs_ref.at[pl.ds(0, tm * SLAB), :], sem.at[sl]).wait()

    @pl.when(i >= 2)
    def _():
        wait_copies(slot)

    h = _rms(x_ref[...], g_ref[...])
    for s in range(SLAB):
        slab[slot, pl.ds(s, tm, stride=SLAB), :] = h[:, s * LANES:(s + 1) * LANES]

    def body(c, carry):
        for u in range(GATHER_UNROLL):
            r = c * GATHER_UNROLL + u
            src = slab.at[slot, pl.ds(pl.multiple_of(r * SLAB, SLAB), SLAB), :]
            for k, pos_ref in enumerate((pos1_ref, pos2_ref)):
                p = pos_ref[i * tm + r]
                pltpu.make_async_copy(src, xs_ref.at[pl.ds(pl.multiple_of(p * SLAB, SLAB), SLAB), :],
                                      sem.at[slot]).start(priority=k % N_DMA_PRIORITIES)
        return carry

    lax.fori_loop(0, tm // GATHER_UNROLL, body, 0)

    @pl.when(i == last)
    def _():
        wait_copies(1 - slot)
        wait_copies(slot)


def _dispatch(pos1, pos2, x, g, xs0):
    tm = TM_DISP
    return pl.pallas_call(
        _dispatch_kernel,
        grid_spec=pltpu.PrefetchScalarGridSpec(
            num_scalar_prefetch=2,
            grid=(TOKENS // tm,),
            in_specs=[
                pl.BlockSpec((tm, D_MODEL), lambda i, p1, p2: (i, 0)),
                pl.BlockSpec((1, D_MODEL), lambda i, p1, p2: (0, 0)),
                pl.BlockSpec(memory_space=pl.ANY),
            ],
            out_specs=pl.BlockSpec(memory_space=pl.ANY),
            scratch_shapes=[pltpu.VMEM((2, tm * SLAB, LANES), F32), pltpu.SemaphoreType.DMA((2,))],
        ),
        out_shape=jax.ShapeDtypeStruct((SORTED_ROWS, LANES), F32),
        input_output_aliases={4: 0},
        compiler_params=_cparams(("arbitrary",), 48),
        name="moe_dispatch",
    )(pos1, pos2, x, g, xs0)


def _expert_kernel(te_ref, nt_ref, xs_ref, wg_ref, wu_ref, wd_ref, ys_ref, xbuf, sem):
    del te_ref
    tm = TM_EXP
    i = pl.program_id(0)
    n = nt_ref[0]

    def tile_copy(tile):
        slot = tile % EXPERT_SLOTS
        rows = pl.ds(pl.multiple_of(tile * (tm * SLAB), tm * SLAB), tm * SLAB)
        return pltpu.make_async_copy(xs_ref.at[rows, :], xbuf.at[slot], sem.at[slot])

    @pl.when(i == 0)
    def _():
        for ahead in range(EXPERT_PREFETCH):
            @pl.when(ahead < n)
            def _():
                tile_copy(ahead).start()

    @pl.when(i + EXPERT_PREFETCH < n)
    def _():
        tile_copy(i + EXPERT_PREFETCH).start()

    @pl.when(i < n)
    def _():
        tile_copy(i).wait()
        slot = i % EXPERT_SLOTS
        xt = jnp.concatenate([xbuf[slot, pl.ds(s, tm, stride=SLAB), :].astype(BF16) for s in range(SLAB)], axis=1)
        hg = jnp.dot(xt, wg_ref[...].astype(BF16), preferred_element_type=F32)
        hu = jnp.dot(xt, wu_ref[...].astype(BF16), preferred_element_type=F32)
        hh = (hg * _sigmoid(hg) * hu).astype(BF16)
        y = jnp.dot(hh, wd_ref[...].astype(BF16), preferred_element_type=F32)
        for s in range(SLAB):
            ys_ref[pl.ds(s, tm, stride=SLAB), :] = y[:, s * LANES:(s + 1) * LANES]

    @pl.when(i >= n)
    def _():
        ys_ref[...] = jnp.zeros_like(ys_ref)


def _experts(layer, tile_expert, n_tiles, xs, w_gate, w_up, w_down):
    tm = TM_EXP
    wspec = lambda rows, cols: pl.BlockSpec((None, None, rows, cols), lambda i, te, nt: (layer, te[i], 0, 0))
    return pl.pallas_call(
        _expert_kernel,
        grid_spec=pltpu.PrefetchScalarGridSpec(
            num_scalar_prefetch=2,
            grid=(MAX_TILES,),
            in_specs=[
                pl.BlockSpec(memory_space=pl.ANY),
                wspec(D_MODEL, EXPERT_HIDDEN),
                wspec(D_MODEL, EXPERT_HIDDEN),
                wspec(EXPERT_HIDDEN, D_MODEL),
            ],
            out_specs=pl.BlockSpec((tm * SLAB, LANES), lambda i, te, nt: (i, 0)),
            scratch_shapes=[pltpu.VMEM((EXPERT_SLOTS, tm * SLAB, LANES), F32), pltpu.SemaphoreType.DMA((EXPERT_SLOTS,))],
        ),
        out_shape=jax.ShapeDtypeStruct((SORTED_ROWS, LANES), F32),
        compiler_params=_cparams(("arbitrary",), 48),
        name="moe_experts",
    )(tile_expert, n_tiles, xs, w_gate, w_up, w_down)


def _start_slab_gathers(idx_refs, base, n_rows, src_hbm, dst_bufs, sem):
    def body(c, carry):
        for u in range(GATHER_UNROLL):
            r = c * GATHER_UNROLL + u
            for k, (idx_ref, dst) in enumerate(zip(idx_refs, dst_bufs)):
                t = idx_ref[base + r]
                pltpu.make_async_copy(src_hbm.at[pl.ds(pl.multiple_of(t * SLAB, SLAB), SLAB), :],
                                      dst.at[pl.ds(pl.multiple_of(r * SLAB, SLAB), SLAB), :],
                                      sem).start(priority=k % N_DMA_PRIORITIES)
        return carry

    lax.fori_loop(0, n_rows // GATHER_UNROLL, body, 0)


def _wait_slab_gathers(n_rows, src_hbm, dst, sem):
    pltpu.make_async_copy(src_hbm.at[pl.ds(0, n_rows * SLAB), :], dst, sem).wait()


def _combine_ple_kernel(pos1_ref, pos2_ref, x_ref, meta_ref, ys_ref, p_ref, g_ref, wg_ref, wp_ref, fg_ref,
                        out_ref, cbuf, sem, *, final):
    tm = TM_COMB
    i = pl.program_id(0)

    def gather(tile, slot):
        _start_slab_gathers((pos1_ref, pos2_ref), tile * tm, tm, ys_ref, (cbuf.at[slot, 0], cbuf.at[slot, 1]),
                            sem.at[slot])

    @pl.when(i == 0)
    def _():
        gather(0, 0)

    @pl.when(i + 1 < pl.num_programs(0))
    def _():
        gather(i + 1, (i + 1) % 2)

    slot = i % 2
    for k in range(2):
        _wait_slab_gathers(tm, ys_ref, cbuf.at[slot, k], sem.at[slot])
    meta = meta_ref[...]
    w1 = meta[:, META_W1:META_W1 + 1]
    w2 = meta[:, META_W2:META_W2 + 1]
    moe = jnp.concatenate([w1 * cbuf[slot, 0, pl.ds(s, tm, stride=SLAB), :]
                           + w2 * cbuf[slot, 1, pl.ds(s, tm, stride=SLAB), :] for s in range(SLAB)], axis=1)
    x = x_ref[...] + moe
    h = _rms(x, g_ref[...]).astype(BF16)
    gate = _sigmoid(jnp.dot(h, wg_ref[...], preferred_element_type=F32))
    emb = jnp.dot(p_ref[...].astype(BF16), wp_ref[...], preferred_element_type=F32)
    y = x + gate * emb
    if final:
        y = _rms(y, fg_ref[...])
    out_ref[...] = y


def _combine_ple(layer, pos1, pos2, x, meta, ys, p, g, wg, wp, fg, final):
    tm = TM_COMB
    const = lambda i, p1, p2: (0, 0)
    rows = lambda i, p1, p2: (i, 0)
    return pl.pallas_call(
        functools.partial(_combine_ple_kernel, final=final),
        grid_spec=pltpu.PrefetchScalarGridSpec(
            num_scalar_prefetch=2,
            grid=(TOKENS // tm,),
            in_specs=[
                pl.BlockSpec((tm, D_MODEL), rows),
                pl.BlockSpec((tm, LANES), rows),
                pl.BlockSpec(memory_space=pl.ANY),
                pl.BlockSpec((None, tm, PLE_DIM), lambda i, p1, p2: (layer, i, 0)),
                pl.BlockSpec((1, D_MODEL), const),
                pl.BlockSpec((D_MODEL, D_MODEL), const),
                pl.BlockSpec((PLE_DIM, D_MODEL), const),
                pl.BlockSpec((1, D_MODEL), const),
            ],
            out_specs=pl.BlockSpec((tm, D_MODEL), rows),
            scratch_shapes=[pltpu.VMEM((2, 2, tm * SLAB, LANES), F32), pltpu.SemaphoreType.DMA((2,))],
        ),
        out_shape=jax.ShapeDtypeStruct((TOKENS, D_MODEL), F32),
        compiler_params=_cparams(("arbitrary",), 48),
        name="moe_combine_ple",
    )(pos1, pos2, x, meta, ys, p, g, wg, wp, fg)


def _in_proj_weights(w_in):
    k_rope = w_in[:, MAIN_COLS:MAIN_COLS + QK_ROPE]
    glu = w_in[:, MAIN_COLS + QK_ROPE:]
    half = QK_ROPE // 2
    zn = jnp.zeros((D_MODEL, QK_NOPE), F32)
    zp = jnp.zeros((D_MODEL, HEAD_PAD - QK_NOPE - QK_ROPE), F32)
    tail = jnp.concatenate([glu, zn, k_rope, zp, zn, k_rope[:, half:], k_rope[:, :half], zp], axis=1)
    return w_in[:, :MAIN_COLS].astype(BF16), tail.astype(BF16)


def _q_weight(w_uq):
    scale = (QK_NOPE + QK_ROPE) ** -0.5 * LOG2_E
    w = (w_uq * scale).reshape(Q_LORA, N_HEADS, QK_NOPE + QK_ROPE)
    zp = jnp.zeros((Q_LORA, N_HEADS, HEAD_PAD - QK_NOPE - QK_ROPE), F32)
    return jnp.concatenate([w, zp], axis=2).reshape(Q_LORA, N_HEADS * HEAD_PAD).astype(BF16)


def _kv_weight(w_ukv):
    w = w_ukv.reshape(KV_LORA, N_HEADS, QK_NOPE + V_HEAD)
    k_nope, v = w[:, :, :QK_NOPE], w[:, :, QK_NOPE:]
    z = jnp.zeros_like(v)
    k_part = jnp.concatenate([k_nope, jnp.zeros_like(k_nope)], axis=2).reshape(KV_LORA, N_HEADS * HEAD_PAD)
    odd = (jnp.arange(N_HEADS) % 2 == 1)[None, :, None]
    v_part = jnp.concatenate([jnp.where(odd, z, v), jnp.where(odd, v, z)], axis=2).reshape(KV_LORA, N_HEADS * HEAD_PAD)
    return jnp.concatenate([k_part, v_part], axis=1).astype(BF16)


def _router_weight(w_rg, b_rg, w_re, b_re):
    pad = LANES - N_EXPERTS - N_GROUPS
    w = jnp.concatenate([w_re, w_rg, jnp.zeros((D_MODEL, pad), F32)], axis=1)
    b = jnp.concatenate([b_re, b_rg, jnp.zeros((pad,), F32)]).reshape(1, LANES)
    w_hi = w.astype(BF16)
    w_lo = (w - w_hi.astype(F32)).astype(BF16)
    return jnp.concatenate([w_hi, w_lo], axis=1), b


def kernel(x, p, positions, ln_mix_g, w_in, conv_a_w, w_out_a, q_norm_g, w_uq, kv_norm_g, w_ukv, w_out_b, conv_c_w, ln_c_g, ln_c_b, w_out_c, w_o, ln_ffn_g, w_route_grp, b_route_grp, w_route_exp, b_route_exp, w_exp_gate, w_exp_up, w_exp_down, ln_ple_g, w_ple_gate, w_ple, final_norm_g):
    c_tab, s_tab = _rope_tables(positions)
    xf = x.reshape(TOKENS, D_MODEL)
    pf = p.reshape(DEPTH, TOKENS, PLE_DIM)
    row = lambda a: a.reshape(1, -1)
    for i in range(DEPTH):
        proj, tail = _inproj(xf, row(ln_mix_g[i]), *_in_proj_weights(w_in[i]))
        q, k, v = _qkv(proj, tail, c_tab, s_tab, row(q_norm_g[i]), row(kv_norm_g[i]), _q_weight(w_uq[i]), _kv_weight(w_ukv[i]))
        att = _attention(q, k, v)
        xf = _mixer_tail(proj, tail, att, xf, conv_a_w[i], w_out_a[i].astype(BF16), conv_c_w[i], row(ln_c_g[i]),
                         row(ln_c_b[i]), w_out_c[i].astype(BF16), w_out_b[i].astype(BF16), w_o[i].astype(BF16))
        wr, br = _router_weight(w_route_grp[i], b_route_grp[i], w_route_exp[i], b_route_exp[i])
        meta, meta_t, counts, xs0 = _router(xf, row(ln_ffn_g[i]), wr, br)
        pos1, pos2, tile_expert, n_tiles = _dispatch_plan(meta_t, counts)
        xs = _dispatch(pos1, pos2, xf, row(ln_ffn_g[i]), xs0)
        ys = _experts(i, tile_expert, n_tiles, xs, w_exp_gate, w_exp_up, w_exp_down)
        xf = _combine_ple(i, pos1, pos2, xf, meta, ys, pf, row(ln_ple_g[i]), w_ple_gate[i].astype(BF16),
                          w_ple[i].astype(BF16), row(final_norm_g), final=(i == DEPTH - 1))
    return xf.reshape(BATCH, SEQ, D_MODEL)
```

```python
import functools

import jax
import jax.numpy as jnp
from jax import lax
from jax.experimental import pallas as pl
from jax.experimental.pallas import tpu as pltpu

D_MODEL = 1024
BATCH = 8
SEQ = 2048
DEPTH = 2
TOKENS = BATCH * SEQ
PLE_DIM = 256
SC_WIDTH = 512
SC_KERNEL = 3
N_HEADS = 8
QK_NOPE = 64
QK_ROPE = 32
V_HEAD = 64
Q_LORA = 768
KV_LORA = 256
ROPE_THETA = 10000.0
CONF_WIDTH = 512
CONF_KERNEL = 31
N_GROUPS = 4
EXPERTS_PER_GROUP = 8
N_EXPERTS = N_GROUPS * EXPERTS_PER_GROUP
EXPERT_HIDDEN = 256
EPS = 1e-6
LOG2_E = 1.4426950408889634

LANES = 128
HEAD_PAD = 128
F32 = jnp.float32
BF16 = jnp.bfloat16

COL_GATES = 0
COL_SC = 3 * D_MODEL
COL_QLAT = COL_SC + 3 * SC_WIDTH
COL_KVLAT = COL_QLAT + Q_LORA
MAIN_COLS = COL_KVLAT + KV_LORA
TCOL_GLU = 0
TCOL_KR = 2 * CONF_WIDTH
TCOL_KRSW = TCOL_KR + HEAD_PAD
TAIL_COLS = TCOL_KRSW + HEAD_PAD

ROUTER_GROUP_LANE = N_EXPERTS

TM_INPROJ = 1024
TN_INPROJ = MAIN_COLS // 2
TN_SPLIT = 1536
TM_QKV = 512
T_ATTN = 512
HEADS_PER_STEP = 4
TM_MIX = 512
CONV_CHUNK = 64
HALO_C = 32
HALO_A = 16
TM_ROUTE = 512
TM_EXP = 512
EXPERT_PREFETCH = 2
EXPERT_SLOTS = EXPERT_PREFETCH + 1
TM_DISP = 512
TM_COMB = 256
TOP_K = 2
MAX_TILES = TOKENS * TOP_K // TM_EXP + N_EXPERTS
SLAB = D_MODEL // LANES
GATHER_UNROLL = 16
N_DMA_PRIORITIES = 2
META_E1, META_E2, META_R1, META_R2, META_W1, META_W2 = range(6)
META_ROWS = 8
SORTED_ROWS = MAX_TILES * TM_EXP * SLAB


def _cparams(semantics, vmem_mb):
    return pltpu.CompilerParams(dimension_semantics=semantics, vmem_limit_bytes=vmem_mb * 1024 * 1024)


def _sigmoid(x):
    return 1.0 / (1.0 + jnp.exp(-x))


def _rms(x, g):
    return x * lax.rsqrt(jnp.mean(x * x, axis=-1, keepdims=True) + EPS) * g


def _place(x, onehot):
    x1 = x.astype(BF16)
    r1 = x - x1.astype(F32)
    x2 = r1.astype(BF16)
    x3 = (r1 - x2.astype(F32)).astype(BF16)
    return (jnp.dot(x1, onehot, preferred_element_type=F32) + jnp.dot(x2, onehot, preferred_element_type=F32)
            + jnp.dot(x3, onehot, preferred_element_type=F32))


def _rope_kernel(pos_ref, freq_ref, c_ref, s_ref):
    half = QK_ROPE // 2
    per_row = LANES // half
    rows = TOKENS // per_row
    ang = pos_ref[...].astype(F32) * freq_ref[...]
    cos = jnp.cos(ang)
    sin = jnp.sin(ang)
    src = lax.broadcasted_iota(jnp.int32, (LANES, LANES), 0)
    dst = lax.broadcasted_iota(jnp.int32, (LANES, LANES), 1)
    lane = lax.broadcasted_iota(jnp.int32, (1, LANES), 1)
    ones_nope = jnp.where(lane < QK_NOPE, 1.0, 0.0)
    sign = jnp.where(lane < QK_NOPE + half, -1.0, 1.0)
    for j in range(per_row):
        f = src - half * j
        hit = (dst == QK_NOPE + f) | (dst == QK_NOPE + half + f)
        onehot = jnp.where((f >= 0) & (f < half) & hit, 1.0, 0.0).astype(BF16)
        c_ref[pl.ds(j, rows, stride=per_row), :] = _place(cos, onehot) + ones_nope
        s_ref[pl.ds(j, rows, stride=per_row), :] = _place(sin, onehot) * sign


def _rope_tables(positions):
    half = QK_ROPE // 2
    inv_freq = ROPE_THETA ** (-jnp.arange(0, QK_ROPE, 2, dtype=F32) / QK_ROPE)
    rows = TOKENS * half // LANES
    pos_rep = jnp.broadcast_to(positions.reshape(TOKENS, 1), (TOKENS, half)).reshape(rows, LANES)
    freq = jnp.tile(inv_freq, LANES // half).reshape(1, LANES)
    return pl.pallas_call(
        _rope_kernel,
        out_shape=(jax.ShapeDtypeStruct((TOKENS, HEAD_PAD), F32),) * 2,
        compiler_params=pltpu.CompilerParams(vmem_limit_bytes=48 * 1024 * 1024),
        name="rope_tables",
    )(pos_rep, freq)


def _inproj_kernel(x_ref, g_ref, wm_ref, wt_ref, om_ref, ot_ref, h_ref):
    j = pl.program_id(1)
    n_main = MAIN_COLS // TN_INPROJ

    @pl.when(j == 0)
    def _():
        h_ref[...] = _rms(x_ref[...], g_ref[...]).astype(BF16)

    @pl.when(j < n_main)
    def _():
        for lo, hi in ((0, TN_SPLIT), (TN_SPLIT, TN_INPROJ)):
            om_ref[:, lo:hi] = jnp.dot(h_ref[...], wm_ref[:, lo:hi], preferred_element_type=F32).astype(BF16)

    @pl.when(j == n_main)
    def _():
        ot_ref[...] = jnp.dot(h_ref[...], wt_ref[...], preferred_element_type=F32).astype(BF16)


def _inproj(x, g, w_main, w_tail):
    tm, tn = TM_INPROJ, TN_INPROJ
    n_main = MAIN_COLS // tn
    main_col = lambda i, j: jnp.minimum(j, n_main - 1)
    return pl.pallas_call(
        _inproj_kernel,
        grid=(TOKENS // tm, n_main + 1),
        in_specs=[
            pl.BlockSpec((tm, D_MODEL), lambda i, j: (i, 0)),
            pl.BlockSpec((1, D_MODEL), lambda i, j: (0, 0)),
            pl.BlockSpec((D_MODEL, tn), lambda i, j: (0, main_col(i, j))),
            pl.BlockSpec((D_MODEL, TAIL_COLS), lambda i, j: (0, 0)),
        ],
        out_specs=[
            pl.BlockSpec((tm, tn), lambda i, j: (i, main_col(i, j))),
            pl.BlockSpec((tm, TAIL_COLS), lambda i, j: (i, 0)),
        ],
        out_shape=(jax.ShapeDtypeStruct((TOKENS, MAIN_COLS), BF16), jax.ShapeDtypeStruct((TOKENS, TAIL_COLS), BF16)),
        scratch_shapes=[pltpu.VMEM((tm, D_MODEL), BF16)],
        compiler_params=_cparams(("parallel", "arbitrary"), 56),
        name="in_proj",
    )(x, g, w_main, w_tail)


def _qkv_kernel(ql_ref, kvl_ref, kr_ref, krsw_ref, c_ref, s_ref, qg_ref, kvg_ref, wq_ref, wkv_ref,
                q_out, k_out, v_out):
    c = c_ref[...]
    s = s_ref[...]
    width = N_HEADS * HEAD_PAD
    half = QK_ROPE // 2
    low_half = lax.broadcasted_iota(jnp.int32, c.shape, 1) < QK_NOPE + half

    def swap_halves(x):
        return jnp.where(low_half, pltpu.roll(x, HEAD_PAD - half, axis=1), pltpu.roll(x, half, axis=1))

    qn = _rms(ql_ref[...].astype(F32), qg_ref[...]).astype(BF16)
    qq = jnp.dot(qn, wq_ref[...], preferred_element_type=F32)
    for h in range(N_HEADS):
        lo, hi = h * HEAD_PAD, (h + 1) * HEAD_PAD
        q_out[:, lo:hi] = (qq[:, lo:hi] * c + swap_halves(qq[:, lo:hi]) * s).astype(BF16)
    kvn = _rms(kvl_ref[...].astype(F32), kvg_ref[...]).astype(BF16)
    kk = jnp.dot(kvn, wkv_ref[...], preferred_element_type=F32)
    kr = kr_ref[...].astype(F32) * c + krsw_ref[...].astype(F32) * s
    for h in range(N_HEADS):
        lo, hi = h * HEAD_PAD, (h + 1) * HEAD_PAD
        k_out[:, lo:hi] = (kk[:, lo:hi] + kr).astype(BF16)
    v_out[...] = kk[:, width:].astype(BF16)


def _qkv(proj, tail, c_tab, s_tab, qg, kvg, wq, wkv):
    tm = TM_QKV
    width = N_HEADS * HEAD_PAD
    row = lambda blk: (lambda i: (i, blk))
    const = lambda i: (0, 0)
    return pl.pallas_call(
        _qkv_kernel,
        grid=(TOKENS // tm,),
        in_specs=[
            pl.BlockSpec((tm, Q_LORA), row(COL_QLAT // Q_LORA)),
            pl.BlockSpec((tm, KV_LORA), row(COL_KVLAT // KV_LORA)),
            pl.BlockSpec((tm, HEAD_PAD), row(TCOL_KR // HEAD_PAD)),
            pl.BlockSpec((tm, HEAD_PAD), row(TCOL_KRSW // HEAD_PAD)),
            pl.BlockSpec((tm, HEAD_PAD), row(0)),
            pl.BlockSpec((tm, HEAD_PAD), row(0)),
            pl.BlockSpec((1, Q_LORA), const),
            pl.BlockSpec((1, KV_LORA), const),
            pl.BlockSpec((Q_LORA, width), const),
            pl.BlockSpec((KV_LORA, 2 * width), const),
        ],
        out_specs=[pl.BlockSpec((tm, width), row(0))] * 3,
        out_shape=(jax.ShapeDtypeStruct((TOKENS, width), BF16),) * 3,
        compiler_params=_cparams(("parallel",), 48),
        name="qkv_prep",
    )(proj, proj, tail, tail, c_tab, s_tab, qg, kvg, wq, wkv)


def _attn_kernel(q_ref, k_ref, v_ref, o_ref, *state):
    t = T_ATTN
    nh = HEADS_PER_STEP
    qi = pl.program_id(2)
    nt = (((1,), (1,)), ((), ()))
    m_sc, l_sc, acc_sc = state[0:nh], state[nh:2 * nh], state[2 * nh:3 * nh]
    for h in range(nh):
        m_sc[h][...] = jnp.full((t, LANES), -jnp.inf, F32)
        l_sc[h][...] = jnp.zeros((t, LANES), F32)
        acc_sc[h][...] = jnp.zeros((t, LANES), F32)

    def block(j, r0, nr, c0, nc, masked):
        start = pl.multiple_of(j * t + c0, nc)
        rows = slice(r0, r0 + nr)
        for h in range(nh):
            lo, hi = h * HEAD_PAD, (h + 1) * HEAD_PAD
            s = lax.dot_general(q_ref[rows, lo:hi], k_ref[pl.ds(start, nc), lo:hi], nt, preferred_element_type=F32)
            if masked:
                row_id = r0 + lax.broadcasted_iota(jnp.int32, (nr, nc), 0)
                col_id = c0 + lax.broadcasted_iota(jnp.int32, (nr, nc), 1)
                s = jnp.where(row_id >= col_id, s, -jnp.inf)
            blocks = [s[:, c * LANES:(c + 1) * LANES] for c in range(nc // LANES)]
            bmax = functools.reduce(jnp.maximum, blocks)
            m_old = m_sc[h][rows, :]
            m_new = jnp.maximum(m_old, jnp.max(bmax, axis=-1, keepdims=True))
            alpha = jnp.exp2(m_old - m_new)
            ps = [jnp.exp2(b - m_new) for b in blocks]
            p = jnp.concatenate(ps, axis=1).astype(BF16)
            l_sc[h][rows, :] = alpha * l_sc[h][rows, :] + functools.reduce(jnp.add, ps)
            acc_sc[h][rows, :] = alpha * acc_sc[h][rows, :] + jnp.dot(p, v_ref[pl.ds(start, nc), lo:hi],
                                                                       preferred_element_type=F32)
            m_sc[h][rows, :] = m_new

    def body(j, carry):
        block(j, 0, t, 0, t, False)
        return carry

    lax.fori_loop(0, qi, body, 0)
    half = t // 2
    block(qi, 0, t, 0, half, True)
    block(qi, half, half, half, half, True)
    out = [acc_sc[h][...] / jnp.sum(l_sc[h][...], axis=-1, keepdims=True) for h in range(nh)]
    o_ref[...] = jnp.concatenate([out[h] + out[h + 1] for h in range(0, nh, 2)], axis=1).astype(BF16)


def _attention(q, k, v):
    t = T_ATTN
    nq = SEQ // t
    nh = HEADS_PER_STEP
    return pl.pallas_call(
        _attn_kernel,
        grid=(BATCH, N_HEADS // nh, nq),
        in_specs=[
            pl.BlockSpec((t, nh * HEAD_PAD), lambda b, hg, i: (b * nq + i, hg)),
            pl.BlockSpec((SEQ, nh * HEAD_PAD), lambda b, hg, i: (b, hg)),
            pl.BlockSpec((SEQ, nh * HEAD_PAD), lambda b, hg, i: (b, hg)),
        ],
        out_specs=pl.BlockSpec((t, nh * V_HEAD), lambda b, hg, i: (b * nq + i, hg)),
        out_shape=jax.ShapeDtypeStruct((TOKENS, N_HEADS * V_HEAD), BF16),
        scratch_shapes=[pltpu.VMEM((t, LANES), F32)] * (3 * nh),
        compiler_params=_cparams(("parallel", "parallel", "arbitrary"), 48),
        name="mla_attention",
    )(q, k, v)


def _mixer_tail_kernel(gates_ref, sc_ref, sch_ref, gv_ref, gg_ref, gvh_ref, ggh_ref, att_ref, x_ref,
                       cwa_ref, woa_ref, cwc_ref, lng_ref, lnb_ref, woc_ref, wob_ref, wo_ref,
                       out_ref, cbuf, ubuf, shifted, vbuf):
    tm = TM_MIX
    has_past = (pl.program_id(0) % (SEQ // tm)) != 0

    sc = sc_ref[...]
    sc_b = sc[:, 0:SC_WIDTH].astype(F32)
    cbuf[8:8 + tm, :] = sc[:, SC_WIDTH:2 * SC_WIDTH].astype(F32) * sc[:, 2 * SC_WIDTH:].astype(F32)
    sch = sch_ref[...].astype(F32)[HALO_A - 8:HALO_A, :]
    cbuf[0:8, :] = jnp.where(has_past, sch[:, SC_WIDTH:2 * SC_WIDTH] * sch[:, 2 * SC_WIDTH:], 0.0)
    conv_a = cwa_ref[0:1, :] * cbuf[6:6 + tm, :]
    for t in range(1, SC_KERNEL):
        conv_a = conv_a + cwa_ref[t:t + 1, :] * cbuf[6 + t:6 + t + tm, :]
    y_a = jnp.dot((sc_b * conv_a).astype(BF16), woa_ref[...], preferred_element_type=F32)

    ubuf[HALO_C:HALO_C + tm, :] = gv_ref[...].astype(F32) * _sigmoid(gg_ref[...].astype(F32))
    ubuf[0:HALO_C, :] = jnp.where(has_past, gvh_ref[...].astype(F32) * _sigmoid(ggh_ref[...].astype(F32)), 0.0)
    rows = tm + HALO_C - 8
    u_all = ubuf[...]
    for b in range(1, 8):
        shifted[b - 1, 0:rows, :] = pltpu.roll(u_all, tm + HALO_C - b, axis=0)[0:rows, :]
    base = HALO_C - (CONF_KERNEL - 1)
    for r0 in range(0, tm, CONV_CHUNK):
        acc = None
        for t in range(CONF_KERNEL):
            off = base + t
            a0 = r0 + off - off % 8
            src = ubuf[a0:a0 + CONV_CHUNK, :] if off % 8 == 0 else shifted[off % 8 - 1, a0:a0 + CONV_CHUNK, :]
            term = cwc_ref[t:t + 1, :] * src
            acc = term if acc is None else acc + term
        mu = jnp.mean(acc, axis=-1, keepdims=True)
        xc = acc - mu
        var = jnp.mean(xc * xc, axis=-1, keepdims=True)
        y = xc * lax.rsqrt(var + EPS) * lng_ref[...] + lnb_ref[...]
        vbuf[r0:r0 + CONV_CHUNK, :] = (y * _sigmoid(y)).astype(BF16)
    y_c = jnp.dot(vbuf[...], woc_ref[...], preferred_element_type=F32)

    y_b = jnp.dot(att_ref[...], wob_ref[...], preferred_element_type=F32)

    g = gates_ref[...]
    merged = (_sigmoid(g[:, 0:D_MODEL].astype(F32)) * y_a
              + _sigmoid(g[:, D_MODEL:2 * D_MODEL].astype(F32)) * y_b
              + _sigmoid(g[:, 2 * D_MODEL:].astype(F32)) * y_c)
    out_ref[...] = x_ref[...] + jnp.dot(merged.astype(BF16), wo_ref[...], preferred_element_type=F32)


def _mixer_tail(proj, tail, att, x, cwa, woa, cwc, lng, lnb, woc, wob, wo):
    tm = TM_MIX
    row = lambda width, col: pl.BlockSpec((tm, width), lambda i: (i, col // width))
    halo = lambda rows, width, col: pl.BlockSpec(
        (rows, width), lambda i: (jnp.maximum(i * (tm // rows) - 1, 0), col // width))
    const = lambda a: pl.BlockSpec(a.shape, lambda i: (0,) * a.ndim)
    weights = (cwa, woa, cwc, lng, lnb, woc, wob, wo)
    return pl.pallas_call(
        _mixer_tail_kernel,
        grid=(TOKENS // tm,),
        in_specs=[
            row(3 * D_MODEL, COL_GATES),
            row(3 * SC_WIDTH, COL_SC),
            halo(HALO_A, 3 * SC_WIDTH, COL_SC),
            row(CONF_WIDTH, TCOL_GLU),
            row(CONF_WIDTH, TCOL_GLU + CONF_WIDTH),
            halo(HALO_C, CONF_WIDTH, TCOL_GLU),
            halo(HALO_C, CONF_WIDTH, TCOL_GLU + CONF_WIDTH),
            pl.BlockSpec((tm, N_HEADS * V_HEAD), lambda i: (i, 0)),
            pl.BlockSpec((tm, D_MODEL), lambda i: (i, 0)),
        ] + [const(a) for a in weights],
        out_specs=pl.BlockSpec((tm, D_MODEL), lambda i: (i, 0)),
        out_shape=jax.ShapeDtypeStruct((TOKENS, D_MODEL), F32),
        scratch_shapes=[
            pltpu.VMEM((tm + 8, SC_WIDTH), F32),
            pltpu.VMEM((tm + HALO_C, CONF_WIDTH), F32),
            pltpu.VMEM((7, tm + HALO_C - 8, CONF_WIDTH), F32),
            pltpu.VMEM((tm, CONF_WIDTH), BF16),
        ],
        compiler_params=_cparams(("parallel",), 56),
        name="mixer_tail",
    )(proj, proj, proj, tail, tail, tail, tail, att, x, *weights)


def _route(logits):
    lane = lax.broadcasted_iota(jnp.int32, logits.shape, 1)
    lane_f = lane.astype(F32)
    neg = -jnp.inf
    big = float(LANES)
    is_grp = (lane >= ROUTER_GROUP_LANE) & (lane < ROUTER_GROUP_LANE + N_GROUPS)
    glog = jnp.where(is_grp, logits, neg)
    gmax = jnp.max(glog, axis=-1, keepdims=True)
    gidx = jnp.min(jnp.where(glog == gmax, lane_f, big), axis=-1, keepdims=True)
    p_sel = 1.0 / jnp.sum(jnp.exp(glog - gmax), axis=-1, keepdims=True)
    first = (gidx - ROUTER_GROUP_LANE) * EXPERTS_PER_GROUP
    in_grp = (lane_f >= first) & (lane_f < first + EXPERTS_PER_GROUP)
    el = jnp.where(in_grp, logits, neg)
    m1 = jnp.max(el, axis=-1, keepdims=True)
    i1 = jnp.min(jnp.where(el == m1, lane_f, big), axis=-1, keepdims=True)
    el2 = jnp.where(lane_f == i1, neg, el)
    m2 = jnp.max(el2, axis=-1, keepdims=True)
    i2 = jnp.min(jnp.where(el2 == m2, lane_f, big), axis=-1, keepdims=True)
    e2 = jnp.exp(m2 - m1)
    w1 = p_sel / (1.0 + e2)
    w2 = w1 * e2
    return i1, i2, w1, w2


def _router_kernel(x_ref, g_ref, wr_ref, br_ref, meta_ref, meta_t_ref, cnt_ref, xs0_ref, run_ref):
    tm = TM_ROUTE

    @pl.when(pl.program_id(0) == 0)
    def _():
        run_ref[...] = jnp.zeros_like(run_ref)

    h = _rms(x_ref[...], g_ref[...])
    h_hi = h.astype(BF16)
    h_lo = (h - h_hi.astype(F32)).astype(BF16)
    hi_terms = jnp.dot(h_hi, wr_ref[...], preferred_element_type=F32)
    logits = (hi_terms[:, :LANES] + hi_terms[:, LANES:]
              + jnp.dot(h_lo, wr_ref[:, :LANES], preferred_element_type=F32) + br_ref[...])
    i1, i2, w1, w2 = _route(logits)
    lane = lax.broadcasted_iota(jnp.int32, (tm, LANES), 1)
    lane_f = lane.astype(F32)
    oh1 = lane_f == i1
    oh2 = lane_f == i2
    onehot = jnp.where(oh1, 1.0, 0.0) + jnp.where(oh2, 1.0, 0.0)
    row_id = lax.broadcasted_iota(jnp.int32, (tm, tm), 0)
    col_id = lax.broadcasted_iota(jnp.int32, (tm, tm), 1)
    below = jnp.where(row_id > col_id, 1.0, 0.0).astype(BF16)
    before = run_ref[...] + jnp.dot(below, onehot.astype(BF16), preferred_element_type=F32)
    r1 = jnp.sum(jnp.where(oh1, before, 0.0), axis=-1, keepdims=True)
    r2 = jnp.sum(jnp.where(oh2, before, 0.0), axis=-1, keepdims=True)
    run_ref[...] += jnp.sum(onehot, axis=0, keepdims=True)
    cnt_ref[...] = run_ref[...]
    meta = jnp.zeros((tm, LANES), F32)
    for col, val in enumerate((i1, i2, r1, r2, w1, w2)):
        meta = jnp.where(lane == col, val, meta)
    meta_ref[...] = meta
    meta_t_ref[...] = meta.T[0:META_ROWS, :]
    xs0_ref[...] = jnp.zeros_like(xs0_ref)


def _router(x, g, wr, br):
    tm = TM_ROUTE
    const = lambda i: (0, 0)
    return pl.pallas_call(
        _router_kernel,
        grid=(TOKENS // tm,),
        in_specs=[
            pl.BlockSpec((tm, D_MODEL), lambda i: (i, 0)),
            pl.BlockSpec((1, D_MODEL), const),
            pl.BlockSpec((D_MODEL, 2 * LANES), const),
            pl.BlockSpec((1, LANES), const),
        ],
        out_specs=[
            pl.BlockSpec((tm, LANES), lambda i: (i, 0)),
            pl.BlockSpec((META_ROWS, tm), lambda i: (0, i)),
            pl.BlockSpec((1, LANES), const),
            pl.BlockSpec((SORTED_ROWS // (TOKENS // tm), LANES), lambda i: (i, 0)),
        ],
        out_shape=(
            jax.ShapeDtypeStruct((TOKENS, LANES), F32),
            jax.ShapeDtypeStruct((META_ROWS, TOKENS), F32),
            jax.ShapeDtypeStruct((1, LANES), F32),
            jax.ShapeDtypeStruct((SORTED_ROWS, LANES), F32),
        ),
        scratch_shapes=[pltpu.VMEM((1, LANES), F32)],
        compiler_params=_cparams(("arbitrary",), 48),
        name="moe_router",
    )(x, g, wr, br)


def _dispatch_plan(meta_t, counts):
    e1 = meta_t[META_E1].astype(jnp.int32)
    e2 = meta_t[META_E2].astype(jnp.int32)
    r1 = meta_t[META_R1].astype(jnp.int32)
    r2 = meta_t[META_R2].astype(jnp.int32)
    cnt = counts[0, :N_EXPERTS].astype(jnp.int32)
    tiles = (cnt + TM_EXP - 1) // TM_EXP
    tile_end = jnp.cumsum(tiles)
    first_slot = ((tile_end - tiles) * TM_EXP)[:, None]
    expert = jnp.arange(N_EXPERTS, dtype=jnp.int32)[:, None]
    pos1 = jnp.sum(jnp.where(e1[None, :] == expert, first_slot, 0), axis=0) + r1
    pos2 = jnp.sum(jnp.where(e2[None, :] == expert, first_slot, 0), axis=0) + r2
    n_tiles = tile_end[-1:]
    tile_id = jnp.minimum(jnp.arange(MAX_TILES, dtype=jnp.int32), n_tiles - 1)
    tile_expert = jnp.sum((tile_id[:, None] >= tile_end[None, :]).astype(jnp.int32), axis=1)
    return pos1, pos2, tile_expert, n_tiles


def _dispatch_kernel(pos1_ref, pos2_ref, x_ref, g_ref, xs_in_ref, xs_ref, slab, sem):
    del xs_in_ref
    tm = TM_DISP
    i = pl.program_id(0)
    last = pl.num_programs(0) - 1
    slot = i % 2

    def wait_copies(sl):
        for _ in range(TOP_K):
            pltpu.make_async_copy(slab.at[sl], xs_ref.at[pl.ds(0, tm * SLAB), :], sem.at[sl]).wait()

    @pl.when(i >= 2)
    def _():
        wait_copies(slot)

    h = _rms(x_ref[...], g_ref[...])
    for s in range(SLAB):
        slab[slot, pl.ds(s, tm, stride=SLAB), :] = h[:, s * LANES:(s + 1) * LANES]

    def body(c, carry):
        for u in range(GATHER_UNROLL):
            r = c * GATHER_UNROLL + u
            src = slab.at[slot, pl.ds(pl.multiple_of(r * SLAB, SLAB), SLAB), :]
            for k, pos_ref in enumerate((pos1_ref, pos2_ref)):
                p = pos_ref[i * tm + r]
                pltpu.make_async_copy(src, xs_ref.at[pl.ds(pl.multiple_of(p * SLAB, SLAB), SLAB), :],
                                      sem.at[slot]).start(priority=k % N_DMA_PRIORITIES)
        return carry

    lax.fori_loop(0, tm // GATHER_UNROLL, body, 0)

    @pl.when(i == last)
    def _():
        wait_copies(1 - slot)
        wait_copies(slot)


def _dispatch(pos1, pos2, x, g, xs0):
    tm = TM_DISP
    return pl.pallas_call(
        _dispatch_kernel,
        grid_spec=pltpu.PrefetchScalarGridSpec(
            num_scalar_prefetch=2,
            grid=(TOKENS // tm,),
            in_specs=[
                pl.BlockSpec((tm, D_MODEL), lambda i, p1, p2: (i, 0)),
                pl.BlockSpec((1, D_MODEL), lambda i, p1, p2: (0, 0)),
                pl.BlockSpec(memory_space=pl.ANY),
            ],
            out_specs=pl.BlockSpec(memory_space=pl.ANY),
            scratch_shapes=[pltpu.VMEM((2, tm * SLAB, LANES), F32), pltpu.SemaphoreType.DMA((2,))],
        ),
        out_shape=jax.ShapeDtypeStruct((SORTED_ROWS, LANES), F32),
        input_output_aliases={4: 0},
        compiler_params=_cparams(("arbitrary",), 48),
        name="moe_dispatch",
    )(pos1, pos2, x, g, xs0)


def _expert_kernel(te_ref, nt_ref, xs_ref, wg_ref, wu_ref, wd_ref, ys_ref, xbuf, sem):
    del te_ref
    tm = TM_EXP
    i = pl.program_id(0)
    n = nt_ref[0]

    def tile_copy(tile):
        slot = tile % EXPERT_SLOTS
        rows = pl.ds(pl.multiple_of(tile * (tm * SLAB), tm * SLAB), tm * SLAB)
        return pltpu.make_async_copy(xs_ref.at[rows, :], xbuf.at[slot], sem.at[slot])

    @pl.when(i == 0)
    def _():
        for ahead in range(EXPERT_PREFETCH):
            @pl.when(ahead < n)
            def _():
                tile_copy(ahead).start()

    @pl.when(i + EXPERT_PREFETCH < n)
    def _():
        tile_copy(i + EXPERT_PREFETCH).start()

    @pl.when(i < n)
    def _():
        tile_copy(i).wait()
        slot = i % EXPERT_SLOTS
        xt = jnp.concatenate([xbuf[slot, pl.ds(s, tm, stride=SLAB), :].astype(BF16) for s in range(SLAB)], axis=1)
        hg = jnp.dot(xt, wg_ref[...].astype(BF16), preferred_element_type=F32)
        hu = jnp.dot(xt, wu_ref[...].astype(BF16), preferred_element_type=F32)
        hh = (hg * _sigmoid(hg) * hu).astype(BF16)
        y = jnp.dot(hh, wd_ref[...].astype(BF16), preferred_element_type=F32)
        for s in range(SLAB):
            ys_ref[pl.ds(s, tm, stride=SLAB), :] = y[:, s * LANES:(s + 1) * LANES]

    @pl.when(i >= n)
    def _():
        ys_ref[...] = jnp.zeros_like(ys_ref)


def _experts(layer, tile_expert, n_tiles, xs, w_gate, w_up, w_down):
    tm = TM_EXP
    wspec = lambda rows, cols: pl.BlockSpec((None, None, rows, cols), lambda i, te, nt: (layer, te[i], 0, 0))
    return pl.pallas_call(
        _expert_kernel,
        grid_spec=pltpu.PrefetchScalarGridSpec(
            num_scalar_prefetch=2,
            grid=(MAX_TILES,),
            in_specs=[
                pl.BlockSpec(memory_space=pl.ANY),
                wspec(D_MODEL, EXPERT_HIDDEN),
                wspec(D_MODEL, EXPERT_HIDDEN),
                wspec(EXPERT_HIDDEN, D_MODEL),
            ],
            out_specs=pl.BlockSpec((tm * SLAB, LANES), lambda i, te, nt: (i, 0)),
            scratch_shapes=[pltpu.VMEM((EXPERT_SLOTS, tm * SLAB, LANES), F32), pltpu.SemaphoreType.DMA((EXPERT_SLOTS,))],
        ),
        out_shape=jax.ShapeDtypeStruct((SORTED_ROWS, LANES), F32),
        compiler_params=_cparams(("arbitrary",), 48),
        name="moe_experts",
    )(tile_expert, n_tiles, xs, w_gate, w_up, w_down)


def _start_slab_gathers(idx_refs, base, n_rows, src_hbm, dst_bufs, sem):
    def body(c, carry):
        for u in range(GATHER_UNROLL):
            r = c * GATHER_UNROLL + u
            for k, (idx_ref, dst) in enumerate(zip(idx_refs, dst_bufs)):
                t = idx_ref[base + r]
                pltpu.make_async_copy(src_hbm.at[pl.ds(pl.multiple_of(t * SLAB, SLAB), SLAB), :],
                                      dst.at[pl.ds(pl.multiple_of(r * SLAB, SLAB), SLAB), :],
                                      sem).start(priority=k % N_DMA_PRIORITIES)
        return carry

    lax.fori_loop(0, n_rows // GATHER_UNROLL, body, 0)


def _wait_slab_gathers(n_rows, src_hbm, dst, sem):
    pltpu.make_async_copy(src_hbm.at[pl.ds(0, n_rows * SLAB), :], dst, sem).wait()


def _combine_ple_kernel(pos1_ref, pos2_ref, x_ref, meta_ref, ys_ref, p_ref, g_ref, wg_ref, wp_ref, fg_ref,
                        out_ref, cbuf, sem, *, final):
    tm = TM_COMB
    i = pl.program_id(0)

    def gather(tile, slot):
        _start_slab_gathers((pos1_ref, pos2_ref), tile * tm, tm, ys_ref, (cbuf.at[slot, 0], cbuf.at[slot, 1]),
                            sem.at[slot])

    @pl.when(i == 0)
    def _():
        gather(0, 0)

    @pl.when(i + 1 < pl.num_programs(0))
    def _():
        gather(i + 1, (i + 1) % 2)

    slot = i % 2
    for k in range(2):
        _wait_slab_gathers(tm, ys_ref, cbuf.at[slot, k], sem.at[slot])
    meta = meta_ref[...]
    w1 = meta[:, META_W1:META_W1 + 1]
    w2 = meta[:, META_W2:META_W2 + 1]
    moe = jnp.concatenate([w1 * cbuf[slot, 0, pl.ds(s, tm, stride=SLAB), :]
                           + w2 * cbuf[slot, 1, pl.ds(s, tm, stride=SLAB), :] for s in range(SLAB)], axis=1)
    x = x_ref[...] + moe
    h = _rms(x, g_ref[...]).astype(BF16)
    gate = _sigmoid(jnp.dot(h, wg_ref[...], preferred_element_type=F32))
    emb = jnp.dot(p_ref[...].astype(BF16), wp_ref[...], preferred_element_type=F32)
    y = x + gate * emb
    if final:
        y = _rms(y, fg_ref[...])
    out_ref[...] = y


def _combine_ple(layer, pos1, pos2, x, meta, ys, p, g, wg, wp, fg, final):
    tm = TM_COMB
    const = lambda i, p1, p2: (0, 0)
    rows = lambda i, p1, p2: (i, 0)
    return pl.pallas_call(
        functools.partial(_combine_ple_kernel, final=final),
        grid_spec=pltpu.PrefetchScalarGridSpec(
            num_scalar_prefetch=2,
            grid=(TOKENS // tm,),
            in_specs=[
                pl.BlockSpec((tm, D_MODEL), rows),
                pl.BlockSpec((tm, LANES), rows),
                pl.BlockSpec(memory_space=pl.ANY),
                pl.BlockSpec((None, tm, PLE_DIM), lambda i, p1, p2: (layer, i, 0)),
                pl.BlockSpec((1, D_MODEL), const),
                pl.BlockSpec((D_MODEL, D_MODEL), const),
                pl.BlockSpec((PLE_DIM, D_MODEL), const),
                pl.BlockSpec((1, D_MODEL), const),
            ],
            out_specs=pl.BlockSpec((tm, D_MODEL), rows),
            scratch_shapes=[pltpu.VMEM((2, 2, tm * SLAB, LANES), F32), pltpu.SemaphoreType.DMA((2,))],
        ),
        out_shape=jax.ShapeDtypeStruct((TOKENS, D_MODEL), F32),
        compiler_params=_cparams(("arbitrary",), 48),
        name="moe_combine_ple",
    )(pos1, pos2, x, meta, ys, p, g, wg, wp, fg)


def _in_proj_weights(w_in):
    k_rope = w_in[:, MAIN_COLS:MAIN_COLS + QK_ROPE]
    glu = w_in[:, MAIN_COLS + QK_ROPE:]
    half = QK_ROPE // 2
    zn = jnp.zeros((D_MODEL, QK_NOPE), F32)
    zp = jnp.zeros((D_MODEL, HEAD_PAD - QK_NOPE - QK_ROPE), F32)
    tail = jnp.concatenate([glu, zn, k_rope, zp, zn, k_rope[:, half:], k_rope[:, :half], zp], axis=1)
    return w_in[:, :MAIN_COLS].astype(BF16), tail.astype(BF16)


def _q_weight(w_uq):
    scale = (QK_NOPE + QK_ROPE) ** -0.5 * LOG2_E
    w = (w_uq * scale).reshape(Q_LORA, N_HEADS, QK_NOPE + QK_ROPE)
    zp = jnp.zeros((Q_LORA, N_HEADS, HEAD_PAD - QK_NOPE - QK_ROPE), F32)
    return jnp.concatenate([w, zp], axis=2).reshape(Q_LORA, N_HEADS * HEAD_PAD).astype(BF16)


def _kv_weight(w_ukv):
    w = w_ukv.reshape(KV_LORA, N_HEADS, QK_NOPE + V_HEAD)
    k_nope, v = w[:, :, :QK_NOPE], w[:, :, QK_NOPE:]
    z = jnp.zeros_like(v)
    k_part = jnp.concatenate([k_nope, jnp.zeros_like(k_nope)], axis=2).reshape(KV_LORA, N_HEADS * HEAD_PAD)
    odd = (jnp.arange(N_HEADS) % 2 == 1)[None, :, None]
    v_part = jnp.concatenate([jnp.where(odd, z, v), jnp.where(odd, v, z)], axis=2).reshape(KV_LORA, N_HEADS * HEAD_PAD)
    return jnp.concatenate([k_part, v_part], axis=1).astype(BF16)


def _router_weight(w_rg, b_rg, w_re, b_re):
    pad = LANES - N_EXPERTS - N_GROUPS
    w = jnp.concatenate([w_re, w_rg, jnp.zeros((D_MODEL, pad), F32)], axis=1)
    b = jnp.concatenate([b_re, b_rg, jnp.zeros((pad,), F32)]).reshape(1, LANES)
    w_hi = w.astype(BF16)
    w_lo = (w - w_hi.astype(F32)).astype(BF16)
    return jnp.concatenate([w_hi, w_lo], axis=1), b


def kernel(x, p, positions, ln_mix_g, w_in, conv_a_w, w_out_a, q_norm_g, w_uq, kv_norm_g, w_ukv, w_out_b, conv_c_w, ln_c_g, ln_c_b, w_out_c, w_o, ln_ffn_g, w_route_grp, b_route_grp, w_route_exp, b_route_exp, w_exp_gate, w_exp_up, w_exp_down, ln_ple_g, w_ple_gate, w_ple, final_norm_g):
    c_tab, s_tab = _rope_tables(positions)
    xf = x.reshape(TOKENS, D_MODEL)
    pf = p.reshape(DEPTH, TOKENS, PLE_DIM)
    row = lambda a: a.reshape(1, -1)
    for i in range(DEPTH):
        proj, tail = _inproj(xf, row(ln_mix_g[i]), *_in_proj_weights(w_in[i]))
        q, k, v = _qkv(proj, tail, c_tab, s_tab, row(q_norm_g[i]), row(kv_norm_g[i]), _q_weight(w_uq[i]), _kv_weight(w_ukv[i]))
        att = _attention(q, k, v)
        xf = _mixer_tail(proj, tail, att, xf, conv_a_w[i], w_out_a[i].astype(BF16), conv_c_w[i], row(ln_c_g[i]),
                         row(ln_c_b[i]), w_out_c[i].astype(BF16), w_out_b[i].astype(BF16), w_o[i].astype(BF16))
        wr, br = _router_weight(w_route_grp[i], b_route_grp[i], w_route_exp[i], b_route_exp[i])
        meta, meta_t, counts, xs0 = _router(xf, row(ln_ffn_g[i]), wr, br)
        pos1, pos2, tile_expert, n_tiles = _dispatch_plan(meta_t, counts)
        xs = _dispatch(pos1, pos2, xf, row(ln_ffn_g[i]), xs0)
        ys = _experts(i, tile_expert, n_tiles, xs, w_exp_gate, w_exp_up, w_exp_down)
        xf = _combine_ple(i, pos1, pos2, xf, meta, ys, pf, row(ln_ple_g[i]), w_ple_gate[i].astype(BF16),
                          w_ple[i].astype(BF16), row(final_norm_g), final=(i == DEPTH - 1))
    return xf.reshape(BATCH, SEQ, D_MODEL)
```

```python
import functools

import jax
import jax.numpy as jnp
from jax import lax
from jax.experimental import pallas as pl
from jax.experimental.pallas import tpu as pltpu

D_MODEL = 1024
BATCH = 8
SEQ = 2048
DEPTH = 2
TOKENS = BATCH * SEQ
PLE_DIM = 256
SC_WIDTH = 512
SC_KERNEL = 3
N_HEADS = 8
QK_NOPE = 64
QK_ROPE = 32
V_HEAD = 64
Q_LORA = 768
KV_LORA = 256
ROPE_THETA = 10000.0
CONF_WIDTH = 512
CONF_KERNEL = 31
N_GROUPS = 4
EXPERTS_PER_GROUP = 8
N_EXPERTS = N_GROUPS * EXPERTS_PER_GROUP
EXPERT_HIDDEN = 256
EPS = 1e-6
LOG2_E = 1.4426950408889634

LANES = 128
HEAD_PAD = 128
F32 = jnp.float32
BF16 = jnp.bfloat16

COL_GATES = 0
COL_SC = 3 * D_MODEL
COL_QLAT = COL_SC + 3 * SC_WIDTH
COL_KVLAT = COL_QLAT + Q_LORA
MAIN_COLS = COL_KVLAT + KV_LORA
TCOL_GLU = 0
TCOL_KR = 2 * CONF_WIDTH
TCOL_KRSW = TCOL_KR + HEAD_PAD
TAIL_COLS = TCOL_KRSW + HEAD_PAD

ROUTER_GROUP_LANE = N_EXPERTS

TM_INPROJ = 1024
TN_INPROJ = MAIN_COLS // 2
TN_SPLIT = 1536
TM_QKV = 512
T_ATTN = 512
HEADS_PER_STEP = 4
TM_MIX = 512
CONV_CHUNK = 64
HALO_C = 32
HALO_A = 16
TM_ROUTE = 512
TM_EXP = 512
EXPERT_PREFETCH = 2
EXPERT_SLOTS = EXPERT_PREFETCH + 1
TM_DISP = 512
TM_COMB = 256
TOP_K = 2
MAX_TILES = TOKENS * TOP_K // TM_EXP + N_EXPERTS
SLAB = D_MODEL // LANES
GATHER_UNROLL = 16
ROW_CHAINS = 2
N_DMA_PRIORITIES = 2
META_E1, META_E2, META_R1, META_R2, META_W1, META_W2 = range(6)
META_ROWS = 8
SORTED_ROWS = MAX_TILES * TM_EXP * SLAB


def _cparams(semantics, vmem_mb):
    return pltpu.CompilerParams(dimension_semantics=semantics, vmem_limit_bytes=vmem_mb * 1024 * 1024)


def _sigmoid(x):
    return 1.0 / (1.0 + jnp.exp(-x))


def _rms(x, g):
    return x * lax.rsqrt(jnp.mean(x * x, axis=-1, keepdims=True) + EPS) * g


def _place(x, onehot):
    x1 = x.astype(BF16)
    r1 = x - x1.astype(F32)
    x2 = r1.astype(BF16)
    x3 = (r1 - x2.astype(F32)).astype(BF16)
    return (jnp.dot(x1, onehot, preferred_element_type=F32) + jnp.dot(x2, onehot, preferred_element_type=F32)
            + jnp.dot(x3, onehot, preferred_element_type=F32))


def _rope_kernel(pos_ref, freq_ref, c_ref, s_ref):
    half = QK_ROPE // 2
    per_row = LANES // half
    rows = TOKENS // per_row
    ang = pos_ref[...].astype(F32) * freq_ref[...]
    cos = jnp.cos(ang)
    sin = jnp.sin(ang)
    src = lax.broadcasted_iota(jnp.int32, (LANES, LANES), 0)
    dst = lax.broadcasted_iota(jnp.int32, (LANES, LANES), 1)
    lane = lax.broadcasted_iota(jnp.int32, (1, LANES), 1)
    ones_nope = jnp.where(lane < QK_NOPE, 1.0, 0.0)
    sign = jnp.where(lane < QK_NOPE + half, -1.0, 1.0)
    for j in range(per_row):
        f = src - half * j
        hit = (dst == QK_NOPE + f) | (dst == QK_NOPE + half + f)
        onehot = jnp.where((f >= 0) & (f < half) & hit, 1.0, 0.0).astype(BF16)
        c_ref[pl.ds(j, rows, stride=per_row), :] = _place(cos, onehot) + ones_nope
        s_ref[pl.ds(j, rows, stride=per_row), :] = _place(sin, onehot) * sign


def _rope_tables(positions):
    half = QK_ROPE // 2
    inv_freq = ROPE_THETA ** (-jnp.arange(0, QK_ROPE, 2, dtype=F32) / QK_ROPE)
    rows = TOKENS * half // LANES
    pos_rep = jnp.broadcast_to(positions.reshape(TOKENS, 1), (TOKENS, half)).reshape(rows, LANES)
    freq = jnp.tile(inv_freq, LANES // half).reshape(1, LANES)
    return pl.pallas_call(
        _rope_kernel,
        out_shape=(jax.ShapeDtypeStruct((TOKENS, HEAD_PAD), F32),) * 2,
        compiler_params=pltpu.CompilerParams(vmem_limit_bytes=48 * 1024 * 1024),
        name="rope_tables",
    )(pos_rep, freq)


def _inproj_kernel(x_ref, g_ref, wm_ref, wt_ref, om_ref, ot_ref, h_ref):
    j = pl.program_id(1)
    n_main = MAIN_COLS // TN_INPROJ

    @pl.when(j == 0)
    def _():
        h_ref[...] = _rms(x_ref[...], g_ref[...]).astype(BF16)

    @pl.when(j < n_main)
    def _():
        for lo, hi in ((0, TN_SPLIT), (TN_SPLIT, TN_INPROJ)):
            om_ref[:, lo:hi] = jnp.dot(h_ref[...], wm_ref[:, lo:hi], preferred_element_type=F32).astype(BF16)

    @pl.when(j == n_main)
    def _():
        ot_ref[...] = jnp.dot(h_ref[...], wt_ref[...], preferred_element_type=F32).astype(BF16)


def _inproj(x, g, w_main, w_tail):
    tm, tn = TM_INPROJ, TN_INPROJ
    n_main = MAIN_COLS // tn
    main_col = lambda i, j: jnp.minimum(j, n_main - 1)
    return pl.pallas_call(
        _inproj_kernel,
        grid=(TOKENS // tm, n_main + 1),
        in_specs=[
            pl.BlockSpec((tm, D_MODEL), lambda i, j: (i, 0)),
            pl.BlockSpec((1, D_MODEL), lambda i, j: (0, 0)),
            pl.BlockSpec((D_MODEL, tn), lambda i, j: (0, main_col(i, j))),
            pl.BlockSpec((D_MODEL, TAIL_COLS), lambda i, j: (0, 0)),
        ],
        out_specs=[
            pl.BlockSpec((tm, tn), lambda i, j: (i, main_col(i, j))),
            pl.BlockSpec((tm, TAIL_COLS), lambda i, j: (i, 0)),
        ],
        out_shape=(jax.ShapeDtypeStruct((TOKENS, MAIN_COLS), BF16), jax.ShapeDtypeStruct((TOKENS, TAIL_COLS), BF16)),
        scratch_shapes=[pltpu.VMEM((tm, D_MODEL), BF16)],
        compiler_params=_cparams(("parallel", "arbitrary"), 56),
        name="in_proj",
    )(x, g, w_main, w_tail)


def _qkv_kernel(ql_ref, kvl_ref, kr_ref, krsw_ref, c_ref, s_ref, qg_ref, kvg_ref, wq_ref, wkv_ref,
                q_out, k_out, v_out):
    c = c_ref[...]
    s = s_ref[...]
    width = N_HEADS * HEAD_PAD
    half = QK_ROPE // 2
    low_half = lax.broadcasted_iota(jnp.int32, c.shape, 1) < QK_NOPE + half

    def swap_halves(x):
        return jnp.where(low_half, pltpu.roll(x, HEAD_PAD - half, axis=1), pltpu.roll(x, half, axis=1))

    qn = _rms(ql_ref[...].astype(F32), qg_ref[...]).astype(BF16)
    qq = jnp.dot(qn, wq_ref[...], preferred_element_type=F32)
    for h in range(N_HEADS):
        lo, hi = h * HEAD_PAD, (h + 1) * HEAD_PAD
        q_out[:, lo:hi] = (qq[:, lo:hi] * c + swap_halves(qq[:, lo:hi]) * s).astype(BF16)
    kvn = _rms(kvl_ref[...].astype(F32), kvg_ref[...]).astype(BF16)
    kk = jnp.dot(kvn, wkv_ref[...], preferred_element_type=F32)
    kr = kr_ref[...].astype(F32) * c + krsw_ref[...].astype(F32) * s
    for h in range(N_HEADS):
        lo, hi = h * HEAD_PAD, (h + 1) * HEAD_PAD
        k_out[:, lo:hi] = (kk[:, lo:hi] + kr).astype(BF16)
    v_out[...] = kk[:, width:].astype(BF16)


def _qkv(proj, tail, c_tab, s_tab, qg, kvg, wq, wkv):
    tm = TM_QKV
    width = N_HEADS * HEAD_PAD
    row = lambda blk: (lambda i: (i, blk))
    const = lambda i: (0, 0)
    return pl.pallas_call(
        _qkv_kernel,
        grid=(TOKENS // tm,),
        in_specs=[
            pl.BlockSpec((tm, Q_LORA), row(COL_QLAT // Q_LORA)),
            pl.BlockSpec((tm, KV_LORA), row(COL_KVLAT // KV_LORA)),
            pl.BlockSpec((tm, HEAD_PAD), row(TCOL_KR // HEAD_PAD)),
            pl.BlockSpec((tm, HEAD_PAD), row(TCOL_KRSW // HEAD_PAD)),
            pl.BlockSpec((tm, HEAD_PAD), row(0)),
            pl.BlockSpec((tm, HEAD_PAD), row(0)),
            pl.BlockSpec((1, Q_LORA), const),
            pl.BlockSpec((1, KV_LORA), const),
            pl.BlockSpec((Q_LORA, width), const),
            pl.BlockSpec((KV_LORA, 2 * width), const),
        ],
        out_specs=[pl.BlockSpec((tm, width), row(0))] * 3,
        out_shape=(jax.ShapeDtypeStruct((TOKENS, width), BF16),) * 3,
        compiler_params=_cparams(("parallel",), 48),
        name="qkv_prep",
    )(proj, proj, tail, tail, c_tab, s_tab, qg, kvg, wq, wkv)


def _attn_kernel(q_ref, k_ref, v_ref, o_ref, *state):
    t = T_ATTN
    nh = HEADS_PER_STEP
    qi = pl.program_id(2)
    nt = (((1,), (1,)), ((), ()))
    m_sc, l_sc, acc_sc = state[0:nh], state[nh:2 * nh], state[2 * nh:3 * nh]
    for h in range(nh):
        m_sc[h][...] = jnp.full((t, LANES), -jnp.inf, F32)
        l_sc[h][...] = jnp.zeros((t, LANES), F32)
        acc_sc[h][...] = jnp.zeros((t, LANES), F32)

    def block(j, r0, nr, c0, nc, masked):
        start = pl.multiple_of(j * t + c0, nc)
        rows = slice(r0, r0 + nr)
        for h in range(nh):
            lo, hi = h * HEAD_PAD, (h + 1) * HEAD_PAD
            s = lax.dot_general(q_ref[rows, lo:hi], k_ref[pl.ds(start, nc), lo:hi], nt, preferred_element_type=F32)
            if masked:
                row_id = r0 + lax.broadcasted_iota(jnp.int32, (nr, nc), 0)
                col_id = c0 + lax.broadcasted_iota(jnp.int32, (nr, nc), 1)
                s = jnp.where(row_id >= col_id, s, -jnp.inf)
            blocks = [s[:, c * LANES:(c + 1) * LANES] for c in range(nc // LANES)]
            bmax = functools.reduce(jnp.maximum, blocks)
            m_old = m_sc[h][rows, :]
            m_new = jnp.maximum(m_old, jnp.max(bmax, axis=-1, keepdims=True))
            alpha = jnp.exp2(m_old - m_new)
            ps = [jnp.exp2(b - m_new) for b in blocks]
            p = jnp.concatenate(ps, axis=1).astype(BF16)
            l_sc[h][rows, :] = alpha * l_sc[h][rows, :] + functools.reduce(jnp.add, ps)
            acc_sc[h][rows, :] = alpha * acc_sc[h][rows, :] + jnp.dot(p, v_ref[pl.ds(start, nc), lo:hi],
                                                                       preferred_element_type=F32)
            m_sc[h][rows, :] = m_new

    def body(j, carry):
        block(j, 0, t, 0, t, False)
        return carry

    lax.fori_loop(0, qi, body, 0)
    half = t // 2
    block(qi, 0, t, 0, half, True)
    block(qi, half, half, half, half, True)
    out = [acc_sc[h][...] / jnp.sum(l_sc[h][...], axis=-1, keepdims=True) for h in range(nh)]
    o_ref[...] = jnp.concatenate([out[h] + out[h + 1] for h in range(0, nh, 2)], axis=1).astype(BF16)


def _attention(q, k, v):
    t = T_ATTN
    nq = SEQ // t
    nh = HEADS_PER_STEP
    return pl.pallas_call(
        _attn_kernel,
        grid=(BATCH, N_HEADS // nh, nq),
        in_specs=[
            pl.BlockSpec((t, nh * HEAD_PAD), lambda b, hg, i: (b * nq + i, hg)),
            pl.BlockSpec((SEQ, nh * HEAD_PAD), lambda b, hg, i: (b, hg)),
            pl.BlockSpec((SEQ, nh * HEAD_PAD), lambda b, hg, i: (b, hg)),
        ],
        out_specs=pl.BlockSpec((t, nh * V_HEAD), lambda b, hg, i: (b * nq + i, hg)),
        out_shape=jax.ShapeDtypeStruct((TOKENS, N_HEADS * V_HEAD), BF16),
        scratch_shapes=[pltpu.VMEM((t, LANES), F32)] * (3 * nh),
        compiler_params=_cparams(("parallel", "parallel", "arbitrary"), 48),
        name="mla_attention",
    )(q, k, v)


def _mixer_tail_kernel(gates_ref, sc_ref, sch_ref, gv_ref, gg_ref, gvh_ref, ggh_ref, att_ref, x_ref,
                       cwa_ref, woa_ref, cwc_ref, lng_ref, lnb_ref, woc_ref, wob_ref, wo_ref,
                       out_ref, cbuf, ubuf, shifted, vbuf):
    tm = TM_MIX
    has_past = (pl.program_id(0) % (SEQ // tm)) != 0

    sc = sc_ref[...]
    sc_b = sc[:, 0:SC_WIDTH].astype(F32)
    cbuf[8:8 + tm, :] = sc[:, SC_WIDTH:2 * SC_WIDTH].astype(F32) * sc[:, 2 * SC_WIDTH:].astype(F32)
    sch = sch_ref[...].astype(F32)[HALO_A - 8:HALO_A, :]
    cbuf[0:8, :] = jnp.where(has_past, sch[:, SC_WIDTH:2 * SC_WIDTH] * sch[:, 2 * SC_WIDTH:], 0.0)
    conv_a = cwa_ref[0:1, :] * cbuf[6:6 + tm, :]
    for t in range(1, SC_KERNEL):
        conv_a = conv_a + cwa_ref[t:t + 1, :] * cbuf[6 + t:6 + t + tm, :]
    y_a = jnp.dot((sc_b * conv_a).astype(BF16), woa_ref[...], preferred_element_type=F32)

    ubuf[HALO_C:HALO_C + tm, :] = gv_ref[...].astype(F32) * _sigmoid(gg_ref[...].astype(F32))
    ubuf[0:HALO_C, :] = jnp.where(has_past, gvh_ref[...].astype(F32) * _sigmoid(ggh_ref[...].astype(F32)), 0.0)
    rows = tm + HALO_C - 8
    u_all = ubuf[...]
    for b in range(1, 8):
        shifted[b - 1, 0:rows, :] = pltpu.roll(u_all, tm + HALO_C - b, axis=0)[0:rows, :]
    base = HALO_C - (CONF_KERNEL - 1)
    for r0 in range(0, tm, CONV_CHUNK):
        acc = None
        for t in range(CONF_KERNEL):
            off = base + t
            a0 = r0 + off - off % 8
            src = ubuf[a0:a0 + CONV_CHUNK, :] if off % 8 == 0 else shifted[off % 8 - 1, a0:a0 + CONV_CHUNK, :]
            term = cwc_ref[t:t + 1, :] * src
            acc = term if acc is None else acc + term
        mu = jnp.mean(acc, axis=-1, keepdims=True)
        xc = acc - mu
        var = jnp.mean(xc * xc, axis=-1, keepdims=True)
        y = xc * lax.rsqrt(var + EPS) * lng_ref[...] + lnb_ref[...]
        vbuf[r0:r0 + CONV_CHUNK, :] = (y * _sigmoid(y)).astype(BF16)
    y_c = jnp.dot(vbuf[...], woc_ref[...], preferred_element_type=F32)

    y_b = jnp.dot(att_ref[...], wob_ref[...], preferred_element_type=F32)

    g = gates_ref[...]
    merged = (_sigmoid(g[:, 0:D_MODEL].astype(F32)) * y_a
              + _sigmoid(g[:, D_MODEL:2 * D_MODEL].astype(F32)) * y_b
              + _sigmoid(g[:, 2 * D_MODEL:].astype(F32)) * y_c)
    out_ref[...] = x_ref[...] + jnp.dot(merged.astype(BF16), wo_ref[...], preferred_element_type=F32)


def _mixer_tail(proj, tail, att, x, cwa, woa, cwc, lng, lnb, woc, wob, wo):
    tm = TM_MIX
    row = lambda width, col: pl.BlockSpec((tm, width), lambda i: (i, col // width))
    halo = lambda rows, width, col: pl.BlockSpec(
        (rows, width), lambda i: (jnp.maximum(i * (tm // rows) - 1, 0), col // width))
    const = lambda a: pl.BlockSpec(a.shape, lambda i: (0,) * a.ndim)
    weights = (cwa, woa, cwc, lng, lnb, woc, wob, wo)
    return pl.pallas_call(
        _mixer_tail_kernel,
        grid=(TOKENS // tm,),
        in_specs=[
            row(3 * D_MODEL, COL_GATES),
            row(3 * SC_WIDTH, COL_SC),
            halo(HALO_A, 3 * SC_WIDTH, COL_SC),
            row(CONF_WIDTH, TCOL_GLU),
            row(CONF_WIDTH, TCOL_GLU + CONF_WIDTH),
            halo(HALO_C, CONF_WIDTH, TCOL_GLU),
            halo(HALO_C, CONF_WIDTH, TCOL_GLU + CONF_WIDTH),
            pl.BlockSpec((tm, N_HEADS * V_HEAD), lambda i: (i, 0)),
            pl.BlockSpec((tm, D_MODEL), lambda i: (i, 0)),
        ] + [const(a) for a in weights],
        out_specs=pl.BlockSpec((tm, D_MODEL), lambda i: (i, 0)),
        out_shape=jax.ShapeDtypeStruct((TOKENS, D_MODEL), F32),
        scratch_shapes=[
            pltpu.VMEM((tm + 8, SC_WIDTH), F32),
            pltpu.VMEM((tm + HALO_C, CONF_WIDTH), F32),
            pltpu.VMEM((7, tm + HALO_C - 8, CONF_WIDTH), F32),
            pltpu.VMEM((tm, CONF_WIDTH), BF16),
        ],
        compiler_params=_cparams(("parallel",), 56),
        name="mixer_tail",
    )(proj, proj, proj, tail, tail, tail, tail, att, x, *weights)


def _route(logits):
    lane = lax.broadcasted_iota(jnp.int32, logits.shape, 1)
    lane_f = lane.astype(F32)
    neg = -jnp.inf
    big = float(LANES)
    is_grp = (lane >= ROUTER_GROUP_LANE) & (lane < ROUTER_GROUP_LANE + N_GROUPS)
    glog = jnp.where(is_grp, logits, neg)
    gmax = jnp.max(glog, axis=-1, keepdims=True)
    gidx = jnp.min(jnp.where(glog == gmax, lane_f, big), axis=-1, keepdims=True)
    p_sel = 1.0 / jnp.sum(jnp.exp(glog - gmax), axis=-1, keepdims=True)
    first = (gidx - ROUTER_GROUP_LANE) * EXPERTS_PER_GROUP
    in_grp = (lane_f >= first) & (lane_f < first + EXPERTS_PER_GROUP)
    el = jnp.where(in_grp, logits, neg)
    m1 = jnp.max(el, axis=-1, keepdims=True)
    i1 = jnp.min(jnp.where(el == m1, lane_f, big), axis=-1, keepdims=True)
    el2 = jnp.where(lane_f == i1, neg, el)
    m2 = jnp.max(el2, axis=-1, keepdims=True)
    i2 = jnp.min(jnp.where(el2 == m2, lane_f, big), axis=-1, keepdims=True)
    e2 = jnp.exp(m2 - m1)
    w1 = p_sel / (1.0 + e2)
    w2 = w1 * e2
    return i1, i2, w1, w2


def _router_kernel(x_ref, g_ref, wr_ref, br_ref, meta_ref, meta_t_ref, cnt_ref, xs0_ref, run_ref):
    tm = TM_ROUTE

    @pl.when(pl.program_id(0) == 0)
    def _():
        run_ref[...] = jnp.zeros_like(run_ref)

    h = _rms(x_ref[...], g_ref[...])
    h_hi = h.astype(BF16)
    h_lo = (h - h_hi.astype(F32)).astype(BF16)
    hi_terms = jnp.dot(h_hi, wr_ref[...], preferred_element_type=F32)
    logits = (hi_terms[:, :LANES] + hi_terms[:, LANES:]
              + jnp.dot(h_lo, wr_ref[:, :LANES], preferred_element_type=F32) + br_ref[...])
    i1, i2, w1, w2 = _route(logits)
    lane = lax.broadcasted_iota(jnp.int32, (tm, LANES), 1)
    lane_f = lane.astype(F32)
    oh1 = lane_f == i1
    oh2 = lane_f == i2
    onehot = jnp.where(oh1, 1.0, 0.0) + jnp.where(oh2, 1.0, 0.0)
    row_id = lax.broadcasted_iota(jnp.int32, (tm, tm), 0)
    col_id = lax.broadcasted_iota(jnp.int32, (tm, tm), 1)
    below = jnp.where(row_id > col_id, 1.0, 0.0).astype(BF16)
    before = run_ref[...] + jnp.dot(below, onehot.astype(BF16), preferred_element_type=F32)
    r1 = jnp.sum(jnp.where(oh1, before, 0.0), axis=-1, keepdims=True)
    r2 = jnp.sum(jnp.where(oh2, before, 0.0), axis=-1, keepdims=True)
    run_ref[...] += jnp.sum(onehot, axis=0, keepdims=True)
    cnt_ref[...] = run_ref[...]
    meta = jnp.zeros((tm, LANES), F32)
    for col, val in enumerate((i1, i2, r1, r2, w1, w2)):
        meta = jnp.where(lane == col, val, meta)
    meta_ref[...] = meta
    meta_t_ref[...] = meta.T[0:META_ROWS, :]
    xs0_ref[...] = jnp.zeros_like(xs0_ref)


def _router(x, g, wr, br):
    tm = TM_ROUTE
    const = lambda i: (0, 0)
    return pl.pallas_call(
        _router_kernel,
        grid=(TOKENS // tm,),
        in_specs=[
            pl.BlockSpec((tm, D_MODEL), lambda i: (i, 0)),
            pl.BlockSpec((1, D_MODEL), const),
            pl.BlockSpec((D_MODEL, 2 * LANES), const),
            pl.BlockSpec((1, LANES), const),
        ],
        out_specs=[
            pl.BlockSpec((tm, LANES), lambda i: (i, 0)),
            pl.BlockSpec((META_ROWS, tm), lambda i: (0, i)),
            pl.BlockSpec((1, LANES), const),
            pl.BlockSpec((SORTED_ROWS // (TOKENS // tm), LANES), lambda i: (i, 0)),
        ],
        out_shape=(
            jax.ShapeDtypeStruct((TOKENS, LANES), F32),
            jax.ShapeDtypeStruct((META_ROWS, TOKENS), F32),
            jax.ShapeDtypeStruct((1, LANES), F32),
            jax.ShapeDtypeStruct((SORTED_ROWS, LANES), F32),
        ),
        scratch_shapes=[pltpu.VMEM((1, LANES), F32)],
        compiler_params=_cparams(("arbitrary",), 48),
        name="moe_router",
    )(x, g, wr, br)


def _dispatch_plan(meta_t, counts):
    e1 = meta_t[META_E1].astype(jnp.int32)
    e2 = meta_t[META_E2].astype(jnp.int32)
    r1 = meta_t[META_R1].astype(jnp.int32)
    r2 = meta_t[META_R2].astype(jnp.int32)
    cnt = counts[0, :N_EXPERTS].astype(jnp.int32)
    tiles = (cnt + TM_EXP - 1) // TM_EXP
    tile_end = jnp.cumsum(tiles)
    first_slot = ((tile_end - tiles) * TM_EXP)[:, None]
    expert = jnp.arange(N_EXPERTS, dtype=jnp.int32)[:, None]
    pos1 = jnp.sum(jnp.where(e1[None, :] == expert, first_slot, 0), axis=0) + r1
    pos2 = jnp.sum(jnp.where(e2[None, :] == expert, first_slot, 0), axis=0) + r2
    n_tiles = tile_end[-1:]
    tile_id = jnp.minimum(jnp.arange(MAX_TILES, dtype=jnp.int32), n_tiles - 1)
    tile_expert = jnp.sum((tile_id[:, None] >= tile_end[None, :]).astype(jnp.int32), axis=1)
    return pos1, pos2, tile_expert, n_tiles


def _dispatch_kernel(pos1_ref, pos2_ref, x_ref, g_ref, xs_in_ref, xs_ref, slab, sem):
    del xs_in_ref
    tm = TM_DISP
    i = pl.program_id(0)
    last = pl.num_programs(0) - 1
    slot = i % 2

    def wait_copies(sl):
        for _ in range(TOP_K):
            pltpu.make_async_copy(slab.at[sl], xs_ref.at[pl.ds(0, tm * SLAB), :], sem.at[sl]).wait()

    @pl.when(i >= 2)
    def _():
        wait_copies(slot)

    h = _rms(x_ref[...], g_ref[...])
    for s in range(SLAB):
        slab[slot, pl.ds(s, tm, stride=SLAB), :] = h[:, s * LANES:(s + 1) * LANES]

    def body(c, carry):
        for u in range(GATHER_UNROLL):
            r = c * GATHER_UNROLL + u
            src = slab.at[slot, pl.ds(pl.multiple_of(r * SLAB, SLAB), SLAB), :]
            for k, pos_ref in enumerate((pos1_ref, pos2_ref)):
                p = pos_ref[i * tm + r]
                pltpu.make_async_copy(src, xs_ref.at[pl.ds(pl.multiple_of(p * SLAB, SLAB), SLAB), :],
                                      sem.at[slot]).start(priority=k % N_DMA_PRIORITIES)
        return carry

    lax.fori_loop(0, tm // GATHER_UNROLL, body, 0)

    @pl.when(i == last)
    def _():
        wait_copies(1 - slot)
        wait_copies(slot)


def _dispatch(pos1, pos2, x, g, xs0):
    tm = TM_DISP
    return pl.pallas_call(
        _dispatch_kernel,
        grid_spec=pltpu.PrefetchScalarGridSpec(
            num_scalar_prefetch=2,
            grid=(TOKENS // tm,),
            in_specs=[
                pl.BlockSpec((tm, D_MODEL), lambda i, p1, p2: (i, 0)),
                pl.BlockSpec((1, D_MODEL), lambda i, p1, p2: (0, 0)),
                pl.BlockSpec(memory_space=pl.ANY),
            ],
            out_specs=pl.BlockSpec(memory_space=pl.ANY),
            scratch_shapes=[pltpu.VMEM((2, tm * SLAB, LANES), F32), pltpu.SemaphoreType.DMA((2,))],
        ),
        out_shape=jax.ShapeDtypeStruct((SORTED_ROWS, LANES), F32),
        input_output_aliases={4: 0},
        compiler_params=_cparams(("arbitrary",), 48),
        name="moe_dispatch",
    )(pos1, pos2, x, g, xs0)


def _expert_kernel(te_ref, nt_ref, xs_ref, wg_ref, wu_ref, wd_ref, ys_ref, xbuf, sem):
    del te_ref
    tm = TM_EXP
    i = pl.program_id(0)
    n = nt_ref[0]

    def tile_copy(tile):
        slot = tile % EXPERT_SLOTS
        rows = pl.ds(pl.multiple_of(tile * (tm * SLAB), tm * SLAB), tm * SLAB)
        return pltpu.make_async_copy(xs_ref.at[rows, :], xbuf.at[slot], sem.at[slot])

    @pl.when(i == 0)
    def _():
        for ahead in range(EXPERT_PREFETCH):
            @pl.when(ahead < n)
            def _():
                tile_copy(ahead).start()

    @pl.when(i + EXPERT_PREFETCH < n)
    def _():
        tile_copy(i + EXPERT_PREFETCH).start()

    @pl.when(i < n)
    def _():
        tile_copy(i).wait()
        slot = i % EXPERT_SLOTS
        wg = wg_ref[...].astype(BF16)
        wu = wu_ref[...].astype(BF16)
        wd = wd_ref[...].astype(BF16)
        rows = tm // ROW_CHAINS
        for c in range(ROW_CHAINS):
            first = c * rows * SLAB
            xt = jnp.concatenate([xbuf[slot, pl.ds(first + s, rows, stride=SLAB), :].astype(BF16)
                                  for s in range(SLAB)], axis=1)
            hg = jnp.dot(xt, wg, preferred_element_type=F32)
            hu = jnp.dot(xt, wu, preferred_element_type=F32)
            hh = (hg * _sigmoid(hg) * hu).astype(BF16)
            y = jnp.dot(hh, wd, preferred_element_type=F32)
            for s in range(SLAB):
                ys_ref[pl.ds(first + s, rows, stride=SLAB), :] = y[:, s * LANES:(s + 1) * LANES]

    @pl.when(i >= n)
    def _():
        ys_ref[...] = jnp.zeros_like(ys_ref)


def _experts(layer, tile_expert, n_tiles, xs, w_gate, w_up, w_down):
    tm = TM_EXP
    wspec = lambda rows, cols: pl.BlockSpec((None, None, rows, cols), lambda i, te, nt: (layer, te[i], 0, 0))
    return pl.pallas_call(
        _expert_kernel,
        grid_spec=pltpu.PrefetchScalarGridSpec(
            num_scalar_prefetch=2,
            grid=(MAX_TILES,),
            in_specs=[
                pl.BlockSpec(memory_space=pl.ANY),
                wspec(D_MODEL, EXPERT_HIDDEN),
                wspec(D_MODEL, EXPERT_HIDDEN),
                wspec(EXPERT_HIDDEN, D_MODEL),
            ],
            out_specs=pl.BlockSpec((tm * SLAB, LANES), lambda i, te, nt: (i, 0)),
            scratch_shapes=[pltpu.VMEM((EXPERT_SLOTS, tm * SLAB, LANES), F32), pltpu.SemaphoreType.DMA((EXPERT_SLOTS,))],
        ),
        out_shape=jax.ShapeDtypeStruct((SORTED_ROWS, LANES), F32),
        compiler_params=_cparams(("arbitrary",), 48),
        name="moe_experts",
    )(tile_expert, n_tiles, xs, w_gate, w_up, w_down)


def _start_slab_gathers(idx_refs, base, n_rows, src_hbm, dst_bufs, sem):
    def body(c, carry):
        for u in range(GATHER_UNROLL):
            r = c * GATHER_UNROLL + u
            for k, (idx_ref, dst) in enumerate(zip(idx_refs, dst_bufs)):
                t = idx_ref[base + r]
                pltpu.make_async_copy(src_hbm.at[pl.ds(pl.multiple_of(t * SLAB, SLAB), SLAB), :],
                                      dst.at[pl.ds(pl.multiple_of(r * SLAB, SLAB), SLAB), :],
                                      sem).start(priority=k % N_DMA_PRIORITIES)
        return carry

    lax.fori_loop(0, n_rows // GATHER_UNROLL, body, 0)


def _wait_slab_gathers(n_rows, src_hbm, dst, sem):
    pltpu.make_async_copy(src_hbm.at[pl.ds(0, n_rows * SLAB), :], dst, sem).wait()


def _combine_ple_kernel(pos1_ref, pos2_ref, x_ref, meta_ref, ys_ref, p_ref, g_ref, wg_ref, wp_ref, fg_ref,
                        out_ref, cbuf, sem, *, final):
    tm = TM_COMB
    i = pl.program_id(0)

    def gather(tile, slot):
        _start_slab_gathers((pos1_ref, pos2_ref), tile * tm, tm, ys_ref, (cbuf.at[slot, 0], cbuf.at[slot, 1]),
                            sem.at[slot])

    @pl.when(i == 0)
    def _():
        gather(0, 0)

    @pl.when(i + 1 < pl.num_programs(0))
    def _():
        gather(i + 1, (i + 1) % 2)

    slot = i % 2
    for k in range(2):
        _wait_slab_gathers(tm, ys_ref, cbuf.at[slot, k], sem.at[slot])
    n_rows = tm // ROW_CHAINS
    for c in range(ROW_CHAINS):
        rows = slice(c * n_rows, (c + 1) * n_rows)
        first = c * n_rows * SLAB
        meta = meta_ref[rows, :]
        w1 = meta[:, META_W1:META_W1 + 1]
        w2 = meta[:, META_W2:META_W2 + 1]
        moe = jnp.concatenate([w1 * cbuf[slot, 0, pl.ds(first + s, n_rows, stride=SLAB), :]
                               + w2 * cbuf[slot, 1, pl.ds(first + s, n_rows, stride=SLAB), :]
                               for s in range(SLAB)], axis=1)
        x = x_ref[rows, :] + moe
        h = _rms(x, g_ref[...]).astype(BF16)
        gate = _sigmoid(jnp.dot(h, wg_ref[...], preferred_element_type=F32))
        emb = jnp.dot(p_ref[rows, :].astype(BF16), wp_ref[...], preferred_element_type=F32)
        y = x + gate * emb
        if final:
            y = _rms(y, fg_ref[...])
        out_ref[rows, :] = y


def _combine_ple(layer, pos1, pos2, x, meta, ys, p, g, wg, wp, fg, final):
    tm = TM_COMB
    const = lambda i, p1, p2: (0, 0)
    rows = lambda i, p1, p2: (i, 0)
    return pl.pallas_call(
        functools.partial(_combine_ple_kernel, final=final),
        grid_spec=pltpu.PrefetchScalarGridSpec(
            num_scalar_prefetch=2,
            grid=(TOKENS // tm,),
            in_specs=[
                pl.BlockSpec((tm, D_MODEL), rows),
                pl.BlockSpec((tm, LANES), rows),
                pl.BlockSpec(memory_space=pl.ANY),
                pl.BlockSpec((None, tm, PLE_DIM), lambda i, p1, p2: (layer, i, 0)),
                pl.BlockSpec((1, D_MODEL), const),
                pl.BlockSpec((D_MODEL, D_MODEL), const),
                pl.BlockSpec((PLE_DIM, D_MODEL), const),
                pl.BlockSpec((1, D_MODEL), const),
            ],
            out_specs=pl.BlockSpec((tm, D_MODEL), rows),
            scratch_shapes=[pltpu.VMEM((2, 2, tm * SLAB, LANES), F32), pltpu.SemaphoreType.DMA((2,))],
        ),
        out_shape=jax.ShapeDtypeStruct((TOKENS, D_MODEL), F32),
        compiler_params=_cparams(("arbitrary",), 48),
        name="moe_combine_ple",
    )(pos1, pos2, x, meta, ys, p, g, wg, wp, fg)


def _in_proj_weights(w_in):
    k_rope = w_in[:, MAIN_COLS:MAIN_COLS + QK_ROPE]
    glu = w_in[:, MAIN_COLS + QK_ROPE:]
    half = QK_ROPE // 2
    zn = jnp.zeros((D_MODEL, QK_NOPE), F32)
    zp = jnp.zeros((D_MODEL, HEAD_PAD - QK_NOPE - QK_ROPE), F32)
    tail = jnp.concatenate([glu, zn, k_rope, zp, zn, k_rope[:, half:], k_rope[:, :half], zp], axis=1)
    return w_in[:, :MAIN_COLS].astype(BF16), tail.astype(BF16)


def _q_weight(w_uq):
    scale = (QK_NOPE + QK_ROPE) ** -0.5 * LOG2_E
    w = (w_uq * scale).reshape(Q_LORA, N_HEADS, QK_NOPE + QK_ROPE)
    zp = jnp.zeros((Q_LORA, N_HEADS, HEAD_PAD - QK_NOPE - QK_ROPE), F32)
    return jnp.concatenate([w, zp], axis=2).reshape(Q_LORA, N_HEADS * HEAD_PAD).astype(BF16)


def _kv_weight(w_ukv):
    w = w_ukv.reshape(KV_LORA, N_HEADS, QK_NOPE + V_HEAD)
    k_nope, v = w[:, :, :QK_NOPE], w[:, :, QK_NOPE:]
    z = jnp.zeros_like(v)
    k_part = jnp.concatenate([k_nope, jnp.zeros_like(k_nope)], axis=2).reshape(KV_LORA, N_HEADS * HEAD_PAD)
    odd = (jnp.arange(N_HEADS) % 2 == 1)[None, :, None]
    v_part = jnp.concatenate([jnp.where(odd, z, v), jnp.where(odd, v, z)], axis=2).reshape(KV_LORA, N_HEADS * HEAD_PAD)
    return jnp.concatenate([k_part, v_part], axis=1).astype(BF16)


def _router_weight(w_rg, b_rg, w_re, b_re):
    pad = LANES - N_EXPERTS - N_GROUPS
    w = jnp.concatenate([w_re, w_rg, jnp.zeros((D_MODEL, pad), F32)], axis=1)
    b = jnp.concatenate([b_re, b_rg, jnp.zeros((pad,), F32)]).reshape(1, LANES)
    w_hi = w.astype(BF16)
    w_lo = (w - w_hi.astype(F32)).astype(BF16)
    return jnp.concatenate([w_hi, w_lo], axis=1), b


def kernel(x, p, positions, ln_mix_g, w_in, conv_a_w, w_out_a, q_norm_g, w_uq, kv_norm_g, w_ukv, w_out_b, conv_c_w, ln_c_g, ln_c_b, w_out_c, w_o, ln_ffn_g, w_route_grp, b_route_grp, w_route_exp, b_route_exp, w_exp_gate, w_exp_up, w_exp_down, ln_ple_g, w_ple_gate, w_ple, final_norm_g):
    c_tab, s_tab = _rope_tables(positions)
    xf = x.reshape(TOKENS, D_MODEL)
    pf = p.reshape(DEPTH, TOKENS, PLE_DIM)
    row = lambda a: a.reshape(1, -1)
    for i in range(DEPTH):
        proj, tail = _inproj(xf, row(ln_mix_g[i]), *_in_proj_weights(w_in[i]))
        q, k, v = _qkv(proj, tail, c_tab, s_tab, row(q_norm_g[i]), row(kv_norm_g[i]), _q_weight(w_uq[i]), _kv_weight(w_ukv[i]))
        att = _attention(q, k, v)
        xf = _mixer_tail(proj, tail, att, xf, conv_a_w[i], w_out_a[i].astype(BF16), conv_c_w[i], row(ln_c_g[i]),
                         row(ln_c_b[i]), w_out_c[i].astype(BF16), w_out_b[i].astype(BF16), w_o[i].astype(BF16))
        wr, br = _router_weight(w_route_grp[i], b_route_grp[i], w_route_exp[i], b_route_exp[i])
        meta, meta_t, counts, xs0 = _router(xf, row(ln_ffn_g[i]), wr, br)
        pos1, pos2, tile_expert, n_tiles = _dispatch_plan(meta_t, counts)
        xs = _dispatch(pos1, pos2, xf, row(ln_ffn_g[i]), xs0)
        ys = _experts(i, tile_expert, n_tiles, xs, w_exp_gate, w_exp_up, w_exp_down)
        xf = _combine_ple(i, pos1, pos2, xf, meta, ys, pf, row(ln_ple_g[i]), w_ple_gate[i].astype(BF16),
                          w_ple[i].astype(BF16), row(final_norm_g), final=(i == DEPTH - 1))
    return xf.reshape(BATCH, SEQ, D_MODEL)
```

```python
import functools

import jax
import jax.numpy as jnp
from jax import lax
from jax.experimental import pallas as pl
from jax.experimental.pallas import tpu as pltpu

D_MODEL = 1024
BATCH = 8
SEQ = 2048
DEPTH = 2
TOKENS = BATCH * SEQ
PLE_DIM = 256
SC_WIDTH = 512
SC_KERNEL = 3
N_HEADS = 8
QK_NOPE = 64
QK_ROPE = 32
V_HEAD = 64
Q_LORA = 768
KV_LORA = 256
ROPE_THETA = 10000.0
CONF_WIDTH = 512
CONF_KERNEL = 31
N_GROUPS = 4
EXPERTS_PER_GROUP = 8
N_EXPERTS = N_GROUPS * EXPERTS_PER_GROUP
EXPERT_HIDDEN = 256
EPS = 1e-6
LOG2_E = 1.4426950408889634

LANES = 128
HEAD_PAD = 128
F32 = jnp.float32
BF16 = jnp.bfloat16

COL_GATES = 0
COL_SC = 3 * D_MODEL
COL_QLAT = COL_SC + 3 * SC_WIDTH
COL_KVLAT = COL_QLAT + Q_LORA
MAIN_COLS = COL_KVLAT + KV_LORA
TCOL_GLU = 0
TCOL_KR = 2 * CONF_WIDTH
TCOL_KRSW = TCOL_KR + HEAD_PAD
TAIL_COLS = TCOL_KRSW + HEAD_PAD

ROUTER_GROUP_LANE = N_EXPERTS

TM_INPROJ = 1024
TN_INPROJ = MAIN_COLS // 2
TN_SPLIT = 1536
TM_QKV = 512
T_ATTN = 512
HEADS_PER_STEP = 8
TM_MIX = 512
CONV_CHUNK = 64
HALO_C = 32
HALO_A = 16
TM_ROUTE = 512
TM_EXP = 512
EXPERT_PREFETCH = 2
EXPERT_SLOTS = EXPERT_PREFETCH + 1
TM_DISP = 512
TM_COMB = 256
TOP_K = 2
MAX_TILES = TOKENS * TOP_K // TM_EXP + N_EXPERTS
SLAB = D_MODEL // LANES
GATHER_UNROLL = 16
N_DMA_PRIORITIES = 2
META_E1, META_E2, META_R1, META_R2, META_W1, META_W2 = range(6)
META_ROWS = 8
SORTED_ROWS = MAX_TILES * TM_EXP * SLAB


def _cparams(semantics, vmem_mb):
    return pltpu.CompilerParams(dimension_semantics=semantics, vmem_limit_bytes=vmem_mb * 1024 * 1024)


def _sigmoid(x):
    return 1.0 / (1.0 + jnp.exp(-x))


def _rms(x, g):
    return x * lax.rsqrt(jnp.mean(x * x, axis=-1, keepdims=True) + EPS) * g


def _place(x, onehot):
    x1 = x.astype(BF16)
    r1 = x - x1.astype(F32)
    x2 = r1.astype(BF16)
    x3 = (r1 - x2.astype(F32)).astype(BF16)
    return (jnp.dot(x1, onehot, preferred_element_type=F32) + jnp.dot(x2, onehot, preferred_element_type=F32)
            + jnp.dot(x3, onehot, preferred_element_type=F32))


def _rope_kernel(pos_ref, freq_ref, c_ref, s_ref):
    half = QK_ROPE // 2
    per_row = LANES // half
    rows = TOKENS // per_row
    ang = pos_ref[...].astype(F32) * freq_ref[...]
    cos = jnp.cos(ang)
    sin = jnp.sin(ang)
    src = lax.broadcasted_iota(jnp.int32, (LANES, LANES), 0)
    dst = lax.broadcasted_iota(jnp.int32, (LANES, LANES), 1)
    lane = lax.broadcasted_iota(jnp.int32, (1, LANES), 1)
    ones_nope = jnp.where(lane < QK_NOPE, 1.0, 0.0)
    sign = jnp.where(lane < QK_NOPE + half, -1.0, 1.0)
    for j in range(per_row):
        f = src - half * j
        hit = (dst == QK_NOPE + f) | (dst == QK_NOPE + half + f)
        onehot = jnp.where((f >= 0) & (f < half) & hit, 1.0, 0.0).astype(BF16)
        c_ref[pl.ds(j, rows, stride=per_row), :] = _place(cos, onehot) + ones_nope
        s_ref[pl.ds(j, rows, stride=per_row), :] = _place(sin, onehot) * sign


def _rope_tables(positions):
    half = QK_ROPE // 2
    inv_freq = ROPE_THETA ** (-jnp.arange(0, QK_ROPE, 2, dtype=F32) / QK_ROPE)
    rows = TOKENS * half // LANES
    pos_rep = jnp.broadcast_to(positions.reshape(TOKENS, 1), (TOKENS, half)).reshape(rows, LANES)
    freq = jnp.tile(inv_freq, LANES // half).reshape(1, LANES)
    return pl.pallas_call(
        _rope_kernel,
        out_shape=(jax.ShapeDtypeStruct((TOKENS, HEAD_PAD), F32),) * 2,
        compiler_params=pltpu.CompilerParams(vmem_limit_bytes=48 * 1024 * 1024),
        name="rope_tables",
    )(pos_rep, freq)


def _inproj_kernel(x_ref, g_ref, wm_ref, wt_ref, om_ref, ot_ref):
    j = pl.program_id(0)
    n_main = MAIN_COLS // TN_INPROJ
    h = _rms(x_ref[...], g_ref[...]).astype(BF16)

    @pl.when(j < n_main)
    def _():
        for lo, hi in ((0, TN_SPLIT), (TN_SPLIT, TN_INPROJ)):
            om_ref[:, lo:hi] = jnp.dot(h, wm_ref[:, lo:hi], preferred_element_type=F32).astype(BF16)

    @pl.when(j == n_main)
    def _():
        ot_ref[...] = jnp.dot(h, wt_ref[...], preferred_element_type=F32).astype(BF16)


def _inproj(x, g, w_main, w_tail):
    tm, tn = TM_INPROJ, TN_INPROJ
    n_main = MAIN_COLS // tn
    n_tok = TOKENS // tm
    main_col = lambda j: jnp.minimum(j, n_main - 1)
    main_row = lambda j, i: jnp.where(j < n_main, i, n_tok - 1)
    tail_row = lambda j, i: jnp.where(j < n_main, 0, i)
    return pl.pallas_call(
        _inproj_kernel,
        grid=(n_main + 1, n_tok),
        in_specs=[
            pl.BlockSpec((tm, D_MODEL), lambda j, i: (i, 0)),
            pl.BlockSpec((1, D_MODEL), lambda j, i: (0, 0)),
            pl.BlockSpec((D_MODEL, tn), lambda j, i: (0, main_col(j))),
            pl.BlockSpec((D_MODEL, TAIL_COLS), lambda j, i: (0, 0)),
        ],
        out_specs=[
            pl.BlockSpec((tm, tn), lambda j, i: (main_row(j, i), main_col(j))),
            pl.BlockSpec((tm, TAIL_COLS), lambda j, i: (tail_row(j, i), 0)),
        ],
        out_shape=(jax.ShapeDtypeStruct((TOKENS, MAIN_COLS), BF16), jax.ShapeDtypeStruct((TOKENS, TAIL_COLS), BF16)),
        compiler_params=_cparams(("arbitrary", "arbitrary"), 56),
        name="in_proj",
    )(x, g, w_main, w_tail)


def _qkv_kernel(ql_ref, kvl_ref, kr_ref, krsw_ref, c_ref, s_ref, qg_ref, kvg_ref, wq_ref, wkv_ref,
                q_out, k_out, v_out):
    c = c_ref[...]
    s = s_ref[...]
    width = N_HEADS * HEAD_PAD
    half = QK_ROPE // 2
    low_half = lax.broadcasted_iota(jnp.int32, c.shape, 1) < QK_NOPE + half

    def swap_halves(x):
        return jnp.where(low_half, pltpu.roll(x, HEAD_PAD - half, axis=1), pltpu.roll(x, half, axis=1))

    qn = _rms(ql_ref[...].astype(F32), qg_ref[...]).astype(BF16)
    qq = jnp.dot(qn, wq_ref[...], preferred_element_type=F32)
    for h in range(N_HEADS):
        lo, hi = h * HEAD_PAD, (h + 1) * HEAD_PAD
        q_out[:, lo:hi] = (qq[:, lo:hi] * c + swap_halves(qq[:, lo:hi]) * s).astype(BF16)
    kvn = _rms(kvl_ref[...].astype(F32), kvg_ref[...]).astype(BF16)
    kk = jnp.dot(kvn, wkv_ref[...], preferred_element_type=F32)
    kr = kr_ref[...].astype(F32) * c + krsw_ref[...].astype(F32) * s
    for h in range(N_HEADS):
        lo, hi = h * HEAD_PAD, (h + 1) * HEAD_PAD
        k_out[:, lo:hi] = (kk[:, lo:hi] + kr).astype(BF16)
    v_out[...] = kk[:, width:].astype(BF16)


def _qkv(proj, tail, c_tab, s_tab, qg, kvg, wq, wkv):
    tm = TM_QKV
    width = N_HEADS * HEAD_PAD
    row = lambda blk: (lambda i: (i, blk))
    const = lambda i: (0, 0)
    return pl.pallas_call(
        _qkv_kernel,
        grid=(TOKENS // tm,),
        in_specs=[
            pl.BlockSpec((tm, Q_LORA), row(COL_QLAT // Q_LORA)),
            pl.BlockSpec((tm, KV_LORA), row(COL_KVLAT // KV_LORA)),
            pl.BlockSpec((tm, HEAD_PAD), row(TCOL_KR // HEAD_PAD)),
            pl.BlockSpec((tm, HEAD_PAD), row(TCOL_KRSW // HEAD_PAD)),
            pl.BlockSpec((tm, HEAD_PAD), row(0)),
            pl.BlockSpec((tm, HEAD_PAD), row(0)),
            pl.BlockSpec((1, Q_LORA), const),
            pl.BlockSpec((1, KV_LORA), const),
            pl.BlockSpec((Q_LORA, width), const),
            pl.BlockSpec((KV_LORA, 2 * width), const),
        ],
        out_specs=[pl.BlockSpec((tm, width), row(0))] * 3,
        out_shape=(jax.ShapeDtypeStruct((TOKENS, width), BF16),) * 3,
        compiler_params=_cparams(("parallel",), 48),
        name="qkv_prep",
    )(proj, proj, tail, tail, c_tab, s_tab, qg, kvg, wq, wkv)


def _attn_kernel(q_ref, k_ref, v_ref, o_ref, *state):
    t = T_ATTN
    nh = HEADS_PER_STEP
    qi = pl.program_id(2)
    nt = (((1,), (1,)), ((), ()))
    m_sc, l_sc, acc_sc = state[0:nh], state[nh:2 * nh], state[2 * nh:3 * nh]
    for h in range(nh):
        m_sc[h][...] = jnp.full((t, LANES), -jnp.inf, F32)
        l_sc[h][...] = jnp.zeros((t, LANES), F32)
        acc_sc[h][...] = jnp.zeros((t, LANES), F32)

    def block(j, r0, nr, c0, nc, masked):
        start = pl.multiple_of(j * t + c0, nc)
        rows = slice(r0, r0 + nr)
        for h in range(nh):
            lo, hi = h * HEAD_PAD, (h + 1) * HEAD_PAD
            s = lax.dot_general(q_ref[rows, lo:hi], k_ref[pl.ds(start, nc), lo:hi], nt, preferred_element_type=F32)
            if masked:
                row_id = r0 + lax.broadcasted_iota(jnp.int32, (nr, nc), 0)
                col_id = c0 + lax.broadcasted_iota(jnp.int32, (nr, nc), 1)
                s = jnp.where(row_id >= col_id, s, -jnp.inf)
            blocks = [s[:, c * LANES:(c + 1) * LANES] for c in range(nc // LANES)]
            bmax = functools.reduce(jnp.maximum, blocks)
            m_old = m_sc[h][rows, :]
            m_new = jnp.maximum(m_old, jnp.max(bmax, axis=-1, keepdims=True))
            alpha = jnp.exp2(m_old - m_new)
            ps = [jnp.exp2(b - m_new) for b in blocks]
            p = jnp.concatenate(ps, axis=1).astype(BF16)
            l_sc[h][rows, :] = alpha * l_sc[h][rows, :] + functools.reduce(jnp.add, ps)
            acc_sc[h][rows, :] = alpha * acc_sc[h][rows, :] + jnp.dot(p, v_ref[pl.ds(start, nc), lo:hi],
                                                                       preferred_element_type=F32)
            m_sc[h][rows, :] = m_new

    def body(j, carry):
        block(j, 0, t, 0, t, False)
        return carry

    lax.fori_loop(0, qi, body, 0)
    half = t // 2
    block(qi, 0, t, 0, half, True)
    block(qi, half, half, half, half, True)
    out = [acc_sc[h][...] / jnp.sum(l_sc[h][...], axis=-1, keepdims=True) for h in range(nh)]
    o_ref[...] = jnp.concatenate([out[h] + out[h + 1] for h in range(0, nh, 2)], axis=1).astype(BF16)


def _attention(q, k, v):
    t = T_ATTN
    nq = SEQ // t
    nh = HEADS_PER_STEP
    return pl.pallas_call(
        _attn_kernel,
        grid=(BATCH, N_HEADS // nh, nq),
        in_specs=[
            pl.BlockSpec((t, nh * HEAD_PAD), lambda b, hg, i: (b * nq + i, hg)),
            pl.BlockSpec((SEQ, nh * HEAD_PAD), lambda b, hg, i: (b, hg)),
            pl.BlockSpec((SEQ, nh * HEAD_PAD), lambda b, hg, i: (b, hg)),
        ],
        out_specs=pl.BlockSpec((t, nh * V_HEAD), lambda b, hg, i: (b * nq + i, hg)),
        out_shape=jax.ShapeDtypeStruct((TOKENS, N_HEADS * V_HEAD), BF16),
        scratch_shapes=[pltpu.VMEM((t, LANES), F32)] * (3 * nh),
        compiler_params=_cparams(("parallel", "parallel", "arbitrary"), 48),
        name="mla_attention",
    )(q, k, v)


def _mixer_tail_kernel(gates_ref, sc_ref, sch_ref, gv_ref, gg_ref, gvh_ref, ggh_ref, att_ref, x_ref,
                       cwa_ref, woa_ref, cwc_ref, lng_ref, lnb_ref, woc_ref, wob_ref, wo_ref,
                       out_ref, cbuf, ubuf, shifted, vbuf):
    tm = TM_MIX
    has_past = (pl.program_id(0) % (SEQ // tm)) != 0

    sc = sc_ref[...]
    sc_b = sc[:, 0:SC_WIDTH].astype(F32)
    cbuf[8:8 + tm, :] = sc[:, SC_WIDTH:2 * SC_WIDTH].astype(F32) * sc[:, 2 * SC_WIDTH:].astype(F32)
    sch = sch_ref[...].astype(F32)[HALO_A - 8:HALO_A, :]
    cbuf[0:8, :] = jnp.where(has_past, sch[:, SC_WIDTH:2 * SC_WIDTH] * sch[:, 2 * SC_WIDTH:], 0.0)
    conv_a = cwa_ref[0:1, :] * cbuf[6:6 + tm, :]
    for t in range(1, SC_KERNEL):
        conv_a = conv_a + cwa_ref[t:t + 1, :] * cbuf[6 + t:6 + t + tm, :]
    y_a = jnp.dot((sc_b * conv_a).astype(BF16), woa_ref[...], preferred_element_type=F32)

    ubuf[HALO_C:HALO_C + tm, :] = gv_ref[...].astype(F32) * _sigmoid(gg_ref[...].astype(F32))
    ubuf[0:HALO_C, :] = jnp.where(has_past, gvh_ref[...].astype(F32) * _sigmoid(ggh_ref[...].astype(F32)), 0.0)
    rows = tm + HALO_C - 8
    u_all = ubuf[...]
    for b in range(1, 8):
        shifted[b - 1, 0:rows, :] = pltpu.roll(u_all, tm + HALO_C - b, axis=0)[0:rows, :]
    base = HALO_C - (CONF_KERNEL - 1)
    for r0 in range(0, tm, CONV_CHUNK):
        acc = None
        for t in range(CONF_KERNEL):
            off = base + t
            a0 = r0 + off - off % 8
            src = ubuf[a0:a0 + CONV_CHUNK, :] if off % 8 == 0 else shifted[off % 8 - 1, a0:a0 + CONV_CHUNK, :]
            term = cwc_ref[t:t + 1, :] * src
            acc = term if acc is None else acc + term
        mu = jnp.mean(acc, axis=-1, keepdims=True)
        xc = acc - mu
        var = jnp.mean(xc * xc, axis=-1, keepdims=True)
        y = xc * lax.rsqrt(var + EPS) * lng_ref[...] + lnb_ref[...]
        vbuf[r0:r0 + CONV_CHUNK, :] = (y * _sigmoid(y)).astype(BF16)
    y_c = jnp.dot(vbuf[...], woc_ref[...], preferred_element_type=F32)

    y_b = jnp.dot(att_ref[...], wob_ref[...], preferred_element_type=F32)

    g = gates_ref[...]
    merged = (_sigmoid(g[:, 0:D_MODEL].astype(F32)) * y_a
              + _sigmoid(g[:, D_MODEL:2 * D_MODEL].astype(F32)) * y_b
              + _sigmoid(g[:, 2 * D_MODEL:].astype(F32)) * y_c)
    out_ref[...] = x_ref[...] + jnp.dot(merged.astype(BF16), wo_ref[...], preferred_element_type=F32)


def _mixer_tail(proj, tail, att, x, cwa, woa, cwc, lng, lnb, woc, wob, wo):
    tm = TM_MIX
    row = lambda width, col: pl.BlockSpec((tm, width), lambda i: (i, col // width))
    halo = lambda rows, width, col: pl.BlockSpec(
        (rows, width), lambda i: (jnp.maximum(i * (tm // rows) - 1, 0), col // width))
    const = lambda a: pl.BlockSpec(a.shape, lambda i: (0,) * a.ndim)
    weights = (cwa, woa, cwc, lng, lnb, woc, wob, wo)
    return pl.pallas_call(
        _mixer_tail_kernel,
        grid=(TOKENS // tm,),
        in_specs=[
            row(3 * D_MODEL, COL_GATES),
            row(3 * SC_WIDTH, COL_SC),
            halo(HALO_A, 3 * SC_WIDTH, COL_SC),
            row(CONF_WIDTH, TCOL_GLU),
            row(CONF_WIDTH, TCOL_GLU + CONF_WIDTH),
            halo(HALO_C, CONF_WIDTH, TCOL_GLU),
            halo(HALO_C, CONF_WIDTH, TCOL_GLU + CONF_WIDTH),
            pl.BlockSpec((tm, N_HEADS * V_HEAD), lambda i: (i, 0)),
            pl.BlockSpec((tm, D_MODEL), lambda i: (i, 0)),
        ] + [const(a) for a in weights],
        out_specs=pl.BlockSpec((tm, D_MODEL), lambda i: (i, 0)),
        out_shape=jax.ShapeDtypeStruct((TOKENS, D_MODEL), F32),
        scratch_shapes=[
            pltpu.VMEM((tm + 8, SC_WIDTH), F32),
            pltpu.VMEM((tm + HALO_C, CONF_WIDTH), F32),
            pltpu.VMEM((7, tm + HALO_C - 8, CONF_WIDTH), F32),
            pltpu.VMEM((tm, CONF_WIDTH), BF16),
        ],
        compiler_params=_cparams(("parallel",), 56),
        name="mixer_tail",
    )(proj, proj, proj, tail, tail, tail, tail, att, x, *weights)


def _route(logits):
    lane = lax.broadcasted_iota(jnp.int32, logits.shape, 1)
    lane_f = lane.astype(F32)
    neg = -jnp.inf
    big = float(LANES)
    is_grp = (lane >= ROUTER_GROUP_LANE) & (lane < ROUTER_GROUP_LANE + N_GROUPS)
    glog = jnp.where(is_grp, logits, neg)
    gmax = jnp.max(glog, axis=-1, keepdims=True)
    gidx = jnp.min(jnp.where(glog == gmax, lane_f, big), axis=-1, keepdims=True)
    p_sel = 1.0 / jnp.sum(jnp.exp(glog - gmax), axis=-1, keepdims=True)
    first = (gidx - ROUTER_GROUP_LANE) * EXPERTS_PER_GROUP
    in_grp = (lane_f >= first) & (lane_f < first + EXPERTS_PER_GROUP)
    el = jnp.where(in_grp, logits, neg)
    m1 = jnp.max(el, axis=-1, keepdims=True)
    i1 = jnp.min(jnp.where(el == m1, lane_f, big), axis=-1, keepdims=True)
    el2 = jnp.where(lane_f == i1, neg, el)
    m2 = jnp.max(el2, axis=-1, keepdims=True)
    i2 = jnp.min(jnp.where(el2 == m2, lane_f, big), axis=-1, keepdims=True)
    e2 = jnp.exp(m2 - m1)
    w1 = p_sel / (1.0 + e2)
    w2 = w1 * e2
    return i1, i2, w1, w2


def _router_kernel(x_ref, g_ref, wr_ref, br_ref, meta_ref, meta_t_ref, cnt_ref, xs0_ref, run_ref):
    tm = TM_ROUTE

    @pl.when(pl.program_id(0) == 0)
    def _():
        run_ref[...] = jnp.zeros_like(run_ref)

    h = _rms(x_ref[...], g_ref[...])
    h_hi = h.astype(BF16)
    h_lo = (h - h_hi.astype(F32)).astype(BF16)
    hi_terms = jnp.dot(h_hi, wr_ref[...], preferred_element_type=F32)
    logits = (hi_terms[:, :LANES] + hi_terms[:, LANES:]
              + jnp.dot(h_lo, wr_ref[:, :LANES], preferred_element_type=F32) + br_ref[...])
    i1, i2, w1, w2 = _route(logits)
    lane = lax.broadcasted_iota(jnp.int32, (tm, LANES), 1)
    lane_f = lane.astype(F32)
    oh1 = lane_f == i1
    oh2 = lane_f == i2
    onehot = jnp.where(oh1, 1.0, 0.0) + jnp.where(oh2, 1.0, 0.0)
    row_id = lax.broadcasted_iota(jnp.int32, (tm, tm), 0)
    col_id = lax.broadcasted_iota(jnp.int32, (tm, tm), 1)
    below = jnp.where(row_id > col_id, 1.0, 0.0).astype(BF16)
    before = run_ref[...] + jnp.dot(below, onehot.astype(BF16), preferred_element_type=F32)
    r1 = jnp.sum(jnp.where(oh1, before, 0.0), axis=-1, keepdims=True)
    r2 = jnp.sum(jnp.where(oh2, before, 0.0), axis=-1, keepdims=True)
    run_ref[...] += jnp.sum(onehot, axis=0, keepdims=True)
    cnt_ref[...] = run_ref[...]
    meta = jnp.zeros((tm, LANES), F32)
    for col, val in enumerate((i1, i2, r1, r2, w1, w2)):
        meta = jnp.where(lane == col, val, meta)
    meta_ref[...] = meta
    meta_t_ref[...] = meta.T[0:META_ROWS, :]
    xs0_ref[...] = jnp.zeros_like(xs0_ref)


def _router(x, g, wr, br):
    tm = TM_ROUTE
    const = lambda i: (0, 0)
    return pl.pallas_call(
        _router_kernel,
        grid=(TOKENS // tm,),
        in_specs=[
            pl.BlockSpec((tm, D_MODEL), lambda i: (i, 0)),
            pl.BlockSpec((1, D_MODEL), const),
            pl.BlockSpec((D_MODEL, 2 * LANES), const),
            pl.BlockSpec((1, LANES), const),
        ],
        out_specs=[
            pl.BlockSpec((tm, LANES), lambda i: (i, 0)),
            pl.BlockSpec((META_ROWS, tm), lambda i: (0, i)),
            pl.BlockSpec((1, LANES), const),
            pl.BlockSpec((SORTED_ROWS // (TOKENS // tm), LANES), lambda i: (i, 0)),
        ],
        out_shape=(
            jax.ShapeDtypeStruct((TOKENS, LANES), F32),
            jax.ShapeDtypeStruct((META_ROWS, TOKENS), F32),
            jax.ShapeDtypeStruct((1, LANES), F32),
            jax.ShapeDtypeStruct((SORTED_ROWS, LANES), F32),
        ),
        scratch_shapes=[pltpu.VMEM((1, LANES), F32)],
        compiler_params=_cparams(("arbitrary",), 48),
        name="moe_router",
    )(x, g, wr, br)


def _dispatch_plan(meta_t, counts):
    e1 = meta_t[META_E1].astype(jnp.int32)
    e2 = meta_t[META_E2].astype(jnp.int32)
    r1 = meta_t[META_R1].astype(jnp.int32)
    r2 = meta_t[META_R2].astype(jnp.int32)
    cnt = counts[0, :N_EXPERTS].astype(jnp.int32)
    tiles = (cnt + TM_EXP - 1) // TM_EXP
    tile_end = jnp.cumsum(tiles)
    first_slot = ((tile_end - tiles) * TM_EXP)[:, None]
    expert = jnp.arange(N_EXPERTS, dtype=jnp.int32)[:, None]
    pos1 = jnp.sum(jnp.where(e1[None, :] == expert, first_slot, 0), axis=0) + r1
    pos2 = jnp.sum(jnp.where(e2[None, :] == expert, first_slot, 0), axis=0) + r2
    n_tiles = tile_end[-1:]
    tile_id = jnp.minimum(jnp.arange(MAX_TILES, dtype=jnp.int32), n_tiles - 1)
    tile_expert = jnp.sum((tile_id[:, None] >= tile_end[None, :]).astype(jnp.int32), axis=1)
    return pos1, pos2, tile_expert, n_tiles


def _dispatch_kernel(pos1_ref, pos2_ref, x_ref, g_ref, xs_in_ref, xs_ref, slab, sem):
    del xs_in_ref
    tm = TM_DISP
    i = pl.program_id(0)
    last = pl.num_programs(0) - 1
    slot = i % 2

    def wait_copies(sl):
        for _ in range(TOP_K):
            pltpu.make_async_copy(slab.at[sl], xs_ref.at[pl.ds(0, tm * SLAB), :], sem.at[sl]).wait()

    @pl.when(i >= 2)
    def _():
        wait_copies(slot)

    h = _rms(x_ref[...], g_ref[...])
    for s in range(SLAB):
        slab[slot, pl.ds(s, tm, stride=SLAB), :] = h[:, s * LANES:(s + 1) * LANES]

    def body(c, carry):
        for u in range(GATHER_UNROLL):
            r = c * GATHER_UNROLL + u
            src = slab.at[slot, pl.ds(pl.multiple_of(r * SLAB, SLAB), SLAB), :]
            for k, pos_ref in enumerate((pos1_ref, pos2_ref)):
                p = pos_ref[i * tm + r]
                pltpu.make_async_copy(src, xs_ref.at[pl.ds(pl.multiple_of(p * SLAB, SLAB), SLAB), :],
                                      sem.at[slot]).start(priority=k % N_DMA_PRIORITIES)
        return carry

    lax.fori_loop(0, tm // GATHER_UNROLL, body, 0)

    @pl.when(i == last)
    def _():
        wait_copies(1 - slot)
        wait_copies(slot)


def _dispatch(pos1, pos2, x, g, xs0):
    tm = TM_DISP
    return pl.pallas_call(
        _dispatch_kernel,
        grid_spec=pltpu.PrefetchScalarGridSpec(
            num_scalar_prefetch=2,
            grid=(TOKENS // tm,),
            in_specs=[
                pl.BlockSpec((tm, D_MODEL), lambda i, p1, p2: (i, 0)),
                pl.BlockSpec((1, D_MODEL), lambda i, p1, p2: (0, 0)),
                pl.BlockSpec(memory_space=pl.ANY),
            ],
            out_specs=pl.BlockSpec(memory_space=pl.ANY),
            scratch_shapes=[pltpu.VMEM((2, tm * SLAB, LANES), F32), pltpu.SemaphoreType.DMA((2,))],
        ),
        out_shape=jax.ShapeDtypeStruct((SORTED_ROWS, LANES), F32),
        input_output_aliases={4: 0},
        compiler_params=_cparams(("arbitrary",), 48),
        name="moe_dispatch",
    )(pos1, pos2, x, g, xs0)


def _expert_kernel(te_ref, nt_ref, xs_ref, wg_ref, wu_ref, wd_ref, ys_ref, xbuf, sem):
    del te_ref
    tm = TM_EXP
    i = pl.program_id(0)
    n = nt_ref[0]

    def tile_copy(tile):
        slot = tile % EXPERT_SLOTS
        rows = pl.ds(pl.multiple_of(tile * (tm * SLAB), tm * SLAB), tm * SLAB)
        return pltpu.make_async_copy(xs_ref.at[rows, :], xbuf.at[slot], sem.at[slot])

    @pl.when(i == 0)
    def _():
        for ahead in range(EXPERT_PREFETCH):
            @pl.when(ahead < n)
            def _():
                tile_copy(ahead).start()

    @pl.when(i + EXPERT_PREFETCH < n)
    def _():
        tile_copy(i + EXPERT_PREFETCH).start()

    @pl.when(i < n)
    def _():
        tile_copy(i).wait()
        slot = i % EXPERT_SLOTS
        xt = jnp.concatenate([xbuf[slot, pl.ds(s, tm, stride=SLAB), :].astype(BF16) for s in range(SLAB)], axis=1)
        hg = jnp.dot(xt, wg_ref[...].astype(BF16), preferred_element_type=F32)
        hu = jnp.dot(xt, wu_ref[...].astype(BF16), preferred_element_type=F32)
        hh = (hg * _sigmoid(hg) * hu).astype(BF16)
        y = jnp.dot(hh, wd_ref[...].astype(BF16), preferred_element_type=F32)
        for s in range(SLAB):
            ys_ref[pl.ds(s, tm, stride=SLAB), :] = y[:, s * LANES:(s + 1) * LANES]

    @pl.when(i >= n)
    def _():
        ys_ref[...] = jnp.zeros_like(ys_ref)


def _experts(layer, tile_expert, n_tiles, xs, w_gate, w_up, w_down):
    tm = TM_EXP
    wspec = lambda rows, cols: pl.BlockSpec((None, None, rows, cols), lambda i, te, nt: (layer, te[i], 0, 0))
    return pl.pallas_call(
        _expert_kernel,
        grid_spec=pltpu.PrefetchScalarGridSpec(
            num_scalar_prefetch=2,
            grid=(MAX_TILES,),
            in_specs=[
                pl.BlockSpec(memory_space=pl.ANY),
                wspec(D_MODEL, EXPERT_HIDDEN),
                wspec(D_MODEL, EXPERT_HIDDEN),
                wspec(EXPERT_HIDDEN, D_MODEL),
            ],
            out_specs=pl.BlockSpec((tm * SLAB, LANES), lambda i, te, nt: (i, 0)),
            scratch_shapes=[pltpu.VMEM((EXPERT_SLOTS, tm * SLAB, LANES), F32), pltpu.SemaphoreType.DMA((EXPERT_SLOTS,))],
        ),
        out_shape=jax.ShapeDtypeStruct((SORTED_ROWS, LANES), F32),
        compiler_params=_cparams(("arbitrary",), 48),
        name="moe_experts",
    )(tile_expert, n_tiles, xs, w_gate, w_up, w_down)


def _start_slab_gathers(idx_refs, base, n_rows, src_hbm, dst_bufs, sem):
    def body(c, carry):
        for u in range(GATHER_UNROLL):
            r = c * GATHER_UNROLL + u
            for k, (idx_ref, dst) in enumerate(zip(idx_refs, dst_bufs)):
                t = idx_ref[base + r]
                pltpu.make_async_copy(src_hbm.at[pl.ds(pl.multiple_of(t * SLAB, SLAB), SLAB), :],
                                      dst.at[pl.ds(pl.multiple_of(r * SLAB, SLAB), SLAB), :],
                                      sem).start(priority=k % N_DMA_PRIORITIES)
        return carry

    lax.fori_loop(0, n_rows // GATHER_UNROLL, body, 0)


def _wait_slab_gathers(n_rows, src_hbm, dst, sem):
    pltpu.make_async_copy(src_hbm.at[pl.ds(0, n_rows * SLAB), :], dst, sem).wait()


def _combine_ple_kernel(pos1_ref, pos2_ref, x_ref, meta_ref, ys_ref, p_ref, g_ref, wg_ref, wp_ref, fg_ref,
                        out_ref, cbuf, sem, *, final):
    tm = TM_COMB
    i = pl.program_id(0)

    def gather(tile, slot):
        _start_slab_gathers((pos1_ref, pos2_ref), tile * tm, tm, ys_ref, (cbuf.at[slot, 0], cbuf.at[slot, 1]),
                            sem.at[slot])

    @pl.when(i == 0)
    def _():
        gather(0, 0)

    @pl.when(i + 1 < pl.num_programs(0))
    def _():
        gather(i + 1, (i + 1) % 2)

    slot = i % 2
    for k in range(2):
        _wait_slab_gathers(tm, ys_ref, cbuf.at[slot, k], sem.at[slot])
    meta = meta_ref[...]
    w1 = meta[:, META_W1:META_W1 + 1]
    w2 = meta[:, META_W2:META_W2 + 1]
    moe = jnp.concatenate([w1 * cbuf[slot, 0, pl.ds(s, tm, stride=SLAB), :]
                           + w2 * cbuf[slot, 1, pl.ds(s, tm, stride=SLAB), :] for s in range(SLAB)], axis=1)
    x = x_ref[...] + moe
    h = _rms(x, g_ref[...]).astype(BF16)
    gate = _sigmoid(jnp.dot(h, wg_ref[...], preferred_element_type=F32))
    emb = jnp.dot(p_ref[...].astype(BF16), wp_ref[...], preferred_element_type=F32)
    y = x + gate * emb
    if final:
        y = _rms(y, fg_ref[...])
    out_ref[...] = y


def _combine_ple(layer, pos1, pos2, x, meta, ys, p, g, wg, wp, fg, final):
    tm = TM_COMB
    const = lambda i, p1, p2: (0, 0)
    rows = lambda i, p1, p2: (i, 0)
    return pl.pallas_call(
        functools.partial(_combine_ple_kernel, final=final),
        grid_spec=pltpu.PrefetchScalarGridSpec(
            num_scalar_prefetch=2,
            grid=(TOKENS // tm,),
            in_specs=[
                pl.BlockSpec((tm, D_MODEL), rows),
                pl.BlockSpec((tm, LANES), rows),
                pl.BlockSpec(memory_space=pl.ANY),
                pl.BlockSpec((None, tm, PLE_DIM), lambda i, p1, p2: (layer, i, 0)),
                pl.BlockSpec((1, D_MODEL), const),
                pl.BlockSpec((D_MODEL, D_MODEL), const),
                pl.BlockSpec((PLE_DIM, D_MODEL), const),
                pl.BlockSpec((1, D_MODEL), const),
            ],
            out_specs=pl.BlockSpec((tm, D_MODEL), rows),
            scratch_shapes=[pltpu.VMEM((2, 2, tm * SLAB, LANES), F32), pltpu.SemaphoreType.DMA((2,))],
        ),
        out_shape=jax.ShapeDtypeStruct((TOKENS, D_MODEL), F32),
        compiler_params=_cparams(("arbitrary",), 48),
        name="moe_combine_ple",
    )(pos1, pos2, x, meta, ys, p, g, wg, wp, fg)


def _in_proj_weights(w_in):
    k_rope = w_in[:, MAIN_COLS:MAIN_COLS + QK_ROPE]
    glu = w_in[:, MAIN_COLS + QK_ROPE:]
    half = QK_ROPE // 2
    zn = jnp.zeros((D_MODEL, QK_NOPE), F32)
    zp = jnp.zeros((D_MODEL, HEAD_PAD - QK_NOPE - QK_ROPE), F32)
    tail = jnp.concatenate([glu, zn, k_rope, zp, zn, k_rope[:, half:], k_rope[:, :half], zp], axis=1)
    return w_in[:, :MAIN_COLS].astype(BF16), tail.astype(BF16)


def _q_weight(w_uq):
    scale = (QK_NOPE + QK_ROPE) ** -0.5 * LOG2_E
    w = (w_uq * scale).reshape(Q_LORA, N_HEADS, QK_NOPE + QK_ROPE)
    zp = jnp.zeros((Q_LORA, N_HEADS, HEAD_PAD - QK_NOPE - QK_ROPE), F32)
    return jnp.concatenate([w, zp], axis=2).reshape(Q_LORA, N_HEADS * HEAD_PAD).astype(BF16)


def _kv_weight(w_ukv):
    w = w_ukv.reshape(KV_LORA, N_HEADS, QK_NOPE + V_HEAD)
    k_nope, v = w[:, :, :QK_NOPE], w[:, :, QK_NOPE:]
    z = jnp.zeros_like(v)
    k_part = jnp.concatenate([k_nope, jnp.zeros_like(k_nope)], axis=2).reshape(KV_LORA, N_HEADS * HEAD_PAD)
    odd = (jnp.arange(N_HEADS) % 2 == 1)[None, :, None]
    v_part = jnp.concatenate([jnp.where(odd, z, v), jnp.where(odd, v, z)], axis=2).reshape(KV_LORA, N_HEADS * HEAD_PAD)
    return jnp.concatenate([k_part, v_part], axis=1).astype(BF16)


def _router_weight(w_rg, b_rg, w_re, b_re):
    pad = LANES - N_EXPERTS - N_GROUPS
    w = jnp.concatenate([w_re, w_rg, jnp.zeros((D_MODEL, pad), F32)], axis=1)
    b = jnp.concatenate([b_re, b_rg, jnp.zeros((pad,), F32)]).reshape(1, LANES)
    w_hi = w.astype(BF16)
    w_lo = (w - w_hi.astype(F32)).astype(BF16)
    return jnp.concatenate([w_hi, w_lo], axis=1), b


def kernel(x, p, positions, ln_mix_g, w_in, conv_a_w, w_out_a, q_norm_g, w_uq, kv_norm_g, w_ukv, w_out_b, conv_c_w, ln_c_g, ln_c_b, w_out_c, w_o, ln_ffn_g, w_route_grp, b_route_grp, w_route_exp, b_route_exp, w_exp_gate, w_exp_up, w_exp_down, ln_ple_g, w_ple_gate, w_ple, final_norm_g):
    c_tab, s_tab = _rope_tables(positions)
    xf = x.reshape(TOKENS, D_MODEL)
    pf = p.reshape(DEPTH, TOKENS, PLE_DIM)
    row = lambda a: a.reshape(1, -1)
    for i in range(DEPTH):
        proj, tail = _inproj(xf, row(ln_mix_g[i]), *_in_proj_weights(w_in[i]))
        q, k, v = _qkv(proj, tail, c_tab, s_tab, row(q_norm_g[i]), row(kv_norm_g[i]), _q_weight(w_uq[i]), _kv_weight(w_ukv[i]))
        att = _attention(q, k, v)
        xf = _mixer_tail(proj, tail, att, xf, conv_a_w[i], w_out_a[i].astype(BF16), conv_c_w[i], row(ln_c_g[i]),
                         row(ln_c_b[i]), w_out_c[i].astype(BF16), w_out_b[i].astype(BF16), w_o[i].astype(BF16))
        wr, br = _router_weight(w_route_grp[i], b_route_grp[i], w_route_exp[i], b_route_exp[i])
        meta, meta_t, counts, xs0 = _router(xf, row(ln_ffn_g[i]), wr, br)
        pos1, pos2, tile_expert, n_tiles = _dispatch_plan(meta_t, counts)
        xs = _dispatch(pos1, pos2, xf, row(ln_ffn_g[i]), xs0)
        ys = _experts(i, tile_expert, n_tiles, xs, w_exp_gate, w_exp_up, w_exp_down)
        xf = _combine_ple(i, pos1, pos2, xf, meta, ys, pf, row(ln_ple_g[i]), w_ple_gate[i].astype(BF16),
                          w_ple[i].astype(BF16), row(final_norm_g), final=(i == DEPTH - 1))
    return xf.reshape(BATCH, SEQ, D_MODEL)
```

```python
import functools

import jax
import jax.numpy as jnp
from jax import lax
from jax.experimental import pallas as pl
from jax.experimental.pallas import tpu as pltpu

D_MODEL = 1024
BATCH = 8
SEQ = 2048
DEPTH = 2
TOKENS = BATCH * SEQ
PLE_DIM = 256
SC_WIDTH = 512
SC_KERNEL = 3
N_HEADS = 8
QK_NOPE = 64
QK_ROPE = 32
V_HEAD = 64
Q_LORA = 768
KV_LORA = 256
ROPE_THETA = 10000.0
CONF_WIDTH = 512
CONF_KERNEL = 31
N_GROUPS = 4
EXPERTS_PER_GROUP = 8
N_EXPERTS = N_GROUPS * EXPERTS_PER_GROUP
EXPERT_HIDDEN = 256
EPS = 1e-6
LOG2_E = 1.4426950408889634

LANES = 128
HEAD_PAD = 128
F32 = jnp.float32
BF16 = jnp.bfloat16

COL_GATES = 0
COL_SC = 3 * D_MODEL
COL_QLAT = COL_SC + 3 * SC_WIDTH
COL_KVLAT = COL_QLAT + Q_LORA
MAIN_COLS = COL_KVLAT + KV_LORA
TCOL_GLU = 0
TCOL_KR = 2 * CONF_WIDTH
TCOL_KRSW = TCOL_KR + HEAD_PAD
TAIL_COLS = TCOL_KRSW + HEAD_PAD

ROUTER_GROUP_LANE = N_EXPERTS

TM_INPROJ = 1024
TN_INPROJ = MAIN_COLS // 2
TN_SPLIT = 1536
TM_QKV = 512
T_ATTN = 512
HEADS_PER_STEP = 8
TM_MIX = 512
CONV_CHUNK = 64
HALO_C = 32
HALO_A = 16
TM_ROUTE = 512
TM_EXP = 512
EXPERT_PREFETCH = 2
EXPERT_SLOTS = EXPERT_PREFETCH + 1
TM_DISP = 512
TM_COMB = 256
TOP_K = 2
MAX_TILES = TOKENS * TOP_K // TM_EXP + N_EXPERTS
SLAB = D_MODEL // LANES
GATHER_UNROLL = 16
N_DMA_PRIORITIES = 2
META_E1, META_E2, META_R1, META_R2, META_W1, META_W2 = range(6)
META_ROWS = 8
SORTED_ROWS = MAX_TILES * TM_EXP * SLAB


def _cparams(semantics, vmem_mb):
    return pltpu.CompilerParams(dimension_semantics=semantics, vmem_limit_bytes=vmem_mb * 1024 * 1024)


def _sigmoid(x):
    return 1.0 / (1.0 + jnp.exp2(x * -LOG2_E))


def _rms(x, g):
    return x * lax.rsqrt(jnp.mean(x * x, axis=-1, keepdims=True) + EPS) * g


def _place(x, onehot):
    x1 = x.astype(BF16)
    r1 = x - x1.astype(F32)
    x2 = r1.astype(BF16)
    x3 = (r1 - x2.astype(F32)).astype(BF16)
    return (jnp.dot(x1, onehot, preferred_element_type=F32) + jnp.dot(x2, onehot, preferred_element_type=F32)
            + jnp.dot(x3, onehot, preferred_element_type=F32))


def _rope_kernel(pos_ref, freq_ref, c_ref, s_ref):
    half = QK_ROPE // 2
    per_row = LANES // half
    rows = TOKENS // per_row
    ang = pos_ref[...].astype(F32) * freq_ref[...]
    cos = jnp.cos(ang)
    sin = jnp.sin(ang)
    src = lax.broadcasted_iota(jnp.int32, (LANES, LANES), 0)
    dst = lax.broadcasted_iota(jnp.int32, (LANES, LANES), 1)
    lane = lax.broadcasted_iota(jnp.int32, (1, LANES), 1)
    ones_nope = jnp.where(lane < QK_NOPE, 1.0, 0.0)
    sign = jnp.where(lane < QK_NOPE + half, -1.0, 1.0)
    for j in range(per_row):
        f = src - half * j
        hit = (dst == QK_NOPE + f) | (dst == QK_NOPE + half + f)
        onehot = jnp.where((f >= 0) & (f < half) & hit, 1.0, 0.0).astype(BF16)
        c_ref[pl.ds(j, rows, stride=per_row), :] = _place(cos, onehot) + ones_nope
        s_ref[pl.ds(j, rows, stride=per_row), :] = _place(sin, onehot) * sign


def _rope_tables(positions):
    half = QK_ROPE // 2
    inv_freq = ROPE_THETA ** (-jnp.arange(0, QK_ROPE, 2, dtype=F32) / QK_ROPE)
    rows = TOKENS * half // LANES
    pos_rep = jnp.broadcast_to(positions.reshape(TOKENS, 1), (TOKENS, half)).reshape(rows, LANES)
    freq = jnp.tile(inv_freq, LANES // half).reshape(1, LANES)
    return pl.pallas_call(
        _rope_kernel,
        out_shape=(jax.ShapeDtypeStruct((TOKENS, HEAD_PAD), F32),) * 2,
        compiler_params=pltpu.CompilerParams(vmem_limit_bytes=48 * 1024 * 1024),
        name="rope_tables",
    )(pos_rep, freq)


def _inproj_kernel(x_ref, g_ref, wm_ref, wt_ref, om_ref, ot_ref, h_ref):
    j = pl.program_id(1)
    n_main = MAIN_COLS // TN_INPROJ

    @pl.when(j == 0)
    def _():
        h_ref[...] = _rms(x_ref[...], g_ref[...]).astype(BF16)

    @pl.when(j < n_main)
    def _():
        for lo, hi in ((0, TN_SPLIT), (TN_SPLIT, TN_INPROJ)):
            om_ref[:, lo:hi] = jnp.dot(h_ref[...], wm_ref[:, lo:hi], preferred_element_type=F32).astype(BF16)

    @pl.when(j == n_main)
    def _():
        ot_ref[...] = jnp.dot(h_ref[...], wt_ref[...], preferred_element_type=F32).astype(BF16)


def _inproj(x, g, w_main, w_tail):
    tm, tn = TM_INPROJ, TN_INPROJ
    n_main = MAIN_COLS // tn
    main_col = lambda i, j: jnp.minimum(j, n_main - 1)
    return pl.pallas_call(
        _inproj_kernel,
        grid=(TOKENS // tm, n_main + 1),
        in_specs=[
            pl.BlockSpec((tm, D_MODEL), lambda i, j: (i, 0)),
            pl.BlockSpec((1, D_MODEL), lambda i, j: (0, 0)),
            pl.BlockSpec((D_MODEL, tn), lambda i, j: (0, main_col(i, j))),
            pl.BlockSpec((D_MODEL, TAIL_COLS), lambda i, j: (0, 0)),
        ],
        out_specs=[
            pl.BlockSpec((tm, tn), lambda i, j: (i, main_col(i, j))),
            pl.BlockSpec((tm, TAIL_COLS), lambda i, j: (i, 0)),
        ],
        out_shape=(jax.ShapeDtypeStruct((TOKENS, MAIN_COLS), BF16), jax.ShapeDtypeStruct((TOKENS, TAIL_COLS), BF16)),
        scratch_shapes=[pltpu.VMEM((tm, D_MODEL), BF16)],
        compiler_params=_cparams(("parallel", "arbitrary"), 56),
        name="in_proj",
    )(x, g, w_main, w_tail)


def _qkv_kernel(ql_ref, kvl_ref, kr_ref, krsw_ref, c_ref, s_ref, qg_ref, kvg_ref, wq_ref, wkv_ref,
                q_out, k_out, v_out):
    c = c_ref[...]
    s = s_ref[...]
    width = N_HEADS * HEAD_PAD
    half = QK_ROPE // 2
    low_half = lax.broadcasted_iota(jnp.int32, c.shape, 1) < QK_NOPE + half

    def swap_halves(x):
        return jnp.where(low_half, pltpu.roll(x, HEAD_PAD - half, axis=1), pltpu.roll(x, half, axis=1))

    qn = _rms(ql_ref[...].astype(F32), qg_ref[...]).astype(BF16)
    qq = jnp.dot(qn, wq_ref[...], preferred_element_type=F32)
    for h in range(N_HEADS):
        lo, hi = h * HEAD_PAD, (h + 1) * HEAD_PAD
        q_out[:, lo:hi] = (qq[:, lo:hi] * c + swap_halves(qq[:, lo:hi]) * s).astype(BF16)
    kvn = _rms(kvl_ref[...].astype(F32), kvg_ref[...]).astype(BF16)
    kk = jnp.dot(kvn, wkv_ref[...], preferred_element_type=F32)
    kr = kr_ref[...].astype(F32) * c + krsw_ref[...].astype(F32) * s
    for h in range(N_HEADS):
        lo, hi = h * HEAD_PAD, (h + 1) * HEAD_PAD
        k_out[:, lo:hi] = (kk[:, lo:hi] + kr).astype(BF16)
    v_out[...] = kk[:, width:].astype(BF16)


def _qkv(proj, tail, c_tab, s_tab, qg, kvg, wq, wkv):
    tm = TM_QKV
    width = N_HEADS * HEAD_PAD
    row = lambda blk: (lambda i: (i, blk))
    const = lambda i: (0, 0)
    return pl.pallas_call(
        _qkv_kernel,
        grid=(TOKENS // tm,),
        in_specs=[
            pl.BlockSpec((tm, Q_LORA), row(COL_QLAT // Q_LORA)),
            pl.BlockSpec((tm, KV_LORA), row(COL_KVLAT // KV_LORA)),
            pl.BlockSpec((tm, HEAD_PAD), row(TCOL_KR // HEAD_PAD)),
            pl.BlockSpec((tm, HEAD_PAD), row(TCOL_KRSW // HEAD_PAD)),
            pl.BlockSpec((tm, HEAD_PAD), row(0)),
            pl.BlockSpec((tm, HEAD_PAD), row(0)),
            pl.BlockSpec((1, Q_LORA), const),
            pl.BlockSpec((1, KV_LORA), const),
            pl.BlockSpec((Q_LORA, width), const),
            pl.BlockSpec((KV_LORA, 2 * width), const),
        ],
        out_specs=[pl.BlockSpec((tm, width), row(0))] * 3,
        out_shape=(jax.ShapeDtypeStruct((TOKENS, width), BF16),) * 3,
        compiler_params=_cparams(("parallel",), 48),
        name="qkv_prep",
    )(proj, proj, tail, tail, c_tab, s_tab, qg, kvg, wq, wkv)


def _attn_kernel(q_ref, k_ref, v_ref, o_ref, *state):
    t = T_ATTN
    nh = HEADS_PER_STEP
    qi = pl.program_id(2)
    nt = (((1,), (1,)), ((), ()))
    m_sc, l_sc, acc_sc = state[0:nh], state[nh:2 * nh], state[2 * nh:3 * nh]
    for h in range(nh):
        m_sc[h][...] = jnp.full((t, LANES), -jnp.inf, F32)
        l_sc[h][...] = jnp.zeros((t, LANES), F32)
        acc_sc[h][...] = jnp.zeros((t, LANES), F32)

    def block(j, r0, nr, c0, nc, masked):
        start = pl.multiple_of(j * t + c0, nc)
        rows = slice(r0, r0 + nr)
        for h in range(nh):
            lo, hi = h * HEAD_PAD, (h + 1) * HEAD_PAD
            s = lax.dot_general(q_ref[rows, lo:hi], k_ref[pl.ds(start, nc), lo:hi], nt, preferred_element_type=F32)
            if masked:
                row_id = r0 + lax.broadcasted_iota(jnp.int32, (nr, nc), 0)
                col_id = c0 + lax.broadcasted_iota(jnp.int32, (nr, nc), 1)
                s = jnp.where(row_id >= col_id, s, -jnp.inf)
            blocks = [s[:, c * LANES:(c + 1) * LANES] for c in range(nc // LANES)]
            bmax = functools.reduce(jnp.maximum, blocks)
            m_old = m_sc[h][rows, :]
            m_new = jnp.maximum(m_old, jnp.max(bmax, axis=-1, keepdims=True))
            alpha = jnp.exp2(m_old - m_new)
            ps = [jnp.exp2(b - m_new) for b in blocks]
            p = jnp.concatenate(ps, axis=1).astype(BF16)
            l_sc[h][rows, :] = alpha * l_sc[h][rows, :] + functools.reduce(jnp.add, ps)
            acc_sc[h][rows, :] = alpha * acc_sc[h][rows, :] + jnp.dot(p, v_ref[pl.ds(start, nc), lo:hi],
                                                                       preferred_element_type=F32)
            m_sc[h][rows, :] = m_new

    def body(j, carry):
        block(j, 0, t, 0, t, False)
        return carry

    lax.fori_loop(0, qi, body, 0)
    half = t // 2
    block(qi, 0, t, 0, half, True)
    block(qi, half, half, half, half, True)
    out = [acc_sc[h][...] / jnp.sum(l_sc[h][...], axis=-1, keepdims=True) for h in range(nh)]
    o_ref[...] = jnp.concatenate([out[h] + out[h + 1] for h in range(0, nh, 2)], axis=1).astype(BF16)


def _attention(q, k, v):
    t = T_ATTN
    nq = SEQ // t
    nh = HEADS_PER_STEP
    return pl.pallas_call(
        _attn_kernel,
        grid=(BATCH, N_HEADS // nh, nq),
        in_specs=[
            pl.BlockSpec((t, nh * HEAD_PAD), lambda b, hg, i: (b * nq + i, hg)),
            pl.BlockSpec((SEQ, nh * HEAD_PAD), lambda b, hg, i: (b, hg)),
            pl.BlockSpec((SEQ, nh * HEAD_PAD), lambda b, hg, i: (b, hg)),
        ],
        out_specs=pl.BlockSpec((t, nh * V_HEAD), lambda b, hg, i: (b * nq + i, hg)),
        out_shape=jax.ShapeDtypeStruct((TOKENS, N_HEADS * V_HEAD), BF16),
        scratch_shapes=[pltpu.VMEM((t, LANES), F32)] * (3 * nh),
        compiler_params=_cparams(("parallel", "parallel", "arbitrary"), 48),
        name="mla_attention",
    )(q, k, v)


def _mixer_tail_kernel(gates_ref, sc_ref, sch_ref, gv_ref, gg_ref, gvh_ref, ggh_ref, att_ref, x_ref,
                       cwa_ref, woa_ref, cwc_ref, lng_ref, lnb_ref, woc_ref, wob_ref, wo_ref,
                       out_ref, cbuf, ubuf, shifted, vbuf):
    tm = TM_MIX
    has_past = (pl.program_id(0) % (SEQ // tm)) != 0

    sc = sc_ref[...]
    sc_b = sc[:, 0:SC_WIDTH].astype(F32)
    cbuf[8:8 + tm, :] = sc[:, SC_WIDTH:2 * SC_WIDTH].astype(F32) * sc[:, 2 * SC_WIDTH:].astype(F32)
    sch = sch_ref[...].astype(F32)[HALO_A - 8:HALO_A, :]
    cbuf[0:8, :] = jnp.where(has_past, sch[:, SC_WIDTH:2 * SC_WIDTH] * sch[:, 2 * SC_WIDTH:], 0.0)
    conv_a = cwa_ref[0:1, :] * cbuf[6:6 + tm, :]
    for t in range(1, SC_KERNEL):
        conv_a = conv_a + cwa_ref[t:t + 1, :] * cbuf[6 + t:6 + t + tm, :]
    y_a = jnp.dot((sc_b * conv_a).astype(BF16), woa_ref[...], preferred_element_type=F32)

    ubuf[HALO_C:HALO_C + tm, :] = gv_ref[...].astype(F32) * _sigmoid(gg_ref[...].astype(F32))
    ubuf[0:HALO_C, :] = jnp.where(has_past, gvh_ref[...].astype(F32) * _sigmoid(ggh_ref[...].astype(F32)), 0.0)
    rows = tm + HALO_C - 8
    u_all = ubuf[...]
    for b in range(1, 8):
        shifted[b - 1, 0:rows, :] = pltpu.roll(u_all, tm + HALO_C - b, axis=0)[0:rows, :]
    base = HALO_C - (CONF_KERNEL - 1)
    for r0 in range(0, tm, CONV_CHUNK):
        acc = None
        for t in range(CONF_KERNEL):
            off = base + t
            a0 = r0 + off - off % 8
            src = ubuf[a0:a0 + CONV_CHUNK, :] if off % 8 == 0 else shifted[off % 8 - 1, a0:a0 + CONV_CHUNK, :]
            term = cwc_ref[t:t + 1, :] * src
            acc = term if acc is None else acc + term
        mu = jnp.mean(acc, axis=-1, keepdims=True)
        xc = acc - mu
        var = jnp.mean(xc * xc, axis=-1, keepdims=True)
        y = xc * lax.rsqrt(var + EPS) * lng_ref[...] + lnb_ref[...]
        vbuf[r0:r0 + CONV_CHUNK, :] = (y * _sigmoid(y)).astype(BF16)
    y_c = jnp.dot(vbuf[...], woc_ref[...], preferred_element_type=F32)

    y_b = jnp.dot(att_ref[...], wob_ref[...], preferred_element_type=F32)

    g = gates_ref[...]
    merged = (_sigmoid(g[:, 0:D_MODEL].astype(F32)) * y_a
              + _sigmoid(g[:, D_MODEL:2 * D_MODEL].astype(F32)) * y_b
              + _sigmoid(g[:, 2 * D_MODEL:].astype(F32)) * y_c)
    out_ref[...] = x_ref[...] + jnp.dot(merged.astype(BF16), wo_ref[...], preferred_element_type=F32)


def _mixer_tail(proj, tail, att, x, cwa, woa, cwc, lng, lnb, woc, wob, wo):
    tm = TM_MIX
    row = lambda width, col: pl.BlockSpec((tm, width), lambda i: (i, col // width))
    halo = lambda rows, width, col: pl.BlockSpec(
        (rows, width), lambda i: (jnp.maximum(i * (tm // rows) - 1, 0), col // width))
    const = lambda a: pl.BlockSpec(a.shape, lambda i: (0,) * a.ndim)
    weights = (cwa, woa, cwc, lng, lnb, woc, wob, wo)
    return pl.pallas_call(
        _mixer_tail_kernel,
        grid=(TOKENS // tm,),
        in_specs=[
            row(3 * D_MODEL, COL_GATES),
            row(3 * SC_WIDTH, COL_SC),
            halo(HALO_A, 3 * SC_WIDTH, COL_SC),
            row(CONF_WIDTH, TCOL_GLU),
            row(CONF_WIDTH, TCOL_GLU + CONF_WIDTH),
            halo(HALO_C, CONF_WIDTH, TCOL_GLU),
            halo(HALO_C, CONF_WIDTH, TCOL_GLU + CONF_WIDTH),
            pl.BlockSpec((tm, N_HEADS * V_HEAD), lambda i: (i, 0)),
            pl.BlockSpec((tm, D_MODEL), lambda i: (i, 0)),
        ] + [const(a) for a in weights],
        out_specs=pl.BlockSpec((tm, D_MODEL), lambda i: (i, 0)),
        out_shape=jax.ShapeDtypeStruct((TOKENS, D_MODEL), F32),
        scratch_shapes=[
            pltpu.VMEM((tm + 8, SC_WIDTH), F32),
            pltpu.VMEM((tm + HALO_C, CONF_WIDTH), F32),
            pltpu.VMEM((7, tm + HALO_C - 8, CONF_WIDTH), F32),
            pltpu.VMEM((tm, CONF_WIDTH), BF16),
        ],
        compiler_params=_cparams(("parallel",), 56),
        name="mixer_tail",
    )(proj, proj, proj, tail, tail, tail, tail, att, x, *weights)


def _route(logits):
    lane = lax.broadcasted_iota(jnp.int32, logits.shape, 1)
    lane_f = lane.astype(F32)
    neg = -jnp.inf
    big = float(LANES)
    is_grp = (lane >= ROUTER_GROUP_LANE) & (lane < ROUTER_GROUP_LANE + N_GROUPS)
    glog = jnp.where(is_grp, logits, neg)
    gmax = jnp.max(glog, axis=-1, keepdims=True)
    gidx = jnp.min(jnp.where(glog == gmax, lane_f, big), axis=-1, keepdims=True)
    p_sel = 1.0 / jnp.sum(jnp.exp(glog - gmax), axis=-1, keepdims=True)
    first = (gidx - ROUTER_GROUP_LANE) * EXPERTS_PER_GROUP
    in_grp = (lane_f >= first) & (lane_f < first + EXPERTS_PER_GROUP)
    el = jnp.where(in_grp, logits, neg)
    m1 = jnp.max(el, axis=-1, keepdims=True)
    i1 = jnp.min(jnp.where(el == m1, lane_f, big), axis=-1, keepdims=True)
    el2 = jnp.where(lane_f == i1, neg, el)
    m2 = jnp.max(el2, axis=-1, keepdims=True)
    i2 = jnp.min(jnp.where(el2 == m2, lane_f, big), axis=-1, keepdims=True)
    e2 = jnp.exp(m2 - m1)
    w1 = p_sel / (1.0 + e2)
    w2 = w1 * e2
    return i1, i2, w1, w2


def _router_kernel(x_ref, g_ref, wr_ref, br_ref, meta_ref, meta_t_ref, cnt_ref, xs0_ref, run_ref):
    tm = TM_ROUTE

    @pl.when(pl.program_id(0) == 0)
    def _():
        run_ref[...] = jnp.zeros_like(run_ref)

    h = _rms(x_ref[...], g_ref[...])
    h_hi = h.astype(BF16)
    h_lo = (h - h_hi.astype(F32)).astype(BF16)
    hi_terms = jnp.dot(h_hi, wr_ref[...], preferred_element_type=F32)
    logits = (hi_terms[:, :LANES] + hi_terms[:, LANES:]
              + jnp.dot(h_lo, wr_ref[:, :LANES], preferred_element_type=F32) + br_ref[...])
    i1, i2, w1, w2 = _route(logits)
    lane = lax.broadcasted_iota(jnp.int32, (tm, LANES), 1)
    lane_f = lane.astype(F32)
    oh1 = lane_f == i1
    oh2 = lane_f == i2
    onehot = jnp.where(oh1, 1.0, 0.0) + jnp.where(oh2, 1.0, 0.0)
    row_id = lax.broadcasted_iota(jnp.int32, (tm, tm), 0)
    col_id = lax.broadcasted_iota(jnp.int32, (tm, tm), 1)
    below = jnp.where(row_id > col_id, 1.0, 0.0).astype(BF16)
    before = run_ref[...] + jnp.dot(below, onehot.astype(BF16), preferred_element_type=F32)
    r1 = jnp.sum(jnp.where(oh1, before, 0.0), axis=-1, keepdims=True)
    r2 = jnp.sum(jnp.where(oh2, before, 0.0), axis=-1, keepdims=True)
    run_ref[...] += jnp.sum(onehot, axis=0, keepdims=True)
    cnt_ref[...] = run_ref[...]
    meta = jnp.zeros((tm, LANES), F32)
    for col, val in enumerate((i1, i2, r1, r2, w1, w2)):
        meta = jnp.where(lane == col, val, meta)
    meta_ref[...] = meta
    meta_t_ref[...] = meta.T[0:META_ROWS, :]
    xs0_ref[...] = jnp.zeros_like(xs0_ref)


def _router(x, g, wr, br):
    tm = TM_ROUTE
    const = lambda i: (0, 0)
    return pl.pallas_call(
        _router_kernel,
        grid=(TOKENS // tm,),
        in_specs=[
            pl.BlockSpec((tm, D_MODEL), lambda i: (i, 0)),
            pl.BlockSpec((1, D_MODEL), const),
            pl.BlockSpec((D_MODEL, 2 * LANES), const),
            pl.BlockSpec((1, LANES), const),
        ],
        out_specs=[
            pl.BlockSpec((tm, LANES), lambda i: (i, 0)),
            pl.BlockSpec((META_ROWS, tm), lambda i: (0, i)),
            pl.BlockSpec((1, LANES), const),
            pl.BlockSpec((SORTED_ROWS // (TOKENS // tm), LANES), lambda i: (i, 0)),
        ],
        out_shape=(
            jax.ShapeDtypeStruct((TOKENS, LANES), F32),
            jax.ShapeDtypeStruct((META_ROWS, TOKENS), F32),
            jax.ShapeDtypeStruct((1, LANES), F32),
            jax.ShapeDtypeStruct((SORTED_ROWS, LANES), F32),
        ),
        scratch_shapes=[pltpu.VMEM((1, LANES), F32)],
        compiler_params=_cparams(("arbitrary",), 48),
        name="moe_router",
    )(x, g, wr, br)


def _dispatch_plan(meta_t, counts):
    e1 = meta_t[META_E1].astype(jnp.int32)
    e2 = meta_t[META_E2].astype(jnp.int32)
    r1 = meta_t[META_R1].astype(jnp.int32)
    r2 = meta_t[META_R2].astype(jnp.int32)
    cnt = counts[0, :N_EXPERTS].astype(jnp.int32)
    tiles = (cnt + TM_EXP - 1) // TM_EXP
    tile_end = jnp.cumsum(tiles)
    first_slot = ((tile_end - tiles) * TM_EXP)[:, None]
    expert = jnp.arange(N_EXPERTS, dtype=jnp.int32)[:, None]
    pos1 = jnp.sum(jnp.where(e1[None, :] == expert, first_slot, 0), axis=0) + r1
    pos2 = jnp.sum(jnp.where(e2[None, :] == expert, first_slot, 0), axis=0) + r2
    n_tiles = tile_end[-1:]
    tile_id = jnp.minimum(jnp.arange(MAX_TILES, dtype=jnp.int32), n_tiles - 1)
    tile_expert = jnp.sum((tile_id[:, None] >= tile_end[None, :]).astype(jnp.int32), axis=1)
    return pos1, pos2, tile_expert, n_tiles


def _dispatch_kernel(pos1_ref, pos2_ref, x_ref, g_ref, xs_in_ref, xs_ref, slab, sem):
    del xs_in_ref
    tm = TM_DISP
    i = pl.program_id(0)
    last = pl.num_programs(0) - 1
    slot = i % 2

    def wait_copies(sl):
        for _ in range(TOP_K):
            pltpu.make_async_copy(slab.at[sl], xs_ref.at[pl.ds(0, tm * SLAB), :], sem.at[sl]).wait()

    @pl.when(i >= 2)
    def _():
        wait_copies(slot)

    h = _rms(x_ref[...], g_ref[...])
    for s in range(SLAB):
        slab[slot, pl.ds(s, tm, stride=SLAB), :] = h[:, s * LANES:(s + 1) * LANES]

    def body(c, carry):
        for u in range(GATHER_UNROLL):
            r = c * GATHER_UNROLL + u
            src = slab.at[slot, pl.ds(pl.multiple_of(r * SLAB, SLAB), SLAB), :]
            for k, pos_ref in enumerate((pos1_ref, pos2_ref)):
                p = pos_ref[i * tm + r]
                pltpu.make_async_copy(src, xs_ref.at[pl.ds(pl.multiple_of(p * SLAB, SLAB), SLAB), :],
                                      sem.at[slot]).start(priority=k % N_DMA_PRIORITIES)
        return carry

    lax.fori_loop(0, tm // GATHER_UNROLL, body, 0)

    @pl.when(i == last)
    def _():
        wait_copies(1 - slot)
        wait_copies(slot)


def _dispatch(pos1, pos2, x, g, xs0):
    tm = TM_DISP
    return pl.pallas_call(
        _dispatch_kernel,
        grid_spec=pltpu.PrefetchScalarGridSpec(
            num_scalar_prefetch=2,
            grid=(TOKENS // tm,),
            in_specs=[
                pl.BlockSpec((tm, D_MODEL), lambda i, p1, p2: (i, 0)),
                pl.BlockSpec((1, D_MODEL), lambda i, p1, p2: (0, 0)),
                pl.BlockSpec(memory_space=pl.ANY),
            ],
            out_specs=pl.BlockSpec(memory_space=pl.ANY),
            scratch_shapes=[pltpu.VMEM((2, tm * SLAB, LANES), F32), pltpu.SemaphoreType.DMA((2,))],
        ),
        out_shape=jax.ShapeDtypeStruct((SORTED_ROWS, LANES), F32),
        input_output_aliases={4: 0},
        compiler_params=_cparams(("arbitrary",), 48),
        name="moe_dispatch",
    )(pos1, pos2, x, g, xs0)


def _expert_kernel(te_ref, nt_ref, xs_ref, wg_ref, wu_ref, wd_ref, ys_ref, xbuf, sem):
    del te_ref
    tm = TM_EXP
    i = pl.program_id(0)
    n = nt_ref[0]

    def tile_copy(tile):
        slot = tile % EXPERT_SLOTS
        rows = pl.ds(pl.multiple_of(tile * (tm * SLAB), tm * SLAB), tm * SLAB)
        return pltpu.make_async_copy(xs_ref.at[rows, :], xbuf.at[slot], sem.at[slot])

    @pl.when(i == 0)
    def _():
        for ahead in range(EXPERT_PREFETCH):
            @pl.when(ahead < n)
            def _():
                tile_copy(ahead).start()

    @pl.when(i + EXPERT_PREFETCH < n)
    def _():
        tile_copy(i + EXPERT_PREFETCH).start()

    @pl.when(i < n)
    def _():
        tile_copy(i).wait()
        slot = i % EXPERT_SLOTS
        xt = jnp.concatenate([xbuf[slot, pl.ds(s, tm, stride=SLAB), :].astype(BF16) for s in range(SLAB)], axis=1)
        hg = jnp.dot(xt, wg_ref[...].astype(BF16), preferred_element_type=F32)
        hu = jnp.dot(xt, wu_ref[...].astype(BF16), preferred_element_type=F32)
        hh = (hg * _sigmoid(hg) * hu).astype(BF16)
        y = jnp.dot(hh, wd_ref[...].astype(BF16), preferred_element_type=F32)
        for s in range(SLAB):
            ys_ref[pl.ds(s, tm, stride=SLAB), :] = y[:, s * LANES:(s + 1) * LANES]

    @pl.when(i >= n)
    def _():
        ys_ref[...] = jnp.zeros_like(ys_ref)


def _experts(layer, tile_expert, n_tiles, xs, w_gate, w_up, w_down):
    tm = TM_EXP
    wspec = lambda rows, cols: pl.BlockSpec((None, None, rows, cols), lambda i, te, nt: (layer, te[i], 0, 0))
    return pl.pallas_call(
        _expert_kernel,
        grid_spec=pltpu.PrefetchScalarGridSpec(
            num_scalar_prefetch=2,
            grid=(MAX_TILES,),
            in_specs=[
                pl.BlockSpec(memory_space=pl.ANY),
                wspec(D_MODEL, EXPERT_HIDDEN),
                wspec(D_MODEL, EXPERT_HIDDEN),
                wspec(EXPERT_HIDDEN, D_MODEL),
            ],
            out_specs=pl.BlockSpec((tm * SLAB, LANES), lambda i, te, nt: (i, 0)),
            scratch_shapes=[pltpu.VMEM((EXPERT_SLOTS, tm * SLAB, LANES), F32), pltpu.SemaphoreType.DMA((EXPERT_SLOTS,))],
        ),
        out_shape=jax.ShapeDtypeStruct((SORTED_ROWS, LANES), F32),
        compiler_params=_cparams(("arbitrary",), 48),
        name="moe_experts",
    )(tile_expert, n_tiles, xs, w_gate, w_up, w_down)


def _start_slab_gathers(idx_refs, base, n_rows, src_hbm, dst_bufs, sem):
    def body(c, carry):
        for u in range(GATHER_UNROLL):
            r = c * GATHER_UNROLL + u
            for k, (idx_ref, dst) in enumerate(zip(idx_refs, dst_bufs)):
                t = idx_ref[base + r]
                pltpu.make_async_copy(src_hbm.at[pl.ds(pl.multiple_of(t * SLAB, SLAB), SLAB), :],
                                      dst.at[pl.ds(pl.multiple_of(r * SLAB, SLAB), SLAB), :],
                                      sem).start(priority=k % N_DMA_PRIORITIES)
        return carry

    lax.fori_loop(0, n_rows // GATHER_UNROLL, body, 0)


def _wait_slab_gathers(n_rows, src_hbm, dst, sem):
    pltpu.make_async_copy(src_hbm.at[pl.ds(0, n_rows * SLAB), :], dst, sem).wait()


def _combine_ple_kernel(pos1_ref, pos2_ref, x_ref, meta_ref, ys_ref, p_ref, g_ref, wg_ref, wp_ref, fg_ref,
                        out_ref, cbuf, sem, *, final):
    tm = TM_COMB
    i = pl.program_id(0)

    def gather(tile, slot):
        _start_slab_gathers((pos1_ref, pos2_ref), tile * tm, tm, ys_ref, (cbuf.at[slot, 0], cbuf.at[slot, 1]),
                            sem.at[slot])

    @pl.when(i == 0)
    def _():
        gather(0, 0)

    @pl.when(i + 1 < pl.num_programs(0))
    def _():
        gather(i + 1, (i + 1) % 2)

    slot = i % 2
    for k in range(2):
        _wait_slab_gathers(tm, ys_ref, cbuf.at[slot, k], sem.at[slot])
    meta = meta_ref[...]
    w1 = meta[:, META_W1:META_W1 + 1]
    w2 = meta[:, META_W2:META_W2 + 1]
    moe = jnp.concatenate([w1 * cbuf[slot, 0, pl.ds(s, tm, stride=SLAB), :]
                           + w2 * cbuf[slot, 1, pl.ds(s, tm, stride=SLAB), :] for s in range(SLAB)], axis=1)
    x = x_ref[...] + moe
    h = _rms(x, g_ref[...]).astype(BF16)
    gate = _sigmoid(jnp.dot(h, wg_ref[...], preferred_element_type=F32))
    emb = jnp.dot(p_ref[...].astype(BF16), wp_ref[...], preferred_element_type=F32)
    y = x + gate * emb
    if final:
        y = _rms(y, fg_ref[...])
    out_ref[...] = y


def _combine_ple(layer, pos1, pos2, x, meta, ys, p, g, wg, wp, fg, final):
    tm = TM_COMB
    const = lambda i, p1, p2: (0, 0)
    rows = lambda i, p1, p2: (i, 0)
    return pl.pallas_call(
        functools.partial(_combine_ple_kernel, final=final),
        grid_spec=pltpu.PrefetchScalarGridSpec(
            num_scalar_prefetch=2,
            grid=(TOKENS // tm,),
            in_specs=[
                pl.BlockSpec((tm, D_MODEL), rows),
                pl.BlockSpec((tm, LANES), rows),
                pl.BlockSpec(memory_space=pl.ANY),
                pl.BlockSpec((None, tm, PLE_DIM), lambda i, p1, p2: (layer, i, 0)),
                pl.BlockSpec((1, D_MODEL), const),
                pl.BlockSpec((D_MODEL, D_MODEL), const),
                pl.BlockSpec((PLE_DIM, D_MODEL), const),
                pl.BlockSpec((1, D_MODEL), const),
            ],
            out_specs=pl.BlockSpec((tm, D_MODEL), rows),
            scratch_shapes=[pltpu.VMEM((2, 2, tm * SLAB, LANES), F32), pltpu.SemaphoreType.DMA((2,))],
        ),
        out_shape=jax.ShapeDtypeStruct((TOKENS, D_MODEL), F32),
        compiler_params=_cparams(("arbitrary",), 48),
        name="moe_combine_ple",
    )(pos1, pos2, x, meta, ys, p, g, wg, wp, fg)


def _in_proj_weights(w_in):
    k_rope = w_in[:, MAIN_COLS:MAIN_COLS + QK_ROPE]
    glu = w_in[:, MAIN_COLS + QK_ROPE:]
    half = QK_ROPE // 2
    zn = jnp.zeros((D_MODEL, QK_NOPE), F32)
    zp = jnp.zeros((D_MODEL, HEAD_PAD - QK_NOPE - QK_ROPE), F32)
    tail = jnp.concatenate([glu, zn, k_rope, zp, zn, k_rope[:, half:], k_rope[:, :half], zp], axis=1)
    return w_in[:, :MAIN_COLS].astype(BF16), tail.astype(BF16)


def _q_weight(w_uq):
    scale = (QK_NOPE + QK_ROPE) ** -0.5 * LOG2_E
    w = (w_uq * scale).reshape(Q_LORA, N_HEADS, QK_NOPE + QK_ROPE)
    zp = jnp.zeros((Q_LORA, N_HEADS, HEAD_PAD - QK_NOPE - QK_ROPE), F32)
    return jnp.concatenate([w, zp], axis=2).reshape(Q_LORA, N_HEADS * HEAD_PAD).astype(BF16)


def _kv_weight(w_ukv):
    w = w_ukv.reshape(KV_LORA, N_HEADS, QK_NOPE + V_HEAD)
    k_nope, v = w[:, :, :QK_NOPE], w[:, :, QK_NOPE:]
    z = jnp.zeros_like(v)
    k_part = jnp.concatenate([k_nope, jnp.zeros_like(k_nope)], axis=2).reshape(KV_LORA, N_HEADS * HEAD_PAD)
    odd = (jnp.arange(N_HEADS) % 2 == 1)[None, :, None]
    v_part = jnp.concatenate([jnp.where(odd, z, v), jnp.where(odd, v, z)], axis=2).reshape(KV_LORA, N_HEADS * HEAD_PAD)
    return jnp.concatenate([k_part, v_part], axis=1).astype(BF16)


def _router_weight(w_rg, b_rg, w_re, b_re):
    pad = LANES - N_EXPERTS - N_GROUPS
    w = jnp.concatenate([w_re, w_rg, jnp.zeros((D_MODEL, pad), F32)], axis=1)
    b = jnp.concatenate([b_re, b_rg, jnp.zeros((pad,), F32)]).reshape(1, LANES)
    w_hi = w.astype(BF16)
    w_lo = (w - w_hi.astype(F32)).astype(BF16)
    return jnp.concatenate([w_hi, w_lo], axis=1), b


def kernel(x, p, positions, ln_mix_g, w_in, conv_a_w, w_out_a, q_norm_g, w_uq, kv_norm_g, w_ukv, w_out_b, conv_c_w, ln_c_g, ln_c_b, w_out_c, w_o, ln_ffn_g, w_route_grp, b_route_grp, w_route_exp, b_route_exp, w_exp_gate, w_exp_up, w_exp_down, ln_ple_g, w_ple_gate, w_ple, final_norm_g):
    c_tab, s_tab = _rope_tables(positions)
    xf = x.reshape(TOKENS, D_MODEL)
    pf = p.reshape(DEPTH, TOKENS, PLE_DIM)
    row = lambda a: a.reshape(1, -1)
    for i in range(DEPTH):
        proj, tail = _inproj(xf, row(ln_mix_g[i]), *_in_proj_weights(w_in[i]))
        q, k, v = _qkv(proj, tail, c_tab, s_tab, row(q_norm_g[i]), row(kv_norm_g[i]), _q_weight(w_uq[i]), _kv_weight(w_ukv[i]))
        att = _attention(q, k, v)
        xf = _mixer_tail(proj, tail, att, xf, conv_a_w[i], w_out_a[i].astype(BF16), conv_c_w[i], row(ln_c_g[i]),
                         row(ln_c_b[i]), w_out_c[i].astype(BF16), w_out_b[i].astype(BF16), w_o[i].astype(BF16))
        wr, br = _router_weight(w_route_grp[i], b_route_grp[i], w_route_exp[i], b_route_exp[i])
        meta, meta_t, counts, xs0 = _router(xf, row(ln_ffn_g[i]), wr, br)
        pos1, pos2, tile_expert, n_tiles = _dispatch_plan(meta_t, counts)
        xs = _dispatch(pos1, pos2, xf, row(ln_ffn_g[i]), xs0)
        ys = _experts(i, tile_expert, n_tiles, xs, w_exp_gate, w_exp_up, w_exp_down)
        xf = _combine_ple(i, pos1, pos2, xf, meta, ys, pf, row(ln_ple_g[i]), w_ple_gate[i].astype(BF16),
                          w_ple[i].astype(BF16), row(final_norm_g), final=(i == DEPTH - 1))
    return xf.reshape(BATCH, SEQ, D_MODEL)
```

```python
import functools

import jax
import jax.numpy as jnp
from jax import lax
from jax.experimental import pallas as pl
from jax.experimental.pallas import tpu as pltpu

D_MODEL = 1024
BATCH = 8
SEQ = 2048
DEPTH = 2
TOKENS = BATCH * SEQ
PLE_DIM = 256
SC_WIDTH = 512
SC_KERNEL = 3
N_HEADS = 8
QK_NOPE = 64
QK_ROPE = 32
V_HEAD = 64
Q_LORA = 768
KV_LORA = 256
ROPE_THETA = 10000.0
CONF_WIDTH = 512
CONF_KERNEL = 31
N_GROUPS = 4
EXPERTS_PER_GROUP = 8
N_EXPERTS = N_GROUPS * EXPERTS_PER_GROUP
EXPERT_HIDDEN = 256
EPS = 1e-6
LOG2_E = 1.4426950408889634

LANES = 128
HEAD_PAD = 128
F32 = jnp.float32
BF16 = jnp.bfloat16
I32 = jnp.int32

COL_GATES = 0
COL_SC = 3 * D_MODEL
COL_QLAT = COL_SC + 3 * SC_WIDTH
COL_KVLAT = COL_QLAT + Q_LORA
MAIN_COLS = COL_KVLAT + KV_LORA
TCOL_GLU = 0
TCOL_KR = 2 * CONF_WIDTH
TCOL_KRSW = TCOL_KR + HEAD_PAD
TAIL_COLS = TCOL_KRSW + HEAD_PAD

ROUTER_GROUP_LANE = N_EXPERTS

TM_INPROJ = 1024
TN_INPROJ = MAIN_COLS // 2
TN_SPLIT = 1536
TM_QKV = 512
T_ATTN = 512
HEADS_PER_STEP = 8
TM_MIX = 512
CONV_CHUNK = 64
HALO_C = 32
HALO_A = 16
TM_ROUTE = 512
TM_EXP = 512
EXPERT_PREFETCH = 2
EXPERT_SLOTS = EXPERT_PREFETCH + 1
TM_DISP = 512
TM_COMB = 256
TOP_K = 2
MAX_TILES = TOKENS * TOP_K // TM_EXP + N_EXPERTS
SLAB = D_MODEL // 2 // LANES
GATHER_UNROLL = 16
N_DMA_PRIORITIES = 2
META_E1, META_E2, META_R1, META_R2, META_W1, META_W2 = range(6)
META_ROWS = 8
SORTED_ROWS = MAX_TILES * TM_EXP * SLAB


def _cparams(semantics, vmem_mb):
    return pltpu.CompilerParams(dimension_semantics=semantics, vmem_limit_bytes=vmem_mb * 1024 * 1024)


def _sigmoid(x):
    return 1.0 / (1.0 + jnp.exp2(x * -LOG2_E))


def _rms(x, g):
    return x * lax.rsqrt(jnp.mean(x * x, axis=-1, keepdims=True) + EPS) * g


def _pack_rows(x):
    half = D_MODEL // 2
    return pltpu.pack_elementwise([x[:, :half], x[:, half:]], packed_dtype=BF16)


def _unpack_rows(w):
    return jnp.concatenate([pltpu.unpack_elementwise(w, index=k, packed_dtype=BF16, unpacked_dtype=F32)
                            for k in range(2)], axis=1)


def _store_slabs(put, w):
    for s in range(SLAB):
        put(s, w[:, s * LANES:(s + 1) * LANES])


def _load_slabs(get):
    return jnp.concatenate([get(s) for s in range(SLAB)], axis=1)


def _place(x, onehot):
    x1 = x.astype(BF16)
    r1 = x - x1.astype(F32)
    x2 = r1.astype(BF16)
    x3 = (r1 - x2.astype(F32)).astype(BF16)
    return (jnp.dot(x1, onehot, preferred_element_type=F32) + jnp.dot(x2, onehot, preferred_element_type=F32)
            + jnp.dot(x3, onehot, preferred_element_type=F32))


def _rope_kernel(pos_ref, freq_ref, c_ref, s_ref):
    half = QK_ROPE // 2
    per_row = LANES // half
    rows = TOKENS // per_row
    ang = pos_ref[...].astype(F32) * freq_ref[...]
    cos = jnp.cos(ang)
    sin = jnp.sin(ang)
    src = lax.broadcasted_iota(jnp.int32, (LANES, LANES), 0)
    dst = lax.broadcasted_iota(jnp.int32, (LANES, LANES), 1)
    lane = lax.broadcasted_iota(jnp.int32, (1, LANES), 1)
    ones_nope = jnp.where(lane < QK_NOPE, 1.0, 0.0)
    sign = jnp.where(lane < QK_NOPE + half, -1.0, 1.0)
    for j in range(per_row):
        f = src - half * j
        hit = (dst == QK_NOPE + f) | (dst == QK_NOPE + half + f)
        onehot = jnp.where((f >= 0) & (f < half) & hit, 1.0, 0.0).astype(BF16)
        c_ref[pl.ds(j, rows, stride=per_row), :] = _place(cos, onehot) + ones_nope
        s_ref[pl.ds(j, rows, stride=per_row), :] = _place(sin, onehot) * sign


def _rope_tables(positions):
    half = QK_ROPE // 2
    inv_freq = ROPE_THETA ** (-jnp.arange(0, QK_ROPE, 2, dtype=F32) / QK_ROPE)
    rows = TOKENS * half // LANES
    pos_rep = jnp.broadcast_to(positions.reshape(TOKENS, 1), (TOKENS, half)).reshape(rows, LANES)
    freq = jnp.tile(inv_freq, LANES // half).reshape(1, LANES)
    return pl.pallas_call(
        _rope_kernel,
        out_shape=(jax.ShapeDtypeStruct((TOKENS, HEAD_PAD), F32),) * 2,
        compiler_params=pltpu.CompilerParams(vmem_limit_bytes=48 * 1024 * 1024),
        name="rope_tables",
    )(pos_rep, freq)


def _inproj_kernel(x_ref, g_ref, wm_ref, wt_ref, om_ref, ot_ref, h_ref):
    j = pl.program_id(1)
    n_main = MAIN_COLS // TN_INPROJ

    @pl.when(j == 0)
    def _():
        h_ref[...] = _rms(x_ref[...], g_ref[...]).astype(BF16)

    @pl.when(j < n_main)
    def _():
        for lo, hi in ((0, TN_SPLIT), (TN_SPLIT, TN_INPROJ)):
            om_ref[:, lo:hi] = jnp.dot(h_ref[...], wm_ref[:, lo:hi], preferred_element_type=F32).astype(BF16)

    @pl.when(j == n_main)
    def _():
        ot_ref[...] = jnp.dot(h_ref[...], wt_ref[...], preferred_element_type=F32).astype(BF16)


def _inproj(x, g, w_main, w_tail):
    tm, tn = TM_INPROJ, TN_INPROJ
    n_main = MAIN_COLS // tn
    main_col = lambda i, j: jnp.minimum(j, n_main - 1)
    return pl.pallas_call(
        _inproj_kernel,
        grid=(TOKENS // tm, n_main + 1),
        in_specs=[
            pl.BlockSpec((tm, D_MODEL), lambda i, j: (i, 0)),
            pl.BlockSpec((1, D_MODEL), lambda i, j: (0, 0)),
            pl.BlockSpec((D_MODEL, tn), lambda i, j: (0, main_col(i, j))),
            pl.BlockSpec((D_MODEL, TAIL_COLS), lambda i, j: (0, 0)),
        ],
        out_specs=[
            pl.BlockSpec((tm, tn), lambda i, j: (i, main_col(i, j))),
            pl.BlockSpec((tm, TAIL_COLS), lambda i, j: (i, 0)),
        ],
        out_shape=(jax.ShapeDtypeStruct((TOKENS, MAIN_COLS), BF16), jax.ShapeDtypeStruct((TOKENS, TAIL_COLS), BF16)),
        scratch_shapes=[pltpu.VMEM((tm, D_MODEL), BF16)],
        compiler_params=_cparams(("parallel", "arbitrary"), 56),
        name="in_proj",
    )(x, g, w_main, w_tail)


def _qkv_kernel(ql_ref, kvl_ref, kr_ref, krsw_ref, c_ref, s_ref, qg_ref, kvg_ref, wq_ref, wkv_ref,
                q_out, k_out, v_out):
    c = c_ref[...]
    s = s_ref[...]
    width = N_HEADS * HEAD_PAD
    half = QK_ROPE // 2
    low_half = lax.broadcasted_iota(jnp.int32, c.shape, 1) < QK_NOPE + half

    def swap_halves(x):
        return jnp.where(low_half, pltpu.roll(x, HEAD_PAD - half, axis=1), pltpu.roll(x, half, axis=1))

    qn = _rms(ql_ref[...].astype(F32), qg_ref[...]).astype(BF16)
    qq = jnp.dot(qn, wq_ref[...], preferred_element_type=F32)
    for h in range(N_HEADS):
        lo, hi = h * HEAD_PAD, (h + 1) * HEAD_PAD
        q_out[:, lo:hi] = (qq[:, lo:hi] * c + swap_halves(qq[:, lo:hi]) * s).astype(BF16)
    kvn = _rms(kvl_ref[...].astype(F32), kvg_ref[...]).astype(BF16)
    kk = jnp.dot(kvn, wkv_ref[...], preferred_element_type=F32)
    kr = kr_ref[...].astype(F32) * c + krsw_ref[...].astype(F32) * s
    for h in range(N_HEADS):
        lo, hi = h * HEAD_PAD, (h + 1) * HEAD_PAD
        k_out[:, lo:hi] = (kk[:, lo:hi] + kr).astype(BF16)
    v_out[...] = kk[:, width:].astype(BF16)


def _qkv(proj, tail, c_tab, s_tab, qg, kvg, wq, wkv):
    tm = TM_QKV
    width = N_HEADS * HEAD_PAD
    row = lambda blk: (lambda i: (i, blk))
    const = lambda i: (0, 0)
    return pl.pallas_call(
        _qkv_kernel,
        grid=(TOKENS // tm,),
        in_specs=[
            pl.BlockSpec((tm, Q_LORA), row(COL_QLAT // Q_LORA)),
            pl.BlockSpec((tm, KV_LORA), row(COL_KVLAT // KV_LORA)),
            pl.BlockSpec((tm, HEAD_PAD), row(TCOL_KR // HEAD_PAD)),
            pl.BlockSpec((tm, HEAD_PAD), row(TCOL_KRSW // HEAD_PAD)),
            pl.BlockSpec((tm, HEAD_PAD), row(0)),
            pl.BlockSpec((tm, HEAD_PAD), row(0)),
            pl.BlockSpec((1, Q_LORA), const),
            pl.BlockSpec((1, KV_LORA), const),
            pl.BlockSpec((Q_LORA, width), const),
            pl.BlockSpec((KV_LORA, 2 * width), const),
        ],
        out_specs=[pl.BlockSpec((tm, width), row(0))] * 3,
        out_shape=(jax.ShapeDtypeStruct((TOKENS, width), BF16),) * 3,
        compiler_params=_cparams(("parallel",), 48),
        name="qkv_prep",
    )(proj, proj, tail, tail, c_tab, s_tab, qg, kvg, wq, wkv)


def _attn_kernel(q_ref, k_ref, v_ref, o_ref, *state):
    t = T_ATTN
    nh = HEADS_PER_STEP
    qi = pl.program_id(2)
    nt = (((1,), (1,)), ((), ()))
    m_sc, l_sc, acc_sc = state[0:nh], state[nh:2 * nh], state[2 * nh:3 * nh]
    for h in range(nh):
        m_sc[h][...] = jnp.full((t, LANES), -jnp.inf, F32)
        l_sc[h][...] = jnp.zeros((t, LANES), F32)
        acc_sc[h][...] = jnp.zeros((t, LANES), F32)

    def block(j, r0, nr, c0, nc, masked):
        start = pl.multiple_of(j * t + c0, nc)
        rows = slice(r0, r0 + nr)
        for h in range(nh):
            lo, hi = h * HEAD_PAD, (h + 1) * HEAD_PAD
            s = lax.dot_general(q_ref[rows, lo:hi], k_ref[pl.ds(start, nc), lo:hi], nt, preferred_element_type=F32)
            if masked:
                row_id = r0 + lax.broadcasted_iota(jnp.int32, (nr, nc), 0)
                col_id = c0 + lax.broadcasted_iota(jnp.int32, (nr, nc), 1)
                s = jnp.where(row_id >= col_id, s, -jnp.inf)
            blocks = [s[:, c * LANES:(c + 1) * LANES] for c in range(nc // LANES)]
            bmax = functools.reduce(jnp.maximum, blocks)
            m_old = m_sc[h][rows, :]
            m_new = jnp.maximum(m_old, jnp.max(bmax, axis=-1, keepdims=True))
            alpha = jnp.exp2(m_old - m_new)
            ps = [jnp.exp2(b - m_new) for b in blocks]
            p = jnp.concatenate(ps, axis=1).astype(BF16)
            l_sc[h][rows, :] = alpha * l_sc[h][rows, :] + functools.reduce(jnp.add, ps)
            acc_sc[h][rows, :] = alpha * acc_sc[h][rows, :] + jnp.dot(p, v_ref[pl.ds(start, nc), lo:hi],
                                                                       preferred_element_type=F32)
            m_sc[h][rows, :] = m_new

    def body(j, carry):
        block(j, 0, t, 0, t, False)
        return carry

    lax.fori_loop(0, qi, body, 0)
    half = t // 2
    block(qi, 0, t, 0, half, True)
    block(qi, half, half, half, half, True)
    out = [acc_sc[h][...] / jnp.sum(l_sc[h][...], axis=-1, keepdims=True) for h in range(nh)]
    o_ref[...] = jnp.concatenate([out[h] + out[h + 1] for h in range(0, nh, 2)], axis=1).astype(BF16)


def _attention(q, k, v):
    t = T_ATTN
    nq = SEQ // t
    nh = HEADS_PER_STEP
    return pl.pallas_call(
        _attn_kernel,
        grid=(BATCH, N_HEADS // nh, nq),
        in_specs=[
            pl.BlockSpec((t, nh * HEAD_PAD), lambda b, hg, i: (b * nq + i, hg)),
            pl.BlockSpec((SEQ, nh * HEAD_PAD), lambda b, hg, i: (b, hg)),
            pl.BlockSpec((SEQ, nh * HEAD_PAD), lambda b, hg, i: (b, hg)),
        ],
        out_specs=pl.BlockSpec((t, nh * V_HEAD), lambda b, hg, i: (b * nq + i, hg)),
        out_shape=jax.ShapeDtypeStruct((TOKENS, N_HEADS * V_HEAD), BF16),
        scratch_shapes=[pltpu.VMEM((t, LANES), F32)] * (3 * nh),
        compiler_params=_cparams(("parallel", "parallel", "arbitrary"), 48),
        name="mla_attention",
    )(q, k, v)


def _mixer_tail_kernel(gates_ref, sc_ref, sch_ref, gv_ref, gg_ref, gvh_ref, ggh_ref, att_ref, x_ref,
                       cwa_ref, woa_ref, cwc_ref, lng_ref, lnb_ref, woc_ref, wob_ref, wo_ref,
                       out_ref, cbuf, ubuf, shifted, vbuf):
    tm = TM_MIX
    has_past = (pl.program_id(0) % (SEQ // tm)) != 0

    sc = sc_ref[...]
    sc_b = sc[:, 0:SC_WIDTH].astype(F32)
    cbuf[8:8 + tm, :] = sc[:, SC_WIDTH:2 * SC_WIDTH].astype(F32) * sc[:, 2 * SC_WIDTH:].astype(F32)
    sch = sch_ref[...].astype(F32)[HALO_A - 8:HALO_A, :]
    cbuf[0:8, :] = jnp.where(has_past, sch[:, SC_WIDTH:2 * SC_WIDTH] * sch[:, 2 * SC_WIDTH:], 0.0)
    conv_a = cwa_ref[0:1, :] * cbuf[6:6 + tm, :]
    for t in range(1, SC_KERNEL):
        conv_a = conv_a + cwa_ref[t:t + 1, :] * cbuf[6 + t:6 + t + tm, :]
    y_a = jnp.dot((sc_b * conv_a).astype(BF16), woa_ref[...], preferred_element_type=F32)

    ubuf[HALO_C:HALO_C + tm, :] = gv_ref[...].astype(F32) * _sigmoid(gg_ref[...].astype(F32))
    ubuf[0:HALO_C, :] = jnp.where(has_past, gvh_ref[...].astype(F32) * _sigmoid(ggh_ref[...].astype(F32)), 0.0)
    rows = tm + HALO_C - 8
    u_all = ubuf[...]
    for b in range(1, 8):
        shifted[b - 1, 0:rows, :] = pltpu.roll(u_all, tm + HALO_C - b, axis=0)[0:rows, :]
    base = HALO_C - (CONF_KERNEL - 1)
    for r0 in range(0, tm, CONV_CHUNK):
        acc = None
        for t in range(CONF_KERNEL):
            off = base + t
            a0 = r0 + off - off % 8
            src = ubuf[a0:a0 + CONV_CHUNK, :] if off % 8 == 0 else shifted[off % 8 - 1, a0:a0 + CONV_CHUNK, :]
            term = cwc_ref[t:t + 1, :] * src
            acc = term if acc is None else acc + term
        mu = jnp.mean(acc, axis=-1, keepdims=True)
        xc = acc - mu
        var = jnp.mean(xc * xc, axis=-1, keepdims=True)
        y = xc * lax.rsqrt(var + EPS) * lng_ref[...] + lnb_ref[...]
        vbuf[r0:r0 + CONV_CHUNK, :] = (y * _sigmoid(y)).astype(BF16)
    y_c = jnp.dot(vbuf[...], woc_ref[...], preferred_element_type=F32)

    y_b = jnp.dot(att_ref[...], wob_ref[...], preferred_element_type=F32)

    g = gates_ref[...]
    merged = (_sigmoid(g[:, 0:D_MODEL].astype(F32)) * y_a
              + _sigmoid(g[:, D_MODEL:2 * D_MODEL].astype(F32)) * y_b
              + _sigmoid(g[:, 2 * D_MODEL:].astype(F32)) * y_c)
    out_ref[...] = x_ref[...] + jnp.dot(merged.astype(BF16), wo_ref[...], preferred_element_type=F32)


def _mixer_tail(proj, tail, att, x, cwa, woa, cwc, lng, lnb, woc, wob, wo):
    tm = TM_MIX
    row = lambda width, col: pl.BlockSpec((tm, width), lambda i: (i, col // width))
    halo = lambda rows, width, col: pl.BlockSpec(
        (rows, width), lambda i: (jnp.maximum(i * (tm // rows) - 1, 0), col // width))
    const = lambda a: pl.BlockSpec(a.shape, lambda i: (0,) * a.ndim)
    weights = (cwa, woa, cwc, lng, lnb, woc, wob, wo)
    return pl.pallas_call(
        _mixer_tail_kernel,
        grid=(TOKENS // tm,),
        in_specs=[
            row(3 * D_MODEL, COL_GATES),
            row(3 * SC_WIDTH, COL_SC),
            halo(HALO_A, 3 * SC_WIDTH, COL_SC),
            row(CONF_WIDTH, TCOL_GLU),
            row(CONF_WIDTH, TCOL_GLU + CONF_WIDTH),
            halo(HALO_C, CONF_WIDTH, TCOL_GLU),
            halo(HALO_C, CONF_WIDTH, TCOL_GLU + CONF_WIDTH),
            pl.BlockSpec((tm, N_HEADS * V_HEAD), lambda i: (i, 0)),
            pl.BlockSpec((tm, D_MODEL), lambda i: (i, 0)),
        ] + [const(a) for a in weights],
        out_specs=pl.BlockSpec((tm, D_MODEL), lambda i: (i, 0)),
        out_shape=jax.ShapeDtypeStruct((TOKENS, D_MODEL), F32),
        scratch_shapes=[
            pltpu.VMEM((tm + 8, SC_WIDTH), F32),
            pltpu.VMEM((tm + HALO_C, CONF_WIDTH), F32),
            pltpu.VMEM((7, tm + HALO_C - 8, CONF_WIDTH), F32),
            pltpu.VMEM((tm, CONF_WIDTH), BF16),
        ],
        compiler_params=_cparams(("parallel",), 56),
        name="mixer_tail",
    )(proj, proj, proj, tail, tail, tail, tail, att, x, *weights)


def _route(logits):
    lane = lax.broadcasted_iota(jnp.int32, logits.shape, 1)
    lane_f = lane.astype(F32)
    neg = -jnp.inf
    big = float(LANES)
    is_grp = (lane >= ROUTER_GROUP_LANE) & (lane < ROUTER_GROUP_LANE + N_GROUPS)
    glog = jnp.where(is_grp, logits, neg)
    gmax = jnp.max(glog, axis=-1, keepdims=True)
    gidx = jnp.min(jnp.where(glog == gmax, lane_f, big), axis=-1, keepdims=True)
    p_sel = 1.0 / jnp.sum(jnp.exp(glog - gmax), axis=-1, keepdims=True)
    first = (gidx - ROUTER_GROUP_LANE) * EXPERTS_PER_GROUP
    in_grp = (lane_f >= first) & (lane_f < first + EXPERTS_PER_GROUP)
    el = jnp.where(in_grp, logits, neg)
    m1 = jnp.max(el, axis=-1, keepdims=True)
    i1 = jnp.min(jnp.where(el == m1, lane_f, big), axis=-1, keepdims=True)
    el2 = jnp.where(lane_f == i1, neg, el)
    m2 = jnp.max(el2, axis=-1, keepdims=True)
    i2 = jnp.min(jnp.where(el2 == m2, lane_f, big), axis=-1, keepdims=True)
    e2 = jnp.exp(m2 - m1)
    w1 = p_sel / (1.0 + e2)
    w2 = w1 * e2
    return i1, i2, w1, w2


def _router_kernel(x_ref, g_ref, wr_ref, br_ref, meta_ref, meta_t_ref, cnt_ref, xs0_ref, run_ref):
    tm = TM_ROUTE

    @pl.when(pl.program_id(0) == 0)
    def _():
        run_ref[...] = jnp.zeros_like(run_ref)

    h = _rms(x_ref[...], g_ref[...])
    h_hi = h.astype(BF16)
    h_lo = (h - h_hi.astype(F32)).astype(BF16)
    hi_terms = jnp.dot(h_hi, wr_ref[...], preferred_element_type=F32)
    logits = (hi_terms[:, :LANES] + hi_terms[:, LANES:]
              + jnp.dot(h_lo, wr_ref[:, :LANES], preferred_element_type=F32) + br_ref[...])
    i1, i2, w1, w2 = _route(logits)
    lane = lax.broadcasted_iota(jnp.int32, (tm, LANES), 1)
    lane_f = lane.astype(F32)
    oh1 = lane_f == i1
    oh2 = lane_f == i2
    onehot = jnp.where(oh1, 1.0, 0.0) + jnp.where(oh2, 1.0, 0.0)
    row_id = lax.broadcasted_iota(jnp.int32, (tm, tm), 0)
    col_id = lax.broadcasted_iota(jnp.int32, (tm, tm), 1)
    below = jnp.where(row_id > col_id, 1.0, 0.0).astype(BF16)
    before = run_ref[...] + jnp.dot(below, onehot.astype(BF16), preferred_element_type=F32)
    r1 = jnp.sum(jnp.where(oh1, before, 0.0), axis=-1, keepdims=True)
    r2 = jnp.sum(jnp.where(oh2, before, 0.0), axis=-1, keepdims=True)
    run_ref[...] += jnp.sum(onehot, axis=0, keepdims=True)
    cnt_ref[...] = run_ref[...]
    meta = jnp.zeros((tm, LANES), F32)
    for col, val in enumerate((i1, i2, r1, r2, w1, w2)):
        meta = jnp.where(lane == col, val, meta)
    meta_ref[...] = meta
    meta_t_ref[...] = meta.T[0:META_ROWS, :]
    xs0_ref[...] = jnp.zeros_like(xs0_ref)


def _router(x, g, wr, br):
    tm = TM_ROUTE
    const = lambda i: (0, 0)
    return pl.pallas_call(
        _router_kernel,
        grid=(TOKENS // tm,),
        in_specs=[
            pl.BlockSpec((tm, D_MODEL), lambda i: (i, 0)),
            pl.BlockSpec((1, D_MODEL), const),
            pl.BlockSpec((D_MODEL, 2 * LANES), const),
            pl.BlockSpec((1, LANES), const),
        ],
        out_specs=[
            pl.BlockSpec((tm, LANES), lambda i: (i, 0)),
            pl.BlockSpec((META_ROWS, tm), lambda i: (0, i)),
            pl.BlockSpec((1, LANES), const),
            pl.BlockSpec((SORTED_ROWS // (TOKENS // tm), LANES), lambda i: (i, 0)),
        ],
        out_shape=(
            jax.ShapeDtypeStruct((TOKENS, LANES), F32),
            jax.ShapeDtypeStruct((META_ROWS, TOKENS), F32),
            jax.ShapeDtypeStruct((1, LANES), F32),
            jax.ShapeDtypeStruct((SORTED_ROWS, LANES), I32),
        ),
        scratch_shapes=[pltpu.VMEM((1, LANES), F32)],
        compiler_params=_cparams(("arbitrary",), 48),
        name="moe_router",
    )(x, g, wr, br)


def _dispatch_plan(meta_t, counts):
    e1 = meta_t[META_E1].astype(jnp.int32)
    e2 = meta_t[META_E2].astype(jnp.int32)
    r1 = meta_t[META_R1].astype(jnp.int32)
    r2 = meta_t[META_R2].astype(jnp.int32)
    cnt = counts[0, :N_EXPERTS].astype(jnp.int32)
    tiles = (cnt + TM_EXP - 1) // TM_EXP
    tile_end = jnp.cumsum(tiles)
    first_slot = ((tile_end - tiles) * TM_EXP)[:, None]
    expert = jnp.arange(N_EXPERTS, dtype=jnp.int32)[:, None]
    pos1 = jnp.sum(jnp.where(e1[None, :] == expert, first_slot, 0), axis=0) + r1
    pos2 = jnp.sum(jnp.where(e2[None, :] == expert, first_slot, 0), axis=0) + r2
    n_tiles = tile_end[-1:]
    tile_id = jnp.minimum(jnp.arange(MAX_TILES, dtype=jnp.int32), n_tiles - 1)
    tile_expert = jnp.sum((tile_id[:, None] >= tile_end[None, :]).astype(jnp.int32), axis=1)
    return pos1, pos2, tile_expert, n_tiles


def _dispatch_kernel(pos1_ref, pos2_ref, x_ref, g_ref, xs_in_ref, xs_ref, slab, sem):
    del xs_in_ref
    tm = TM_DISP
    i = pl.program_id(0)
    last = pl.num_programs(0) - 1
    slot = i % 2

    def wait_copies(sl):
        for _ in range(TOP_K):
            pltpu.make_async_copy(slab.at[sl], xs_ref.at[pl.ds(0, tm * SLAB), :], sem.at[sl]).wait()

    @pl.when(i >= 2)
    def _():
        wait_copies(slot)

    def put(s, block):
        slab[slot, pl.ds(s, tm, stride=SLAB), :] = block

    _store_slabs(put, _pack_rows(_rms(x_ref[...], g_ref[...])))

    def body(c, carry):
        for u in range(GATHER_UNROLL):
            r = c * GATHER_UNROLL + u
            src = slab.at[slot, pl.ds(pl.multiple_of(r * SLAB, SLAB), SLAB), :]
            for k, pos_ref in enumerate((pos1_ref, pos2_ref)):
                p = pos_ref[i * tm + r]
                pltpu.make_async_copy(src, xs_ref.at[pl.ds(pl.multiple_of(p * SLAB, SLAB), SLAB), :],
                                      sem.at[slot]).start(priority=k % N_DMA_PRIORITIES)
        return carry

    lax.fori_loop(0, tm // GATHER_UNROLL, body, 0)

    @pl.when(i == last)
    def _():
        wait_copies(1 - slot)
        wait_copies(slot)


def _dispatch(pos1, pos2, x, g, xs0):
    tm = TM_DISP
    return pl.pallas_call(
        _dispatch_kernel,
        grid_spec=pltpu.PrefetchScalarGridSpec(
            num_scalar_prefetch=2,
            grid=(TOKENS // tm,),
            in_specs=[
                pl.BlockSpec((tm, D_MODEL), lambda i, p1, p2: (i, 0)),
                pl.BlockSpec((1, D_MODEL), lambda i, p1, p2: (0, 0)),
                pl.BlockSpec(memory_space=pl.ANY),
            ],
            out_specs=pl.BlockSpec(memory_space=pl.ANY),
            scratch_shapes=[pltpu.VMEM((2, tm * SLAB, LANES), I32), pltpu.SemaphoreType.DMA((2,))],
        ),
        out_shape=jax.ShapeDtypeStruct((SORTED_ROWS, LANES), I32),
        input_output_aliases={4: 0},
        compiler_params=_cparams(("arbitrary",), 48),
        name="moe_dispatch",
    )(pos1, pos2, x, g, xs0)


def _expert_kernel(te_ref, nt_ref, xs_ref, wg_ref, wu_ref, wd_ref, ys_ref, xbuf, sem):
    del te_ref
    tm = TM_EXP
    i = pl.program_id(0)
    n = nt_ref[0]

    def tile_copy(tile):
        slot = tile % EXPERT_SLOTS
        rows = pl.ds(pl.multiple_of(tile * (tm * SLAB), tm * SLAB), tm * SLAB)
        return pltpu.make_async_copy(xs_ref.at[rows, :], xbuf.at[slot], sem.at[slot])

    @pl.when(i == 0)
    def _():
        for ahead in range(EXPERT_PREFETCH):
            @pl.when(ahead < n)
            def _():
                tile_copy(ahead).start()

    @pl.when(i + EXPERT_PREFETCH < n)
    def _():
        tile_copy(i + EXPERT_PREFETCH).start()

    @pl.when(i < n)
    def _():
        tile_copy(i).wait()
        slot = i % EXPERT_SLOTS
        xt = _unpack_rows(_load_slabs(lambda s: xbuf[slot, pl.ds(s, tm, stride=SLAB), :])).astype(BF16)
        hg = jnp.dot(xt, wg_ref[...].astype(BF16), preferred_element_type=F32)
        hu = jnp.dot(xt, wu_ref[...].astype(BF16), preferred_element_type=F32)
        hh = (hg * _sigmoid(hg) * hu).astype(BF16)
        y = jnp.dot(hh, wd_ref[...].astype(BF16), preferred_element_type=F32)

        def put(s, block):
            ys_ref[pl.ds(s, tm, stride=SLAB), :] = block

        _store_slabs(put, _pack_rows(y))

    @pl.when(i >= n)
    def _():
        ys_ref[...] = jnp.zeros_like(ys_ref)


def _experts(layer, tile_expert, n_tiles, xs, w_gate, w_up, w_down):
    tm = TM_EXP
    wspec = lambda rows, cols: pl.BlockSpec((None, None, rows, cols), lambda i, te, nt: (layer, te[i], 0, 0))
    return pl.pallas_call(
        _expert_kernel,
        grid_spec=pltpu.PrefetchScalarGridSpec(
            num_scalar_prefetch=2,
            grid=(MAX_TILES,),
            in_specs=[
                pl.BlockSpec(memory_space=pl.ANY),
                wspec(D_MODEL, EXPERT_HIDDEN),
                wspec(D_MODEL, EXPERT_HIDDEN),
                wspec(EXPERT_HIDDEN, D_MODEL),
            ],
            out_specs=pl.BlockSpec((tm * SLAB, LANES), lambda i, te, nt: (i, 0)),
            scratch_shapes=[pltpu.VMEM((EXPERT_SLOTS, tm * SLAB, LANES), I32), pltpu.SemaphoreType.DMA((EXPERT_SLOTS,))],
        ),
        out_shape=jax.ShapeDtypeStruct((SORTED_ROWS, LANES), I32),
        compiler_params=_cparams(("arbitrary",), 48),
        name="moe_experts",
    )(tile_expert, n_tiles, xs, w_gate, w_up, w_down)


def _start_slab_gathers(idx_refs, base, n_rows, src_hbm, dst_bufs, sem):
    def body(c, carry):
        for u in range(GATHER_UNROLL):
            r = c * GATHER_UNROLL + u
            for k, (idx_ref, dst) in enumerate(zip(idx_refs, dst_bufs)):
                t = idx_ref[base + r]
                pltpu.make_async_copy(src_hbm.at[pl.ds(pl.multiple_of(t * SLAB, SLAB), SLAB), :],
                                      dst.at[pl.ds(pl.multiple_of(r * SLAB, SLAB), SLAB), :],
                                      sem).start(priority=k % N_DMA_PRIORITIES)
        return carry

    lax.fori_loop(0, n_rows // GATHER_UNROLL, body, 0)


def _wait_slab_gathers(n_rows, src_hbm, dst, sem):
    pltpu.make_async_copy(src_hbm.at[pl.ds(0, n_rows * SLAB), :], dst, sem).wait()


def _combine_ple_kernel(pos1_ref, pos2_ref, x_ref, meta_ref, ys_ref, p_ref, g_ref, wg_ref, wp_ref, fg_ref,
                        out_ref, cbuf, sem, *, final):
    tm = TM_COMB
    i = pl.program_id(0)

    def gather(tile, slot):
        _start_slab_gathers((pos1_ref, pos2_ref), tile * tm, tm, ys_ref, (cbuf.at[slot, 0], cbuf.at[slot, 1]),
                            sem.at[slot])

    @pl.when(i == 0)
    def _():
        gather(0, 0)

    @pl.when(i + 1 < pl.num_programs(0))
    def _():
        gather(i + 1, (i + 1) % 2)

    slot = i % 2
    for k in range(2):
        _wait_slab_gathers(tm, ys_ref, cbuf.at[slot, k], sem.at[slot])
    meta = meta_ref[...]
    w1 = meta[:, META_W1:META_W1 + 1]
    w2 = meta[:, META_W2:META_W2 + 1]
    y1, y2 = (_unpack_rows(_load_slabs(lambda s, k=k: cbuf[slot, k, pl.ds(s, tm, stride=SLAB), :]))
              for k in range(TOP_K))
    x = x_ref[...] + (w1 * y1 + w2 * y2)
    h = _rms(x, g_ref[...]).astype(BF16)
    gate = _sigmoid(jnp.dot(h, wg_ref[...], preferred_element_type=F32))
    emb = jnp.dot(p_ref[...].astype(BF16), wp_ref[...], preferred_element_type=F32)
    y = x + gate * emb
    if final:
        y = _rms(y, fg_ref[...])
    out_ref[...] = y


def _combine_ple(layer, pos1, pos2, x, meta, ys, p, g, wg, wp, fg, final):
    tm = TM_COMB
    const = lambda i, p1, p2: (0, 0)
    rows = lambda i, p1, p2: (i, 0)
    return pl.pallas_call(
        functools.partial(_combine_ple_kernel, final=final),
        grid_spec=pltpu.PrefetchScalarGridSpec(
            num_scalar_prefetch=2,
            grid=(TOKENS // tm,),
            in_specs=[
                pl.BlockSpec((tm, D_MODEL), rows),
                pl.BlockSpec((tm, LANES), rows),
                pl.BlockSpec(memory_space=pl.ANY),
                pl.BlockSpec((None, tm, PLE_DIM), lambda i, p1, p2: (layer, i, 0)),
                pl.BlockSpec((1, D_MODEL), const),
                pl.BlockSpec((D_MODEL, D_MODEL), const),
                pl.BlockSpec((PLE_DIM, D_MODEL), const),
                pl.BlockSpec((1, D_MODEL), const),
            ],
            out_specs=pl.BlockSpec((tm, D_MODEL), rows),
            scratch_shapes=[pltpu.VMEM((2, TOP_K, tm * SLAB, LANES), I32), pltpu.SemaphoreType.DMA((2,))],
        ),
        out_shape=jax.ShapeDtypeStruct((TOKENS, D_MODEL), F32),
        compiler_params=_cparams(("arbitrary",), 48),
        name="moe_combine_ple",
    )(pos1, pos2, x, meta, ys, p, g, wg, wp, fg)


def _in_proj_weights(w_in):
    k_rope = w_in[:, MAIN_COLS:MAIN_COLS + QK_ROPE]
    glu = w_in[:, MAIN_COLS + QK_ROPE:]
    half = QK_ROPE // 2
    zn = jnp.zeros((D_MODEL, QK_NOPE), F32)
    zp = jnp.zeros((D_MODEL, HEAD_PAD - QK_NOPE - QK_ROPE), F32)
    tail = jnp.concatenate([glu, zn, k_rope, zp, zn, k_rope[:, half:], k_rope[:, :half], zp], axis=1)
    return w_in[:, :MAIN_COLS].astype(BF16), tail.astype(BF16)


def _q_weight(w_uq):
    scale = (QK_NOPE + QK_ROPE) ** -0.5 * LOG2_E
    w = (w_uq * scale).reshape(Q_LORA, N_HEADS, QK_NOPE + QK_ROPE)
    zp = jnp.zeros((Q_LORA, N_HEADS, HEAD_PAD - QK_NOPE - QK_ROPE), F32)
    return jnp.concatenate([w, zp], axis=2).reshape(Q_LORA, N_HEADS * HEAD_PAD).astype(BF16)


def _kv_weight(w_ukv):
    w = w_ukv.reshape(KV_LORA, N_HEADS, QK_NOPE + V_HEAD)
    k_nope, v = w[:, :, :QK_NOPE], w[:, :, QK_NOPE:]
    z = jnp.zeros_like(v)
    k_part = jnp.concatenate([k_nope, jnp.zeros_like(k_nope)], axis=2).reshape(KV_LORA, N_HEADS * HEAD_PAD)
    odd = (jnp.arange(N_HEADS) % 2 == 1)[None, :, None]
    v_part = jnp.concatenate([jnp.where(odd, z, v), jnp.where(odd, v, z)], axis=2).reshape(KV_LORA, N_HEADS * HEAD_PAD)
    return jnp.concatenate([k_part, v_part], axis=1).astype(BF16)


def _router_weight(w_rg, b_rg, w_re, b_re):
    pad = LANES - N_EXPERTS - N_GROUPS
    w = jnp.concatenate([w_re, w_rg, jnp.zeros((D_MODEL, pad), F32)], axis=1)
    b = jnp.concatenate([b_re, b_rg, jnp.zeros((pad,), F32)]).reshape(1, LANES)
    w_hi = w.astype(BF16)
    w_lo = (w - w_hi.astype(F32)).astype(BF16)
    return jnp.concatenate([w_hi, w_lo], axis=1), b


def kernel(x, p, positions, ln_mix_g, w_in, conv_a_w, w_out_a, q_norm_g, w_uq, kv_norm_g, w_ukv, w_out_b, conv_c_w, ln_c_g, ln_c_b, w_out_c, w_o, ln_ffn_g, w_route_grp, b_route_grp, w_route_exp, b_route_exp, w_exp_gate, w_exp_up, w_exp_down, ln_ple_g, w_ple_gate, w_ple, final_norm_g):
    c_tab, s_tab = _rope_tables(positions)
    xf = x.reshape(TOKENS, D_MODEL)
    pf = p.reshape(DEPTH, TOKENS, PLE_DIM)
    row = lambda a: a.reshape(1, -1)
    for i in range(DEPTH):
        proj, tail = _inproj(xf, row(ln_mix_g[i]), *_in_proj_weights(w_in[i]))
        q, k, v = _qkv(proj, tail, c_tab, s_tab, row(q_norm_g[i]), row(kv_norm_g[i]), _q_weight(w_uq[i]), _kv_weight(w_ukv[i]))
        att = _attention(q, k, v)
        xf = _mixer_tail(proj, tail, att, xf, conv_a_w[i], w_out_a[i].astype(BF16), conv_c_w[i], row(ln_c_g[i]),
                         row(ln_c_b[i]), w_out_c[i].astype(BF16), w_out_b[i].astype(BF16), w_o[i].astype(BF16))
        wr, br = _router_weight(w_route_grp[i], b_route_grp[i], w_route_exp[i], b_route_exp[i])
        meta, meta_t, counts, xs0 = _router(xf, row(ln_ffn_g[i]), wr, br)
        pos1, pos2, tile_expert, n_tiles = _dispatch_plan(meta_t, counts)
        xs = _dispatch(pos1, pos2, xf, row(ln_ffn_g[i]), xs0)
        ys = _experts(i, tile_expert, n_tiles, xs, w_exp_gate, w_exp_up, w_exp_down)
        xf = _combine_ple(i, pos1, pos2, xf, meta, ys, pf, row(ln_ple_g[i]), w_ple_gate[i].astype(BF16),
                          w_ple[i].astype(BF16), row(final_norm_g), final=(i == DEPTH - 1))
    return xf.reshape(BATCH, SEQ, D_MODEL)
```

```python
import functools

import jax
import jax.numpy as jnp
from jax import lax
from jax.experimental import pallas as pl
from jax.experimental.pallas import tpu as pltpu

D_MODEL = 1024
BATCH = 8
SEQ = 2048
DEPTH = 2
TOKENS = BATCH * SEQ
PLE_DIM = 256
SC_WIDTH = 512
SC_KERNEL = 3
N_HEADS = 8
QK_NOPE = 64
QK_ROPE = 32
V_HEAD = 64
Q_LORA = 768
KV_LORA = 256
ROPE_THETA = 10000.0
CONF_WIDTH = 512
CONF_KERNEL = 31
N_GROUPS = 4
EXPERTS_PER_GROUP = 8
N_EXPERTS = N_GROUPS * EXPERTS_PER_GROUP
EXPERT_HIDDEN = 256
EPS = 1e-6
LOG2_E = 1.4426950408889634

LANES = 128
HEAD_PAD = 128
F32 = jnp.float32
BF16 = jnp.bfloat16

COL_GATES = 0
COL_SC = 3 * D_MODEL
COL_QLAT = COL_SC + 3 * SC_WIDTH
COL_KVLAT = COL_QLAT + Q_LORA
MAIN_COLS = COL_KVLAT + KV_LORA
TCOL_GLU = 0
TCOL_KR = 2 * CONF_WIDTH
TCOL_KRSW = TCOL_KR + HEAD_PAD
TAIL_COLS = TCOL_KRSW + HEAD_PAD

ROUTER_GROUP_LANE = N_EXPERTS

MXU_TILE = 256
VMEM_MB = 48
VMEM_BIG_MB = 56

TM_INPROJ = 1024
TN_INPROJ = MAIN_COLS // 2
TN_SPLIT = 6 * MXU_TILE
TM_QKV = 512
T_ATTN = 512
HEADS_PER_STEP = 8
TM_MIX = 512
CONV_CHUNK = 64
HALO_C = 32
HALO_A = 16
TM_ROUTE = 512
TM_EXP = 512
EXPERT_PREFETCH = 2
EXPERT_SLOTS = EXPERT_PREFETCH + 1
TM_DISP = 512
TM_COMB = 256
TOP_K = 2
MAX_TILES = TOKENS * TOP_K // TM_EXP + N_EXPERTS
SLAB = D_MODEL // LANES
GATHER_UNROLL = 16
N_DMA_PRIORITIES = 2
META_E1, META_E2, META_R1, META_R2, META_W1, META_W2 = range(6)
META_ROWS = 8
SORTED_ROWS = MAX_TILES * TM_EXP * SLAB


def _cparams(semantics, vmem_mb):
    return pltpu.CompilerParams(dimension_semantics=semantics, vmem_limit_bytes=vmem_mb * 1024 * 1024)


def _sigmoid(x):
    return 1.0 / (1.0 + jnp.exp2(x * -LOG2_E))


def _rms(x, g):
    return x * lax.rsqrt(jnp.mean(x * x, axis=-1, keepdims=True) + EPS) * g


def _place(x, onehot):
    x1 = x.astype(BF16)
    r1 = x - x1.astype(F32)
    x2 = r1.astype(BF16)
    x3 = (r1 - x2.astype(F32)).astype(BF16)
    return (jnp.dot(x1, onehot, preferred_element_type=F32) + jnp.dot(x2, onehot, preferred_element_type=F32)
            + jnp.dot(x3, onehot, preferred_element_type=F32))


def _rope_kernel(pos_ref, freq_ref, c_ref, s_ref):
    half = QK_ROPE // 2
    per_row = LANES // half
    rows = TOKENS // per_row
    ang = pos_ref[...].astype(F32) * freq_ref[...]
    cos = jnp.cos(ang)
    sin = jnp.sin(ang)
    src = lax.broadcasted_iota(jnp.int32, (LANES, LANES), 0)
    dst = lax.broadcasted_iota(jnp.int32, (LANES, LANES), 1)
    lane = lax.broadcasted_iota(jnp.int32, (1, LANES), 1)
    ones_nope = jnp.where(lane < QK_NOPE, 1.0, 0.0)
    sign = jnp.where(lane < QK_NOPE + half, -1.0, 1.0)
    for j in range(per_row):
        f = src - half * j
        hit = (dst == QK_NOPE + f) | (dst == QK_NOPE + half + f)
        onehot = jnp.where((f >= 0) & (f < half) & hit, 1.0, 0.0).astype(BF16)
        c_ref[pl.ds(j, rows, stride=per_row), :] = _place(cos, onehot) + ones_nope
        s_ref[pl.ds(j, rows, stride=per_row), :] = _place(sin, onehot) * sign


def _rope_tables(positions):
    half = QK_ROPE // 2
    inv_freq = ROPE_THETA ** (-jnp.arange(0, QK_ROPE, 2, dtype=F32) / QK_ROPE)
    rows = TOKENS * half // LANES
    pos_rep = jnp.broadcast_to(positions.reshape(TOKENS, 1), (TOKENS, half)).reshape(rows, LANES)
    freq = jnp.tile(inv_freq, LANES // half).reshape(1, LANES)
    return pl.pallas_call(
        _rope_kernel,
        out_shape=(jax.ShapeDtypeStruct((TOKENS, HEAD_PAD), F32),) * 2,
        compiler_params=_cparams(None, VMEM_MB),
        name="rope_tables",
    )(pos_rep, freq)


def _inproj_kernel(x_ref, g_ref, wm_ref, wt_ref, om_ref, ot_ref, h_ref):
    j = pl.program_id(1)
    n_main = MAIN_COLS // TN_INPROJ

    @pl.when(j == 0)
    def _():
        h_ref[...] = _rms(x_ref[...], g_ref[...]).astype(BF16)

    @pl.when(j < n_main)
    def _():
        for lo, hi in ((0, TN_SPLIT), (TN_SPLIT, TN_INPROJ)):
            om_ref[:, lo:hi] = jnp.dot(h_ref[...], wm_ref[:, lo:hi], preferred_element_type=F32).astype(BF16)

    @pl.when(j == n_main)
    def _():
        ot_ref[...] = jnp.dot(h_ref[...], wt_ref[...], preferred_element_type=F32).astype(BF16)


def _inproj(x, g, w_main, w_tail):
    tm, tn = TM_INPROJ, TN_INPROJ
    n_main = MAIN_COLS // tn
    main_col = lambda i, j: jnp.minimum(j, n_main - 1)
    return pl.pallas_call(
        _inproj_kernel,
        grid=(TOKENS // tm, n_main + 1),
        in_specs=[
            pl.BlockSpec((tm, D_MODEL), lambda i, j: (i, 0)),
            pl.BlockSpec((1, D_MODEL), lambda i, j: (0, 0)),
            pl.BlockSpec((D_MODEL, tn), lambda i, j: (0, main_col(i, j))),
            pl.BlockSpec((D_MODEL, TAIL_COLS), lambda i, j: (0, 0)),
        ],
        out_specs=[
            pl.BlockSpec((tm, tn), lambda i, j: (i, main_col(i, j))),
            pl.BlockSpec((tm, TAIL_COLS), lambda i, j: (i, 0)),
        ],
        out_shape=(jax.ShapeDtypeStruct((TOKENS, MAIN_COLS), BF16), jax.ShapeDtypeStruct((TOKENS, TAIL_COLS), BF16)),
        scratch_shapes=[pltpu.VMEM((tm, D_MODEL), BF16)],
        compiler_params=_cparams(("parallel", "arbitrary"), VMEM_BIG_MB),
        name="in_proj",
    )(x, g, w_main, w_tail)


def _qkv_kernel(ql_ref, kvl_ref, kr_ref, krsw_ref, c_ref, s_ref, qg_ref, kvg_ref, wq_ref, wkv_ref,
                q_out, k_out, v_out):
    c = c_ref[...]
    s = s_ref[...]
    width = N_HEADS * HEAD_PAD
    half = QK_ROPE // 2
    low_half = lax.broadcasted_iota(jnp.int32, c.shape, 1) < QK_NOPE + half

    def swap_halves(x):
        return jnp.where(low_half, pltpu.roll(x, HEAD_PAD - half, axis=1), pltpu.roll(x, half, axis=1))

    qn = _rms(ql_ref[...].astype(F32), qg_ref[...]).astype(BF16)
    qq = jnp.dot(qn, wq_ref[...], preferred_element_type=F32)
    for h in range(N_HEADS):
        lo, hi = h * HEAD_PAD, (h + 1) * HEAD_PAD
        q_out[:, lo:hi] = (qq[:, lo:hi] * c + swap_halves(qq[:, lo:hi]) * s).astype(BF16)
    kvn = _rms(kvl_ref[...].astype(F32), kvg_ref[...]).astype(BF16)
    kk = jnp.dot(kvn, wkv_ref[...], preferred_element_type=F32)
    kr = kr_ref[...].astype(F32) * c + krsw_ref[...].astype(F32) * s
    for h in range(N_HEADS):
        lo, hi = h * HEAD_PAD, (h + 1) * HEAD_PAD
        k_out[:, lo:hi] = (kk[:, lo:hi] + kr).astype(BF16)
    v_out[...] = kk[:, width:].astype(BF16)


def _qkv(proj, tail, c_tab, s_tab, qg, kvg, wq, wkv):
    tm = TM_QKV
    width = N_HEADS * HEAD_PAD
    row = lambda blk: (lambda i: (i, blk))
    const = lambda i: (0, 0)
    return pl.pallas_call(
        _qkv_kernel,
        grid=(TOKENS // tm,),
        in_specs=[
            pl.BlockSpec((tm, Q_LORA), row(COL_QLAT // Q_LORA)),
            pl.BlockSpec((tm, KV_LORA), row(COL_KVLAT // KV_LORA)),
            pl.BlockSpec((tm, HEAD_PAD), row(TCOL_KR // HEAD_PAD)),
            pl.BlockSpec((tm, HEAD_PAD), row(TCOL_KRSW // HEAD_PAD)),
            pl.BlockSpec((tm, HEAD_PAD), row(0)),
            pl.BlockSpec((tm, HEAD_PAD), row(0)),
            pl.BlockSpec((1, Q_LORA), const),
            pl.BlockSpec((1, KV_LORA), const),
            pl.BlockSpec((Q_LORA, width), const),
            pl.BlockSpec((KV_LORA, 2 * width), const),
        ],
        out_specs=[pl.BlockSpec((tm, width), row(0))] * 3,
        out_shape=(jax.ShapeDtypeStruct((TOKENS, width), BF16),) * 3,
        compiler_params=_cparams(("parallel",), VMEM_MB),
        name="qkv_prep",
    )(proj, proj, tail, tail, c_tab, s_tab, qg, kvg, wq, wkv)


def _attn_kernel(q_ref, k_ref, v_ref, o_ref, *state):
    t = T_ATTN
    nh = HEADS_PER_STEP
    qi = pl.program_id(2)
    nt = (((1,), (1,)), ((), ()))
    m_sc, l_sc, acc_sc = state[0:nh], state[nh:2 * nh], state[2 * nh:3 * nh]
    for h in range(nh):
        m_sc[h][...] = jnp.full((t, LANES), -jnp.inf, F32)
        l_sc[h][...] = jnp.zeros((t, LANES), F32)
        acc_sc[h][...] = jnp.zeros((t, LANES), F32)

    def block(j, r0, nr, c0, nc, masked):
        start = pl.multiple_of(j * t + c0, nc)
        rows = slice(r0, r0 + nr)
        for h in range(nh):
            lo, hi = h * HEAD_PAD, (h + 1) * HEAD_PAD
            s = lax.dot_general(q_ref[rows, lo:hi], k_ref[pl.ds(start, nc), lo:hi], nt, preferred_element_type=F32)
            if masked:
                row_id = r0 + lax.broadcasted_iota(jnp.int32, (nr, nc), 0)
                col_id = c0 + lax.broadcasted_iota(jnp.int32, (nr, nc), 1)
                s = jnp.where(row_id >= col_id, s, -jnp.inf)
            blocks = [s[:, c * LANES:(c + 1) * LANES] for c in range(nc // LANES)]
            bmax = functools.reduce(jnp.maximum, blocks)
            m_old = m_sc[h][rows, :]
            m_new = jnp.maximum(m_old, jnp.max(bmax, axis=-1, keepdims=True))
            alpha = jnp.exp2(m_old - m_new)
            ps = [jnp.exp2(b - m_new) for b in blocks]
            p = jnp.concatenate(ps, axis=1).astype(BF16)
            l_sc[h][rows, :] = alpha * l_sc[h][rows, :] + functools.reduce(jnp.add, ps)
            acc_sc[h][rows, :] = alpha * acc_sc[h][rows, :] + jnp.dot(p, v_ref[pl.ds(start, nc), lo:hi],
                                                                       preferred_element_type=F32)
            m_sc[h][rows, :] = m_new

    def body(j, carry):
        block(j, 0, t, 0, t, False)
        return carry

    lax.fori_loop(0, qi, body, 0)
    half = t // 2
    block(qi, 0, t, 0, half, True)
    block(qi, half, half, half, half, True)
    out = [acc_sc[h][...] / jnp.sum(l_sc[h][...], axis=-1, keepdims=True) for h in range(nh)]
    o_ref[...] = jnp.concatenate([out[h] + out[h + 1] for h in range(0, nh, 2)], axis=1).astype(BF16)


def _attention(q, k, v):
    t = T_ATTN
    nq = SEQ // t
    nh = HEADS_PER_STEP
    return pl.pallas_call(
        _attn_kernel,
        grid=(BATCH, N_HEADS // nh, nq),
        in_specs=[
            pl.BlockSpec((t, nh * HEAD_PAD), lambda b, hg, i: (b * nq + i, hg)),
            pl.BlockSpec((SEQ, nh * HEAD_PAD), lambda b, hg, i: (b, hg)),
            pl.BlockSpec((SEQ, nh * HEAD_PAD), lambda b, hg, i: (b, hg)),
        ],
        out_specs=pl.BlockSpec((t, nh * V_HEAD), lambda b, hg, i: (b * nq + i, hg)),
        out_shape=jax.ShapeDtypeStruct((TOKENS, N_HEADS * V_HEAD), BF16),
        scratch_shapes=[pltpu.VMEM((t, LANES), F32)] * (3 * nh),
        compiler_params=_cparams(("parallel", "parallel", "arbitrary"), VMEM_MB),
        name="mla_attention",
    )(q, k, v)


def _mixer_tail_kernel(gates_ref, sc_ref, sch_ref, gv_ref, gg_ref, gvh_ref, ggh_ref, att_ref, x_ref,
                       cwa_ref, woa_ref, cwc_ref, lng_ref, lnb_ref, woc_ref, wob_ref, wo_ref,
                       out_ref, cbuf, ubuf, shifted, vbuf):
    tm = TM_MIX
    has_past = (pl.program_id(0) % (SEQ // tm)) != 0

    sc = sc_ref[...]
    sc_b = sc[:, 0:SC_WIDTH].astype(F32)
    cbuf[8:8 + tm, :] = sc[:, SC_WIDTH:2 * SC_WIDTH].astype(F32) * sc[:, 2 * SC_WIDTH:].astype(F32)
    sch = sch_ref[...].astype(F32)[HALO_A - 8:HALO_A, :]
    cbuf[0:8, :] = jnp.where(has_past, sch[:, SC_WIDTH:2 * SC_WIDTH] * sch[:, 2 * SC_WIDTH:], 0.0)
    conv_a = cwa_ref[0:1, :] * cbuf[6:6 + tm, :]
    for t in range(1, SC_KERNEL):
        conv_a = conv_a + cwa_ref[t:t + 1, :] * cbuf[6 + t:6 + t + tm, :]
    y_a = jnp.dot((sc_b * conv_a).astype(BF16), woa_ref[...], preferred_element_type=F32)

    ubuf[HALO_C:HALO_C + tm, :] = gv_ref[...].astype(F32) * _sigmoid(gg_ref[...].astype(F32))
    ubuf[0:HALO_C, :] = jnp.where(has_past, gvh_ref[...].astype(F32) * _sigmoid(ggh_ref[...].astype(F32)), 0.0)
    rows = tm + HALO_C - 8
    u_all = ubuf[...]
    for b in range(1, 8):
        shifted[b - 1, 0:rows, :] = pltpu.roll(u_all, tm + HALO_C - b, axis=0)[0:rows, :]
    base = HALO_C - (CONF_KERNEL - 1)
    for r0 in range(0, tm, CONV_CHUNK):
        acc = None
        for t in range(CONF_KERNEL):
            off = base + t
            a0 = r0 + off - off % 8
            src = ubuf[a0:a0 + CONV_CHUNK, :] if off % 8 == 0 else shifted[off % 8 - 1, a0:a0 + CONV_CHUNK, :]
            term = cwc_ref[t:t + 1, :] * src
            acc = term if acc is None else acc + term
        mu = jnp.mean(acc, axis=-1, keepdims=True)
        xc = acc - mu
        var = jnp.mean(xc * xc, axis=-1, keepdims=True)
        y = xc * lax.rsqrt(var + EPS) * lng_ref[...] + lnb_ref[...]
        vbuf[r0:r0 + CONV_CHUNK, :] = (y * _sigmoid(y)).astype(BF16)
    y_c = jnp.dot(vbuf[...], woc_ref[...], preferred_element_type=F32)

    y_b = jnp.dot(att_ref[...], wob_ref[...], preferred_element_type=F32)

    g = gates_ref[...]
    merged = (_sigmoid(g[:, 0:D_MODEL].astype(F32)) * y_a
              + _sigmoid(g[:, D_MODEL:2 * D_MODEL].astype(F32)) * y_b
              + _sigmoid(g[:, 2 * D_MODEL:].astype(F32)) * y_c)
    out_ref[...] = x_ref[...] + jnp.dot(merged.astype(BF16), wo_ref[...], preferred_element_type=F32)


def _mixer_tail(proj, tail, att, x, cwa, woa, cwc, lng, lnb, woc, wob, wo):
    tm = TM_MIX
    row = lambda width, col: pl.BlockSpec((tm, width), lambda i: (i, col // width))
    halo = lambda rows, width, col: pl.BlockSpec(
        (rows, width), lambda i: (jnp.maximum(i * (tm // rows) - 1, 0), col // width))
    const = lambda a: pl.BlockSpec(a.shape, lambda i: (0,) * a.ndim)
    weights = (cwa, woa, cwc, lng, lnb, woc, wob, wo)
    return pl.pallas_call(
        _mixer_tail_kernel,
        grid=(TOKENS // tm,),
        in_specs=[
            row(3 * D_MODEL, COL_GATES),
            row(3 * SC_WIDTH, COL_SC),
            halo(HALO_A, 3 * SC_WIDTH, COL_SC),
            row(CONF_WIDTH, TCOL_GLU),
            row(CONF_WIDTH, TCOL_GLU + CONF_WIDTH),
            halo(HALO_C, CONF_WIDTH, TCOL_GLU),
            halo(HALO_C, CONF_WIDTH, TCOL_GLU + CONF_WIDTH),
            pl.BlockSpec((tm, N_HEADS * V_HEAD), lambda i: (i, 0)),
            pl.BlockSpec((tm, D_MODEL), lambda i: (i, 0)),
        ] + [const(a) for a in weights],
        out_specs=pl.BlockSpec((tm, D_MODEL), lambda i: (i, 0)),
        out_shape=jax.ShapeDtypeStruct((TOKENS, D_MODEL), F32),
        scratch_shapes=[
            pltpu.VMEM((tm + 8, SC_WIDTH), F32),
            pltpu.VMEM((tm + HALO_C, CONF_WIDTH), F32),
            pltpu.VMEM((7, tm + HALO_C - 8, CONF_WIDTH), F32),
            pltpu.VMEM((tm, CONF_WIDTH), BF16),
        ],
        compiler_params=_cparams(("parallel",), VMEM_BIG_MB),
        name="mixer_tail",
    )(proj, proj, proj, tail, tail, tail, tail, att, x, *weights)


def _route(logits):
    lane = lax.broadcasted_iota(jnp.int32, logits.shape, 1)
    lane_f = lane.astype(F32)
    neg = -jnp.inf
    big = float(LANES)
    is_grp = (lane >= ROUTER_GROUP_LANE) & (lane < ROUTER_GROUP_LANE + N_GROUPS)
    glog = jnp.where(is_grp, logits, neg)
    gmax = jnp.max(glog, axis=-1, keepdims=True)
    gidx = jnp.min(jnp.where(glog == gmax, lane_f, big), axis=-1, keepdims=True)
    p_sel = 1.0 / jnp.sum(jnp.exp(glog - gmax), axis=-1, keepdims=True)
    first = (gidx - ROUTER_GROUP_LANE) * EXPERTS_PER_GROUP
    in_grp = (lane_f >= first) & (lane_f < first + EXPERTS_PER_GROUP)
    el = jnp.where(in_grp, logits, neg)
    m1 = jnp.max(el, axis=-1, keepdims=True)
    i1 = jnp.min(jnp.where(el == m1, lane_f, big), axis=-1, keepdims=True)
    el2 = jnp.where(lane_f == i1, neg, el)
    m2 = jnp.max(el2, axis=-1, keepdims=True)
    i2 = jnp.min(jnp.where(el2 == m2, lane_f, big), axis=-1, keepdims=True)
    e2 = jnp.exp(m2 - m1)
    w1 = p_sel / (1.0 + e2)
    w2 = w1 * e2
    return i1, i2, w1, w2


def _router_kernel(x_ref, g_ref, wr_ref, br_ref, meta_ref, meta_t_ref, cnt_ref, xs0_ref, run_ref):
    tm = TM_ROUTE

    @pl.when(pl.program_id(0) == 0)
    def _():
        run_ref[...] = jnp.zeros_like(run_ref)

    h = _rms(x_ref[...], g_ref[...])
    h_hi = h.astype(BF16)
    h_lo = (h - h_hi.astype(F32)).astype(BF16)
    hi_terms = jnp.dot(h_hi, wr_ref[...], preferred_element_type=F32)
    logits = (hi_terms[:, :LANES] + hi_terms[:, LANES:]
              + jnp.dot(h_lo, wr_ref[:, :LANES], preferred_element_type=F32) + br_ref[...])
    i1, i2, w1, w2 = _route(logits)
    lane = lax.broadcasted_iota(jnp.int32, (tm, LANES), 1)
    lane_f = lane.astype(F32)
    oh1 = lane_f == i1
    oh2 = lane_f == i2
    onehot = jnp.where(oh1, 1.0, 0.0) + jnp.where(oh2, 1.0, 0.0)
    row_id = lax.broadcasted_iota(jnp.int32, (tm, tm), 0)
    col_id = lax.broadcasted_iota(jnp.int32, (tm, tm), 1)
    below = jnp.where(row_id > col_id, 1.0, 0.0).astype(BF16)
    before = run_ref[...] + jnp.dot(below, onehot.astype(BF16), preferred_element_type=F32)
    r1 = jnp.sum(jnp.where(oh1, before, 0.0), axis=-1, keepdims=True)
    r2 = jnp.sum(jnp.where(oh2, before, 0.0), axis=-1, keepdims=True)
    run_ref[...] += jnp.sum(onehot, axis=0, keepdims=True)
    cnt_ref[...] = run_ref[...]
    meta = jnp.zeros((tm, LANES), F32)
    for col, val in enumerate((i1, i2, r1, r2, w1, w2)):
        meta = jnp.where(lane == col, val, meta)
    meta_ref[...] = meta
    meta_t_ref[...] = meta.T[0:META_ROWS, :]
    xs0_ref[...] = jnp.zeros_like(xs0_ref)


def _router(x, g, wr, br):
    tm = TM_ROUTE
    const = lambda i: (0, 0)
    return pl.pallas_call(
        _router_kernel,
        grid=(TOKENS // tm,),
        in_specs=[
            pl.BlockSpec((tm, D_MODEL), lambda i: (i, 0)),
            pl.BlockSpec((1, D_MODEL), const),
            pl.BlockSpec((D_MODEL, 2 * LANES), const),
            pl.BlockSpec((1, LANES), const),
        ],
        out_specs=[
            pl.BlockSpec((tm, LANES), lambda i: (i, 0)),
            pl.BlockSpec((META_ROWS, tm), lambda i: (0, i)),
            pl.BlockSpec((1, LANES), const),
            pl.BlockSpec((SORTED_ROWS // (TOKENS // tm), LANES), lambda i: (i, 0)),
        ],
        out_shape=(
            jax.ShapeDtypeStruct((TOKENS, LANES), F32),
            jax.ShapeDtypeStruct((META_ROWS, TOKENS), F32),
            jax.ShapeDtypeStruct((1, LANES), F32),
            jax.ShapeDtypeStruct((SORTED_ROWS, LANES), F32),
        ),
        scratch_shapes=[pltpu.VMEM((1, LANES), F32)],
        compiler_params=_cparams(("arbitrary",), VMEM_MB),
        name="moe_router",
    )(x, g, wr, br)


def _dispatch_plan(meta_t, counts):
    e1 = meta_t[META_E1].astype(jnp.int32)
    e2 = meta_t[META_E2].astype(jnp.int32)
    r1 = meta_t[META_R1].astype(jnp.int32)
    r2 = meta_t[META_R2].astype(jnp.int32)
    cnt = counts[0, :N_EXPERTS].astype(jnp.int32)
    tiles = (cnt + TM_EXP - 1) // TM_EXP
    tile_end = jnp.cumsum(tiles)
    first_slot = ((tile_end - tiles) * TM_EXP)[:, None]
    expert = jnp.arange(N_EXPERTS, dtype=jnp.int32)[:, None]
    pos1 = jnp.sum(jnp.where(e1[None, :] == expert, first_slot, 0), axis=0) + r1
    pos2 = jnp.sum(jnp.where(e2[None, :] == expert, first_slot, 0), axis=0) + r2
    n_tiles = tile_end[-1:]
    tile_id = jnp.minimum(jnp.arange(MAX_TILES, dtype=jnp.int32), n_tiles - 1)
    tile_expert = jnp.sum((tile_id[:, None] >= tile_end[None, :]).astype(jnp.int32), axis=1)
    return pos1, pos2, tile_expert, n_tiles


def _dispatch_kernel(pos1_ref, pos2_ref, x_ref, g_ref, xs_in_ref, xs_ref, slab, sem):
    del xs_in_ref
    tm = TM_DISP
    i = pl.program_id(0)
    last = pl.num_programs(0) - 1
    slot = i % 2

    def wait_copies(sl):
        for _ in range(TOP_K):
            pltpu.make_async_copy(slab.at[sl], xs_ref.at[pl.ds(0, tm * SLAB), :], sem.at[sl]).wait()

    @pl.when(i >= 2)
    def _():
        wait_copies(slot)

    h = _rms(x_ref[...], g_ref[...])
    for s in range(SLAB):
        slab[slot, pl.ds(s, tm, stride=SLAB), :] = h[:, s * LANES:(s + 1) * LANES]

    def body(c, carry):
        for u in range(GATHER_UNROLL):
            r = c * GATHER_UNROLL + u
            src = slab.at[slot, pl.ds(pl.multiple_of(r * SLAB, SLAB), SLAB), :]
            for k, pos_ref in enumerate((pos1_ref, pos2_ref)):
                p = pos_ref[i * tm + r]
                pltpu.make_async_copy(src, xs_ref.at[pl.ds(pl.multiple_of(p * SLAB, SLAB), SLAB), :],
                                      sem.at[slot]).start(priority=k % N_DMA_PRIORITIES)
        return carry

    lax.fori_loop(0, tm // GATHER_UNROLL, body, 0)

    @pl.when(i == last)
    def _():
        wait_copies(1 - slot)
        wait_copies(slot)


def _dispatch(pos1, pos2, x, g, xs0):
    tm = TM_DISP
    return pl.pallas_call(
        _dispatch_kernel,
        grid_spec=pltpu.PrefetchScalarGridSpec(
            num_scalar_prefetch=2,
            grid=(TOKENS // tm,),
            in_specs=[
                pl.BlockSpec((tm, D_MODEL), lambda i, p1, p2: (i, 0)),
                pl.BlockSpec((1, D_MODEL), lambda i, p1, p2: (0, 0)),
                pl.BlockSpec(memory_space=pl.ANY),
            ],
            out_specs=pl.BlockSpec(memory_space=pl.ANY),
            scratch_shapes=[pltpu.VMEM((2, tm * SLAB, LANES), F32), pltpu.SemaphoreType.DMA((2,))],
        ),
        out_shape=jax.ShapeDtypeStruct((SORTED_ROWS, LANES), F32),
        input_output_aliases={4: 0},
        compiler_params=_cparams(("arbitrary",), VMEM_MB),
        name="moe_dispatch",
    )(pos1, pos2, x, g, xs0)


def _expert_kernel(te_ref, nt_ref, xs_ref, wg_ref, wu_ref, wd_ref, ys_ref, xbuf, sem):
    del te_ref
    tm = TM_EXP
    i = pl.program_id(0)
    n = nt_ref[0]

    def tile_copy(tile):
        slot = tile % EXPERT_SLOTS
        rows = pl.ds(pl.multiple_of(tile * (tm * SLAB), tm * SLAB), tm * SLAB)
        return pltpu.make_async_copy(xs_ref.at[rows, :], xbuf.at[slot], sem.at[slot])

    @pl.when(i == 0)
    def _():
        for ahead in range(EXPERT_PREFETCH):
            @pl.when(ahead < n)
            def _():
                tile_copy(ahead).start()

    @pl.when(i + EXPERT_PREFETCH < n)
    def _():
        tile_copy(i + EXPERT_PREFETCH).start()

    @pl.when(i < n)
    def _():
        tile_copy(i).wait()
        slot = i % EXPERT_SLOTS
        xt = jnp.concatenate([xbuf[slot, pl.ds(s, tm, stride=SLAB), :].astype(BF16) for s in range(SLAB)], axis=1)
        hg = jnp.dot(xt, wg_ref[...].astype(BF16), preferred_element_type=F32)
        hu = jnp.dot(xt, wu_ref[...].astype(BF16), preferred_element_type=F32)
        hh = (hg * _sigmoid(hg) * hu).astype(BF16)
        y = jnp.dot(hh, wd_ref[...].astype(BF16), preferred_element_type=F32)
        for s in range(SLAB):
            ys_ref[pl.ds(s, tm, stride=SLAB), :] = y[:, s * LANES:(s + 1) * LANES]

    @pl.when(i >= n)
    def _():
        ys_ref[...] = jnp.zeros_like(ys_ref)


def _experts(layer, tile_expert, n_tiles, xs, w_gate, w_up, w_down):
    tm = TM_EXP
    wspec = lambda rows, cols: pl.BlockSpec((None, None, rows, cols), lambda i, te, nt: (layer, te[i], 0, 0))
    return pl.pallas_call(
        _expert_kernel,
        grid_spec=pltpu.PrefetchScalarGridSpec(
            num_scalar_prefetch=2,
            grid=(MAX_TILES,),
            in_specs=[
                pl.BlockSpec(memory_space=pl.ANY),
                wspec(D_MODEL, EXPERT_HIDDEN),
                wspec(D_MODEL, EXPERT_HIDDEN),
                wspec(EXPERT_HIDDEN, D_MODEL),
            ],
            out_specs=pl.BlockSpec((tm * SLAB, LANES), lambda i, te, nt: (i, 0)),
            scratch_shapes=[pltpu.VMEM((EXPERT_SLOTS, tm * SLAB, LANES), F32), pltpu.SemaphoreType.DMA((EXPERT_SLOTS,))],
        ),
        out_shape=jax.ShapeDtypeStruct((SORTED_ROWS, LANES), F32),
        compiler_params=_cparams(("arbitrary",), VMEM_MB),
        name="moe_experts",
    )(tile_expert, n_tiles, xs, w_gate, w_up, w_down)


def _start_slab_gathers(idx_refs, base, n_rows, src_hbm, dst_bufs, sem):
    def body(c, carry):
        for u in range(GATHER_UNROLL):
            r = c * GATHER_UNROLL + u
            for k, (idx_ref, dst) in enumerate(zip(idx_refs, dst_bufs)):
                t = idx_ref[base + r]
                pltpu.make_async_copy(src_hbm.at[pl.ds(pl.multiple_of(t * SLAB, SLAB), SLAB), :],
                                      dst.at[pl.ds(pl.multiple_of(r * SLAB, SLAB), SLAB), :],
                                      sem).start(priority=k % N_DMA_PRIORITIES)
        return carry

    lax.fori_loop(0, n_rows // GATHER_UNROLL, body, 0)


def _wait_slab_gathers(n_rows, src_hbm, dst, sem):
    pltpu.make_async_copy(src_hbm.at[pl.ds(0, n_rows * SLAB), :], dst, sem).wait()


def _combine_ple_kernel(pos1_ref, pos2_ref, x_ref, meta_ref, ys_ref, p_ref, g_ref, wg_ref, wp_ref, fg_ref,
                        out_ref, cbuf, sem, *, final):
    tm = TM_COMB
    i = pl.program_id(0)

    def gather(tile, slot):
        _start_slab_gathers((pos1_ref, pos2_ref), tile * tm, tm, ys_ref, (cbuf.at[slot, 0], cbuf.at[slot, 1]),
                            sem.at[slot])

    @pl.when(i == 0)
    def _():
        gather(0, 0)

    @pl.when(i + 1 < pl.num_programs(0))
    def _():
        gather(i + 1, (i + 1) % 2)

    slot = i % 2
    for k in range(2):
        _wait_slab_gathers(tm, ys_ref, cbuf.at[slot, k], sem.at[slot])
    meta = meta_ref[...]
    w1 = meta[:, META_W1:META_W1 + 1]
    w2 = meta[:, META_W2:META_W2 + 1]
    moe = jnp.concatenate([w1 * cbuf[slot, 0, pl.ds(s, tm, stride=SLAB), :]
                           + w2 * cbuf[slot, 1, pl.ds(s, tm, stride=SLAB), :] for s in range(SLAB)], axis=1)
    x = x_ref[...] + moe
    h = _rms(x, g_ref[...]).astype(BF16)
    gate = _sigmoid(jnp.dot(h, wg_ref[...], preferred_element_type=F32))
    emb = jnp.dot(p_ref[...].astype(BF16), wp_ref[...], preferred_element_type=F32)
    y = x + gate * emb
    if final:
        y = _rms(y, fg_ref[...])
    out_ref[...] = y


def _combine_ple(layer, pos1, pos2, x, meta, ys, p, g, wg, wp, fg, final):
    tm = TM_COMB
    const = lambda i, p1, p2: (0, 0)
    rows = lambda i, p1, p2: (i, 0)
    return pl.pallas_call(
        functools.partial(_combine_ple_kernel, final=final),
        grid_spec=pltpu.PrefetchScalarGridSpec(
            num_scalar_prefetch=2,
            grid=(TOKENS // tm,),
            in_specs=[
                pl.BlockSpec((tm, D_MODEL), rows),
                pl.BlockSpec((tm, LANES), rows),
                pl.BlockSpec(memory_space=pl.ANY),
                pl.BlockSpec((None, tm, PLE_DIM), lambda i, p1, p2: (layer, i, 0)),
                pl.BlockSpec((1, D_MODEL), const),
                pl.BlockSpec((D_MODEL, D_MODEL), const),
                pl.BlockSpec((PLE_DIM, D_MODEL), const),
                pl.BlockSpec((1, D_MODEL), const),
            ],
            out_specs=pl.BlockSpec((tm, D_MODEL), rows),
            scratch_shapes=[pltpu.VMEM((2, 2, tm * SLAB, LANES), F32), pltpu.SemaphoreType.DMA((2,))],
        ),
        out_shape=jax.ShapeDtypeStruct((TOKENS, D_MODEL), F32),
        compiler_params=_cparams(("arbitrary",), VMEM_MB),
        name="moe_combine_ple",
    )(pos1, pos2, x, meta, ys, p, g, wg, wp, fg)


def _in_proj_weights(w_in):
    k_rope = w_in[:, MAIN_COLS:MAIN_COLS + QK_ROPE]
    glu = w_in[:, MAIN_COLS + QK_ROPE:]
    half = QK_ROPE // 2
    zn = jnp.zeros((D_MODEL, QK_NOPE), F32)
    zp = jnp.zeros((D_MODEL, HEAD_PAD - QK_NOPE - QK_ROPE), F32)
    tail = jnp.concatenate([glu, zn, k_rope, zp, zn, k_rope[:, half:], k_rope[:, :half], zp], axis=1)
    return w_in[:, :MAIN_COLS].astype(BF16), tail.astype(BF16)


def _q_weight(w_uq):
    scale = (QK_NOPE + QK_ROPE) ** -0.5 * LOG2_E
    w = (w_uq * scale).reshape(Q_LORA, N_HEADS, QK_NOPE + QK_ROPE)
    zp = jnp.zeros((Q_LORA, N_HEADS, HEAD_PAD - QK_NOPE - QK_ROPE), F32)
    return jnp.concatenate([w, zp], axis=2).reshape(Q_LORA, N_HEADS * HEAD_PAD).astype(BF16)


def _kv_weight(w_ukv):
    w = w_ukv.reshape(KV_LORA, N_HEADS, QK_NOPE + V_HEAD)
    k_nope, v = w[:, :, :QK_NOPE], w[:, :, QK_NOPE:]
    z = jnp.zeros_like(v)
    k_part = jnp.concatenate([k_nope, jnp.zeros_like(k_nope)], axis=2).reshape(KV_LORA, N_HEADS * HEAD_PAD)
    odd = (jnp.arange(N_HEADS) % 2 == 1)[None, :, None]
    v_part = jnp.concatenate([jnp.where(odd, z, v), jnp.where(odd, v, z)], axis=2).reshape(KV_LORA, N_HEADS * HEAD_PAD)
    return jnp.concatenate([k_part, v_part], axis=1).astype(BF16)


def _router_weight(w_rg, b_rg, w_re, b_re):
    pad = LANES - N_EXPERTS - N_GROUPS
    w = jnp.concatenate([w_re, w_rg, jnp.zeros((D_MODEL, pad), F32)], axis=1)
    b = jnp.concatenate([b_re, b_rg, jnp.zeros((pad,), F32)]).reshape(1, LANES)
    w_hi = w.astype(BF16)
    w_lo = (w - w_hi.astype(F32)).astype(BF16)
    return jnp.concatenate([w_hi, w_lo], axis=1), b


def kernel(x, p, positions, ln_mix_g, w_in, conv_a_w, w_out_a, q_norm_g, w_uq, kv_norm_g, w_ukv, w_out_b, conv_c_w, ln_c_g, ln_c_b, w_out_c, w_o, ln_ffn_g, w_route_grp, b_route_grp, w_route_exp, b_route_exp, w_exp_gate, w_exp_up, w_exp_down, ln_ple_g, w_ple_gate, w_ple, final_norm_g):
    c_tab, s_tab = _rope_tables(positions)
    xf = x.reshape(TOKENS, D_MODEL)
    pf = p.reshape(DEPTH, TOKENS, PLE_DIM)
    row = lambda a: a.reshape(1, -1)
    for i in range(DEPTH):
        proj, tail = _inproj(xf, row(ln_mix_g[i]), *_in_proj_weights(w_in[i]))
        q, k, v = _qkv(proj, tail, c_tab, s_tab, row(q_norm_g[i]), row(kv_norm_g[i]), _q_weight(w_uq[i]), _kv_weight(w_ukv[i]))
        att = _attention(q, k, v)
        xf = _mixer_tail(proj, tail, att, xf, conv_a_w[i], w_out_a[i].astype(BF16), conv_c_w[i], row(ln_c_g[i]),
                         row(ln_c_b[i]), w_out_c[i].astype(BF16), w_out_b[i].astype(BF16), w_o[i].astype(BF16))
        wr, br = _router_weight(w_route_grp[i], b_route_grp[i], w_route_exp[i], b_route_exp[i])
        meta, meta_t, counts, xs0 = _router(xf, row(ln_ffn_g[i]), wr, br)
        pos1, pos2, tile_expert, n_tiles = _dispatch_plan(meta_t, counts)
        xs = _dispatch(pos1, pos2, xf, row(ln_ffn_g[i]), xs0)
        ys = _experts(i, tile_expert, n_tiles, xs, w_exp_gate, w_exp_up, w_exp_down)
        xf = _combine_ple(i, pos1, pos2, xf, meta, ys, pf, row(ln_ple_g[i]), w_ple_gate[i].astype(BF16),
                          w_ple[i].astype(BF16), row(final_norm_g), final=(i == DEPTH - 1))
    return xf.reshape(BATCH, SEQ, D_MODEL)
```

```python
import functools

import jax
import jax.numpy as jnp
from jax import lax
from jax.experimental import pallas as pl
from jax.experimental.pallas import tpu as pltpu

D_MODEL = 1024
BATCH = 8
SEQ = 2048
DEPTH = 2
TOKENS = BATCH * SEQ
PLE_DIM = 256
SC_WIDTH = 512
SC_KERNEL = 3
N_HEADS = 8
QK_NOPE = 64
QK_ROPE = 32
V_HEAD = 64
Q_LORA = 768
KV_LORA = 256
ROPE_THETA = 10000.0
CONF_WIDTH = 512
CONF_KERNEL = 31
N_GROUPS = 4
EXPERTS_PER_GROUP = 8
N_EXPERTS = N_GROUPS * EXPERTS_PER_GROUP
EXPERT_HIDDEN = 256
EPS = 1e-6
LOG2_E = 1.4426950408889634

LANES = 128
HEAD_PAD = 128
F32 = jnp.float32
BF16 = jnp.bfloat16

COL_GATES = 0
COL_SC = 3 * D_MODEL
COL_QLAT = COL_SC + 3 * SC_WIDTH
COL_KVLAT = COL_QLAT + Q_LORA
MAIN_COLS = COL_KVLAT + KV_LORA
TCOL_GLU = 0
TCOL_KR = 2 * CONF_WIDTH
TCOL_KRSW = TCOL_KR + HEAD_PAD
TAIL_COLS = TCOL_KRSW + HEAD_PAD

ROUTER_GROUP_LANE = N_EXPERTS

MXU_TILE = 256
VMEM_MB = 48
VMEM_BIG_MB = 56

TM_INPROJ = 1024
TN_INPROJ = MAIN_COLS // 2
TN_SPLIT = 6 * MXU_TILE
TM_QKV = 512
T_ATTN = 512
HEADS_PER_STEP = 8
TM_MIX = 512
CONV_CHUNK = 64
HALO_C = 32
HALO_A = 16
TM_ROUTE = 512
TM_EXP = 512
EXPERT_PREFETCH = 3
EXPERT_SLOTS = EXPERT_PREFETCH + 1
TM_DISP = 512
TM_COMB = 256
TOP_K = 2
MAX_TILES = TOKENS * TOP_K // TM_EXP + N_EXPERTS
SLAB = D_MODEL // LANES
GATHER_UNROLL = 16
N_DMA_PRIORITIES = 2
META_E1, META_E2, META_R1, META_R2, META_W1, META_W2 = range(6)
META_ROWS = 8
SORTED_ROWS = MAX_TILES * TM_EXP * SLAB


def _cparams(semantics, vmem_mb):
    return pltpu.CompilerParams(dimension_semantics=semantics, vmem_limit_bytes=vmem_mb * 1024 * 1024)


def _sigmoid(x):
    return 1.0 / (1.0 + jnp.exp2(x * -LOG2_E))


def _rms(x, g):
    return x * lax.rsqrt(jnp.mean(x * x, axis=-1, keepdims=True) + EPS) * g


def _place(x, onehot):
    x1 = x.astype(BF16)
    r1 = x - x1.astype(F32)
    x2 = r1.astype(BF16)
    x3 = (r1 - x2.astype(F32)).astype(BF16)
    return (jnp.dot(x1, onehot, preferred_element_type=F32) + jnp.dot(x2, onehot, preferred_element_type=F32)
            + jnp.dot(x3, onehot, preferred_element_type=F32))


def _rope_kernel(pos_ref, freq_ref, c_ref, s_ref):
    half = QK_ROPE // 2
    per_row = LANES // half
    rows = TOKENS // per_row
    ang = pos_ref[...].astype(F32) * freq_ref[...]
    cos = jnp.cos(ang)
    sin = jnp.sin(ang)
    src = lax.broadcasted_iota(jnp.int32, (LANES, LANES), 0)
    dst = lax.broadcasted_iota(jnp.int32, (LANES, LANES), 1)
    lane = lax.broadcasted_iota(jnp.int32, (1, LANES), 1)
    ones_nope = jnp.where(lane < QK_NOPE, 1.0, 0.0)
    sign = jnp.where(lane < QK_NOPE + half, -1.0, 1.0)
    for j in range(per_row):
        f = src - half * j
        hit = (dst == QK_NOPE + f) | (dst == QK_NOPE + half + f)
        onehot = jnp.where((f >= 0) & (f < half) & hit, 1.0, 0.0).astype(BF16)
        c_ref[pl.ds(j, rows, stride=per_row), :] = _place(cos, onehot) + ones_nope
        s_ref[pl.ds(j, rows, stride=per_row), :] = _place(sin, onehot) * sign


def _rope_tables(positions):
    half = QK_ROPE // 2
    inv_freq = ROPE_THETA ** (-jnp.arange(0, QK_ROPE, 2, dtype=F32) / QK_ROPE)
    rows = TOKENS * half // LANES
    pos_rep = jnp.broadcast_to(positions.reshape(TOKENS, 1), (TOKENS, half)).reshape(rows, LANES)
    freq = jnp.tile(inv_freq, LANES // half).reshape(1, LANES)
    return pl.pallas_call(
        _rope_kernel,
        out_shape=(jax.ShapeDtypeStruct((TOKENS, HEAD_PAD), F32),) * 2,
        compiler_params=_cparams(None, VMEM_MB),
        name="rope_tables",
    )(pos_rep, freq)


def _inproj_kernel(x_ref, g_ref, wm_ref, wt_ref, om_ref, ot_ref, h_ref):
    j = pl.program_id(1)
    n_main = MAIN_COLS // TN_INPROJ

    @pl.when(j == 0)
    def _():
        h_ref[...] = _rms(x_ref[...], g_ref[...]).astype(BF16)

    @pl.when(j < n_main)
    def _():
        for lo, hi in ((0, TN_SPLIT), (TN_SPLIT, TN_INPROJ)):
            om_ref[:, lo:hi] = jnp.dot(h_ref[...], wm_ref[:, lo:hi], preferred_element_type=F32).astype(BF16)

    @pl.when(j == n_main)
    def _():
        ot_ref[...] = jnp.dot(h_ref[...], wt_ref[...], preferred_element_type=F32).astype(BF16)


def _inproj(x, g, w_main, w_tail):
    tm, tn = TM_INPROJ, TN_INPROJ
    n_main = MAIN_COLS // tn
    main_col = lambda i, j: jnp.minimum(j, n_main - 1)
    return pl.pallas_call(
        _inproj_kernel,
        grid=(TOKENS // tm, n_main + 1),
        in_specs=[
            pl.BlockSpec((tm, D_MODEL), lambda i, j: (i, 0)),
            pl.BlockSpec((1, D_MODEL), lambda i, j: (0, 0)),
            pl.BlockSpec((D_MODEL, tn), lambda i, j: (0, main_col(i, j))),
            pl.BlockSpec((D_MODEL, TAIL_COLS), lambda i, j: (0, 0)),
        ],
        out_specs=[
            pl.BlockSpec((tm, tn), lambda i, j: (i, main_col(i, j))),
            pl.BlockSpec((tm, TAIL_COLS), lambda i, j: (i, 0)),
        ],
        out_shape=(jax.ShapeDtypeStruct((TOKENS, MAIN_COLS), BF16), jax.ShapeDtypeStruct((TOKENS, TAIL_COLS), BF16)),
        scratch_shapes=[pltpu.VMEM((tm, D_MODEL), BF16)],
        compiler_params=_cparams(("parallel", "arbitrary"), VMEM_BIG_MB),
        name="in_proj",
    )(x, g, w_main, w_tail)


def _qkv_kernel(ql_ref, kvl_ref, kr_ref, krsw_ref, c_ref, s_ref, qg_ref, kvg_ref, wq_ref, wkv_ref,
                q_out, k_out, v_out):
    c = c_ref[...]
    s = s_ref[...]
    width = N_HEADS * HEAD_PAD
    half = QK_ROPE // 2
    low_half = lax.broadcasted_iota(jnp.int32, c.shape, 1) < QK_NOPE + half

    def swap_halves(x):
        return jnp.where(low_half, pltpu.roll(x, HEAD_PAD - half, axis=1), pltpu.roll(x, half, axis=1))

    qn = _rms(ql_ref[...].astype(F32), qg_ref[...]).astype(BF16)
    qq = jnp.dot(qn, wq_ref[...], preferred_element_type=F32)
    for h in range(N_HEADS):
        lo, hi = h * HEAD_PAD, (h + 1) * HEAD_PAD
        q_out[:, lo:hi] = (qq[:, lo:hi] * c + swap_halves(qq[:, lo:hi]) * s).astype(BF16)
    kvn = _rms(kvl_ref[...].astype(F32), kvg_ref[...]).astype(BF16)
    kk = jnp.dot(kvn, wkv_ref[...], preferred_element_type=F32)
    kr = kr_ref[...].astype(F32) * c + krsw_ref[...].astype(F32) * s
    for h in range(N_HEADS):
        lo, hi = h * HEAD_PAD, (h + 1) * HEAD_PAD
        k_out[:, lo:hi] = (kk[:, lo:hi] + kr).astype(BF16)
    v_out[...] = kk[:, width:].astype(BF16)


def _qkv(proj, tail, c_tab, s_tab, qg, kvg, wq, wkv):
    tm = TM_QKV
    width = N_HEADS * HEAD_PAD
    row = lambda blk: (lambda i: (i, blk))
    const = lambda i: (0, 0)
    return pl.pallas_call(
        _qkv_kernel,
        grid=(TOKENS // tm,),
        in_specs=[
            pl.BlockSpec((tm, Q_LORA), row(COL_QLAT // Q_LORA)),
            pl.BlockSpec((tm, KV_LORA), row(COL_KVLAT // KV_LORA)),
            pl.BlockSpec((tm, HEAD_PAD), row(TCOL_KR // HEAD_PAD)),
            pl.BlockSpec((tm, HEAD_PAD), row(TCOL_KRSW // HEAD_PAD)),
            pl.BlockSpec((tm, HEAD_PAD), row(0)),
            pl.BlockSpec((tm, HEAD_PAD), row(0)),
            pl.BlockSpec((1, Q_LORA), const),
            pl.BlockSpec((1, KV_LORA), const),
            pl.BlockSpec((Q_LORA, width), const),
            pl.BlockSpec((KV_LORA, 2 * width), const),
        ],
        out_specs=[pl.BlockSpec((tm, width), row(0))] * 3,
        out_shape=(jax.ShapeDtypeStruct((TOKENS, width), BF16),) * 3,
        compiler_params=_cparams(("parallel",), VMEM_MB),
        name="qkv_prep",
    )(proj, proj, tail, tail, c_tab, s_tab, qg, kvg, wq, wkv)


def _attn_kernel(q_ref, k_ref, v_ref, o_ref, *state):
    t = T_ATTN
    nh = HEADS_PER_STEP
    qi = pl.program_id(2)
    nt = (((1,), (1,)), ((), ()))
    m_sc, l_sc, acc_sc = state[0:nh], state[nh:2 * nh], state[2 * nh:3 * nh]
    for h in range(nh):
        m_sc[h][...] = jnp.full((t, LANES), -jnp.inf, F32)
        l_sc[h][...] = jnp.zeros((t, LANES), F32)
        acc_sc[h][...] = jnp.zeros((t, LANES), F32)

    def block(j, r0, nr, c0, nc, masked):
        start = pl.multiple_of(j * t + c0, nc)
        rows = slice(r0, r0 + nr)
        for h in range(nh):
            lo, hi = h * HEAD_PAD, (h + 1) * HEAD_PAD
            s = lax.dot_general(q_ref[rows, lo:hi], k_ref[pl.ds(start, nc), lo:hi], nt, preferred_element_type=F32)
            if masked:
                row_id = r0 + lax.broadcasted_iota(jnp.int32, (nr, nc), 0)
                col_id = c0 + lax.broadcasted_iota(jnp.int32, (nr, nc), 1)
                s = jnp.where(row_id >= col_id, s, -jnp.inf)
            blocks = [s[:, c * LANES:(c + 1) * LANES] for c in range(nc // LANES)]
            bmax = functools.reduce(jnp.maximum, blocks)
            m_old = m_sc[h][rows, :]
            m_new = jnp.maximum(m_old, jnp.max(bmax, axis=-1, keepdims=True))
            alpha = jnp.exp2(m_old - m_new)
            ps = [jnp.exp2(b - m_new) for b in blocks]
            p = jnp.concatenate(ps, axis=1).astype(BF16)
            l_sc[h][rows, :] = alpha * l_sc[h][rows, :] + functools.reduce(jnp.add, ps)
            acc_sc[h][rows, :] = alpha * acc_sc[h][rows, :] + jnp.dot(p, v_ref[pl.ds(start, nc), lo:hi],
                                                                       preferred_element_type=F32)
            m_sc[h][rows, :] = m_new

    def body(j, carry):
        block(j, 0, t, 0, t, False)
        return carry

    lax.fori_loop(0, qi, body, 0)
    half = t // 2
    block(qi, 0, t, 0, half, True)
    block(qi, half, half, half, half, True)
    out = [acc_sc[h][...] / jnp.sum(l_sc[h][...], axis=-1, keepdims=True) for h in range(nh)]
    o_ref[...] = jnp.concatenate([out[h] + out[h + 1] for h in range(0, nh, 2)], axis=1).astype(BF16)


def _attention(q, k, v):
    t = T_ATTN
    nq = SEQ // t
    nh = HEADS_PER_STEP
    return pl.pallas_call(
        _attn_kernel,
        grid=(BATCH, N_HEADS // nh, nq),
        in_specs=[
            pl.BlockSpec((t, nh * HEAD_PAD), lambda b, hg, i: (b * nq + i, hg)),
            pl.BlockSpec((SEQ, nh * HEAD_PAD), lambda b, hg, i: (b, hg)),
            pl.BlockSpec((SEQ, nh * HEAD_PAD), lambda b, hg, i: (b, hg)),
        ],
        out_specs=pl.BlockSpec((t, nh * V_HEAD), lambda b, hg, i: (b * nq + i, hg)),
        out_shape=jax.ShapeDtypeStruct((TOKENS, N_HEADS * V_HEAD), BF16),
        scratch_shapes=[pltpu.VMEM((t, LANES), F32)] * (3 * nh),
        compiler_params=_cparams(("parallel", "parallel", "arbitrary"), VMEM_MB),
        name="mla_attention",
    )(q, k, v)


def _mixer_tail_kernel(gates_ref, sc_ref, sch_ref, gv_ref, gg_ref, gvh_ref, ggh_ref, att_ref, x_ref,
                       cwa_ref, woa_ref, cwc_ref, lng_ref, lnb_ref, woc_ref, wob_ref, wo_ref,
                       out_ref, cbuf, ubuf, shifted, vbuf):
    tm = TM_MIX
    has_past = (pl.program_id(0) % (SEQ // tm)) != 0

    sc = sc_ref[...]
    sc_b = sc[:, 0:SC_WIDTH].astype(F32)
    cbuf[8:8 + tm, :] = sc[:, SC_WIDTH:2 * SC_WIDTH].astype(F32) * sc[:, 2 * SC_WIDTH:].astype(F32)
    sch = sch_ref[...].astype(F32)[HALO_A - 8:HALO_A, :]
    cbuf[0:8, :] = jnp.where(has_past, sch[:, SC_WIDTH:2 * SC_WIDTH] * sch[:, 2 * SC_WIDTH:], 0.0)
    conv_a = cwa_ref[0:1, :] * cbuf[6:6 + tm, :]
    for t in range(1, SC_KERNEL):
        conv_a = conv_a + cwa_ref[t:t + 1, :] * cbuf[6 + t:6 + t + tm, :]
    y_a = jnp.dot((sc_b * conv_a).astype(BF16), woa_ref[...], preferred_element_type=F32)

    ubuf[HALO_C:HALO_C + tm, :] = gv_ref[...].astype(F32) * _sigmoid(gg_ref[...].astype(F32))
    ubuf[0:HALO_C, :] = jnp.where(has_past, gvh_ref[...].astype(F32) * _sigmoid(ggh_ref[...].astype(F32)), 0.0)
    rows = tm + HALO_C - 8
    u_all = ubuf[...]
    for b in range(1, 8):
        shifted[b - 1, 0:rows, :] = pltpu.roll(u_all, tm + HALO_C - b, axis=0)[0:rows, :]
    base = HALO_C - (CONF_KERNEL - 1)
    for r0 in range(0, tm, CONV_CHUNK):
        acc = None
        for t in range(CONF_KERNEL):
            off = base + t
            a0 = r0 + off - off % 8
            src = ubuf[a0:a0 + CONV_CHUNK, :] if off % 8 == 0 else shifted[off % 8 - 1, a0:a0 + CONV_CHUNK, :]
            term = cwc_ref[t:t + 1, :] * src
            acc = term if acc is None else acc + term
        mu = jnp.mean(acc, axis=-1, keepdims=True)
        xc = acc - mu
        var = jnp.mean(xc * xc, axis=-1, keepdims=True)
        y = xc * lax.rsqrt(var + EPS) * lng_ref[...] + lnb_ref[...]
        vbuf[r0:r0 + CONV_CHUNK, :] = (y * _sigmoid(y)).astype(BF16)
    y_c = jnp.dot(vbuf[...], woc_ref[...], preferred_element_type=F32)

    y_b = jnp.dot(att_ref[...], wob_ref[...], preferred_element_type=F32)

    g = gates_ref[...]
    merged = (_sigmoid(g[:, 0:D_MODEL].astype(F32)) * y_a
              + _sigmoid(g[:, D_MODEL:2 * D_MODEL].astype(F32)) * y_b
              + _sigmoid(g[:, 2 * D_MODEL:].astype(F32)) * y_c)
    out_ref[...] = x_ref[...] + jnp.dot(merged.astype(BF16), wo_ref[...], preferred_element_type=F32)


def _mixer_tail(proj, tail, att, x, cwa, woa, cwc, lng, lnb, woc, wob, wo):
    tm = TM_MIX
    row = lambda width, col: pl.BlockSpec((tm, width), lambda i: (i, col // width))
    halo = lambda rows, width, col: pl.BlockSpec(
        (rows, width), lambda i: (jnp.maximum(i * (tm // rows) - 1, 0), col // width))
    const = lambda a: pl.BlockSpec(a.shape, lambda i: (0,) * a.ndim)
    weights = (cwa, woa, cwc, lng, lnb, woc, wob, wo)
    return pl.pallas_call(
        _mixer_tail_kernel,
        grid=(TOKENS // tm,),
        in_specs=[
            row(3 * D_MODEL, COL_GATES),
            row(3 * SC_WIDTH, COL_SC),
            halo(HALO_A, 3 * SC_WIDTH, COL_SC),
            row(CONF_WIDTH, TCOL_GLU),
            row(CONF_WIDTH, TCOL_GLU + CONF_WIDTH),
            halo(HALO_C, CONF_WIDTH, TCOL_GLU),
            halo(HALO_C, CONF_WIDTH, TCOL_GLU + CONF_WIDTH),
            pl.BlockSpec((tm, N_HEADS * V_HEAD), lambda i: (i, 0)),
            pl.BlockSpec((tm, D_MODEL), lambda i: (i, 0)),
        ] + [const(a) for a in weights],
        out_specs=pl.BlockSpec((tm, D_MODEL), lambda i: (i, 0)),
        out_shape=jax.ShapeDtypeStruct((TOKENS, D_MODEL), F32),
        scratch_shapes=[
            pltpu.VMEM((tm + 8, SC_WIDTH), F32),
            pltpu.VMEM((tm + HALO_C, CONF_WIDTH), F32),
            pltpu.VMEM((7, tm + HALO_C - 8, CONF_WIDTH), F32),
            pltpu.VMEM((tm, CONF_WIDTH), BF16),
        ],
        compiler_params=_cparams(("parallel",), VMEM_BIG_MB),
        name="mixer_tail",
    )(proj, proj, proj, tail, tail, tail, tail, att, x, *weights)


def _route(logits):
    lane = lax.broadcasted_iota(jnp.int32, logits.shape, 1)
    lane_f = lane.astype(F32)
    neg = -jnp.inf
    big = float(LANES)
    is_grp = (lane >= ROUTER_GROUP_LANE) & (lane < ROUTER_GROUP_LANE + N_GROUPS)
    glog = jnp.where(is_grp, logits, neg)
    gmax = jnp.max(glog, axis=-1, keepdims=True)
    gidx = jnp.min(jnp.where(glog == gmax, lane_f, big), axis=-1, keepdims=True)
    p_sel = 1.0 / jnp.sum(jnp.exp(glog - gmax), axis=-1, keepdims=True)
    first = (gidx - ROUTER_GROUP_LANE) * EXPERTS_PER_GROUP
    in_grp = (lane_f >= first) & (lane_f < first + EXPERTS_PER_GROUP)
    el = jnp.where(in_grp, logits, neg)
    m1 = jnp.max(el, axis=-1, keepdims=True)
    i1 = jnp.min(jnp.where(el == m1, lane_f, big), axis=-1, keepdims=True)
    el2 = jnp.where(lane_f == i1, neg, el)
    m2 = jnp.max(el2, axis=-1, keepdims=True)
    i2 = jnp.min(jnp.where(el2 == m2, lane_f, big), axis=-1, keepdims=True)
    e2 = jnp.exp(m2 - m1)
    w1 = p_sel / (1.0 + e2)
    w2 = w1 * e2
    return i1, i2, w1, w2


def _router_kernel(x_ref, g_ref, wr_ref, br_ref, meta_ref, meta_t_ref, cnt_ref, xs0_ref, run_ref):
    tm = TM_ROUTE

    @pl.when(pl.program_id(0) == 0)
    def _():
        run_ref[...] = jnp.zeros_like(run_ref)

    h = _rms(x_ref[...], g_ref[...])
    h_hi = h.astype(BF16)
    h_lo = (h - h_hi.astype(F32)).astype(BF16)
    hi_terms = jnp.dot(h_hi, wr_ref[...], preferred_element_type=F32)
    logits = (hi_terms[:, :LANES] + hi_terms[:, LANES:]
              + jnp.dot(h_lo, wr_ref[:, :LANES], preferred_element_type=F32) + br_ref[...])
    i1, i2, w1, w2 = _route(logits)
    lane = lax.broadcasted_iota(jnp.int32, (tm, LANES), 1)
    lane_f = lane.astype(F32)
    oh1 = lane_f == i1
    oh2 = lane_f == i2
    onehot = jnp.where(oh1, 1.0, 0.0) + jnp.where(oh2, 1.0, 0.0)
    row_id = lax.broadcasted_iota(jnp.int32, (tm, tm), 0)
    col_id = lax.broadcasted_iota(jnp.int32, (tm, tm), 1)
    below = jnp.where(row_id > col_id, 1.0, 0.0).astype(BF16)
    before = run_ref[...] + jnp.dot(below, onehot.astype(BF16), preferred_element_type=F32)
    r1 = jnp.sum(jnp.where(oh1, before, 0.0), axis=-1, keepdims=True)
    r2 = jnp.sum(jnp.where(oh2, before, 0.0), axis=-1, keepdims=True)
    run_ref[...] += jnp.sum(onehot, axis=0, keepdims=True)
    cnt_ref[...] = run_ref[...]
    meta = jnp.zeros((tm, LANES), F32)
    for col, val in enumerate((i1, i2, r1, r2, w1, w2)):
        meta = jnp.where(lane == col, val, meta)
    meta_ref[...] = meta
    meta_t_ref[...] = meta.T[0:META_ROWS, :]
    xs0_ref[...] = jnp.zeros_like(xs0_ref)


def _router(x, g, wr, br):
    tm = TM_ROUTE
    const = lambda i: (0, 0)
    return pl.pallas_call(
        _router_kernel,
        grid=(TOKENS // tm,),
        in_specs=[
            pl.BlockSpec((tm, D_MODEL), lambda i: (i, 0)),
            pl.BlockSpec((1, D_MODEL), const),
            pl.BlockSpec((D_MODEL, 2 * LANES), const),
            pl.BlockSpec((1, LANES), const),
        ],
        out_specs=[
            pl.BlockSpec((tm, LANES), lambda i: (i, 0)),
            pl.BlockSpec((META_ROWS, tm), lambda i: (0, i)),
            pl.BlockSpec((1, LANES), const),
            pl.BlockSpec((SORTED_ROWS // (TOKENS // tm), LANES), lambda i: (i, 0)),
        ],
        out_shape=(
            jax.ShapeDtypeStruct((TOKENS, LANES), F32),
            jax.ShapeDtypeStruct((META_ROWS, TOKENS), F32),
            jax.ShapeDtypeStruct((1, LANES), F32),
            jax.ShapeDtypeStruct((SORTED_ROWS, LANES), F32),
        ),
        scratch_shapes=[pltpu.VMEM((1, LANES), F32)],
        compiler_params=_cparams(("arbitrary",), VMEM_MB),
        name="moe_router",
    )(x, g, wr, br)


def _dispatch_plan(meta_t, counts):
    e1 = meta_t[META_E1].astype(jnp.int32)
    e2 = meta_t[META_E2].astype(jnp.int32)
    r1 = meta_t[META_R1].astype(jnp.int32)
    r2 = meta_t[META_R2].astype(jnp.int32)
    cnt = counts[0, :N_EXPERTS].astype(jnp.int32)
    tiles = (cnt + TM_EXP - 1) // TM_EXP
    tile_end = jnp.cumsum(tiles)
    first_slot = ((tile_end - tiles) * TM_EXP)[:, None]
    expert = jnp.arange(N_EXPERTS, dtype=jnp.int32)[:, None]
    pos1 = jnp.sum(jnp.where(e1[None, :] == expert, first_slot, 0), axis=0) + r1
    pos2 = jnp.sum(jnp.where(e2[None, :] == expert, first_slot, 0), axis=0) + r2
    n_tiles = tile_end[-1:]
    tile_id = jnp.minimum(jnp.arange(MAX_TILES, dtype=jnp.int32), n_tiles - 1)
    tile_expert = jnp.sum((tile_id[:, None] >= tile_end[None, :]).astype(jnp.int32), axis=1)
    return pos1, pos2, tile_expert, n_tiles


def _dispatch_kernel(pos1_ref, pos2_ref, x_ref, g_ref, xs_in_ref, xs_ref, slab, sem):
    del xs_in_ref
    tm = TM_DISP
    i = pl.program_id(0)
    last = pl.num_programs(0) - 1
    slot = i % 2

    def wait_copies(sl):
        for _ in range(TOP_K):
            pltpu.make_async_copy(slab.at[sl], xs_ref.at[pl.ds(0, tm * SLAB), :], sem.at[sl]).wait()

    @pl.when(i >= 2)
    def _():
        wait_copies(slot)

    h = _rms(x_ref[...], g_ref[...])
    for s in range(SLAB):
        slab[slot, pl.ds(s, tm, stride=SLAB), :] = h[:, s * LANES:(s + 1) * LANES]

    def body(c, carry):
        for u in range(GATHER_UNROLL):
            r = c * GATHER_UNROLL + u
            src = slab.at[slot, pl.ds(pl.multiple_of(r * SLAB, SLAB), SLAB), :]
            for k, pos_ref in enumerate((pos1_ref, pos2_ref)):
                p = pos_ref[i * tm + r]
                pltpu.make_async_copy(src, xs_ref.at[pl.ds(pl.multiple_of(p * SLAB, SLAB), SLAB), :],
                                      sem.at[slot]).start(priority=k % N_DMA_PRIORITIES)
        return carry

    lax.fori_loop(0, tm // GATHER_UNROLL, body, 0)

    @pl.when(i == last)
    def _():
        wait_copies(1 - slot)
        wait_copies(slot)


def _dispatch(pos1, pos2, x, g, xs0):
    tm = TM_DISP
    return pl.pallas_call(
        _dispatch_kernel,
        grid_spec=pltpu.PrefetchScalarGridSpec(
            num_scalar_prefetch=2,
            grid=(TOKENS // tm,),
            in_specs=[
                pl.BlockSpec((tm, D_MODEL), lambda i, p1, p2: (i, 0)),
                pl.BlockSpec((1, D_MODEL), lambda i, p1, p2: (0, 0)),
                pl.BlockSpec(memory_space=pl.ANY),
            ],
            out_specs=pl.BlockSpec(memory_space=pl.ANY),
            scratch_shapes=[pltpu.VMEM((2, tm * SLAB, LANES), F32), pltpu.SemaphoreType.DMA((2,))],
        ),
        out_shape=jax.ShapeDtypeStruct((SORTED_ROWS, LANES), F32),
        input_output_aliases={4: 0},
        compiler_params=_cparams(("arbitrary",), VMEM_MB),
        name="moe_dispatch",
    )(pos1, pos2, x, g, xs0)


def _expert_kernel(te_ref, nt_ref, xs_ref, wg_ref, wu_ref, wd_ref, ys_ref, xbuf, sem):
    del te_ref
    tm = TM_EXP
    i = pl.program_id(0)
    n = nt_ref[0]

    def tile_copy(tile):
        slot = tile % EXPERT_SLOTS
        rows = pl.ds(pl.multiple_of(tile * (tm * SLAB), tm * SLAB), tm * SLAB)
        return pltpu.make_async_copy(xs_ref.at[rows, :], xbuf.at[slot], sem.at[slot])

    @pl.when(i == 0)
    def _():
        for ahead in range(EXPERT_PREFETCH):
            @pl.when(ahead < n)
            def _():
                tile_copy(ahead).start()

    @pl.when(i + EXPERT_PREFETCH < n)
    def _():
        tile_copy(i + EXPERT_PREFETCH).start()

    @pl.when(i < n)
    def _():
        tile_copy(i).wait()
        slot = i % EXPERT_SLOTS
        xt = jnp.concatenate([xbuf[slot, pl.ds(s, tm, stride=SLAB), :].astype(BF16) for s in range(SLAB)], axis=1)
        hg = jnp.dot(xt, wg_ref[...].astype(BF16), preferred_element_type=F32)
        hu = jnp.dot(xt, wu_ref[...].astype(BF16), preferred_element_type=F32)
        hh = (hg * _sigmoid(hg) * hu).astype(BF16)
        y = jnp.dot(hh, wd_ref[...].astype(BF16), preferred_element_type=F32)
        for s in range(SLAB):
            ys_ref[pl.ds(s, tm, stride=SLAB), :] = y[:, s * LANES:(s + 1) * LANES]

    @pl.when(i >= n)
    def _():
        ys_ref[...] = jnp.zeros_like(ys_ref)


def _experts(layer, tile_expert, n_tiles, xs, w_gate, w_up, w_down):
    tm = TM_EXP
    wspec = lambda rows, cols: pl.BlockSpec((None, None, rows, cols), lambda i, te, nt: (layer, te[i], 0, 0))
    return pl.pallas_call(
        _expert_kernel,
        grid_spec=pltpu.PrefetchScalarGridSpec(
            num_scalar_prefetch=2,
            grid=(MAX_TILES,),
            in_specs=[
                pl.BlockSpec(memory_space=pl.ANY),
                wspec(D_MODEL, EXPERT_HIDDEN),
                wspec(D_MODEL, EXPERT_HIDDEN),
                wspec(EXPERT_HIDDEN, D_MODEL),
            ],
            out_specs=pl.BlockSpec((tm * SLAB, LANES), lambda i, te, nt: (i, 0)),
            scratch_shapes=[pltpu.VMEM((EXPERT_SLOTS, tm * SLAB, LANES), F32), pltpu.SemaphoreType.DMA((EXPERT_SLOTS,))],
        ),
        out_shape=jax.ShapeDtypeStruct((SORTED_ROWS, LANES), F32),
        compiler_params=_cparams(("arbitrary",), VMEM_MB),
        name="moe_experts",
    )(tile_expert, n_tiles, xs, w_gate, w_up, w_down)


def _start_slab_gathers(idx_refs, base, n_rows, src_hbm, dst_bufs, sem):
    def body(c, carry):
        for u in range(GATHER_UNROLL):
            r = c * GATHER_UNROLL + u
            for k, (idx_ref, dst) in enumerate(zip(idx_refs, dst_bufs)):
                t = idx_ref[base + r]
                pltpu.make_async_copy(src_hbm.at[pl.ds(pl.multiple_of(t * SLAB, SLAB), SLAB), :],
                                      dst.at[pl.ds(pl.multiple_of(r * SLAB, SLAB), SLAB), :],
                                      sem).start(priority=k % N_DMA_PRIORITIES)
        return carry

    lax.fori_loop(0, n_rows // GATHER_UNROLL, body, 0)


def _wait_slab_gathers(n_rows, src_hbm, dst, sem):
    pltpu.make_async_copy(src_hbm.at[pl.ds(0, n_rows * SLAB), :], dst, sem).wait()


def _combine_ple_kernel(pos1_ref, pos2_ref, x_ref, meta_ref, ys_ref, p_ref, g_ref, wg_ref, wp_ref, fg_ref,
                        out_ref, cbuf, sem, *, final):
    tm = TM_COMB
    i = pl.program_id(0)

    def gather(tile, slot):
        _start_slab_gathers((pos1_ref, pos2_ref), tile * tm, tm, ys_ref, (cbuf.at[slot, 0], cbuf.at[slot, 1]),
                            sem.at[slot])

    @pl.when(i == 0)
    def _():
        gather(0, 0)

    @pl.when(i + 1 < pl.num_programs(0))
    def _():
        gather(i + 1, (i + 1) % 2)

    slot = i % 2
    for k in range(2):
        _wait_slab_gathers(tm, ys_ref, cbuf.at[slot, k], sem.at[slot])
    meta = meta_ref[...]
    w1 = meta[:, META_W1:META_W1 + 1]
    w2 = meta[:, META_W2:META_W2 + 1]
    moe = jnp.concatenate([w1 * cbuf[slot, 0, pl.ds(s, tm, stride=SLAB), :]
                           + w2 * cbuf[slot, 1, pl.ds(s, tm, stride=SLAB), :] for s in range(SLAB)], axis=1)
    x = x_ref[...] + moe
    h = _rms(x, g_ref[...]).astype(BF16)
    gate = _sigmoid(jnp.dot(h, wg_ref[...], preferred_element_type=F32))
    emb = jnp.dot(p_ref[...].astype(BF16), wp_ref[...], preferred_element_type=F32)
    y = x + gate * emb
    if final:
        y = _rms(y, fg_ref[...])
    out_ref[...] = y


def _combine_ple(layer, pos1, pos2, x, meta, ys, p, g, wg, wp, fg, final):
    tm = TM_COMB
    const = lambda i, p1, p2: (0, 0)
    rows = lambda i, p1, p2: (i, 0)
    return pl.pallas_call(
        functools.partial(_combine_ple_kernel, final=final),
        grid_spec=pltpu.PrefetchScalarGridSpec(
            num_scalar_prefetch=2,
            grid=(TOKENS // tm,),
            in_specs=[
                pl.BlockSpec((tm, D_MODEL), rows),
                pl.BlockSpec((tm, LANES), rows),
                pl.BlockSpec(memory_space=pl.ANY),
                pl.BlockSpec((None, tm, PLE_DIM), lambda i, p1, p2: (layer, i, 0)),
                pl.BlockSpec((1, D_MODEL), const),
                pl.BlockSpec((D_MODEL, D_MODEL), const),
                pl.BlockSpec((PLE_DIM, D_MODEL), const),
                pl.BlockSpec((1, D_MODEL), const),
            ],
            out_specs=pl.BlockSpec((tm, D_MODEL), rows),
            scratch_shapes=[pltpu.VMEM((2, 2, tm * SLAB, LANES), F32), pltpu.SemaphoreType.DMA((2,))],
        ),
        out_shape=jax.ShapeDtypeStruct((TOKENS, D_MODEL), F32),
        compiler_params=_cparams(("arbitrary",), VMEM_MB),
        name="moe_combine_ple",
    )(pos1, pos2, x, meta, ys, p, g, wg, wp, fg)


def _in_proj_weights(w_in):
    k_rope = w_in[:, MAIN_COLS:MAIN_COLS + QK_ROPE]
    glu = w_in[:, MAIN_COLS + QK_ROPE:]
    half = QK_ROPE // 2
    zn = jnp.zeros((D_MODEL, QK_NOPE), F32)
    zp = jnp.zeros((D_MODEL, HEAD_PAD - QK_NOPE - QK_ROPE), F32)
    tail = jnp.concatenate([glu, zn, k_rope, zp, zn, k_rope[:, half:], k_rope[:, :half], zp], axis=1)
    return w_in[:, :MAIN_COLS].astype(BF16), tail.astype(BF16)


def _q_weight(w_uq):
    scale = (QK_NOPE + QK_ROPE) ** -0.5 * LOG2_E
    w = (w_uq * scale).reshape(Q_LORA, N_HEADS, QK_NOPE + QK_ROPE)
    zp = jnp.zeros((Q_LORA, N_HEADS, HEAD_PAD - QK_NOPE - QK_ROPE), F32)
    return jnp.concatenate([w, zp], axis=2).reshape(Q_LORA, N_HEADS * HEAD_PAD).astype(BF16)


def _kv_weight(w_ukv):
    w = w_ukv.reshape(KV_LORA, N_HEADS, QK_NOPE + V_HEAD)
    k_nope, v = w[:, :, :QK_NOPE], w[:, :, QK_NOPE:]
    z = jnp.zeros_like(v)
    k_part = jnp.concatenate([k_nope, jnp.zeros_like(k_nope)], axis=2).reshape(KV_LORA, N_HEADS * HEAD_PAD)
    odd = (jnp.arange(N_HEADS) % 2 == 1)[None, :, None]
    v_part = jnp.concatenate([jnp.where(odd, z, v), jnp.where(odd, v, z)], axis=2).reshape(KV_LORA, N_HEADS * HEAD_PAD)
    return jnp.concatenate([k_part, v_part], axis=1).astype(BF16)


def _router_weight(w_rg, b_rg, w_re, b_re):
    pad = LANES - N_EXPERTS - N_GROUPS
    w = jnp.concatenate([w_re, w_rg, jnp.zeros((D_MODEL, pad), F32)], axis=1)
    b = jnp.concatenate([b_re, b_rg, jnp.zeros((pad,), F32)]).reshape(1, LANES)
    w_hi = w.astype(BF16)
    w_lo = (w - w_hi.astype(F32)).astype(BF16)
    return jnp.concatenate([w_hi, w_lo], axis=1), b


def kernel(x, p, positions, ln_mix_g, w_in, conv_a_w, w_out_a, q_norm_g, w_uq, kv_norm_g, w_ukv, w_out_b, conv_c_w, ln_c_g, ln_c_b, w_out_c, w_o, ln_ffn_g, w_route_grp, b_route_grp, w_route_exp, b_route_exp, w_exp_gate, w_exp_up, w_exp_down, ln_ple_g, w_ple_gate, w_ple, final_norm_g):
    c_tab, s_tab = _rope_tables(positions)
    xf = x.reshape(TOKENS, D_MODEL)
    pf = p.reshape(DEPTH, TOKENS, PLE_DIM)
    row = lambda a: a.reshape(1, -1)
    for i in range(DEPTH):
        proj, tail = _inproj(xf, row(ln_mix_g[i]), *_in_proj_weights(w_in[i]))
        q, k, v = _qkv(proj, tail, c_tab, s_tab, row(q_norm_g[i]), row(kv_norm_g[i]), _q_weight(w_uq[i]), _kv_weight(w_ukv[i]))
        att = _attention(q, k, v)
        xf = _mixer_tail(proj, tail, att, xf, conv_a_w[i], w_out_a[i].astype(BF16), conv_c_w[i], row(ln_c_g[i]),
                         row(ln_c_b[i]), w_out_c[i].astype(BF16), w_out_b[i].astype(BF16), w_o[i].astype(BF16))
        wr, br = _router_weight(w_route_grp[i], b_route_grp[i], w_route_exp[i], b_route_exp[i])
        meta, meta_t, counts, xs0 = _router(xf, row(ln_ffn_g[i]), wr, br)
        pos1, pos2, tile_expert, n_tiles = _dispatch_plan(meta_t, counts)
        xs = _dispatch(pos1, pos2, xf, row(ln_ffn_g[i]), xs0)
        ys = _experts(i, tile_expert, n_tiles, xs, w_exp_gate, w_exp_up, w_exp_down)
        xf = _combine_ple(i, pos1, pos2, xf, meta, ys, pf, row(ln_ple_g[i]), w_ple_gate[i].astype(BF16),
                          w_ple[i].astype(BF16), row(final_norm_g), final=(i == DEPTH - 1))
    return xf.reshape(BATCH, SEQ, D_MODEL)
```

```python
import functools

import jax
import jax.numpy as jnp
from jax import lax
from jax.experimental import pallas as pl
from jax.experimental.pallas import tpu as pltpu

D_MODEL = 1024
BATCH = 8
SEQ = 2048
DEPTH = 2
TOKENS = BATCH * SEQ
PLE_DIM = 256
SC_WIDTH = 512
SC_KERNEL = 3
N_HEADS = 8
QK_NOPE = 64
QK_ROPE = 32
V_HEAD = 64
Q_LORA = 768
KV_LORA = 256
ROPE_THETA = 10000.0
CONF_WIDTH = 512
CONF_KERNEL = 31
N_GROUPS = 4
EXPERTS_PER_GROUP = 8
N_EXPERTS = N_GROUPS * EXPERTS_PER_GROUP
EXPERT_HIDDEN = 256
EPS = 1e-6
LOG2_E = 1.4426950408889634

LANES = 128
HEAD_PAD = 128
F32 = jnp.float32
BF16 = jnp.bfloat16

COL_GATES = 0
COL_SC = 3 * D_MODEL
COL_QLAT = COL_SC + 3 * SC_WIDTH
COL_KVLAT = COL_QLAT + Q_LORA
MAIN_COLS = COL_KVLAT + KV_LORA
TCOL_GLU = 0
TCOL_KR = 2 * CONF_WIDTH
TCOL_KRSW = TCOL_KR + HEAD_PAD
TAIL_COLS = TCOL_KRSW + HEAD_PAD

ROUTER_GROUP_LANE = N_EXPERTS

MXU_TILE = 256
VMEM_MB = 48
VMEM_BIG_MB = 56

TM_INPROJ = 1024
TN_INPROJ = MAIN_COLS // 2
TN_SPLIT = 6 * MXU_TILE
TM_QKV = 512
T_ATTN = 512
HEADS_PER_STEP = 8
TM_MIX = 512
CONV_CHUNK = 64
HALO_C = 32
HALO_A = 16
TM_EXP = 512
EXPERT_PREFETCH = 2
EXPERT_SLOTS = EXPERT_PREFETCH + 1
TM_DISP = 512
TM_COMB = 256
TOP_K = 2
MAX_TILES = TOKENS * TOP_K // TM_EXP + N_EXPERTS
SLAB = D_MODEL // LANES
GATHER_UNROLL = 16
N_DMA_PRIORITIES = 2
META_E1, META_E2, META_R1, META_R2, META_W1, META_W2 = range(6)
META_ROWS = 8
SORTED_ROWS = MAX_TILES * TM_EXP * SLAB
ZERO_COPIES = 6
ZERO_ROWS = SORTED_ROWS // (TOKENS // TM_MIX) // ZERO_COPIES


def _cparams(semantics, vmem_mb):
    return pltpu.CompilerParams(dimension_semantics=semantics, vmem_limit_bytes=vmem_mb * 1024 * 1024)


def _sigmoid(x):
    return 1.0 / (1.0 + jnp.exp2(x * -LOG2_E))


def _rms(x, g):
    return x * lax.rsqrt(jnp.mean(x * x, axis=-1, keepdims=True) + EPS) * g


def _place(x, onehot):
    x1 = x.astype(BF16)
    r1 = x - x1.astype(F32)
    x2 = r1.astype(BF16)
    x3 = (r1 - x2.astype(F32)).astype(BF16)
    return (jnp.dot(x1, onehot, preferred_element_type=F32) + jnp.dot(x2, onehot, preferred_element_type=F32)
            + jnp.dot(x3, onehot, preferred_element_type=F32))


def _rope_kernel(pos_ref, freq_ref, c_ref, s_ref):
    half = QK_ROPE // 2
    per_row = LANES // half
    rows = TOKENS // per_row
    ang = pos_ref[...].astype(F32) * freq_ref[...]
    cos = jnp.cos(ang)
    sin = jnp.sin(ang)
    src = lax.broadcasted_iota(jnp.int32, (LANES, LANES), 0)
    dst = lax.broadcasted_iota(jnp.int32, (LANES, LANES), 1)
    lane = lax.broadcasted_iota(jnp.int32, (1, LANES), 1)
    ones_nope = jnp.where(lane < QK_NOPE, 1.0, 0.0)
    sign = jnp.where(lane < QK_NOPE + half, -1.0, 1.0)
    for j in range(per_row):
        f = src - half * j
        hit = (dst == QK_NOPE + f) | (dst == QK_NOPE + half + f)
        onehot = jnp.where((f >= 0) & (f < half) & hit, 1.0, 0.0).astype(BF16)
        c_ref[pl.ds(j, rows, stride=per_row), :] = _place(cos, onehot) + ones_nope
        s_ref[pl.ds(j, rows, stride=per_row), :] = _place(sin, onehot) * sign


def _rope_tables(positions):
    half = QK_ROPE // 2
    inv_freq = ROPE_THETA ** (-jnp.arange(0, QK_ROPE, 2, dtype=F32) / QK_ROPE)
    rows = TOKENS * half // LANES
    pos_rep = jnp.broadcast_to(positions.reshape(TOKENS, 1), (TOKENS, half)).reshape(rows, LANES)
    freq = jnp.tile(inv_freq, LANES // half).reshape(1, LANES)
    return pl.pallas_call(
        _rope_kernel,
        out_shape=(jax.ShapeDtypeStruct((TOKENS, HEAD_PAD), F32),) * 2,
        compiler_params=_cparams(None, VMEM_MB),
        name="rope_tables",
    )(pos_rep, freq)


def _inproj_kernel(x_ref, g_ref, wm_ref, wt_ref, om_ref, ot_ref, h_ref):
    j = pl.program_id(1)
    n_main = MAIN_COLS // TN_INPROJ

    @pl.when(j == 0)
    def _():
        h_ref[...] = _rms(x_ref[...], g_ref[...]).astype(BF16)

    @pl.when(j < n_main)
    def _():
        for lo, hi in ((0, TN_SPLIT), (TN_SPLIT, TN_INPROJ)):
            om_ref[:, lo:hi] = jnp.dot(h_ref[...], wm_ref[:, lo:hi], preferred_element_type=F32).astype(BF16)

    @pl.when(j == n_main)
    def _():
        ot_ref[...] = jnp.dot(h_ref[...], wt_ref[...], preferred_element_type=F32).astype(BF16)


def _inproj(x, g, w_main, w_tail):
    tm, tn = TM_INPROJ, TN_INPROJ
    n_main = MAIN_COLS // tn
    main_col = lambda i, j: jnp.minimum(j, n_main - 1)
    return pl.pallas_call(
        _inproj_kernel,
        grid=(TOKENS // tm, n_main + 1),
        in_specs=[
            pl.BlockSpec((tm, D_MODEL), lambda i, j: (i, 0)),
            pl.BlockSpec((1, D_MODEL), lambda i, j: (0, 0)),
            pl.BlockSpec((D_MODEL, tn), lambda i, j: (0, main_col(i, j))),
            pl.BlockSpec((D_MODEL, TAIL_COLS), lambda i, j: (0, 0)),
        ],
        out_specs=[
            pl.BlockSpec((tm, tn), lambda i, j: (i, main_col(i, j))),
            pl.BlockSpec((tm, TAIL_COLS), lambda i, j: (i, 0)),
        ],
        out_shape=(jax.ShapeDtypeStruct((TOKENS, MAIN_COLS), BF16), jax.ShapeDtypeStruct((TOKENS, TAIL_COLS), BF16)),
        scratch_shapes=[pltpu.VMEM((tm, D_MODEL), BF16)],
        compiler_params=_cparams(("parallel", "arbitrary"), VMEM_BIG_MB),
        name="in_proj",
    )(x, g, w_main, w_tail)


def _qkv_kernel(ql_ref, kvl_ref, kr_ref, krsw_ref, c_ref, s_ref, qg_ref, kvg_ref, wq_ref, wkv_ref,
                q_out, k_out, v_out):
    c = c_ref[...]
    s = s_ref[...]
    width = N_HEADS * HEAD_PAD
    half = QK_ROPE // 2
    low_half = lax.broadcasted_iota(jnp.int32, c.shape, 1) < QK_NOPE + half

    def swap_halves(x):
        return jnp.where(low_half, pltpu.roll(x, HEAD_PAD - half, axis=1), pltpu.roll(x, half, axis=1))

    qn = _rms(ql_ref[...].astype(F32), qg_ref[...]).astype(BF16)
    qq = jnp.dot(qn, wq_ref[...], preferred_element_type=F32)
    for h in range(N_HEADS):
        lo, hi = h * HEAD_PAD, (h + 1) * HEAD_PAD
        q_out[:, lo:hi] = (qq[:, lo:hi] * c + swap_halves(qq[:, lo:hi]) * s).astype(BF16)
    kvn = _rms(kvl_ref[...].astype(F32), kvg_ref[...]).astype(BF16)
    kk = jnp.dot(kvn, wkv_ref[...], preferred_element_type=F32)
    kr = kr_ref[...].astype(F32) * c + krsw_ref[...].astype(F32) * s
    for h in range(N_HEADS):
        lo, hi = h * HEAD_PAD, (h + 1) * HEAD_PAD
        k_out[:, lo:hi] = (kk[:, lo:hi] + kr).astype(BF16)
    v_out[...] = kk[:, width:].astype(BF16)


def _qkv(proj, tail, c_tab, s_tab, qg, kvg, wq, wkv):
    tm = TM_QKV
    width = N_HEADS * HEAD_PAD
    row = lambda blk: (lambda i: (i, blk))
    const = lambda i: (0, 0)
    return pl.pallas_call(
        _qkv_kernel,
        grid=(TOKENS // tm,),
        in_specs=[
            pl.BlockSpec((tm, Q_LORA), row(COL_QLAT // Q_LORA)),
            pl.BlockSpec((tm, KV_LORA), row(COL_KVLAT // KV_LORA)),
            pl.BlockSpec((tm, HEAD_PAD), row(TCOL_KR // HEAD_PAD)),
            pl.BlockSpec((tm, HEAD_PAD), row(TCOL_KRSW // HEAD_PAD)),
            pl.BlockSpec((tm, HEAD_PAD), row(0)),
            pl.BlockSpec((tm, HEAD_PAD), row(0)),
            pl.BlockSpec((1, Q_LORA), const),
            pl.BlockSpec((1, KV_LORA), const),
            pl.BlockSpec((Q_LORA, width), const),
            pl.BlockSpec((KV_LORA, 2 * width), const),
        ],
        out_specs=[pl.BlockSpec((tm, width), row(0))] * 3,
        out_shape=(jax.ShapeDtypeStruct((TOKENS, width), BF16),) * 3,
        compiler_params=_cparams(("parallel",), VMEM_MB),
        name="qkv_prep",
    )(proj, proj, tail, tail, c_tab, s_tab, qg, kvg, wq, wkv)


def _attn_kernel(q_ref, k_ref, v_ref, o_ref, *state):
    t = T_ATTN
    nh = HEADS_PER_STEP
    qi = pl.program_id(2)
    nt = (((1,), (1,)), ((), ()))
    m_sc, l_sc, acc_sc = state[0:nh], state[nh:2 * nh], state[2 * nh:3 * nh]
    for h in range(nh):
        m_sc[h][...] = jnp.full((t, LANES), -jnp.inf, F32)
        l_sc[h][...] = jnp.zeros((t, LANES), F32)
        acc_sc[h][...] = jnp.zeros((t, LANES), F32)

    def block(j, r0, nr, c0, nc, masked):
        start = pl.multiple_of(j * t + c0, nc)
        rows = slice(r0, r0 + nr)
        for h in range(nh):
            lo, hi = h * HEAD_PAD, (h + 1) * HEAD_PAD
            s = lax.dot_general(q_ref[rows, lo:hi], k_ref[pl.ds(start, nc), lo:hi], nt, preferred_element_type=F32)
            if masked:
                row_id = r0 + lax.broadcasted_iota(jnp.int32, (nr, nc), 0)
                col_id = c0 + lax.broadcasted_iota(jnp.int32, (nr, nc), 1)
                s = jnp.where(row_id >= col_id, s, -jnp.inf)
            blocks = [s[:, c * LANES:(c + 1) * LANES] for c in range(nc // LANES)]
            bmax = functools.reduce(jnp.maximum, blocks)
            m_old = m_sc[h][rows, :]
            m_new = jnp.maximum(m_old, jnp.max(bmax, axis=-1, keepdims=True))
            alpha = jnp.exp2(m_old - m_new)
            ps = [jnp.exp2(b - m_new) for b in blocks]
            p = jnp.concatenate(ps, axis=1).astype(BF16)
            l_sc[h][rows, :] = alpha * l_sc[h][rows, :] + functools.reduce(jnp.add, ps)
            acc_sc[h][rows, :] = alpha * acc_sc[h][rows, :] + jnp.dot(p, v_ref[pl.ds(start, nc), lo:hi],
                                                                       preferred_element_type=F32)
            m_sc[h][rows, :] = m_new

    def body(j, carry):
        block(j, 0, t, 0, t, False)
        return carry

    lax.fori_loop(0, qi, body, 0)
    half = t // 2
    block(qi, 0, t, 0, half, True)
    block(qi, half, half, half, half, True)
    out = [acc_sc[h][...] / jnp.sum(l_sc[h][...], axis=-1, keepdims=True) for h in range(nh)]
    o_ref[...] = jnp.concatenate([out[h] + out[h + 1] for h in range(0, nh, 2)], axis=1).astype(BF16)


def _attention(q, k, v):
    t = T_ATTN
    nq = SEQ // t
    nh = HEADS_PER_STEP
    return pl.pallas_call(
        _attn_kernel,
        grid=(BATCH, N_HEADS // nh, nq),
        in_specs=[
            pl.BlockSpec((t, nh * HEAD_PAD), lambda b, hg, i: (b * nq + i, hg)),
            pl.BlockSpec((SEQ, nh * HEAD_PAD), lambda b, hg, i: (b, hg)),
            pl.BlockSpec((SEQ, nh * HEAD_PAD), lambda b, hg, i: (b, hg)),
        ],
        out_specs=pl.BlockSpec((t, nh * V_HEAD), lambda b, hg, i: (b * nq + i, hg)),
        out_shape=jax.ShapeDtypeStruct((TOKENS, N_HEADS * V_HEAD), BF16),
        scratch_shapes=[pltpu.VMEM((t, LANES), F32)] * (3 * nh),
        compiler_params=_cparams(("parallel", "parallel", "arbitrary"), VMEM_MB),
        name="mla_attention",
    )(q, k, v)


def _mixer_tail_kernel(gates_ref, sc_ref, sch_ref, gv_ref, gg_ref, gvh_ref, ggh_ref, att_ref, x_ref,
                       cwa_ref, woa_ref, cwc_ref, lng_ref, lnb_ref, woc_ref, wob_ref, wo_ref, gf_ref, wr_ref, br_ref,
                       out_ref, meta_ref, meta_t_ref, cnt_ref, xs0_ref, cbuf, ubuf, shifted, vbuf, run_ref, zbuf, zsem):
    tm = TM_MIX
    step = pl.program_id(0)

    @pl.when(step == 0)
    def _():
        run_ref[...] = jnp.zeros_like(run_ref)
        zbuf[...] = jnp.zeros_like(zbuf)

    def zero_copy(c):
        rows = pl.ds(pl.multiple_of((step * ZERO_COPIES + c) * ZERO_ROWS, ZERO_ROWS), ZERO_ROWS)
        return pltpu.make_async_copy(zbuf, xs0_ref.at[rows, :], zsem.at[0])

    for c in range(ZERO_COPIES):
        zero_copy(c).start()
    has_past = (pl.program_id(0) % (SEQ // tm)) != 0

    sc = sc_ref[...]
    sc_b = sc[:, 0:SC_WIDTH].astype(F32)
    cbuf[8:8 + tm, :] = sc[:, SC_WIDTH:2 * SC_WIDTH].astype(F32) * sc[:, 2 * SC_WIDTH:].astype(F32)
    sch = sch_ref[...].astype(F32)[HALO_A - 8:HALO_A, :]
    cbuf[0:8, :] = jnp.where(has_past, sch[:, SC_WIDTH:2 * SC_WIDTH] * sch[:, 2 * SC_WIDTH:], 0.0)
    conv_a = cwa_ref[0:1, :] * cbuf[6:6 + tm, :]
    for t in range(1, SC_KERNEL):
        conv_a = conv_a + cwa_ref[t:t + 1, :] * cbuf[6 + t:6 + t + tm, :]
    y_a = jnp.dot((sc_b * conv_a).astype(BF16), woa_ref[...], preferred_element_type=F32)

    ubuf[HALO_C:HALO_C + tm, :] = gv_ref[...].astype(F32) * _sigmoid(gg_ref[...].astype(F32))
    ubuf[0:HALO_C, :] = jnp.where(has_past, gvh_ref[...].astype(F32) * _sigmoid(ggh_ref[...].astype(F32)), 0.0)
    rows = tm + HALO_C - 8
    u_all = ubuf[...]
    for b in range(1, 8):
        shifted[b - 1, 0:rows, :] = pltpu.roll(u_all, tm + HALO_C - b, axis=0)[0:rows, :]
    base = HALO_C - (CONF_KERNEL - 1)
    for r0 in range(0, tm, CONV_CHUNK):
        acc = None
        for t in range(CONF_KERNEL):
            off = base + t
            a0 = r0 + off - off % 8
            src = ubuf[a0:a0 + CONV_CHUNK, :] if off % 8 == 0 else shifted[off % 8 - 1, a0:a0 + CONV_CHUNK, :]
            term = cwc_ref[t:t + 1, :] * src
            acc = term if acc is None else acc + term
        mu = jnp.mean(acc, axis=-1, keepdims=True)
        xc = acc - mu
        var = jnp.mean(xc * xc, axis=-1, keepdims=True)
        y = xc * lax.rsqrt(var + EPS) * lng_ref[...] + lnb_ref[...]
        vbuf[r0:r0 + CONV_CHUNK, :] = (y * _sigmoid(y)).astype(BF16)
    y_c = jnp.dot(vbuf[...], woc_ref[...], preferred_element_type=F32)

    y_b = jnp.dot(att_ref[...], wob_ref[...], preferred_element_type=F32)

    g = gates_ref[...]
    merged = (_sigmoid(g[:, 0:D_MODEL].astype(F32)) * y_a
              + _sigmoid(g[:, D_MODEL:2 * D_MODEL].astype(F32)) * y_b
              + _sigmoid(g[:, 2 * D_MODEL:].astype(F32)) * y_c)
    x_new = x_ref[...] + jnp.dot(merged.astype(BF16), wo_ref[...], preferred_element_type=F32)
    out_ref[...] = x_new
    _route_tile(x_new, gf_ref, wr_ref, br_ref, run_ref, meta_ref, meta_t_ref, cnt_ref)
    for c in range(ZERO_COPIES):
        zero_copy(c).wait()


def _mixer_tail(proj, tail, att, x, cwa, woa, cwc, lng, lnb, woc, wob, wo, gf, wr, br):
    tm = TM_MIX
    row = lambda width, col: pl.BlockSpec((tm, width), lambda i: (i, col // width))
    halo = lambda rows, width, col: pl.BlockSpec(
        (rows, width), lambda i: (jnp.maximum(i * (tm // rows) - 1, 0), col // width))
    const = lambda a: pl.BlockSpec(a.shape, lambda i: (0,) * a.ndim)
    weights = (cwa, woa, cwc, lng, lnb, woc, wob, wo, gf, wr, br)
    return pl.pallas_call(
        _mixer_tail_kernel,
        grid=(TOKENS // tm,),
        in_specs=[
            row(3 * D_MODEL, COL_GATES),
            row(3 * SC_WIDTH, COL_SC),
            halo(HALO_A, 3 * SC_WIDTH, COL_SC),
            row(CONF_WIDTH, TCOL_GLU),
            row(CONF_WIDTH, TCOL_GLU + CONF_WIDTH),
            halo(HALO_C, CONF_WIDTH, TCOL_GLU),
            halo(HALO_C, CONF_WIDTH, TCOL_GLU + CONF_WIDTH),
            pl.BlockSpec((tm, N_HEADS * V_HEAD), lambda i: (i, 0)),
            pl.BlockSpec((tm, D_MODEL), lambda i: (i, 0)),
        ] + [const(a) for a in weights],
        out_specs=[
            pl.BlockSpec((tm, D_MODEL), lambda i: (i, 0)),
            pl.BlockSpec((tm, LANES), lambda i: (i, 0)),
            pl.BlockSpec((META_ROWS, tm), lambda i: (0, i)),
            pl.BlockSpec((1, LANES), lambda i: (0, 0)),
            pl.BlockSpec(memory_space=pl.ANY),
        ],
        out_shape=(
            jax.ShapeDtypeStruct((TOKENS, D_MODEL), F32),
            jax.ShapeDtypeStruct((TOKENS, LANES), F32),
            jax.ShapeDtypeStruct((META_ROWS, TOKENS), F32),
            jax.ShapeDtypeStruct((1, LANES), F32),
            jax.ShapeDtypeStruct((SORTED_ROWS, LANES), F32),
        ),
        scratch_shapes=[
            pltpu.VMEM((tm + 8, SC_WIDTH), F32),
            pltpu.VMEM((tm + HALO_C, CONF_WIDTH), F32),
            pltpu.VMEM((7, tm + HALO_C - 8, CONF_WIDTH), F32),
            pltpu.VMEM((tm, CONF_WIDTH), BF16),
            pltpu.VMEM((1, LANES), F32),
            pltpu.VMEM((ZERO_ROWS, LANES), F32),
            pltpu.SemaphoreType.DMA((1,)),
        ],
        compiler_params=_cparams(("arbitrary",), VMEM_BIG_MB),
        name="mixer_tail",
    )(proj, proj, proj, tail, tail, tail, tail, att, x, *weights)


def _route(logits):
    lane = lax.broadcasted_iota(jnp.int32, logits.shape, 1)
    lane_f = lane.astype(F32)
    neg = -jnp.inf
    big = float(LANES)
    is_grp = (lane >= ROUTER_GROUP_LANE) & (lane < ROUTER_GROUP_LANE + N_GROUPS)
    glog = jnp.where(is_grp, logits, neg)
    gmax = jnp.max(glog, axis=-1, keepdims=True)
    gidx = jnp.min(jnp.where(glog == gmax, lane_f, big), axis=-1, keepdims=True)
    p_sel = 1.0 / jnp.sum(jnp.exp(glog - gmax), axis=-1, keepdims=True)
    first = (gidx - ROUTER_GROUP_LANE) * EXPERTS_PER_GROUP
    in_grp = (lane_f >= first) & (lane_f < first + EXPERTS_PER_GROUP)
    el = jnp.where(in_grp, logits, neg)
    m1 = jnp.max(el, axis=-1, keepdims=True)
    i1 = jnp.min(jnp.where(el == m1, lane_f, big), axis=-1, keepdims=True)
    el2 = jnp.where(lane_f == i1, neg, el)
    m2 = jnp.max(el2, axis=-1, keepdims=True)
    i2 = jnp.min(jnp.where(el2 == m2, lane_f, big), axis=-1, keepdims=True)
    e2 = jnp.exp(m2 - m1)
    w1 = p_sel / (1.0 + e2)
    w2 = w1 * e2
    return i1, i2, w1, w2


def _route_tile(x, g_ref, wr_ref, br_ref, run_ref, meta_ref, meta_t_ref, cnt_ref):
    tm = x.shape[0]
    h = _rms(x, g_ref[...])
    h_hi = h.astype(BF16)
    h_lo = (h - h_hi.astype(F32)).astype(BF16)
    hi_terms = jnp.dot(h_hi, wr_ref[...], preferred_element_type=F32)
    logits = (hi_terms[:, :LANES] + hi_terms[:, LANES:]
              + jnp.dot(h_lo, wr_ref[:, :LANES], preferred_element_type=F32) + br_ref[...])
    i1, i2, w1, w2 = _route(logits)
    lane = lax.broadcasted_iota(jnp.int32, (tm, LANES), 1)
    lane_f = lane.astype(F32)
    oh1 = lane_f == i1
    oh2 = lane_f == i2
    onehot = jnp.where(oh1, 1.0, 0.0) + jnp.where(oh2, 1.0, 0.0)
    row_id = lax.broadcasted_iota(jnp.int32, (tm, tm), 0)
    col_id = lax.broadcasted_iota(jnp.int32, (tm, tm), 1)
    below = jnp.where(row_id > col_id, 1.0, 0.0).astype(BF16)
    before = run_ref[...] + jnp.dot(below, onehot.astype(BF16), preferred_element_type=F32)
    r1 = jnp.sum(jnp.where(oh1, before, 0.0), axis=-1, keepdims=True)
    r2 = jnp.sum(jnp.where(oh2, before, 0.0), axis=-1, keepdims=True)
    run_ref[...] += jnp.sum(onehot, axis=0, keepdims=True)
    cnt_ref[...] = run_ref[...]
    meta = jnp.zeros((tm, LANES), F32)
    for col, val in enumerate((i1, i2, r1, r2, w1, w2)):
        meta = jnp.where(lane == col, val, meta)
    meta_ref[...] = meta
    meta_t_ref[...] = meta.T[0:META_ROWS, :]


def _dispatch_plan(meta_t, counts):
    e1 = meta_t[META_E1].astype(jnp.int32)
    e2 = meta_t[META_E2].astype(jnp.int32)
    r1 = meta_t[META_R1].astype(jnp.int32)
    r2 = meta_t[META_R2].astype(jnp.int32)
    cnt = counts[0, :N_EXPERTS].astype(jnp.int32)
    tiles = (cnt + TM_EXP - 1) // TM_EXP
    tile_end = jnp.cumsum(tiles)
    first_slot = ((tile_end - tiles) * TM_EXP)[:, None]
    expert = jnp.arange(N_EXPERTS, dtype=jnp.int32)[:, None]
    pos1 = jnp.sum(jnp.where(e1[None, :] == expert, first_slot, 0), axis=0) + r1
    pos2 = jnp.sum(jnp.where(e2[None, :] == expert, first_slot, 0), axis=0) + r2
    n_tiles = tile_end[-1:]
    tile_id = jnp.minimum(jnp.arange(MAX_TILES, dtype=jnp.int32), n_tiles - 1)
    tile_expert = jnp.sum((tile_id[:, None] >= tile_end[None, :]).astype(jnp.int32), axis=1)
    return pos1, pos2, tile_expert, n_tiles


def _dispatch_kernel(pos1_ref, pos2_ref, x_ref, g_ref, xs_in_ref, xs_ref, slab, sem):
    del xs_in_ref
    tm = TM_DISP
    i = pl.program_id(0)
    last = pl.num_programs(0) - 1
    slot = i % 2

    def wait_copies(sl):
        for _ in range(TOP_K):
            pltpu.make_async_copy(slab.at[sl], xs_ref.at[pl.ds(0, tm * SLAB), :], sem.at[sl]).wait()

    @pl.when(i >= 2)
    def _():
        wait_copies(slot)

    h = _rms(x_ref[...], g_ref[...])
    for s in range(SLAB):
        slab[slot, pl.ds(s, tm, stride=SLAB), :] = h[:, s * LANES:(s + 1) * LANES]

    def body(c, carry):
        for u in range(GATHER_UNROLL):
            r = c * GATHER_UNROLL + u
            src = slab.at[slot, pl.ds(pl.multiple_of(r * SLAB, SLAB), SLAB), :]
            for k, pos_ref in enumerate((pos1_ref, pos2_ref)):
                p = pos_ref[i * tm + r]
                pltpu.make_async_copy(src, xs_ref.at[pl.ds(pl.multiple_of(p * SLAB, SLAB), SLAB), :],
                                      sem.at[slot]).start(priority=k % N_DMA_PRIORITIES)
        return carry

    lax.fori_loop(0, tm // GATHER_UNROLL, body, 0)

    @pl.when(i == last)
    def _():
        wait_copies(1 - slot)
        wait_copies(slot)


def _dispatch(pos1, pos2, x, g, xs0):
    tm = TM_DISP
    return pl.pallas_call(
        _dispatch_kernel,
        grid_spec=pltpu.PrefetchScalarGridSpec(
            num_scalar_prefetch=2,
            grid=(TOKENS // tm,),
            in_specs=[
                pl.BlockSpec((tm, D_MODEL), lambda i, p1, p2: (i, 0)),
                pl.BlockSpec((1, D_MODEL), lambda i, p1, p2: (0, 0)),
                pl.BlockSpec(memory_space=pl.ANY),
            ],
            out_specs=pl.BlockSpec(memory_space=pl.ANY),
            scratch_shapes=[pltpu.VMEM((2, tm * SLAB, LANES), F32), pltpu.SemaphoreType.DMA((2,))],
        ),
        out_shape=jax.ShapeDtypeStruct((SORTED_ROWS, LANES), F32),
        input_output_aliases={4: 0},
        compiler_params=_cparams(("arbitrary",), VMEM_MB),
        name="moe_dispatch",
    )(pos1, pos2, x, g, xs0)


def _expert_kernel(te_ref, nt_ref, xs_ref, wg_ref, wu_ref, wd_ref, ys_ref, xbuf, sem):
    del te_ref
    tm = TM_EXP
    i = pl.program_id(0)
    n = nt_ref[0]

    def tile_copy(tile):
        slot = tile % EXPERT_SLOTS
        rows = pl.ds(pl.multiple_of(tile * (tm * SLAB), tm * SLAB), tm * SLAB)
        return pltpu.make_async_copy(xs_ref.at[rows, :], xbuf.at[slot], sem.at[slot])

    @pl.when(i == 0)
    def _():
        for ahead in range(EXPERT_PREFETCH):
            @pl.when(ahead < n)
            def _():
                tile_copy(ahead).start()

    @pl.when(i + EXPERT_PREFETCH < n)
    def _():
        tile_copy(i + EXPERT_PREFETCH).start()

    @pl.when(i < n)
    def _():
        tile_copy(i).wait()
        slot = i % EXPERT_SLOTS
        xt = jnp.concatenate([xbuf[slot, pl.ds(s, tm, stride=SLAB), :].astype(BF16) for s in range(SLAB)], axis=1)
        hg = jnp.dot(xt, wg_ref[...].astype(BF16), preferred_element_type=F32)
        hu = jnp.dot(xt, wu_ref[...].astype(BF16), preferred_element_type=F32)
        hh = (hg * _sigmoid(hg) * hu).astype(BF16)
        y = jnp.dot(hh, wd_ref[...].astype(BF16), preferred_element_type=F32)
        for s in range(SLAB):
            ys_ref[pl.ds(s, tm, stride=SLAB), :] = y[:, s * LANES:(s + 1) * LANES]

    @pl.when(i >= n)
    def _():
        ys_ref[...] = jnp.zeros_like(ys_ref)


def _experts(layer, tile_expert, n_tiles, xs, w_gate, w_up, w_down):
    tm = TM_EXP
    wspec = lambda rows, cols: pl.BlockSpec((None, None, rows, cols), lambda i, te, nt: (layer, te[i], 0, 0))
    return pl.pallas_call(
        _expert_kernel,
        grid_spec=pltpu.PrefetchScalarGridSpec(
            num_scalar_prefetch=2,
            grid=(MAX_TILES,),
            in_specs=[
                pl.BlockSpec(memory_space=pl.ANY),
                wspec(D_MODEL, EXPERT_HIDDEN),
                wspec(D_MODEL, EXPERT_HIDDEN),
                wspec(EXPERT_HIDDEN, D_MODEL),
            ],
            out_specs=pl.BlockSpec((tm * SLAB, LANES), lambda i, te, nt: (i, 0)),
            scratch_shapes=[pltpu.VMEM((EXPERT_SLOTS, tm * SLAB, LANES), F32), pltpu.SemaphoreType.DMA((EXPERT_SLOTS,))],
        ),
        out_shape=jax.ShapeDtypeStruct((SORTED_ROWS, LANES), F32),
        compiler_params=_cparams(("arbitrary",), VMEM_MB),
        name="moe_experts",
    )(tile_expert, n_tiles, xs, w_gate, w_up, w_down)


def _start_slab_gathers(idx_refs, base, n_rows, src_hbm, dst_bufs, sem):
    def body(c, carry):
        for u in range(GATHER_UNROLL):
            r = c * GATHER_UNROLL + u
            for k, (idx_ref, dst) in enumerate(zip(idx_refs, dst_bufs)):
                t = idx_ref[base + r]
                pltpu.make_async_copy(src_hbm.at[pl.ds(pl.multiple_of(t * SLAB, SLAB), SLAB), :],
                                      dst.at[pl.ds(pl.multiple_of(r * SLAB, SLAB), SLAB), :],
                                      sem).start(priority=k % N_DMA_PRIORITIES)
        return carry

    lax.fori_loop(0, n_rows // GATHER_UNROLL, body, 0)


def _wait_slab_gathers(n_rows, src_hbm, dst, sem):
    pltpu.make_async_copy(src_hbm.at[pl.ds(0, n_rows * SLAB), :], dst, sem).wait()


def _combine_ple_kernel(pos1_ref, pos2_ref, x_ref, meta_ref, ys_ref, p_ref, g_ref, wg_ref, wp_ref, fg_ref,
                        out_ref, cbuf, sem, *, final):
    tm = TM_COMB
    i = pl.program_id(0)

    def gather(tile, slot):
        _start_slab_gathers((pos1_ref, pos2_ref), tile * tm, tm, ys_ref, (cbuf.at[slot, 0], cbuf.at[slot, 1]),
                            sem.at[slot])

    @pl.when(i == 0)
    def _():
        gather(0, 0)

    @pl.when(i + 1 < pl.num_programs(0))
    def _():
        gather(i + 1, (i + 1) % 2)

    slot = i % 2
    for k in range(2):
        _wait_slab_gathers(tm, ys_ref, cbuf.at[slot, k], sem.at[slot])
    meta = meta_ref[...]
    w1 = meta[:, META_W1:META_W1 + 1]
    w2 = meta[:, META_W2:META_W2 + 1]
    moe = jnp.concatenate([w1 * cbuf[slot, 0, pl.ds(s, tm, stride=SLAB), :]
                           + w2 * cbuf[slot, 1, pl.ds(s, tm, stride=SLAB), :] for s in range(SLAB)], axis=1)
    x = x_ref[...] + moe
    h = _rms(x, g_ref[...]).astype(BF16)
    gate = _sigmoid(jnp.dot(h, wg_ref[...], preferred_element_type=F32))
    emb = jnp.dot(p_ref[...].astype(BF16), wp_ref[...], preferred_element_type=F32)
    y = x + gate * emb
    if final:
        y = _rms(y, fg_ref[...])
    out_ref[...] = y


def _combine_ple(layer, pos1, pos2, x, meta, ys, p, g, wg, wp, fg, final):
    tm = TM_COMB
    const = lambda i, p1, p2: (0, 0)
    rows = lambda i, p1, p2: (i, 0)
    return pl.pallas_call(
        functools.partial(_combine_ple_kernel, final=final),
        grid_spec=pltpu.PrefetchScalarGridSpec(
            num_scalar_prefetch=2,
            grid=(TOKENS // tm,),
            in_specs=[
                pl.BlockSpec((tm, D_MODEL), rows),
                pl.BlockSpec((tm, LANES), rows),
                pl.BlockSpec(memory_space=pl.ANY),
                pl.BlockSpec((None, tm, PLE_DIM), lambda i, p1, p2: (layer, i, 0)),
                pl.BlockSpec((1, D_MODEL), const),
                pl.BlockSpec((D_MODEL, D_MODEL), const),
                pl.BlockSpec((PLE_DIM, D_MODEL), const),
                pl.BlockSpec((1, D_MODEL), const),
            ],
            out_specs=pl.BlockSpec((tm, D_MODEL), rows),
            scratch_shapes=[pltpu.VMEM((2, 2, tm * SLAB, LANES), F32), pltpu.SemaphoreType.DMA((2,))],
        ),
        out_shape=jax.ShapeDtypeStruct((TOKENS, D_MODEL), F32),
        compiler_params=_cparams(("arbitrary",), VMEM_MB),
        name="moe_combine_ple",
    )(pos1, pos2, x, meta, ys, p, g, wg, wp, fg)


def _in_proj_weights(w_in):
    k_rope = w_in[:, MAIN_COLS:MAIN_COLS + QK_ROPE]
    glu = w_in[:, MAIN_COLS + QK_ROPE:]
    half = QK_ROPE // 2
    zn = jnp.zeros((D_MODEL, QK_NOPE), F32)
    zp = jnp.zeros((D_MODEL, HEAD_PAD - QK_NOPE - QK_ROPE), F32)
    tail = jnp.concatenate([glu, zn, k_rope, zp, zn, k_rope[:, half:], k_rope[:, :half], zp], axis=1)
    return w_in[:, :MAIN_COLS].astype(BF16), tail.astype(BF16)


def _q_weight(w_uq):
    scale = (QK_NOPE + QK_ROPE) ** -0.5 * LOG2_E
    w = (w_uq * scale).reshape(Q_LORA, N_HEADS, QK_NOPE + QK_ROPE)
    zp = jnp.zeros((Q_LORA, N_HEADS, HEAD_PAD - QK_NOPE - QK_ROPE), F32)
    return jnp.concatenate([w, zp], axis=2).reshape(Q_LORA, N_HEADS * HEAD_PAD).astype(BF16)


def _kv_weight(w_ukv):
    w = w_ukv.reshape(KV_LORA, N_HEADS, QK_NOPE + V_HEAD)
    k_nope, v = w[:, :, :QK_NOPE], w[:, :, QK_NOPE:]
    z = jnp.zeros_like(v)
    k_part = jnp.concatenate([k_nope, jnp.zeros_like(k_nope)], axis=2).reshape(KV_LORA, N_HEADS * HEAD_PAD)
    odd = (jnp.arange(N_HEADS) % 2 == 1)[None, :, None]
    v_part = jnp.concatenate([jnp.where(odd, z, v), jnp.where(odd, v, z)], axis=2).reshape(KV_LORA, N_HEADS * HEAD_PAD)
    return jnp.concatenate([k_part, v_part], axis=1).astype(BF16)


def _router_weight(w_rg, b_rg, w_re, b_re):
    pad = LANES - N_EXPERTS - N_GROUPS
    w = jnp.concatenate([w_re, w_rg, jnp.zeros((D_MODEL, pad), F32)], axis=1)
    b = jnp.concatenate([b_re, b_rg, jnp.zeros((pad,), F32)]).reshape(1, LANES)
    w_hi = w.astype(BF16)
    w_lo = (w - w_hi.astype(F32)).astype(BF16)
    return jnp.concatenate([w_hi, w_lo], axis=1), b


def kernel(x, p, positions, ln_mix_g, w_in, conv_a_w, w_out_a, q_norm_g, w_uq, kv_norm_g, w_ukv, w_out_b, conv_c_w, ln_c_g, ln_c_b, w_out_c, w_o, ln_ffn_g, w_route_grp, b_route_grp, w_route_exp, b_route_exp, w_exp_gate, w_exp_up, w_exp_down, ln_ple_g, w_ple_gate, w_ple, final_norm_g):
    c_tab, s_tab = _rope_tables(positions)
    xf = x.reshape(TOKENS, D_MODEL)
    pf = p.reshape(DEPTH, TOKENS, PLE_DIM)
    row = lambda a: a.reshape(1, -1)
    for i in range(DEPTH):
        proj, tail = _inproj(xf, row(ln_mix_g[i]), *_in_proj_weights(w_in[i]))
        q, k, v = _qkv(proj, tail, c_tab, s_tab, row(q_norm_g[i]), row(kv_norm_g[i]), _q_weight(w_uq[i]), _kv_weight(w_ukv[i]))
        att = _attention(q, k, v)
        wr, br = _router_weight(w_route_grp[i], b_route_grp[i], w_route_exp[i], b_route_exp[i])
        xf, meta, meta_t, counts, xs0 = _mixer_tail(
            proj, tail, att, xf, conv_a_w[i], w_out_a[i].astype(BF16), conv_c_w[i], row(ln_c_g[i]), row(ln_c_b[i]),
            w_out_c[i].astype(BF16), w_out_b[i].astype(BF16), w_o[i].astype(BF16), row(ln_ffn_g[i]), wr, br)
        pos1, pos2, tile_expert, n_tiles = _dispatch_plan(meta_t, counts)
        xs = _dispatch(pos1, pos2, xf, row(ln_ffn_g[i]), xs0)
        ys = _experts(i, tile_expert, n_tiles, xs, w_exp_gate, w_exp_up, w_exp_down)
        xf = _combine_ple(i, pos1, pos2, xf, meta, ys, pf, row(ln_ple_g[i]), w_ple_gate[i].astype(BF16),
                          w_ple[i].astype(BF16), row(final_norm_g), final=(i == DEPTH - 1))
    return xf.reshape(BATCH, SEQ, D_MODEL)
```

```python
import functools

import jax
import jax.numpy as jnp
from jax import lax
from jax.experimental import pallas as pl
from jax.experimental.pallas import tpu as pltpu

D_MODEL = 1024
BATCH = 8
SEQ = 2048
DEPTH = 2
TOKENS = BATCH * SEQ
PLE_DIM = 256
SC_WIDTH = 512
SC_KERNEL = 3
N_HEADS = 8
QK_NOPE = 64
QK_ROPE = 32
V_HEAD = 64
Q_LORA = 768
KV_LORA = 256
ROPE_THETA = 10000.0
CONF_WIDTH = 512
CONF_KERNEL = 31
N_GROUPS = 4
EXPERTS_PER_GROUP = 8
N_EXPERTS = N_GROUPS * EXPERTS_PER_GROUP
EXPERT_HIDDEN = 256
EPS = 1e-6
LOG2_E = 1.4426950408889634

LANES = 128
HEAD_PAD = 128
F32 = jnp.float32
BF16 = jnp.bfloat16

COL_GATES = 0
COL_SC = 3 * D_MODEL
COL_QLAT = COL_SC + 3 * SC_WIDTH
COL_KVLAT = COL_QLAT + Q_LORA
MAIN_COLS = COL_KVLAT + KV_LORA
TCOL_GLU = 0
TCOL_KR = 2 * CONF_WIDTH
TCOL_KRSW = TCOL_KR + HEAD_PAD
TAIL_COLS = TCOL_KRSW + HEAD_PAD

ROUTER_GROUP_LANE = N_EXPERTS

MXU_TILE = 256
VMEM_MB = 48
VMEM_BIG_MB = 56

TM_INPROJ = 1024
TN_INPROJ = MAIN_COLS // 2
TN_SPLIT = 6 * MXU_TILE
TM_QKV = 512
T_ATTN = 512
HEADS_PER_STEP = 8
TM_MIX = 256
CONV_CHUNK = 64
HALO_C = 32
HALO_A = 16
TM_EXP = 512
EXPERT_PREFETCH = 2
EXPERT_SLOTS = EXPERT_PREFETCH + 1
TM_DISP = 512
TM_COMB = 256
TOP_K = 2
MAX_TILES = TOKENS * TOP_K // TM_EXP + N_EXPERTS
SLAB = D_MODEL // LANES
GATHER_UNROLL = 16
N_DMA_PRIORITIES = 2
META_E1, META_E2, META_R1, META_R2, META_W1, META_W2 = range(6)
META_ROWS = 8
SORTED_ROWS = MAX_TILES * TM_EXP * SLAB
ZERO_COPIES = 6
ZERO_ROWS = SORTED_ROWS // (TOKENS // TM_MIX) // ZERO_COPIES


def _cparams(semantics, vmem_mb):
    return pltpu.CompilerParams(dimension_semantics=semantics, vmem_limit_bytes=vmem_mb * 1024 * 1024)


def _sigmoid(x):
    return 1.0 / (1.0 + jnp.exp2(x * -LOG2_E))


def _rms(x, g):
    return x * lax.rsqrt(jnp.mean(x * x, axis=-1, keepdims=True) + EPS) * g


def _place(x, onehot):
    x1 = x.astype(BF16)
    r1 = x - x1.astype(F32)
    x2 = r1.astype(BF16)
    x3 = (r1 - x2.astype(F32)).astype(BF16)
    return (jnp.dot(x1, onehot, preferred_element_type=F32) + jnp.dot(x2, onehot, preferred_element_type=F32)
            + jnp.dot(x3, onehot, preferred_element_type=F32))


def _rope_kernel(pos_ref, freq_ref, c_ref, s_ref):
    half = QK_ROPE // 2
    per_row = LANES // half
    rows = TOKENS // per_row
    ang = pos_ref[...].astype(F32) * freq_ref[...]
    cos = jnp.cos(ang)
    sin = jnp.sin(ang)
    src = lax.broadcasted_iota(jnp.int32, (LANES, LANES), 0)
    dst = lax.broadcasted_iota(jnp.int32, (LANES, LANES), 1)
    lane = lax.broadcasted_iota(jnp.int32, (1, LANES), 1)
    ones_nope = jnp.where(lane < QK_NOPE, 1.0, 0.0)
    sign = jnp.where(lane < QK_NOPE + half, -1.0, 1.0)
    for j in range(per_row):
        f = src - half * j
        hit = (dst == QK_NOPE + f) | (dst == QK_NOPE + half + f)
        onehot = jnp.where((f >= 0) & (f < half) & hit, 1.0, 0.0).astype(BF16)
        c_ref[pl.ds(j, rows, stride=per_row), :] = _place(cos, onehot) + ones_nope
        s_ref[pl.ds(j, rows, stride=per_row), :] = _place(sin, onehot) * sign


def _rope_tables(positions):
    half = QK_ROPE // 2
    inv_freq = ROPE_THETA ** (-jnp.arange(0, QK_ROPE, 2, dtype=F32) / QK_ROPE)
    rows = TOKENS * half // LANES
    pos_rep = jnp.broadcast_to(positions.reshape(TOKENS, 1), (TOKENS, half)).reshape(rows, LANES)
    freq = jnp.tile(inv_freq, LANES // half).reshape(1, LANES)
    return pl.pallas_call(
        _rope_kernel,
        out_shape=(jax.ShapeDtypeStruct((TOKENS, HEAD_PAD), F32),) * 2,
        compiler_params=_cparams(None, VMEM_MB),
        name="rope_tables",
    )(pos_rep, freq)


def _inproj_kernel(x_ref, g_ref, wm_ref, wt_ref, om_ref, ot_ref, h_ref):
    j = pl.program_id(1)
    n_main = MAIN_COLS // TN_INPROJ

    @pl.when(j == 0)
    def _():
        h_ref[...] = _rms(x_ref[...], g_ref[...]).astype(BF16)

    @pl.when(j < n_main)
    def _():
        for lo, hi in ((0, TN_SPLIT), (TN_SPLIT, TN_INPROJ)):
            om_ref[:, lo:hi] = jnp.dot(h_ref[...], wm_ref[:, lo:hi], preferred_element_type=F32).astype(BF16)

    @pl.when(j == n_main)
    def _():
        ot_ref[...] = jnp.dot(h_ref[...], wt_ref[...], preferred_element_type=F32).astype(BF16)


def _inproj(x, g, w_main, w_tail):
    tm, tn = TM_INPROJ, TN_INPROJ
    n_main = MAIN_COLS // tn
    main_col = lambda i, j: jnp.minimum(j, n_main - 1)
    return pl.pallas_call(
        _inproj_kernel,
        grid=(TOKENS // tm, n_main + 1),
        in_specs=[
            pl.BlockSpec((tm, D_MODEL), lambda i, j: (i, 0)),
            pl.BlockSpec((1, D_MODEL), lambda i, j: (0, 0)),
            pl.BlockSpec((D_MODEL, tn), lambda i, j: (0, main_col(i, j))),
            pl.BlockSpec((D_MODEL, TAIL_COLS), lambda i, j: (0, 0)),
        ],
        out_specs=[
            pl.BlockSpec((tm, tn), lambda i, j: (i, main_col(i, j))),
            pl.BlockSpec((tm, TAIL_COLS), lambda i, j: (i, 0)),
        ],
        out_shape=(jax.ShapeDtypeStruct((TOKENS, MAIN_COLS), BF16), jax.ShapeDtypeStruct((TOKENS, TAIL_COLS), BF16)),
        scratch_shapes=[pltpu.VMEM((tm, D_MODEL), BF16)],
        compiler_params=_cparams(("parallel", "arbitrary"), VMEM_BIG_MB),
        name="in_proj",
    )(x, g, w_main, w_tail)


def _qkv_kernel(ql_ref, kvl_ref, kr_ref, krsw_ref, c_ref, s_ref, qg_ref, kvg_ref, wq_ref, wkv_ref,
                q_out, k_out, v_out):
    c = c_ref[...]
    s = s_ref[...]
    width = N_HEADS * HEAD_PAD
    half = QK_ROPE // 2
    low_half = lax.broadcasted_iota(jnp.int32, c.shape, 1) < QK_NOPE + half

    def swap_halves(x):
        return jnp.where(low_half, pltpu.roll(x, HEAD_PAD - half, axis=1), pltpu.roll(x, half, axis=1))

    qn = _rms(ql_ref[...].astype(F32), qg_ref[...]).astype(BF16)
    qq = jnp.dot(qn, wq_ref[...], preferred_element_type=F32)
    for h in range(N_HEADS):
        lo, hi = h * HEAD_PAD, (h + 1) * HEAD_PAD
        q_out[:, lo:hi] = (qq[:, lo:hi] * c + swap_halves(qq[:, lo:hi]) * s).astype(BF16)
    kvn = _rms(kvl_ref[...].astype(F32), kvg_ref[...]).astype(BF16)
    kk = jnp.dot(kvn, wkv_ref[...], preferred_element_type=F32)
    kr = kr_ref[...].astype(F32) * c + krsw_ref[...].astype(F32) * s
    for h in range(N_HEADS):
        lo, hi = h * HEAD_PAD, (h + 1) * HEAD_PAD
        k_out[:, lo:hi] = (kk[:, lo:hi] + kr).astype(BF16)
    v_out[...] = kk[:, width:].astype(BF16)


def _qkv(proj, tail, c_tab, s_tab, qg, kvg, wq, wkv):
    tm = TM_QKV
    width = N_HEADS * HEAD_PAD
    row = lambda blk: (lambda i: (i, blk))
    const = lambda i: (0, 0)
    return pl.pallas_call(
        _qkv_kernel,
        grid=(TOKENS // tm,),
        in_specs=[
            pl.BlockSpec((tm, Q_LORA), row(COL_QLAT // Q_LORA)),
            pl.BlockSpec((tm, KV_LORA), row(COL_KVLAT // KV_LORA)),
            pl.BlockSpec((tm, HEAD_PAD), row(TCOL_KR // HEAD_PAD)),
            pl.BlockSpec((tm, HEAD_PAD), row(TCOL_KRSW // HEAD_PAD)),
            pl.BlockSpec((tm, HEAD_PAD), row(0)),
            pl.BlockSpec((tm, HEAD_PAD), row(0)),
            pl.BlockSpec((1, Q_LORA), const),
            pl.BlockSpec((1, KV_LORA), const),
            pl.BlockSpec((Q_LORA, width), const),
            pl.BlockSpec((KV_LORA, 2 * width), const),
        ],
        out_specs=[pl.BlockSpec((tm, width), row(0))] * 3,
        out_shape=(jax.ShapeDtypeStruct((TOKENS, width), BF16),) * 3,
        compiler_params=_cparams(("parallel",), VMEM_MB),
        name="qkv_prep",
    )(proj, proj, tail, tail, c_tab, s_tab, qg, kvg, wq, wkv)


def _attn_kernel(q_ref, k_ref, v_ref, o_ref, *state):
    t = T_ATTN
    nh = HEADS_PER_STEP
    qi = pl.program_id(2)
    nt = (((1,), (1,)), ((), ()))
    m_sc, l_sc, acc_sc = state[0:nh], state[nh:2 * nh], state[2 * nh:3 * nh]
    for h in range(nh):
        m_sc[h][...] = jnp.full((t, LANES), -jnp.inf, F32)
        l_sc[h][...] = jnp.zeros((t, LANES), F32)
        acc_sc[h][...] = jnp.zeros((t, LANES), F32)

    def block(j, r0, nr, c0, nc, masked):
        start = pl.multiple_of(j * t + c0, nc)
        rows = slice(r0, r0 + nr)
        for h in range(nh):
            lo, hi = h * HEAD_PAD, (h + 1) * HEAD_PAD
            s = lax.dot_general(q_ref[rows, lo:hi], k_ref[pl.ds(start, nc), lo:hi], nt, preferred_element_type=F32)
            if masked:
                row_id = r0 + lax.broadcasted_iota(jnp.int32, (nr, nc), 0)
                col_id = c0 + lax.broadcasted_iota(jnp.int32, (nr, nc), 1)
                s = jnp.where(row_id >= col_id, s, -jnp.inf)
            blocks = [s[:, c * LANES:(c + 1) * LANES] for c in range(nc // LANES)]
            bmax = functools.reduce(jnp.maximum, blocks)
            m_old = m_sc[h][rows, :]
            m_new = jnp.maximum(m_old, jnp.max(bmax, axis=-1, keepdims=True))
            alpha = jnp.exp2(m_old - m_new)
            ps = [jnp.exp2(b - m_new) for b in blocks]
            p = jnp.concatenate(ps, axis=1).astype(BF16)
            l_sc[h][rows, :] = alpha * l_sc[h][rows, :] + functools.reduce(jnp.add, ps)
            acc_sc[h][rows, :] = alpha * acc_sc[h][rows, :] + jnp.dot(p, v_ref[pl.ds(start, nc), lo:hi],
                                                                       preferred_element_type=F32)
            m_sc[h][rows, :] = m_new

    def body(j, carry):
        block(j, 0, t, 0, t, False)
        return carry

    lax.fori_loop(0, qi, body, 0)
    half = t // 2
    block(qi, 0, t, 0, half, True)
    block(qi, half, half, half, half, True)
    out = [acc_sc[h][...] / jnp.sum(l_sc[h][...], axis=-1, keepdims=True) for h in range(nh)]
    o_ref[...] = jnp.concatenate([out[h] + out[h + 1] for h in range(0, nh, 2)], axis=1).astype(BF16)


def _attention(q, k, v):
    t = T_ATTN
    nq = SEQ // t
    nh = HEADS_PER_STEP
    return pl.pallas_call(
        _attn_kernel,
        grid=(BATCH, N_HEADS // nh, nq),
        in_specs=[
            pl.BlockSpec((t, nh * HEAD_PAD), lambda b, hg, i: (b * nq + i, hg)),
            pl.BlockSpec((SEQ, nh * HEAD_PAD), lambda b, hg, i: (b, hg)),
            pl.BlockSpec((SEQ, nh * HEAD_PAD), lambda b, hg, i: (b, hg)),
        ],
        out_specs=pl.BlockSpec((t, nh * V_HEAD), lambda b, hg, i: (b * nq + i, hg)),
        out_shape=jax.ShapeDtypeStruct((TOKENS, N_HEADS * V_HEAD), BF16),
        scratch_shapes=[pltpu.VMEM((t, LANES), F32)] * (3 * nh),
        compiler_params=_cparams(("parallel", "parallel", "arbitrary"), VMEM_MB),
        name="mla_attention",
    )(q, k, v)


def _mixer_tail_kernel(gates_ref, sc_ref, sch_ref, gv_ref, gg_ref, gvh_ref, ggh_ref, att_ref, x_ref,
                       cwa_ref, woa_ref, cwc_ref, lng_ref, lnb_ref, woc_ref, wob_ref, wo_ref, gf_ref, wr_ref, br_ref,
                       out_ref, meta_ref, meta_t_ref, cnt_ref, xs0_ref, cbuf, ubuf, shifted, vbuf, run_ref, zbuf, zsem):
    tm = TM_MIX
    step = pl.program_id(0)

    @pl.when(step == 0)
    def _():
        run_ref[...] = jnp.zeros_like(run_ref)
        zbuf[...] = jnp.zeros_like(zbuf)

    def zero_copy(c):
        rows = pl.ds(pl.multiple_of((step * ZERO_COPIES + c) * ZERO_ROWS, ZERO_ROWS), ZERO_ROWS)
        return pltpu.make_async_copy(zbuf, xs0_ref.at[rows, :], zsem.at[0])

    for c in range(ZERO_COPIES):
        zero_copy(c).start()
    has_past = (pl.program_id(0) % (SEQ // tm)) != 0

    sc = sc_ref[...]
    sc_b = sc[:, 0:SC_WIDTH].astype(F32)
    cbuf[8:8 + tm, :] = sc[:, SC_WIDTH:2 * SC_WIDTH].astype(F32) * sc[:, 2 * SC_WIDTH:].astype(F32)
    sch = sch_ref[...].astype(F32)[HALO_A - 8:HALO_A, :]
    cbuf[0:8, :] = jnp.where(has_past, sch[:, SC_WIDTH:2 * SC_WIDTH] * sch[:, 2 * SC_WIDTH:], 0.0)
    conv_a = cwa_ref[0:1, :] * cbuf[6:6 + tm, :]
    for t in range(1, SC_KERNEL):
        conv_a = conv_a + cwa_ref[t:t + 1, :] * cbuf[6 + t:6 + t + tm, :]
    y_a = jnp.dot((sc_b * conv_a).astype(BF16), woa_ref[...], preferred_element_type=F32)

    ubuf[HALO_C:HALO_C + tm, :] = gv_ref[...].astype(F32) * _sigmoid(gg_ref[...].astype(F32))
    ubuf[0:HALO_C, :] = jnp.where(has_past, gvh_ref[...].astype(F32) * _sigmoid(ggh_ref[...].astype(F32)), 0.0)
    rows = tm + HALO_C - 8
    u_all = ubuf[...]
    for b in range(1, 8):
        shifted[b - 1, 0:rows, :] = pltpu.roll(u_all, tm + HALO_C - b, axis=0)[0:rows, :]
    base = HALO_C - (CONF_KERNEL - 1)
    for r0 in range(0, tm, CONV_CHUNK):
        acc = None
        for t in range(CONF_KERNEL):
            off = base + t
            a0 = r0 + off - off % 8
            src = ubuf[a0:a0 + CONV_CHUNK, :] if off % 8 == 0 else shifted[off % 8 - 1, a0:a0 + CONV_CHUNK, :]
            term = cwc_ref[t:t + 1, :] * src
            acc = term if acc is None else acc + term
        mu = jnp.mean(acc, axis=-1, keepdims=True)
        xc = acc - mu
        var = jnp.mean(xc * xc, axis=-1, keepdims=True)
        y = xc * lax.rsqrt(var + EPS) * lng_ref[...] + lnb_ref[...]
        vbuf[r0:r0 + CONV_CHUNK, :] = (y * _sigmoid(y)).astype(BF16)
    y_c = jnp.dot(vbuf[...], woc_ref[...], preferred_element_type=F32)

    y_b = jnp.dot(att_ref[...], wob_ref[...], preferred_element_type=F32)

    g = gates_ref[...]
    merged = (_sigmoid(g[:, 0:D_MODEL].astype(F32)) * y_a
              + _sigmoid(g[:, D_MODEL:2 * D_MODEL].astype(F32)) * y_b
              + _sigmoid(g[:, 2 * D_MODEL:].astype(F32)) * y_c)
    x_new = x_ref[...] + jnp.dot(merged.astype(BF16), wo_ref[...], preferred_element_type=F32)
    out_ref[...] = x_new
    _route_tile(x_new, gf_ref, wr_ref, br_ref, run_ref, meta_ref, meta_t_ref, cnt_ref)
    for c in range(ZERO_COPIES):
        zero_copy(c).wait()


def _mixer_tail(proj, tail, att, x, cwa, woa, cwc, lng, lnb, woc, wob, wo, gf, wr, br):
    tm = TM_MIX
    row = lambda width, col: pl.BlockSpec((tm, width), lambda i: (i, col // width))
    halo = lambda rows, width, col: pl.BlockSpec(
        (rows, width), lambda i: (jnp.maximum(i * (tm // rows) - 1, 0), col // width))
    const = lambda a: pl.BlockSpec(a.shape, lambda i: (0,) * a.ndim)
    weights = (cwa, woa, cwc, lng, lnb, woc, wob, wo, gf, wr, br)
    return pl.pallas_call(
        _mixer_tail_kernel,
        grid=(TOKENS // tm,),
        in_specs=[
            row(3 * D_MODEL, COL_GATES),
            row(3 * SC_WIDTH, COL_SC),
            halo(HALO_A, 3 * SC_WIDTH, COL_SC),
            row(CONF_WIDTH, TCOL_GLU),
            row(CONF_WIDTH, TCOL_GLU + CONF_WIDTH),
            halo(HALO_C, CONF_WIDTH, TCOL_GLU),
            halo(HALO_C, CONF_WIDTH, TCOL_GLU + CONF_WIDTH),
            pl.BlockSpec((tm, N_HEADS * V_HEAD), lambda i: (i, 0)),
            pl.BlockSpec((tm, D_MODEL), lambda i: (i, 0)),
        ] + [const(a) for a in weights],
        out_specs=[
            pl.BlockSpec((tm, D_MODEL), lambda i: (i, 0)),
            pl.BlockSpec((tm, LANES), lambda i: (i, 0)),
            pl.BlockSpec((META_ROWS, tm), lambda i: (0, i)),
            pl.BlockSpec((1, LANES), lambda i: (0, 0)),
            pl.BlockSpec(memory_space=pl.ANY),
        ],
        out_shape=(
            jax.ShapeDtypeStruct((TOKENS, D_MODEL), F32),
            jax.ShapeDtypeStruct((TOKENS, LANES), F32),
            jax.ShapeDtypeStruct((META_ROWS, TOKENS), F32),
            jax.ShapeDtypeStruct((1, LANES), F32),
            jax.ShapeDtypeStruct((SORTED_ROWS, LANES), F32),
        ),
        scratch_shapes=[
            pltpu.VMEM((tm + 8, SC_WIDTH), F32),
            pltpu.VMEM((tm + HALO_C, CONF_WIDTH), F32),
            pltpu.VMEM((7, tm + HALO_C - 8, CONF_WIDTH), F32),
            pltpu.VMEM((tm, CONF_WIDTH), BF16),
            pltpu.VMEM((1, LANES), F32),
            pltpu.VMEM((ZERO_ROWS, LANES), F32),
            pltpu.SemaphoreType.DMA((1,)),
        ],
        compiler_params=_cparams(("arbitrary",), VMEM_BIG_MB),
        name="mixer_tail",
    )(proj, proj, proj, tail, tail, tail, tail, att, x, *weights)


def _route(logits):
    lane = lax.broadcasted_iota(jnp.int32, logits.shape, 1)
    lane_f = lane.astype(F32)
    neg = -jnp.inf
    big = float(LANES)
    is_grp = (lane >= ROUTER_GROUP_LANE) & (lane < ROUTER_GROUP_LANE + N_GROUPS)
    glog = jnp.where(is_grp, logits, neg)
    gmax = jnp.max(glog, axis=-1, keepdims=True)
    gidx = jnp.min(jnp.where(glog == gmax, lane_f, big), axis=-1, keepdims=True)
    p_sel = 1.0 / jnp.sum(jnp.exp(glog - gmax), axis=-1, keepdims=True)
    first = (gidx - ROUTER_GROUP_LANE) * EXPERTS_PER_GROUP
    in_grp = (lane_f >= first) & (lane_f < first + EXPERTS_PER_GROUP)
    el = jnp.where(in_grp, logits, neg)
    m1 = jnp.max(el, axis=-1, keepdims=True)
    i1 = jnp.min(jnp.where(el == m1, lane_f, big), axis=-1, keepdims=True)
    el2 = jnp.where(lane_f == i1, neg, el)
    m2 = jnp.max(el2, axis=-1, keepdims=True)
    i2 = jnp.min(jnp.where(el2 == m2, lane_f, big), axis=-1, keepdims=True)
    e2 = jnp.exp(m2 - m1)
    w1 = p_sel / (1.0 + e2)
    w2 = w1 * e2
    return i1, i2, w1, w2


def _route_tile(x, g_ref, wr_ref, br_ref, run_ref, meta_ref, meta_t_ref, cnt_ref):
    tm = x.shape[0]
    h = _rms(x, g_ref[...])
    h_hi = h.astype(BF16)
    h_lo = (h - h_hi.astype(F32)).astype(BF16)
    hi_terms = jnp.dot(h_hi, wr_ref[...], preferred_element_type=F32)
    logits = (hi_terms[:, :LANES] + hi_terms[:, LANES:]
              + jnp.dot(h_lo, wr_ref[:, :LANES], preferred_element_type=F32) + br_ref[...])
    i1, i2, w1, w2 = _route(logits)
    lane = lax.broadcasted_iota(jnp.int32, (tm, LANES), 1)
    lane_f = lane.astype(F32)
    oh1 = lane_f == i1
    oh2 = lane_f == i2
    onehot = jnp.where(oh1, 1.0, 0.0) + jnp.where(oh2, 1.0, 0.0)
    row_id = lax.broadcasted_iota(jnp.int32, (tm, tm), 0)
    col_id = lax.broadcasted_iota(jnp.int32, (tm, tm), 1)
    below = jnp.where(row_id > col_id, 1.0, 0.0).astype(BF16)
    before = run_ref[...] + jnp.dot(below, onehot.astype(BF16), preferred_element_type=F32)
    r1 = jnp.sum(jnp.where(oh1, before, 0.0), axis=-1, keepdims=True)
    r2 = jnp.sum(jnp.where(oh2, before, 0.0), axis=-1, keepdims=True)
    run_ref[...] += jnp.sum(onehot, axis=0, keepdims=True)
    cnt_ref[...] = run_ref[...]
    meta = jnp.zeros((tm, LANES), F32)
    for col, val in enumerate((i1, i2, r1, r2, w1, w2)):
        meta = jnp.where(lane == col, val, meta)
    meta_ref[...] = meta
    meta_t_ref[...] = meta.T[0:META_ROWS, :]


def _dispatch_plan(meta_t, counts):
    e1 = meta_t[META_E1].astype(jnp.int32)
    e2 = meta_t[META_E2].astype(jnp.int32)
    r1 = meta_t[META_R1].astype(jnp.int32)
    r2 = meta_t[META_R2].astype(jnp.int32)
    cnt = counts[0, :N_EXPERTS].astype(jnp.int32)
    tiles = (cnt + TM_EXP - 1) // TM_EXP
    tile_end = jnp.cumsum(tiles)
    first_slot = ((tile_end - tiles) * TM_EXP)[:, None]
    expert = jnp.arange(N_EXPERTS, dtype=jnp.int32)[:, None]
    pos1 = jnp.sum(jnp.where(e1[None, :] == expert, first_slot, 0), axis=0) + r1
    pos2 = jnp.sum(jnp.where(e2[None, :] == expert, first_slot, 0), axis=0) + r2
    n_tiles = tile_end[-1:]
    tile_id = jnp.minimum(jnp.arange(MAX_TILES, dtype=jnp.int32), n_tiles - 1)
    tile_expert = jnp.sum((tile_id[:, None] >= tile_end[None, :]).astype(jnp.int32), axis=1)
    return pos1, pos2, tile_expert, n_tiles


def _dispatch_kernel(pos1_ref, pos2_ref, x_ref, g_ref, xs_in_ref, xs_ref, slab, sem):
    del xs_in_ref
    tm = TM_DISP
    i = pl.program_id(0)
    last = pl.num_programs(0) - 1
    slot = i % 2

    def wait_copies(sl):
        for _ in range(TOP_K):
            pltpu.make_async_copy(slab.at[sl], xs_ref.at[pl.ds(0, tm * SLAB), :], sem.at[sl]).wait()

    @pl.when(i >= 2)
    def _():
        wait_copies(slot)

    h = _rms(x_ref[...], g_ref[...])
    for s in range(SLAB):
        slab[slot, pl.ds(s, tm, stride=SLAB), :] = h[:, s * LANES:(s + 1) * LANES]

    def body(c, carry):
        for u in range(GATHER_UNROLL):
            r = c * GATHER_UNROLL + u
            src = slab.at[slot, pl.ds(pl.multiple_of(r * SLAB, SLAB), SLAB), :]
            for k, pos_ref in enumerate((pos1_ref, pos2_ref)):
                p = pos_ref[i * tm + r]
                pltpu.make_async_copy(src, xs_ref.at[pl.ds(pl.multiple_of(p * SLAB, SLAB), SLAB), :],
                                      sem.at[slot]).start(priority=k % N_DMA_PRIORITIES)
        return carry

    lax.fori_loop(0, tm // GATHER_UNROLL, body, 0)

    @pl.when(i == last)
    def _():
        wait_copies(1 - slot)
        wait_copies(slot)


def _dispatch(pos1, pos2, x, g, xs0):
    tm = TM_DISP
    return pl.pallas_call(
        _dispatch_kernel,
        grid_spec=pltpu.PrefetchScalarGridSpec(
            num_scalar_prefetch=2,
            grid=(TOKENS // tm,),
            in_specs=[
                pl.BlockSpec((tm, D_MODEL), lambda i, p1, p2: (i, 0)),
                pl.BlockSpec((1, D_MODEL), lambda i, p1, p2: (0, 0)),
                pl.BlockSpec(memory_space=pl.ANY),
            ],
            out_specs=pl.BlockSpec(memory_space=pl.ANY),
            scratch_shapes=[pltpu.VMEM((2, tm * SLAB, LANES), F32), pltpu.SemaphoreType.DMA((2,))],
        ),
        out_shape=jax.ShapeDtypeStruct((SORTED_ROWS, LANES), F32),
        input_output_aliases={4: 0},
        compiler_params=_cparams(("arbitrary",), VMEM_MB),
        name="moe_dispatch",
    )(pos1, pos2, x, g, xs0)


def _expert_kernel(te_ref, nt_ref, xs_ref, wg_ref, wu_ref, wd_ref, ys_ref, xbuf, sem):
    del te_ref
    tm = TM_EXP
    i = pl.program_id(0)
    n = nt_ref[0]

    def tile_copy(tile):
        slot = tile % EXPERT_SLOTS
        rows = pl.ds(pl.multiple_of(tile * (tm * SLAB), tm * SLAB), tm * SLAB)
        return pltpu.make_async_copy(xs_ref.at[rows, :], xbuf.at[slot], sem.at[slot])

    @pl.when(i == 0)
    def _():
        for ahead in range(EXPERT_PREFETCH):
            @pl.when(ahead < n)
            def _():
                tile_copy(ahead).start()

    @pl.when(i + EXPERT_PREFETCH < n)
    def _():
        tile_copy(i + EXPERT_PREFETCH).start()

    @pl.when(i < n)
    def _():
        tile_copy(i).wait()
        slot = i % EXPERT_SLOTS
        xt = jnp.concatenate([xbuf[slot, pl.ds(s, tm, stride=SLAB), :].astype(BF16) for s in range(SLAB)], axis=1)
        hg = jnp.dot(xt, wg_ref[...].astype(BF16), preferred_element_type=F32)
        hu = jnp.dot(xt, wu_ref[...].astype(BF16), preferred_element_type=F32)
        hh = (hg * _sigmoid(hg) * hu).astype(BF16)
        y = jnp.dot(hh, wd_ref[...].astype(BF16), preferred_element_type=F32)
        for s in range(SLAB):
            ys_ref[pl.ds(s, tm, stride=SLAB), :] = y[:, s * LANES:(s + 1) * LANES]

    @pl.when(i >= n)
    def _():
        ys_ref[...] = jnp.zeros_like(ys_ref)


def _experts(layer, tile_expert, n_tiles, xs, w_gate, w_up, w_down):
    tm = TM_EXP
    wspec = lambda rows, cols: pl.BlockSpec((None, None, rows, cols), lambda i, te, nt: (layer, te[i], 0, 0))
    return pl.pallas_call(
        _expert_kernel,
        grid_spec=pltpu.PrefetchScalarGridSpec(
            num_scalar_prefetch=2,
            grid=(MAX_TILES,),
            in_specs=[
                pl.BlockSpec(memory_space=pl.ANY),
                wspec(D_MODEL, EXPERT_HIDDEN),
                wspec(D_MODEL, EXPERT_HIDDEN),
                wspec(EXPERT_HIDDEN, D_MODEL),
            ],
            out_specs=pl.BlockSpec((tm * SLAB, LANES), lambda i, te, nt: (i, 0)),
            scratch_shapes=[pltpu.VMEM((EXPERT_SLOTS, tm * SLAB, LANES), F32), pltpu.SemaphoreType.DMA((EXPERT_SLOTS,))],
        ),
        out_shape=jax.ShapeDtypeStruct((SORTED_ROWS, LANES), F32),
        compiler_params=_cparams(("arbitrary",), VMEM_MB),
        name="moe_experts",
    )(tile_expert, n_tiles, xs, w_gate, w_up, w_down)


def _start_slab_gathers(idx_refs, base, n_rows, src_hbm, dst_bufs, sem):
    def body(c, carry):
        for u in range(GATHER_UNROLL):
            r = c * GATHER_UNROLL + u
            for k, (idx_ref, dst) in enumerate(zip(idx_refs, dst_bufs)):
                t = idx_ref[base + r]
                pltpu.make_async_copy(src_hbm.at[pl.ds(pl.multiple_of(t * SLAB, SLAB), SLAB), :],
                                      dst.at[pl.ds(pl.multiple_of(r * SLAB, SLAB), SLAB), :],
                                      sem).start(priority=k % N_DMA_PRIORITIES)
        return carry

    lax.fori_loop(0, n_rows // GATHER_UNROLL, body, 0)


def _wait_slab_gathers(n_rows, src_hbm, dst, sem):
    pltpu.make_async_copy(src_hbm.at[pl.ds(0, n_rows * SLAB), :], dst, sem).wait()


def _combine_ple_kernel(pos1_ref, pos2_ref, x_ref, meta_ref, ys_ref, p_ref, g_ref, wg_ref, wp_ref, fg_ref,
                        out_ref, cbuf, sem, *, final):
    tm = TM_COMB
    i = pl.program_id(0)

    def gather(tile, slot):
        _start_slab_gathers((pos1_ref, pos2_ref), tile * tm, tm, ys_ref, (cbuf.at[slot, 0], cbuf.at[slot, 1]),
                            sem.at[slot])

    @pl.when(i == 0)
    def _():
        gather(0, 0)

    @pl.when(i + 1 < pl.num_programs(0))
    def _():
        gather(i + 1, (i + 1) % 2)

    slot = i % 2
    for k in range(2):
        _wait_slab_gathers(tm, ys_ref, cbuf.at[slot, k], sem.at[slot])
    meta = meta_ref[...]
    w1 = meta[:, META_W1:META_W1 + 1]
    w2 = meta[:, META_W2:META_W2 + 1]
    moe = jnp.concatenate([w1 * cbuf[slot, 0, pl.ds(s, tm, stride=SLAB), :]
                           + w2 * cbuf[slot, 1, pl.ds(s, tm, stride=SLAB), :] for s in range(SLAB)], axis=1)
    x = x_ref[...] + moe
    h = _rms(x, g_ref[...]).astype(BF16)
    gate = _sigmoid(jnp.dot(h, wg_ref[...], preferred_element_type=F32))
    emb = jnp.dot(p_ref[...].astype(BF16), wp_ref[...], preferred_element_type=F32)
    y = x + gate * emb
    if final:
        y = _rms(y, fg_ref[...])
    out_ref[...] = y


def _combine_ple(layer, pos1, pos2, x, meta, ys, p, g, wg, wp, fg, final):
    tm = TM_COMB
    const = lambda i, p1, p2: (0, 0)
    rows = lambda i, p1, p2: (i, 0)
    return pl.pallas_call(
        functools.partial(_combine_ple_kernel, final=final),
        grid_spec=pltpu.PrefetchScalarGridSpec(
            num_scalar_prefetch=2,
            grid=(TOKENS // tm,),
            in_specs=[
                pl.BlockSpec((tm, D_MODEL), rows),
                pl.BlockSpec((tm, LANES), rows),
                pl.BlockSpec(memory_space=pl.ANY),
                pl.BlockSpec((None, tm, PLE_DIM), lambda i, p1, p2: (layer, i, 0)),
                pl.BlockSpec((1, D_MODEL), const),
                pl.BlockSpec((D_MODEL, D_MODEL), const),
                pl.BlockSpec((PLE_DIM, D_MODEL), const),
                pl.BlockSpec((1, D_MODEL), const),
            ],
            out_specs=pl.BlockSpec((tm, D_MODEL), rows),
            scratch_shapes=[pltpu.VMEM((2, 2, tm * SLAB, LANES), F32), pltpu.SemaphoreType.DMA((2,))],
        ),
        out_shape=jax.ShapeDtypeStruct((TOKENS, D_MODEL), F32),
        compiler_params=_cparams(("arbitrary",), VMEM_MB),
        name="moe_combine_ple",
    )(pos1, pos2, x, meta, ys, p, g, wg, wp, fg)


def _in_proj_weights(w_in):
    k_rope = w_in[:, MAIN_COLS:MAIN_COLS + QK_ROPE]
    glu = w_in[:, MAIN_COLS + QK_ROPE:]
    half = QK_ROPE // 2
    zn = jnp.zeros((D_MODEL, QK_NOPE), F32)
    zp = jnp.zeros((D_MODEL, HEAD_PAD - QK_NOPE - QK_ROPE), F32)
    tail = jnp.concatenate([glu, zn, k_rope, zp, zn, k_rope[:, half:], k_rope[:, :half], zp], axis=1)
    return w_in[:, :MAIN_COLS].astype(BF16), tail.astype(BF16)


def _q_weight(w_uq):
    scale = (QK_NOPE + QK_ROPE) ** -0.5 * LOG2_E
    w = (w_uq * scale).reshape(Q_LORA, N_HEADS, QK_NOPE + QK_ROPE)
    zp = jnp.zeros((Q_LORA, N_HEADS, HEAD_PAD - QK_NOPE - QK_ROPE), F32)
    return jnp.concatenate([w, zp], axis=2).reshape(Q_LORA, N_HEADS * HEAD_PAD).astype(BF16)


def _kv_weight(w_ukv):
    w = w_ukv.reshape(KV_LORA, N_HEADS, QK_NOPE + V_HEAD)
    k_nope, v = w[:, :, :QK_NOPE], w[:, :, QK_NOPE:]
    z = jnp.zeros_like(v)
    k_part = jnp.concatenate([k_nope, jnp.zeros_like(k_nope)], axis=2).reshape(KV_LORA, N_HEADS * HEAD_PAD)
    odd = (jnp.arange(N_HEADS) % 2 == 1)[None, :, None]
    v_part = jnp.concatenate([jnp.where(odd, z, v), jnp.where(odd, v, z)], axis=2).reshape(KV_LORA, N_HEADS * HEAD_PAD)
    return jnp.concatenate([k_part, v_part], axis=1).astype(BF16)


def _router_weight(w_rg, b_rg, w_re, b_re):
    pad = LANES - N_EXPERTS - N_GROUPS
    w = jnp.concatenate([w_re, w_rg, jnp.zeros((D_MODEL, pad), F32)], axis=1)
    b = jnp.concatenate([b_re, b_rg, jnp.zeros((pad,), F32)]).reshape(1, LANES)
    w_hi = w.astype(BF16)
    w_lo = (w - w_hi.astype(F32)).astype(BF16)
    return jnp.concatenate([w_hi, w_lo], axis=1), b


def kernel(x, p, positions, ln_mix_g, w_in, conv_a_w, w_out_a, q_norm_g, w_uq, kv_norm_g, w_ukv, w_out_b, conv_c_w, ln_c_g, ln_c_b, w_out_c, w_o, ln_ffn_g, w_route_grp, b_route_grp, w_route_exp, b_route_exp, w_exp_gate, w_exp_up, w_exp_down, ln_ple_g, w_ple_gate, w_ple, final_norm_g):
    c_tab, s_tab = _rope_tables(positions)
    xf = x.reshape(TOKENS, D_MODEL)
    pf = p.reshape(DEPTH, TOKENS, PLE_DIM)
    row = lambda a: a.reshape(1, -1)
    for i in range(DEPTH):
        proj, tail = _inproj(xf, row(ln_mix_g[i]), *_in_proj_weights(w_in[i]))
        q, k, v = _qkv(proj, tail, c_tab, s_tab, row(q_norm_g[i]), row(kv_norm_g[i]), _q_weight(w_uq[i]), _kv_weight(w_ukv[i]))
        att = _attention(q, k, v)
        wr, br = _router_weight(w_route_grp[i], b_route_grp[i], w_route_exp[i], b_route_exp[i])
        xf, meta, meta_t, counts, xs0 = _mixer_tail(
            proj, tail, att, xf, conv_a_w[i], w_out_a[i].astype(BF16), conv_c_w[i], row(ln_c_g[i]), row(ln_c_b[i]),
            w_out_c[i].astype(BF16), w_out_b[i].astype(BF16), w_o[i].astype(BF16), row(ln_ffn_g[i]), wr, br)
        pos1, pos2, tile_expert, n_tiles = _dispatch_plan(meta_t, counts)
        xs = _dispatch(pos1, pos2, xf, row(ln_ffn_g[i]), xs0)
        ys = _experts(i, tile_expert, n_tiles, xs, w_exp_gate, w_exp_up, w_exp_down)
        xf = _combine_ple(i, pos1, pos2, xf, meta, ys, pf, row(ln_ple_g[i]), w_ple_gate[i].astype(BF16),
                          w_ple[i].astype(BF16), row(final_norm_g), final=(i == DEPTH - 1))
    return xf.reshape(BATCH, SEQ, D_MODEL)
```

```python
import functools

import jax
import jax.numpy as jnp
from jax import lax
from jax.experimental import pallas as pl
from jax.experimental.pallas import tpu as pltpu

D_MODEL = 1024
BATCH = 8
SEQ = 2048
DEPTH = 2
TOKENS = BATCH * SEQ
PLE_DIM = 256
SC_WIDTH = 512
SC_KERNEL = 3
N_HEADS = 8
QK_NOPE = 64
QK_ROPE = 32
V_HEAD = 64
Q_LORA = 768
KV_LORA = 256
ROPE_THETA = 10000.0
CONF_WIDTH = 512
CONF_KERNEL = 31
N_GROUPS = 4
EXPERTS_PER_GROUP = 8
N_EXPERTS = N_GROUPS * EXPERTS_PER_GROUP
EXPERT_HIDDEN = 256
EPS = 1e-6
LOG2_E = 1.4426950408889634

LANES = 128
HEAD_PAD = 128
F32 = jnp.float32
BF16 = jnp.bfloat16

COL_GATES = 0
COL_SC = 3 * D_MODEL
COL_QLAT = COL_SC + 3 * SC_WIDTH
COL_KVLAT = COL_QLAT + Q_LORA
MAIN_COLS = COL_KVLAT + KV_LORA
TCOL_GLU = 0
TCOL_KR = 2 * CONF_WIDTH
TCOL_KRSW = TCOL_KR + HEAD_PAD
TAIL_COLS = TCOL_KRSW + HEAD_PAD

ROUTER_GROUP_LANE = N_EXPERTS

MXU_TILE = 256
VMEM_MB = 48
VMEM_BIG_MB = 56

TM_INPROJ = 1024
TN_INPROJ = MAIN_COLS // 2
TN_SPLIT = 6 * MXU_TILE
TM_QKV = 512
T_ATTN = 512
HEADS_PER_STEP = 8
TM_MIX = 512
HALO_C = 32
HALO_A = 16
TM_EXP = 512
EXPERT_PREFETCH = 2
EXPERT_SLOTS = EXPERT_PREFETCH + 1
TM_DISP = 512
TM_COMB = 256
TOP_K = 2
MAX_TILES = TOKENS * TOP_K // TM_EXP + N_EXPERTS
SLAB = D_MODEL // LANES
GATHER_UNROLL = 16
N_DMA_PRIORITIES = 2
META_E1, META_E2, META_R1, META_R2, META_W1, META_W2 = range(6)
META_ROWS = 8
SORTED_ROWS = MAX_TILES * TM_EXP * SLAB
ZERO_COPIES = 6
ZERO_ROWS = SORTED_ROWS // (TOKENS // TM_MIX) // ZERO_COPIES


def _cparams(semantics, vmem_mb):
    return pltpu.CompilerParams(dimension_semantics=semantics, vmem_limit_bytes=vmem_mb * 1024 * 1024)


def _sigmoid(x):
    return 1.0 / (1.0 + jnp.exp2(x * -LOG2_E))


def _rms(x, g):
    return x * lax.rsqrt(jnp.mean(x * x, axis=-1, keepdims=True) + EPS) * g


def _place(x, onehot):
    x1 = x.astype(BF16)
    r1 = x - x1.astype(F32)
    x2 = r1.astype(BF16)
    x3 = (r1 - x2.astype(F32)).astype(BF16)
    return (jnp.dot(x1, onehot, preferred_element_type=F32) + jnp.dot(x2, onehot, preferred_element_type=F32)
            + jnp.dot(x3, onehot, preferred_element_type=F32))


def _rope_kernel(pos_ref, freq_ref, c_ref, s_ref):
    half = QK_ROPE // 2
    per_row = LANES // half
    rows = TOKENS // per_row
    ang = pos_ref[...].astype(F32) * freq_ref[...]
    cos = jnp.cos(ang)
    sin = jnp.sin(ang)
    src = lax.broadcasted_iota(jnp.int32, (LANES, LANES), 0)
    dst = lax.broadcasted_iota(jnp.int32, (LANES, LANES), 1)
    lane = lax.broadcasted_iota(jnp.int32, (1, LANES), 1)
    ones_nope = jnp.where(lane < QK_NOPE, 1.0, 0.0)
    sign = jnp.where(lane < QK_NOPE + half, -1.0, 1.0)
    for j in range(per_row):
        f = src - half * j
        hit = (dst == QK_NOPE + f) | (dst == QK_NOPE + half + f)
        onehot = jnp.where((f >= 0) & (f < half) & hit, 1.0, 0.0).astype(BF16)
        c_ref[pl.ds(j, rows, stride=per_row), :] = _place(cos, onehot) + ones_nope
        s_ref[pl.ds(j, rows, stride=per_row), :] = _place(sin, onehot) * sign


def _rope_tables(positions):
    half = QK_ROPE // 2
    inv_freq = ROPE_THETA ** (-jnp.arange(0, QK_ROPE, 2, dtype=F32) / QK_ROPE)
    rows = TOKENS * half // LANES
    pos_rep = jnp.broadcast_to(positions.reshape(TOKENS, 1), (TOKENS, half)).reshape(rows, LANES)
    freq = jnp.tile(inv_freq, LANES // half).reshape(1, LANES)
    return pl.pallas_call(
        _rope_kernel,
        out_shape=(jax.ShapeDtypeStruct((TOKENS, HEAD_PAD), F32),) * 2,
        compiler_params=_cparams(None, VMEM_MB),
        name="rope_tables",
    )(pos_rep, freq)


def _inproj_kernel(x_ref, g_ref, wm_ref, wt_ref, om_ref, ot_ref, h_ref):
    j = pl.program_id(1)
    n_main = MAIN_COLS // TN_INPROJ

    @pl.when(j == 0)
    def _():
        h_ref[...] = _rms(x_ref[...], g_ref[...]).astype(BF16)

    @pl.when(j < n_main)
    def _():
        for lo, hi in ((0, TN_SPLIT), (TN_SPLIT, TN_INPROJ)):
            om_ref[:, lo:hi] = jnp.dot(h_ref[...], wm_ref[:, lo:hi], preferred_element_type=F32).astype(BF16)

    @pl.when(j == n_main)
    def _():
        ot_ref[...] = jnp.dot(h_ref[...], wt_ref[...], preferred_element_type=F32).astype(BF16)


def _inproj(x, g, w_main, w_tail):
    tm, tn = TM_INPROJ, TN_INPROJ
    n_main = MAIN_COLS // tn
    main_col = lambda i, j: jnp.minimum(j, n_main - 1)
    return pl.pallas_call(
        _inproj_kernel,
        grid=(TOKENS // tm, n_main + 1),
        in_specs=[
            pl.BlockSpec((tm, D_MODEL), lambda i, j: (i, 0)),
            pl.BlockSpec((1, D_MODEL), lambda i, j: (0, 0)),
            pl.BlockSpec((D_MODEL, tn), lambda i, j: (0, main_col(i, j))),
            pl.BlockSpec((D_MODEL, TAIL_COLS), lambda i, j: (0, 0)),
        ],
        out_specs=[
            pl.BlockSpec((tm, tn), lambda i, j: (i, main_col(i, j))),
            pl.BlockSpec((tm, TAIL_COLS), lambda i, j: (i, 0)),
        ],
        out_shape=(jax.ShapeDtypeStruct((TOKENS, MAIN_COLS), BF16), jax.ShapeDtypeStruct((TOKENS, TAIL_COLS), BF16)),
        scratch_shapes=[pltpu.VMEM((tm, D_MODEL), BF16)],
        compiler_params=_cparams(("parallel", "arbitrary"), VMEM_BIG_MB),
        name="in_proj",
    )(x, g, w_main, w_tail)


def _qkv_kernel(ql_ref, kvl_ref, kr_ref, krsw_ref, c_ref, s_ref, qg_ref, kvg_ref, wq_ref, wkv_ref,
                q_out, k_out, v_out):
    c = c_ref[...]
    s = s_ref[...]
    width = N_HEADS * HEAD_PAD
    half = QK_ROPE // 2
    low_half = lax.broadcasted_iota(jnp.int32, c.shape, 1) < QK_NOPE + half

    def swap_halves(x):
        return jnp.where(low_half, pltpu.roll(x, HEAD_PAD - half, axis=1), pltpu.roll(x, half, axis=1))

    qn = _rms(ql_ref[...].astype(F32), qg_ref[...]).astype(BF16)
    qq = jnp.dot(qn, wq_ref[...], preferred_element_type=F32)
    for h in range(N_HEADS):
        lo, hi = h * HEAD_PAD, (h + 1) * HEAD_PAD
        q_out[:, lo:hi] = (qq[:, lo:hi] * c + swap_halves(qq[:, lo:hi]) * s).astype(BF16)
    kvn = _rms(kvl_ref[...].astype(F32), kvg_ref[...]).astype(BF16)
    kk = jnp.dot(kvn, wkv_ref[...], preferred_element_type=F32)
    kr = kr_ref[...].astype(F32) * c + krsw_ref[...].astype(F32) * s
    for h in range(N_HEADS):
        lo, hi = h * HEAD_PAD, (h + 1) * HEAD_PAD
        k_out[:, lo:hi] = (kk[:, lo:hi] + kr).astype(BF16)
    v_out[...] = kk[:, width:].astype(BF16)


def _qkv(proj, tail, c_tab, s_tab, qg, kvg, wq, wkv):
    tm = TM_QKV
    width = N_HEADS * HEAD_PAD
    row = lambda blk: (lambda i: (i, blk))
    const = lambda i: (0, 0)
    return pl.pallas_call(
        _qkv_kernel,
        grid=(TOKENS // tm,),
        in_specs=[
            pl.BlockSpec((tm, Q_LORA), row(COL_QLAT // Q_LORA)),
            pl.BlockSpec((tm, KV_LORA), row(COL_KVLAT // KV_LORA)),
            pl.BlockSpec((tm, HEAD_PAD), row(TCOL_KR // HEAD_PAD)),
            pl.BlockSpec((tm, HEAD_PAD), row(TCOL_KRSW // HEAD_PAD)),
            pl.BlockSpec((tm, HEAD_PAD), row(0)),
            pl.BlockSpec((tm, HEAD_PAD), row(0)),
            pl.BlockSpec((1, Q_LORA), const),
            pl.BlockSpec((1, KV_LORA), const),
            pl.BlockSpec((Q_LORA, width), const),
            pl.BlockSpec((KV_LORA, 2 * width), const),
        ],
        out_specs=[pl.BlockSpec((tm, width), row(0))] * 3,
        out_shape=(jax.ShapeDtypeStruct((TOKENS, width), BF16),) * 3,
        compiler_params=_cparams(("parallel",), VMEM_MB),
        name="qkv_prep",
    )(proj, proj, tail, tail, c_tab, s_tab, qg, kvg, wq, wkv)


def _attn_kernel(q_ref, k_ref, v_ref, o_ref, *state):
    t = T_ATTN
    nh = HEADS_PER_STEP
    qi = pl.program_id(2)
    nt = (((1,), (1,)), ((), ()))
    m_sc, l_sc, acc_sc = state[0:nh], state[nh:2 * nh], state[2 * nh:3 * nh]
    for h in range(nh):
        m_sc[h][...] = jnp.full((t, LANES), -jnp.inf, F32)
        l_sc[h][...] = jnp.zeros((t, LANES), F32)
        acc_sc[h][...] = jnp.zeros((t, LANES), F32)

    def block(j, r0, nr, c0, nc, masked):
        start = pl.multiple_of(j * t + c0, nc)
        rows = slice(r0, r0 + nr)
        for h in range(nh):
            lo, hi = h * HEAD_PAD, (h + 1) * HEAD_PAD
            s = lax.dot_general(q_ref[rows, lo:hi], k_ref[pl.ds(start, nc), lo:hi], nt, preferred_element_type=F32)
            if masked:
                row_id = r0 + lax.broadcasted_iota(jnp.int32, (nr, nc), 0)
                col_id = c0 + lax.broadcasted_iota(jnp.int32, (nr, nc), 1)
                s = jnp.where(row_id >= col_id, s, -jnp.inf)
            blocks = [s[:, c * LANES:(c + 1) * LANES] for c in range(nc // LANES)]
            bmax = functools.reduce(jnp.maximum, blocks)
            m_old = m_sc[h][rows, :]
            m_new = jnp.maximum(m_old, jnp.max(bmax, axis=-1, keepdims=True))
            alpha = jnp.exp2(m_old - m_new)
            ps = [jnp.exp2(b - m_new) for b in blocks]
            p = jnp.concatenate(ps, axis=1).astype(BF16)
            l_sc[h][rows, :] = alpha * l_sc[h][rows, :] + functools.reduce(jnp.add, ps)
            acc_sc[h][rows, :] = alpha * acc_sc[h][rows, :] + jnp.dot(p, v_ref[pl.ds(start, nc), lo:hi],
                                                                       preferred_element_type=F32)
            m_sc[h][rows, :] = m_new

    def body(j, carry):
        block(j, 0, t, 0, t, False)
        return carry

    lax.fori_loop(0, qi, body, 0)
    half = t // 2
    block(qi, 0, t, 0, half, True)
    block(qi, half, half, half, half, True)
    out = [acc_sc[h][...] / jnp.sum(l_sc[h][...], axis=-1, keepdims=True) for h in range(nh)]
    o_ref[...] = jnp.concatenate([out[h] + out[h + 1] for h in range(0, nh, 2)], axis=1).astype(BF16)


def _attention(q, k, v):
    t = T_ATTN
    nq = SEQ // t
    nh = HEADS_PER_STEP
    return pl.pallas_call(
        _attn_kernel,
        grid=(BATCH, N_HEADS // nh, nq),
        in_specs=[
            pl.BlockSpec((t, nh * HEAD_PAD), lambda b, hg, i: (b * nq + i, hg)),
            pl.BlockSpec((SEQ, nh * HEAD_PAD), lambda b, hg, i: (b, hg)),
            pl.BlockSpec((SEQ, nh * HEAD_PAD), lambda b, hg, i: (b, hg)),
        ],
        out_specs=pl.BlockSpec((t, nh * V_HEAD), lambda b, hg, i: (b * nq + i, hg)),
        out_shape=jax.ShapeDtypeStruct((TOKENS, N_HEADS * V_HEAD), BF16),
        scratch_shapes=[pltpu.VMEM((t, LANES), F32)] * (3 * nh),
        compiler_params=_cparams(("parallel", "parallel", "arbitrary"), VMEM_MB),
        name="mla_attention",
    )(q, k, v)


def _mixer_tail_kernel(gates_ref, sc_ref, sch_ref, gv_ref, gg_ref, gvh_ref, ggh_ref, att_ref, x_ref,
                       cwa_ref, woa_ref, cwc_ref, lng_ref, lnb_ref, woc_ref, wob_ref, wo_ref, gf_ref, wr_ref, br_ref,
                       out_ref, meta_ref, meta_t_ref, cnt_ref, xs0_ref, cbuf, ubuf, shifted, below, run_ref, zbuf, zsem):
    tm = TM_MIX
    step = pl.program_id(0)

    @pl.when(step == 0)
    def _():
        run_ref[...] = jnp.zeros_like(run_ref)
        zbuf[...] = jnp.zeros_like(zbuf)
        row_id = lax.broadcasted_iota(jnp.int32, (tm, tm), 0)
        col_id = lax.broadcasted_iota(jnp.int32, (tm, tm), 1)
        below[...] = jnp.where(row_id > col_id, 1.0, 0.0).astype(BF16)

    def zero_copy(c):
        rows = pl.ds(pl.multiple_of((step * ZERO_COPIES + c) * ZERO_ROWS, ZERO_ROWS), ZERO_ROWS)
        return pltpu.make_async_copy(zbuf, xs0_ref.at[rows, :], zsem.at[0])

    for c in range(ZERO_COPIES):
        zero_copy(c).start()
    has_past = (pl.program_id(0) % (SEQ // tm)) != 0

    sc = sc_ref[...]
    sc_b = sc[:, 0:SC_WIDTH].astype(F32)
    cbuf[8:8 + tm, :] = sc[:, SC_WIDTH:2 * SC_WIDTH].astype(F32) * sc[:, 2 * SC_WIDTH:].astype(F32)
    sch = sch_ref[...].astype(F32)[HALO_A - 8:HALO_A, :]
    cbuf[0:8, :] = jnp.where(has_past, sch[:, SC_WIDTH:2 * SC_WIDTH] * sch[:, 2 * SC_WIDTH:], 0.0)
    conv_a = cwa_ref[0:1, :] * cbuf[6:6 + tm, :]
    for t in range(1, SC_KERNEL):
        conv_a = conv_a + cwa_ref[t:t + 1, :] * cbuf[6 + t:6 + t + tm, :]
    y_a = jnp.dot((sc_b * conv_a).astype(BF16), woa_ref[...], preferred_element_type=F32)

    ubuf[HALO_C:HALO_C + tm, :] = gv_ref[...].astype(F32) * _sigmoid(gg_ref[...].astype(F32))
    ubuf[0:HALO_C, :] = jnp.where(has_past, gvh_ref[...].astype(F32) * _sigmoid(ggh_ref[...].astype(F32)), 0.0)
    rows = tm + HALO_C - 8
    u_all = ubuf[...]
    for b in range(1, 8):
        shifted[b - 1, 0:rows, :] = pltpu.roll(u_all, tm + HALO_C - b, axis=0)[0:rows, :]
    base = HALO_C - (CONF_KERNEL - 1)
    acc = None
    for t in range(CONF_KERNEL):
        off = base + t
        a0 = off - off % 8
        src = ubuf[a0:a0 + tm, :] if off % 8 == 0 else shifted[off % 8 - 1, a0:a0 + tm, :]
        term = cwc_ref[t:t + 1, :] * src
        acc = term if acc is None else acc + term
    mu = jnp.mean(acc, axis=-1, keepdims=True)
    xc = acc - mu
    var = jnp.mean(xc * xc, axis=-1, keepdims=True)
    y = xc * lax.rsqrt(var + EPS) * lng_ref[...] + lnb_ref[...]
    y_c = jnp.dot((y * _sigmoid(y)).astype(BF16), woc_ref[...], preferred_element_type=F32)

    y_b = jnp.dot(att_ref[...], wob_ref[...], preferred_element_type=F32)

    g = gates_ref[...]
    merged = (_sigmoid(g[:, 0:D_MODEL].astype(F32)) * y_a
              + _sigmoid(g[:, D_MODEL:2 * D_MODEL].astype(F32)) * y_b
              + _sigmoid(g[:, 2 * D_MODEL:].astype(F32)) * y_c)
    x_new = x_ref[...] + jnp.dot(merged.astype(BF16), wo_ref[...], preferred_element_type=F32)
    out_ref[...] = x_new
    _route_tile(x_new, gf_ref, wr_ref, br_ref, below, run_ref, meta_ref, meta_t_ref, cnt_ref)
    for c in range(ZERO_COPIES):
        zero_copy(c).wait()


def _mixer_tail(proj, tail, att, x, cwa, woa, cwc, lng, lnb, woc, wob, wo, gf, wr, br):
    tm = TM_MIX
    row = lambda width, col: pl.BlockSpec((tm, width), lambda i: (i, col // width))
    halo = lambda rows, width, col: pl.BlockSpec(
        (rows, width), lambda i: (jnp.maximum(i * (tm // rows) - 1, 0), col // width))
    const = lambda a: pl.BlockSpec(a.shape, lambda i: (0,) * a.ndim)
    weights = (cwa, woa, cwc, lng, lnb, woc, wob, wo, gf, wr, br)
    return pl.pallas_call(
        _mixer_tail_kernel,
        grid=(TOKENS // tm,),
        in_specs=[
            row(3 * D_MODEL, COL_GATES),
            row(3 * SC_WIDTH, COL_SC),
            halo(HALO_A, 3 * SC_WIDTH, COL_SC),
            row(CONF_WIDTH, TCOL_GLU),
            row(CONF_WIDTH, TCOL_GLU + CONF_WIDTH),
            halo(HALO_C, CONF_WIDTH, TCOL_GLU),
            halo(HALO_C, CONF_WIDTH, TCOL_GLU + CONF_WIDTH),
            pl.BlockSpec((tm, N_HEADS * V_HEAD), lambda i: (i, 0)),
            pl.BlockSpec((tm, D_MODEL), lambda i: (i, 0)),
        ] + [const(a) for a in weights],
        out_specs=[
            pl.BlockSpec((tm, D_MODEL), lambda i: (i, 0)),
            pl.BlockSpec((tm, LANES), lambda i: (i, 0)),
            pl.BlockSpec((META_ROWS, tm), lambda i: (0, i)),
            pl.BlockSpec((1, LANES), lambda i: (0, 0)),
            pl.BlockSpec(memory_space=pl.ANY),
        ],
        out_shape=(
            jax.ShapeDtypeStruct((TOKENS, D_MODEL), F32),
            jax.ShapeDtypeStruct((TOKENS, LANES), F32),
            jax.ShapeDtypeStruct((META_ROWS, TOKENS), F32),
            jax.ShapeDtypeStruct((1, LANES), F32),
            jax.ShapeDtypeStruct((SORTED_ROWS, LANES), F32),
        ),
        scratch_shapes=[
            pltpu.VMEM((tm + 8, SC_WIDTH), F32),
            pltpu.VMEM((tm + HALO_C, CONF_WIDTH), F32),
            pltpu.VMEM((7, tm + HALO_C - 8, CONF_WIDTH), F32),
            pltpu.VMEM((tm, tm), BF16),
            pltpu.VMEM((1, LANES), F32),
            pltpu.VMEM((ZERO_ROWS, LANES), F32),
            pltpu.SemaphoreType.DMA((1,)),
        ],
        compiler_params=_cparams(("arbitrary",), VMEM_BIG_MB),
        name="mixer_tail",
    )(proj, proj, proj, tail, tail, tail, tail, att, x, *weights)


def _route(logits):
    lane = lax.broadcasted_iota(jnp.int32, logits.shape, 1)
    lane_f = lane.astype(F32)
    neg = -jnp.inf
    big = float(LANES)
    is_grp = (lane >= ROUTER_GROUP_LANE) & (lane < ROUTER_GROUP_LANE + N_GROUPS)
    glog = jnp.where(is_grp, logits, neg)
    gmax = jnp.max(glog, axis=-1, keepdims=True)
    gidx = jnp.min(jnp.where(glog == gmax, lane_f, big), axis=-1, keepdims=True)
    p_sel = 1.0 / jnp.sum(jnp.exp(glog - gmax), axis=-1, keepdims=True)
    first = (gidx - ROUTER_GROUP_LANE) * EXPERTS_PER_GROUP
    in_grp = (lane_f >= first) & (lane_f < first + EXPERTS_PER_GROUP)
    el = jnp.where(in_grp, logits, neg)
    m1 = jnp.max(el, axis=-1, keepdims=True)
    i1 = jnp.min(jnp.where(el == m1, lane_f, big), axis=-1, keepdims=True)
    el2 = jnp.where(lane_f == i1, neg, el)
    m2 = jnp.max(el2, axis=-1, keepdims=True)
    i2 = jnp.min(jnp.where(el2 == m2, lane_f, big), axis=-1, keepdims=True)
    e2 = jnp.exp(m2 - m1)
    w1 = p_sel / (1.0 + e2)
    w2 = w1 * e2
    return i1, i2, w1, w2


def _route_tile(x, g_ref, wr_ref, br_ref, below_ref, run_ref, meta_ref, meta_t_ref, cnt_ref):
    tm = x.shape[0]
    h = _rms(x, g_ref[...])
    h_hi = h.astype(BF16)
    h_lo = (h - h_hi.astype(F32)).astype(BF16)
    hi_terms = jnp.dot(h_hi, wr_ref[...], preferred_element_type=F32)
    logits = (hi_terms[:, :LANES] + hi_terms[:, LANES:]
              + jnp.dot(h_lo, wr_ref[:, :LANES], preferred_element_type=F32) + br_ref[...])
    i1, i2, w1, w2 = _route(logits)
    lane = lax.broadcasted_iota(jnp.int32, (tm, LANES), 1)
    lane_f = lane.astype(F32)
    oh1 = lane_f == i1
    oh2 = lane_f == i2
    onehot = jnp.where(oh1, 1.0, 0.0) + jnp.where(oh2, 1.0, 0.0)
    before = run_ref[...] + jnp.dot(below_ref[...], onehot.astype(BF16), preferred_element_type=F32)
    r1 = jnp.sum(jnp.where(oh1, before, 0.0), axis=-1, keepdims=True)
    r2 = jnp.sum(jnp.where(oh2, before, 0.0), axis=-1, keepdims=True)
    run_ref[...] += jnp.sum(onehot, axis=0, keepdims=True)
    cnt_ref[...] = run_ref[...]
    meta = jnp.zeros((tm, LANES), F32)
    for col, val in enumerate((i1, i2, r1, r2, w1, w2)):
        meta = jnp.where(lane == col, val, meta)
    meta_ref[...] = meta
    meta_t_ref[...] = meta.T[0:META_ROWS, :]


def _dispatch_plan(meta_t, counts):
    e1 = meta_t[META_E1].astype(jnp.int32)
    e2 = meta_t[META_E2].astype(jnp.int32)
    r1 = meta_t[META_R1].astype(jnp.int32)
    r2 = meta_t[META_R2].astype(jnp.int32)
    cnt = counts[0, :N_EXPERTS].astype(jnp.int32)
    tiles = (cnt + TM_EXP - 1) // TM_EXP
    tile_end = jnp.cumsum(tiles)
    first_slot = ((tile_end - tiles) * TM_EXP)[:, None]
    expert = jnp.arange(N_EXPERTS, dtype=jnp.int32)[:, None]
    pos1 = jnp.sum(jnp.where(e1[None, :] == expert, first_slot, 0), axis=0) + r1
    pos2 = jnp.sum(jnp.where(e2[None, :] == expert, first_slot, 0), axis=0) + r2
    n_tiles = tile_end[-1:]
    tile_id = jnp.minimum(jnp.arange(MAX_TILES, dtype=jnp.int32), n_tiles - 1)
    tile_expert = jnp.sum((tile_id[:, None] >= tile_end[None, :]).astype(jnp.int32), axis=1)
    return pos1, pos2, tile_expert, n_tiles


def _dispatch_kernel(pos1_ref, pos2_ref, x_ref, g_ref, xs_in_ref, xs_ref, slab, sem):
    del xs_in_ref
    tm = TM_DISP
    i = pl.program_id(0)
    last = pl.num_programs(0) - 1
    slot = i % 2

    def wait_copies(sl):
        for _ in range(TOP_K):
            pltpu.make_async_copy(slab.at[sl], xs_ref.at[pl.ds(0, tm * SLAB), :], sem.at[sl]).wait()

    @pl.when(i >= 2)
    def _():
        wait_copies(slot)

    h = _rms(x_ref[...], g_ref[...])
    for s in range(SLAB):
        slab[slot, pl.ds(s, tm, stride=SLAB), :] = h[:, s * LANES:(s + 1) * LANES]

    def body(c, carry):
        for u in range(GATHER_UNROLL):
            r = c * GATHER_UNROLL + u
            src = slab.at[slot, pl.ds(pl.multiple_of(r * SLAB, SLAB), SLAB), :]
            for k, pos_ref in enumerate((pos1_ref, pos2_ref)):
                p = pos_ref[i * tm + r]
                pltpu.make_async_copy(src, xs_ref.at[pl.ds(pl.multiple_of(p * SLAB, SLAB), SLAB), :],
                                      sem.at[slot]).start(priority=k % N_DMA_PRIORITIES)
        return carry

    lax.fori_loop(0, tm // GATHER_UNROLL, body, 0)

    @pl.when(i == last)
    def _():
        wait_copies(1 - slot)
        wait_copies(slot)


def _dispatch(pos1, pos2, x, g, xs0):
    tm = TM_DISP
    return pl.pallas_call(
        _dispatch_kernel,
        grid_spec=pltpu.PrefetchScalarGridSpec(
            num_scalar_prefetch=2,
            grid=(TOKENS // tm,),
            in_specs=[
                pl.BlockSpec((tm, D_MODEL), lambda i, p1, p2: (i, 0)),
                pl.BlockSpec((1, D_MODEL), lambda i, p1, p2: (0, 0)),
                pl.BlockSpec(memory_space=pl.ANY),
            ],
            out_specs=pl.BlockSpec(memory_space=pl.ANY),
            scratch_shapes=[pltpu.VMEM((2, tm * SLAB, LANES), F32), pltpu.SemaphoreType.DMA((2,))],
        ),
        out_shape=jax.ShapeDtypeStruct((SORTED_ROWS, LANES), F32),
        input_output_aliases={4: 0},
        compiler_params=_cparams(("arbitrary",), VMEM_MB),
        name="moe_dispatch",
    )(pos1, pos2, x, g, xs0)


def _expert_kernel(te_ref, nt_ref, xs_ref, wg_ref, wu_ref, wd_ref, ys_ref, xbuf, sem):
    del te_ref
    tm = TM_EXP
    i = pl.program_id(0)
    n = nt_ref[0]

    def tile_copy(tile):
        slot = tile % EXPERT_SLOTS
        rows = pl.ds(pl.multiple_of(tile * (tm * SLAB), tm * SLAB), tm * SLAB)
        return pltpu.make_async_copy(xs_ref.at[rows, :], xbuf.at[slot], sem.at[slot])

    @pl.when(i == 0)
    def _():
        for ahead in range(EXPERT_PREFETCH):
            @pl.when(ahead < n)
            def _():
                tile_copy(ahead).start()

    @pl.when(i + EXPERT_PREFETCH < n)
    def _():
        tile_copy(i + EXPERT_PREFETCH).start()

    @pl.when(i < n)
    def _():
        tile_copy(i).wait()
        slot = i % EXPERT_SLOTS
        xt = jnp.concatenate([xbuf[slot, pl.ds(s, tm, stride=SLAB), :].astype(BF16) for s in range(SLAB)], axis=1)
        hg = jnp.dot(xt, wg_ref[...].astype(BF16), preferred_element_type=F32)
        hu = jnp.dot(xt, wu_ref[...].astype(BF16), preferred_element_type=F32)
        hh = (hg * _sigmoid(hg) * hu).astype(BF16)
        y = jnp.dot(hh, wd_ref[...].astype(BF16), preferred_element_type=F32)
        for s in range(SLAB):
            ys_ref[pl.ds(s, tm, stride=SLAB), :] = y[:, s * LANES:(s + 1) * LANES]

    @pl.when(i >= n)
    def _():
        ys_ref[...] = jnp.zeros_like(ys_ref)


def _experts(layer, tile_expert, n_tiles, xs, w_gate, w_up, w_down):
    tm = TM_EXP
    wspec = lambda rows, cols: pl.BlockSpec((None, None, rows, cols), lambda i, te, nt: (layer, te[i], 0, 0))
    return pl.pallas_call(
        _expert_kernel,
        grid_spec=pltpu.PrefetchScalarGridSpec(
            num_scalar_prefetch=2,
            grid=(MAX_TILES,),
            in_specs=[
                pl.BlockSpec(memory_space=pl.ANY),
                wspec(D_MODEL, EXPERT_HIDDEN),
                wspec(D_MODEL, EXPERT_HIDDEN),
                wspec(EXPERT_HIDDEN, D_MODEL),
            ],
            out_specs=pl.BlockSpec((tm * SLAB, LANES), lambda i, te, nt: (i, 0)),
            scratch_shapes=[pltpu.VMEM((EXPERT_SLOTS, tm * SLAB, LANES), F32), pltpu.SemaphoreType.DMA((EXPERT_SLOTS,))],
        ),
        out_shape=jax.ShapeDtypeStruct((SORTED_ROWS, LANES), F32),
        compiler_params=_cparams(("arbitrary",), VMEM_MB),
        name="moe_experts",
    )(tile_expert, n_tiles, xs, w_gate, w_up, w_down)


def _start_slab_gathers(idx_refs, base, n_rows, src_hbm, dst_bufs, sem):
    def body(c, carry):
        for u in range(GATHER_UNROLL):
            r = c * GATHER_UNROLL + u
            for k, (idx_ref, dst) in enumerate(zip(idx_refs, dst_bufs)):
                t = idx_ref[base + r]
                pltpu.make_async_copy(src_hbm.at[pl.ds(pl.multiple_of(t * SLAB, SLAB), SLAB), :],
                                      dst.at[pl.ds(pl.multiple_of(r * SLAB, SLAB), SLAB), :],
                                      sem).start(priority=k % N_DMA_PRIORITIES)
        return carry

    lax.fori_loop(0, n_rows // GATHER_UNROLL, body, 0)


def _wait_slab_gathers(n_rows, src_hbm, dst, sem):
    pltpu.make_async_copy(src_hbm.at[pl.ds(0, n_rows * SLAB), :], dst, sem).wait()


def _combine_ple_kernel(pos1_ref, pos2_ref, x_ref, meta_ref, ys_ref, p_ref, g_ref, wg_ref, wp_ref, fg_ref,
                        out_ref, cbuf, sem, *, final):
    tm = TM_COMB
    i = pl.program_id(0)

    def gather(tile, slot):
        _start_slab_gathers((pos1_ref, pos2_ref), tile * tm, tm, ys_ref, (cbuf.at[slot, 0], cbuf.at[slot, 1]),
                            sem.at[slot])

    @pl.when(i == 0)
    def _():
        gather(0, 0)

    @pl.when(i + 1 < pl.num_programs(0))
    def _():
        gather(i + 1, (i + 1) % 2)

    slot = i % 2
    for k in range(2):
        _wait_slab_gathers(tm, ys_ref, cbuf.at[slot, k], sem.at[slot])
    meta = meta_ref[...]
    w1 = meta[:, META_W1:META_W1 + 1]
    w2 = meta[:, META_W2:META_W2 + 1]
    moe = jnp.concatenate([w1 * cbuf[slot, 0, pl.ds(s, tm, stride=SLAB), :]
                           + w2 * cbuf[slot, 1, pl.ds(s, tm, stride=SLAB), :] for s in range(SLAB)], axis=1)
    x = x_ref[...] + moe
    h = _rms(x, g_ref[...]).astype(BF16)
    gate = _sigmoid(jnp.dot(h, wg_ref[...], preferred_element_type=F32))
    emb = jnp.dot(p_ref[...].astype(BF16), wp_ref[...], preferred_element_type=F32)
    y = x + gate * emb
    if final:
        y = _rms(y, fg_ref[...])
    out_ref[...] = y


def _combine_ple(layer, pos1, pos2, x, meta, ys, p, g, wg, wp, fg, final):
    tm = TM_COMB
    const = lambda i, p1, p2: (0, 0)
    rows = lambda i, p1, p2: (i, 0)
    return pl.pallas_call(
        functools.partial(_combine_ple_kernel, final=final),
        grid_spec=pltpu.PrefetchScalarGridSpec(
            num_scalar_prefetch=2,
            grid=(TOKENS // tm,),
            in_specs=[
                pl.BlockSpec((tm, D_MODEL), rows),
                pl.BlockSpec((tm, LANES), rows),
                pl.BlockSpec(memory_space=pl.ANY),
                pl.BlockSpec((None, tm, PLE_DIM), lambda i, p1, p2: (layer, i, 0)),
                pl.BlockSpec((1, D_MODEL), const),
                pl.BlockSpec((D_MODEL, D_MODEL), const),
                pl.BlockSpec((PLE_DIM, D_MODEL), const),
                pl.BlockSpec((1, D_MODEL), const),
            ],
            out_specs=pl.BlockSpec((tm, D_MODEL), rows),
            scratch_shapes=[pltpu.VMEM((2, 2, tm * SLAB, LANES), F32), pltpu.SemaphoreType.DMA((2,))],
        ),
        out_shape=jax.ShapeDtypeStruct((TOKENS, D_MODEL), F32),
        compiler_params=_cparams(("arbitrary",), VMEM_MB),
        name="moe_combine_ple",
    )(pos1, pos2, x, meta, ys, p, g, wg, wp, fg)


def _in_proj_weights(w_in):
    k_rope = w_in[:, MAIN_COLS:MAIN_COLS + QK_ROPE]
    glu = w_in[:, MAIN_COLS + QK_ROPE:]
    half = QK_ROPE // 2
    zn = jnp.zeros((D_MODEL, QK_NOPE), F32)
    zp = jnp.zeros((D_MODEL, HEAD_PAD - QK_NOPE - QK_ROPE), F32)
    tail = jnp.concatenate([glu, zn, k_rope, zp, zn, k_rope[:, half:], k_rope[:, :half], zp], axis=1)
    return w_in[:, :MAIN_COLS].astype(BF16), tail.astype(BF16)


def _q_weight(w_uq):
    scale = (QK_NOPE + QK_ROPE) ** -0.5 * LOG2_E
    w = (w_uq * scale).reshape(Q_LORA, N_HEADS, QK_NOPE + QK_ROPE)
    zp = jnp.zeros((Q_LORA, N_HEADS, HEAD_PAD - QK_NOPE - QK_ROPE), F32)
    return jnp.concatenate([w, zp], axis=2).reshape(Q_LORA, N_HEADS * HEAD_PAD).astype(BF16)


def _kv_weight(w_ukv):
    w = w_ukv.reshape(KV_LORA, N_HEADS, QK_NOPE + V_HEAD)
    k_nope, v = w[:, :, :QK_NOPE], w[:, :, QK_NOPE:]
    z = jnp.zeros_like(v)
    k_part = jnp.concatenate([k_nope, jnp.zeros_like(k_nope)], axis=2).reshape(KV_LORA, N_HEADS * HEAD_PAD)
    odd = (jnp.arange(N_HEADS) % 2 == 1)[None, :, None]
    v_part = jnp.concatenate([jnp.where(odd, z, v), jnp.where(odd, v, z)], axis=2).reshape(KV_LORA, N_HEADS * HEAD_PAD)
    return jnp.concatenate([k_part, v_part], axis=1).astype(BF16)


def _router_weight(w_rg, b_rg, w_re, b_re):
    pad = LANES - N_EXPERTS - N_GROUPS
    w = jnp.concatenate([w_re, w_rg, jnp.zeros((D_MODEL, pad), F32)], axis=1)
    b = jnp.concatenate([b_re, b_rg, jnp.zeros((pad,), F32)]).reshape(1, LANES)
    w_hi = w.astype(BF16)
    w_lo = (w - w_hi.astype(F32)).astype(BF16)
    return jnp.concatenate([w_hi, w_lo], axis=1), b


def kernel(x, p, positions, ln_mix_g, w_in, conv_a_w, w_out_a, q_norm_g, w_uq, kv_norm_g, w_ukv, w_out_b, conv_c_w, ln_c_g, ln_c_b, w_out_c, w_o, ln_ffn_g, w_route_grp, b_route_grp, w_route_exp, b_route_exp, w_exp_gate, w_exp_up, w_exp_down, ln_ple_g, w_ple_gate, w_ple, final_norm_g):
    c_tab, s_tab = _rope_tables(positions)
    xf = x.reshape(TOKENS, D_MODEL)
    pf = p.reshape(DEPTH, TOKENS, PLE_DIM)
    row = lambda a: a.reshape(1, -1)
    for i in range(DEPTH):
        proj, tail = _inproj(xf, row(ln_mix_g[i]), *_in_proj_weights(w_in[i]))
        q, k, v = _qkv(proj, tail, c_tab, s_tab, row(q_norm_g[i]), row(kv_norm_g[i]), _q_weight(w_uq[i]), _kv_weight(w_ukv[i]))
        att = _attention(q, k, v)
        wr, br = _router_weight(w_route_grp[i], b_route_grp[i], w_route_exp[i], b_route_exp[i])
        xf, meta, meta_t, counts, xs0 = _mixer_tail(
            proj, tail, att, xf, conv_a_w[i], w_out_a[i].astype(BF16), conv_c_w[i], row(ln_c_g[i]), row(ln_c_b[i]),
            w_out_c[i].astype(BF16), w_out_b[i].astype(BF16), w_o[i].astype(BF16), row(ln_ffn_g[i]), wr, br)
        pos1, pos2, tile_expert, n_tiles = _dispatch_plan(meta_t, counts)
        xs = _dispatch(pos1, pos2, xf, row(ln_ffn_g[i]), xs0)
        ys = _experts(i, tile_expert, n_tiles, xs, w_exp_gate, w_exp_up, w_exp_down)
        xf = _combine_ple(i, pos1, pos2, xf, meta, ys, pf, row(ln_ple_g[i]), w_ple_gate[i].astype(BF16),
                          w_ple[i].astype(BF16), row(final_norm_g), final=(i == DEPTH - 1))
    return xf.reshape(BATCH, SEQ, D_MODEL)
```

```python
import functools

import jax
import jax.numpy as jnp
from jax import lax
from jax.experimental import pallas as pl
from jax.experimental.pallas import tpu as pltpu

D_MODEL = 1024
BATCH = 8
SEQ = 2048
DEPTH = 2
TOKENS = BATCH * SEQ
PLE_DIM = 256
SC_WIDTH = 512
SC_KERNEL = 3
N_HEADS = 8
QK_NOPE = 64
QK_ROPE = 32
V_HEAD = 64
Q_LORA = 768
KV_LORA = 256
ROPE_THETA = 10000.0
CONF_WIDTH = 512
CONF_KERNEL = 31
N_GROUPS = 4
EXPERTS_PER_GROUP = 8
N_EXPERTS = N_GROUPS * EXPERTS_PER_GROUP
EXPERT_HIDDEN = 256
EPS = 1e-6
LOG2_E = 1.4426950408889634

LANES = 128
HEAD_PAD = 128
F32 = jnp.float32
BF16 = jnp.bfloat16

COL_GATES = 0
COL_SC = 3 * D_MODEL
COL_QLAT = COL_SC + 3 * SC_WIDTH
COL_KVLAT = COL_QLAT + Q_LORA
MAIN_COLS = COL_KVLAT + KV_LORA
TCOL_GLU = 0
TCOL_KR = 2 * CONF_WIDTH
TCOL_KRSW = TCOL_KR + HEAD_PAD
TAIL_COLS = TCOL_KRSW + HEAD_PAD

ROUTER_GROUP_LANE = N_EXPERTS

MXU_TILE = 256
VMEM_MB = 48
VMEM_BIG_MB = 56

TM_INPROJ = 1024
TN_INPROJ = MAIN_COLS // 2
TN_SPLIT = 6 * MXU_TILE
TM_QKV = 512
T_ATTN = 512
HEADS_PER_STEP = 8
TM_MIX = 512
HALO_C = 32
HALO_A = 16
TM_EXP = 512
EXPERT_PREFETCH = 2
EXPERT_SLOTS = EXPERT_PREFETCH + 1
TM_DISP = 512
TM_COMB = 256
TOP_K = 2
MAX_TILES = TOKENS * TOP_K // TM_EXP + N_EXPERTS
SLAB = D_MODEL // LANES
GATHER_UNROLL = 16
N_DMA_PRIORITIES = 1
META_E1, META_E2, META_R1, META_R2, META_W1, META_W2 = range(6)
META_ROWS = 8
SORTED_ROWS = MAX_TILES * TM_EXP * SLAB
ZERO_COPIES = 6
ZERO_ROWS = SORTED_ROWS // (TOKENS // TM_MIX) // ZERO_COPIES


def _cparams(semantics, vmem_mb):
    return pltpu.CompilerParams(dimension_semantics=semantics, vmem_limit_bytes=vmem_mb * 1024 * 1024)


def _sigmoid(x):
    return 1.0 / (1.0 + jnp.exp2(x * -LOG2_E))


def _rms(x, g):
    return x * lax.rsqrt(jnp.mean(x * x, axis=-1, keepdims=True) + EPS) * g


def _place(x, onehot):
    x1 = x.astype(BF16)
    r1 = x - x1.astype(F32)
    x2 = r1.astype(BF16)
    x3 = (r1 - x2.astype(F32)).astype(BF16)
    return (jnp.dot(x1, onehot, preferred_element_type=F32) + jnp.dot(x2, onehot, preferred_element_type=F32)
            + jnp.dot(x3, onehot, preferred_element_type=F32))


def _rope_kernel(pos_ref, freq_ref, c_ref, s_ref):
    half = QK_ROPE // 2
    per_row = LANES // half
    rows = TOKENS // per_row
    ang = pos_ref[...].astype(F32) * freq_ref[...]
    cos = jnp.cos(ang)
    sin = jnp.sin(ang)
    src = lax.broadcasted_iota(jnp.int32, (LANES, LANES), 0)
    dst = lax.broadcasted_iota(jnp.int32, (LANES, LANES), 1)
    lane = lax.broadcasted_iota(jnp.int32, (1, LANES), 1)
    ones_nope = jnp.where(lane < QK_NOPE, 1.0, 0.0)
    sign = jnp.where(lane < QK_NOPE + half, -1.0, 1.0)
    for j in range(per_row):
        f = src - half * j
        hit = (dst == QK_NOPE + f) | (dst == QK_NOPE + half + f)
        onehot = jnp.where((f >= 0) & (f < half) & hit, 1.0, 0.0).astype(BF16)
        c_ref[pl.ds(j, rows, stride=per_row), :] = _place(cos, onehot) + ones_nope
        s_ref[pl.ds(j, rows, stride=per_row), :] = _place(sin, onehot) * sign


def _rope_tables(positions):
    half = QK_ROPE // 2
    inv_freq = ROPE_THETA ** (-jnp.arange(0, QK_ROPE, 2, dtype=F32) / QK_ROPE)
    rows = TOKENS * half // LANES
    pos_rep = jnp.broadcast_to(positions.reshape(TOKENS, 1), (TOKENS, half)).reshape(rows, LANES)
    freq = jnp.tile(inv_freq, LANES // half).reshape(1, LANES)
    return pl.pallas_call(
        _rope_kernel,
        out_shape=(jax.ShapeDtypeStruct((TOKENS, HEAD_PAD), F32),) * 2,
        compiler_params=_cparams(None, VMEM_MB),
        name="rope_tables",
    )(pos_rep, freq)


def _inproj_kernel(x_ref, g_ref, wm_ref, wt_ref, om_ref, ot_ref, h_ref):
    j = pl.program_id(1)
    n_main = MAIN_COLS // TN_INPROJ

    @pl.when(j == 0)
    def _():
        h_ref[...] = _rms(x_ref[...], g_ref[...]).astype(BF16)

    @pl.when(j < n_main)
    def _():
        for lo, hi in ((0, TN_SPLIT), (TN_SPLIT, TN_INPROJ)):
            om_ref[:, lo:hi] = jnp.dot(h_ref[...], wm_ref[:, lo:hi], preferred_element_type=F32).astype(BF16)

    @pl.when(j == n_main)
    def _():
        ot_ref[...] = jnp.dot(h_ref[...], wt_ref[...], preferred_element_type=F32).astype(BF16)


def _inproj(x, g, w_main, w_tail):
    tm, tn = TM_INPROJ, TN_INPROJ
    n_main = MAIN_COLS // tn
    main_col = lambda i, j: jnp.minimum(j, n_main - 1)
    return pl.pallas_call(
        _inproj_kernel,
        grid=(TOKENS // tm, n_main + 1),
        in_specs=[
            pl.BlockSpec((tm, D_MODEL), lambda i, j: (i, 0)),
            pl.BlockSpec((1, D_MODEL), lambda i, j: (0, 0)),
            pl.BlockSpec((D_MODEL, tn), lambda i, j: (0, main_col(i, j))),
            pl.BlockSpec((D_MODEL, TAIL_COLS), lambda i, j: (0, 0)),
        ],
        out_specs=[
            pl.BlockSpec((tm, tn), lambda i, j: (i, main_col(i, j))),
            pl.BlockSpec((tm, TAIL_COLS), lambda i, j: (i, 0)),
        ],
        out_shape=(jax.ShapeDtypeStruct((TOKENS, MAIN_COLS), BF16), jax.ShapeDtypeStruct((TOKENS, TAIL_COLS), BF16)),
        scratch_shapes=[pltpu.VMEM((tm, D_MODEL), BF16)],
        compiler_params=_cparams(("parallel", "arbitrary"), VMEM_BIG_MB),
        name="in_proj",
    )(x, g, w_main, w_tail)


def _qkv_kernel(ql_ref, kvl_ref, kr_ref, krsw_ref, c_ref, s_ref, qg_ref, kvg_ref, wq_ref, wkv_ref,
                q_out, k_out, v_out):
    c = c_ref[...]
    s = s_ref[...]
    width = N_HEADS * HEAD_PAD
    half = QK_ROPE // 2
    low_half = lax.broadcasted_iota(jnp.int32, c.shape, 1) < QK_NOPE + half

    def swap_halves(x):
        return jnp.where(low_half, pltpu.roll(x, HEAD_PAD - half, axis=1), pltpu.roll(x, half, axis=1))

    qn = _rms(ql_ref[...].astype(F32), qg_ref[...]).astype(BF16)
    qq = jnp.dot(qn, wq_ref[...], preferred_element_type=F32)
    for h in range(N_HEADS):
        lo, hi = h * HEAD_PAD, (h + 1) * HEAD_PAD
        q_out[:, lo:hi] = (qq[:, lo:hi] * c + swap_halves(qq[:, lo:hi]) * s).astype(BF16)
    kvn = _rms(kvl_ref[...].astype(F32), kvg_ref[...]).astype(BF16)
    kk = jnp.dot(kvn, wkv_ref[...], preferred_element_type=F32)
    kr = kr_ref[...].astype(F32) * c + krsw_ref[...].astype(F32) * s
    for h in range(N_HEADS):
        lo, hi = h * HEAD_PAD, (h + 1) * HEAD_PAD
        k_out[:, lo:hi] = (kk[:, lo:hi] + kr).astype(BF16)
    v_out[...] = kk[:, width:].astype(BF16)


def _qkv(proj, tail, c_tab, s_tab, qg, kvg, wq, wkv):
    tm = TM_QKV
    width = N_HEADS * HEAD_PAD
    row = lambda blk: (lambda i: (i, blk))
    const = lambda i: (0, 0)
    return pl.pallas_call(
        _qkv_kernel,
        grid=(TOKENS // tm,),
        in_specs=[
            pl.BlockSpec((tm, Q_LORA), row(COL_QLAT // Q_LORA)),
            pl.BlockSpec((tm, KV_LORA), row(COL_KVLAT // KV_LORA)),
            pl.BlockSpec((tm, HEAD_PAD), row(TCOL_KR // HEAD_PAD)),
            pl.BlockSpec((tm, HEAD_PAD), row(TCOL_KRSW // HEAD_PAD)),
            pl.BlockSpec((tm, HEAD_PAD), row(0)),
            pl.BlockSpec((tm, HEAD_PAD), row(0)),
            pl.BlockSpec((1, Q_LORA), const),
            pl.BlockSpec((1, KV_LORA), const),
            pl.BlockSpec((Q_LORA, width), const),
            pl.BlockSpec((KV_LORA, 2 * width), const),
        ],
        out_specs=[pl.BlockSpec((tm, width), row(0))] * 3,
        out_shape=(jax.ShapeDtypeStruct((TOKENS, width), BF16),) * 3,
        compiler_params=_cparams(("parallel",), VMEM_MB),
        name="qkv_prep",
    )(proj, proj, tail, tail, c_tab, s_tab, qg, kvg, wq, wkv)


def _attn_kernel(q_ref, k_ref, v_ref, o_ref, *state):
    t = T_ATTN
    nh = HEADS_PER_STEP
    qi = pl.program_id(2)
    nt = (((1,), (1,)), ((), ()))
    m_sc, l_sc, acc_sc = state[0:nh], state[nh:2 * nh], state[2 * nh:3 * nh]
    for h in range(nh):
        m_sc[h][...] = jnp.full((t, LANES), -jnp.inf, F32)
        l_sc[h][...] = jnp.zeros((t, LANES), F32)
        acc_sc[h][...] = jnp.zeros((t, LANES), F32)

    def block(j, r0, nr, c0, nc, masked):
        start = pl.multiple_of(j * t + c0, nc)
        rows = slice(r0, r0 + nr)
        for h in range(nh):
            lo, hi = h * HEAD_PAD, (h + 1) * HEAD_PAD
            s = lax.dot_general(q_ref[rows, lo:hi], k_ref[pl.ds(start, nc), lo:hi], nt, preferred_element_type=F32)
            if masked:
                row_id = r0 + lax.broadcasted_iota(jnp.int32, (nr, nc), 0)
                col_id = c0 + lax.broadcasted_iota(jnp.int32, (nr, nc), 1)
                s = jnp.where(row_id >= col_id, s, -jnp.inf)
            blocks = [s[:, c * LANES:(c + 1) * LANES] for c in range(nc // LANES)]
            bmax = functools.reduce(jnp.maximum, blocks)
            m_old = m_sc[h][rows, :]
            m_new = jnp.maximum(m_old, jnp.max(bmax, axis=-1, keepdims=True))
            alpha = jnp.exp2(m_old - m_new)
            ps = [jnp.exp2(b - m_new) for b in blocks]
            p = jnp.concatenate(ps, axis=1).astype(BF16)
            l_sc[h][rows, :] = alpha * l_sc[h][rows, :] + functools.reduce(jnp.add, ps)
            acc_sc[h][rows, :] = alpha * acc_sc[h][rows, :] + jnp.dot(p, v_ref[pl.ds(start, nc), lo:hi],
                                                                       preferred_element_type=F32)
            m_sc[h][rows, :] = m_new

    def body(j, carry):
        block(j, 0, t, 0, t, False)
        return carry

    lax.fori_loop(0, qi, body, 0)
    half = t // 2
    block(qi, 0, t, 0, half, True)
    block(qi, half, half, half, half, True)
    out = [acc_sc[h][...] / jnp.sum(l_sc[h][...], axis=-1, keepdims=True) for h in range(nh)]
    o_ref[...] = jnp.concatenate([out[h] + out[h + 1] for h in range(0, nh, 2)], axis=1).astype(BF16)


def _attention(q, k, v):
    t = T_ATTN
    nq = SEQ // t
    nh = HEADS_PER_STEP
    return pl.pallas_call(
        _attn_kernel,
        grid=(BATCH, N_HEADS // nh, nq),
        in_specs=[
            pl.BlockSpec((t, nh * HEAD_PAD), lambda b, hg, i: (b * nq + i, hg)),
            pl.BlockSpec((SEQ, nh * HEAD_PAD), lambda b, hg, i: (b, hg)),
            pl.BlockSpec((SEQ, nh * HEAD_PAD), lambda b, hg, i: (b, hg)),
        ],
        out_specs=pl.BlockSpec((t, nh * V_HEAD), lambda b, hg, i: (b * nq + i, hg)),
        out_shape=jax.ShapeDtypeStruct((TOKENS, N_HEADS * V_HEAD), BF16),
        scratch_shapes=[pltpu.VMEM((t, LANES), F32)] * (3 * nh),
        compiler_params=_cparams(("parallel", "parallel", "arbitrary"), VMEM_MB),
        name="mla_attention",
    )(q, k, v)


def _mixer_tail_kernel(gates_ref, sc_ref, sch_ref, gv_ref, gg_ref, gvh_ref, ggh_ref, att_ref, x_ref,
                       cwa_ref, woa_ref, cwc_ref, lng_ref, lnb_ref, woc_ref, wob_ref, wo_ref, gf_ref, wr_ref, br_ref,
                       out_ref, meta_ref, meta_t_ref, cnt_ref, xs0_ref, cbuf, ubuf, shifted, below, run_ref, zbuf, zsem):
    tm = TM_MIX
    step = pl.program_id(0)

    @pl.when(step == 0)
    def _():
        run_ref[...] = jnp.zeros_like(run_ref)
        zbuf[...] = jnp.zeros_like(zbuf)
        row_id = lax.broadcasted_iota(jnp.int32, (tm, tm), 0)
        col_id = lax.broadcasted_iota(jnp.int32, (tm, tm), 1)
        below[...] = jnp.where(row_id > col_id, 1.0, 0.0).astype(BF16)

    def zero_copy(c):
        rows = pl.ds(pl.multiple_of((step * ZERO_COPIES + c) * ZERO_ROWS, ZERO_ROWS), ZERO_ROWS)
        return pltpu.make_async_copy(zbuf, xs0_ref.at[rows, :], zsem.at[0])

    for c in range(ZERO_COPIES):
        zero_copy(c).start()
    has_past = (pl.program_id(0) % (SEQ // tm)) != 0

    sc = sc_ref[...]
    sc_b = sc[:, 0:SC_WIDTH].astype(F32)
    cbuf[8:8 + tm, :] = sc[:, SC_WIDTH:2 * SC_WIDTH].astype(F32) * sc[:, 2 * SC_WIDTH:].astype(F32)
    sch = sch_ref[...].astype(F32)[HALO_A - 8:HALO_A, :]
    cbuf[0:8, :] = jnp.where(has_past, sch[:, SC_WIDTH:2 * SC_WIDTH] * sch[:, 2 * SC_WIDTH:], 0.0)
    conv_a = cwa_ref[0:1, :] * cbuf[6:6 + tm, :]
    for t in range(1, SC_KERNEL):
        conv_a = conv_a + cwa_ref[t:t + 1, :] * cbuf[6 + t:6 + t + tm, :]
    y_a = jnp.dot((sc_b * conv_a).astype(BF16), woa_ref[...], preferred_element_type=F32)

    ubuf[HALO_C:HALO_C + tm, :] = gv_ref[...].astype(F32) * _sigmoid(gg_ref[...].astype(F32))
    ubuf[0:HALO_C, :] = jnp.where(has_past, gvh_ref[...].astype(F32) * _sigmoid(ggh_ref[...].astype(F32)), 0.0)
    rows = tm + HALO_C - 8
    u_all = ubuf[...]
    for b in range(1, 8):
        shifted[b - 1, 0:rows, :] = pltpu.roll(u_all, tm + HALO_C - b, axis=0)[0:rows, :]
    base = HALO_C - (CONF_KERNEL - 1)
    acc = None
    for t in range(CONF_KERNEL):
        off = base + t
        a0 = off - off % 8
        src = ubuf[a0:a0 + tm, :] if off % 8 == 0 else shifted[off % 8 - 1, a0:a0 + tm, :]
        term = cwc_ref[t:t + 1, :] * src
        acc = term if acc is None else acc + term
    mu = jnp.mean(acc, axis=-1, keepdims=True)
    xc = acc - mu
    var = jnp.mean(xc * xc, axis=-1, keepdims=True)
    y = xc * lax.rsqrt(var + EPS) * lng_ref[...] + lnb_ref[...]
    y_c = jnp.dot((y * _sigmoid(y)).astype(BF16), woc_ref[...], preferred_element_type=F32)

    y_b = jnp.dot(att_ref[...], wob_ref[...], preferred_element_type=F32)

    g = gates_ref[...]
    merged = (_sigmoid(g[:, 0:D_MODEL].astype(F32)) * y_a
              + _sigmoid(g[:, D_MODEL:2 * D_MODEL].astype(F32)) * y_b
              + _sigmoid(g[:, 2 * D_MODEL:].astype(F32)) * y_c)
    x_new = x_ref[...] + jnp.dot(merged.astype(BF16), wo_ref[...], preferred_element_type=F32)
    out_ref[...] = x_new
    _route_tile(x_new, gf_ref, wr_ref, br_ref, below, run_ref, meta_ref, meta_t_ref, cnt_ref)
    for c in range(ZERO_COPIES):
        zero_copy(c).wait()


def _mixer_tail(proj, tail, att, x, cwa, woa, cwc, lng, lnb, woc, wob, wo, gf, wr, br):
    tm = TM_MIX
    row = lambda width, col: pl.BlockSpec((tm, width), lambda i: (i, col // width))
    halo = lambda rows, width, col: pl.BlockSpec(
        (rows, width), lambda i: (jnp.maximum(i * (tm // rows) - 1, 0), col // width))
    const = lambda a: pl.BlockSpec(a.shape, lambda i: (0,) * a.ndim)
    weights = (cwa, woa, cwc, lng, lnb, woc, wob, wo, gf, wr, br)
    return pl.pallas_call(
        _mixer_tail_kernel,
        grid=(TOKENS // tm,),
        in_specs=[
            row(3 * D_MODEL, COL_GATES),
            row(3 * SC_WIDTH, COL_SC),
            halo(HALO_A, 3 * SC_WIDTH, COL_SC),
            row(CONF_WIDTH, TCOL_GLU),
            row(CONF_WIDTH, TCOL_GLU + CONF_WIDTH),
            halo(HALO_C, CONF_WIDTH, TCOL_GLU),
            halo(HALO_C, CONF_WIDTH, TCOL_GLU + CONF_WIDTH),
            pl.BlockSpec((tm, N_HEADS * V_HEAD), lambda i: (i, 0)),
            pl.BlockSpec((tm, D_MODEL), lambda i: (i, 0)),
        ] + [const(a) for a in weights],
        out_specs=[
            pl.BlockSpec((tm, D_MODEL), lambda i: (i, 0)),
            pl.BlockSpec((tm, LANES), lambda i: (i, 0)),
            pl.BlockSpec((META_ROWS, tm), lambda i: (0, i)),
            pl.BlockSpec((1, LANES), lambda i: (0, 0)),
            pl.BlockSpec(memory_space=pl.ANY),
        ],
        out_shape=(
            jax.ShapeDtypeStruct((TOKENS, D_MODEL), F32),
            jax.ShapeDtypeStruct((TOKENS, LANES), F32),
            jax.ShapeDtypeStruct((META_ROWS, TOKENS), F32),
            jax.ShapeDtypeStruct((1, LANES), F32),
            jax.ShapeDtypeStruct((SORTED_ROWS, LANES), F32),
        ),
        scratch_shapes=[
            pltpu.VMEM((tm + 8, SC_WIDTH), F32),
            pltpu.VMEM((tm + HALO_C, CONF_WIDTH), F32),
            pltpu.VMEM((7, tm + HALO_C - 8, CONF_WIDTH), F32),
            pltpu.VMEM((tm, tm), BF16),
            pltpu.VMEM((1, LANES), F32),
            pltpu.VMEM((ZERO_ROWS, LANES), F32),
            pltpu.SemaphoreType.DMA((1,)),
        ],
        compiler_params=_cparams(("arbitrary",), VMEM_BIG_MB),
        name="mixer_tail",
    )(proj, proj, proj, tail, tail, tail, tail, att, x, *weights)


def _route(logits):
    lane = lax.broadcasted_iota(jnp.int32, logits.shape, 1)
    lane_f = lane.astype(F32)
    neg = -jnp.inf
    big = float(LANES)
    is_grp = (lane >= ROUTER_GROUP_LANE) & (lane < ROUTER_GROUP_LANE + N_GROUPS)
    glog = jnp.where(is_grp, logits, neg)
    gmax = jnp.max(glog, axis=-1, keepdims=True)
    gidx = jnp.min(jnp.where(glog == gmax, lane_f, big), axis=-1, keepdims=True)
    p_sel = 1.0 / jnp.sum(jnp.exp(glog - gmax), axis=-1, keepdims=True)
    first = (gidx - ROUTER_GROUP_LANE) * EXPERTS_PER_GROUP
    in_grp = (lane_f >= first) & (lane_f < first + EXPERTS_PER_GROUP)
    el = jnp.where(in_grp, logits, neg)
    m1 = jnp.max(el, axis=-1, keepdims=True)
    i1 = jnp.min(jnp.where(el == m1, lane_f, big), axis=-1, keepdims=True)
    el2 = jnp.where(lane_f == i1, neg, el)
    m2 = jnp.max(el2, axis=-1, keepdims=True)
    i2 = jnp.min(jnp.where(el2 == m2, lane_f, big), axis=-1, keepdims=True)
    e2 = jnp.exp(m2 - m1)
    w1 = p_sel / (1.0 + e2)
    w2 = w1 * e2
    return i1, i2, w1, w2


def _route_tile(x, g_ref, wr_ref, br_ref, below_ref, run_ref, meta_ref, meta_t_ref, cnt_ref):
    tm = x.shape[0]
    h = _rms(x, g_ref[...])
    h_hi = h.astype(BF16)
    h_lo = (h - h_hi.astype(F32)).astype(BF16)
    hi_terms = jnp.dot(h_hi, wr_ref[...], preferred_element_type=F32)
    logits = (hi_terms[:, :LANES] + hi_terms[:, LANES:]
              + jnp.dot(h_lo, wr_ref[:, :LANES], preferred_element_type=F32) + br_ref[...])
    i1, i2, w1, w2 = _route(logits)
    lane = lax.broadcasted_iota(jnp.int32, (tm, LANES), 1)
    lane_f = lane.astype(F32)
    oh1 = lane_f == i1
    oh2 = lane_f == i2
    onehot = jnp.where(oh1, 1.0, 0.0) + jnp.where(oh2, 1.0, 0.0)
    before = run_ref[...] + jnp.dot(below_ref[...], onehot.astype(BF16), preferred_element_type=F32)
    r1 = jnp.sum(jnp.where(oh1, before, 0.0), axis=-1, keepdims=True)
    r2 = jnp.sum(jnp.where(oh2, before, 0.0), axis=-1, keepdims=True)
    run_ref[...] += jnp.sum(onehot, axis=0, keepdims=True)
    cnt_ref[...] = run_ref[...]
    meta = jnp.zeros((tm, LANES), F32)
    for col, val in enumerate((i1, i2, r1, r2, w1, w2)):
        meta = jnp.where(lane == col, val, meta)
    meta_ref[...] = meta
    meta_t_ref[...] = meta.T[0:META_ROWS, :]


def _dispatch_plan(meta_t, counts):
    e1 = meta_t[META_E1].astype(jnp.int32)
    e2 = meta_t[META_E2].astype(jnp.int32)
    r1 = meta_t[META_R1].astype(jnp.int32)
    r2 = meta_t[META_R2].astype(jnp.int32)
    cnt = counts[0, :N_EXPERTS].astype(jnp.int32)
    tiles = (cnt + TM_EXP - 1) // TM_EXP
    tile_end = jnp.cumsum(tiles)
    first_slot = ((tile_end - tiles) * TM_EXP)[:, None]
    expert = jnp.arange(N_EXPERTS, dtype=jnp.int32)[:, None]
    pos1 = jnp.sum(jnp.where(e1[None, :] == expert, first_slot, 0), axis=0) + r1
    pos2 = jnp.sum(jnp.where(e2[None, :] == expert, first_slot, 0), axis=0) + r2
    n_tiles = tile_end[-1:]
    tile_id = jnp.minimum(jnp.arange(MAX_TILES, dtype=jnp.int32), n_tiles - 1)
    tile_expert = jnp.sum((tile_id[:, None] >= tile_end[None, :]).astype(jnp.int32), axis=1)
    return pos1, pos2, tile_expert, n_tiles


def _dispatch_kernel(pos1_ref, pos2_ref, x_ref, g_ref, xs_in_ref, xs_ref, slab, sem):
    del xs_in_ref
    tm = TM_DISP
    i = pl.program_id(0)
    last = pl.num_programs(0) - 1
    slot = i % 2

    def wait_copies(sl):
        for _ in range(TOP_K):
            pltpu.make_async_copy(slab.at[sl], xs_ref.at[pl.ds(0, tm * SLAB), :], sem.at[sl]).wait()

    @pl.when(i >= 2)
    def _():
        wait_copies(slot)

    h = _rms(x_ref[...], g_ref[...])
    for s in range(SLAB):
        slab[slot, pl.ds(s, tm, stride=SLAB), :] = h[:, s * LANES:(s + 1) * LANES]

    def body(c, carry):
        for u in range(GATHER_UNROLL):
            r = c * GATHER_UNROLL + u
            src = slab.at[slot, pl.ds(pl.multiple_of(r * SLAB, SLAB), SLAB), :]
            for k, pos_ref in enumerate((pos1_ref, pos2_ref)):
                p = pos_ref[i * tm + r]
                pltpu.make_async_copy(src, xs_ref.at[pl.ds(pl.multiple_of(p * SLAB, SLAB), SLAB), :],
                                      sem.at[slot]).start(priority=k % N_DMA_PRIORITIES)
        return carry

    lax.fori_loop(0, tm // GATHER_UNROLL, body, 0)

    @pl.when(i == last)
    def _():
        wait_copies(1 - slot)
        wait_copies(slot)


def _dispatch(pos1, pos2, x, g, xs0):
    tm = TM_DISP
    return pl.pallas_call(
        _dispatch_kernel,
        grid_spec=pltpu.PrefetchScalarGridSpec(
            num_scalar_prefetch=2,
            grid=(TOKENS // tm,),
            in_specs=[
                pl.BlockSpec((tm, D_MODEL), lambda i, p1, p2: (i, 0)),
                pl.BlockSpec((1, D_MODEL), lambda i, p1, p2: (0, 0)),
                pl.BlockSpec(memory_space=pl.ANY),
            ],
            out_specs=pl.BlockSpec(memory_space=pl.ANY),
            scratch_shapes=[pltpu.VMEM((2, tm * SLAB, LANES), F32), pltpu.SemaphoreType.DMA((2,))],
        ),
        out_shape=jax.ShapeDtypeStruct((SORTED_ROWS, LANES), F32),
        input_output_aliases={4: 0},
        compiler_params=_cparams(("arbitrary",), VMEM_MB),
        name="moe_dispatch",
    )(pos1, pos2, x, g, xs0)


def _expert_kernel(te_ref, nt_ref, xs_ref, wg_ref, wu_ref, wd_ref, ys_ref, xbuf, sem):
    del te_ref
    tm = TM_EXP
    i = pl.program_id(0)
    n = nt_ref[0]

    def tile_copy(tile):
        slot = tile % EXPERT_SLOTS
        rows = pl.ds(pl.multiple_of(tile * (tm * SLAB), tm * SLAB), tm * SLAB)
        return pltpu.make_async_copy(xs_ref.at[rows, :], xbuf.at[slot], sem.at[slot])

    @pl.when(i == 0)
    def _():
        for ahead in range(EXPERT_PREFETCH):
            @pl.when(ahead < n)
            def _():
                tile_copy(ahead).start()

    @pl.when(i + EXPERT_PREFETCH < n)
    def _():
        tile_copy(i + EXPERT_PREFETCH).start()

    @pl.when(i < n)
    def _():
        tile_copy(i).wait()
        slot = i % EXPERT_SLOTS
        xt = jnp.concatenate([xbuf[slot, pl.ds(s, tm, stride=SLAB), :].astype(BF16) for s in range(SLAB)], axis=1)
        hg = jnp.dot(xt, wg_ref[...].astype(BF16), preferred_element_type=F32)
        hu = jnp.dot(xt, wu_ref[...].astype(BF16), preferred_element_type=F32)
        hh = (hg * _sigmoid(hg) * hu).astype(BF16)
        y = jnp.dot(hh, wd_ref[...].astype(BF16), preferred_element_type=F32)
        for s in range(SLAB):
            ys_ref[pl.ds(s, tm, stride=SLAB), :] = y[:, s * LANES:(s + 1) * LANES]

    @pl.when(i >= n)
    def _():
        ys_ref[...] = jnp.zeros_like(ys_ref)


def _experts(layer, tile_expert, n_tiles, xs, w_gate, w_up, w_down):
    tm = TM_EXP
    wspec = lambda rows, cols: pl.BlockSpec((None, None, rows, cols), lambda i, te, nt: (layer, te[i], 0, 0))
    return pl.pallas_call(
        _expert_kernel,
        grid_spec=pltpu.PrefetchScalarGridSpec(
            num_scalar_prefetch=2,
            grid=(MAX_TILES,),
            in_specs=[
                pl.BlockSpec(memory_space=pl.ANY),
                wspec(D_MODEL, EXPERT_HIDDEN),
                wspec(D_MODEL, EXPERT_HIDDEN),
                wspec(EXPERT_HIDDEN, D_MODEL),
            ],
            out_specs=pl.BlockSpec((tm * SLAB, LANES), lambda i, te, nt: (i, 0)),
            scratch_shapes=[pltpu.VMEM((EXPERT_SLOTS, tm * SLAB, LANES), F32), pltpu.SemaphoreType.DMA((EXPERT_SLOTS,))],
        ),
        out_shape=jax.ShapeDtypeStruct((SORTED_ROWS, LANES), F32),
        compiler_params=_cparams(("arbitrary",), VMEM_MB),
        name="moe_experts",
    )(tile_expert, n_tiles, xs, w_gate, w_up, w_down)


def _start_slab_gathers(idx_refs, base, n_rows, src_hbm, dst_bufs, sem):
    def body(c, carry):
        for u in range(GATHER_UNROLL):
            r = c * GATHER_UNROLL + u
            for k, (idx_ref, dst) in enumerate(zip(idx_refs, dst_bufs)):
                t = idx_ref[base + r]
                pltpu.make_async_copy(src_hbm.at[pl.ds(pl.multiple_of(t * SLAB, SLAB), SLAB), :],
                                      dst.at[pl.ds(pl.multiple_of(r * SLAB, SLAB), SLAB), :],
                                      sem).start(priority=k % N_DMA_PRIORITIES)
        return carry

    lax.fori_loop(0, n_rows // GATHER_UNROLL, body, 0)


def _wait_slab_gathers(n_rows, src_hbm, dst, sem):
    pltpu.make_async_copy(src_hbm.at[pl.ds(0, n_rows * SLAB), :], dst, sem).wait()


def _combine_ple_kernel(pos1_ref, pos2_ref, x_ref, meta_ref, ys_ref, p_ref, g_ref, wg_ref, wp_ref, fg_ref,
                        out_ref, cbuf, sem, *, final):
    tm = TM_COMB
    i = pl.program_id(0)

    def gather(tile, slot):
        _start_slab_gathers((pos1_ref, pos2_ref), tile * tm, tm, ys_ref, (cbuf.at[slot, 0], cbuf.at[slot, 1]),
                            sem.at[slot])

    @pl.when(i == 0)
    def _():
        gather(0, 0)

    @pl.when(i + 1 < pl.num_programs(0))
    def _():
        gather(i + 1, (i + 1) % 2)

    slot = i % 2
    for k in range(2):
        _wait_slab_gathers(tm, ys_ref, cbuf.at[slot, k], sem.at[slot])
    meta = meta_ref[...]
    w1 = meta[:, META_W1:META_W1 + 1]
    w2 = meta[:, META_W2:META_W2 + 1]
    moe = jnp.concatenate([w1 * cbuf[slot, 0, pl.ds(s, tm, stride=SLAB), :]
                           + w2 * cbuf[slot, 1, pl.ds(s, tm, stride=SLAB), :] for s in range(SLAB)], axis=1)
    x = x_ref[...] + moe
    h = _rms(x, g_ref[...]).astype(BF16)
    gate = _sigmoid(jnp.dot(h, wg_ref[...], preferred_element_type=F32))
    emb = jnp.dot(p_ref[...].astype(BF16), wp_ref[...], preferred_element_type=F32)
    y = x + gate * emb
    if final:
        y = _rms(y, fg_ref[...])
    out_ref[...] = y


def _combine_ple(layer, pos1, pos2, x, meta, ys, p, g, wg, wp, fg, final):
    tm = TM_COMB
    const = lambda i, p1, p2: (0, 0)
    rows = lambda i, p1, p2: (i, 0)
    return pl.pallas_call(
        functools.partial(_combine_ple_kernel, final=final),
        grid_spec=pltpu.PrefetchScalarGridSpec(
            num_scalar_prefetch=2,
            grid=(TOKENS // tm,),
            in_specs=[
                pl.BlockSpec((tm, D_MODEL), rows),
                pl.BlockSpec((tm, LANES), rows),
                pl.BlockSpec(memory_space=pl.ANY),
                pl.BlockSpec((None, tm, PLE_DIM), lambda i, p1, p2: (layer, i, 0)),
                pl.BlockSpec((1, D_MODEL), const),
                pl.BlockSpec((D_MODEL, D_MODEL), const),
                pl.BlockSpec((PLE_DIM, D_MODEL), const),
                pl.BlockSpec((1, D_MODEL), const),
            ],
            out_specs=pl.BlockSpec((tm, D_MODEL), rows),
            scratch_shapes=[pltpu.VMEM((2, 2, tm * SLAB, LANES), F32), pltpu.SemaphoreType.DMA((2,))],
        ),
        out_shape=jax.ShapeDtypeStruct((TOKENS, D_MODEL), F32),
        compiler_params=_cparams(("arbitrary",), VMEM_MB),
        name="moe_combine_ple",
    )(pos1, pos2, x, meta, ys, p, g, wg, wp, fg)


def _in_proj_weights(w_in):
    k_rope = w_in[:, MAIN_COLS:MAIN_COLS + QK_ROPE]
    glu = w_in[:, MAIN_COLS + QK_ROPE:]
    half = QK_ROPE // 2
    zn = jnp.zeros((D_MODEL, QK_NOPE), F32)
    zp = jnp.zeros((D_MODEL, HEAD_PAD - QK_NOPE - QK_ROPE), F32)
    tail = jnp.concatenate([glu, zn, k_rope, zp, zn, k_rope[:, half:], k_rope[:, :half], zp], axis=1)
    return w_in[:, :MAIN_COLS].astype(BF16), tail.astype(BF16)


def _q_weight(w_uq):
    scale = (QK_NOPE + QK_ROPE) ** -0.5 * LOG2_E
    w = (w_uq * scale).reshape(Q_LORA, N_HEADS, QK_NOPE + QK_ROPE)
    zp = jnp.zeros((Q_LORA, N_HEADS, HEAD_PAD - QK_NOPE - QK_ROPE), F32)
    return jnp.concatenate([w, zp], axis=2).reshape(Q_LORA, N_HEADS * HEAD_PAD).astype(BF16)


def _kv_weight(w_ukv):
    w = w_ukv.reshape(KV_LORA, N_HEADS, QK_NOPE + V_HEAD)
    k_nope, v = w[:, :, :QK_NOPE], w[:, :, QK_NOPE:]
    z = jnp.zeros_like(v)
    k_part = jnp.concatenate([k_nope, jnp.zeros_like(k_nope)], axis=2).reshape(KV_LORA, N_HEADS * HEAD_PAD)
    odd = (jnp.arange(N_HEADS) % 2 == 1)[None, :, None]
    v_part = jnp.concatenate([jnp.where(odd, z, v), jnp.where(odd, v, z)], axis=2).reshape(KV_LORA, N_HEADS * HEAD_PAD)
    return jnp.concatenate([k_part, v_part], axis=1).astype(BF16)


def _router_weight(w_rg, b_rg, w_re, b_re):
    pad = LANES - N_EXPERTS - N_GROUPS
    w = jnp.concatenate([w_re, w_rg, jnp.zeros((D_MODEL, pad), F32)], axis=1)
    b = jnp.concatenate([b_re, b_rg, jnp.zeros((pad,), F32)]).reshape(1, LANES)
    w_hi = w.astype(BF16)
    w_lo = (w - w_hi.astype(F32)).astype(BF16)
    return jnp.concatenate([w_hi, w_lo], axis=1), b


def kernel(x, p, positions, ln_mix_g, w_in, conv_a_w, w_out_a, q_norm_g, w_uq, kv_norm_g, w_ukv, w_out_b, conv_c_w, ln_c_g, ln_c_b, w_out_c, w_o, ln_ffn_g, w_route_grp, b_route_grp, w_route_exp, b_route_exp, w_exp_gate, w_exp_up, w_exp_down, ln_ple_g, w_ple_gate, w_ple, final_norm_g):
    c_tab, s_tab = _rope_tables(positions)
    xf = x.reshape(TOKENS, D_MODEL)
    pf = p.reshape(DEPTH, TOKENS, PLE_DIM)
    row = lambda a: a.reshape(1, -1)
    for i in range(DEPTH):
        proj, tail = _inproj(xf, row(ln_mix_g[i]), *_in_proj_weights(w_in[i]))
        q, k, v = _qkv(proj, tail, c_tab, s_tab, row(q_norm_g[i]), row(kv_norm_g[i]), _q_weight(w_uq[i]), _kv_weight(w_ukv[i]))
        att = _attention(q, k, v)
        wr, br = _router_weight(w_route_grp[i], b_route_grp[i], w_route_exp[i], b_route_exp[i])
        xf, meta, meta_t, counts, xs0 = _mixer_tail(
            proj, tail, att, xf, conv_a_w[i], w_out_a[i].astype(BF16), conv_c_w[i], row(ln_c_g[i]), row(ln_c_b[i]),
            w_out_c[i].astype(BF16), w_out_b[i].astype(BF16), w_o[i].astype(BF16), row(ln_ffn_g[i]), wr, br)
        pos1, pos2, tile_expert, n_tiles = _dispatch_plan(meta_t, counts)
        xs = _dispatch(pos1, pos2, xf, row(ln_ffn_g[i]), xs0)
        ys = _experts(i, tile_expert, n_tiles, xs, w_exp_gate, w_exp_up, w_exp_down)
        xf = _combine_ple(i, pos1, pos2, xf, meta, ys, pf, row(ln_ple_g[i]), w_ple_gate[i].astype(BF16),
                          w_ple[i].astype(BF16), row(final_norm_g), final=(i == DEPTH - 1))
    return xf.reshape(BATCH, SEQ, D_MODEL)
```

```python
import functools

import jax
import jax.numpy as jnp
from jax import lax
from jax.experimental import pallas as pl
from jax.experimental.pallas import tpu as pltpu

D_MODEL = 1024
BATCH = 8
SEQ = 2048
DEPTH = 2
TOKENS = BATCH * SEQ
PLE_DIM = 256
SC_WIDTH = 512
SC_KERNEL = 3
N_HEADS = 8
QK_NOPE = 64
QK_ROPE = 32
V_HEAD = 64
Q_LORA = 768
KV_LORA = 256
ROPE_THETA = 10000.0
CONF_WIDTH = 512
CONF_KERNEL = 31
N_GROUPS = 4
EXPERTS_PER_GROUP = 8
N_EXPERTS = N_GROUPS * EXPERTS_PER_GROUP
EXPERT_HIDDEN = 256
EPS = 1e-6
LOG2_E = 1.4426950408889634

LANES = 128
HEAD_PAD = 128
F32 = jnp.float32
BF16 = jnp.bfloat16

COL_GATES = 0
COL_SC = 3 * D_MODEL
COL_QLAT = COL_SC + 3 * SC_WIDTH
COL_KVLAT = COL_QLAT + Q_LORA
MAIN_COLS = COL_KVLAT + KV_LORA
TCOL_GLU = 0
TCOL_KR = 2 * CONF_WIDTH
TCOL_KRSW = TCOL_KR + HEAD_PAD
TAIL_COLS = TCOL_KRSW + HEAD_PAD

ROUTER_GROUP_LANE = N_EXPERTS

MXU_TILE = 256
VMEM_MB = 48
VMEM_BIG_MB = 56

TM_INPROJ = 512
TN_INPROJ = MAIN_COLS // 2
TN_SPLIT = 6 * MXU_TILE
TM_QKV = 512
T_ATTN = 512
HEADS_PER_STEP = 8
TM_MIX = 512
HALO_C = 32
HALO_A = 16
TM_EXP = 512
EXPERT_PREFETCH = 2
EXPERT_SLOTS = EXPERT_PREFETCH + 1
TM_DISP = 512
TM_COMB = 256
TOP_K = 2
MAX_TILES = TOKENS * TOP_K // TM_EXP + N_EXPERTS
SLAB = D_MODEL // LANES
GATHER_UNROLL = 16
N_DMA_PRIORITIES = 2
META_E1, META_E2, META_R1, META_R2, META_W1, META_W2 = range(6)
META_ROWS = 8
SORTED_ROWS = MAX_TILES * TM_EXP * SLAB
ZERO_COPIES = 6
ZERO_ROWS = SORTED_ROWS // (TOKENS // TM_MIX) // ZERO_COPIES


def _cparams(semantics, vmem_mb):
    return pltpu.CompilerParams(dimension_semantics=semantics, vmem_limit_bytes=vmem_mb * 1024 * 1024)


def _sigmoid(x):
    return 1.0 / (1.0 + jnp.exp2(x * -LOG2_E))


def _rms(x, g):
    return x * lax.rsqrt(jnp.mean(x * x, axis=-1, keepdims=True) + EPS) * g


def _place(x, onehot):
    x1 = x.astype(BF16)
    r1 = x - x1.astype(F32)
    x2 = r1.astype(BF16)
    x3 = (r1 - x2.astype(F32)).astype(BF16)
    return (jnp.dot(x1, onehot, preferred_element_type=F32) + jnp.dot(x2, onehot, preferred_element_type=F32)
            + jnp.dot(x3, onehot, preferred_element_type=F32))


def _rope_kernel(pos_ref, freq_ref, c_ref, s_ref):
    half = QK_ROPE // 2
    per_row = LANES // half
    rows = TOKENS // per_row
    ang = pos_ref[...].astype(F32) * freq_ref[...]
    cos = jnp.cos(ang)
    sin = jnp.sin(ang)
    src = lax.broadcasted_iota(jnp.int32, (LANES, LANES), 0)
    dst = lax.broadcasted_iota(jnp.int32, (LANES, LANES), 1)
    lane = lax.broadcasted_iota(jnp.int32, (1, LANES), 1)
    ones_nope = jnp.where(lane < QK_NOPE, 1.0, 0.0)
    sign = jnp.where(lane < QK_NOPE + half, -1.0, 1.0)
    for j in range(per_row):
        f = src - half * j
        hit = (dst == QK_NOPE + f) | (dst == QK_NOPE + half + f)
        onehot = jnp.where((f >= 0) & (f < half) & hit, 1.0, 0.0).astype(BF16)
        c_ref[pl.ds(j, rows, stride=per_row), :] = _place(cos, onehot) + ones_nope
        s_ref[pl.ds(j, rows, stride=per_row), :] = _place(sin, onehot) * sign


def _rope_tables(positions):
    half = QK_ROPE // 2
    inv_freq = ROPE_THETA ** (-jnp.arange(0, QK_ROPE, 2, dtype=F32) / QK_ROPE)
    rows = TOKENS * half // LANES
    pos_rep = jnp.broadcast_to(positions.reshape(TOKENS, 1), (TOKENS, half)).reshape(rows, LANES)
    freq = jnp.tile(inv_freq, LANES // half).reshape(1, LANES)
    return pl.pallas_call(
        _rope_kernel,
        out_shape=(jax.ShapeDtypeStruct((TOKENS, HEAD_PAD), F32),) * 2,
        compiler_params=_cparams(None, VMEM_MB),
        name="rope_tables",
    )(pos_rep, freq)


def _inproj_kernel(x_ref, g_ref, wm_ref, wt_ref, om_ref, ot_ref, h_ref):
    j = pl.program_id(1)
    n_main = MAIN_COLS // TN_INPROJ

    @pl.when(j == 0)
    def _():
        h_ref[...] = _rms(x_ref[...], g_ref[...]).astype(BF16)

    for jj in range(n_main):
        @pl.when(j == jj)
        def _():
            for lo, hi in ((0, TN_SPLIT), (TN_SPLIT, TN_INPROJ)):
                cols = slice(jj * TN_INPROJ + lo, jj * TN_INPROJ + hi)
                om_ref[:, lo:hi] = jnp.dot(h_ref[...], wm_ref[:, cols], preferred_element_type=F32).astype(BF16)

    @pl.when(j == n_main)
    def _():
        ot_ref[...] = jnp.dot(h_ref[...], wt_ref[...], preferred_element_type=F32).astype(BF16)


def _inproj(x, g, w_main, w_tail):
    tm, tn = TM_INPROJ, TN_INPROJ
    n_main = MAIN_COLS // tn
    main_col = lambda i, j: jnp.minimum(j, n_main - 1)
    return pl.pallas_call(
        _inproj_kernel,
        grid=(TOKENS // tm, n_main + 1),
        in_specs=[
            pl.BlockSpec((tm, D_MODEL), lambda i, j: (i, 0)),
            pl.BlockSpec((1, D_MODEL), lambda i, j: (0, 0)),
            pl.BlockSpec((D_MODEL, MAIN_COLS), lambda i, j: (0, 0)),
            pl.BlockSpec((D_MODEL, TAIL_COLS), lambda i, j: (0, 0)),
        ],
        out_specs=[
            pl.BlockSpec((tm, tn), lambda i, j: (i, main_col(i, j))),
            pl.BlockSpec((tm, TAIL_COLS), lambda i, j: (i, 0)),
        ],
        out_shape=(jax.ShapeDtypeStruct((TOKENS, MAIN_COLS), BF16), jax.ShapeDtypeStruct((TOKENS, TAIL_COLS), BF16)),
        scratch_shapes=[pltpu.VMEM((tm, D_MODEL), BF16)],
        compiler_params=_cparams(("parallel", "arbitrary"), VMEM_BIG_MB),
        name="in_proj",
    )(x, g, w_main, w_tail)


def _qkv_kernel(ql_ref, kvl_ref, kr_ref, krsw_ref, c_ref, s_ref, qg_ref, kvg_ref, wq_ref, wkv_ref,
                q_out, k_out, v_out):
    c = c_ref[...]
    s = s_ref[...]
    width = N_HEADS * HEAD_PAD
    half = QK_ROPE // 2
    low_half = lax.broadcasted_iota(jnp.int32, c.shape, 1) < QK_NOPE + half

    def swap_halves(x):
        return jnp.where(low_half, pltpu.roll(x, HEAD_PAD - half, axis=1), pltpu.roll(x, half, axis=1))

    qn = _rms(ql_ref[...].astype(F32), qg_ref[...]).astype(BF16)
    qq = jnp.dot(qn, wq_ref[...], preferred_element_type=F32)
    for h in range(N_HEADS):
        lo, hi = h * HEAD_PAD, (h + 1) * HEAD_PAD
        q_out[:, lo:hi] = (qq[:, lo:hi] * c + swap_halves(qq[:, lo:hi]) * s).astype(BF16)
    kvn = _rms(kvl_ref[...].astype(F32), kvg_ref[...]).astype(BF16)
    kk = jnp.dot(kvn, wkv_ref[...], preferred_element_type=F32)
    kr = kr_ref[...].astype(F32) * c + krsw_ref[...].astype(F32) * s
    for h in range(N_HEADS):
        lo, hi = h * HEAD_PAD, (h + 1) * HEAD_PAD
        k_out[:, lo:hi] = (kk[:, lo:hi] + kr).astype(BF16)
    v_out[...] = kk[:, width:].astype(BF16)


def _qkv(proj, tail, c_tab, s_tab, qg, kvg, wq, wkv):
    tm = TM_QKV
    width = N_HEADS * HEAD_PAD
    row = lambda blk: (lambda i: (i, blk))
    const = lambda i: (0, 0)
    return pl.pallas_call(
        _qkv_kernel,
        grid=(TOKENS // tm,),
        in_specs=[
            pl.BlockSpec((tm, Q_LORA), row(COL_QLAT // Q_LORA)),
            pl.BlockSpec((tm, KV_LORA), row(COL_KVLAT // KV_LORA)),
            pl.BlockSpec((tm, HEAD_PAD), row(TCOL_KR // HEAD_PAD)),
            pl.BlockSpec((tm, HEAD_PAD), row(TCOL_KRSW // HEAD_PAD)),
            pl.BlockSpec((tm, HEAD_PAD), row(0)),
            pl.BlockSpec((tm, HEAD_PAD), row(0)),
            pl.BlockSpec((1, Q_LORA), const),
            pl.BlockSpec((1, KV_LORA), const),
            pl.BlockSpec((Q_LORA, width), const),
            pl.BlockSpec((KV_LORA, 2 * width), const),
        ],
        out_specs=[pl.BlockSpec((tm, width), row(0))] * 3,
        out_shape=(jax.ShapeDtypeStruct((TOKENS, width), BF16),) * 3,
        compiler_params=_cparams(("parallel",), VMEM_MB),
        name="qkv_prep",
    )(proj, proj, tail, tail, c_tab, s_tab, qg, kvg, wq, wkv)


def _attn_kernel(q_ref, k_ref, v_ref, o_ref, *state):
    t = T_ATTN
    nh = HEADS_PER_STEP
    qi = pl.program_id(2)
    nt = (((1,), (1,)), ((), ()))
    m_sc, l_sc, acc_sc = state[0:nh], state[nh:2 * nh], state[2 * nh:3 * nh]
    for h in range(nh):
        m_sc[h][...] = jnp.full((t, LANES), -jnp.inf, F32)
        l_sc[h][...] = jnp.zeros((t, LANES), F32)
        acc_sc[h][...] = jnp.zeros((t, LANES), F32)

    def block(j, r0, nr, c0, nc, masked):
        start = pl.multiple_of(j * t + c0, nc)
        rows = slice(r0, r0 + nr)
        for h in range(nh):
            lo, hi = h * HEAD_PAD, (h + 1) * HEAD_PAD
            s = lax.dot_general(q_ref[rows, lo:hi], k_ref[pl.ds(start, nc), lo:hi], nt, preferred_element_type=F32)
            if masked:
                row_id = r0 + lax.broadcasted_iota(jnp.int32, (nr, nc), 0)
                col_id = c0 + lax.broadcasted_iota(jnp.int32, (nr, nc), 1)
                s = jnp.where(row_id >= col_id, s, -jnp.inf)
            blocks = [s[:, c * LANES:(c + 1) * LANES] for c in range(nc // LANES)]
            bmax = functools.reduce(jnp.maximum, blocks)
            m_old = m_sc[h][rows, :]
            m_new = jnp.maximum(m_old, jnp.max(bmax, axis=-1, keepdims=True))
            alpha = jnp.exp2(m_old - m_new)
            ps = [jnp.exp2(b - m_new) for b in blocks]
            p = jnp.concatenate(ps, axis=1).astype(BF16)
            l_sc[h][rows, :] = alpha * l_sc[h][rows, :] + functools.reduce(jnp.add, ps)
            acc_sc[h][rows, :] = alpha * acc_sc[h][rows, :] + jnp.dot(p, v_ref[pl.ds(start, nc), lo:hi],
                                                                       preferred_element_type=F32)
            m_sc[h][rows, :] = m_new

    def body(j, carry):
        block(j, 0, t, 0, t, False)
        return carry

    lax.fori_loop(0, qi, body, 0)
    half = t // 2
    block(qi, 0, t, 0, half, True)
    block(qi, half, half, half, half, True)
    out = [acc_sc[h][...] / jnp.sum(l_sc[h][...], axis=-1, keepdims=True) for h in range(nh)]
    o_ref[...] = jnp.concatenate([out[h] + out[h + 1] for h in range(0, nh, 2)], axis=1).astype(BF16)


def _attention(q, k, v):
    t = T_ATTN
    nq = SEQ // t
    nh = HEADS_PER_STEP
    return pl.pallas_call(
        _attn_kernel,
        grid=(BATCH, N_HEADS // nh, nq),
        in_specs=[
            pl.BlockSpec((t, nh * HEAD_PAD), lambda b, hg, i: (b * nq + i, hg)),
            pl.BlockSpec((SEQ, nh * HEAD_PAD), lambda b, hg, i: (b, hg)),
            pl.BlockSpec((SEQ, nh * HEAD_PAD), lambda b, hg, i: (b, hg)),
        ],
        out_specs=pl.BlockSpec((t, nh * V_HEAD), lambda b, hg, i: (b * nq + i, hg)),
        out_shape=jax.ShapeDtypeStruct((TOKENS, N_HEADS * V_HEAD), BF16),
        scratch_shapes=[pltpu.VMEM((t, LANES), F32)] * (3 * nh),
        compiler_params=_cparams(("parallel", "parallel", "arbitrary"), VMEM_MB),
        name="mla_attention",
    )(q, k, v)


def _mixer_tail_kernel(gates_ref, sc_ref, sch_ref, gv_ref, gg_ref, gvh_ref, ggh_ref, att_ref, x_ref,
                       cwa_ref, woa_ref, cwc_ref, lng_ref, lnb_ref, woc_ref, wob_ref, wo_ref, gf_ref, wr_ref, br_ref,
                       out_ref, meta_ref, meta_t_ref, cnt_ref, xs0_ref, cbuf, ubuf, shifted, below, run_ref, zbuf, zsem):
    tm = TM_MIX
    step = pl.program_id(0)

    @pl.when(step == 0)
    def _():
        run_ref[...] = jnp.zeros_like(run_ref)
        zbuf[...] = jnp.zeros_like(zbuf)
        row_id = lax.broadcasted_iota(jnp.int32, (tm, tm), 0)
        col_id = lax.broadcasted_iota(jnp.int32, (tm, tm), 1)
        below[...] = jnp.where(row_id > col_id, 1.0, 0.0).astype(BF16)

    def zero_copy(c):
        rows = pl.ds(pl.multiple_of((step * ZERO_COPIES + c) * ZERO_ROWS, ZERO_ROWS), ZERO_ROWS)
        return pltpu.make_async_copy(zbuf, xs0_ref.at[rows, :], zsem.at[0])

    for c in range(ZERO_COPIES):
        zero_copy(c).start()
    has_past = (pl.program_id(0) % (SEQ // tm)) != 0

    sc = sc_ref[...]
    sc_b = sc[:, 0:SC_WIDTH].astype(F32)
    cbuf[8:8 + tm, :] = sc[:, SC_WIDTH:2 * SC_WIDTH].astype(F32) * sc[:, 2 * SC_WIDTH:].astype(F32)
    sch = sch_ref[...].astype(F32)[HALO_A - 8:HALO_A, :]
    cbuf[0:8, :] = jnp.where(has_past, sch[:, SC_WIDTH:2 * SC_WIDTH] * sch[:, 2 * SC_WIDTH:], 0.0)
    conv_a = cwa_ref[0:1, :] * cbuf[6:6 + tm, :]
    for t in range(1, SC_KERNEL):
        conv_a = conv_a + cwa_ref[t:t + 1, :] * cbuf[6 + t:6 + t + tm, :]
    y_a = jnp.dot((sc_b * conv_a).astype(BF16), woa_ref[...], preferred_element_type=F32)

    ubuf[HALO_C:HALO_C + tm, :] = gv_ref[...].astype(F32) * _sigmoid(gg_ref[...].astype(F32))
    ubuf[0:HALO_C, :] = jnp.where(has_past, gvh_ref[...].astype(F32) * _sigmoid(ggh_ref[...].astype(F32)), 0.0)
    rows = tm + HALO_C - 8
    u_all = ubuf[...]
    for b in range(1, 8):
        shifted[b - 1, 0:rows, :] = pltpu.roll(u_all, tm + HALO_C - b, axis=0)[0:rows, :]
    base = HALO_C - (CONF_KERNEL - 1)
    acc = None
    for t in range(CONF_KERNEL):
        off = base + t
        a0 = off - off % 8
        src = ubuf[a0:a0 + tm, :] if off % 8 == 0 else shifted[off % 8 - 1, a0:a0 + tm, :]
        term = cwc_ref[t:t + 1, :] * src
        acc = term if acc is None else acc + term
    mu = jnp.mean(acc, axis=-1, keepdims=True)
    xc = acc - mu
    var = jnp.mean(xc * xc, axis=-1, keepdims=True)
    y = xc * lax.rsqrt(var + EPS) * lng_ref[...] + lnb_ref[...]
    y_c = jnp.dot((y * _sigmoid(y)).astype(BF16), woc_ref[...], preferred_element_type=F32)

    y_b = jnp.dot(att_ref[...], wob_ref[...], preferred_element_type=F32)

    g = gates_ref[...]
    merged = (_sigmoid(g[:, 0:D_MODEL].astype(F32)) * y_a
              + _sigmoid(g[:, D_MODEL:2 * D_MODEL].astype(F32)) * y_b
              + _sigmoid(g[:, 2 * D_MODEL:].astype(F32)) * y_c)
    x_new = x_ref[...] + jnp.dot(merged.astype(BF16), wo_ref[...], preferred_element_type=F32)
    out_ref[...] = x_new
    _route_tile(x_new, gf_ref, wr_ref, br_ref, below, run_ref, meta_ref, meta_t_ref, cnt_ref)
    for c in range(ZERO_COPIES):
        zero_copy(c).wait()


def _mixer_tail(proj, tail, att, x, cwa, woa, cwc, lng, lnb, woc, wob, wo, gf, wr, br):
    tm = TM_MIX
    row = lambda width, col: pl.BlockSpec((tm, width), lambda i: (i, col // width))
    halo = lambda rows, width, col: pl.BlockSpec(
        (rows, width), lambda i: (jnp.maximum(i * (tm // rows) - 1, 0), col // width))
    const = lambda a: pl.BlockSpec(a.shape, lambda i: (0,) * a.ndim)
    weights = (cwa, woa, cwc, lng, lnb, woc, wob, wo, gf, wr, br)
    return pl.pallas_call(
        _mixer_tail_kernel,
        grid=(TOKENS // tm,),
        in_specs=[
            row(3 * D_MODEL, COL_GATES),
            row(3 * SC_WIDTH, COL_SC),
            halo(HALO_A, 3 * SC_WIDTH, COL_SC),
            row(CONF_WIDTH, TCOL_GLU),
            row(CONF_WIDTH, TCOL_GLU + CONF_WIDTH),
            halo(HALO_C, CONF_WIDTH, TCOL_GLU),
            halo(HALO_C, CONF_WIDTH, TCOL_GLU + CONF_WIDTH),
            pl.BlockSpec((tm, N_HEADS * V_HEAD), lambda i: (i, 0)),
            pl.BlockSpec((tm, D_MODEL), lambda i: (i, 0)),
        ] + [const(a) for a in weights],
        out_specs=[
            pl.BlockSpec((tm, D_MODEL), lambda i: (i, 0)),
            pl.BlockSpec((tm, LANES), lambda i: (i, 0)),
            pl.BlockSpec((META_ROWS, tm), lambda i: (0, i)),
            pl.BlockSpec((1, LANES), lambda i: (0, 0)),
            pl.BlockSpec(memory_space=pl.ANY),
        ],
        out_shape=(
            jax.ShapeDtypeStruct((TOKENS, D_MODEL), F32),
            jax.ShapeDtypeStruct((TOKENS, LANES), F32),
            jax.ShapeDtypeStruct((META_ROWS, TOKENS), F32),
            jax.ShapeDtypeStruct((1, LANES), F32),
            jax.ShapeDtypeStruct((SORTED_ROWS, LANES), F32),
        ),
        scratch_shapes=[
            pltpu.VMEM((tm + 8, SC_WIDTH), F32),
            pltpu.VMEM((tm + HALO_C, CONF_WIDTH), F32),
            pltpu.VMEM((7, tm + HALO_C - 8, CONF_WIDTH), F32),
            pltpu.VMEM((tm, tm), BF16),
            pltpu.VMEM((1, LANES), F32),
            pltpu.VMEM((ZERO_ROWS, LANES), F32),
            pltpu.SemaphoreType.DMA((1,)),
        ],
        compiler_params=_cparams(("arbitrary",), VMEM_BIG_MB),
        name="mixer_tail",
    )(proj, proj, proj, tail, tail, tail, tail, att, x, *weights)


def _route(logits):
    lane = lax.broadcasted_iota(jnp.int32, logits.shape, 1)
    lane_f = lane.astype(F32)
    neg = -jnp.inf
    big = float(LANES)
    is_grp = (lane >= ROUTER_GROUP_LANE) & (lane < ROUTER_GROUP_LANE + N_GROUPS)
    glog = jnp.where(is_grp, logits, neg)
    gmax = jnp.max(glog, axis=-1, keepdims=True)
    gidx = jnp.min(jnp.where(glog == gmax, lane_f, big), axis=-1, keepdims=True)
    p_sel = 1.0 / jnp.sum(jnp.exp(glog - gmax), axis=-1, keepdims=True)
    first = (gidx - ROUTER_GROUP_LANE) * EXPERTS_PER_GROUP
    in_grp = (lane_f >= first) & (lane_f < first + EXPERTS_PER_GROUP)
    el = jnp.where(in_grp, logits, neg)
    m1 = jnp.max(el, axis=-1, keepdims=True)
    i1 = jnp.min(jnp.where(el == m1, lane_f, big), axis=-1, keepdims=True)
    el2 = jnp.where(lane_f == i1, neg, el)
    m2 = jnp.max(el2, axis=-1, keepdims=True)
    i2 = jnp.min(jnp.where(el2 == m2, lane_f, big), axis=-1, keepdims=True)
    e2 = jnp.exp(m2 - m1)
    w1 = p_sel / (1.0 + e2)
    w2 = w1 * e2
    return i1, i2, w1, w2


def _route_tile(x, g_ref, wr_ref, br_ref, below_ref, run_ref, meta_ref, meta_t_ref, cnt_ref):
    tm = x.shape[0]
    h = _rms(x, g_ref[...])
    h_hi = h.astype(BF16)
    h_lo = (h - h_hi.astype(F32)).astype(BF16)
    hi_terms = jnp.dot(h_hi, wr_ref[...], preferred_element_type=F32)
    logits = (hi_terms[:, :LANES] + hi_terms[:, LANES:]
              + jnp.dot(h_lo, wr_ref[:, :LANES], preferred_element_type=F32) + br_ref[...])
    i1, i2, w1, w2 = _route(logits)
    lane = lax.broadcasted_iota(jnp.int32, (tm, LANES), 1)
    lane_f = lane.astype(F32)
    oh1 = lane_f == i1
    oh2 = lane_f == i2
    onehot = jnp.where(oh1, 1.0, 0.0) + jnp.where(oh2, 1.0, 0.0)
    before = run_ref[...] + jnp.dot(below_ref[...], onehot.astype(BF16), preferred_element_type=F32)
    r1 = jnp.sum(jnp.where(oh1, before, 0.0), axis=-1, keepdims=True)
    r2 = jnp.sum(jnp.where(oh2, before, 0.0), axis=-1, keepdims=True)
    run_ref[...] += jnp.sum(onehot, axis=0, keepdims=True)
    cnt_ref[...] = run_ref[...]
    meta = jnp.zeros((tm, LANES), F32)
    for col, val in enumerate((i1, i2, r1, r2, w1, w2)):
        meta = jnp.where(lane == col, val, meta)
    meta_ref[...] = meta
    meta_t_ref[...] = meta.T[0:META_ROWS, :]


def _dispatch_plan(meta_t, counts):
    e1 = meta_t[META_E1].astype(jnp.int32)
    e2 = meta_t[META_E2].astype(jnp.int32)
    r1 = meta_t[META_R1].astype(jnp.int32)
    r2 = meta_t[META_R2].astype(jnp.int32)
    cnt = counts[0, :N_EXPERTS].astype(jnp.int32)
    tiles = (cnt + TM_EXP - 1) // TM_EXP
    tile_end = jnp.cumsum(tiles)
    first_slot = ((tile_end - tiles) * TM_EXP)[:, None]
    expert = jnp.arange(N_EXPERTS, dtype=jnp.int32)[:, None]
    pos1 = jnp.sum(jnp.where(e1[None, :] == expert, first_slot, 0), axis=0) + r1
    pos2 = jnp.sum(jnp.where(e2[None, :] == expert, first_slot, 0), axis=0) + r2
    n_tiles = tile_end[-1:]
    tile_id = jnp.minimum(jnp.arange(MAX_TILES, dtype=jnp.int32), n_tiles - 1)
    tile_expert = jnp.sum((tile_id[:, None] >= tile_end[None, :]).astype(jnp.int32), axis=1)
    return pos1, pos2, tile_expert, n_tiles


def _dispatch_kernel(pos1_ref, pos2_ref, x_ref, g_ref, xs_in_ref, xs_ref, slab, sem):
    del xs_in_ref
    tm = TM_DISP
    i = pl.program_id(0)
    last = pl.num_programs(0) - 1
    slot = i % 2

    def wait_copies(sl):
        for _ in range(TOP_K):
            pltpu.make_async_copy(slab.at[sl], xs_ref.at[pl.ds(0, tm * SLAB), :], sem.at[sl]).wait()

    @pl.when(i >= 2)
    def _():
        wait_copies(slot)

    h = _rms(x_ref[...], g_ref[...])
    for s in range(SLAB):
        slab[slot, pl.ds(s, tm, stride=SLAB), :] = h[:, s * LANES:(s + 1) * LANES]

    def body(c, carry):
        for u in range(GATHER_UNROLL):
            r = c * GATHER_UNROLL + u
            src = slab.at[slot, pl.ds(pl.multiple_of(r * SLAB, SLAB), SLAB), :]
            for k, pos_ref in enumerate((pos1_ref, pos2_ref)):
                p = pos_ref[i * tm + r]
                pltpu.make_async_copy(src, xs_ref.at[pl.ds(pl.multiple_of(p * SLAB, SLAB), SLAB), :],
                                      sem.at[slot]).start(priority=k % N_DMA_PRIORITIES)
        return carry

    lax.fori_loop(0, tm // GATHER_UNROLL, body, 0)

    @pl.when(i == last)
    def _():
        wait_copies(1 - slot)
        wait_copies(slot)


def _dispatch(pos1, pos2, x, g, xs0):
    tm = TM_DISP
    return pl.pallas_call(
        _dispatch_kernel,
        grid_spec=pltpu.PrefetchScalarGridSpec(
            num_scalar_prefetch=2,
            grid=(TOKENS // tm,),
            in_specs=[
                pl.BlockSpec((tm, D_MODEL), lambda i, p1, p2: (i, 0)),
                pl.BlockSpec((1, D_MODEL), lambda i, p1, p2: (0, 0)),
                pl.BlockSpec(memory_space=pl.ANY),
            ],
            out_specs=pl.BlockSpec(memory_space=pl.ANY),
            scratch_shapes=[pltpu.VMEM((2, tm * SLAB, LANES), F32), pltpu.SemaphoreType.DMA((2,))],
        ),
        out_shape=jax.ShapeDtypeStruct((SORTED_ROWS, LANES), F32),
        input_output_aliases={4: 0},
        compiler_params=_cparams(("arbitrary",), VMEM_MB),
        name="moe_dispatch",
    )(pos1, pos2, x, g, xs0)


def _expert_kernel(te_ref, nt_ref, xs_ref, wg_ref, wu_ref, wd_ref, ys_ref, xbuf, sem):
    del te_ref
    tm = TM_EXP
    i = pl.program_id(0)
    n = nt_ref[0]

    def tile_copy(tile):
        slot = tile % EXPERT_SLOTS
        rows = pl.ds(pl.multiple_of(tile * (tm * SLAB), tm * SLAB), tm * SLAB)
        return pltpu.make_async_copy(xs_ref.at[rows, :], xbuf.at[slot], sem.at[slot])

    @pl.when(i == 0)
    def _():
        for ahead in range(EXPERT_PREFETCH):
            @pl.when(ahead < n)
            def _():
                tile_copy(ahead).start()

    @pl.when(i + EXPERT_PREFETCH < n)
    def _():
        tile_copy(i + EXPERT_PREFETCH).start()

    @pl.when(i < n)
    def _():
        tile_copy(i).wait()
        slot = i % EXPERT_SLOTS
        xt = jnp.concatenate([xbuf[slot, pl.ds(s, tm, stride=SLAB), :].astype(BF16) for s in range(SLAB)], axis=1)
        hg = jnp.dot(xt, wg_ref[...].astype(BF16), preferred_element_type=F32)
        hu = jnp.dot(xt, wu_ref[...].astype(BF16), preferred_element_type=F32)
        hh = (hg * _sigmoid(hg) * hu).astype(BF16)
        y = jnp.dot(hh, wd_ref[...].astype(BF16), preferred_element_type=F32)
        for s in range(SLAB):
            ys_ref[pl.ds(s, tm, stride=SLAB), :] = y[:, s * LANES:(s + 1) * LANES]

    @pl.when(i >= n)
    def _():
        ys_ref[...] = jnp.zeros_like(ys_ref)


def _experts(layer, tile_expert, n_tiles, xs, w_gate, w_up, w_down):
    tm = TM_EXP
    wspec = lambda rows, cols: pl.BlockSpec((None, None, rows, cols), lambda i, te, nt: (layer, te[i], 0, 0))
    return pl.pallas_call(
        _expert_kernel,
        grid_spec=pltpu.PrefetchScalarGridSpec(
            num_scalar_prefetch=2,
            grid=(MAX_TILES,),
            in_specs=[
                pl.BlockSpec(memory_space=pl.ANY),
                wspec(D_MODEL, EXPERT_HIDDEN),
                wspec(D_MODEL, EXPERT_HIDDEN),
                wspec(EXPERT_HIDDEN, D_MODEL),
            ],
            out_specs=pl.BlockSpec((tm * SLAB, LANES), lambda i, te, nt: (i, 0)),
            scratch_shapes=[pltpu.VMEM((EXPERT_SLOTS, tm * SLAB, LANES), F32), pltpu.SemaphoreType.DMA((EXPERT_SLOTS,))],
        ),
        out_shape=jax.ShapeDtypeStruct((SORTED_ROWS, LANES), F32),
        compiler_params=_cparams(("arbitrary",), VMEM_MB),
        name="moe_experts",
    )(tile_expert, n_tiles, xs, w_gate, w_up, w_down)


def _start_slab_gathers(idx_refs, base, n_rows, src_hbm, dst_bufs, sem):
    def body(c, carry):
        for u in range(GATHER_UNROLL):
            r = c * GATHER_UNROLL + u
            for k, (idx_ref, dst) in enumerate(zip(idx_refs, dst_bufs)):
                t = idx_ref[base + r]
                pltpu.make_async_copy(src_hbm.at[pl.ds(pl.multiple_of(t * SLAB, SLAB), SLAB), :],
                                      dst.at[pl.ds(pl.multiple_of(r * SLAB, SLAB), SLAB), :],
                                      sem).start(priority=k % N_DMA_PRIORITIES)
        return carry

    lax.fori_loop(0, n_rows // GATHER_UNROLL, body, 0)


def _wait_slab_gathers(n_rows, src_hbm, dst, sem):
    pltpu.make_async_copy(src_hbm.at[pl.ds(0, n_rows * SLAB), :], dst, sem).wait()


def _combine_ple_kernel(pos1_ref, pos2_ref, x_ref, meta_ref, ys_ref, p_ref, g_ref, wg_ref, wp_ref, fg_ref,
                        out_ref, cbuf, sem, *, final):
    tm = TM_COMB
    i = pl.program_id(0)

    def gather(tile, slot):
        _start_slab_gathers((pos1_ref, pos2_ref), tile * tm, tm, ys_ref, (cbuf.at[slot, 0], cbuf.at[slot, 1]),
                            sem.at[slot])

    @pl.when(i == 0)
    def _():
        gather(0, 0)

    @pl.when(i + 1 < pl.num_programs(0))
    def _():
        gather(i + 1, (i + 1) % 2)

    slot = i % 2
    for k in range(2):
        _wait_slab_gathers(tm, ys_ref, cbuf.at[slot, k], sem.at[slot])
    meta = meta_ref[...]
    w1 = meta[:, META_W1:META_W1 + 1]
    w2 = meta[:, META_W2:META_W2 + 1]
    moe = jnp.concatenate([w1 * cbuf[slot, 0, pl.ds(s, tm, stride=SLAB), :]
                           + w2 * cbuf[slot, 1, pl.ds(s, tm, stride=SLAB), :] for s in range(SLAB)], axis=1)
    x = x_ref[...] + moe
    h = _rms(x, g_ref[...]).astype(BF16)
    gate = _sigmoid(jnp.dot(h, wg_ref[...], preferred_element_type=F32))
    emb = jnp.dot(p_ref[...].astype(BF16), wp_ref[...], preferred_element_type=F32)
    y = x + gate * emb
    if final:
        y = _rms(y, fg_ref[...])
    out_ref[...] = y


def _combine_ple(layer, pos1, pos2, x, meta, ys, p, g, wg, wp, fg, final):
    tm = TM_COMB
    const = lambda i, p1, p2: (0, 0)
    rows = lambda i, p1, p2: (i, 0)
    return pl.pallas_call(
        functools.partial(_combine_ple_kernel, final=final),
        grid_spec=pltpu.PrefetchScalarGridSpec(
            num_scalar_prefetch=2,
            grid=(TOKENS // tm,),
            in_specs=[
                pl.BlockSpec((tm, D_MODEL), rows),
                pl.BlockSpec((tm, LANES), rows),
                pl.BlockSpec(memory_space=pl.ANY),
                pl.BlockSpec((None, tm, PLE_DIM), lambda i, p1, p2: (layer, i, 0)),
                pl.BlockSpec((1, D_MODEL), const),
                pl.BlockSpec((D_MODEL, D_MODEL), const),
                pl.BlockSpec((PLE_DIM, D_MODEL), const),
                pl.BlockSpec((1, D_MODEL), const),
            ],
            out_specs=pl.BlockSpec((tm, D_MODEL), rows),
            scratch_shapes=[pltpu.VMEM((2, 2, tm * SLAB, LANES), F32), pltpu.SemaphoreType.DMA((2,))],
        ),
        out_shape=jax.ShapeDtypeStruct((TOKENS, D_MODEL), F32),
        compiler_params=_cparams(("arbitrary",), VMEM_MB),
        name="moe_combine_ple",
    )(pos1, pos2, x, meta, ys, p, g, wg, wp, fg)


def _in_proj_weights(w_in):
    k_rope = w_in[:, MAIN_COLS:MAIN_COLS + QK_ROPE]
    glu = w_in[:, MAIN_COLS + QK_ROPE:]
    half = QK_ROPE // 2
    zn = jnp.zeros((D_MODEL, QK_NOPE), F32)
    zp = jnp.zeros((D_MODEL, HEAD_PAD - QK_NOPE - QK_ROPE), F32)
    tail = jnp.concatenate([glu, zn, k_rope, zp, zn, k_rope[:, half:], k_rope[:, :half], zp], axis=1)
    return w_in[:, :MAIN_COLS].astype(BF16), tail.astype(BF16)


def _q_weight(w_uq):
    scale = (QK_NOPE + QK_ROPE) ** -0.5 * LOG2_E
    w = (w_uq * scale).reshape(Q_LORA, N_HEADS, QK_NOPE + QK_ROPE)
    zp = jnp.zeros((Q_LORA, N_HEADS, HEAD_PAD - QK_NOPE - QK_ROPE), F32)
    return jnp.concatenate([w, zp], axis=2).reshape(Q_LORA, N_HEADS * HEAD_PAD).astype(BF16)


def _kv_weight(w_ukv):
    w = w_ukv.reshape(KV_LORA, N_HEADS, QK_NOPE + V_HEAD)
    k_nope, v = w[:, :, :QK_NOPE], w[:, :, QK_NOPE:]
    z = jnp.zeros_like(v)
    k_part = jnp.concatenate([k_nope, jnp.zeros_like(k_nope)], axis=2).reshape(KV_LORA, N_HEADS * HEAD_PAD)
    odd = (jnp.arange(N_HEADS) % 2 == 1)[None, :, None]
    v_part = jnp.concatenate([jnp.where(odd, z, v), jnp.where(odd, v, z)], axis=2).reshape(KV_LORA, N_HEADS * HEAD_PAD)
    return jnp.concatenate([k_part, v_part], axis=1).astype(BF16)


def _router_weight(w_rg, b_rg, w_re, b_re):
    pad = LANES - N_EXPERTS - N_GROUPS
    w = jnp.concatenate([w_re, w_rg, jnp.zeros((D_MODEL, pad), F32)], axis=1)
    b = jnp.concatenate([b_re, b_rg, jnp.zeros((pad,), F32)]).reshape(1, LANES)
    w_hi = w.astype(BF16)
    w_lo = (w - w_hi.astype(F32)).astype(BF16)
    return jnp.concatenate([w_hi, w_lo], axis=1), b


def kernel(x, p, positions, ln_mix_g, w_in, conv_a_w, w_out_a, q_norm_g, w_uq, kv_norm_g, w_ukv, w_out_b, conv_c_w, ln_c_g, ln_c_b, w_out_c, w_o, ln_ffn_g, w_route_grp, b_route_grp, w_route_exp, b_route_exp, w_exp_gate, w_exp_up, w_exp_down, ln_ple_g, w_ple_gate, w_ple, final_norm_g):
    c_tab, s_tab = _rope_tables(positions)
    xf = x.reshape(TOKENS, D_MODEL)
    pf = p.reshape(DEPTH, TOKENS, PLE_DIM)
    row = lambda a: a.reshape(1, -1)
    for i in range(DEPTH):
        proj, tail = _inproj(xf, row(ln_mix_g[i]), *_in_proj_weights(w_in[i]))
        q, k, v = _qkv(proj, tail, c_tab, s_tab, row(q_norm_g[i]), row(kv_norm_g[i]), _q_weight(w_uq[i]), _kv_weight(w_ukv[i]))
        att = _attention(q, k, v)
        wr, br = _router_weight(w_route_grp[i], b_route_grp[i], w_route_exp[i], b_route_exp[i])
        xf, meta, meta_t, counts, xs0 = _mixer_tail(
            proj, tail, att, xf, conv_a_w[i], w_out_a[i].astype(BF16), conv_c_w[i], row(ln_c_g[i]), row(ln_c_b[i]),
            w_out_c[i].astype(BF16), w_out_b[i].astype(BF16), w_o[i].astype(BF16), row(ln_ffn_g[i]), wr, br)
        pos1, pos2, tile_expert, n_tiles = _dispatch_plan(meta_t, counts)
        xs = _dispatch(pos1, pos2, xf, row(ln_ffn_g[i]), xs0)
        ys = _experts(i, tile_expert, n_tiles, xs, w_exp_gate, w_exp_up, w_exp_down)
        xf = _combine_ple(i, pos1, pos2, xf, meta, ys, pf, row(ln_ple_g[i]), w_ple_gate[i].astype(BF16),
                          w_ple[i].astype(BF16), row(final_norm_g), final=(i == DEPTH - 1))
    return xf.reshape(BATCH, SEQ, D_MODEL)
```

```python
import functools

import jax
import jax.numpy as jnp
from jax import lax
from jax.experimental import pallas as pl
from jax.experimental.pallas import tpu as pltpu

D_MODEL = 1024
BATCH = 8
SEQ = 2048
DEPTH = 2
TOKENS = BATCH * SEQ
PLE_DIM = 256
SC_WIDTH = 512
SC_KERNEL = 3
N_HEADS = 8
QK_NOPE = 64
QK_ROPE = 32
V_HEAD = 64
Q_LORA = 768
KV_LORA = 256
ROPE_THETA = 10000.0
CONF_WIDTH = 512
CONF_KERNEL = 31
N_GROUPS = 4
EXPERTS_PER_GROUP = 8
N_EXPERTS = N_GROUPS * EXPERTS_PER_GROUP
EXPERT_HIDDEN = 256
EPS = 1e-6
LOG2_E = 1.4426950408889634

LANES = 128
HEAD_PAD = 128
F32 = jnp.float32
BF16 = jnp.bfloat16

COL_GATES = 0
COL_SC = 3 * D_MODEL
COL_QLAT = COL_SC + 3 * SC_WIDTH
COL_KVLAT = COL_QLAT + Q_LORA
MAIN_COLS = COL_KVLAT + KV_LORA
TCOL_GLU = 0
TCOL_KR = 2 * CONF_WIDTH
TCOL_KRSW = TCOL_KR + HEAD_PAD
TAIL_COLS = TCOL_KRSW + HEAD_PAD

ROUTER_GROUP_LANE = N_EXPERTS

MXU_TILE = 256
VMEM_MB = 48
VMEM_BIG_MB = 56

TM_INPROJ = 512
TN_INPROJ = 6 * MXU_TILE
TM_QKV = 512
T_ATTN = 512
HEADS_PER_STEP = 8
TM_MIX = 512
HALO_C = 32
HALO_A = 16
TM_EXP = 512
EXPERT_PREFETCH = 2
EXPERT_SLOTS = EXPERT_PREFETCH + 1
TM_DISP = 512
TM_COMB = 256
TOP_K = 2
MAX_TILES = TOKENS * TOP_K // TM_EXP + N_EXPERTS
SLAB = D_MODEL // LANES
GATHER_UNROLL = 16
N_DMA_PRIORITIES = 2
META_E1, META_E2, META_R1, META_R2, META_W1, META_W2 = range(6)
META_ROWS = 8
SORTED_ROWS = MAX_TILES * TM_EXP * SLAB
ZERO_COPIES = 6
ZERO_ROWS = SORTED_ROWS // (TOKENS // TM_MIX) // ZERO_COPIES


def _cparams(semantics, vmem_mb):
    return pltpu.CompilerParams(dimension_semantics=semantics, vmem_limit_bytes=vmem_mb * 1024 * 1024)


def _sigmoid(x):
    return 1.0 / (1.0 + jnp.exp2(x * -LOG2_E))


def _rms(x, g):
    return x * lax.rsqrt(jnp.mean(x * x, axis=-1, keepdims=True) + EPS) * g


def _place(x, onehot):
    x1 = x.astype(BF16)
    r1 = x - x1.astype(F32)
    x2 = r1.astype(BF16)
    x3 = (r1 - x2.astype(F32)).astype(BF16)
    return (jnp.dot(x1, onehot, preferred_element_type=F32) + jnp.dot(x2, onehot, preferred_element_type=F32)
            + jnp.dot(x3, onehot, preferred_element_type=F32))


def _rope_kernel(pos_ref, freq_ref, c_ref, s_ref):
    half = QK_ROPE // 2
    per_row = LANES // half
    rows = TOKENS // per_row
    ang = pos_ref[...].astype(F32) * freq_ref[...]
    cos = jnp.cos(ang)
    sin = jnp.sin(ang)
    src = lax.broadcasted_iota(jnp.int32, (LANES, LANES), 0)
    dst = lax.broadcasted_iota(jnp.int32, (LANES, LANES), 1)
    lane = lax.broadcasted_iota(jnp.int32, (1, LANES), 1)
    ones_nope = jnp.where(lane < QK_NOPE, 1.0, 0.0)
    sign = jnp.where(lane < QK_NOPE + half, -1.0, 1.0)
    for j in range(per_row):
        f = src - half * j
        hit = (dst == QK_NOPE + f) | (dst == QK_NOPE + half + f)
        onehot = jnp.where((f >= 0) & (f < half) & hit, 1.0, 0.0).astype(BF16)
        c_ref[pl.ds(j, rows, stride=per_row), :] = _place(cos, onehot) + ones_nope
        s_ref[pl.ds(j, rows, stride=per_row), :] = _place(sin, onehot) * sign


def _rope_tables(positions):
    half = QK_ROPE // 2
    inv_freq = ROPE_THETA ** (-jnp.arange(0, QK_ROPE, 2, dtype=F32) / QK_ROPE)
    rows = TOKENS * half // LANES
    pos_rep = jnp.broadcast_to(positions.reshape(TOKENS, 1), (TOKENS, half)).reshape(rows, LANES)
    freq = jnp.tile(inv_freq, LANES // half).reshape(1, LANES)
    return pl.pallas_call(
        _rope_kernel,
        out_shape=(jax.ShapeDtypeStruct((TOKENS, HEAD_PAD), F32),) * 2,
        compiler_params=_cparams(None, VMEM_MB),
        name="rope_tables",
    )(pos_rep, freq)


def _inproj_kernel(x_ref, g_ref, wm_ref, wt_ref, om_ref, ot_ref):
    h = _rms(x_ref[...], g_ref[...]).astype(BF16)
    for lo in range(0, MAIN_COLS, TN_INPROJ):
        hi = min(lo + TN_INPROJ, MAIN_COLS)
        om_ref[:, lo:hi] = jnp.dot(h, wm_ref[:, lo:hi], preferred_element_type=F32).astype(BF16)
    ot_ref[...] = jnp.dot(h, wt_ref[...], preferred_element_type=F32).astype(BF16)


def _inproj(x, g, w_main, w_tail):
    tm = TM_INPROJ
    return pl.pallas_call(
        _inproj_kernel,
        grid=(TOKENS // tm,),
        in_specs=[
            pl.BlockSpec((tm, D_MODEL), lambda i: (i, 0)),
            pl.BlockSpec((1, D_MODEL), lambda i: (0, 0)),
            pl.BlockSpec((D_MODEL, MAIN_COLS), lambda i: (0, 0)),
            pl.BlockSpec((D_MODEL, TAIL_COLS), lambda i: (0, 0)),
        ],
        out_specs=[
            pl.BlockSpec((tm, MAIN_COLS), lambda i: (i, 0)),
            pl.BlockSpec((tm, TAIL_COLS), lambda i: (i, 0)),
        ],
        out_shape=(jax.ShapeDtypeStruct((TOKENS, MAIN_COLS), BF16), jax.ShapeDtypeStruct((TOKENS, TAIL_COLS), BF16)),
        compiler_params=_cparams(("parallel",), VMEM_BIG_MB),
        name="in_proj",
    )(x, g, w_main, w_tail)


def _qkv_kernel(ql_ref, kvl_ref, kr_ref, krsw_ref, c_ref, s_ref, qg_ref, kvg_ref, wq_ref, wkv_ref,
                q_out, k_out, v_out):
    c = c_ref[...]
    s = s_ref[...]
    width = N_HEADS * HEAD_PAD
    half = QK_ROPE // 2
    low_half = lax.broadcasted_iota(jnp.int32, c.shape, 1) < QK_NOPE + half

    def swap_halves(x):
        return jnp.where(low_half, pltpu.roll(x, HEAD_PAD - half, axis=1), pltpu.roll(x, half, axis=1))

    qn = _rms(ql_ref[...].astype(F32), qg_ref[...]).astype(BF16)
    qq = jnp.dot(qn, wq_ref[...], preferred_element_type=F32)
    for h in range(N_HEADS):
        lo, hi = h * HEAD_PAD, (h + 1) * HEAD_PAD
        q_out[:, lo:hi] = (qq[:, lo:hi] * c + swap_halves(qq[:, lo:hi]) * s).astype(BF16)
    kvn = _rms(kvl_ref[...].astype(F32), kvg_ref[...]).astype(BF16)
    kk = jnp.dot(kvn, wkv_ref[...], preferred_element_type=F32)
    kr = kr_ref[...].astype(F32) * c + krsw_ref[...].astype(F32) * s
    for h in range(N_HEADS):
        lo, hi = h * HEAD_PAD, (h + 1) * HEAD_PAD
        k_out[:, lo:hi] = (kk[:, lo:hi] + kr).astype(BF16)
    v_out[...] = kk[:, width:].astype(BF16)


def _qkv(proj, tail, c_tab, s_tab, qg, kvg, wq, wkv):
    tm = TM_QKV
    width = N_HEADS * HEAD_PAD
    row = lambda blk: (lambda i: (i, blk))
    const = lambda i: (0, 0)
    return pl.pallas_call(
        _qkv_kernel,
        grid=(TOKENS // tm,),
        in_specs=[
            pl.BlockSpec((tm, Q_LORA), row(COL_QLAT // Q_LORA)),
            pl.BlockSpec((tm, KV_LORA), row(COL_KVLAT // KV_LORA)),
            pl.BlockSpec((tm, HEAD_PAD), row(TCOL_KR // HEAD_PAD)),
            pl.BlockSpec((tm, HEAD_PAD), row(TCOL_KRSW // HEAD_PAD)),
            pl.BlockSpec((tm, HEAD_PAD), row(0)),
            pl.BlockSpec((tm, HEAD_PAD), row(0)),
            pl.BlockSpec((1, Q_LORA), const),
            pl.BlockSpec((1, KV_LORA), const),
            pl.BlockSpec((Q_LORA, width), const),
            pl.BlockSpec((KV_LORA, 2 * width), const),
        ],
        out_specs=[pl.BlockSpec((tm, width), row(0))] * 3,
        out_shape=(jax.ShapeDtypeStruct((TOKENS, width), BF16),) * 3,
        compiler_params=_cparams(("parallel",), VMEM_MB),
        name="qkv_prep",
    )(proj, proj, tail, tail, c_tab, s_tab, qg, kvg, wq, wkv)


def _attn_kernel(q_ref, k_ref, v_ref, o_ref, *state):
    t = T_ATTN
    nh = HEADS_PER_STEP
    qi = pl.program_id(2)
    nt = (((1,), (1,)), ((), ()))
    m_sc, l_sc, acc_sc = state[0:nh], state[nh:2 * nh], state[2 * nh:3 * nh]
    for h in range(nh):
        m_sc[h][...] = jnp.full((t, LANES), -jnp.inf, F32)
        l_sc[h][...] = jnp.zeros((t, LANES), F32)
        acc_sc[h][...] = jnp.zeros((t, LANES), F32)

    def block(j, r0, nr, c0, nc, masked):
        start = pl.multiple_of(j * t + c0, nc)
        rows = slice(r0, r0 + nr)
        for h in range(nh):
            lo, hi = h * HEAD_PAD, (h + 1) * HEAD_PAD
            s = lax.dot_general(q_ref[rows, lo:hi], k_ref[pl.ds(start, nc), lo:hi], nt, preferred_element_type=F32)
            if masked:
                row_id = r0 + lax.broadcasted_iota(jnp.int32, (nr, nc), 0)
                col_id = c0 + lax.broadcasted_iota(jnp.int32, (nr, nc), 1)
                s = jnp.where(row_id >= col_id, s, -jnp.inf)
            blocks = [s[:, c * LANES:(c + 1) * LANES] for c in range(nc // LANES)]
            bmax = functools.reduce(jnp.maximum, blocks)
            m_old = m_sc[h][rows, :]
            m_new = jnp.maximum(m_old, jnp.max(bmax, axis=-1, keepdims=True))
            alpha = jnp.exp2(m_old - m_new)
            ps = [jnp.exp2(b - m_new) for b in blocks]
            p = jnp.concatenate(ps, axis=1).astype(BF16)
            l_sc[h][rows, :] = alpha * l_sc[h][rows, :] + functools.reduce(jnp.add, ps)
            acc_sc[h][rows, :] = alpha * acc_sc[h][rows, :] + jnp.dot(p, v_ref[pl.ds(start, nc), lo:hi],
                                                                       preferred_element_type=F32)
            m_sc[h][rows, :] = m_new

    def body(j, carry):
        block(j, 0, t, 0, t, False)
        return carry

    lax.fori_loop(0, qi, body, 0)
    half = t // 2
    block(qi, 0, t, 0, half, True)
    block(qi, half, half, half, half, True)
    out = [acc_sc[h][...] / jnp.sum(l_sc[h][...], axis=-1, keepdims=True) for h in range(nh)]
    o_ref[...] = jnp.concatenate([out[h] + out[h + 1] for h in range(0, nh, 2)], axis=1).astype(BF16)


def _attention(q, k, v):
    t = T_ATTN
    nq = SEQ // t
    nh = HEADS_PER_STEP
    return pl.pallas_call(
        _attn_kernel,
        grid=(BATCH, N_HEADS // nh, nq),
        in_specs=[
            pl.BlockSpec((t, nh * HEAD_PAD), lambda b, hg, i: (b * nq + i, hg)),
            pl.BlockSpec((SEQ, nh * HEAD_PAD), lambda b, hg, i: (b, hg)),
            pl.BlockSpec((SEQ, nh * HEAD_PAD), lambda b, hg, i: (b, hg)),
        ],
        out_specs=pl.BlockSpec((t, nh * V_HEAD), lambda b, hg, i: (b * nq + i, hg)),
        out_shape=jax.ShapeDtypeStruct((TOKENS, N_HEADS * V_HEAD), BF16),
        scratch_shapes=[pltpu.VMEM((t, LANES), F32)] * (3 * nh),
        compiler_params=_cparams(("parallel", "parallel", "arbitrary"), VMEM_MB),
        name="mla_attention",
    )(q, k, v)


def _mixer_tail_kernel(gates_ref, sc_ref, sch_ref, gv_ref, gg_ref, gvh_ref, ggh_ref, att_ref, x_ref,
                       cwa_ref, woa_ref, cwc_ref, lng_ref, lnb_ref, woc_ref, wob_ref, wo_ref, gf_ref, wr_ref, br_ref,
                       out_ref, meta_ref, meta_t_ref, cnt_ref, xs0_ref, cbuf, ubuf, shifted, below, run_ref, zbuf, zsem):
    tm = TM_MIX
    step = pl.program_id(0)

    @pl.when(step == 0)
    def _():
        run_ref[...] = jnp.zeros_like(run_ref)
        zbuf[...] = jnp.zeros_like(zbuf)
        row_id = lax.broadcasted_iota(jnp.int32, (tm, tm), 0)
        col_id = lax.broadcasted_iota(jnp.int32, (tm, tm), 1)
        below[...] = jnp.where(row_id > col_id, 1.0, 0.0).astype(BF16)

    def zero_copy(c):
        rows = pl.ds(pl.multiple_of((step * ZERO_COPIES + c) * ZERO_ROWS, ZERO_ROWS), ZERO_ROWS)
        return pltpu.make_async_copy(zbuf, xs0_ref.at[rows, :], zsem.at[0])

    for c in range(ZERO_COPIES):
        zero_copy(c).start()
    has_past = (pl.program_id(0) % (SEQ // tm)) != 0

    sc = sc_ref[...]
    sc_b = sc[:, 0:SC_WIDTH].astype(F32)
    cbuf[8:8 + tm, :] = sc[:, SC_WIDTH:2 * SC_WIDTH].astype(F32) * sc[:, 2 * SC_WIDTH:].astype(F32)
    sch = sch_ref[...].astype(F32)[HALO_A - 8:HALO_A, :]
    cbuf[0:8, :] = jnp.where(has_past, sch[:, SC_WIDTH:2 * SC_WIDTH] * sch[:, 2 * SC_WIDTH:], 0.0)
    conv_a = cwa_ref[0:1, :] * cbuf[6:6 + tm, :]
    for t in range(1, SC_KERNEL):
        conv_a = conv_a + cwa_ref[t:t + 1, :] * cbuf[6 + t:6 + t + tm, :]
    y_a = jnp.dot((sc_b * conv_a).astype(BF16), woa_ref[...], preferred_element_type=F32)

    ubuf[HALO_C:HALO_C + tm, :] = gv_ref[...].astype(F32) * _sigmoid(gg_ref[...].astype(F32))
    ubuf[0:HALO_C, :] = jnp.where(has_past, gvh_ref[...].astype(F32) * _sigmoid(ggh_ref[...].astype(F32)), 0.0)
    rows = tm + HALO_C - 8
    u_all = ubuf[...]
    for b in range(1, 8):
        shifted[b - 1, 0:rows, :] = pltpu.roll(u_all, tm + HALO_C - b, axis=0)[0:rows, :]
    base = HALO_C - (CONF_KERNEL - 1)
    acc = None
    for t in range(CONF_KERNEL):
        off = base + t
        a0 = off - off % 8
        src = ubuf[a0:a0 + tm, :] if off % 8 == 0 else shifted[off % 8 - 1, a0:a0 + tm, :]
        term = cwc_ref[t:t + 1, :] * src
        acc = term if acc is None else acc + term
    mu = jnp.mean(acc, axis=-1, keepdims=True)
    xc = acc - mu
    var = jnp.mean(xc * xc, axis=-1, keepdims=True)
    y = xc * lax.rsqrt(var + EPS) * lng_ref[...] + lnb_ref[...]
    y_c = jnp.dot((y * _sigmoid(y)).astype(BF16), woc_ref[...], preferred_element_type=F32)

    y_b = jnp.dot(att_ref[...], wob_ref[...], preferred_element_type=F32)

    g = gates_ref[...]
    merged = (_sigmoid(g[:, 0:D_MODEL].astype(F32)) * y_a
              + _sigmoid(g[:, D_MODEL:2 * D_MODEL].astype(F32)) * y_b
              + _sigmoid(g[:, 2 * D_MODEL:].astype(F32)) * y_c)
    x_new = x_ref[...] + jnp.dot(merged.astype(BF16), wo_ref[...], preferred_element_type=F32)
    out_ref[...] = x_new
    _route_tile(x_new, gf_ref, wr_ref, br_ref, below, run_ref, meta_ref, meta_t_ref, cnt_ref)
    for c in range(ZERO_COPIES):
        zero_copy(c).wait()


def _mixer_tail(proj, tail, att, x, cwa, woa, cwc, lng, lnb, woc, wob, wo, gf, wr, br):
    tm = TM_MIX
    row = lambda width, col: pl.BlockSpec((tm, width), lambda i: (i, col // width))
    halo = lambda rows, width, col: pl.BlockSpec(
        (rows, width), lambda i: (jnp.maximum(i * (tm // rows) - 1, 0), col // width))
    const = lambda a: pl.BlockSpec(a.shape, lambda i: (0,) * a.ndim)
    weights = (cwa, woa, cwc, lng, lnb, woc, wob, wo, gf, wr, br)
    return pl.pallas_call(
        _mixer_tail_kernel,
        grid=(TOKENS // tm,),
        in_specs=[
            row(3 * D_MODEL, COL_GATES),
            row(3 * SC_WIDTH, COL_SC),
            halo(HALO_A, 3 * SC_WIDTH, COL_SC),
            row(CONF_WIDTH, TCOL_GLU),
            row(CONF_WIDTH, TCOL_GLU + CONF_WIDTH),
            halo(HALO_C, CONF_WIDTH, TCOL_GLU),
            halo(HALO_C, CONF_WIDTH, TCOL_GLU + CONF_WIDTH),
            pl.BlockSpec((tm, N_HEADS * V_HEAD), lambda i: (i, 0)),
            pl.BlockSpec((tm, D_MODEL), lambda i: (i, 0)),
        ] + [const(a) for a in weights],
        out_specs=[
            pl.BlockSpec((tm, D_MODEL), lambda i: (i, 0)),
            pl.BlockSpec((tm, LANES), lambda i: (i, 0)),
            pl.BlockSpec((META_ROWS, tm), lambda i: (0, i)),
            pl.BlockSpec((1, LANES), lambda i: (0, 0)),
            pl.BlockSpec(memory_space=pl.ANY),
        ],
        out_shape=(
            jax.ShapeDtypeStruct((TOKENS, D_MODEL), F32),
            jax.ShapeDtypeStruct((TOKENS, LANES), F32),
            jax.ShapeDtypeStruct((META_ROWS, TOKENS), F32),
            jax.ShapeDtypeStruct((1, LANES), F32),
            jax.ShapeDtypeStruct((SORTED_ROWS, LANES), F32),
        ),
        scratch_shapes=[
            pltpu.VMEM((tm + 8, SC_WIDTH), F32),
            pltpu.VMEM((tm + HALO_C, CONF_WIDTH), F32),
            pltpu.VMEM((7, tm + HALO_C - 8, CONF_WIDTH), F32),
            pltpu.VMEM((tm, tm), BF16),
            pltpu.VMEM((1, LANES), F32),
            pltpu.VMEM((ZERO_ROWS, LANES), F32),
            pltpu.SemaphoreType.DMA((1,)),
        ],
        compiler_params=_cparams(("arbitrary",), VMEM_BIG_MB),
        name="mixer_tail",
    )(proj, proj, proj, tail, tail, tail, tail, att, x, *weights)


def _route(logits):
    lane = lax.broadcasted_iota(jnp.int32, logits.shape, 1)
    lane_f = lane.astype(F32)
    neg = -jnp.inf
    big = float(LANES)
    is_grp = (lane >= ROUTER_GROUP_LANE) & (lane < ROUTER_GROUP_LANE + N_GROUPS)
    glog = jnp.where(is_grp, logits, neg)
    gmax = jnp.max(glog, axis=-1, keepdims=True)
    gidx = jnp.min(jnp.where(glog == gmax, lane_f, big), axis=-1, keepdims=True)
    p_sel = 1.0 / jnp.sum(jnp.exp(glog - gmax), axis=-1, keepdims=True)
    first = (gidx - ROUTER_GROUP_LANE) * EXPERTS_PER_GROUP
    in_grp = (lane_f >= first) & (lane_f < first + EXPERTS_PER_GROUP)
    el = jnp.where(in_grp, logits, neg)
    m1 = jnp.max(el, axis=-1, keepdims=True)
    i1 = jnp.min(jnp.where(el == m1, lane_f, big), axis=-1, keepdims=True)
    el2 = jnp.where(lane_f == i1, neg, el)
    m2 = jnp.max(el2, axis=-1, keepdims=True)
    i2 = jnp.min(jnp.where(el2 == m2, lane_f, big), axis=-1, keepdims=True)
    e2 = jnp.exp(m2 - m1)
    w1 = p_sel / (1.0 + e2)
    w2 = w1 * e2
    return i1, i2, w1, w2


def _route_tile(x, g_ref, wr_ref, br_ref, below_ref, run_ref, meta_ref, meta_t_ref, cnt_ref):
    tm = x.shape[0]
    h = _rms(x, g_ref[...])
    h_hi = h.astype(BF16)
    h_lo = (h - h_hi.astype(F32)).astype(BF16)
    hi_terms = jnp.dot(h_hi, wr_ref[...], preferred_element_type=F32)
    logits = (hi_terms[:, :LANES] + hi_terms[:, LANES:]
              + jnp.dot(h_lo, wr_ref[:, :LANES], preferred_element_type=F32) + br_ref[...])
    i1, i2, w1, w2 = _route(logits)
    lane = lax.broadcasted_iota(jnp.int32, (tm, LANES), 1)
    lane_f = lane.astype(F32)
    oh1 = lane_f == i1
    oh2 = lane_f == i2
    onehot = jnp.where(oh1, 1.0, 0.0) + jnp.where(oh2, 1.0, 0.0)
    before = run_ref[...] + jnp.dot(below_ref[...], onehot.astype(BF16), preferred_element_type=F32)
    r1 = jnp.sum(jnp.where(oh1, before, 0.0), axis=-1, keepdims=True)
    r2 = jnp.sum(jnp.where(oh2, before, 0.0), axis=-1, keepdims=True)
    run_ref[...] += jnp.sum(onehot, axis=0, keepdims=True)
    cnt_ref[...] = run_ref[...]
    meta = jnp.zeros((tm, LANES), F32)
    for col, val in enumerate((i1, i2, r1, r2, w1, w2)):
        meta = jnp.where(lane == col, val, meta)
    meta_ref[...] = meta
    meta_t_ref[...] = meta.T[0:META_ROWS, :]


def _dispatch_plan(meta_t, counts):
    e1 = meta_t[META_E1].astype(jnp.int32)
    e2 = meta_t[META_E2].astype(jnp.int32)
    r1 = meta_t[META_R1].astype(jnp.int32)
    r2 = meta_t[META_R2].astype(jnp.int32)
    cnt = counts[0, :N_EXPERTS].astype(jnp.int32)
    tiles = (cnt + TM_EXP - 1) // TM_EXP
    tile_end = jnp.cumsum(tiles)
    first_slot = ((tile_end - tiles) * TM_EXP)[:, None]
    expert = jnp.arange(N_EXPERTS, dtype=jnp.int32)[:, None]
    pos1 = jnp.sum(jnp.where(e1[None, :] == expert, first_slot, 0), axis=0) + r1
    pos2 = jnp.sum(jnp.where(e2[None, :] == expert, first_slot, 0), axis=0) + r2
    n_tiles = tile_end[-1:]
    tile_id = jnp.minimum(jnp.arange(MAX_TILES, dtype=jnp.int32), n_tiles - 1)
    tile_expert = jnp.sum((tile_id[:, None] >= tile_end[None, :]).astype(jnp.int32), axis=1)
    return pos1, pos2, tile_expert, n_tiles


def _dispatch_kernel(pos1_ref, pos2_ref, x_ref, g_ref, xs_in_ref, xs_ref, slab, sem):
    del xs_in_ref
    tm = TM_DISP
    i = pl.program_id(0)
    last = pl.num_programs(0) - 1
    slot = i % 2

    def wait_copies(sl):
        for _ in range(TOP_K):
            pltpu.make_async_copy(slab.at[sl], xs_ref.at[pl.ds(0, tm * SLAB), :], sem.at[sl]).wait()

    @pl.when(i >= 2)
    def _():
        wait_copies(slot)

    h = _rms(x_ref[...], g_ref[...])
    for s in range(SLAB):
        slab[slot, pl.ds(s, tm, stride=SLAB), :] = h[:, s * LANES:(s + 1) * LANES]

    def body(c, carry):
        for u in range(GATHER_UNROLL):
            r = c * GATHER_UNROLL + u
            src = slab.at[slot, pl.ds(pl.multiple_of(r * SLAB, SLAB), SLAB), :]
            for k, pos_ref in enumerate((pos1_ref, pos2_ref)):
                p = pos_ref[i * tm + r]
                pltpu.make_async_copy(src, xs_ref.at[pl.ds(pl.multiple_of(p * SLAB, SLAB), SLAB), :],
                                      sem.at[slot]).start(priority=k % N_DMA_PRIORITIES)
        return carry

    lax.fori_loop(0, tm // GATHER_UNROLL, body, 0)

    @pl.when(i == last)
    def _():
        wait_copies(1 - slot)
        wait_copies(slot)


def _dispatch(pos1, pos2, x, g, xs0):
    tm = TM_DISP
    return pl.pallas_call(
        _dispatch_kernel,
        grid_spec=pltpu.PrefetchScalarGridSpec(
            num_scalar_prefetch=2,
            grid=(TOKENS // tm,),
            in_specs=[
                pl.BlockSpec((tm, D_MODEL), lambda i, p1, p2: (i, 0)),
                pl.BlockSpec((1, D_MODEL), lambda i, p1, p2: (0, 0)),
                pl.BlockSpec(memory_space=pl.ANY),
            ],
            out_specs=pl.BlockSpec(memory_space=pl.ANY),
            scratch_shapes=[pltpu.VMEM((2, tm * SLAB, LANES), F32), pltpu.SemaphoreType.DMA((2,))],
        ),
        out_shape=jax.ShapeDtypeStruct((SORTED_ROWS, LANES), F32),
        input_output_aliases={4: 0},
        compiler_params=_cparams(("arbitrary",), VMEM_MB),
        name="moe_dispatch",
    )(pos1, pos2, x, g, xs0)


def _expert_kernel(te_ref, nt_ref, xs_ref, wg_ref, wu_ref, wd_ref, ys_ref, xbuf, sem):
    del te_ref
    tm = TM_EXP
    i = pl.program_id(0)
    n = nt_ref[0]

    def tile_copy(tile):
        slot = tile % EXPERT_SLOTS
        rows = pl.ds(pl.multiple_of(tile * (tm * SLAB), tm * SLAB), tm * SLAB)
        return pltpu.make_async_copy(xs_ref.at[rows, :], xbuf.at[slot], sem.at[slot])

    @pl.when(i == 0)
    def _():
        for ahead in range(EXPERT_PREFETCH):
            @pl.when(ahead < n)
            def _():
                tile_copy(ahead).start()

    @pl.when(i + EXPERT_PREFETCH < n)
    def _():
        tile_copy(i + EXPERT_PREFETCH).start()

    @pl.when(i < n)
    def _():
        tile_copy(i).wait()
        slot = i % EXPERT_SLOTS
        xt = jnp.concatenate([xbuf[slot, pl.ds(s, tm, stride=SLAB), :].astype(BF16) for s in range(SLAB)], axis=1)
        hg = jnp.dot(xt, wg_ref[...].astype(BF16), preferred_element_type=F32)
        hu = jnp.dot(xt, wu_ref[...].astype(BF16), preferred_element_type=F32)
        hh = (hg * _sigmoid(hg) * hu).astype(BF16)
        y = jnp.dot(hh, wd_ref[...].astype(BF16), preferred_element_type=F32)
        for s in range(SLAB):
            ys_ref[pl.ds(s, tm, stride=SLAB), :] = y[:, s * LANES:(s + 1) * LANES]

    @pl.when(i >= n)
    def _():
        ys_ref[...] = jnp.zeros_like(ys_ref)


def _experts(layer, tile_expert, n_tiles, xs, w_gate, w_up, w_down):
    tm = TM_EXP
    wspec = lambda rows, cols: pl.BlockSpec((None, None, rows, cols), lambda i, te, nt: (layer, te[i], 0, 0))
    return pl.pallas_call(
        _expert_kernel,
        grid_spec=pltpu.PrefetchScalarGridSpec(
            num_scalar_prefetch=2,
            grid=(MAX_TILES,),
            in_specs=[
                pl.BlockSpec(memory_space=pl.ANY),
                wspec(D_MODEL, EXPERT_HIDDEN),
                wspec(D_MODEL, EXPERT_HIDDEN),
                wspec(EXPERT_HIDDEN, D_MODEL),
            ],
            out_specs=pl.BlockSpec((tm * SLAB, LANES), lambda i, te, nt: (i, 0)),
            scratch_shapes=[pltpu.VMEM((EXPERT_SLOTS, tm * SLAB, LANES), F32), pltpu.SemaphoreType.DMA((EXPERT_SLOTS,))],
        ),
        out_shape=jax.ShapeDtypeStruct((SORTED_ROWS, LANES), F32),
        compiler_params=_cparams(("arbitrary",), VMEM_MB),
        name="moe_experts",
    )(tile_expert, n_tiles, xs, w_gate, w_up, w_down)


def _start_slab_gathers(idx_refs, base, n_rows, src_hbm, dst_bufs, sem):
    def body(c, carry):
        for u in range(GATHER_UNROLL):
            r = c * GATHER_UNROLL + u
            for k, (idx_ref, dst) in enumerate(zip(idx_refs, dst_bufs)):
                t = idx_ref[base + r]
                pltpu.make_async_copy(src_hbm.at[pl.ds(pl.multiple_of(t * SLAB, SLAB), SLAB), :],
                                      dst.at[pl.ds(pl.multiple_of(r * SLAB, SLAB), SLAB), :],
                                      sem).start(priority=k % N_DMA_PRIORITIES)
        return carry

    lax.fori_loop(0, n_rows // GATHER_UNROLL, body, 0)


def _wait_slab_gathers(n_rows, src_hbm, dst, sem):
    pltpu.make_async_copy(src_hbm.at[pl.ds(0, n_rows * SLAB), :], dst, sem).wait()


def _combine_ple_kernel(pos1_ref, pos2_ref, x_ref, meta_ref, ys_ref, p_ref, g_ref, wg_ref, wp_ref, fg_ref,
                        out_ref, cbuf, sem, *, final):
    tm = TM_COMB
    i = pl.program_id(0)

    def gather(tile, slot):
        _start_slab_gathers((pos1_ref, pos2_ref), tile * tm, tm, ys_ref, (cbuf.at[slot, 0], cbuf.at[slot, 1]),
                            sem.at[slot])

    @pl.when(i == 0)
    def _():
        gather(0, 0)

    @pl.when(i + 1 < pl.num_programs(0))
    def _():
        gather(i + 1, (i + 1) % 2)

    slot = i % 2
    for k in range(2):
        _wait_slab_gathers(tm, ys_ref, cbuf.at[slot, k], sem.at[slot])
    meta = meta_ref[...]
    w1 = meta[:, META_W1:META_W1 + 1]
    w2 = meta[:, META_W2:META_W2 + 1]
    moe = jnp.concatenate([w1 * cbuf[slot, 0, pl.ds(s, tm, stride=SLAB), :]
                           + w2 * cbuf[slot, 1, pl.ds(s, tm, stride=SLAB), :] for s in range(SLAB)], axis=1)
    x = x_ref[...] + moe
    h = _rms(x, g_ref[...]).astype(BF16)
    gate = _sigmoid(jnp.dot(h, wg_ref[...], preferred_element_type=F32))
    emb = jnp.dot(p_ref[...].astype(BF16), wp_ref[...], preferred_element_type=F32)
    y = x + gate * emb
    if final:
        y = _rms(y, fg_ref[...])
    out_ref[...] = y


def _combine_ple(layer, pos1, pos2, x, meta, ys, p, g, wg, wp, fg, final):
    tm = TM_COMB
    const = lambda i, p1, p2: (0, 0)
    rows = lambda i, p1, p2: (i, 0)
    return pl.pallas_call(
        functools.partial(_combine_ple_kernel, final=final),
        grid_spec=pltpu.PrefetchScalarGridSpec(
            num_scalar_prefetch=2,
            grid=(TOKENS // tm,),
            in_specs=[
                pl.BlockSpec((tm, D_MODEL), rows),
                pl.BlockSpec((tm, LANES), rows),
                pl.BlockSpec(memory_space=pl.ANY),
                pl.BlockSpec((None, tm, PLE_DIM), lambda i, p1, p2: (layer, i, 0)),
                pl.BlockSpec((1, D_MODEL), const),
                pl.BlockSpec((D_MODEL, D_MODEL), const),
                pl.BlockSpec((PLE_DIM, D_MODEL), const),
                pl.BlockSpec((1, D_MODEL), const),
            ],
            out_specs=pl.BlockSpec((tm, D_MODEL), rows),
            scratch_shapes=[pltpu.VMEM((2, 2, tm * SLAB, LANES), F32), pltpu.SemaphoreType.DMA((2,))],
        ),
        out_shape=jax.ShapeDtypeStruct((TOKENS, D_MODEL), F32),
        compiler_params=_cparams(("arbitrary",), VMEM_MB),
        name="moe_combine_ple",
    )(pos1, pos2, x, meta, ys, p, g, wg, wp, fg)


def _in_proj_weights(w_in):
    k_rope = w_in[:, MAIN_COLS:MAIN_COLS + QK_ROPE]
    glu = w_in[:, MAIN_COLS + QK_ROPE:]
    half = QK_ROPE // 2
    zn = jnp.zeros((D_MODEL, QK_NOPE), F32)
    zp = jnp.zeros((D_MODEL, HEAD_PAD - QK_NOPE - QK_ROPE), F32)
    tail = jnp.concatenate([glu, zn, k_rope, zp, zn, k_rope[:, half:], k_rope[:, :half], zp], axis=1)
    return w_in[:, :MAIN_COLS].astype(BF16), tail.astype(BF16)


def _q_weight(w_uq):
    scale = (QK_NOPE + QK_ROPE) ** -0.5 * LOG2_E
    w = (w_uq * scale).reshape(Q_LORA, N_HEADS, QK_NOPE + QK_ROPE)
    zp = jnp.zeros((Q_LORA, N_HEADS, HEAD_PAD - QK_NOPE - QK_ROPE), F32)
    return jnp.concatenate([w, zp], axis=2).reshape(Q_LORA, N_HEADS * HEAD_PAD).astype(BF16)


def _kv_weight(w_ukv):
    w = w_ukv.reshape(KV_LORA, N_HEADS, QK_NOPE + V_HEAD)
    k_nope, v = w[:, :, :QK_NOPE], w[:, :, QK_NOPE:]
    z = jnp.zeros_like(v)
    k_part = jnp.concatenate([k_nope, jnp.zeros_like(k_nope)], axis=2).reshape(KV_LORA, N_HEADS * HEAD_PAD)
    odd = (jnp.arange(N_HEADS) % 2 == 1)[None, :, None]
    v_part = jnp.concatenate([jnp.where(odd, z, v), jnp.where(odd, v, z)], axis=2).reshape(KV_LORA, N_HEADS * HEAD_PAD)
    return jnp.concatenate([k_part, v_part], axis=1).astype(BF16)


def _router_weight(w_rg, b_rg, w_re, b_re):
    pad = LANES - N_EXPERTS - N_GROUPS
    w = jnp.concatenate([w_re, w_rg, jnp.zeros((D_MODEL, pad), F32)], axis=1)
    b = jnp.concatenate([b_re, b_rg, jnp.zeros((pad,), F32)]).reshape(1, LANES)
    w_hi = w.astype(BF16)
    w_lo = (w - w_hi.astype(F32)).astype(BF16)
    return jnp.concatenate([w_hi, w_lo], axis=1), b


def kernel(x, p, positions, ln_mix_g, w_in, conv_a_w, w_out_a, q_norm_g, w_uq, kv_norm_g, w_ukv, w_out_b, conv_c_w, ln_c_g, ln_c_b, w_out_c, w_o, ln_ffn_g, w_route_grp, b_route_grp, w_route_exp, b_route_exp, w_exp_gate, w_exp_up, w_exp_down, ln_ple_g, w_ple_gate, w_ple, final_norm_g):
    c_tab, s_tab = _rope_tables(positions)
    xf = x.reshape(TOKENS, D_MODEL)
    pf = p.reshape(DEPTH, TOKENS, PLE_DIM)
    row = lambda a: a.reshape(1, -1)
    for i in range(DEPTH):
        proj, tail = _inproj(xf, row(ln_mix_g[i]), *_in_proj_weights(w_in[i]))
        q, k, v = _qkv(proj, tail, c_tab, s_tab, row(q_norm_g[i]), row(kv_norm_g[i]), _q_weight(w_uq[i]), _kv_weight(w_ukv[i]))
        att = _attention(q, k, v)
        wr, br = _router_weight(w_route_grp[i], b_route_grp[i], w_route_exp[i], b_route_exp[i])
        xf, meta, meta_t, counts, xs0 = _mixer_tail(
            proj, tail, att, xf, conv_a_w[i], w_out_a[i].astype(BF16), conv_c_w[i], row(ln_c_g[i]), row(ln_c_b[i]),
            w_out_c[i].astype(BF16), w_out_b[i].astype(BF16), w_o[i].astype(BF16), row(ln_ffn_g[i]), wr, br)
        pos1, pos2, tile_expert, n_tiles = _dispatch_plan(meta_t, counts)
        xs = _dispatch(pos1, pos2, xf, row(ln_ffn_g[i]), xs0)
        ys = _experts(i, tile_expert, n_tiles, xs, w_exp_gate, w_exp_up, w_exp_down)
        xf = _combine_ple(i, pos1, pos2, xf, meta, ys, pf, row(ln_ple_g[i]), w_ple_gate[i].astype(BF16),
                          w_ple[i].astype(BF16), row(final_norm_g), final=(i == DEPTH - 1))
    return xf.reshape(BATCH, SEQ, D_MODEL)
```

```python
import functools

import jax
import jax.numpy as jnp
from jax import lax
from jax.experimental import pallas as pl
from jax.experimental.pallas import tpu as pltpu

D_MODEL = 1024
BATCH = 8
SEQ = 2048
DEPTH = 2
TOKENS = BATCH * SEQ
PLE_DIM = 256
SC_WIDTH = 512
SC_KERNEL = 3
N_HEADS = 8
QK_NOPE = 64
QK_ROPE = 32
V_HEAD = 64
Q_LORA = 768
KV_LORA = 256
ROPE_THETA = 10000.0
CONF_WIDTH = 512
CONF_KERNEL = 31
N_GROUPS = 4
EXPERTS_PER_GROUP = 8
N_EXPERTS = N_GROUPS * EXPERTS_PER_GROUP
EXPERT_HIDDEN = 256
EPS = 1e-6
LOG2_E = 1.4426950408889634

LANES = 128
HEAD_PAD = 128
F32 = jnp.float32
BF16 = jnp.bfloat16

COL_GATES = 0
COL_SC = 3 * D_MODEL
COL_QLAT = COL_SC + 3 * SC_WIDTH
COL_KVLAT = COL_QLAT + Q_LORA
MAIN_COLS = COL_KVLAT + KV_LORA
TCOL_GLU = 0
TCOL_KR = 2 * CONF_WIDTH
TCOL_KRSW = TCOL_KR + HEAD_PAD
TAIL_COLS = TCOL_KRSW + HEAD_PAD

ROUTER_GROUP_LANE = N_EXPERTS

MXU_TILE = 256
VMEM_MB = 48
VMEM_BIG_MB = 56

TM_INPROJ = 512
TN_INPROJ = 6 * MXU_TILE
TM_QKV = 1024
T_ATTN = 512
HEADS_PER_STEP = 8
TM_MIX = 512
HALO_C = 32
HALO_A = 16
TM_EXP = 512
EXPERT_PREFETCH = 2
EXPERT_SLOTS = EXPERT_PREFETCH + 1
TM_DISP = 512
TM_COMB = 256
TOP_K = 2
MAX_TILES = TOKENS * TOP_K // TM_EXP + N_EXPERTS
SLAB = D_MODEL // LANES
GATHER_UNROLL = 16
N_DMA_PRIORITIES = 2
META_E1, META_E2, META_R1, META_R2, META_W1, META_W2 = range(6)
META_ROWS = 8
SORTED_ROWS = MAX_TILES * TM_EXP * SLAB
ZERO_COPIES = 6
ZERO_ROWS = SORTED_ROWS // (TOKENS // TM_MIX) // ZERO_COPIES


def _cparams(semantics, vmem_mb):
    return pltpu.CompilerParams(dimension_semantics=semantics, vmem_limit_bytes=vmem_mb * 1024 * 1024)


def _sigmoid(x):
    return 1.0 / (1.0 + jnp.exp2(x * -LOG2_E))


def _rms(x, g):
    return x * lax.rsqrt(jnp.mean(x * x, axis=-1, keepdims=True) + EPS) * g


def _place(x, onehot):
    x1 = x.astype(BF16)
    r1 = x - x1.astype(F32)
    x2 = r1.astype(BF16)
    x3 = (r1 - x2.astype(F32)).astype(BF16)
    return (jnp.dot(x1, onehot, preferred_element_type=F32) + jnp.dot(x2, onehot, preferred_element_type=F32)
            + jnp.dot(x3, onehot, preferred_element_type=F32))


def _rope_kernel(pos_ref, freq_ref, c_ref, s_ref):
    half = QK_ROPE // 2
    per_row = LANES // half
    rows = TOKENS // per_row
    ang = pos_ref[...].astype(F32) * freq_ref[...]
    cos = jnp.cos(ang)
    sin = jnp.sin(ang)
    src = lax.broadcasted_iota(jnp.int32, (LANES, LANES), 0)
    dst = lax.broadcasted_iota(jnp.int32, (LANES, LANES), 1)
    lane = lax.broadcasted_iota(jnp.int32, (1, LANES), 1)
    ones_nope = jnp.where(lane < QK_NOPE, 1.0, 0.0)
    sign = jnp.where(lane < QK_NOPE + half, -1.0, 1.0)
    for j in range(per_row):
        f = src - half * j
        hit = (dst == QK_NOPE + f) | (dst == QK_NOPE + half + f)
        onehot = jnp.where((f >= 0) & (f < half) & hit, 1.0, 0.0).astype(BF16)
        c_ref[pl.ds(j, rows, stride=per_row), :] = _place(cos, onehot) + ones_nope
        s_ref[pl.ds(j, rows, stride=per_row), :] = _place(sin, onehot) * sign


def _rope_tables(positions):
    half = QK_ROPE // 2
    inv_freq = ROPE_THETA ** (-jnp.arange(0, QK_ROPE, 2, dtype=F32) / QK_ROPE)
    rows = TOKENS * half // LANES
    pos_rep = jnp.broadcast_to(positions.reshape(TOKENS, 1), (TOKENS, half)).reshape(rows, LANES)
    freq = jnp.tile(inv_freq, LANES // half).reshape(1, LANES)
    return pl.pallas_call(
        _rope_kernel,
        out_shape=(jax.ShapeDtypeStruct((TOKENS, HEAD_PAD), F32),) * 2,
        compiler_params=_cparams(None, VMEM_MB),
        name="rope_tables",
    )(pos_rep, freq)


def _inproj_kernel(x_ref, g_ref, wm_ref, wt_ref, om_ref, ot_ref):
    h = _rms(x_ref[...], g_ref[...]).astype(BF16)
    for lo in range(0, MAIN_COLS, TN_INPROJ):
        hi = min(lo + TN_INPROJ, MAIN_COLS)
        om_ref[:, lo:hi] = jnp.dot(h, wm_ref[:, lo:hi], preferred_element_type=F32).astype(BF16)
    ot_ref[...] = jnp.dot(h, wt_ref[...], preferred_element_type=F32).astype(BF16)


def _inproj(x, g, w_main, w_tail):
    tm = TM_INPROJ
    return pl.pallas_call(
        _inproj_kernel,
        grid=(TOKENS // tm,),
        in_specs=[
            pl.BlockSpec((tm, D_MODEL), lambda i: (i, 0)),
            pl.BlockSpec((1, D_MODEL), lambda i: (0, 0)),
            pl.BlockSpec((D_MODEL, MAIN_COLS), lambda i: (0, 0)),
            pl.BlockSpec((D_MODEL, TAIL_COLS), lambda i: (0, 0)),
        ],
        out_specs=[
            pl.BlockSpec((tm, MAIN_COLS), lambda i: (i, 0)),
            pl.BlockSpec((tm, TAIL_COLS), lambda i: (i, 0)),
        ],
        out_shape=(jax.ShapeDtypeStruct((TOKENS, MAIN_COLS), BF16), jax.ShapeDtypeStruct((TOKENS, TAIL_COLS), BF16)),
        compiler_params=_cparams(("parallel",), VMEM_BIG_MB),
        name="in_proj",
    )(x, g, w_main, w_tail)


def _qkv_kernel(ql_ref, kvl_ref, kr_ref, krsw_ref, c_ref, s_ref, qg_ref, kvg_ref, wq_ref, wkv_ref,
                q_out, k_out, v_out):
    c = c_ref[...]
    s = s_ref[...]
    width = N_HEADS * HEAD_PAD
    half = QK_ROPE // 2
    low_half = lax.broadcasted_iota(jnp.int32, c.shape, 1) < QK_NOPE + half

    def swap_halves(x):
        return jnp.where(low_half, pltpu.roll(x, HEAD_PAD - half, axis=1), pltpu.roll(x, half, axis=1))

    qn = _rms(ql_ref[...].astype(F32), qg_ref[...]).astype(BF16)
    qq = jnp.dot(qn, wq_ref[...], preferred_element_type=F32)
    for h in range(N_HEADS):
        lo, hi = h * HEAD_PAD, (h + 1) * HEAD_PAD
        q_out[:, lo:hi] = (qq[:, lo:hi] * c + swap_halves(qq[:, lo:hi]) * s).astype(BF16)
    kvn = _rms(kvl_ref[...].astype(F32), kvg_ref[...]).astype(BF16)
    kk = jnp.dot(kvn, wkv_ref[...], preferred_element_type=F32)
    kr = kr_ref[...].astype(F32) * c + krsw_ref[...].astype(F32) * s
    for h in range(N_HEADS):
        lo, hi = h * HEAD_PAD, (h + 1) * HEAD_PAD
        k_out[:, lo:hi] = (kk[:, lo:hi] + kr).astype(BF16)
    v_out[...] = kk[:, width:].astype(BF16)


def _qkv(proj, tail, c_tab, s_tab, qg, kvg, wq, wkv):
    tm = TM_QKV
    width = N_HEADS * HEAD_PAD
    row = lambda blk: (lambda i: (i, blk))
    const = lambda i: (0, 0)
    return pl.pallas_call(
        _qkv_kernel,
        grid=(TOKENS // tm,),
        in_specs=[
            pl.BlockSpec((tm, Q_LORA), row(COL_QLAT // Q_LORA)),
            pl.BlockSpec((tm, KV_LORA), row(COL_KVLAT // KV_LORA)),
            pl.BlockSpec((tm, HEAD_PAD), row(TCOL_KR // HEAD_PAD)),
            pl.BlockSpec((tm, HEAD_PAD), row(TCOL_KRSW // HEAD_PAD)),
            pl.BlockSpec((tm, HEAD_PAD), row(0)),
            pl.BlockSpec((tm, HEAD_PAD), row(0)),
            pl.BlockSpec((1, Q_LORA), const),
            pl.BlockSpec((1, KV_LORA), const),
            pl.BlockSpec((Q_LORA, width), const),
            pl.BlockSpec((KV_LORA, 2 * width), const),
        ],
        out_specs=[pl.BlockSpec((tm, width), row(0))] * 3,
        out_shape=(jax.ShapeDtypeStruct((TOKENS, width), BF16),) * 3,
        compiler_params=_cparams(("parallel",), VMEM_MB),
        name="qkv_prep",
    )(proj, proj, tail, tail, c_tab, s_tab, qg, kvg, wq, wkv)


def _attn_kernel(q_ref, k_ref, v_ref, o_ref, *state):
    t = T_ATTN
    nh = HEADS_PER_STEP
    qi = pl.program_id(2)
    nt = (((1,), (1,)), ((), ()))
    m_sc, l_sc, acc_sc = state[0:nh], state[nh:2 * nh], state[2 * nh:3 * nh]
    for h in range(nh):
        m_sc[h][...] = jnp.full((t, LANES), -jnp.inf, F32)
        l_sc[h][...] = jnp.zeros((t, LANES), F32)
        acc_sc[h][...] = jnp.zeros((t, LANES), F32)

    def block(j, r0, nr, c0, nc, masked):
        start = pl.multiple_of(j * t + c0, nc)
        rows = slice(r0, r0 + nr)
        for h in range(nh):
            lo, hi = h * HEAD_PAD, (h + 1) * HEAD_PAD
            s = lax.dot_general(q_ref[rows, lo:hi], k_ref[pl.ds(start, nc), lo:hi], nt, preferred_element_type=F32)
            if masked:
                row_id = r0 + lax.broadcasted_iota(jnp.int32, (nr, nc), 0)
                col_id = c0 + lax.broadcasted_iota(jnp.int32, (nr, nc), 1)
                s = jnp.where(row_id >= col_id, s, -jnp.inf)
            blocks = [s[:, c * LANES:(c + 1) * LANES] for c in range(nc // LANES)]
            bmax = functools.reduce(jnp.maximum, blocks)
            m_old = m_sc[h][rows, :]
            m_new = jnp.maximum(m_old, jnp.max(bmax, axis=-1, keepdims=True))
            alpha = jnp.exp2(m_old - m_new)
            ps = [jnp.exp2(b - m_new) for b in blocks]
            p = jnp.concatenate(ps, axis=1).astype(BF16)
            l_sc[h][rows, :] = alpha * l_sc[h][rows, :] + functools.reduce(jnp.add, ps)
            acc_sc[h][rows, :] = alpha * acc_sc[h][rows, :] + jnp.dot(p, v_ref[pl.ds(start, nc), lo:hi],
                                                                       preferred_element_type=F32)
            m_sc[h][rows, :] = m_new

    def body(j, carry):
        block(j, 0, t, 0, t, False)
        return carry

    lax.fori_loop(0, qi, body, 0)
    half = t // 2
    block(qi, 0, t, 0, half, True)
    block(qi, half, half, half, half, True)
    out = [acc_sc[h][...] / jnp.sum(l_sc[h][...], axis=-1, keepdims=True) for h in range(nh)]
    o_ref[...] = jnp.concatenate([out[h] + out[h + 1] for h in range(0, nh, 2)], axis=1).astype(BF16)


def _attention(q, k, v):
    t = T_ATTN
    nq = SEQ // t
    nh = HEADS_PER_STEP
    return pl.pallas_call(
        _attn_kernel,
        grid=(BATCH, N_HEADS // nh, nq),
        in_specs=[
            pl.BlockSpec((t, nh * HEAD_PAD), lambda b, hg, i: (b * nq + i, hg)),
            pl.BlockSpec((SEQ, nh * HEAD_PAD), lambda b, hg, i: (b, hg)),
            pl.BlockSpec((SEQ, nh * HEAD_PAD), lambda b, hg, i: (b, hg)),
        ],
        out_specs=pl.BlockSpec((t, nh * V_HEAD), lambda b, hg, i: (b * nq + i, hg)),
        out_shape=jax.ShapeDtypeStruct((TOKENS, N_HEADS * V_HEAD), BF16),
        scratch_shapes=[pltpu.VMEM((t, LANES), F32)] * (3 * nh),
        compiler_params=_cparams(("parallel", "parallel", "arbitrary"), VMEM_MB),
        name="mla_attention",
    )(q, k, v)


def _mixer_tail_kernel(gates_ref, sc_ref, sch_ref, gv_ref, gg_ref, gvh_ref, ggh_ref, att_ref, x_ref,
                       cwa_ref, woa_ref, cwc_ref, lng_ref, lnb_ref, woc_ref, wob_ref, wo_ref, gf_ref, wr_ref, br_ref,
                       out_ref, meta_ref, meta_t_ref, cnt_ref, xs0_ref, cbuf, ubuf, shifted, below, run_ref, zbuf, zsem):
    tm = TM_MIX
    step = pl.program_id(0)

    @pl.when(step == 0)
    def _():
        run_ref[...] = jnp.zeros_like(run_ref)
        zbuf[...] = jnp.zeros_like(zbuf)
        row_id = lax.broadcasted_iota(jnp.int32, (tm, tm), 0)
        col_id = lax.broadcasted_iota(jnp.int32, (tm, tm), 1)
        below[...] = jnp.where(row_id > col_id, 1.0, 0.0).astype(BF16)

    def zero_copy(c):
        rows = pl.ds(pl.multiple_of((step * ZERO_COPIES + c) * ZERO_ROWS, ZERO_ROWS), ZERO_ROWS)
        return pltpu.make_async_copy(zbuf, xs0_ref.at[rows, :], zsem.at[0])

    for c in range(ZERO_COPIES):
        zero_copy(c).start()
    has_past = (pl.program_id(0) % (SEQ // tm)) != 0

    sc = sc_ref[...]
    sc_b = sc[:, 0:SC_WIDTH].astype(F32)
    cbuf[8:8 + tm, :] = sc[:, SC_WIDTH:2 * SC_WIDTH].astype(F32) * sc[:, 2 * SC_WIDTH:].astype(F32)
    sch = sch_ref[...].astype(F32)[HALO_A - 8:HALO_A, :]
    cbuf[0:8, :] = jnp.where(has_past, sch[:, SC_WIDTH:2 * SC_WIDTH] * sch[:, 2 * SC_WIDTH:], 0.0)
    conv_a = cwa_ref[0:1, :] * cbuf[6:6 + tm, :]
    for t in range(1, SC_KERNEL):
        conv_a = conv_a + cwa_ref[t:t + 1, :] * cbuf[6 + t:6 + t + tm, :]
    y_a = jnp.dot((sc_b * conv_a).astype(BF16), woa_ref[...], preferred_element_type=F32)

    ubuf[HALO_C:HALO_C + tm, :] = gv_ref[...].astype(F32) * _sigmoid(gg_ref[...].astype(F32))
    ubuf[0:HALO_C, :] = jnp.where(has_past, gvh_ref[...].astype(F32) * _sigmoid(ggh_ref[...].astype(F32)), 0.0)
    rows = tm + HALO_C - 8
    u_all = ubuf[...]
    for b in range(1, 8):
        shifted[b - 1, 0:rows, :] = pltpu.roll(u_all, tm + HALO_C - b, axis=0)[0:rows, :]
    base = HALO_C - (CONF_KERNEL - 1)
    acc = None
    for t in range(CONF_KERNEL):
        off = base + t
        a0 = off - off % 8
        src = ubuf[a0:a0 + tm, :] if off % 8 == 0 else shifted[off % 8 - 1, a0:a0 + tm, :]
        term = cwc_ref[t:t + 1, :] * src
        acc = term if acc is None else acc + term
    mu = jnp.mean(acc, axis=-1, keepdims=True)
    xc = acc - mu
    var = jnp.mean(xc * xc, axis=-1, keepdims=True)
    y = xc * lax.rsqrt(var + EPS) * lng_ref[...] + lnb_ref[...]
    y_c = jnp.dot((y * _sigmoid(y)).astype(BF16), woc_ref[...], preferred_element_type=F32)

    y_b = jnp.dot(att_ref[...], wob_ref[...], preferred_element_type=F32)

    g = gates_ref[...]
    merged = (_sigmoid(g[:, 0:D_MODEL].astype(F32)) * y_a
              + _sigmoid(g[:, D_MODEL:2 * D_MODEL].astype(F32)) * y_b
              + _sigmoid(g[:, 2 * D_MODEL:].astype(F32)) * y_c)
    x_new = x_ref[...] + jnp.dot(merged.astype(BF16), wo_ref[...], preferred_element_type=F32)
    out_ref[...] = x_new
    _route_tile(x_new, gf_ref, wr_ref, br_ref, below, run_ref, meta_ref, meta_t_ref, cnt_ref)
    for c in range(ZERO_COPIES):
        zero_copy(c).wait()


def _mixer_tail(proj, tail, att, x, cwa, woa, cwc, lng, lnb, woc, wob, wo, gf, wr, br):
    tm = TM_MIX
    row = lambda width, col: pl.BlockSpec((tm, width), lambda i: (i, col // width))
    halo = lambda rows, width, col: pl.BlockSpec(
        (rows, width), lambda i: (jnp.maximum(i * (tm // rows) - 1, 0), col // width))
    const = lambda a: pl.BlockSpec(a.shape, lambda i: (0,) * a.ndim)
    weights = (cwa, woa, cwc, lng, lnb, woc, wob, wo, gf, wr, br)
    return pl.pallas_call(
        _mixer_tail_kernel,
        grid=(TOKENS // tm,),
        in_specs=[
            row(3 * D_MODEL, COL_GATES),
            row(3 * SC_WIDTH, COL_SC),
            halo(HALO_A, 3 * SC_WIDTH, COL_SC),
            row(CONF_WIDTH, TCOL_GLU),
            row(CONF_WIDTH, TCOL_GLU + CONF_WIDTH),
            halo(HALO_C, CONF_WIDTH, TCOL_GLU),
            halo(HALO_C, CONF_WIDTH, TCOL_GLU + CONF_WIDTH),
            pl.BlockSpec((tm, N_HEADS * V_HEAD), lambda i: (i, 0)),
            pl.BlockSpec((tm, D_MODEL), lambda i: (i, 0)),
        ] + [const(a) for a in weights],
        out_specs=[
            pl.BlockSpec((tm, D_MODEL), lambda i: (i, 0)),
            pl.BlockSpec((tm, LANES), lambda i: (i, 0)),
            pl.BlockSpec((META_ROWS, tm), lambda i: (0, i)),
            pl.BlockSpec((1, LANES), lambda i: (0, 0)),
            pl.BlockSpec(memory_space=pl.ANY),
        ],
        out_shape=(
            jax.ShapeDtypeStruct((TOKENS, D_MODEL), F32),
            jax.ShapeDtypeStruct((TOKENS, LANES), F32),
            jax.ShapeDtypeStruct((META_ROWS, TOKENS), F32),
            jax.ShapeDtypeStruct((1, LANES), F32),
            jax.ShapeDtypeStruct((SORTED_ROWS, LANES), F32),
        ),
        scratch_shapes=[
            pltpu.VMEM((tm + 8, SC_WIDTH), F32),
            pltpu.VMEM((tm + HALO_C, CONF_WIDTH), F32),
            pltpu.VMEM((7, tm + HALO_C - 8, CONF_WIDTH), F32),
            pltpu.VMEM((tm, tm), BF16),
            pltpu.VMEM((1, LANES), F32),
            pltpu.VMEM((ZERO_ROWS, LANES), F32),
            pltpu.SemaphoreType.DMA((1,)),
        ],
        compiler_params=_cparams(("arbitrary",), VMEM_BIG_MB),
        name="mixer_tail",
    )(proj, proj, proj, tail, tail, tail, tail, att, x, *weights)


def _route(logits):
    lane = lax.broadcasted_iota(jnp.int32, logits.shape, 1)
    lane_f = lane.astype(F32)
    neg = -jnp.inf
    big = float(LANES)
    is_grp = (lane >= ROUTER_GROUP_LANE) & (lane < ROUTER_GROUP_LANE + N_GROUPS)
    glog = jnp.where(is_grp, logits, neg)
    gmax = jnp.max(glog, axis=-1, keepdims=True)
    gidx = jnp.min(jnp.where(glog == gmax, lane_f, big), axis=-1, keepdims=True)
    p_sel = 1.0 / jnp.sum(jnp.exp(glog - gmax), axis=-1, keepdims=True)
    first = (gidx - ROUTER_GROUP_LANE) * EXPERTS_PER_GROUP
    in_grp = (lane_f >= first) & (lane_f < first + EXPERTS_PER_GROUP)
    el = jnp.where(in_grp, logits, neg)
    m1 = jnp.max(el, axis=-1, keepdims=True)
    i1 = jnp.min(jnp.where(el == m1, lane_f, big), axis=-1, keepdims=True)
    el2 = jnp.where(lane_f == i1, neg, el)
    m2 = jnp.max(el2, axis=-1, keepdims=True)
    i2 = jnp.min(jnp.where(el2 == m2, lane_f, big), axis=-1, keepdims=True)
    e2 = jnp.exp(m2 - m1)
    w1 = p_sel / (1.0 + e2)
    w2 = w1 * e2
    return i1, i2, w1, w2


def _route_tile(x, g_ref, wr_ref, br_ref, below_ref, run_ref, meta_ref, meta_t_ref, cnt_ref):
    tm = x.shape[0]
    h = _rms(x, g_ref[...])
    h_hi = h.astype(BF16)
    h_lo = (h - h_hi.astype(F32)).astype(BF16)
    hi_terms = jnp.dot(h_hi, wr_ref[...], preferred_element_type=F32)
    logits = (hi_terms[:, :LANES] + hi_terms[:, LANES:]
              + jnp.dot(h_lo, wr_ref[:, :LANES], preferred_element_type=F32) + br_ref[...])
    i1, i2, w1, w2 = _route(logits)
    lane = lax.broadcasted_iota(jnp.int32, (tm, LANES), 1)
    lane_f = lane.astype(F32)
    oh1 = lane_f == i1
    oh2 = lane_f == i2
    onehot = jnp.where(oh1, 1.0, 0.0) + jnp.where(oh2, 1.0, 0.0)
    before = run_ref[...] + jnp.dot(below_ref[...], onehot.astype(BF16), preferred_element_type=F32)
    r1 = jnp.sum(jnp.where(oh1, before, 0.0), axis=-1, keepdims=True)
    r2 = jnp.sum(jnp.where(oh2, before, 0.0), axis=-1, keepdims=True)
    run_ref[...] += jnp.sum(onehot, axis=0, keepdims=True)
    cnt_ref[...] = run_ref[...]
    meta = jnp.zeros((tm, LANES), F32)
    for col, val in enumerate((i1, i2, r1, r2, w1, w2)):
        meta = jnp.where(lane == col, val, meta)
    meta_ref[...] = meta
    meta_t_ref[...] = meta.T[0:META_ROWS, :]


def _dispatch_plan(meta_t, counts):
    e1 = meta_t[META_E1].astype(jnp.int32)
    e2 = meta_t[META_E2].astype(jnp.int32)
    r1 = meta_t[META_R1].astype(jnp.int32)
    r2 = meta_t[META_R2].astype(jnp.int32)
    cnt = counts[0, :N_EXPERTS].astype(jnp.int32)
    tiles = (cnt + TM_EXP - 1) // TM_EXP
    tile_end = jnp.cumsum(tiles)
    first_slot = ((tile_end - tiles) * TM_EXP)[:, None]
    expert = jnp.arange(N_EXPERTS, dtype=jnp.int32)[:, None]
    pos1 = jnp.sum(jnp.where(e1[None, :] == expert, first_slot, 0), axis=0) + r1
    pos2 = jnp.sum(jnp.where(e2[None, :] == expert, first_slot, 0), axis=0) + r2
    n_tiles = tile_end[-1:]
    tile_id = jnp.minimum(jnp.arange(MAX_TILES, dtype=jnp.int32), n_tiles - 1)
    tile_expert = jnp.sum((tile_id[:, None] >= tile_end[None, :]).astype(jnp.int32), axis=1)
    return pos1, pos2, tile_expert, n_tiles


def _dispatch_kernel(pos1_ref, pos2_ref, x_ref, g_ref, xs_in_ref, xs_ref, slab, sem):
    del xs_in_ref
    tm = TM_DISP
    i = pl.program_id(0)
    last = pl.num_programs(0) - 1
    slot = i % 2

    def wait_copies(sl):
        for _ in range(TOP_K):
            pltpu.make_async_copy(slab.at[sl], xs_ref.at[pl.ds(0, tm * SLAB), :], sem.at[sl]).wait()

    @pl.when(i >= 2)
    def _():
        wait_copies(slot)

    h = _rms(x_ref[...], g_ref[...])
    for s in range(SLAB):
        slab[slot, pl.ds(s, tm, stride=SLAB), :] = h[:, s * LANES:(s + 1) * LANES]

    def body(c, carry):
        for u in range(GATHER_UNROLL):
            r = c * GATHER_UNROLL + u
            src = slab.at[slot, pl.ds(pl.multiple_of(r * SLAB, SLAB), SLAB), :]
            for k, pos_ref in enumerate((pos1_ref, pos2_ref)):
                p = pos_ref[i * tm + r]
                pltpu.make_async_copy(src, xs_ref.at[pl.ds(pl.multiple_of(p * SLAB, SLAB), SLAB), :],
                                      sem.at[slot]).start(priority=k % N_DMA_PRIORITIES)
        return carry

    lax.fori_loop(0, tm // GATHER_UNROLL, body, 0)

    @pl.when(i == last)
    def _():
        wait_copies(1 - slot)
        wait_copies(slot)


def _dispatch(pos1, pos2, x, g, xs0):
    tm = TM_DISP
    return pl.pallas_call(
        _dispatch_kernel,
        grid_spec=pltpu.PrefetchScalarGridSpec(
            num_scalar_prefetch=2,
            grid=(TOKENS // tm,),
            in_specs=[
                pl.BlockSpec((tm, D_MODEL), lambda i, p1, p2: (i, 0)),
                pl.BlockSpec((1, D_MODEL), lambda i, p1, p2: (0, 0)),
                pl.BlockSpec(memory_space=pl.ANY),
            ],
            out_specs=pl.BlockSpec(memory_space=pl.ANY),
            scratch_shapes=[pltpu.VMEM((2, tm * SLAB, LANES), F32), pltpu.SemaphoreType.DMA((2,))],
        ),
        out_shape=jax.ShapeDtypeStruct((SORTED_ROWS, LANES), F32),
        input_output_aliases={4: 0},
        compiler_params=_cparams(("arbitrary",), VMEM_MB),
        name="moe_dispatch",
    )(pos1, pos2, x, g, xs0)


def _expert_kernel(te_ref, nt_ref, xs_ref, wg_ref, wu_ref, wd_ref, ys_ref, xbuf, sem):
    del te_ref
    tm = TM_EXP
    i = pl.program_id(0)
    n = nt_ref[0]

    def tile_copy(tile):
        slot = tile % EXPERT_SLOTS
        rows = pl.ds(pl.multiple_of(tile * (tm * SLAB), tm * SLAB), tm * SLAB)
        return pltpu.make_async_copy(xs_ref.at[rows, :], xbuf.at[slot], sem.at[slot])

    @pl.when(i == 0)
    def _():
        for ahead in range(EXPERT_PREFETCH):
            @pl.when(ahead < n)
            def _():
                tile_copy(ahead).start()

    @pl.when(i + EXPERT_PREFETCH < n)
    def _():
        tile_copy(i + EXPERT_PREFETCH).start()

    @pl.when(i < n)
    def _():
        tile_copy(i).wait()
        slot = i % EXPERT_SLOTS
        xt = jnp.concatenate([xbuf[slot, pl.ds(s, tm, stride=SLAB), :].astype(BF16) for s in range(SLAB)], axis=1)
        hg = jnp.dot(xt, wg_ref[...].astype(BF16), preferred_element_type=F32)
        hu = jnp.dot(xt, wu_ref[...].astype(BF16), preferred_element_type=F32)
        hh = (hg * _sigmoid(hg) * hu).astype(BF16)
        y = jnp.dot(hh, wd_ref[...].astype(BF16), preferred_element_type=F32)
        for s in range(SLAB):
            ys_ref[pl.ds(s, tm, stride=SLAB), :] = y[:, s * LANES:(s + 1) * LANES]

    @pl.when(i >= n)
    def _():
        ys_ref[...] = jnp.zeros_like(ys_ref)


def _experts(layer, tile_expert, n_tiles, xs, w_gate, w_up, w_down):
    tm = TM_EXP
    wspec = lambda rows, cols: pl.BlockSpec((None, None, rows, cols), lambda i, te, nt: (layer, te[i], 0, 0))
    return pl.pallas_call(
        _expert_kernel,
        grid_spec=pltpu.PrefetchScalarGridSpec(
            num_scalar_prefetch=2,
            grid=(MAX_TILES,),
            in_specs=[
                pl.BlockSpec(memory_space=pl.ANY),
                wspec(D_MODEL, EXPERT_HIDDEN),
                wspec(D_MODEL, EXPERT_HIDDEN),
                wspec(EXPERT_HIDDEN, D_MODEL),
            ],
            out_specs=pl.BlockSpec((tm * SLAB, LANES), lambda i, te, nt: (i, 0)),
            scratch_shapes=[pltpu.VMEM((EXPERT_SLOTS, tm * SLAB, LANES), F32), pltpu.SemaphoreType.DMA((EXPERT_SLOTS,))],
        ),
        out_shape=jax.ShapeDtypeStruct((SORTED_ROWS, LANES), F32),
        compiler_params=_cparams(("arbitrary",), VMEM_MB),
        name="moe_experts",
    )(tile_expert, n_tiles, xs, w_gate, w_up, w_down)


def _start_slab_gathers(idx_refs, base, n_rows, src_hbm, dst_bufs, sem):
    def body(c, carry):
        for u in range(GATHER_UNROLL):
            r = c * GATHER_UNROLL + u
            for k, (idx_ref, dst) in enumerate(zip(idx_refs, dst_bufs)):
                t = idx_ref[base + r]
                pltpu.make_async_copy(src_hbm.at[pl.ds(pl.multiple_of(t * SLAB, SLAB), SLAB), :],
                                      dst.at[pl.ds(pl.multiple_of(r * SLAB, SLAB), SLAB), :],
                                      sem).start(priority=k % N_DMA_PRIORITIES)
        return carry

    lax.fori_loop(0, n_rows // GATHER_UNROLL, body, 0)


def _wait_slab_gathers(n_rows, src_hbm, dst, sem):
    pltpu.make_async_copy(src_hbm.at[pl.ds(0, n_rows * SLAB), :], dst, sem).wait()


def _combine_ple_kernel(pos1_ref, pos2_ref, x_ref, meta_ref, ys_ref, p_ref, g_ref, wg_ref, wp_ref, fg_ref,
                        out_ref, cbuf, sem, *, final):
    tm = TM_COMB
    i = pl.program_id(0)

    def gather(tile, slot):
        _start_slab_gathers((pos1_ref, pos2_ref), tile * tm, tm, ys_ref, (cbuf.at[slot, 0], cbuf.at[slot, 1]),
                            sem.at[slot])

    @pl.when(i == 0)
    def _():
        gather(0, 0)

    @pl.when(i + 1 < pl.num_programs(0))
    def _():
        gather(i + 1, (i + 1) % 2)

    slot = i % 2
    for k in range(2):
        _wait_slab_gathers(tm, ys_ref, cbuf.at[slot, k], sem.at[slot])
    meta = meta_ref[...]
    w1 = meta[:, META_W1:META_W1 + 1]
    w2 = meta[:, META_W2:META_W2 + 1]
    moe = jnp.concatenate([w1 * cbuf[slot, 0, pl.ds(s, tm, stride=SLAB), :]
                           + w2 * cbuf[slot, 1, pl.ds(s, tm, stride=SLAB), :] for s in range(SLAB)], axis=1)
    x = x_ref[...] + moe
    h = _rms(x, g_ref[...]).astype(BF16)
    gate = _sigmoid(jnp.dot(h, wg_ref[...], preferred_element_type=F32))
    emb = jnp.dot(p_ref[...].astype(BF16), wp_ref[...], preferred_element_type=F32)
    y = x + gate * emb
    if final:
        y = _rms(y, fg_ref[...])
    out_ref[...] = y


def _combine_ple(layer, pos1, pos2, x, meta, ys, p, g, wg, wp, fg, final):
    tm = TM_COMB
    const = lambda i, p1, p2: (0, 0)
    rows = lambda i, p1, p2: (i, 0)
    return pl.pallas_call(
        functools.partial(_combine_ple_kernel, final=final),
        grid_spec=pltpu.PrefetchScalarGridSpec(
            num_scalar_prefetch=2,
            grid=(TOKENS // tm,),
            in_specs=[
                pl.BlockSpec((tm, D_MODEL), rows),
                pl.BlockSpec((tm, LANES), rows),
                pl.BlockSpec(memory_space=pl.ANY),
                pl.BlockSpec((None, tm, PLE_DIM), lambda i, p1, p2: (layer, i, 0)),
                pl.BlockSpec((1, D_MODEL), const),
                pl.BlockSpec((D_MODEL, D_MODEL), const),
                pl.BlockSpec((PLE_DIM, D_MODEL), const),
                pl.BlockSpec((1, D_MODEL), const),
            ],
            out_specs=pl.BlockSpec((tm, D_MODEL), rows),
            scratch_shapes=[pltpu.VMEM((2, 2, tm * SLAB, LANES), F32), pltpu.SemaphoreType.DMA((2,))],
        ),
        out_shape=jax.ShapeDtypeStruct((TOKENS, D_MODEL), F32),
        compiler_params=_cparams(("arbitrary",), VMEM_MB),
        name="moe_combine_ple",
    )(pos1, pos2, x, meta, ys, p, g, wg, wp, fg)


def _in_proj_weights(w_in):
    k_rope = w_in[:, MAIN_COLS:MAIN_COLS + QK_ROPE]
    glu = w_in[:, MAIN_COLS + QK_ROPE:]
    half = QK_ROPE // 2
    zn = jnp.zeros((D_MODEL, QK_NOPE), F32)
    zp = jnp.zeros((D_MODEL, HEAD_PAD - QK_NOPE - QK_ROPE), F32)
    tail = jnp.concatenate([glu, zn, k_rope, zp, zn, k_rope[:, half:], k_rope[:, :half], zp], axis=1)
    return w_in[:, :MAIN_COLS].astype(BF16), tail.astype(BF16)


def _q_weight(w_uq):
    scale = (QK_NOPE + QK_ROPE) ** -0.5 * LOG2_E
    w = (w_uq * scale).reshape(Q_LORA, N_HEADS, QK_NOPE + QK_ROPE)
    zp = jnp.zeros((Q_LORA, N_HEADS, HEAD_PAD - QK_NOPE - QK_ROPE), F32)
    return jnp.concatenate([w, zp], axis=2).reshape(Q_LORA, N_HEADS * HEAD_PAD).astype(BF16)


def _kv_weight(w_ukv):
    w = w_ukv.reshape(KV_LORA, N_HEADS, QK_NOPE + V_HEAD)
    k_nope, v = w[:, :, :QK_NOPE], w[:, :, QK_NOPE:]
    z = jnp.zeros_like(v)
    k_part = jnp.concatenate([k_nope, jnp.zeros_like(k_nope)], axis=2).reshape(KV_LORA, N_HEADS * HEAD_PAD)
    odd = (jnp.arange(N_HEADS) % 2 == 1)[None, :, None]
    v_part = jnp.concatenate([jnp.where(odd, z, v), jnp.where(odd, v, z)], axis=2).reshape(KV_LORA, N_HEADS * HEAD_PAD)
    return jnp.concatenate([k_part, v_part], axis=1).astype(BF16)


def _router_weight(w_rg, b_rg, w_re, b_re):
    pad = LANES - N_EXPERTS - N_GROUPS
    w = jnp.concatenate([w_re, w_rg, jnp.zeros((D_MODEL, pad), F32)], axis=1)
    b = jnp.concatenate([b_re, b_rg, jnp.zeros((pad,), F32)]).reshape(1, LANES)
    w_hi = w.astype(BF16)
    w_lo = (w - w_hi.astype(F32)).astype(BF16)
    return jnp.concatenate([w_hi, w_lo], axis=1), b


def kernel(x, p, positions, ln_mix_g, w_in, conv_a_w, w_out_a, q_norm_g, w_uq, kv_norm_g, w_ukv, w_out_b, conv_c_w, ln_c_g, ln_c_b, w_out_c, w_o, ln_ffn_g, w_route_grp, b_route_grp, w_route_exp, b_route_exp, w_exp_gate, w_exp_up, w_exp_down, ln_ple_g, w_ple_gate, w_ple, final_norm_g):
    c_tab, s_tab = _rope_tables(positions)
    xf = x.reshape(TOKENS, D_MODEL)
    pf = p.reshape(DEPTH, TOKENS, PLE_DIM)
    row = lambda a: a.reshape(1, -1)
    for i in range(DEPTH):
        proj, tail = _inproj(xf, row(ln_mix_g[i]), *_in_proj_weights(w_in[i]))
        q, k, v = _qkv(proj, tail, c_tab, s_tab, row(q_norm_g[i]), row(kv_norm_g[i]), _q_weight(w_uq[i]), _kv_weight(w_ukv[i]))
        att = _attention(q, k, v)
        wr, br = _router_weight(w_route_grp[i], b_route_grp[i], w_route_exp[i], b_route_exp[i])
        xf, meta, meta_t, counts, xs0 = _mixer_tail(
            proj, tail, att, xf, conv_a_w[i], w_out_a[i].astype(BF16), conv_c_w[i], row(ln_c_g[i]), row(ln_c_b[i]),
            w_out_c[i].astype(BF16), w_out_b[i].astype(BF16), w_o[i].astype(BF16), row(ln_ffn_g[i]), wr, br)
        pos1, pos2, tile_expert, n_tiles = _dispatch_plan(meta_t, counts)
        xs = _dispatch(pos1, pos2, xf, row(ln_ffn_g[i]), xs0)
        ys = _experts(i, tile_expert, n_tiles, xs, w_exp_gate, w_exp_up, w_exp_down)
        xf = _combine_ple(i, pos1, pos2, xf, meta, ys, pf, row(ln_ple_g[i]), w_ple_gate[i].astype(BF16),
                          w_ple[i].astype(BF16), row(final_norm_g), final=(i == DEPTH - 1))
    return xf.reshape(BATCH, SEQ, D_MODEL)
```

```python
import functools

import jax
import jax.numpy as jnp
from jax import lax
from jax.experimental import pallas as pl
from jax.experimental.pallas import tpu as pltpu

D_MODEL = 1024
BATCH = 8
SEQ = 2048
DEPTH = 2
TOKENS = BATCH * SEQ
PLE_DIM = 256
SC_WIDTH = 512
SC_KERNEL = 3
N_HEADS = 8
QK_NOPE = 64
QK_ROPE = 32
V_HEAD = 64
Q_LORA = 768
KV_LORA = 256
ROPE_THETA = 10000.0
CONF_WIDTH = 512
CONF_KERNEL = 31
N_GROUPS = 4
EXPERTS_PER_GROUP = 8
N_EXPERTS = N_GROUPS * EXPERTS_PER_GROUP
EXPERT_HIDDEN = 256
EPS = 1e-6
LOG2_E = 1.4426950408889634

LANES = 128
HEAD_PAD = 128
F32 = jnp.float32
BF16 = jnp.bfloat16

COL_GATES = 0
COL_SC = 3 * D_MODEL
COL_QLAT = COL_SC + 3 * SC_WIDTH
COL_KVLAT = COL_QLAT + Q_LORA
MAIN_COLS = COL_KVLAT + KV_LORA
TCOL_GLU = 0
TCOL_KR = 2 * CONF_WIDTH
TCOL_KRSW = TCOL_KR + HEAD_PAD
TAIL_COLS = TCOL_KRSW + HEAD_PAD

ROUTER_GROUP_LANE = N_EXPERTS

MXU_TILE = 256
VMEM_MB = 48
VMEM_BIG_MB = 56

TM_INPROJ = 512
TN_INPROJ = 6 * MXU_TILE
TM_QKV = 1024
T_ATTN = 512
HEADS_PER_STEP = 8
TM_MIX = 512
HALO_C = 32
HALO_A = 16
TM_EXP = 512
EXPERT_PREFETCH = 2
EXPERT_SLOTS = EXPERT_PREFETCH + 1
TM_DISP = 512
TM_COMB = 256
TOP_K = 2
MAX_TILES = TOKENS * TOP_K // TM_EXP + N_EXPERTS
SLAB = D_MODEL // LANES
GATHER_UNROLL = 16
N_DMA_PRIORITIES = 2
META_E1, META_E2, META_R1, META_R2, META_W1, META_W2 = range(6)
META_ROWS = 8
SORTED_ROWS = MAX_TILES * TM_EXP * SLAB
ZERO_COPIES = 6
ZERO_ROWS = SORTED_ROWS // (TOKENS // TM_MIX) // ZERO_COPIES


def _cparams(semantics, vmem_mb):
    return pltpu.CompilerParams(dimension_semantics=semantics, vmem_limit_bytes=vmem_mb * 1024 * 1024)


def _sigmoid(x):
    return 1.0 / (1.0 + jnp.exp2(x * -LOG2_E))


def _rms(x, g):
    return x * lax.rsqrt(jnp.mean(x * x, axis=-1, keepdims=True) + EPS) * g


def _place(x, onehot):
    x1 = x.astype(BF16)
    r1 = x - x1.astype(F32)
    x2 = r1.astype(BF16)
    x3 = (r1 - x2.astype(F32)).astype(BF16)
    return (jnp.dot(x1, onehot, preferred_element_type=F32) + jnp.dot(x2, onehot, preferred_element_type=F32)
            + jnp.dot(x3, onehot, preferred_element_type=F32))


def _rope_kernel(pos_ref, freq_ref, c_ref, s_ref):
    half = QK_ROPE // 2
    per_row = LANES // half
    rows = TOKENS // per_row
    ang = pos_ref[...].astype(F32) * freq_ref[...]
    cos = jnp.cos(ang)
    sin = jnp.sin(ang)
    src = lax.broadcasted_iota(jnp.int32, (LANES, LANES), 0)
    dst = lax.broadcasted_iota(jnp.int32, (LANES, LANES), 1)
    lane = lax.broadcasted_iota(jnp.int32, (1, LANES), 1)
    ones_nope = jnp.where(lane < QK_NOPE, 1.0, 0.0)
    sign = jnp.where(lane < QK_NOPE + half, -1.0, 1.0)
    for j in range(per_row):
        f = src - half * j
        hit = (dst == QK_NOPE + f) | (dst == QK_NOPE + half + f)
        onehot = jnp.where((f >= 0) & (f < half) & hit, 1.0, 0.0).astype(BF16)
        c_ref[pl.ds(j, rows, stride=per_row), :] = _place(cos, onehot) + ones_nope
        s_ref[pl.ds(j, rows, stride=per_row), :] = _place(sin, onehot) * sign


def _rope_tables(positions):
    half = QK_ROPE // 2
    inv_freq = ROPE_THETA ** (-jnp.arange(0, QK_ROPE, 2, dtype=F32) / QK_ROPE)
    rows = TOKENS * half // LANES
    pos_rep = jnp.broadcast_to(positions.reshape(TOKENS, 1), (TOKENS, half)).reshape(rows, LANES)
    freq = jnp.tile(inv_freq, LANES // half).reshape(1, LANES)
    return pl.pallas_call(
        _rope_kernel,
        out_shape=(jax.ShapeDtypeStruct((TOKENS, HEAD_PAD), F32),) * 2,
        compiler_params=_cparams(None, VMEM_MB),
        name="rope_tables",
    )(pos_rep, freq)


def _inproj_kernel(x_ref, g_ref, wm_ref, wt_ref, om_ref, ot_ref):
    h = _rms(x_ref[...], g_ref[...]).astype(BF16)
    for lo in range(0, MAIN_COLS, TN_INPROJ):
        hi = min(lo + TN_INPROJ, MAIN_COLS)
        om_ref[:, lo:hi] = jnp.dot(h, wm_ref[:, lo:hi], preferred_element_type=F32).astype(BF16)
    ot_ref[...] = jnp.dot(h, wt_ref[...], preferred_element_type=F32).astype(BF16)


def _inproj(x, g, w_main, w_tail):
    tm = TM_INPROJ
    return pl.pallas_call(
        _inproj_kernel,
        grid=(TOKENS // tm,),
        in_specs=[
            pl.BlockSpec((tm, D_MODEL), lambda i: (i, 0)),
            pl.BlockSpec((1, D_MODEL), lambda i: (0, 0)),
            pl.BlockSpec((D_MODEL, MAIN_COLS), lambda i: (0, 0)),
            pl.BlockSpec((D_MODEL, TAIL_COLS), lambda i: (0, 0)),
        ],
        out_specs=[
            pl.BlockSpec((tm, MAIN_COLS), lambda i: (i, 0)),
            pl.BlockSpec((tm, TAIL_COLS), lambda i: (i, 0)),
        ],
        out_shape=(jax.ShapeDtypeStruct((TOKENS, MAIN_COLS), BF16), jax.ShapeDtypeStruct((TOKENS, TAIL_COLS), BF16)),
        compiler_params=_cparams(("parallel",), VMEM_BIG_MB),
        name="in_proj",
    )(x, g, w_main, w_tail)


def _qkv_kernel(ql_ref, kvl_ref, kr_ref, krsw_ref, c_ref, s_ref, qg_ref, kvg_ref, wq_ref, wkv_ref,
                q_out, k_out, v_out):
    c = c_ref[...]
    s = s_ref[...]
    width = N_HEADS * HEAD_PAD
    half = QK_ROPE // 2
    low_half = lax.broadcasted_iota(jnp.int32, c.shape, 1) < QK_NOPE + half

    def swap_halves(x):
        return jnp.where(low_half, pltpu.roll(x, HEAD_PAD - half, axis=1), pltpu.roll(x, half, axis=1))

    qn = _rms(ql_ref[...].astype(F32), qg_ref[...]).astype(BF16)
    qq = jnp.dot(qn, wq_ref[...], preferred_element_type=F32)
    for h in range(N_HEADS):
        lo, hi = h * HEAD_PAD, (h + 1) * HEAD_PAD
        q_out[:, lo:hi] = (qq[:, lo:hi] * c + swap_halves(qq[:, lo:hi]) * s).astype(BF16)
    kvn = _rms(kvl_ref[...].astype(F32), kvg_ref[...]).astype(BF16)
    kk = jnp.dot(kvn, wkv_ref[...], preferred_element_type=F32)
    kr = kr_ref[...].astype(F32) * c + krsw_ref[...].astype(F32) * s
    for h in range(N_HEADS):
        lo, hi = h * HEAD_PAD, (h + 1) * HEAD_PAD
        k_out[:, lo:hi] = (kk[:, lo:hi] + kr).astype(BF16)
    v_out[...] = kk[:, width:].astype(BF16)


def _qkv(proj, tail, c_tab, s_tab, qg, kvg, wq, wkv):
    tm = TM_QKV
    width = N_HEADS * HEAD_PAD
    row = lambda blk: (lambda i: (i, blk))
    const = lambda i: (0, 0)
    return pl.pallas_call(
        _qkv_kernel,
        grid=(TOKENS // tm,),
        in_specs=[
            pl.BlockSpec((tm, Q_LORA), row(COL_QLAT // Q_LORA)),
            pl.BlockSpec((tm, KV_LORA), row(COL_KVLAT // KV_LORA)),
            pl.BlockSpec((tm, HEAD_PAD), row(TCOL_KR // HEAD_PAD)),
            pl.BlockSpec((tm, HEAD_PAD), row(TCOL_KRSW // HEAD_PAD)),
            pl.BlockSpec((tm, HEAD_PAD), row(0)),
            pl.BlockSpec((tm, HEAD_PAD), row(0)),
            pl.BlockSpec((1, Q_LORA), const),
            pl.BlockSpec((1, KV_LORA), const),
            pl.BlockSpec((Q_LORA, width), const),
            pl.BlockSpec((KV_LORA, 2 * width), const),
        ],
        out_specs=[pl.BlockSpec((tm, width), row(0))] * 3,
        out_shape=(jax.ShapeDtypeStruct((TOKENS, width), BF16),) * 3,
        compiler_params=_cparams(("parallel",), VMEM_MB),
        name="qkv_prep",
    )(proj, proj, tail, tail, c_tab, s_tab, qg, kvg, wq, wkv)


def _attn_kernel(q_ref, k_ref, v_ref, o_ref, *state):
    t = T_ATTN
    nh = HEADS_PER_STEP
    qi = pl.program_id(2)
    nt = (((1,), (1,)), ((), ()))
    m_sc, l_sc, acc_sc = state[0:nh], state[nh:2 * nh], state[2 * nh:3 * nh]
    for h in range(nh):
        m_sc[h][...] = jnp.full((t, LANES), -jnp.inf, F32)
        l_sc[h][...] = jnp.zeros((t, LANES), F32)
        acc_sc[h][...] = jnp.zeros((t, LANES), F32)

    def block(j, r0, nr, c0, nc, masked):
        start = pl.multiple_of(j * t + c0, nc)
        rows = slice(r0, r0 + nr)
        for h in range(nh):
            lo, hi = h * HEAD_PAD, (h + 1) * HEAD_PAD
            s = lax.dot_general(q_ref[rows, lo:hi], k_ref[pl.ds(start, nc), lo:hi], nt, preferred_element_type=F32)
            if masked:
                row_id = r0 + lax.broadcasted_iota(jnp.int32, (nr, nc), 0)
                col_id = c0 + lax.broadcasted_iota(jnp.int32, (nr, nc), 1)
                s = jnp.where(row_id >= col_id, s, -jnp.inf)
            blocks = [s[:, c * LANES:(c + 1) * LANES] for c in range(nc // LANES)]
            bmax = functools.reduce(jnp.maximum, blocks)
            m_old = m_sc[h][rows, :]
            m_new = jnp.maximum(m_old, jnp.max(bmax, axis=-1, keepdims=True))
            alpha = jnp.exp2(m_old - m_new)
            ps = [jnp.exp2(b - m_new) for b in blocks]
            p = jnp.concatenate(ps, axis=1).astype(BF16)
            l_sc[h][rows, :] = alpha * l_sc[h][rows, :] + functools.reduce(jnp.add, ps)
            acc_sc[h][rows, :] = alpha * acc_sc[h][rows, :] + jnp.dot(p, v_ref[pl.ds(start, nc), lo:hi],
                                                                       preferred_element_type=F32)
            m_sc[h][rows, :] = m_new

    def body(j, carry):
        block(j, 0, t, 0, t, False)
        return carry

    lax.fori_loop(0, qi, body, 0)
    half = t // 2
    block(qi, 0, t, 0, half, True)
    block(qi, half, half, half, half, True)
    out = [acc_sc[h][...] / jnp.sum(l_sc[h][...], axis=-1, keepdims=True) for h in range(nh)]
    o_ref[...] = jnp.concatenate([out[h] + out[h + 1] for h in range(0, nh, 2)], axis=1).astype(BF16)


def _attention(q, k, v):
    t = T_ATTN
    nq = SEQ // t
    nh = HEADS_PER_STEP
    return pl.pallas_call(
        _attn_kernel,
        grid=(BATCH, N_HEADS // nh, nq),
        in_specs=[
            pl.BlockSpec((t, nh * HEAD_PAD), lambda b, hg, i: (b * nq + i, hg)),
            pl.BlockSpec((SEQ, nh * HEAD_PAD), lambda b, hg, i: (b, hg)),
            pl.BlockSpec((SEQ, nh * HEAD_PAD), lambda b, hg, i: (b, hg)),
        ],
        out_specs=pl.BlockSpec((t, nh * V_HEAD), lambda b, hg, i: (b * nq + i, hg)),
        out_shape=jax.ShapeDtypeStruct((TOKENS, N_HEADS * V_HEAD), BF16),
        scratch_shapes=[pltpu.VMEM((t, LANES), F32)] * (3 * nh),
        compiler_params=_cparams(("parallel", "parallel", "arbitrary"), VMEM_MB),
        name="mla_attention",
    )(q, k, v)


def _mixer_tail_kernel(gates_ref, sc_ref, sch_ref, gv_ref, gg_ref, gvh_ref, ggh_ref, att_ref, x_ref,
                       cwa_ref, woa_ref, cwc_ref, lng_ref, lnb_ref, woc_ref, wob_ref, wo_ref, gf_ref, wr_ref, br_ref,
                       out_ref, meta_ref, meta_t_ref, cnt_ref, xs0_ref, cbuf, ubuf, shifted, below, run_ref, zbuf, zsem):
    tm = TM_MIX
    step = pl.program_id(0)

    @pl.when(step == 0)
    def _():
        run_ref[...] = jnp.zeros_like(run_ref)
        zbuf[...] = jnp.zeros_like(zbuf)
        row_id = lax.broadcasted_iota(jnp.int32, (tm, tm), 0)
        col_id = lax.broadcasted_iota(jnp.int32, (tm, tm), 1)
        below[...] = jnp.where(row_id > col_id, 1.0, 0.0).astype(BF16)

    def zero_copy(c):
        rows = pl.ds(pl.multiple_of((step * ZERO_COPIES + c) * ZERO_ROWS, ZERO_ROWS), ZERO_ROWS)
        return pltpu.make_async_copy(zbuf, xs0_ref.at[rows, :], zsem.at[0])

    for c in range(ZERO_COPIES):
        zero_copy(c).start()
    has_past = (pl.program_id(0) % (SEQ // tm)) != 0

    sc = sc_ref[...]
    sc_b = sc[:, 0:SC_WIDTH].astype(F32)
    cbuf[8:8 + tm, :] = sc[:, SC_WIDTH:2 * SC_WIDTH].astype(F32) * sc[:, 2 * SC_WIDTH:].astype(F32)
    sch = sch_ref[...].astype(F32)[HALO_A - 8:HALO_A, :]
    cbuf[0:8, :] = jnp.where(has_past, sch[:, SC_WIDTH:2 * SC_WIDTH] * sch[:, 2 * SC_WIDTH:], 0.0)
    conv_a = cwa_ref[0:1, :] * cbuf[6:6 + tm, :]
    for t in range(1, SC_KERNEL):
        conv_a = conv_a + cwa_ref[t:t + 1, :] * cbuf[6 + t:6 + t + tm, :]
    y_a = jnp.dot((sc_b * conv_a).astype(BF16), woa_ref[...], preferred_element_type=F32)

    ubuf[HALO_C:HALO_C + tm, :] = gv_ref[...].astype(F32) * _sigmoid(gg_ref[...].astype(F32))
    ubuf[0:HALO_C, :] = jnp.where(has_past, gvh_ref[...].astype(F32) * _sigmoid(ggh_ref[...].astype(F32)), 0.0)
    rows = tm + HALO_C - 8
    u_all = ubuf[...]
    for b in range(1, 8):
        shifted[b - 1, 0:rows, :] = pltpu.roll(u_all, tm + HALO_C - b, axis=0)[0:rows, :]
    base = HALO_C - (CONF_KERNEL - 1)
    acc = None
    for t in range(CONF_KERNEL):
        off = base + t
        a0 = off - off % 8
        src = ubuf[a0:a0 + tm, :] if off % 8 == 0 else shifted[off % 8 - 1, a0:a0 + tm, :]
        term = cwc_ref[t:t + 1, :] * src
        acc = term if acc is None else acc + term
    mu = jnp.mean(acc, axis=-1, keepdims=True)
    xc = acc - mu
    var = jnp.mean(xc * xc, axis=-1, keepdims=True)
    y = xc * lax.rsqrt(var + EPS) * lng_ref[...] + lnb_ref[...]
    y_c = jnp.dot((y * _sigmoid(y)).astype(BF16), woc_ref[...], preferred_element_type=F32)

    y_b = jnp.dot(att_ref[...], wob_ref[...], preferred_element_type=F32)

    g = gates_ref[...]
    merged = (_sigmoid(g[:, 0:D_MODEL].astype(F32)) * y_a
              + _sigmoid(g[:, D_MODEL:2 * D_MODEL].astype(F32)) * y_b
              + _sigmoid(g[:, 2 * D_MODEL:].astype(F32)) * y_c)
    x_new = x_ref[...] + jnp.dot(merged.astype(BF16), wo_ref[...], preferred_element_type=F32)
    out_ref[...] = x_new
    _route_tile(x_new, gf_ref, wr_ref, br_ref, below, run_ref, meta_ref, meta_t_ref, cnt_ref)
    for c in range(ZERO_COPIES):
        zero_copy(c).wait()


def _mixer_tail(proj, tail, att, x, cwa, woa, cwc, lng, lnb, woc, wob, wo, gf, wr, br):
    tm = TM_MIX
    row = lambda width, col: pl.BlockSpec((tm, width), lambda i: (i, col // width))
    halo = lambda rows, width, col: pl.BlockSpec(
        (rows, width), lambda i: (jnp.maximum(i * (tm // rows) - 1, 0), col // width))
    const = lambda a: pl.BlockSpec(a.shape, lambda i: (0,) * a.ndim)
    weights = (cwa, woa, cwc, lng, lnb, woc, wob, wo, gf, wr, br)
    return pl.pallas_call(
        _mixer_tail_kernel,
        grid=(TOKENS // tm,),
        in_specs=[
            row(3 * D_MODEL, COL_GATES),
            row(3 * SC_WIDTH, COL_SC),
            halo(HALO_A, 3 * SC_WIDTH, COL_SC),
            row(CONF_WIDTH, TCOL_GLU),
            row(CONF_WIDTH, TCOL_GLU + CONF_WIDTH),
            halo(HALO_C, CONF_WIDTH, TCOL_GLU),
            halo(HALO_C, CONF_WIDTH, TCOL_GLU + CONF_WIDTH),
            pl.BlockSpec((tm, N_HEADS * V_HEAD), lambda i: (i, 0)),
            pl.BlockSpec((tm, D_MODEL), lambda i: (i, 0)),
        ] + [const(a) for a in weights],
        out_specs=[
            pl.BlockSpec((tm, D_MODEL), lambda i: (i, 0)),
            pl.BlockSpec((tm, LANES), lambda i: (i, 0)),
            pl.BlockSpec((META_ROWS, tm), lambda i: (0, i)),
            pl.BlockSpec((1, LANES), lambda i: (0, 0)),
            pl.BlockSpec(memory_space=pl.ANY),
        ],
        out_shape=(
            jax.ShapeDtypeStruct((TOKENS, D_MODEL), F32),
            jax.ShapeDtypeStruct((TOKENS, LANES), F32),
            jax.ShapeDtypeStruct((META_ROWS, TOKENS), F32),
            jax.ShapeDtypeStruct((1, LANES), F32),
            jax.ShapeDtypeStruct((SORTED_ROWS, LANES), F32),
        ),
        scratch_shapes=[
            pltpu.VMEM((tm + 8, SC_WIDTH), F32),
            pltpu.VMEM((tm + HALO_C, CONF_WIDTH), F32),
            pltpu.VMEM((7, tm + HALO_C - 8, CONF_WIDTH), F32),
            pltpu.VMEM((tm, tm), BF16),
            pltpu.VMEM((1, LANES), F32),
            pltpu.VMEM((ZERO_ROWS, LANES), F32),
            pltpu.SemaphoreType.DMA((1,)),
        ],
        compiler_params=_cparams(("arbitrary",), VMEM_BIG_MB),
        name="mixer_tail",
    )(proj, proj, proj, tail, tail, tail, tail, att, x, *weights)


def _route(logits):
    lane = lax.broadcasted_iota(jnp.int32, logits.shape, 1)
    lane_f = lane.astype(F32)
    neg = -jnp.inf
    big = float(LANES)
    is_grp = (lane >= ROUTER_GROUP_LANE) & (lane < ROUTER_GROUP_LANE + N_GROUPS)
    glog = jnp.where(is_grp, logits, neg)
    gmax = jnp.max(glog, axis=-1, keepdims=True)
    gidx = jnp.min(jnp.where(glog == gmax, lane_f, big), axis=-1, keepdims=True)
    p_sel = 1.0 / jnp.sum(jnp.exp(glog - gmax), axis=-1, keepdims=True)
    first = (gidx - ROUTER_GROUP_LANE) * EXPERTS_PER_GROUP
    in_grp = (lane_f >= first) & (lane_f < first + EXPERTS_PER_GROUP)
    el = jnp.where(in_grp, logits, neg)
    m1 = jnp.max(el, axis=-1, keepdims=True)
    i1 = jnp.min(jnp.where(el == m1, lane_f, big), axis=-1, keepdims=True)
    el2 = jnp.where(lane_f == i1, neg, el)
    m2 = jnp.max(el2, axis=-1, keepdims=True)
    i2 = jnp.min(jnp.where(el2 == m2, lane_f, big), axis=-1, keepdims=True)
    e2 = jnp.exp(m2 - m1)
    w1 = p_sel / (1.0 + e2)
    w2 = w1 * e2
    return i1, i2, w1, w2


def _route_tile(x, g_ref, wr_ref, br_ref, below_ref, run_ref, meta_ref, meta_t_ref, cnt_ref):
    tm = x.shape[0]
    h = _rms(x, g_ref[...])
    h_hi = h.astype(BF16)
    h_lo = (h - h_hi.astype(F32)).astype(BF16)
    hi_terms = jnp.dot(h_hi, wr_ref[...], preferred_element_type=F32)
    logits = (hi_terms[:, :LANES] + hi_terms[:, LANES:]
              + jnp.dot(h_lo, wr_ref[:, :LANES], preferred_element_type=F32) + br_ref[...])
    i1, i2, w1, w2 = _route(logits)
    lane = lax.broadcasted_iota(jnp.int32, (tm, LANES), 1)
    lane_f = lane.astype(F32)
    oh1 = lane_f == i1
    oh2 = lane_f == i2
    onehot = jnp.where(oh1, 1.0, 0.0) + jnp.where(oh2, 1.0, 0.0)
    before = run_ref[...] + jnp.dot(below_ref[...], onehot.astype(BF16), preferred_element_type=F32)
    r1 = jnp.sum(jnp.where(oh1, before, 0.0), axis=-1, keepdims=True)
    r2 = jnp.sum(jnp.where(oh2, before, 0.0), axis=-1, keepdims=True)
    run_ref[...] += jnp.sum(onehot, axis=0, keepdims=True)
    cnt_ref[...] = run_ref[...]
    meta = jnp.zeros((tm, LANES), F32)
    for col, val in enumerate((i1, i2, r1, r2, w1, w2)):
        meta = jnp.where(lane == col, val, meta)
    meta_ref[...] = meta
    meta_t_ref[...] = meta.T[0:META_ROWS, :]


def _dispatch_plan(meta_t, counts):
    e1 = meta_t[META_E1].astype(jnp.int32)
    e2 = meta_t[META_E2].astype(jnp.int32)
    r1 = meta_t[META_R1].astype(jnp.int32)
    r2 = meta_t[META_R2].astype(jnp.int32)
    cnt = counts[0, :N_EXPERTS].astype(jnp.int32)
    tiles = (cnt + TM_EXP - 1) // TM_EXP
    tile_end = jnp.cumsum(tiles)
    first_slot = ((tile_end - tiles) * TM_EXP)[:, None]
    expert = jnp.arange(N_EXPERTS, dtype=jnp.int32)[:, None]
    pos1 = jnp.sum(jnp.where(e1[None, :] == expert, first_slot, 0), axis=0) + r1
    pos2 = jnp.sum(jnp.where(e2[None, :] == expert, first_slot, 0), axis=0) + r2
    n_tiles = tile_end[-1:]
    tile_id = jnp.minimum(jnp.arange(MAX_TILES, dtype=jnp.int32), n_tiles - 1)
    tile_expert = jnp.sum((tile_id[:, None] >= tile_end[None, :]).astype(jnp.int32), axis=1)
    return pos1, pos2, tile_expert, n_tiles


def _dispatch_kernel(pos1_ref, pos2_ref, x_ref, g_ref, xs_in_ref, xs_ref, slab, sem):
    del xs_in_ref
    tm = TM_DISP
    i = pl.program_id(0)
    last = pl.num_programs(0) - 1
    slot = i % 2

    def wait_copies(sl):
        for _ in range(TOP_K):
            pltpu.make_async_copy(slab.at[sl], xs_ref.at[pl.ds(0, tm * SLAB), :], sem.at[sl]).wait()

    @pl.when(i >= 2)
    def _():
        wait_copies(slot)

    h = _rms(x_ref[...], g_ref[...])
    for s in range(SLAB):
        slab[slot, pl.ds(s, tm, stride=SLAB), :] = h[:, s * LANES:(s + 1) * LANES]

    def body(c, carry):
        for u in range(GATHER_UNROLL):
            r = c * GATHER_UNROLL + u
            src = slab.at[slot, pl.ds(pl.multiple_of(r * SLAB, SLAB), SLAB), :]
            for k, pos_ref in enumerate((pos1_ref, pos2_ref)):
                p = pos_ref[i * tm + r]
                pltpu.make_async_copy(src, xs_ref.at[pl.ds(pl.multiple_of(p * SLAB, SLAB), SLAB), :],
                                      sem.at[slot]).start(priority=k % N_DMA_PRIORITIES)
        return carry

    lax.fori_loop(0, tm // GATHER_UNROLL, body, 0)

    @pl.when(i == last)
    def _():
        wait_copies(1 - slot)
        wait_copies(slot)


def _dispatch(pos1, pos2, x, g, xs0):
    tm = TM_DISP
    return pl.pallas_call(
        _dispatch_kernel,
        grid_spec=pltpu.PrefetchScalarGridSpec(
            num_scalar_prefetch=2,
            grid=(TOKENS // tm,),
            in_specs=[
                pl.BlockSpec((tm, D_MODEL), lambda i, p1, p2: (i, 0)),
                pl.BlockSpec((1, D_MODEL), lambda i, p1, p2: (0, 0)),
                pl.BlockSpec(memory_space=pl.ANY),
            ],
            out_specs=pl.BlockSpec(memory_space=pl.ANY),
            scratch_shapes=[pltpu.VMEM((2, tm * SLAB, LANES), F32), pltpu.SemaphoreType.DMA((2,))],
        ),
        out_shape=jax.ShapeDtypeStruct((SORTED_ROWS, LANES), F32),
        input_output_aliases={4: 0},
        compiler_params=_cparams(("arbitrary",), VMEM_MB),
        name="moe_dispatch",
    )(pos1, pos2, x, g, xs0)


def _expert_kernel(te_ref, nt_ref, first_ref, grp_ref, nxt_ref, xs_ref, wg_hbm, wu_hbm, wd_hbm, ys_ref,
                   xbuf, sem, wg_buf, wu_buf, wd_buf, wsem, *, layer):
    tm = TM_EXP
    i = pl.program_id(0)
    n = nt_ref[0]

    def weight_copies(expert, slot):
        return [pltpu.make_async_copy(hbm.at[layer, expert], buf.at[slot], wsem.at[slot])
                for hbm, buf in ((wg_hbm, wg_buf), (wu_hbm, wu_buf), (wd_hbm, wd_buf))]

    @pl.when(i == 0)
    def _():
        for c in weight_copies(te_ref[0], 0):
            c.start()

    @pl.when((i < n) & (first_ref[i] == 1))
    def _():
        for c in weight_copies(te_ref[i], grp_ref[i] % 2):
            c.wait()

        @pl.when(nxt_ref[i] >= 0)
        def _():
            for c in weight_copies(nxt_ref[i], (grp_ref[i] + 1) % 2):
                c.start()

    def tile_copy(tile):
        slot = tile % EXPERT_SLOTS
        rows = pl.ds(pl.multiple_of(tile * (tm * SLAB), tm * SLAB), tm * SLAB)
        return pltpu.make_async_copy(xs_ref.at[rows, :], xbuf.at[slot], sem.at[slot])

    @pl.when(i == 0)
    def _():
        for ahead in range(EXPERT_PREFETCH):
            @pl.when(ahead < n)
            def _():
                tile_copy(ahead).start()

    @pl.when(i + EXPERT_PREFETCH < n)
    def _():
        tile_copy(i + EXPERT_PREFETCH).start()

    @pl.when(i < n)
    def _():
        tile_copy(i).wait()
        slot = i % EXPERT_SLOTS
        ws = grp_ref[i] % 2
        xt = jnp.concatenate([xbuf[slot, pl.ds(s, tm, stride=SLAB), :].astype(BF16) for s in range(SLAB)], axis=1)
        hg = jnp.dot(xt, wg_buf[ws].astype(BF16), preferred_element_type=F32)
        hu = jnp.dot(xt, wu_buf[ws].astype(BF16), preferred_element_type=F32)
        hh = (hg * _sigmoid(hg) * hu).astype(BF16)
        y = jnp.dot(hh, wd_buf[ws].astype(BF16), preferred_element_type=F32)
        for s in range(SLAB):
            ys_ref[pl.ds(s, tm, stride=SLAB), :] = y[:, s * LANES:(s + 1) * LANES]

    @pl.when(i >= n)
    def _():
        ys_ref[...] = jnp.zeros_like(ys_ref)


def _experts(layer, tile_expert, n_tiles, xs, w_gate, w_up, w_down):
    tm = TM_EXP
    tile = jnp.arange(MAX_TILES, dtype=jnp.int32)
    prev = jnp.concatenate([jnp.full((1,), -1, jnp.int32), tile_expert[:-1]])
    first = (tile_expert != prev).astype(jnp.int32)
    run = jnp.cumsum(first) - 1
    later_first = (tile[None, :] > tile[:, None]) & (first[None, :] == 1)
    nxt_pos = jnp.min(jnp.where(later_first, tile[None, :], MAX_TILES), axis=1)
    nxt = jnp.sum(jnp.where(tile[None, :] == nxt_pos[:, None], tile_expert[None, :] + 1, 0), axis=1) - 1
    return pl.pallas_call(
        functools.partial(_expert_kernel, layer=layer),
        grid_spec=pltpu.PrefetchScalarGridSpec(
            num_scalar_prefetch=5,
            grid=(MAX_TILES,),
            in_specs=[pl.BlockSpec(memory_space=pl.ANY)] * 4,
            out_specs=pl.BlockSpec((tm * SLAB, LANES), lambda i, *_: (i, 0)),
            scratch_shapes=[
                pltpu.VMEM((EXPERT_SLOTS, tm * SLAB, LANES), F32), pltpu.SemaphoreType.DMA((EXPERT_SLOTS,)),
                pltpu.VMEM((2, D_MODEL, EXPERT_HIDDEN), F32), pltpu.VMEM((2, D_MODEL, EXPERT_HIDDEN), F32),
                pltpu.VMEM((2, EXPERT_HIDDEN, D_MODEL), F32), pltpu.SemaphoreType.DMA((2,)),
            ],
        ),
        out_shape=jax.ShapeDtypeStruct((SORTED_ROWS, LANES), F32),
        compiler_params=_cparams(("arbitrary",), VMEM_MB),
        name="moe_experts",
    )(tile_expert, n_tiles, first, run.astype(jnp.int32), nxt.astype(jnp.int32), xs, w_gate, w_up, w_down)


def _start_slab_gathers(idx_refs, base, n_rows, src_hbm, dst_bufs, sem):
    def body(c, carry):
        for u in range(GATHER_UNROLL):
            r = c * GATHER_UNROLL + u
            for k, (idx_ref, dst) in enumerate(zip(idx_refs, dst_bufs)):
                t = idx_ref[base + r]
                pltpu.make_async_copy(src_hbm.at[pl.ds(pl.multiple_of(t * SLAB, SLAB), SLAB), :],
                                      dst.at[pl.ds(pl.multiple_of(r * SLAB, SLAB), SLAB), :],
                                      sem).start(priority=k % N_DMA_PRIORITIES)
        return carry

    lax.fori_loop(0, n_rows // GATHER_UNROLL, body, 0)


def _wait_slab_gathers(n_rows, src_hbm, dst, sem):
    pltpu.make_async_copy(src_hbm.at[pl.ds(0, n_rows * SLAB), :], dst, sem).wait()


def _combine_ple_kernel(pos1_ref, pos2_ref, x_ref, meta_ref, ys_ref, p_ref, g_ref, wg_ref, wp_ref, fg_ref,
                        out_ref, cbuf, sem, *, final):
    tm = TM_COMB
    i = pl.program_id(0)

    def gather(tile, slot):
        _start_slab_gathers((pos1_ref, pos2_ref), tile * tm, tm, ys_ref, (cbuf.at[slot, 0], cbuf.at[slot, 1]),
                            sem.at[slot])

    @pl.when(i == 0)
    def _():
        gather(0, 0)

    @pl.when(i + 1 < pl.num_programs(0))
    def _():
        gather(i + 1, (i + 1) % 2)

    slot = i % 2
    for k in range(2):
        _wait_slab_gathers(tm, ys_ref, cbuf.at[slot, k], sem.at[slot])
    meta = meta_ref[...]
    w1 = meta[:, META_W1:META_W1 + 1]
    w2 = meta[:, META_W2:META_W2 + 1]
    moe = jnp.concatenate([w1 * cbuf[slot, 0, pl.ds(s, tm, stride=SLAB), :]
                           + w2 * cbuf[slot, 1, pl.ds(s, tm, stride=SLAB), :] for s in range(SLAB)], axis=1)
    x = x_ref[...] + moe
    h = _rms(x, g_ref[...]).astype(BF16)
    gate = _sigmoid(jnp.dot(h, wg_ref[...], preferred_element_type=F32))
    emb = jnp.dot(p_ref[...].astype(BF16), wp_ref[...], preferred_element_type=F32)
    y = x + gate * emb
    if final:
        y = _rms(y, fg_ref[...])
    out_ref[...] = y


def _combine_ple(layer, pos1, pos2, x, meta, ys, p, g, wg, wp, fg, final):
    tm = TM_COMB
    const = lambda i, p1, p2: (0, 0)
    rows = lambda i, p1, p2: (i, 0)
    return pl.pallas_call(
        functools.partial(_combine_ple_kernel, final=final),
        grid_spec=pltpu.PrefetchScalarGridSpec(
            num_scalar_prefetch=2,
            grid=(TOKENS // tm,),
            in_specs=[
                pl.BlockSpec((tm, D_MODEL), rows),
                pl.BlockSpec((tm, LANES), rows),
                pl.BlockSpec(memory_space=pl.ANY),
                pl.BlockSpec((None, tm, PLE_DIM), lambda i, p1, p2: (layer, i, 0)),
                pl.BlockSpec((1, D_MODEL), const),
                pl.BlockSpec((D_MODEL, D_MODEL), const),
                pl.BlockSpec((PLE_DIM, D_MODEL), const),
                pl.BlockSpec((1, D_MODEL), const),
            ],
            out_specs=pl.BlockSpec((tm, D_MODEL), rows),
            scratch_shapes=[pltpu.VMEM((2, 2, tm * SLAB, LANES), F32), pltpu.SemaphoreType.DMA((2,))],
        ),
        out_shape=jax.ShapeDtypeStruct((TOKENS, D_MODEL), F32),
        compiler_params=_cparams(("arbitrary",), VMEM_MB),
        name="moe_combine_ple",
    )(pos1, pos2, x, meta, ys, p, g, wg, wp, fg)


def _in_proj_weights(w_in):
    k_rope = w_in[:, MAIN_COLS:MAIN_COLS + QK_ROPE]
    glu = w_in[:, MAIN_COLS + QK_ROPE:]
    half = QK_ROPE // 2
    zn = jnp.zeros((D_MODEL, QK_NOPE), F32)
    zp = jnp.zeros((D_MODEL, HEAD_PAD - QK_NOPE - QK_ROPE), F32)
    tail = jnp.concatenate([glu, zn, k_rope, zp, zn, k_rope[:, half:], k_rope[:, :half], zp], axis=1)
    return w_in[:, :MAIN_COLS].astype(BF16), tail.astype(BF16)


def _q_weight(w_uq):
    scale = (QK_NOPE + QK_ROPE) ** -0.5 * LOG2_E
    w = (w_uq * scale).reshape(Q_LORA, N_HEADS, QK_NOPE + QK_ROPE)
    zp = jnp.zeros((Q_LORA, N_HEADS, HEAD_PAD - QK_NOPE - QK_ROPE), F32)
    return jnp.concatenate([w, zp], axis=2).reshape(Q_LORA, N_HEADS * HEAD_PAD).astype(BF16)


def _kv_weight(w_ukv):
    w = w_ukv.reshape(KV_LORA, N_HEADS, QK_NOPE + V_HEAD)
    k_nope, v = w[:, :, :QK_NOPE], w[:, :, QK_NOPE:]
    z = jnp.zeros_like(v)
    k_part = jnp.concatenate([k_nope, jnp.zeros_like(k_nope)], axis=2).reshape(KV_LORA, N_HEADS * HEAD_PAD)
    odd = (jnp.arange(N_HEADS) % 2 == 1)[None, :, None]
    v_part = jnp.concatenate([jnp.where(odd, z, v), jnp.where(odd, v, z)], axis=2).reshape(KV_LORA, N_HEADS * HEAD_PAD)
    return jnp.concatenate([k_part, v_part], axis=1).astype(BF16)


def _router_weight(w_rg, b_rg, w_re, b_re):
    pad = LANES - N_EXPERTS - N_GROUPS
    w = jnp.concatenate([w_re, w_rg, jnp.zeros((D_MODEL, pad), F32)], axis=1)
    b = jnp.concatenate([b_re, b_rg, jnp.zeros((pad,), F32)]).reshape(1, LANES)
    w_hi = w.astype(BF16)
    w_lo = (w - w_hi.astype(F32)).astype(BF16)
    return jnp.concatenate([w_hi, w_lo], axis=1), b


def kernel(x, p, positions, ln_mix_g, w_in, conv_a_w, w_out_a, q_norm_g, w_uq, kv_norm_g, w_ukv, w_out_b, conv_c_w, ln_c_g, ln_c_b, w_out_c, w_o, ln_ffn_g, w_route_grp, b_route_grp, w_route_exp, b_route_exp, w_exp_gate, w_exp_up, w_exp_down, ln_ple_g, w_ple_gate, w_ple, final_norm_g):
    c_tab, s_tab = _rope_tables(positions)
    xf = x.reshape(TOKENS, D_MODEL)
    pf = p.reshape(DEPTH, TOKENS, PLE_DIM)
    row = lambda a: a.reshape(1, -1)
    for i in range(DEPTH):
        proj, tail = _inproj(xf, row(ln_mix_g[i]), *_in_proj_weights(w_in[i]))
        q, k, v = _qkv(proj, tail, c_tab, s_tab, row(q_norm_g[i]), row(kv_norm_g[i]), _q_weight(w_uq[i]), _kv_weight(w_ukv[i]))
        att = _attention(q, k, v)
        wr, br = _router_weight(w_route_grp[i], b_route_grp[i], w_route_exp[i], b_route_exp[i])
        xf, meta, meta_t, counts, xs0 = _mixer_tail(
            proj, tail, att, xf, conv_a_w[i], w_out_a[i].astype(BF16), conv_c_w[i], row(ln_c_g[i]), row(ln_c_b[i]),
            w_out_c[i].astype(BF16), w_out_b[i].astype(BF16), w_o[i].astype(BF16), row(ln_ffn_g[i]), wr, br)
        pos1, pos2, tile_expert, n_tiles = _dispatch_plan(meta_t, counts)
        xs = _dispatch(pos1, pos2, xf, row(ln_ffn_g[i]), xs0)
        ys = _experts(i, tile_expert, n_tiles, xs, w_exp_gate, w_exp_up, w_exp_down)
        xf = _combine_ple(i, pos1, pos2, xf, meta, ys, pf, row(ln_ple_g[i]), w_ple_gate[i].astype(BF16),
                          w_ple[i].astype(BF16), row(final_norm_g), final=(i == DEPTH - 1))
    return xf.reshape(BATCH, SEQ, D_MODEL)
```

```python
import functools

import jax
import jax.numpy as jnp
from jax import lax
from jax.experimental import pallas as pl
from jax.experimental.pallas import tpu as pltpu

D_MODEL = 1024
BATCH = 8
SEQ = 2048
DEPTH = 2
TOKENS = BATCH * SEQ
PLE_DIM = 256
SC_WIDTH = 512
SC_KERNEL = 3
N_HEADS = 8
QK_NOPE = 64
QK_ROPE = 32
V_HEAD = 64
Q_LORA = 768
KV_LORA = 256
ROPE_THETA = 10000.0
CONF_WIDTH = 512
CONF_KERNEL = 31
N_GROUPS = 4
EXPERTS_PER_GROUP = 8
N_EXPERTS = N_GROUPS * EXPERTS_PER_GROUP
EXPERT_HIDDEN = 256
EPS = 1e-6
LOG2_E = 1.4426950408889634

LANES = 128
HEAD_PAD = 128
F32 = jnp.float32
BF16 = jnp.bfloat16

COL_GATES = 0
COL_SC = 3 * D_MODEL
COL_QLAT = COL_SC + 3 * SC_WIDTH
COL_KVLAT = COL_QLAT + Q_LORA
MAIN_COLS = COL_KVLAT + KV_LORA
TCOL_GLU = 0
TCOL_KR = 2 * CONF_WIDTH
TCOL_KRSW = TCOL_KR + HEAD_PAD
TAIL_COLS = TCOL_KRSW + HEAD_PAD

ROUTER_GROUP_LANE = N_EXPERTS

MXU_TILE = 256
VMEM_MB = 48
VMEM_BIG_MB = 56

TM_INPROJ = 512
TN_INPROJ = 6 * MXU_TILE
TM_QKV = 1024
T_ATTN = 512
HEADS_PER_STEP = 8
TM_MIX = 512
HALO_C = 32
HALO_A = 16
TM_EXP = 256
EXPERT_PREFETCH = 2
EXPERT_SLOTS = EXPERT_PREFETCH + 1
TM_DISP = 512
TM_COMB = 256
TOP_K = 2
MAX_TILES = TOKENS * TOP_K // TM_EXP + N_EXPERTS
SLAB = D_MODEL // LANES
GATHER_UNROLL = 16
N_DMA_PRIORITIES = 2
META_E1, META_E2, META_R1, META_R2, META_W1, META_W2 = range(6)
META_ROWS = 8
SORTED_ROWS = MAX_TILES * TM_EXP * SLAB
ZERO_COPIES = 6
ZERO_ROWS = SORTED_ROWS // (TOKENS // TM_MIX) // ZERO_COPIES


def _cparams(semantics, vmem_mb):
    return pltpu.CompilerParams(dimension_semantics=semantics, vmem_limit_bytes=vmem_mb * 1024 * 1024)


def _sigmoid(x):
    return 1.0 / (1.0 + jnp.exp2(x * -LOG2_E))


def _rms(x, g):
    return x * lax.rsqrt(jnp.mean(x * x, axis=-1, keepdims=True) + EPS) * g


def _place(x, onehot):
    x1 = x.astype(BF16)
    r1 = x - x1.astype(F32)
    x2 = r1.astype(BF16)
    x3 = (r1 - x2.astype(F32)).astype(BF16)
    return (jnp.dot(x1, onehot, preferred_element_type=F32) + jnp.dot(x2, onehot, preferred_element_type=F32)
            + jnp.dot(x3, onehot, preferred_element_type=F32))


def _rope_kernel(pos_ref, freq_ref, c_ref, s_ref):
    half = QK_ROPE // 2
    per_row = LANES // half
    rows = TOKENS // per_row
    ang = pos_ref[...].astype(F32) * freq_ref[...]
    cos = jnp.cos(ang)
    sin = jnp.sin(ang)
    src = lax.broadcasted_iota(jnp.int32, (LANES, LANES), 0)
    dst = lax.broadcasted_iota(jnp.int32, (LANES, LANES), 1)
    lane = lax.broadcasted_iota(jnp.int32, (1, LANES), 1)
    ones_nope = jnp.where(lane < QK_NOPE, 1.0, 0.0)
    sign = jnp.where(lane < QK_NOPE + half, -1.0, 1.0)
    for j in range(per_row):
        f = src - half * j
        hit = (dst == QK_NOPE + f) | (dst == QK_NOPE + half + f)
        onehot = jnp.where((f >= 0) & (f < half) & hit, 1.0, 0.0).astype(BF16)
        c_ref[pl.ds(j, rows, stride=per_row), :] = _place(cos, onehot) + ones_nope
        s_ref[pl.ds(j, rows, stride=per_row), :] = _place(sin, onehot) * sign


def _rope_tables(positions):
    half = QK_ROPE // 2
    inv_freq = ROPE_THETA ** (-jnp.arange(0, QK_ROPE, 2, dtype=F32) / QK_ROPE)
    rows = TOKENS * half // LANES
    pos_rep = jnp.broadcast_to(positions.reshape(TOKENS, 1), (TOKENS, half)).reshape(rows, LANES)
    freq = jnp.tile(inv_freq, LANES // half).reshape(1, LANES)
    return pl.pallas_call(
        _rope_kernel,
        out_shape=(jax.ShapeDtypeStruct((TOKENS, HEAD_PAD), F32),) * 2,
        compiler_params=_cparams(None, VMEM_MB),
        name="rope_tables",
    )(pos_rep, freq)


def _inproj_kernel(x_ref, g_ref, wm_ref, wt_ref, om_ref, ot_ref):
    h = _rms(x_ref[...], g_ref[...]).astype(BF16)
    for lo in range(0, MAIN_COLS, TN_INPROJ):
        hi = min(lo + TN_INPROJ, MAIN_COLS)
        om_ref[:, lo:hi] = jnp.dot(h, wm_ref[:, lo:hi], preferred_element_type=F32).astype(BF16)
    ot_ref[...] = jnp.dot(h, wt_ref[...], preferred_element_type=F32).astype(BF16)


def _inproj(x, g, w_main, w_tail):
    tm = TM_INPROJ
    return pl.pallas_call(
        _inproj_kernel,
        grid=(TOKENS // tm,),
        in_specs=[
            pl.BlockSpec((tm, D_MODEL), lambda i: (i, 0)),
            pl.BlockSpec((1, D_MODEL), lambda i: (0, 0)),
            pl.BlockSpec((D_MODEL, MAIN_COLS), lambda i: (0, 0)),
            pl.BlockSpec((D_MODEL, TAIL_COLS), lambda i: (0, 0)),
        ],
        out_specs=[
            pl.BlockSpec((tm, MAIN_COLS), lambda i: (i, 0)),
            pl.BlockSpec((tm, TAIL_COLS), lambda i: (i, 0)),
        ],
        out_shape=(jax.ShapeDtypeStruct((TOKENS, MAIN_COLS), BF16), jax.ShapeDtypeStruct((TOKENS, TAIL_COLS), BF16)),
        compiler_params=_cparams(("parallel",), VMEM_BIG_MB),
        name="in_proj",
    )(x, g, w_main, w_tail)


def _qkv_kernel(ql_ref, kvl_ref, kr_ref, krsw_ref, c_ref, s_ref, qg_ref, kvg_ref, wq_ref, wkv_ref,
                q_out, k_out, v_out):
    c = c_ref[...]
    s = s_ref[...]
    width = N_HEADS * HEAD_PAD
    half = QK_ROPE // 2
    low_half = lax.broadcasted_iota(jnp.int32, c.shape, 1) < QK_NOPE + half

    def swap_halves(x):
        return jnp.where(low_half, pltpu.roll(x, HEAD_PAD - half, axis=1), pltpu.roll(x, half, axis=1))

    qn = _rms(ql_ref[...].astype(F32), qg_ref[...]).astype(BF16)
    qq = jnp.dot(qn, wq_ref[...], preferred_element_type=F32)
    for h in range(N_HEADS):
        lo, hi = h * HEAD_PAD, (h + 1) * HEAD_PAD
        q_out[:, lo:hi] = (qq[:, lo:hi] * c + swap_halves(qq[:, lo:hi]) * s).astype(BF16)
    kvn = _rms(kvl_ref[...].astype(F32), kvg_ref[...]).astype(BF16)
    kk = jnp.dot(kvn, wkv_ref[...], preferred_element_type=F32)
    kr = kr_ref[...].astype(F32) * c + krsw_ref[...].astype(F32) * s
    for h in range(N_HEADS):
        lo, hi = h * HEAD_PAD, (h + 1) * HEAD_PAD
        k_out[:, lo:hi] = (kk[:, lo:hi] + kr).astype(BF16)
    v_out[...] = kk[:, width:].astype(BF16)


def _qkv(proj, tail, c_tab, s_tab, qg, kvg, wq, wkv):
    tm = TM_QKV
    width = N_HEADS * HEAD_PAD
    row = lambda blk: (lambda i: (i, blk))
    const = lambda i: (0, 0)
    return pl.pallas_call(
        _qkv_kernel,
        grid=(TOKENS // tm,),
        in_specs=[
            pl.BlockSpec((tm, Q_LORA), row(COL_QLAT // Q_LORA)),
            pl.BlockSpec((tm, KV_LORA), row(COL_KVLAT // KV_LORA)),
            pl.BlockSpec((tm, HEAD_PAD), row(TCOL_KR // HEAD_PAD)),
            pl.BlockSpec((tm, HEAD_PAD), row(TCOL_KRSW // HEAD_PAD)),
            pl.BlockSpec((tm, HEAD_PAD), row(0)),
            pl.BlockSpec((tm, HEAD_PAD), row(0)),
            pl.BlockSpec((1, Q_LORA), const),
            pl.BlockSpec((1, KV_LORA), const),
            pl.BlockSpec((Q_LORA, width), const),
            pl.BlockSpec((KV_LORA, 2 * width), const),
        ],
        out_specs=[pl.BlockSpec((tm, width), row(0))] * 3,
        out_shape=(jax.ShapeDtypeStruct((TOKENS, width), BF16),) * 3,
        compiler_params=_cparams(("parallel",), VMEM_MB),
        name="qkv_prep",
    )(proj, proj, tail, tail, c_tab, s_tab, qg, kvg, wq, wkv)


def _attn_kernel(q_ref, k_ref, v_ref, o_ref, *state):
    t = T_ATTN
    nh = HEADS_PER_STEP
    qi = pl.program_id(2)
    nt = (((1,), (1,)), ((), ()))
    m_sc, l_sc, acc_sc = state[0:nh], state[nh:2 * nh], state[2 * nh:3 * nh]
    for h in range(nh):
        m_sc[h][...] = jnp.full((t, LANES), -jnp.inf, F32)
        l_sc[h][...] = jnp.zeros((t, LANES), F32)
        acc_sc[h][...] = jnp.zeros((t, LANES), F32)

    def block(j, r0, nr, c0, nc, masked):
        start = pl.multiple_of(j * t + c0, nc)
        rows = slice(r0, r0 + nr)
        for h in range(nh):
            lo, hi = h * HEAD_PAD, (h + 1) * HEAD_PAD
            s = lax.dot_general(q_ref[rows, lo:hi], k_ref[pl.ds(start, nc), lo:hi], nt, preferred_element_type=F32)
            if masked:
                row_id = r0 + lax.broadcasted_iota(jnp.int32, (nr, nc), 0)
                col_id = c0 + lax.broadcasted_iota(jnp.int32, (nr, nc), 1)
                s = jnp.where(row_id >= col_id, s, -jnp.inf)
            blocks = [s[:, c * LANES:(c + 1) * LANES] for c in range(nc // LANES)]
            bmax = functools.reduce(jnp.maximum, blocks)
            m_old = m_sc[h][rows, :]
            m_new = jnp.maximum(m_old, jnp.max(bmax, axis=-1, keepdims=True))
            alpha = jnp.exp2(m_old - m_new)
            ps = [jnp.exp2(b - m_new) for b in blocks]
            p = jnp.concatenate(ps, axis=1).astype(BF16)
            l_sc[h][rows, :] = alpha * l_sc[h][rows, :] + functools.reduce(jnp.add, ps)
            acc_sc[h][rows, :] = alpha * acc_sc[h][rows, :] + jnp.dot(p, v_ref[pl.ds(start, nc), lo:hi],
                                                                       preferred_element_type=F32)
            m_sc[h][rows, :] = m_new

    def body(j, carry):
        block(j, 0, t, 0, t, False)
        return carry

    lax.fori_loop(0, qi, body, 0)
    half = t // 2
    block(qi, 0, t, 0, half, True)
    block(qi, half, half, half, half, True)
    out = [acc_sc[h][...] / jnp.sum(l_sc[h][...], axis=-1, keepdims=True) for h in range(nh)]
    o_ref[...] = jnp.concatenate([out[h] + out[h + 1] for h in range(0, nh, 2)], axis=1).astype(BF16)


def _attention(q, k, v):
    t = T_ATTN
    nq = SEQ // t
    nh = HEADS_PER_STEP
    return pl.pallas_call(
        _attn_kernel,
        grid=(BATCH, N_HEADS // nh, nq),
        in_specs=[
            pl.BlockSpec((t, nh * HEAD_PAD), lambda b, hg, i: (b * nq + i, hg)),
            pl.BlockSpec((SEQ, nh * HEAD_PAD), lambda b, hg, i: (b, hg)),
            pl.BlockSpec((SEQ, nh * HEAD_PAD), lambda b, hg, i: (b, hg)),
        ],
        out_specs=pl.BlockSpec((t, nh * V_HEAD), lambda b, hg, i: (b * nq + i, hg)),
        out_shape=jax.ShapeDtypeStruct((TOKENS, N_HEADS * V_HEAD), BF16),
        scratch_shapes=[pltpu.VMEM((t, LANES), F32)] * (3 * nh),
        compiler_params=_cparams(("parallel", "parallel", "arbitrary"), VMEM_MB),
        name="mla_attention",
    )(q, k, v)


def _mixer_tail_kernel(gates_ref, sc_ref, sch_ref, gv_ref, gg_ref, gvh_ref, ggh_ref, att_ref, x_ref,
                       cwa_ref, woa_ref, cwc_ref, lng_ref, lnb_ref, woc_ref, wob_ref, wo_ref, gf_ref, wr_ref, br_ref,
                       out_ref, meta_ref, meta_t_ref, cnt_ref, xs0_ref, cbuf, ubuf, shifted, below, run_ref, zbuf, zsem):
    tm = TM_MIX
    step = pl.program_id(0)

    @pl.when(step == 0)
    def _():
        run_ref[...] = jnp.zeros_like(run_ref)
        zbuf[...] = jnp.zeros_like(zbuf)
        row_id = lax.broadcasted_iota(jnp.int32, (tm, tm), 0)
        col_id = lax.broadcasted_iota(jnp.int32, (tm, tm), 1)
        below[...] = jnp.where(row_id > col_id, 1.0, 0.0).astype(BF16)

    def zero_copy(c):
        rows = pl.ds(pl.multiple_of((step * ZERO_COPIES + c) * ZERO_ROWS, ZERO_ROWS), ZERO_ROWS)
        return pltpu.make_async_copy(zbuf, xs0_ref.at[rows, :], zsem.at[0])

    for c in range(ZERO_COPIES):
        zero_copy(c).start()
    has_past = (pl.program_id(0) % (SEQ // tm)) != 0

    sc = sc_ref[...]
    sc_b = sc[:, 0:SC_WIDTH].astype(F32)
    cbuf[8:8 + tm, :] = sc[:, SC_WIDTH:2 * SC_WIDTH].astype(F32) * sc[:, 2 * SC_WIDTH:].astype(F32)
    sch = sch_ref[...].astype(F32)[HALO_A - 8:HALO_A, :]
    cbuf[0:8, :] = jnp.where(has_past, sch[:, SC_WIDTH:2 * SC_WIDTH] * sch[:, 2 * SC_WIDTH:], 0.0)
    conv_a = cwa_ref[0:1, :] * cbuf[6:6 + tm, :]
    for t in range(1, SC_KERNEL):
        conv_a = conv_a + cwa_ref[t:t + 1, :] * cbuf[6 + t:6 + t + tm, :]
    y_a = jnp.dot((sc_b * conv_a).astype(BF16), woa_ref[...], preferred_element_type=F32)

    ubuf[HALO_C:HALO_C + tm, :] = gv_ref[...].astype(F32) * _sigmoid(gg_ref[...].astype(F32))
    ubuf[0:HALO_C, :] = jnp.where(has_past, gvh_ref[...].astype(F32) * _sigmoid(ggh_ref[...].astype(F32)), 0.0)
    rows = tm + HALO_C - 8
    u_all = ubuf[...]
    for b in range(1, 8):
        shifted[b - 1, 0:rows, :] = pltpu.roll(u_all, tm + HALO_C - b, axis=0)[0:rows, :]
    base = HALO_C - (CONF_KERNEL - 1)
    acc = None
    for t in range(CONF_KERNEL):
        off = base + t
        a0 = off - off % 8
        src = ubuf[a0:a0 + tm, :] if off % 8 == 0 else shifted[off % 8 - 1, a0:a0 + tm, :]
        term = cwc_ref[t:t + 1, :] * src
        acc = term if acc is None else acc + term
    mu = jnp.mean(acc, axis=-1, keepdims=True)
    xc = acc - mu
    var = jnp.mean(xc * xc, axis=-1, keepdims=True)
    y = xc * lax.rsqrt(var + EPS) * lng_ref[...] + lnb_ref[...]
    y_c = jnp.dot((y * _sigmoid(y)).astype(BF16), woc_ref[...], preferred_element_type=F32)

    y_b = jnp.dot(att_ref[...], wob_ref[...], preferred_element_type=F32)

    g = gates_ref[...]
    merged = (_sigmoid(g[:, 0:D_MODEL].astype(F32)) * y_a
              + _sigmoid(g[:, D_MODEL:2 * D_MODEL].astype(F32)) * y_b
              + _sigmoid(g[:, 2 * D_MODEL:].astype(F32)) * y_c)
    x_new = x_ref[...] + jnp.dot(merged.astype(BF16), wo_ref[...], preferred_element_type=F32)
    out_ref[...] = x_new
    _route_tile(x_new, gf_ref, wr_ref, br_ref, below, run_ref, meta_ref, meta_t_ref, cnt_ref)
    for c in range(ZERO_COPIES):
        zero_copy(c).wait()


def _mixer_tail(proj, tail, att, x, cwa, woa, cwc, lng, lnb, woc, wob, wo, gf, wr, br):
    tm = TM_MIX
    row = lambda width, col: pl.BlockSpec((tm, width), lambda i: (i, col // width))
    halo = lambda rows, width, col: pl.BlockSpec(
        (rows, width), lambda i: (jnp.maximum(i * (tm // rows) - 1, 0), col // width))
    const = lambda a: pl.BlockSpec(a.shape, lambda i: (0,) * a.ndim)
    weights = (cwa, woa, cwc, lng, lnb, woc, wob, wo, gf, wr, br)
    return pl.pallas_call(
        _mixer_tail_kernel,
        grid=(TOKENS // tm,),
        in_specs=[
            row(3 * D_MODEL, COL_GATES),
            row(3 * SC_WIDTH, COL_SC),
            halo(HALO_A, 3 * SC_WIDTH, COL_SC),
            row(CONF_WIDTH, TCOL_GLU),
            row(CONF_WIDTH, TCOL_GLU + CONF_WIDTH),
            halo(HALO_C, CONF_WIDTH, TCOL_GLU),
            halo(HALO_C, CONF_WIDTH, TCOL_GLU + CONF_WIDTH),
            pl.BlockSpec((tm, N_HEADS * V_HEAD), lambda i: (i, 0)),
            pl.BlockSpec((tm, D_MODEL), lambda i: (i, 0)),
        ] + [const(a) for a in weights],
        out_specs=[
            pl.BlockSpec((tm, D_MODEL), lambda i: (i, 0)),
            pl.BlockSpec((tm, LANES), lambda i: (i, 0)),
            pl.BlockSpec((META_ROWS, tm), lambda i: (0, i)),
            pl.BlockSpec((1, LANES), lambda i: (0, 0)),
            pl.BlockSpec(memory_space=pl.ANY),
        ],
        out_shape=(
            jax.ShapeDtypeStruct((TOKENS, D_MODEL), F32),
            jax.ShapeDtypeStruct((TOKENS, LANES), F32),
            jax.ShapeDtypeStruct((META_ROWS, TOKENS), F32),
            jax.ShapeDtypeStruct((1, LANES), F32),
            jax.ShapeDtypeStruct((SORTED_ROWS, LANES), F32),
        ),
        scratch_shapes=[
            pltpu.VMEM((tm + 8, SC_WIDTH), F32),
            pltpu.VMEM((tm + HALO_C, CONF_WIDTH), F32),
            pltpu.VMEM((7, tm + HALO_C - 8, CONF_WIDTH), F32),
            pltpu.VMEM((tm, tm), BF16),
            pltpu.VMEM((1, LANES), F32),
            pltpu.VMEM((ZERO_ROWS, LANES), F32),
            pltpu.SemaphoreType.DMA((1,)),
        ],
        compiler_params=_cparams(("arbitrary",), VMEM_BIG_MB),
        name="mixer_tail",
    )(proj, proj, proj, tail, tail, tail, tail, att, x, *weights)


def _route(logits):
    lane = lax.broadcasted_iota(jnp.int32, logits.shape, 1)
    lane_f = lane.astype(F32)
    neg = -jnp.inf
    big = float(LANES)
    is_grp = (lane >= ROUTER_GROUP_LANE) & (lane < ROUTER_GROUP_LANE + N_GROUPS)
    glog = jnp.where(is_grp, logits, neg)
    gmax = jnp.max(glog, axis=-1, keepdims=True)
    gidx = jnp.min(jnp.where(glog == gmax, lane_f, big), axis=-1, keepdims=True)
    p_sel = 1.0 / jnp.sum(jnp.exp(glog - gmax), axis=-1, keepdims=True)
    first = (gidx - ROUTER_GROUP_LANE) * EXPERTS_PER_GROUP
    in_grp = (lane_f >= first) & (lane_f < first + EXPERTS_PER_GROUP)
    el = jnp.where(in_grp, logits, neg)
    m1 = jnp.max(el, axis=-1, keepdims=True)
    i1 = jnp.min(jnp.where(el == m1, lane_f, big), axis=-1, keepdims=True)
    el2 = jnp.where(lane_f == i1, neg, el)
    m2 = jnp.max(el2, axis=-1, keepdims=True)
    i2 = jnp.min(jnp.where(el2 == m2, lane_f, big), axis=-1, keepdims=True)
    e2 = jnp.exp(m2 - m1)
    w1 = p_sel / (1.0 + e2)
    w2 = w1 * e2
    return i1, i2, w1, w2


def _route_tile(x, g_ref, wr_ref, br_ref, below_ref, run_ref, meta_ref, meta_t_ref, cnt_ref):
    tm = x.shape[0]
    h = _rms(x, g_ref[...])
    h_hi = h.astype(BF16)
    h_lo = (h - h_hi.astype(F32)).astype(BF16)
    hi_terms = jnp.dot(h_hi, wr_ref[...], preferred_element_type=F32)
    logits = (hi_terms[:, :LANES] + hi_terms[:, LANES:]
              + jnp.dot(h_lo, wr_ref[:, :LANES], preferred_element_type=F32) + br_ref[...])
    i1, i2, w1, w2 = _route(logits)
    lane = lax.broadcasted_iota(jnp.int32, (tm, LANES), 1)
    lane_f = lane.astype(F32)
    oh1 = lane_f == i1
    oh2 = lane_f == i2
    onehot = jnp.where(oh1, 1.0, 0.0) + jnp.where(oh2, 1.0, 0.0)
    before = run_ref[...] + jnp.dot(below_ref[...], onehot.astype(BF16), preferred_element_type=F32)
    r1 = jnp.sum(jnp.where(oh1, before, 0.0), axis=-1, keepdims=True)
    r2 = jnp.sum(jnp.where(oh2, before, 0.0), axis=-1, keepdims=True)
    run_ref[...] += jnp.sum(onehot, axis=0, keepdims=True)
    cnt_ref[...] = run_ref[...]
    meta = jnp.zeros((tm, LANES), F32)
    for col, val in enumerate((i1, i2, r1, r2, w1, w2)):
        meta = jnp.where(lane == col, val, meta)
    meta_ref[...] = meta
    meta_t_ref[...] = meta.T[0:META_ROWS, :]


def _dispatch_plan(meta_t, counts):
    e1 = meta_t[META_E1].astype(jnp.int32)
    e2 = meta_t[META_E2].astype(jnp.int32)
    r1 = meta_t[META_R1].astype(jnp.int32)
    r2 = meta_t[META_R2].astype(jnp.int32)
    cnt = counts[0, :N_EXPERTS].astype(jnp.int32)
    tiles = (cnt + TM_EXP - 1) // TM_EXP
    tile_end = jnp.cumsum(tiles)
    first_slot = ((tile_end - tiles) * TM_EXP)[:, None]
    expert = jnp.arange(N_EXPERTS, dtype=jnp.int32)[:, None]
    pos1 = jnp.sum(jnp.where(e1[None, :] == expert, first_slot, 0), axis=0) + r1
    pos2 = jnp.sum(jnp.where(e2[None, :] == expert, first_slot, 0), axis=0) + r2
    n_tiles = tile_end[-1:]
    tile_id = jnp.minimum(jnp.arange(MAX_TILES, dtype=jnp.int32), n_tiles - 1)
    tile_expert = jnp.sum((tile_id[:, None] >= tile_end[None, :]).astype(jnp.int32), axis=1)
    return pos1, pos2, tile_expert, n_tiles


def _dispatch_kernel(pos1_ref, pos2_ref, x_ref, g_ref, xs_in_ref, xs_ref, slab, sem):
    del xs_in_ref
    tm = TM_DISP
    i = pl.program_id(0)
    last = pl.num_programs(0) - 1
    slot = i % 2

    def wait_copies(sl):
        for _ in range(TOP_K):
            pltpu.make_async_copy(slab.at[sl], xs_ref.at[pl.ds(0, tm * SLAB), :], sem.at[sl]).wait()

    @pl.when(i >= 2)
    def _():
        wait_copies(slot)

    h = _rms(x_ref[...], g_ref[...])
    for s in range(SLAB):
        slab[slot, pl.ds(s, tm, stride=SLAB), :] = h[:, s * LANES:(s + 1) * LANES]

    def body(c, carry):
        for u in range(GATHER_UNROLL):
            r = c * GATHER_UNROLL + u
            src = slab.at[slot, pl.ds(pl.multiple_of(r * SLAB, SLAB), SLAB), :]
            for k, pos_ref in enumerate((pos1_ref, pos2_ref)):
                p = pos_ref[i * tm + r]
                pltpu.make_async_copy(src, xs_ref.at[pl.ds(pl.multiple_of(p * SLAB, SLAB), SLAB), :],
                                      sem.at[slot]).start(priority=k % N_DMA_PRIORITIES)
        return carry

    lax.fori_loop(0, tm // GATHER_UNROLL, body, 0)

    @pl.when(i == last)
    def _():
        wait_copies(1 - slot)
        wait_copies(slot)


def _dispatch(pos1, pos2, x, g, xs0):
    tm = TM_DISP
    return pl.pallas_call(
        _dispatch_kernel,
        grid_spec=pltpu.PrefetchScalarGridSpec(
            num_scalar_prefetch=2,
            grid=(TOKENS // tm,),
            in_specs=[
                pl.BlockSpec((tm, D_MODEL), lambda i, p1, p2: (i, 0)),
                pl.BlockSpec((1, D_MODEL), lambda i, p1, p2: (0, 0)),
                pl.BlockSpec(memory_space=pl.ANY),
            ],
            out_specs=pl.BlockSpec(memory_space=pl.ANY),
            scratch_shapes=[pltpu.VMEM((2, tm * SLAB, LANES), F32), pltpu.SemaphoreType.DMA((2,))],
        ),
        out_shape=jax.ShapeDtypeStruct((SORTED_ROWS, LANES), F32),
        input_output_aliases={4: 0},
        compiler_params=_cparams(("arbitrary",), VMEM_MB),
        name="moe_dispatch",
    )(pos1, pos2, x, g, xs0)


def _expert_kernel(te_ref, nt_ref, first_ref, grp_ref, nxt_ref, xs_ref, wg_hbm, wu_hbm, wd_hbm, ys_ref,
                   xbuf, sem, wg_buf, wu_buf, wd_buf, wsem, *, layer):
    tm = TM_EXP
    i = pl.program_id(0)
    n = nt_ref[0]

    def weight_copies(expert, slot):
        return [pltpu.make_async_copy(hbm.at[layer, expert], buf.at[slot], wsem.at[slot])
                for hbm, buf in ((wg_hbm, wg_buf), (wu_hbm, wu_buf), (wd_hbm, wd_buf))]

    @pl.when(i == 0)
    def _():
        for c in weight_copies(te_ref[0], 0):
            c.start()

    @pl.when((i < n) & (first_ref[i] == 1))
    def _():
        for c in weight_copies(te_ref[i], grp_ref[i] % 2):
            c.wait()

        @pl.when(nxt_ref[i] >= 0)
        def _():
            for c in weight_copies(nxt_ref[i], (grp_ref[i] + 1) % 2):
                c.start()

    def tile_copy(tile):
        slot = tile % EXPERT_SLOTS
        rows = pl.ds(pl.multiple_of(tile * (tm * SLAB), tm * SLAB), tm * SLAB)
        return pltpu.make_async_copy(xs_ref.at[rows, :], xbuf.at[slot], sem.at[slot])

    @pl.when(i == 0)
    def _():
        for ahead in range(EXPERT_PREFETCH):
            @pl.when(ahead < n)
            def _():
                tile_copy(ahead).start()

    @pl.when(i + EXPERT_PREFETCH < n)
    def _():
        tile_copy(i + EXPERT_PREFETCH).start()

    @pl.when(i < n)
    def _():
        tile_copy(i).wait()
        slot = i % EXPERT_SLOTS
        ws = grp_ref[i] % 2
        xt = jnp.concatenate([xbuf[slot, pl.ds(s, tm, stride=SLAB), :].astype(BF16) for s in range(SLAB)], axis=1)
        hg = jnp.dot(xt, wg_buf[ws].astype(BF16), preferred_element_type=F32)
        hu = jnp.dot(xt, wu_buf[ws].astype(BF16), preferred_element_type=F32)
        hh = (hg * _sigmoid(hg) * hu).astype(BF16)
        y = jnp.dot(hh, wd_buf[ws].astype(BF16), preferred_element_type=F32)
        for s in range(SLAB):
            ys_ref[pl.ds(s, tm, stride=SLAB), :] = y[:, s * LANES:(s + 1) * LANES]

    @pl.when(i >= n)
    def _():
        ys_ref[...] = jnp.zeros_like(ys_ref)


def _experts(layer, tile_expert, n_tiles, xs, w_gate, w_up, w_down):
    tm = TM_EXP
    tile = jnp.arange(MAX_TILES, dtype=jnp.int32)
    prev = jnp.concatenate([jnp.full((1,), -1, jnp.int32), tile_expert[:-1]])
    first = (tile_expert != prev).astype(jnp.int32)
    run = jnp.cumsum(first) - 1
    later_first = (tile[None, :] > tile[:, None]) & (first[None, :] == 1)
    nxt_pos = jnp.min(jnp.where(later_first, tile[None, :], MAX_TILES), axis=1)
    nxt = jnp.sum(jnp.where(tile[None, :] == nxt_pos[:, None], tile_expert[None, :] + 1, 0), axis=1) - 1
    return pl.pallas_call(
        functools.partial(_expert_kernel, layer=layer),
        grid_spec=pltpu.PrefetchScalarGridSpec(
            num_scalar_prefetch=5,
            grid=(MAX_TILES,),
            in_specs=[pl.BlockSpec(memory_space=pl.ANY)] * 4,
            out_specs=pl.BlockSpec((tm * SLAB, LANES), lambda i, *_: (i, 0)),
            scratch_shapes=[
                pltpu.VMEM((EXPERT_SLOTS, tm * SLAB, LANES), F32), pltpu.SemaphoreType.DMA((EXPERT_SLOTS,)),
                pltpu.VMEM((2, D_MODEL, EXPERT_HIDDEN), F32), pltpu.VMEM((2, D_MODEL, EXPERT_HIDDEN), F32),
                pltpu.VMEM((2, EXPERT_HIDDEN, D_MODEL), F32), pltpu.SemaphoreType.DMA((2,)),
            ],
        ),
        out_shape=jax.ShapeDtypeStruct((SORTED_ROWS, LANES), F32),
        compiler_params=_cparams(("arbitrary",), VMEM_MB),
        name="moe_experts",
    )(tile_expert, n_tiles, first, run.astype(jnp.int32), nxt.astype(jnp.int32), xs, w_gate, w_up, w_down)


def _start_slab_gathers(idx_refs, base, n_rows, src_hbm, dst_bufs, sem):
    def body(c, carry):
        for u in range(GATHER_UNROLL):
            r = c * GATHER_UNROLL + u
            for k, (idx_ref, dst) in enumerate(zip(idx_refs, dst_bufs)):
                t = idx_ref[base + r]
                pltpu.make_async_copy(src_hbm.at[pl.ds(pl.multiple_of(t * SLAB, SLAB), SLAB), :],
                                      dst.at[pl.ds(pl.multiple_of(r * SLAB, SLAB), SLAB), :],
                                      sem).start(priority=k % N_DMA_PRIORITIES)
        return carry

    lax.fori_loop(0, n_rows // GATHER_UNROLL, body, 0)


def _wait_slab_gathers(n_rows, src_hbm, dst, sem):
    pltpu.make_async_copy(src_hbm.at[pl.ds(0, n_rows * SLAB), :], dst, sem).wait()


def _combine_ple_kernel(pos1_ref, pos2_ref, x_ref, meta_ref, ys_ref, p_ref, g_ref, wg_ref, wp_ref, fg_ref,
                        out_ref, cbuf, sem, *, final):
    tm = TM_COMB
    i = pl.program_id(0)

    def gather(tile, slot):
        _start_slab_gathers((pos1_ref, pos2_ref), tile * tm, tm, ys_ref, (cbuf.at[slot, 0], cbuf.at[slot, 1]),
                            sem.at[slot])

    @pl.when(i == 0)
    def _():
        gather(0, 0)

    @pl.when(i + 1 < pl.num_programs(0))
    def _():
        gather(i + 1, (i + 1) % 2)

    slot = i % 2
    for k in range(2):
        _wait_slab_gathers(tm, ys_ref, cbuf.at[slot, k], sem.at[slot])
    meta = meta_ref[...]
    w1 = meta[:, META_W1:META_W1 + 1]
    w2 = meta[:, META_W2:META_W2 + 1]
    moe = jnp.concatenate([w1 * cbuf[slot, 0, pl.ds(s, tm, stride=SLAB), :]
                           + w2 * cbuf[slot, 1, pl.ds(s, tm, stride=SLAB), :] for s in range(SLAB)], axis=1)
    x = x_ref[...] + moe
    h = _rms(x, g_ref[...]).astype(BF16)
    gate = _sigmoid(jnp.dot(h, wg_ref[...], preferred_element_type=F32))
    emb = jnp.dot(p_ref[...].astype(BF16), wp_ref[...], preferred_element_type=F32)
    y = x + gate * emb
    if final:
        y = _rms(y, fg_ref[...])
    out_ref[...] = y


def _combine_ple(layer, pos1, pos2, x, meta, ys, p, g, wg, wp, fg, final):
    tm = TM_COMB
    const = lambda i, p1, p2: (0, 0)
    rows = lambda i, p1, p2: (i, 0)
    return pl.pallas_call(
        functools.partial(_combine_ple_kernel, final=final),
        grid_spec=pltpu.PrefetchScalarGridSpec(
            num_scalar_prefetch=2,
            grid=(TOKENS // tm,),
            in_specs=[
                pl.BlockSpec((tm, D_MODEL), rows),
                pl.BlockSpec((tm, LANES), rows),
                pl.BlockSpec(memory_space=pl.ANY),
                pl.BlockSpec((None, tm, PLE_DIM), lambda i, p1, p2: (layer, i, 0)),
                pl.BlockSpec((1, D_MODEL), const),
                pl.BlockSpec((D_MODEL, D_MODEL), const),
                pl.BlockSpec((PLE_DIM, D_MODEL), const),
                pl.BlockSpec((1, D_MODEL), const),
            ],
            out_specs=pl.BlockSpec((tm, D_MODEL), rows),
            scratch_shapes=[pltpu.VMEM((2, 2, tm * SLAB, LANES), F32), pltpu.SemaphoreType.DMA((2,))],
        ),
        out_shape=jax.ShapeDtypeStruct((TOKENS, D_MODEL), F32),
        compiler_params=_cparams(("arbitrary",), VMEM_MB),
        name="moe_combine_ple",
    )(pos1, pos2, x, meta, ys, p, g, wg, wp, fg)


def _in_proj_weights(w_in):
    k_rope = w_in[:, MAIN_COLS:MAIN_COLS + QK_ROPE]
    glu = w_in[:, MAIN_COLS + QK_ROPE:]
    half = QK_ROPE // 2
    zn = jnp.zeros((D_MODEL, QK_NOPE), F32)
    zp = jnp.zeros((D_MODEL, HEAD_PAD - QK_NOPE - QK_ROPE), F32)
    tail = jnp.concatenate([glu, zn, k_rope, zp, zn, k_rope[:, half:], k_rope[:, :half], zp], axis=1)
    return w_in[:, :MAIN_COLS].astype(BF16), tail.astype(BF16)


def _q_weight(w_uq):
    scale = (QK_NOPE + QK_ROPE) ** -0.5 * LOG2_E
    w = (w_uq * scale).reshape(Q_LORA, N_HEADS, QK_NOPE + QK_ROPE)
    zp = jnp.zeros((Q_LORA, N_HEADS, HEAD_PAD - QK_NOPE - QK_ROPE), F32)
    return jnp.concatenate([w, zp], axis=2).reshape(Q_LORA, N_HEADS * HEAD_PAD).astype(BF16)


def _kv_weight(w_ukv):
    w = w_ukv.reshape(KV_LORA, N_HEADS, QK_NOPE + V_HEAD)
    k_nope, v = w[:, :, :QK_NOPE], w[:, :, QK_NOPE:]
    z = jnp.zeros_like(v)
    k_part = jnp.concatenate([k_nope, jnp.zeros_like(k_nope)], axis=2).reshape(KV_LORA, N_HEADS * HEAD_PAD)
    odd = (jnp.arange(N_HEADS) % 2 == 1)[None, :, None]
    v_part = jnp.concatenate([jnp.where(odd, z, v), jnp.where(odd, v, z)], axis=2).reshape(KV_LORA, N_HEADS * HEAD_PAD)
    return jnp.concatenate([k_part, v_part], axis=1).astype(BF16)


def _router_weight(w_rg, b_rg, w_re, b_re):
    pad = LANES - N_EXPERTS - N_GROUPS
    w = jnp.concatenate([w_re, w_rg, jnp.zeros((D_MODEL, pad), F32)], axis=1)
    b = jnp.concatenate([b_re, b_rg, jnp.zeros((pad,), F32)]).reshape(1, LANES)
    w_hi = w.astype(BF16)
    w_lo = (w - w_hi.astype(F32)).astype(BF16)
    return jnp.concatenate([w_hi, w_lo], axis=1), b


def kernel(x, p, positions, ln_mix_g, w_in, conv_a_w, w_out_a, q_norm_g, w_uq, kv_norm_g, w_ukv, w_out_b, conv_c_w, ln_c_g, ln_c_b, w_out_c, w_o, ln_ffn_g, w_route_grp, b_route_grp, w_route_exp, b_route_exp, w_exp_gate, w_exp_up, w_exp_down, ln_ple_g, w_ple_gate, w_ple, final_norm_g):
    c_tab, s_tab = _rope_tables(positions)
    xf = x.reshape(TOKENS, D_MODEL)
    pf = p.reshape(DEPTH, TOKENS, PLE_DIM)
    row = lambda a: a.reshape(1, -1)
    for i in range(DEPTH):
        proj, tail = _inproj(xf, row(ln_mix_g[i]), *_in_proj_weights(w_in[i]))
        q, k, v = _qkv(proj, tail, c_tab, s_tab, row(q_norm_g[i]), row(kv_norm_g[i]), _q_weight(w_uq[i]), _kv_weight(w_ukv[i]))
        att = _attention(q, k, v)
        wr, br = _router_weight(w_route_grp[i], b_route_grp[i], w_route_exp[i], b_route_exp[i])
        xf, meta, meta_t, counts, xs0 = _mixer_tail(
            proj, tail, att, xf, conv_a_w[i], w_out_a[i].astype(BF16), conv_c_w[i], row(ln_c_g[i]), row(ln_c_b[i]),
            w_out_c[i].astype(BF16), w_out_b[i].astype(BF16), w_o[i].astype(BF16), row(ln_ffn_g[i]), wr, br)
        pos1, pos2, tile_expert, n_tiles = _dispatch_plan(meta_t, counts)
        xs = _dispatch(pos1, pos2, xf, row(ln_ffn_g[i]), xs0)
        ys = _experts(i, tile_expert, n_tiles, xs, w_exp_gate, w_exp_up, w_exp_down)
        xf = _combine_ple(i, pos1, pos2, xf, meta, ys, pf, row(ln_ple_g[i]), w_ple_gate[i].astype(BF16),
                          w_ple[i].astype(BF16), row(final_norm_g), final=(i == DEPTH - 1))
    return xf.reshape(BATCH, SEQ, D_MODEL)
```
